```python
import jax, jax.numpy as jnp
from jax import lax
import numpy as np

D_MODEL = 1024
BATCH = 2
SEQ = 8192
DEPTH = 1

CHUNK = 64
QBLOCK = 128
SB_HEADS = 8
SB_HEAD_DIM = 64
SB_WIDTH = SB_HEADS * SB_HEAD_DIM
HG_HEADS = 8
HG_KDIM = 64
HG_VDIM = 64
HG_WIDTH = HG_HEADS * HG_KDIM
HG_VWIDTH = HG_HEADS * HG_VDIM
N_BRANCHES = 2
N_GROUPS = 4
EXPERTS_PER_GROUP = 4
N_EXPERTS = N_GROUPS * EXPERTS_PER_GROUP
TOP_K_INNER = 2
D_EXPERT = 512
IN_COLS = 3 * SB_WIDTH + 2 * HG_WIDTH + 2 * HG_VWIDTH + N_BRANCHES * D_MODEL
EPS = 1e-6

kernel_name = 'hybrid_stickbreak_hgrn2_hmoe'


def rmsnorm(x, g):
    xf = x.astype(jnp.float32)
    var = jnp.mean(xf * xf, axis=-1, keepdims=True)
    return (xf * lax.rsqrt(var + EPS) * g.astype(jnp.float32)).astype(x.dtype)


def stick_breaking_attention(q, k, v):
    b, h, s, dh = q.shape
    nb = s // QBLOCK
    scale = dh ** -0.5
    key_pos = jnp.arange(s)
    q_blocks = q.reshape(b, h, nb, QBLOCK, dh).transpose(2, 0, 1, 3, 4)

    def one_block(args):
        q_blk, blk = args
        z = jnp.einsum('bhqd,bhkd->bhqk', q_blk, k, preferred_element_type=jnp.float32) * scale
        q_pos = blk * QBLOCK + jnp.arange(QBLOCK)
        mask = key_pos[None, :] < q_pos[:, None]
        log_beta = jax.nn.log_sigmoid(z)
        log_keep = jnp.where(mask, jax.nn.log_sigmoid(-z), 0.0)
        later = jnp.concatenate([log_keep[..., 1:], jnp.zeros_like(log_keep[..., :1])], axis=-1)
        log_survive = lax.cumsum(later, axis=3, reverse=True)
        a = jnp.where(mask, jnp.exp(log_beta + log_survive), 0.0)
        return jnp.einsum('bhqk,bhkd->bhqd', a.astype(v.dtype), v)

    out = lax.map(one_block, (q_blocks, jnp.arange(nb)))
    return out.transpose(1, 2, 0, 3, 4).reshape(b, h, s, dh)


def hgrn2_chunk_scan(q, k, log_f, v):
    b, s, h, dk = q.shape
    dv = v.shape[-1]
    nc = s // CHUNK

    def to_chunks(t):
        return t.reshape(b, nc, CHUNK, h, t.shape[-1]).transpose(1, 0, 3, 2, 4)

    causal = jnp.tril(jnp.ones((CHUNK, CHUNK), dtype=bool))

    def step(state, xs):
        qc, kc, gc, vc = xs
        cum = jnp.cumsum(gc, axis=2)
        diff = cum[:, :, :, None, :] - cum[:, :, None, :, :]
        decay = jnp.exp(jnp.where(causal[:, :, None], diff, -jnp.inf))
        scores = jnp.einsum('bhtk,bhsk,bhtsk->bhts', qc, kc, decay)
        o = (jnp.einsum('bhts,bhsv->bhtv', scores, vc)
             + jnp.einsum('bhtk,bhkv->bhtv', qc * jnp.exp(cum), state))
        last = cum[:, :, -1, :]
        state = (state * jnp.exp(last)[..., None]
                 + jnp.einsum('bhsk,bhsv->bhkv', kc * jnp.exp(last[:, :, None, :] - cum), vc))
        return state, o

    init = jnp.zeros((b, h, dk, dv), jnp.float32)
    _, out = lax.scan(step, init, (to_chunks(q), to_chunks(k), to_chunks(log_f), to_chunks(v)))
    return out.transpose(1, 0, 3, 2, 4).reshape(b, s, h, dv)


def mixer_sublayer(x, ln1_g, w_in, w_branch_sb, w_branch_hg, hg_norm_g, lb, w_out):
    f32 = jnp.float32
    bsz, s, _ = x.shape
    h = rmsnorm(x, ln1_g)
    proj = h @ w_in
    widths = [SB_WIDTH] * 3 + [HG_WIDTH] * 2 + [HG_VWIDTH] * 2 + [D_MODEL]
    splits = np.cumsum(widths).tolist()
    q_sb, k_sb, v_sb, q_hg, f_hg, i_hg, g_hg, gate_sb, gate_hg = jnp.split(proj, splits, axis=-1)

    def sb_heads(t):
        return t.reshape(bsz, s, SB_HEADS, SB_HEAD_DIM).transpose(0, 2, 1, 3)
    y_sb = stick_breaking_attention(sb_heads(q_sb), sb_heads(k_sb), sb_heads(v_sb))
    y_sb = y_sb.transpose(0, 2, 1, 3).reshape(bsz, s, SB_WIDTH)

    f = lb + (1.0 - lb) * jax.nn.sigmoid(f_hg.astype(f32))
    def hg_heads(t):
        return t.reshape(bsz, s, HG_HEADS, -1)
    o = hgrn2_chunk_scan(hg_heads(jax.nn.silu(q_hg.astype(f32))), hg_heads(1.0 - f),
                         hg_heads(jnp.log(f)), hg_heads(i_hg.astype(f32)))
    o = o * lax.rsqrt(jnp.mean(o * o, axis=-1, keepdims=True) + EPS)
    o = o.reshape(bsz, s, HG_VWIDTH) * hg_norm_g.astype(f32) * jax.nn.silu(g_hg.astype(f32))
    y_hg = o.astype(x.dtype)

    merged = (jax.nn.sigmoid(gate_sb) * (y_sb @ w_branch_sb)
              + jax.nn.sigmoid(gate_hg) * (y_hg @ w_branch_hg))
    return x + merged @ w_out


def hier_moe(h, w_rg, b_rg, w_re, b_re, w_gate, w_up, w_down):
    f32 = jnp.float32
    bsz, s, d = h.shape
    t = h.reshape(-1, d)
    n = t.shape[0]
    g_logits = (t @ w_rg).astype(f32) + b_rg.astype(f32)
    p_group = jax.nn.softmax(g_logits, axis=-1)
    g_idx = jnp.argmax(g_logits, axis=-1)
    w_grp = jnp.take_along_axis(p_group, g_idx[:, None], axis=1)
    e_logits = ((t @ w_re).astype(f32) + b_re.astype(f32)).reshape(n, N_GROUPS, EXPERTS_PER_GROUP)
    in_group = jnp.take_along_axis(e_logits, g_idx[:, None, None], axis=1)[:, 0]
    top_v, top_i = lax.top_k(in_group, TOP_K_INNER)
    w_sel = jax.nn.softmax(top_v, axis=-1) * w_grp
    expert_ids = g_idx[:, None] * EXPERTS_PER_GROUP + top_i
    combine = jnp.sum(jax.nn.one_hot(expert_ids, N_EXPERTS, dtype=f32) * w_sel[..., None], axis=1)
    y = jnp.zeros((n, d), f32)
    for e in range(N_EXPERTS):
        a = jax.nn.silu(t @ w_gate[e]) * (t @ w_up[e])
        y = y + combine[:, e:e + 1] * (a @ w_down[e]).astype(f32)
    return y.astype(h.dtype).reshape(bsz, s, d)


def setup_inputs(seed: int = 0) -> dict:
    key = jax.random.key(seed)
    ks = jax.random.split(key, 17)
    f32 = jnp.float32

    def nrm(k, shape, scale):
        return jax.random.normal(k, shape, f32) * scale

    return {
        'x': nrm(ks[0], (BATCH, SEQ, D_MODEL), 1.0),
        'ln1_g': 1.0 + nrm(ks[1], (DEPTH, D_MODEL), 0.02),
        'w_in': nrm(ks[2], (DEPTH, D_MODEL, IN_COLS), D_MODEL ** -0.5),
        'w_branch_sb': nrm(ks[3], (DEPTH, SB_WIDTH, D_MODEL), SB_WIDTH ** -0.5),
        'w_branch_hg': nrm(ks[4], (DEPTH, HG_VWIDTH, D_MODEL), HG_VWIDTH ** -0.5),
        'hg_norm_g': 1.0 + nrm(ks[5], (DEPTH, HG_VWIDTH), 0.02),
        'hg_lb_logits': nrm(ks[6], (DEPTH + 1, HG_WIDTH), 0.1),
        'w_out': nrm(ks[7], (DEPTH, D_MODEL, D_MODEL), D_MODEL ** -0.5),
        'ln2_g': 1.0 + nrm(ks[8], (DEPTH, D_MODEL), 0.02),
        'w_router_group': nrm(ks[9], (DEPTH, D_MODEL, N_GROUPS), D_MODEL ** -0.5),
        'b_router_group': nrm(ks[10], (DEPTH, N_GROUPS), 0.01),
        'w_router_expert': nrm(ks[11], (DEPTH, D_MODEL, N_EXPERTS), D_MODEL ** -0.5),
        'b_router_expert': nrm(ks[12], (DEPTH, N_EXPERTS), 0.01),
        'w_exp_gate': nrm(ks[13], (DEPTH, N_EXPERTS, D_MODEL, D_EXPERT), D_MODEL ** -0.5),
        'w_exp_up': nrm(ks[14], (DEPTH, N_EXPERTS, D_MODEL, D_EXPERT), D_MODEL ** -0.5),
        'w_exp_down': nrm(ks[15], (DEPTH, N_EXPERTS, D_EXPERT, D_MODEL), D_EXPERT ** -0.5),
        'final_g': 1.0 + nrm(ks[16], (D_MODEL,), 0.02),
    }


def reference(x, ln1_g, w_in, w_branch_sb, w_branch_hg, hg_norm_g, hg_lb_logits, w_out, ln2_g,
              w_router_group, b_router_group, w_router_expert, b_router_expert,
              w_exp_gate, w_exp_up, w_exp_down, final_g):
    lb_all = jnp.cumsum(jax.nn.softmax(hg_lb_logits.astype(jnp.float32), axis=0), axis=0)
    for l in range(DEPTH):
        x = mixer_sublayer(x, ln1_g[l], w_in[l], w_branch_sb[l], w_branch_hg[l], hg_norm_g[l],
                           lb_all[l], w_out[l])
        x = x + hier_moe(rmsnorm(x, ln2_g[l]), w_router_group[l], b_router_group[l],
                         w_router_expert[l], b_router_expert[l],
                         w_exp_gate[l], w_exp_up[l], w_exp_down[l])
    return rmsnorm(x, final_g)
```

```python
import functools

import jax
import jax.numpy as jnp
import numpy as np
from jax import lax
from jax.experimental import pallas as pl
from jax.experimental.pallas import tpu as pltpu

F32 = jnp.float32
BF16 = jnp.bfloat16

EPS = 1e-6
SB_HEADS = 8
SB_HEAD_DIM = 64
HG_HEADS = 8
HG_DIM = 64
N_GROUPS = 4
EXPERTS_PER_GROUP = 4
N_EXPERTS = N_GROUPS * EXPERTS_PER_GROUP

LANES = 128
VMEM_LIMIT = 48 * 1024 * 1024

INPROJ_TM = 1024
INPROJ_TN = 512
ATTN_T = 256
HGRN_TT = 256
HGRN_C = 32
MERGE_TM = 512
MOE_TM = 1024


def _params(sem):
    return pltpu.CompilerParams(dimension_semantics=sem, vmem_limit_bytes=VMEM_LIMIT)


def _split_dot(x, m, passes):
    acc = None
    r = x
    for p in range(passes):
        h = r.astype(BF16)
        term = jnp.dot(h, m, preferred_element_type=F32)
        acc = term if acc is None else acc + term
        if p + 1 < passes:
            r = r - h.astype(F32)
    return acc


def _inproj_kernel(x_ref, g_ref, w_ref, qkv_ref, rest_ref, h_scr, *, n_qkv_blocks, q_scale):
    j = pl.program_id(1)

    @pl.when(j == 0)
    def _():
        x = x_ref[...]
        var = jnp.mean(x * x, axis=-1, keepdims=True)
        h_scr[...] = (x * lax.rsqrt(var + EPS) * g_ref[...]).astype(BF16)

    acc = jnp.dot(h_scr[...], w_ref[...], preferred_element_type=F32)

    @pl.when(j == 0)
    def _():
        qkv_ref[...] = (acc * q_scale).astype(BF16)

    @pl.when(jnp.logical_and(j > 0, j < n_qkv_blocks))
    def _():
        qkv_ref[...] = acc.astype(BF16)

    @pl.when(j >= n_qkv_blocks)
    def _():
        rest_ref[...] = acc


def _inproj(x2, ln_g, w_in_bf16, sb_width):
    n, d = x2.shape
    cols = w_in_bf16.shape[1]
    tm, tn = min(INPROJ_TM, n), INPROJ_TN
    assert sb_width == tn, "q block must be exactly one column tile"
    nq = 3 * sb_width // tn
    nj = cols // tn
    kern = functools.partial(_inproj_kernel, n_qkv_blocks=nq, q_scale=SB_HEAD_DIM ** -0.5)
    return pl.pallas_call(
        kern,
        grid=(n // tm, nj),
        in_specs=[
            pl.BlockSpec((tm, d), lambda i, j: (i, 0)),
            pl.BlockSpec((1, d), lambda i, j: (0, 0)),
            pl.BlockSpec((d, tn), lambda i, j: (0, j)),
        ],
        out_specs=[
            pl.BlockSpec((tm, tn), lambda i, j: (i, jnp.minimum(j, nq - 1))),
            pl.BlockSpec((tm, tn), lambda i, j: (i, jnp.maximum(j - nq, 0))),
        ],
        out_shape=[
            jax.ShapeDtypeStruct((n, nq * tn), BF16),
            jax.ShapeDtypeStruct((n, cols - nq * tn), F32),
        ],
        scratch_shapes=[pltpu.VMEM((tm, d), BF16)],
        compiler_params=_params(("parallel", "arbitrary")),
        name="inproj",
    )(x2, ln_g.reshape(1, d), w_in_bf16)


def _attn_tile(qh, k, v, tri, carry, diag_mask):
    z = lax.dot_general(qh, k, (((1,), (1,)), ((), ())), preferred_element_type=F32)
    lk = jnp.minimum(-z, 0.0) - jnp.log(1.0 + jnp.exp(-jnp.abs(z)))
    if diag_mask is not None:
        lk = jnp.where(diag_mask, lk, 0.0)
    incl = _split_dot(lk, tri, 2)
    a = jnp.exp(z + incl + carry)
    if diag_mask is not None:
        a = jnp.where(diag_mask, a, 0.0)
    pv = jnp.dot(a.astype(BF16), v, preferred_element_type=F32)
    return pv, carry + incl[:, 0:1]


def _attn_kernel(q_ref, k_ref, v_ref, tri_ref, o_ref, *, t):
    qi = pl.program_id(2)
    q = q_ref[...]
    tri = tri_ref[...]
    lane = lax.broadcasted_iota(jnp.int32, (t, LANES), 1)
    head_lanes = [lane < SB_HEAD_DIM, lane >= SB_HEAD_DIM]
    qh = [jnp.where(m, q, jnp.zeros_like(q)) for m in head_lanes]

    row = lax.broadcasted_iota(jnp.int32, (t, t), 0)
    col = lax.broadcasted_iota(jnp.int32, (t, t), 1)
    diag_mask = col < row

    start = pl.multiple_of(qi * t, t)
    k = k_ref[pl.ds(start, t), :]
    v = v_ref[pl.ds(start, t), :]
    zero_c = jnp.zeros((t, 1), F32)
    state = []
    for h in range(2):
        pv, c = _attn_tile(qh[h], k, v, tri, zero_c, diag_mask)
        state += [pv, c]

    def body(i, st):
        s0 = pl.multiple_of((qi - 1 - i) * t, t)
        kt = k_ref[pl.ds(s0, t), :]
        vt = v_ref[pl.ds(s0, t), :]
        out = []
        for h in range(2):
            pv, c = _attn_tile(qh[h], kt, vt, tri, st[2 * h + 1], None)
            out += [st[2 * h] + pv, c]
        return tuple(out)

    st = lax.fori_loop(0, qi, body, tuple(state))
    o_ref[...] = jnp.where(head_lanes[0], st[0], st[2]).astype(o_ref.dtype)


def _attn(qkv, bsz, seq, tri):
    n = bsz * seq
    t = min(ATTN_T, seq)
    nq = seq // t
    pairs = SB_HEADS * SB_HEAD_DIM // LANES
    kern = functools.partial(_attn_kernel, t=t)
    return pl.pallas_call(
        kern,
        grid=(bsz, pairs, nq),
        in_specs=[
            pl.BlockSpec((t, LANES), lambda b, p, i: (b * nq + i, p)),
            pl.BlockSpec((seq, LANES), lambda b, p, i: (b, pairs + p)),
            pl.BlockSpec((seq, LANES), lambda b, p, i: (b, 2 * pairs + p)),
            pl.BlockSpec((t, t), lambda b, p, i: (0, 0)),
        ],
        out_specs=pl.BlockSpec((t, LANES), lambda b, p, i: (b * nq + i, p)),
        out_shape=jax.ShapeDtypeStruct((n, pairs * LANES), BF16),
        compiler_params=_params(("parallel", "parallel", "arbitrary")),
        name="attn",
    )(qkv, qkv, qkv, tri)


def _hgrn_kernel(q_ref, f_ref, i_ref, g_ref, lbl_ref, ng_ref, tri_ref, bd_ref, o_ref,
                 st_scr, kpad, cpad, vpad, *, tt, c, layer):
    ti = pl.program_id(1)
    w = q_ref.shape[1]

    @pl.when(ti == 0)
    def _():
        st_scr[...] = jnp.zeros_like(st_scr)

    lg = lbl_ref[...]
    e = jnp.exp(lg - jnp.max(lg, axis=0, keepdims=True))
    lb = jnp.sum(e[0:layer + 1], axis=0, keepdims=True) / jnp.sum(e, axis=0, keepdims=True)

    f = lb + (1.0 - lb) * jax.nn.sigmoid(f_ref[...])
    kk = 1.0 - f
    qv = q_ref[...]
    qs = qv * jax.nn.sigmoid(qv)
    v = i_ref[...]
    bd = bd_ref[...]
    cum = _split_dot_left(tri_ref[...], jnp.log(f), 3)

    zpad = jnp.zeros((c, w), F32)
    kpad[0:c, :] = zpad
    cpad[0:c, :] = zpad
    vpad[0:c, :] = zpad
    kpad[c:c + tt, :] = kk
    cpad[c:c + tt, :] = cum
    vpad[c:c + tt, :] = v
    pos = lax.broadcasted_iota(jnp.int32, (tt, 1), 0) % c

    acc = jnp.dot((qs * kk).astype(BF16), bd, preferred_element_type=F32) * v
    for d in range(1, c):
        ks = kpad[c - d:c - d + tt, :]
        cs = cpad[c - d:c - d + tt, :]
        vs = vpad[c - d:c - d + tt, :]
        dd = jnp.where(pos >= d, qs * ks * jnp.exp(cum - cs), 0.0)
        acc = acc + jnp.dot(dd.astype(BF16), bd, preferred_element_type=F32) * vs

    bdmask = bd != 0
    outs = []
    for ci in range(tt // c):
        r0 = ci * c
        cum_c = cum[r0:r0 + c]
        last = cum_c[c - 1:c]
        st = st_scr[...]
        qd = (qs[r0:r0 + c] * jnp.exp(cum_c)).astype(BF16)
        outs.append(lax.dot_general(qd, st.astype(BF16), (((1,), (1,)), ((), ())),
                                    preferred_element_type=F32))
        kd = (kk[r0:r0 + c] * jnp.exp(last - cum_c)).astype(BF16)
        upd = lax.dot_general(v[r0:r0 + c].astype(BF16), kd, (((0,), (0,)), ((), ())),
                              preferred_element_type=F32)
        st_scr[...] = st * jnp.exp(last) + jnp.where(bdmask, upd, 0.0)
    o = acc + jnp.concatenate(outs, axis=0)

    ms = _split_dot(o * o, bd, 2) * (1.0 / HG_DIM)
    gv = g_ref[...]
    o = o * lax.rsqrt(ms + EPS) * ng_ref[...] * (gv * jax.nn.sigmoid(gv))
    o_ref[...] = o.astype(o_ref.dtype)


def _split_dot_left(m, x, passes):
    acc = None
    r = x
    for p in range(passes):
        h = r.astype(BF16)
        term = jnp.dot(m, h, preferred_element_type=F32)
        acc = term if acc is None else acc + term
        if p + 1 < passes:
            r = r - h.astype(F32)
    return acc


def _hgrn(rest, lb_logits, norm_g, bsz, seq, layer, tri, bd):
    n = bsz * seq
    w = HG_HEADS * HG_DIM
    tt = min(HGRN_TT, seq)
    c = min(HGRN_C, tt)
    nt = seq // tt
    kern = functools.partial(_hgrn_kernel, tt=tt, c=c, layer=layer)
    col = lambda j: pl.BlockSpec((tt, w), lambda b, i, j=j: (b * nt + i, j))
    const = lambda shape: pl.BlockSpec(shape, lambda b, i: (0, 0))
    return pl.pallas_call(
        kern,
        grid=(bsz, nt),
        in_specs=[col(0), col(1), col(2), col(3),
                  const(lb_logits.shape), const((1, w)), const((tt, tt)), const((w, w))],
        out_specs=pl.BlockSpec((tt, w), lambda b, i: (b * nt + i, 0)),
        out_shape=jax.ShapeDtypeStruct((n, w), BF16),
        scratch_shapes=[pltpu.VMEM((w, w), F32)] + [pltpu.VMEM((c + tt, w), F32)] * 3,
        compiler_params=_params(("parallel", "arbitrary")),
        name="hgrn",
    )(rest, rest, rest, rest, lb_logits, norm_g.reshape(1, w), tri, bd)


def _route(lg):
    lane = lax.broadcasted_iota(jnp.int32, lg.shape, 1)
    neg = jnp.float32(-jnp.inf)
    big = jnp.int32(LANES)
    gmask = jnp.logical_and(lane >= N_EXPERTS, lane < N_EXPERTS + N_GROUPS)
    gl = jnp.where(gmask, lg, neg)
    gmax = jnp.max(gl, axis=1, keepdims=True)
    gidx = jnp.min(jnp.where(gl == gmax, lane, big), axis=1, keepdims=True) - N_EXPERTS
    w_grp = 1.0 / jnp.sum(jnp.where(gmask, jnp.exp(gl - gmax), 0.0), axis=1, keepdims=True)
    in_grp = jnp.logical_and(lane < N_EXPERTS, lane // EXPERTS_PER_GROUP == gidx)
    l1 = jnp.where(in_grp, lg, neg)
    v1 = jnp.max(l1, axis=1, keepdims=True)
    i1 = jnp.min(jnp.where(l1 == v1, lane, big), axis=1, keepdims=True)
    l2 = jnp.where(jnp.logical_and(in_grp, lane != i1), lg, neg)
    v2 = jnp.max(l2, axis=1, keepdims=True)
    i2 = jnp.min(jnp.where(l2 == v2, lane, big), axis=1, keepdims=True)
    e2 = jnp.exp(v2 - v1)
    p1 = 1.0 / (1.0 + e2)
    p2 = e2 * p1
    return (jnp.where(lane == i1, p1 * w_grp, 0.0) + jnp.where(lane == i2, p2 * w_grp, 0.0))


def _merge_kernel(x_ref, ysb_ref, yhg_ref, gsb_ref, ghg_ref, wbs_ref, wbh_ref, wo_ref,
                  ln_ref, wr_hi_ref, wr_lo_ref, br_ref, x1_ref, t_ref, comb_ref):
    a = jnp.dot(ysb_ref[...], wbs_ref[...], preferred_element_type=F32)
    b = jnp.dot(yhg_ref[...], wbh_ref[...], preferred_element_type=F32)
    merged = jax.nn.sigmoid(gsb_ref[...]) * a + jax.nn.sigmoid(ghg_ref[...]) * b
    x1 = x_ref[...] + jnp.dot(merged.astype(BF16), wo_ref[...], preferred_element_type=F32)
    x1_ref[...] = x1
    var = jnp.mean(x1 * x1, axis=-1, keepdims=True)
    t = x1 * lax.rsqrt(var + EPS) * ln_ref[...]
    t_ref[...] = t.astype(BF16)
    t_hi = t.astype(BF16)
    t_lo = (t - t_hi.astype(F32)).astype(BF16)
    lg = (jnp.dot(t_hi, wr_hi_ref[...], preferred_element_type=F32)
          + jnp.dot(t_lo, wr_hi_ref[...], preferred_element_type=F32)
          + jnp.dot(t_hi, wr_lo_ref[...], preferred_element_type=F32)) + br_ref[...]
    comb_ref[...] = _route(lg)


def _merge(x2, y_sb, y_hg, rest, wbs, wbh, wo, ln_g, wr_hi, wr_lo, br):
    n, d = x2.shape
    tm = min(MERGE_TM, n)
    w_sb, w_hg = y_sb.shape[1], y_hg.shape[1]
    gate_blk = (rest.shape[1] - 2 * d) // d
    row = lambda wdt, j=0: pl.BlockSpec((tm, wdt), lambda i, j=j: (i, j))
    const = lambda shape: pl.BlockSpec(shape, lambda i: (0, 0))
    return pl.pallas_call(
        _merge_kernel,
        grid=(n // tm,),
        in_specs=[row(d), row(w_sb), row(w_hg), row(d, gate_blk), row(d, gate_blk + 1),
                  const(wbs.shape), const(wbh.shape), const(wo.shape), const((1, d)),
                  const(wr_hi.shape), const(wr_lo.shape), const((1, LANES))],
        out_specs=[row(d), row(d), row(LANES)],
        out_shape=[jax.ShapeDtypeStruct((n, d), F32), jax.ShapeDtypeStruct((n, d), BF16),
                   jax.ShapeDtypeStruct((n, LANES), F32)],
        compiler_params=_params(("parallel",)),
        name="merge",
    )(x2, y_sb, y_hg, rest, rest, wbs, wbh, wo, ln_g.reshape(1, d), wr_hi, wr_lo, br)


def _moe_kernel(t_ref, comb_ref, wg_ref, wu_ref, wd_ref, x1_ref, fg_ref, o_ref, acc_scr):
    e = pl.program_id(1)

    @pl.when(e == 0)
    def _():
        acc_scr[...] = jnp.zeros_like(acc_scr)

    t = t_ref[...]
    hg = jnp.dot(t, wg_ref[...], preferred_element_type=F32)
    hu = jnp.dot(t, wu_ref[...], preferred_element_type=F32)
    a = (hg * jax.nn.sigmoid(hg) * hu).astype(BF16)
    y = jnp.dot(a, wd_ref[...], preferred_element_type=F32)
    comb = comb_ref[...]
    lane = lax.broadcasted_iota(jnp.int32, comb.shape, 1)
    ce = jnp.sum(jnp.where(lane == e, comb, 0.0), axis=1, keepdims=True)
    acc_scr[...] += ce * y

    @pl.when(e == pl.num_programs(1) - 1)
    def _():
        x2 = x1_ref[...] + acc_scr[...]
        var = jnp.mean(x2 * x2, axis=-1, keepdims=True)
        o_ref[...] = x2 * lax.rsqrt(var + EPS) * fg_ref[...]


def _moe(t, comb, wg, wu, wd, x1, final_g):
    n, d = t.shape
    ne, _, de = wg.shape
    tm = min(MOE_TM, n)
    return pl.pallas_call(
        _moe_kernel,
        grid=(n // tm, ne),
        in_specs=[
            pl.BlockSpec((tm, d), lambda i, e: (i, 0)),
            pl.BlockSpec((tm, LANES), lambda i, e: (i, 0)),
            pl.BlockSpec((None, d, de), lambda i, e: (e, 0, 0)),
            pl.BlockSpec((None, d, de), lambda i, e: (e, 0, 0)),
            pl.BlockSpec((None, de, d), lambda i, e: (e, 0, 0)),
            pl.BlockSpec((tm, d), lambda i, e: (i, 0)),
            pl.BlockSpec((1, d), lambda i, e: (0, 0)),
        ],
        out_specs=pl.BlockSpec((tm, d), lambda i, e: (i, 0)),
        out_shape=jax.ShapeDtypeStruct((n, d), F32),
        scratch_shapes=[pltpu.VMEM((tm, d), F32)],
        compiler_params=_params(("parallel", "arbitrary")),
        name="moe",
    )(t, comb, wg, wu, wd, x1, final_g.reshape(1, d))


def _suffix_ones(t):
    j = np.arange(t)[:, None]
    s = np.arange(t)[None, :]
    return jnp.asarray((j >= s).astype(np.float32), dtype=BF16)


def _chunk_prefix_ones(tt, c):
    t = np.arange(tt)[:, None]
    j = np.arange(tt)[None, :]
    return jnp.asarray(((j <= t) & (t // c == j // c)).astype(np.float32), dtype=BF16)


def _block_diag_ones(w, blk):
    a = np.arange(w)
    return jnp.asarray((a[:, None] // blk == a[None, :] // blk).astype(np.float32), dtype=BF16)


def kernel(x, ln1_g, w_in, w_branch_sb, w_branch_hg, hg_norm_g, hg_lb_logits, w_out, ln2_g,
           w_router_group, b_router_group, w_router_expert, b_router_expert,
           w_exp_gate, w_exp_up, w_exp_down, final_g):
    bsz, seq, d = x.shape
    depth = w_in.shape[0]
    n = bsz * seq
    sb_width = SB_HEADS * SB_HEAD_DIM
    hg_width = HG_HEADS * HG_DIM

    tri_attn = _suffix_ones(min(ATTN_T, seq))
    tt = min(HGRN_TT, seq)
    tri_hg = _chunk_prefix_ones(tt, min(HGRN_C, tt))
    bd = _block_diag_ones(hg_width, HG_DIM)

    x2 = x.reshape(n, d)
    for l in range(depth):
        qkv, rest = _inproj(x2, ln1_g[l], w_in[l].astype(BF16), sb_width)
        y_sb = _attn(qkv, bsz, seq, tri_attn)
        y_hg = _hgrn(rest, hg_lb_logits, hg_norm_g[l], bsz, seq, l, tri_hg, bd)

        pad = LANES - N_EXPERTS - N_GROUPS
        wr = jnp.concatenate([w_router_expert[l], w_router_group[l],
                              jnp.zeros((d, pad), F32)], axis=1)
        wr_hi = wr.astype(BF16)
        wr_lo = (wr - wr_hi.astype(F32)).astype(BF16)
        br = jnp.concatenate([b_router_expert[l], b_router_group[l],
                              jnp.zeros((pad,), F32)]).reshape(1, LANES)

        last = l == depth - 1
        x1, t, comb = _merge(x2, y_sb, y_hg, rest, w_branch_sb[l].astype(BF16),
                             w_branch_hg[l].astype(BF16), w_out[l].astype(BF16), ln2_g[l],
                             wr_hi, wr_lo, br)
        assert last, "final rmsnorm is fused into the last layer's expert kernel"
        x2 = _moe(t, comb, w_exp_gate[l].astype(BF16), w_exp_up[l].astype(BF16),
                  w_exp_down[l].astype(BF16), x1, final_g)
    return x2.reshape(bsz, seq, d)
```

```python
import functools

import jax
import jax.numpy as jnp
import numpy as np
from jax import lax
from jax.experimental import pallas as pl
from jax.experimental.pallas import tpu as pltpu

F32 = jnp.float32
BF16 = jnp.bfloat16

EPS = 1e-6
SB_HEADS = 8
SB_HEAD_DIM = 64
HG_HEADS = 8
HG_DIM = 64
N_GROUPS = 4
EXPERTS_PER_GROUP = 4
N_EXPERTS = N_GROUPS * EXPERTS_PER_GROUP

LANES = 128
VMEM_LIMIT = 48 * 1024 * 1024

INPROJ_TM = 1024
INPROJ_TN = 512
ATTN_T = 256
HGRN_TT = 256
HGRN_C = 32
MERGE_TM = 512
MOE_TM = 1024


def _params(sem):
    return pltpu.CompilerParams(dimension_semantics=sem, vmem_limit_bytes=VMEM_LIMIT)


def _split_dot(x, m, passes):
    acc = None
    r = x
    for p in range(passes):
        h = r.astype(BF16)
        term = jnp.dot(h, m, preferred_element_type=F32)
        acc = term if acc is None else acc + term
        if p + 1 < passes:
            r = r - h.astype(F32)
    return acc


def _inproj_kernel(x_ref, g_ref, w_ref, qkv_ref, rest_ref, h_scr, *, n_qkv_blocks, q_scale):
    j = pl.program_id(1)

    @pl.when(j == 0)
    def _():
        x = x_ref[...]
        var = jnp.mean(x * x, axis=-1, keepdims=True)
        h_scr[...] = (x * lax.rsqrt(var + EPS) * g_ref[...]).astype(BF16)

    acc = jnp.dot(h_scr[...], w_ref[...], preferred_element_type=F32)

    @pl.when(j == 0)
    def _():
        qkv_ref[...] = (acc * q_scale).astype(BF16)

    @pl.when(jnp.logical_and(j > 0, j < n_qkv_blocks))
    def _():
        qkv_ref[...] = acc.astype(BF16)

    @pl.when(j >= n_qkv_blocks)
    def _():
        rest_ref[...] = acc


def _inproj(x2, ln_g, w_in_bf16, sb_width):
    n, d = x2.shape
    cols = w_in_bf16.shape[1]
    tm, tn = min(INPROJ_TM, n), INPROJ_TN
    assert sb_width == tn, "q block must be exactly one column tile"
    nq = 3 * sb_width // tn
    nj = cols // tn
    kern = functools.partial(_inproj_kernel, n_qkv_blocks=nq, q_scale=SB_HEAD_DIM ** -0.5)
    return pl.pallas_call(
        kern,
        grid=(n // tm, nj),
        in_specs=[
            pl.BlockSpec((tm, d), lambda i, j: (i, 0)),
            pl.BlockSpec((1, d), lambda i, j: (0, 0)),
            pl.BlockSpec((d, tn), lambda i, j: (0, j)),
        ],
        out_specs=[
            pl.BlockSpec((tm, tn), lambda i, j: (i, jnp.minimum(j, nq - 1))),
            pl.BlockSpec((tm, tn), lambda i, j: (i, jnp.maximum(j - nq, 0))),
        ],
        out_shape=[
            jax.ShapeDtypeStruct((n, nq * tn), BF16),
            jax.ShapeDtypeStruct((n, cols - nq * tn), F32),
        ],
        scratch_shapes=[pltpu.VMEM((tm, d), BF16)],
        compiler_params=_params(("parallel", "arbitrary")),
        name="inproj",
    )(x2, ln_g.reshape(1, d), w_in_bf16)


ATTN_STAGES = 3
LOG2E = 1.4426950408889634
MASK_BIAS = -1e30


def _attn_kernel(tq_ref, tk_ref, q_ref, k_ref, v_ref, tri2_ref, bias_ref, o_ref,
                 z0, z1, z2, i0, i1, i2, acc_ref, c_ref, *, t, n_iter):
    zbuf = (z0, z1, z2)
    ibuf = (i0, i1, i2)
    for r in zbuf + ibuf:
        r[...] = jnp.zeros_like(r)
    acc_ref[...] = jnp.zeros_like(acc_ref)
    c_ref[...] = jnp.zeros_like(c_ref)

    lane = lax.broadcasted_iota(jnp.int32, (t, LANES), 1)
    head0 = lane < SB_HEAD_DIM
    nt = (((1,), (1,)), ((), ()))
    sign = jnp.uint32(0x80000000)

    def stage_a(it, slot):
        qi, kj = tq_ref[it], tk_ref[it]
        q = q_ref[pl.ds(pl.multiple_of(qi * t, t), t), :]
        zero = jnp.zeros_like(q)
        q2 = jnp.concatenate([jnp.where(head0, q, zero), jnp.where(head0, zero, q)], axis=0)
        k = k_ref[pl.ds(pl.multiple_of(kj * t, t), t), :]
        z = lax.dot_general(q2, k, nt, preferred_element_type=F32)
        bias = bias_ref[(qi == kj).astype(jnp.int32)]
        zbuf[slot][...] = z * LOG2E + jnp.concatenate([bias, bias], axis=0)

    def stage_b(slot):
        z = zbuf[slot][...]
        neg_abs = pltpu.bitcast(pltpu.bitcast(z, jnp.uint32) | sign, F32)
        p = jnp.maximum(z, 0.0) + jnp.log(1.0 + jnp.exp2(neg_abs)) * LOG2E
        hi = p.astype(BF16)
        lo = (p - hi.astype(F32)).astype(BF16)
        ibuf[slot][...] = jnp.dot(jnp.concatenate([hi, lo], axis=1), tri2_ref[...],
                                  preferred_element_type=F32)

    def stage_c(it, slot):
        u = jnp.maximum(it - (ATTN_STAGES - 1), 0)
        qi, kj = tq_ref[u], tk_ref[u]
        first = qi == kj
        incl = ibuf[slot][...]
        c = jnp.where(first, 0.0, c_ref[...])
        a = jnp.exp2(zbuf[slot][...] - incl - c)
        v = v_ref[pl.ds(pl.multiple_of(kj * t, t), t), :]
        pv = jnp.dot(a.astype(BF16), v, preferred_element_type=F32)
        acc = jnp.where(first, pv, acc_ref[...] + pv)
        acc_ref[...] = acc
        c_ref[...] = c + incl[:, 0:1]
        o_ref[pl.ds(pl.multiple_of(qi * t, t), t), :] = (
            jnp.where(head0, acc[0:t], acc[t:2 * t]).astype(o_ref.dtype))

    def body(m, carry):
        for r in range(ATTN_STAGES):
            it = m * ATTN_STAGES + r
            stage_c(it, (r + 1) % ATTN_STAGES)
            stage_b((r + 2) % ATTN_STAGES)
            stage_a(it, r)
        return carry

    lax.fori_loop(0, n_iter // ATTN_STAGES, body, 0)


def _attn(qkv, bsz, seq, tri2, bias):
    n = bsz * seq
    t = min(ATTN_T, seq)
    nq = seq // t
    pairs = SB_HEADS * SB_HEAD_DIM // LANES
    tiles = [(qi, kj) for qi in range(nq) for kj in range(qi, -1, -1)]
    n_iter = -(-(len(tiles) + ATTN_STAGES - 1) // ATTN_STAGES) * ATTN_STAGES
    tiles += [(0, 0)] * (n_iter - len(tiles))
    tq = jnp.asarray([qi for qi, _ in tiles], jnp.int32)
    tk = jnp.asarray([kj for _, kj in tiles], jnp.int32)
    kern = functools.partial(_attn_kernel, t=t, n_iter=n_iter)
    grid_spec = pltpu.PrefetchScalarGridSpec(
        num_scalar_prefetch=2,
        grid=(bsz, pairs),
        in_specs=[
            pl.BlockSpec((seq, LANES), lambda b, p, *_: (b, p)),
            pl.BlockSpec((seq, LANES), lambda b, p, *_: (b, pairs + p)),
            pl.BlockSpec((seq, LANES), lambda b, p, *_: (b, 2 * pairs + p)),
            pl.BlockSpec((2 * t, t), lambda b, p, *_: (0, 0)),
            pl.BlockSpec((2, t, t), lambda b, p, *_: (0, 0, 0)),
        ],
        out_specs=pl.BlockSpec((seq, LANES), lambda b, p, *_: (b, p)),
        scratch_shapes=([pltpu.VMEM((2 * t, t), F32)] * (2 * ATTN_STAGES)
                        + [pltpu.VMEM((2 * t, LANES), F32), pltpu.VMEM((2 * t, 1), F32)]),
    )
    return pl.pallas_call(
        kern,
        grid_spec=grid_spec,
        out_shape=jax.ShapeDtypeStruct((n, pairs * LANES), BF16),
        compiler_params=_params(("parallel", "parallel")),
        name="attn",
    )(tq, tk, qkv, qkv, qkv, tri2, bias)


def _hgrn_kernel(q_ref, f_ref, i_ref, g_ref, lbl_ref, ng_ref, tri_ref, bd_ref, o_ref,
                 st_scr, kpad, cpad, vpad, *, tt, c, layer):
    ti = pl.program_id(1)
    w = q_ref.shape[1]

    @pl.when(ti == 0)
    def _():
        st_scr[...] = jnp.zeros_like(st_scr)

    lg = lbl_ref[...]
    e = jnp.exp(lg - jnp.max(lg, axis=0, keepdims=True))
    lb = jnp.sum(e[0:layer + 1], axis=0, keepdims=True) / jnp.sum(e, axis=0, keepdims=True)

    f = lb + (1.0 - lb) * jax.nn.sigmoid(f_ref[...])
    kk = 1.0 - f
    qv = q_ref[...]
    qs = qv * jax.nn.sigmoid(qv)
    v = i_ref[...]
    bd = bd_ref[...]
    cum = _split_dot_left(tri_ref[...], jnp.log(f), 3)

    zpad = jnp.zeros((c, w), F32)
    kpad[0:c, :] = zpad
    cpad[0:c, :] = zpad
    vpad[0:c, :] = zpad
    kpad[c:c + tt, :] = kk
    cpad[c:c + tt, :] = cum
    vpad[c:c + tt, :] = v
    pos = lax.broadcasted_iota(jnp.int32, (tt, 1), 0) % c

    acc = jnp.dot((qs * kk).astype(BF16), bd, preferred_element_type=F32) * v
    for d in range(1, c):
        ks = kpad[c - d:c - d + tt, :]
        cs = cpad[c - d:c - d + tt, :]
        vs = vpad[c - d:c - d + tt, :]
        dd = jnp.where(pos >= d, qs * ks * jnp.exp(cum - cs), 0.0)
        acc = acc + jnp.dot(dd.astype(BF16), bd, preferred_element_type=F32) * vs

    bdmask = bd != 0
    outs = []
    for ci in range(tt // c):
        r0 = ci * c
        cum_c = cum[r0:r0 + c]
        last = cum_c[c - 1:c]
        st = st_scr[...]
        qd = (qs[r0:r0 + c] * jnp.exp(cum_c)).astype(BF16)
        outs.append(lax.dot_general(qd, st.astype(BF16), (((1,), (1,)), ((), ())),
                                    preferred_element_type=F32))
        kd = (kk[r0:r0 + c] * jnp.exp(last - cum_c)).astype(BF16)
        upd = lax.dot_general(v[r0:r0 + c].astype(BF16), kd, (((0,), (0,)), ((), ())),
                              preferred_element_type=F32)
        st_scr[...] = st * jnp.exp(last) + jnp.where(bdmask, upd, 0.0)
    o = acc + jnp.concatenate(outs, axis=0)

    ms = _split_dot(o * o, bd, 2) * (1.0 / HG_DIM)
    gv = g_ref[...]
    o = o * lax.rsqrt(ms + EPS) * ng_ref[...] * (gv * jax.nn.sigmoid(gv))
    o_ref[...] = o.astype(o_ref.dtype)


def _split_dot_left(m, x, passes):
    acc = None
    r = x
    for p in range(passes):
        h = r.astype(BF16)
        term = jnp.dot(m, h, preferred_element_type=F32)
        acc = term if acc is None else acc + term
        if p + 1 < passes:
            r = r - h.astype(F32)
    return acc


def _hgrn(rest, lb_logits, norm_g, bsz, seq, layer, tri, bd):
    n = bsz * seq
    w = HG_HEADS * HG_DIM
    tt = min(HGRN_TT, seq)
    c = min(HGRN_C, tt)
    nt = seq // tt
    kern = functools.partial(_hgrn_kernel, tt=tt, c=c, layer=layer)
    col = lambda j: pl.BlockSpec((tt, w), lambda b, i, j=j: (b * nt + i, j))
    const = lambda shape: pl.BlockSpec(shape, lambda b, i: (0, 0))
    return pl.pallas_call(
        kern,
        grid=(bsz, nt),
        in_specs=[col(0), col(1), col(2), col(3),
                  const(lb_logits.shape), const((1, w)), const((tt, tt)), const((w, w))],
        out_specs=pl.BlockSpec((tt, w), lambda b, i: (b * nt + i, 0)),
        out_shape=jax.ShapeDtypeStruct((n, w), BF16),
        scratch_shapes=[pltpu.VMEM((w, w), F32)] + [pltpu.VMEM((c + tt, w), F32)] * 3,
        compiler_params=_params(("parallel", "arbitrary")),
        name="hgrn",
    )(rest, rest, rest, rest, lb_logits, norm_g.reshape(1, w), tri, bd)


def _route(lg):
    lane = lax.broadcasted_iota(jnp.int32, lg.shape, 1)
    neg = jnp.float32(-jnp.inf)
    big = jnp.int32(LANES)
    gmask = jnp.logical_and(lane >= N_EXPERTS, lane < N_EXPERTS + N_GROUPS)
    gl = jnp.where(gmask, lg, neg)
    gmax = jnp.max(gl, axis=1, keepdims=True)
    gidx = jnp.min(jnp.where(gl == gmax, lane, big), axis=1, keepdims=True) - N_EXPERTS
    w_grp = 1.0 / jnp.sum(jnp.where(gmask, jnp.exp(gl - gmax), 0.0), axis=1, keepdims=True)
    in_grp = jnp.logical_and(lane < N_EXPERTS, lane // EXPERTS_PER_GROUP == gidx)
    l1 = jnp.where(in_grp, lg, neg)
    v1 = jnp.max(l1, axis=1, keepdims=True)
    i1 = jnp.min(jnp.where(l1 == v1, lane, big), axis=1, keepdims=True)
    l2 = jnp.where(jnp.logical_and(in_grp, lane != i1), lg, neg)
    v2 = jnp.max(l2, axis=1, keepdims=True)
    i2 = jnp.min(jnp.where(l2 == v2, lane, big), axis=1, keepdims=True)
    e2 = jnp.exp(v2 - v1)
    p1 = 1.0 / (1.0 + e2)
    p2 = e2 * p1
    return (jnp.where(lane == i1, p1 * w_grp, 0.0) + jnp.where(lane == i2, p2 * w_grp, 0.0))


def _merge_kernel(x_ref, ysb_ref, yhg_ref, gsb_ref, ghg_ref, wbs_ref, wbh_ref, wo_ref,
                  ln_ref, wr_hi_ref, wr_lo_ref, br_ref, x1_ref, t_ref, comb_ref):
    a = jnp.dot(ysb_ref[...], wbs_ref[...], preferred_element_type=F32)
    b = jnp.dot(yhg_ref[...], wbh_ref[...], preferred_element_type=F32)
    merged = jax.nn.sigmoid(gsb_ref[...]) * a + jax.nn.sigmoid(ghg_ref[...]) * b
    x1 = x_ref[...] + jnp.dot(merged.astype(BF16), wo_ref[...], preferred_element_type=F32)
    x1_ref[...] = x1
    var = jnp.mean(x1 * x1, axis=-1, keepdims=True)
    t = x1 * lax.rsqrt(var + EPS) * ln_ref[...]
    t_ref[...] = t.astype(BF16)
    t_hi = t.astype(BF16)
    t_lo = (t - t_hi.astype(F32)).astype(BF16)
    lg = (jnp.dot(t_hi, wr_hi_ref[...], preferred_element_type=F32)
          + jnp.dot(t_lo, wr_hi_ref[...], preferred_element_type=F32)
          + jnp.dot(t_hi, wr_lo_ref[...], preferred_element_type=F32)) + br_ref[...]
    comb_ref[...] = _route(lg)


def _merge(x2, y_sb, y_hg, rest, wbs, wbh, wo, ln_g, wr_hi, wr_lo, br):
    n, d = x2.shape
    tm = min(MERGE_TM, n)
    w_sb, w_hg = y_sb.shape[1], y_hg.shape[1]
    gate_blk = (rest.shape[1] - 2 * d) // d
    row = lambda wdt, j=0: pl.BlockSpec((tm, wdt), lambda i, j=j: (i, j))
    const = lambda shape: pl.BlockSpec(shape, lambda i: (0, 0))
    return pl.pallas_call(
        _merge_kernel,
        grid=(n // tm,),
        in_specs=[row(d), row(w_sb), row(w_hg), row(d, gate_blk), row(d, gate_blk + 1),
                  const(wbs.shape), const(wbh.shape), const(wo.shape), const((1, d)),
                  const(wr_hi.shape), const(wr_lo.shape), const((1, LANES))],
        out_specs=[row(d), row(d), row(LANES)],
        out_shape=[jax.ShapeDtypeStruct((n, d), F32), jax.ShapeDtypeStruct((n, d), BF16),
                   jax.ShapeDtypeStruct((n, LANES), F32)],
        compiler_params=_params(("parallel",)),
        name="merge",
    )(x2, y_sb, y_hg, rest, rest, wbs, wbh, wo, ln_g.reshape(1, d), wr_hi, wr_lo, br)


def _moe_kernel(t_ref, comb_ref, wg_ref, wu_ref, wd_ref, x1_ref, fg_ref, o_ref, acc_scr):
    e = pl.program_id(1)

    @pl.when(e == 0)
    def _():
        acc_scr[...] = jnp.zeros_like(acc_scr)

    t = t_ref[...]
    hg = jnp.dot(t, wg_ref[...], preferred_element_type=F32)
    hu = jnp.dot(t, wu_ref[...], preferred_element_type=F32)
    a = (hg * jax.nn.sigmoid(hg) * hu).astype(BF16)
    y = jnp.dot(a, wd_ref[...], preferred_element_type=F32)
    comb = comb_ref[...]
    lane = lax.broadcasted_iota(jnp.int32, comb.shape, 1)
    ce = jnp.sum(jnp.where(lane == e, comb, 0.0), axis=1, keepdims=True)
    acc_scr[...] += ce * y

    @pl.when(e == pl.num_programs(1) - 1)
    def _():
        x2 = x1_ref[...] + acc_scr[...]
        var = jnp.mean(x2 * x2, axis=-1, keepdims=True)
        o_ref[...] = x2 * lax.rsqrt(var + EPS) * fg_ref[...]


def _moe(t, comb, wg, wu, wd, x1, final_g):
    n, d = t.shape
    ne, _, de = wg.shape
    tm = min(MOE_TM, n)
    return pl.pallas_call(
        _moe_kernel,
        grid=(n // tm, ne),
        in_specs=[
            pl.BlockSpec((tm, d), lambda i, e: (i, 0)),
            pl.BlockSpec((tm, LANES), lambda i, e: (i, 0)),
            pl.BlockSpec((None, d, de), lambda i, e: (e, 0, 0)),
            pl.BlockSpec((None, d, de), lambda i, e: (e, 0, 0)),
            pl.BlockSpec((None, de, d), lambda i, e: (e, 0, 0)),
            pl.BlockSpec((tm, d), lambda i, e: (i, 0)),
            pl.BlockSpec((1, d), lambda i, e: (0, 0)),
        ],
        out_specs=pl.BlockSpec((tm, d), lambda i, e: (i, 0)),
        out_shape=jax.ShapeDtypeStruct((n, d), F32),
        scratch_shapes=[pltpu.VMEM((tm, d), F32)],
        compiler_params=_params(("parallel", "arbitrary")),
        name="moe",
    )(t, comb, wg, wu, wd, x1, final_g.reshape(1, d))


def _suffix_ones2(t):
    j = np.arange(t)[:, None]
    s = np.arange(t)[None, :]
    m = (j >= s).astype(np.float32)
    return jnp.asarray(np.concatenate([m, m], axis=0), dtype=BF16)


def _causal_bias(t):
    row = np.arange(t)[:, None]
    col = np.arange(t)[None, :]
    diag = np.where(col < row, 0.0, MASK_BIAS).astype(np.float32)
    return jnp.asarray(np.stack([np.zeros_like(diag), diag]))


def _chunk_prefix_ones(tt, c):
    t = np.arange(tt)[:, None]
    j = np.arange(tt)[None, :]
    return jnp.asarray(((j <= t) & (t // c == j // c)).astype(np.float32), dtype=BF16)


def _block_diag_ones(w, blk):
    a = np.arange(w)
    return jnp.asarray((a[:, None] // blk == a[None, :] // blk).astype(np.float32), dtype=BF16)


def kernel(x, ln1_g, w_in, w_branch_sb, w_branch_hg, hg_norm_g, hg_lb_logits, w_out, ln2_g,
           w_router_group, b_router_group, w_router_expert, b_router_expert,
           w_exp_gate, w_exp_up, w_exp_down, final_g):
    bsz, seq, d = x.shape
    depth = w_in.shape[0]
    n = bsz * seq
    sb_width = SB_HEADS * SB_HEAD_DIM
    hg_width = HG_HEADS * HG_DIM

    tri_attn = _suffix_ones2(min(ATTN_T, seq))
    bias_attn = _causal_bias(min(ATTN_T, seq))
    tt = min(HGRN_TT, seq)
    tri_hg = _chunk_prefix_ones(tt, min(HGRN_C, tt))
    bd = _block_diag_ones(hg_width, HG_DIM)

    x2 = x.reshape(n, d)
    for l in range(depth):
        qkv, rest = _inproj(x2, ln1_g[l], w_in[l].astype(BF16), sb_width)
        y_sb = _attn(qkv, bsz, seq, tri_attn, bias_attn)
        y_hg = _hgrn(rest, hg_lb_logits, hg_norm_g[l], bsz, seq, l, tri_hg, bd)

        pad = LANES - N_EXPERTS - N_GROUPS
        wr = jnp.concatenate([w_router_expert[l], w_router_group[l],
                              jnp.zeros((d, pad), F32)], axis=1)
        wr_hi = wr.astype(BF16)
        wr_lo = (wr - wr_hi.astype(F32)).astype(BF16)
        br = jnp.concatenate([b_router_expert[l], b_router_group[l],
                              jnp.zeros((pad,), F32)]).reshape(1, LANES)

        last = l == depth - 1
        x1, t, comb = _merge(x2, y_sb, y_hg, rest, w_branch_sb[l].astype(BF16),
                             w_branch_hg[l].astype(BF16), w_out[l].astype(BF16), ln2_g[l],
                             wr_hi, wr_lo, br)
        assert last, "final rmsnorm is fused into the last layer's expert kernel"
        x2 = _moe(t, comb, w_exp_gate[l].astype(BF16), w_exp_up[l].astype(BF16),
                  w_exp_down[l].astype(BF16), x1, final_g)
    return x2.reshape(bsz, seq, d)
```

```python
import functools

import jax
import jax.numpy as jnp
import numpy as np
from jax import lax
from jax.experimental import pallas as pl
from jax.experimental.pallas import tpu as pltpu

F32 = jnp.float32
BF16 = jnp.bfloat16

EPS = 1e-6
SB_HEADS = 8
SB_HEAD_DIM = 64
HG_HEADS = 8
HG_DIM = 64
N_GROUPS = 4
EXPERTS_PER_GROUP = 4
N_EXPERTS = N_GROUPS * EXPERTS_PER_GROUP

LANES = 128
SUBLANES = 8
MXU_DIM = 256
LOG2E = 1.4426950408889634
VMEM_LIMIT = 48 * 1024 * 1024

INPROJ_TM = 1024
INPROJ_TN = 512
ATTN_T = MXU_DIM
HGRN_TT = 256
HGRN_C = 32
MERGE_TM = 512
MOE_TM = 1024


def _params(sem):
    return pltpu.CompilerParams(dimension_semantics=sem, vmem_limit_bytes=VMEM_LIMIT)


def _split_dot(x, m, passes):
    acc = None
    r = x
    for p in range(passes):
        h = r.astype(BF16)
        term = jnp.dot(h, m, preferred_element_type=F32)
        acc = term if acc is None else acc + term
        if p + 1 < passes:
            r = r - h.astype(F32)
    return acc


def _inproj_kernel(x_ref, g_ref, w_ref, qkv_ref, rest_ref, h_scr, *, n_qkv_blocks, q_scale):
    j = pl.program_id(1)

    @pl.when(j == 0)
    def _():
        x = x_ref[...]
        var = jnp.mean(x * x, axis=-1, keepdims=True)
        h_scr[...] = (x * lax.rsqrt(var + EPS) * g_ref[...]).astype(BF16)

    acc = jnp.dot(h_scr[...], w_ref[...], preferred_element_type=F32)

    @pl.when(j == 0)
    def _():
        qkv_ref[...] = (acc * q_scale).astype(BF16)

    @pl.when(jnp.logical_and(j > 0, j < n_qkv_blocks))
    def _():
        qkv_ref[...] = acc.astype(BF16)

    @pl.when(j >= n_qkv_blocks)
    def _():
        rest_ref[...] = acc


def _inproj(x2, ln_g, w_in_bf16, sb_width):
    n, d = x2.shape
    cols = w_in_bf16.shape[1]
    tm, tn = min(INPROJ_TM, n), INPROJ_TN
    assert sb_width == tn, "q block must be exactly one column tile"
    nq = 3 * sb_width // tn
    nj = cols // tn
    kern = functools.partial(_inproj_kernel, n_qkv_blocks=nq, q_scale=SB_HEAD_DIM ** -0.5)
    return pl.pallas_call(
        kern,
        grid=(n // tm, nj),
        in_specs=[
            pl.BlockSpec((tm, d), lambda i, j: (i, 0)),
            pl.BlockSpec((1, d), lambda i, j: (0, 0)),
            pl.BlockSpec((d, tn), lambda i, j: (0, j)),
        ],
        out_specs=[
            pl.BlockSpec((tm, tn), lambda i, j: (i, jnp.minimum(j, nq - 1))),
            pl.BlockSpec((tm, tn), lambda i, j: (i, jnp.maximum(j - nq, 0))),
        ],
        out_shape=[
            jax.ShapeDtypeStruct((n, nq * tn), BF16),
            jax.ShapeDtypeStruct((n, cols - nq * tn), F32),
        ],
        scratch_shapes=[pltpu.VMEM((tm, d), BF16)],
        compiler_params=_params(("parallel", "arbitrary")),
        name="inproj",
    )(x2, ln_g.reshape(1, d), w_in_bf16)


ATTN_STAGES = 3
MASK_BIAS = -1e30


def _attn_kernel(tq_ref, tk_ref, q_ref, k_ref, v_ref, tri_ref, bias_ref, o_ref,
                 z0, z1, z2, i0, i1, i2, acc_ref, c_ref, *, t, n_iter):
    zbuf = (z0, z1, z2)
    ibuf = (i0, i1, i2)
    for r in zbuf + ibuf:
        r[...] = jnp.zeros_like(r)
    acc_ref[...] = jnp.zeros_like(acc_ref)
    c_ref[...] = jnp.zeros_like(c_ref)

    lane = lax.broadcasted_iota(jnp.int32, (t, LANES), 1)
    head0 = lane < SB_HEAD_DIM
    nt = (((1,), (1,)), ((), ()))

    def stage_a(it, slot):
        qi, kj = tq_ref[it], tk_ref[it]
        q = q_ref[pl.ds(pl.multiple_of(qi * t, t), t), :]
        zero = jnp.zeros_like(q)
        q2 = jnp.concatenate([jnp.where(head0, q, zero), jnp.where(head0, zero, q)], axis=0)
        k = k_ref[pl.ds(pl.multiple_of(kj * t, t), t), :]
        z = lax.dot_general(q2, k, nt, preferred_element_type=F32)
        bias = bias_ref[(qi == kj).astype(jnp.int32)]
        zbuf[slot][...] = z * LOG2E + jnp.concatenate([bias, bias], axis=0)

    def stage_b(slot):
        z = zbuf[slot][...]
        p = jnp.maximum(z, 0.0) + jnp.log(1.0 + jnp.exp2(-jnp.abs(z))) * LOG2E
        ibuf[slot][...] = jnp.dot(p.astype(BF16), tri_ref[...], preferred_element_type=F32)

    def stage_c(it, slot):
        u = jnp.maximum(it - (ATTN_STAGES - 1), 0)
        qi, kj = tq_ref[u], tk_ref[u]
        first = qi == kj
        incl = ibuf[slot][...]
        c = jnp.where(first, 0.0, c_ref[...])
        a = jnp.exp2(zbuf[slot][...] - incl - c)
        v = v_ref[pl.ds(pl.multiple_of(kj * t, t), t), :]
        pv = jnp.dot(a.astype(BF16), v, preferred_element_type=F32)
        acc = jnp.where(first, pv, acc_ref[...] + pv)
        acc_ref[...] = acc
        c_ref[...] = c + incl[:, 0:1]
        o_ref[pl.ds(pl.multiple_of(qi * t, t), t), :] = (
            jnp.where(head0, acc[0:t], acc[t:2 * t]).astype(o_ref.dtype))

    def body(m, carry):
        for r in range(ATTN_STAGES):
            it = m * ATTN_STAGES + r
            stage_c(it, (r + 1) % ATTN_STAGES)
            stage_b((r + 2) % ATTN_STAGES)
            stage_a(it, r)
        return carry

    lax.fori_loop(0, n_iter // ATTN_STAGES, body, 0)


def _attn(qkv, bsz, seq, tri, bias):
    n = bsz * seq
    t = min(ATTN_T, seq)
    nq = seq // t
    pairs = SB_HEADS * SB_HEAD_DIM // LANES
    tiles = [(qi, kj) for qi in range(nq) for kj in range(qi, -1, -1)]
    n_iter = -(-(len(tiles) + ATTN_STAGES - 1) // ATTN_STAGES) * ATTN_STAGES
    tiles += [(0, 0)] * (n_iter - len(tiles))
    tq = jnp.asarray([qi for qi, _ in tiles], jnp.int32)
    tk = jnp.asarray([kj for _, kj in tiles], jnp.int32)
    kern = functools.partial(_attn_kernel, t=t, n_iter=n_iter)
    grid_spec = pltpu.PrefetchScalarGridSpec(
        num_scalar_prefetch=2,
        grid=(bsz, pairs),
        in_specs=[
            pl.BlockSpec((seq, LANES), lambda b, p, *_: (b, p)),
            pl.BlockSpec((seq, LANES), lambda b, p, *_: (b, pairs + p)),
            pl.BlockSpec((seq, LANES), lambda b, p, *_: (b, 2 * pairs + p)),
            pl.BlockSpec((t, t), lambda b, p, *_: (0, 0)),
            pl.BlockSpec((2, t, t), lambda b, p, *_: (0, 0, 0)),
        ],
        out_specs=pl.BlockSpec((seq, LANES), lambda b, p, *_: (b, p)),
        scratch_shapes=([pltpu.VMEM((2 * t, t), F32)] * (2 * ATTN_STAGES)
                        + [pltpu.VMEM((2 * t, LANES), F32), pltpu.VMEM((2 * t, 1), F32)]),
    )
    return pl.pallas_call(
        kern,
        grid_spec=grid_spec,
        out_shape=jax.ShapeDtypeStruct((n, pairs * LANES), BF16),
        compiler_params=_params(("parallel", "parallel")),
        name="attn",
    )(tq, tk, qkv, qkv, qkv, tri, bias)


def _group_dot(x, bd):
    g = bd.shape[0]
    parts = [jnp.dot(x[:, i:i + g], bd, preferred_element_type=F32)
             for i in range(0, x.shape[1], g)]
    return jnp.concatenate(parts, axis=1)


def _hgrn_kernel(q_ref, f_ref, i_ref, g_ref, lbl_ref, ng_ref, tri_ref, bd_ref, o_ref,
                 st_scr, wpad, vpad, wsh, vsh, *, tt, c, layer):
    ti = pl.program_id(1)
    w = q_ref.shape[1]
    grp = bd_ref.shape[0]
    nch = tt // c

    @pl.when(ti == 0)
    def _():
        st_scr[...] = jnp.zeros_like(st_scr)

    lg = lbl_ref[...]
    e = jnp.exp(lg - jnp.max(lg, axis=0, keepdims=True))
    lb = jnp.sum(e[0:layer + 1], axis=0, keepdims=True) / jnp.sum(e, axis=0, keepdims=True)

    f = lb + (1.0 - lb) * jax.nn.sigmoid(f_ref[...])
    kk = 1.0 - f
    qv = q_ref[...]
    qs = qv * jax.nn.sigmoid(qv)
    v = i_ref[...]
    bd = bd_ref[...]
    cum = _split_dot_left(tri_ref[...], jnp.log(f) * LOG2E, 3)
    wk = jnp.log(kk) * LOG2E - cum

    pos = lax.broadcasted_iota(jnp.int32, (tt, 1), 0) % c
    wpad[0:SUBLANES, :] = jnp.zeros((SUBLANES, w), F32)
    vpad[0:SUBLANES, :] = jnp.zeros((SUBLANES, w), F32)
    wpad[SUBLANES:SUBLANES + tt, :] = wk
    vpad[SUBLANES:SUBLANES + tt, :] = v
    for b in range(SUBLANES):
        wb = wpad[SUBLANES - b:SUBLANES - b + tt, :]
        vb = vpad[SUBLANES - b:SUBLANES - b + tt, :]
        if b:
            wb = jnp.where(pos >= b, wb, -jnp.inf)
        wsh[b] = wb.reshape(nch, c, w)
        vsh[b] = vb.reshape(nch, c, w)

    qs3 = qs.reshape(nch, c, w)
    cum3 = cum.reshape(nch, c, w)
    acc = None
    for a in reversed(range(c // SUBLANES)):
        rows = c - SUBLANES * a
        qa = qs3[:, SUBLANES * a:, :].reshape(nch * rows, w)
        ca = cum3[:, SUBLANES * a:, :].reshape(nch * rows, w)
        acc_a = None
        for b in range(SUBLANES):
            wb = wsh[b, :, 0:rows, :].reshape(nch * rows, w)
            vb = vsh[b, :, 0:rows, :].reshape(nch * rows, w)
            dd = qa * jnp.exp2(ca + wb)
            term = _group_dot(dd.astype(BF16), bd) * vb
            acc_a = term if acc_a is None else acc_a + term
        acc_a = acc_a.reshape(nch, rows, w)
        if acc is not None:
            acc_a = acc_a + jnp.concatenate([jnp.zeros((nch, SUBLANES, w), F32), acc], axis=1)
        acc = acc_a
    acc = acc.reshape(tt, w)

    bdmask = bd != 0
    outs = []
    for ci in range(nch):
        r0 = ci * c
        cum_c = cum[r0:r0 + c]
        last = cum_c[c - 1:c]
        qd = (qs[r0:r0 + c] * jnp.exp2(cum_c)).astype(BF16)
        kd = (kk[r0:r0 + c] * jnp.exp2(last - cum_c)).astype(BF16)
        vc = v[r0:r0 + c].astype(BF16)
        dec = jnp.exp2(last)
        o_parts = []
        for gi in range(w // grp):
            sl = slice(gi * grp, (gi + 1) * grp)
            st = st_scr[gi]
            o_parts.append(lax.dot_general(qd[:, sl], st.astype(BF16), (((1,), (1,)), ((), ())),
                                           preferred_element_type=F32))
            upd = lax.dot_general(vc[:, sl], kd[:, sl], (((0,), (0,)), ((), ())),
                                  preferred_element_type=F32)
            st_scr[gi] = st * dec[:, sl] + jnp.where(bdmask, upd, 0.0)
        outs.append(jnp.concatenate(o_parts, axis=1))
    o = acc + jnp.concatenate(outs, axis=0)

    o2 = o * o
    o2_hi = o2.astype(BF16)
    o2_lo = (o2 - o2_hi.astype(F32)).astype(BF16)
    ms = (_group_dot(o2_hi, bd) + _group_dot(o2_lo, bd)) * (1.0 / HG_DIM)
    gv = g_ref[...]
    o = o * lax.rsqrt(ms + EPS) * ng_ref[...] * (gv * jax.nn.sigmoid(gv))
    o_ref[...] = o.astype(o_ref.dtype)


def _split_dot_left(m, x, passes):
    acc = None
    r = x
    for p in range(passes):
        h = r.astype(BF16)
        term = jnp.dot(m, h, preferred_element_type=F32)
        acc = term if acc is None else acc + term
        if p + 1 < passes:
            r = r - h.astype(F32)
    return acc


def _hgrn(rest, lb_logits, norm_g, bsz, seq, layer, tri, bd):
    n = bsz * seq
    w = HG_HEADS * HG_DIM
    grp = bd.shape[0]
    tt = min(HGRN_TT, seq)
    c = min(HGRN_C, tt)
    nt = seq // tt
    kern = functools.partial(_hgrn_kernel, tt=tt, c=c, layer=layer)
    col = lambda j: pl.BlockSpec((tt, w), lambda b, i, j=j: (b * nt + i, j))
    const = lambda shape: pl.BlockSpec(shape, lambda b, i: (0, 0))
    return pl.pallas_call(
        kern,
        grid=(bsz, nt),
        in_specs=[col(0), col(1), col(2), col(3),
                  const(lb_logits.shape), const((1, w)), const((tt, tt)), const((grp, grp))],
        out_specs=pl.BlockSpec((tt, w), lambda b, i: (b * nt + i, 0)),
        out_shape=jax.ShapeDtypeStruct((n, w), BF16),
        scratch_shapes=([pltpu.VMEM((w // grp, grp, grp), F32)]
                        + [pltpu.VMEM((SUBLANES + tt, w), F32)] * 2
                        + [pltpu.VMEM((SUBLANES, tt // c, c, w), F32)] * 2),
        compiler_params=_params(("parallel", "arbitrary")),
        name="hgrn",
    )(rest, rest, rest, rest, lb_logits, norm_g.reshape(1, w), tri, bd)


def _route(lg):
    lane = lax.broadcasted_iota(jnp.int32, lg.shape, 1)
    neg = jnp.float32(-jnp.inf)
    big = jnp.int32(LANES)
    gmask = jnp.logical_and(lane >= N_EXPERTS, lane < N_EXPERTS + N_GROUPS)
    gl = jnp.where(gmask, lg, neg)
    gmax = jnp.max(gl, axis=1, keepdims=True)
    gidx = jnp.min(jnp.where(gl == gmax, lane, big), axis=1, keepdims=True) - N_EXPERTS
    w_grp = 1.0 / jnp.sum(jnp.where(gmask, jnp.exp(gl - gmax), 0.0), axis=1, keepdims=True)
    in_grp = jnp.logical_and(lane < N_EXPERTS, lane // EXPERTS_PER_GROUP == gidx)
    l1 = jnp.where(in_grp, lg, neg)
    v1 = jnp.max(l1, axis=1, keepdims=True)
    i1 = jnp.min(jnp.where(l1 == v1, lane, big), axis=1, keepdims=True)
    l2 = jnp.where(jnp.logical_and(in_grp, lane != i1), lg, neg)
    v2 = jnp.max(l2, axis=1, keepdims=True)
    i2 = jnp.min(jnp.where(l2 == v2, lane, big), axis=1, keepdims=True)
    e2 = jnp.exp(v2 - v1)
    p1 = 1.0 / (1.0 + e2)
    p2 = e2 * p1
    return (jnp.where(lane == i1, p1 * w_grp, 0.0) + jnp.where(lane == i2, p2 * w_grp, 0.0))


def _merge_kernel(x_ref, ysb_ref, yhg_ref, gsb_ref, ghg_ref, wbs_ref, wbh_ref, wo_ref,
                  ln_ref, wr_hi_ref, wr_lo_ref, br_ref, x1_ref, t_ref, comb_ref):
    a = jnp.dot(ysb_ref[...], wbs_ref[...], preferred_element_type=F32)
    b = jnp.dot(yhg_ref[...], wbh_ref[...], preferred_element_type=F32)
    merged = jax.nn.sigmoid(gsb_ref[...]) * a + jax.nn.sigmoid(ghg_ref[...]) * b
    x1 = x_ref[...] + jnp.dot(merged.astype(BF16), wo_ref[...], preferred_element_type=F32)
    x1_ref[...] = x1
    var = jnp.mean(x1 * x1, axis=-1, keepdims=True)
    t = x1 * lax.rsqrt(var + EPS) * ln_ref[...]
    t_ref[...] = t.astype(BF16)
    t_hi = t.astype(BF16)
    t_lo = (t - t_hi.astype(F32)).astype(BF16)
    lg = (jnp.dot(t_hi, wr_hi_ref[...], preferred_element_type=F32)
          + jnp.dot(t_lo, wr_hi_ref[...], preferred_element_type=F32)
          + jnp.dot(t_hi, wr_lo_ref[...], preferred_element_type=F32)) + br_ref[...]
    comb_ref[...] = _route(lg)


def _merge(x2, y_sb, y_hg, rest, wbs, wbh, wo, ln_g, wr_hi, wr_lo, br):
    n, d = x2.shape
    tm = min(MERGE_TM, n)
    w_sb, w_hg = y_sb.shape[1], y_hg.shape[1]
    gate_blk = (rest.shape[1] - 2 * d) // d
    row = lambda wdt, j=0: pl.BlockSpec((tm, wdt), lambda i, j=j: (i, j))
    const = lambda shape: pl.BlockSpec(shape, lambda i: (0, 0))
    return pl.pallas_call(
        _merge_kernel,
        grid=(n // tm,),
        in_specs=[row(d), row(w_sb), row(w_hg), row(d, gate_blk), row(d, gate_blk + 1),
                  const(wbs.shape), const(wbh.shape), const(wo.shape), const((1, d)),
                  const(wr_hi.shape), const(wr_lo.shape), const((1, LANES))],
        out_specs=[row(d), row(d), row(LANES)],
        out_shape=[jax.ShapeDtypeStruct((n, d), F32), jax.ShapeDtypeStruct((n, d), BF16),
                   jax.ShapeDtypeStruct((n, LANES), F32)],
        compiler_params=_params(("parallel",)),
        name="merge",
    )(x2, y_sb, y_hg, rest, rest, wbs, wbh, wo, ln_g.reshape(1, d), wr_hi, wr_lo, br)


def _moe_kernel(t_ref, comb_ref, wg_ref, wu_ref, wd_ref, x1_ref, fg_ref, o_ref, acc_scr):
    e = pl.program_id(1)

    @pl.when(e == 0)
    def _():
        acc_scr[...] = jnp.zeros_like(acc_scr)

    t = t_ref[...]
    hg = jnp.dot(t, wg_ref[...], preferred_element_type=F32)
    hu = jnp.dot(t, wu_ref[...], preferred_element_type=F32)
    a = (hg * jax.nn.sigmoid(hg) * hu).astype(BF16)
    y = jnp.dot(a, wd_ref[...], preferred_element_type=F32)
    comb = comb_ref[...]
    lane = lax.broadcasted_iota(jnp.int32, comb.shape, 1)
    ce = jnp.sum(jnp.where(lane == e, comb, 0.0), axis=1, keepdims=True)
    acc_scr[...] += ce * y

    @pl.when(e == pl.num_programs(1) - 1)
    def _():
        x2 = x1_ref[...] + acc_scr[...]
        var = jnp.mean(x2 * x2, axis=-1, keepdims=True)
        o_ref[...] = x2 * lax.rsqrt(var + EPS) * fg_ref[...]


def _moe(t, comb, wg, wu, wd, x1, final_g):
    n, d = t.shape
    ne, _, de = wg.shape
    tm = min(MOE_TM, n)
    return pl.pallas_call(
        _moe_kernel,
        grid=(n // tm, ne),
        in_specs=[
            pl.BlockSpec((tm, d), lambda i, e: (i, 0)),
            pl.BlockSpec((tm, LANES), lambda i, e: (i, 0)),
            pl.BlockSpec((None, d, de), lambda i, e: (e, 0, 0)),
            pl.BlockSpec((None, d, de), lambda i, e: (e, 0, 0)),
            pl.BlockSpec((None, de, d), lambda i, e: (e, 0, 0)),
            pl.BlockSpec((tm, d), lambda i, e: (i, 0)),
            pl.BlockSpec((1, d), lambda i, e: (0, 0)),
        ],
        out_specs=pl.BlockSpec((tm, d), lambda i, e: (i, 0)),
        out_shape=jax.ShapeDtypeStruct((n, d), F32),
        scratch_shapes=[pltpu.VMEM((tm, d), F32)],
        compiler_params=_params(("parallel", "arbitrary")),
        name="moe",
    )(t, comb, wg, wu, wd, x1, final_g.reshape(1, d))


def _suffix_ones(t):
    j = np.arange(t)[:, None]
    s = np.arange(t)[None, :]
    return jnp.asarray((j >= s).astype(np.float32), dtype=BF16)


def _causal_bias(t):
    row = np.arange(t)[:, None]
    col = np.arange(t)[None, :]
    diag = np.where(col < row, 0.0, MASK_BIAS).astype(np.float32)
    return jnp.asarray(np.stack([np.zeros_like(diag), diag]))


def _chunk_prefix_ones(tt, c):
    t = np.arange(tt)[:, None]
    j = np.arange(tt)[None, :]
    return jnp.asarray(((j <= t) & (t // c == j // c)).astype(np.float32), dtype=BF16)


def _block_diag_ones(w, blk):
    a = np.arange(w)
    return jnp.asarray((a[:, None] // blk == a[None, :] // blk).astype(np.float32), dtype=BF16)


def kernel(x, ln1_g, w_in, w_branch_sb, w_branch_hg, hg_norm_g, hg_lb_logits, w_out, ln2_g,
           w_router_group, b_router_group, w_router_expert, b_router_expert,
           w_exp_gate, w_exp_up, w_exp_down, final_g):
    bsz, seq, d = x.shape
    depth = w_in.shape[0]
    n = bsz * seq
    sb_width = SB_HEADS * SB_HEAD_DIM
    hg_width = HG_HEADS * HG_DIM

    tri_attn = _suffix_ones(min(ATTN_T, seq))
    bias_attn = _causal_bias(min(ATTN_T, seq))
    tt = min(HGRN_TT, seq)
    tri_hg = _chunk_prefix_ones(tt, min(HGRN_C, tt))
    bd = _block_diag_ones(min(MXU_DIM, hg_width), HG_DIM)

    x2 = x.reshape(n, d)
    for l in range(depth):
        qkv, rest = _inproj(x2, ln1_g[l], w_in[l].astype(BF16), sb_width)
        y_sb = _attn(qkv, bsz, seq, tri_attn, bias_attn)
        y_hg = _hgrn(rest, hg_lb_logits, hg_norm_g[l], bsz, seq, l, tri_hg, bd)

        pad = LANES - N_EXPERTS - N_GROUPS
        wr = jnp.concatenate([w_router_expert[l], w_router_group[l],
                              jnp.zeros((d, pad), F32)], axis=1)
        wr_hi = wr.astype(BF16)
        wr_lo = (wr - wr_hi.astype(F32)).astype(BF16)
        br = jnp.concatenate([b_router_expert[l], b_router_group[l],
                              jnp.zeros((pad,), F32)]).reshape(1, LANES)

        last = l == depth - 1
        x1, t, comb = _merge(x2, y_sb, y_hg, rest, w_branch_sb[l].astype(BF16),
                             w_branch_hg[l].astype(BF16), w_out[l].astype(BF16), ln2_g[l],
                             wr_hi, wr_lo, br)
        assert last, "final rmsnorm is fused into the last layer's expert kernel"
        x2 = _moe(t, comb, w_exp_gate[l].astype(BF16), w_exp_up[l].astype(BF16),
                  w_exp_down[l].astype(BF16), x1, final_g)
    return x2.reshape(bsz, seq, d)
```

```python
import functools

import jax
import jax.numpy as jnp
import numpy as np
from jax import lax
from jax.experimental import pallas as pl
from jax.experimental.pallas import tpu as pltpu

F32 = jnp.float32
BF16 = jnp.bfloat16

EPS = 1e-6
SB_HEADS = 8
SB_HEAD_DIM = 64
HG_HEADS = 8
HG_DIM = 64
N_GROUPS = 4
EXPERTS_PER_GROUP = 4
N_EXPERTS = N_GROUPS * EXPERTS_PER_GROUP

LANES = 128
SUBLANES = 8
MXU_DIM = 256
LOG2E = 1.4426950408889634
VMEM_LIMIT = 48 * 1024 * 1024

INPROJ_TM = 1024
INPROJ_TN = 512
ATTN_T = MXU_DIM
HGRN_TT = 256
HGRN_C = 32
MERGE_TM = 512
MOE_TM = 1024


def _params(sem):
    return pltpu.CompilerParams(dimension_semantics=sem, vmem_limit_bytes=VMEM_LIMIT)


def _split_dot(x, m, passes):
    acc = None
    r = x
    for p in range(passes):
        h = r.astype(BF16)
        term = jnp.dot(h, m, preferred_element_type=F32)
        acc = term if acc is None else acc + term
        if p + 1 < passes:
            r = r - h.astype(F32)
    return acc


def _inproj_kernel(x_ref, g_ref, w_ref, qkv_ref, rest_ref, h_scr, *, n_qkv_blocks, q_scale):
    j = pl.program_id(1)

    @pl.when(j == 0)
    def _():
        x = x_ref[...]
        var = jnp.mean(x * x, axis=-1, keepdims=True)
        h_scr[...] = (x * lax.rsqrt(var + EPS) * g_ref[...]).astype(BF16)

    acc = jnp.dot(h_scr[...], w_ref[...], preferred_element_type=F32)

    @pl.when(j == 0)
    def _():
        qkv_ref[...] = (acc * q_scale).astype(BF16)

    @pl.when(jnp.logical_and(j > 0, j < n_qkv_blocks))
    def _():
        qkv_ref[...] = acc.astype(BF16)

    @pl.when(j >= n_qkv_blocks)
    def _():
        rest_ref[...] = acc


def _inproj(x2, ln_g, w_in_bf16, sb_width):
    n, d = x2.shape
    cols = w_in_bf16.shape[1]
    tm, tn = min(INPROJ_TM, n), INPROJ_TN
    assert sb_width == tn, "q block must be exactly one column tile"
    nq = 3 * sb_width // tn
    nj = cols // tn
    kern = functools.partial(_inproj_kernel, n_qkv_blocks=nq, q_scale=SB_HEAD_DIM ** -0.5)
    return pl.pallas_call(
        kern,
        grid=(n // tm, nj),
        in_specs=[
            pl.BlockSpec((tm, d), lambda i, j: (i, 0)),
            pl.BlockSpec((1, d), lambda i, j: (0, 0)),
            pl.BlockSpec((d, tn), lambda i, j: (0, j)),
        ],
        out_specs=[
            pl.BlockSpec((tm, tn), lambda i, j: (i, jnp.minimum(j, nq - 1))),
            pl.BlockSpec((tm, tn), lambda i, j: (i, jnp.maximum(j - nq, 0))),
        ],
        out_shape=[
            jax.ShapeDtypeStruct((n, nq * tn), BF16),
            jax.ShapeDtypeStruct((n, cols - nq * tn), F32),
        ],
        scratch_shapes=[pltpu.VMEM((tm, d), BF16)],
        compiler_params=_params(("parallel", "arbitrary")),
        name="inproj",
    )(x2, ln_g.reshape(1, d), w_in_bf16)


ATTN_STAGES = 3
MASK_BIAS = -1e30
ATTN_SKIP_LOG2 = 160.0


def _attn_kernel(q_ref, k_ref, v_ref, tri_ref, bias_ref, o_ref,
                 z0, z1, z2, i0, i1, i2, acc_ref, c_ref, *, t, nq):
    zbuf = (z0, z1, z2)
    ibuf = (i0, i1, i2)
    for r in zbuf + ibuf:
        r[...] = jnp.zeros_like(r)
    acc_ref[...] = jnp.zeros_like(acc_ref)
    c_ref[...] = jnp.zeros_like(c_ref)

    lane = lax.broadcasted_iota(jnp.int32, (t, LANES), 1)
    head0 = lane < SB_HEAD_DIM
    nt = (((1,), (1,)), ((), ()))

    def stage_a(qi, kj, slot):
        q = q_ref[pl.ds(pl.multiple_of(qi * t, t), t), :]
        zero = jnp.zeros_like(q)
        q2 = jnp.concatenate([jnp.where(head0, q, zero), jnp.where(head0, zero, q)], axis=0)
        k = k_ref[pl.ds(pl.multiple_of(kj * t, t), t), :]
        z = lax.dot_general(q2, k, nt, preferred_element_type=F32)
        bias = bias_ref[(qi == kj).astype(jnp.int32)]
        zbuf[slot][...] = z * LOG2E + jnp.concatenate([bias, bias], axis=0)

    def stage_b(slot):
        z = zbuf[slot][...]
        p = jnp.maximum(z, 0.0) + jnp.log(1.0 + jnp.exp2(-jnp.abs(z))) * LOG2E
        incl = jnp.dot(p.astype(BF16), tri_ref[...], preferred_element_type=F32)
        ibuf[slot][...] = incl
        return incl[:, 0:1]

    def stage_c(qi, kj, slot):
        first = qi == kj
        incl = ibuf[slot][...]
        c = jnp.where(first, 0.0, c_ref[...])
        a = jnp.exp2(zbuf[slot][...] - incl - c)
        v = v_ref[pl.ds(pl.multiple_of(kj * t, t), t), :]
        pv = jnp.dot(a.astype(BF16), v, preferred_element_type=F32)
        acc = jnp.where(first, pv, acc_ref[...] + pv)
        acc_ref[...] = acc
        c_new = c + incl[:, 0:1]
        c_ref[...] = c_new
        o_ref[pl.ds(pl.multiple_of(qi * t, t), t), :] = (
            jnp.where(head0, acc[0:t], acc[t:2 * t]).astype(o_ref.dtype))
        return c_new

    def block(carry):
        qs, ks, qn, kn, skip_q, drained = carry
        qs, ks = list(qs), list(ks)
        drained = drained + (qn >= nq).astype(jnp.int32)
        for r in range(ATTN_STAGES):
            sc, sb = (r + 1) % ATTN_STAGES, (r + 2) % ATTN_STAGES
            carry_c = stage_c(qs[sc], ks[sc], sc)
            total_b = stage_b(sb)
            carry_b = jnp.where(qs[sb] == ks[sb], 0.0, carry_c) + total_b
            drain = qn >= nq
            qa = jnp.where(drain, 0, qn)
            ka = jnp.where(drain, 0, kn)
            stage_a(qa, ka, r)
            jump = jnp.logical_or(kn == 0, skip_q == qn)
            kn = jnp.where(drain, kn, jnp.where(jump, qn + 1, kn - 1))
            qn = jnp.where(drain, qn, jnp.where(jump, qn + 1, qn))
            skip_q = jnp.where(jnp.min(carry_b) >= ATTN_SKIP_LOG2, qs[sb], skip_q)
            qs[r], ks[r] = qa, ka
        return tuple(qs), tuple(ks), qn, kn, skip_q, drained

    zero = jnp.int32(0)
    init = ((zero,) * ATTN_STAGES, (zero,) * ATTN_STAGES, zero, zero, jnp.int32(-1), zero)
    lax.while_loop(lambda carry: carry[5] < 1, block, init)


def _attn(qkv, bsz, seq, tri, bias):
    n = bsz * seq
    t = min(ATTN_T, seq)
    pairs = SB_HEADS * SB_HEAD_DIM // LANES
    kern = functools.partial(_attn_kernel, t=t, nq=seq // t)
    return pl.pallas_call(
        kern,
        grid=(bsz, pairs),
        in_specs=[
            pl.BlockSpec((seq, LANES), lambda b, p: (b, p)),
            pl.BlockSpec((seq, LANES), lambda b, p: (b, pairs + p)),
            pl.BlockSpec((seq, LANES), lambda b, p: (b, 2 * pairs + p)),
            pl.BlockSpec((t, t), lambda b, p: (0, 0)),
            pl.BlockSpec((2, t, t), lambda b, p: (0, 0, 0)),
        ],
        out_specs=pl.BlockSpec((seq, LANES), lambda b, p: (b, p)),
        out_shape=jax.ShapeDtypeStruct((n, pairs * LANES), BF16),
        scratch_shapes=([pltpu.VMEM((2 * t, t), F32)] * (2 * ATTN_STAGES)
                        + [pltpu.VMEM((2 * t, LANES), F32), pltpu.VMEM((2 * t, 1), F32)]),
        compiler_params=_params(("parallel", "parallel")),
        name="attn",
    )(qkv, qkv, qkv, tri, bias)


def _group_dot(x, bd):
    g = bd.shape[0]
    parts = [jnp.dot(x[:, i:i + g], bd, preferred_element_type=F32)
             for i in range(0, x.shape[1], g)]
    return jnp.concatenate(parts, axis=1)


def _hgrn_kernel(q_ref, f_ref, i_ref, g_ref, lbl_ref, ng_ref, tri_ref, bd_ref, o_ref,
                 st_scr, wpad, vpad, wsh, vsh, *, tt, c, layer):
    ti = pl.program_id(1)
    w = q_ref.shape[1]
    grp = bd_ref.shape[0]
    nch = tt // c

    @pl.when(ti == 0)
    def _():
        st_scr[...] = jnp.zeros_like(st_scr)

    lg = lbl_ref[...]
    e = jnp.exp(lg - jnp.max(lg, axis=0, keepdims=True))
    lb = jnp.sum(e[0:layer + 1], axis=0, keepdims=True) / jnp.sum(e, axis=0, keepdims=True)

    f = lb + (1.0 - lb) * jax.nn.sigmoid(f_ref[...])
    kk = 1.0 - f
    qv = q_ref[...]
    qs = qv * jax.nn.sigmoid(qv)
    v = i_ref[...]
    bd = bd_ref[...]
    cum = _split_dot_left(tri_ref[...], jnp.log(f) * LOG2E, 3)
    wk = jnp.log(kk) * LOG2E - cum

    pos = lax.broadcasted_iota(jnp.int32, (tt, 1), 0) % c
    wpad[0:SUBLANES, :] = jnp.zeros((SUBLANES, w), F32)
    vpad[0:SUBLANES, :] = jnp.zeros((SUBLANES, w), F32)
    wpad[SUBLANES:SUBLANES + tt, :] = wk
    vpad[SUBLANES:SUBLANES + tt, :] = v
    for b in range(SUBLANES):
        wb = wpad[SUBLANES - b:SUBLANES - b + tt, :]
        vb = vpad[SUBLANES - b:SUBLANES - b + tt, :]
        if b:
            wb = jnp.where(pos >= b, wb, -jnp.inf)
        wsh[b] = wb.reshape(nch, c, w)
        vsh[b] = vb.reshape(nch, c, w)

    qs3 = qs.reshape(nch, c, w)
    cum3 = cum.reshape(nch, c, w)
    acc = None
    for a in reversed(range(c // SUBLANES)):
        rows = c - SUBLANES * a
        qa = qs3[:, SUBLANES * a:, :].reshape(nch * rows, w)
        ca = cum3[:, SUBLANES * a:, :].reshape(nch * rows, w)
        acc_a = None
        for b in range(SUBLANES):
            wb = wsh[b, :, 0:rows, :].reshape(nch * rows, w)
            vb = vsh[b, :, 0:rows, :].reshape(nch * rows, w)
            dd = qa * jnp.exp2(ca + wb)
            term = _group_dot(dd.astype(BF16), bd) * vb
            acc_a = term if acc_a is None else acc_a + term
        acc_a = acc_a.reshape(nch, rows, w)
        if acc is not None:
            acc_a = acc_a + jnp.concatenate([jnp.zeros((nch, SUBLANES, w), F32), acc], axis=1)
        acc = acc_a
    acc = acc.reshape(tt, w)

    bdmask = bd != 0
    outs = []
    for ci in range(nch):
        r0 = ci * c
        cum_c = cum[r0:r0 + c]
        last = cum_c[c - 1:c]
        qd = (qs[r0:r0 + c] * jnp.exp2(cum_c)).astype(BF16)
        kd = (kk[r0:r0 + c] * jnp.exp2(last - cum_c)).astype(BF16)
        vc = v[r0:r0 + c].astype(BF16)
        dec = jnp.exp2(last)
        o_parts = []
        for gi in range(w // grp):
            sl = slice(gi * grp, (gi + 1) * grp)
            st = st_scr[gi]
            o_parts.append(lax.dot_general(qd[:, sl], st.astype(BF16), (((1,), (1,)), ((), ())),
                                           preferred_element_type=F32))
            upd = lax.dot_general(vc[:, sl], kd[:, sl], (((0,), (0,)), ((), ())),
                                  preferred_element_type=F32)
            st_scr[gi] = st * dec[:, sl] + jnp.where(bdmask, upd, 0.0)
        outs.append(jnp.concatenate(o_parts, axis=1))
    o = acc + jnp.concatenate(outs, axis=0)

    o2 = o * o
    o2_hi = o2.astype(BF16)
    o2_lo = (o2 - o2_hi.astype(F32)).astype(BF16)
    ms = (_group_dot(o2_hi, bd) + _group_dot(o2_lo, bd)) * (1.0 / HG_DIM)
    gv = g_ref[...]
    o = o * lax.rsqrt(ms + EPS) * ng_ref[...] * (gv * jax.nn.sigmoid(gv))
    o_ref[...] = o.astype(o_ref.dtype)


def _split_dot_left(m, x, passes):
    acc = None
    r = x
    for p in range(passes):
        h = r.astype(BF16)
        term = jnp.dot(m, h, preferred_element_type=F32)
        acc = term if acc is None else acc + term
        if p + 1 < passes:
            r = r - h.astype(F32)
    return acc


def _hgrn(rest, lb_logits, norm_g, bsz, seq, layer, tri, bd):
    n = bsz * seq
    w = HG_HEADS * HG_DIM
    grp = bd.shape[0]
    tt = min(HGRN_TT, seq)
    c = min(HGRN_C, tt)
    nt = seq // tt
    kern = functools.partial(_hgrn_kernel, tt=tt, c=c, layer=layer)
    col = lambda j: pl.BlockSpec((tt, w), lambda b, i, j=j: (b * nt + i, j))
    const = lambda shape: pl.BlockSpec(shape, lambda b, i: (0, 0))
    return pl.pallas_call(
        kern,
        grid=(bsz, nt),
        in_specs=[col(0), col(1), col(2), col(3),
                  const(lb_logits.shape), const((1, w)), const((tt, tt)), const((grp, grp))],
        out_specs=pl.BlockSpec((tt, w), lambda b, i: (b * nt + i, 0)),
        out_shape=jax.ShapeDtypeStruct((n, w), BF16),
        scratch_shapes=([pltpu.VMEM((w // grp, grp, grp), F32)]
                        + [pltpu.VMEM((SUBLANES + tt, w), F32)] * 2
                        + [pltpu.VMEM((SUBLANES, tt // c, c, w), F32)] * 2),
        compiler_params=_params(("parallel", "arbitrary")),
        name="hgrn",
    )(rest, rest, rest, rest, lb_logits, norm_g.reshape(1, w), tri, bd)


def _route(lg):
    lane = lax.broadcasted_iota(jnp.int32, lg.shape, 1)
    neg = jnp.float32(-jnp.inf)
    big = jnp.int32(LANES)
    gmask = jnp.logical_and(lane >= N_EXPERTS, lane < N_EXPERTS + N_GROUPS)
    gl = jnp.where(gmask, lg, neg)
    gmax = jnp.max(gl, axis=1, keepdims=True)
    gidx = jnp.min(jnp.where(gl == gmax, lane, big), axis=1, keepdims=True) - N_EXPERTS
    w_grp = 1.0 / jnp.sum(jnp.where(gmask, jnp.exp(gl - gmax), 0.0), axis=1, keepdims=True)
    in_grp = jnp.logical_and(lane < N_EXPERTS, lane // EXPERTS_PER_GROUP == gidx)
    l1 = jnp.where(in_grp, lg, neg)
    v1 = jnp.max(l1, axis=1, keepdims=True)
    i1 = jnp.min(jnp.where(l1 == v1, lane, big), axis=1, keepdims=True)
    l2 = jnp.where(jnp.logical_and(in_grp, lane != i1), lg, neg)
    v2 = jnp.max(l2, axis=1, keepdims=True)
    i2 = jnp.min(jnp.where(l2 == v2, lane, big), axis=1, keepdims=True)
    e2 = jnp.exp(v2 - v1)
    p1 = 1.0 / (1.0 + e2)
    p2 = e2 * p1
    return (jnp.where(lane == i1, p1 * w_grp, 0.0) + jnp.where(lane == i2, p2 * w_grp, 0.0))


def _merge_kernel(x_ref, ysb_ref, yhg_ref, gsb_ref, ghg_ref, wbs_ref, wbh_ref, wo_ref,
                  ln_ref, wr_hi_ref, wr_lo_ref, br_ref, x1_ref, t_ref, comb_ref):
    a = jnp.dot(ysb_ref[...], wbs_ref[...], preferred_element_type=F32)
    b = jnp.dot(yhg_ref[...], wbh_ref[...], preferred_element_type=F32)
    merged = jax.nn.sigmoid(gsb_ref[...]) * a + jax.nn.sigmoid(ghg_ref[...]) * b
    x1 = x_ref[...] + jnp.dot(merged.astype(BF16), wo_ref[...], preferred_element_type=F32)
    x1_ref[...] = x1
    var = jnp.mean(x1 * x1, axis=-1, keepdims=True)
    t = x1 * lax.rsqrt(var + EPS) * ln_ref[...]
    t_ref[...] = t.astype(BF16)
    t_hi = t.astype(BF16)
    t_lo = (t - t_hi.astype(F32)).astype(BF16)
    lg = (jnp.dot(t_hi, wr_hi_ref[...], preferred_element_type=F32)
          + jnp.dot(t_lo, wr_hi_ref[...], preferred_element_type=F32)
          + jnp.dot(t_hi, wr_lo_ref[...], preferred_element_type=F32)) + br_ref[...]
    comb_ref[...] = _route(lg)


def _merge(x2, y_sb, y_hg, rest, wbs, wbh, wo, ln_g, wr_hi, wr_lo, br):
    n, d = x2.shape
    tm = min(MERGE_TM, n)
    w_sb, w_hg = y_sb.shape[1], y_hg.shape[1]
    gate_blk = (rest.shape[1] - 2 * d) // d
    row = lambda wdt, j=0: pl.BlockSpec((tm, wdt), lambda i, j=j: (i, j))
    const = lambda shape: pl.BlockSpec(shape, lambda i: (0, 0))
    return pl.pallas_call(
        _merge_kernel,
        grid=(n // tm,),
        in_specs=[row(d), row(w_sb), row(w_hg), row(d, gate_blk), row(d, gate_blk + 1),
                  const(wbs.shape), const(wbh.shape), const(wo.shape), const((1, d)),
                  const(wr_hi.shape), const(wr_lo.shape), const((1, LANES))],
        out_specs=[row(d), row(d), row(LANES)],
        out_shape=[jax.ShapeDtypeStruct((n, d), F32), jax.ShapeDtypeStruct((n, d), BF16),
                   jax.ShapeDtypeStruct((n, LANES), F32)],
        compiler_params=_params(("parallel",)),
        name="merge",
    )(x2, y_sb, y_hg, rest, rest, wbs, wbh, wo, ln_g.reshape(1, d), wr_hi, wr_lo, br)


def _moe_kernel(t_ref, comb_ref, wg_ref, wu_ref, wd_ref, x1_ref, fg_ref, o_ref, acc_scr):
    e = pl.program_id(1)

    @pl.when(e == 0)
    def _():
        acc_scr[...] = jnp.zeros_like(acc_scr)

    t = t_ref[...]
    hg = jnp.dot(t, wg_ref[...], preferred_element_type=F32)
    hu = jnp.dot(t, wu_ref[...], preferred_element_type=F32)
    a = (hg * jax.nn.sigmoid(hg) * hu).astype(BF16)
    y = jnp.dot(a, wd_ref[...], preferred_element_type=F32)
    comb = comb_ref[...]
    lane = lax.broadcasted_iota(jnp.int32, comb.shape, 1)
    ce = jnp.sum(jnp.where(lane == e, comb, 0.0), axis=1, keepdims=True)
    acc_scr[...] += ce * y

    @pl.when(e == pl.num_programs(1) - 1)
    def _():
        x2 = x1_ref[...] + acc_scr[...]
        var = jnp.mean(x2 * x2, axis=-1, keepdims=True)
        o_ref[...] = x2 * lax.rsqrt(var + EPS) * fg_ref[...]


def _moe(t, comb, wg, wu, wd, x1, final_g):
    n, d = t.shape
    ne, _, de = wg.shape
    tm = min(MOE_TM, n)
    return pl.pallas_call(
        _moe_kernel,
        grid=(n // tm, ne),
        in_specs=[
            pl.BlockSpec((tm, d), lambda i, e: (i, 0)),
            pl.BlockSpec((tm, LANES), lambda i, e: (i, 0)),
            pl.BlockSpec((None, d, de), lambda i, e: (e, 0, 0)),
            pl.BlockSpec((None, d, de), lambda i, e: (e, 0, 0)),
            pl.BlockSpec((None, de, d), lambda i, e: (e, 0, 0)),
            pl.BlockSpec((tm, d), lambda i, e: (i, 0)),
            pl.BlockSpec((1, d), lambda i, e: (0, 0)),
        ],
        out_specs=pl.BlockSpec((tm, d), lambda i, e: (i, 0)),
        out_shape=jax.ShapeDtypeStruct((n, d), F32),
        scratch_shapes=[pltpu.VMEM((tm, d), F32)],
        compiler_params=_params(("parallel", "arbitrary")),
        name="moe",
    )(t, comb, wg, wu, wd, x1, final_g.reshape(1, d))


def _suffix_ones(t):
    j = np.arange(t)[:, None]
    s = np.arange(t)[None, :]
    return jnp.asarray((j >= s).astype(np.float32), dtype=BF16)


def _causal_bias(t):
    row = np.arange(t)[:, None]
    col = np.arange(t)[None, :]
    diag = np.where(col < row, 0.0, MASK_BIAS).astype(np.float32)
    return jnp.asarray(np.stack([np.zeros_like(diag), diag]))


def _chunk_prefix_ones(tt, c):
    t = np.arange(tt)[:, None]
    j = np.arange(tt)[None, :]
    return jnp.asarray(((j <= t) & (t // c == j // c)).astype(np.float32), dtype=BF16)


def _block_diag_ones(w, blk):
    a = np.arange(w)
    return jnp.asarray((a[:, None] // blk == a[None, :] // blk).astype(np.float32), dtype=BF16)


def kernel(x, ln1_g, w_in, w_branch_sb, w_branch_hg, hg_norm_g, hg_lb_logits, w_out, ln2_g,
           w_router_group, b_router_group, w_router_expert, b_router_expert,
           w_exp_gate, w_exp_up, w_exp_down, final_g):
    bsz, seq, d = x.shape
    depth = w_in.shape[0]
    n = bsz * seq
    sb_width = SB_HEADS * SB_HEAD_DIM
    hg_width = HG_HEADS * HG_DIM

    tri_attn = _suffix_ones(min(ATTN_T, seq))
    bias_attn = _causal_bias(min(ATTN_T, seq))
    tt = min(HGRN_TT, seq)
    tri_hg = _chunk_prefix_ones(tt, min(HGRN_C, tt))
    bd = _block_diag_ones(min(MXU_DIM, hg_width), HG_DIM)

    x2 = x.reshape(n, d)
    for l in range(depth):
        qkv, rest = _inproj(x2, ln1_g[l], w_in[l].astype(BF16), sb_width)
        y_sb = _attn(qkv, bsz, seq, tri_attn, bias_attn)
        y_hg = _hgrn(rest, hg_lb_logits, hg_norm_g[l], bsz, seq, l, tri_hg, bd)

        pad = LANES - N_EXPERTS - N_GROUPS
        wr = jnp.concatenate([w_router_expert[l], w_router_group[l],
                              jnp.zeros((d, pad), F32)], axis=1)
        wr_hi = wr.astype(BF16)
        wr_lo = (wr - wr_hi.astype(F32)).astype(BF16)
        br = jnp.concatenate([b_router_expert[l], b_router_group[l],
                              jnp.zeros((pad,), F32)]).reshape(1, LANES)

        last = l == depth - 1
        x1, t, comb = _merge(x2, y_sb, y_hg, rest, w_branch_sb[l].astype(BF16),
                             w_branch_hg[l].astype(BF16), w_out[l].astype(BF16), ln2_g[l],
                             wr_hi, wr_lo, br)
        assert last, "final rmsnorm is fused into the last layer's expert kernel"
        x2 = _moe(t, comb, w_exp_gate[l].astype(BF16), w_exp_up[l].astype(BF16),
                  w_exp_down[l].astype(BF16), x1, final_g)
    return x2.reshape(bsz, seq, d)
```

```python
import functools

import jax
import jax.numpy as jnp
import numpy as np
from jax import lax
from jax.experimental import pallas as pl
from jax.experimental.pallas import tpu as pltpu

F32 = jnp.float32
BF16 = jnp.bfloat16

EPS = 1e-6
SB_HEADS = 8
SB_HEAD_DIM = 64
HG_HEADS = 8
HG_DIM = 64
N_GROUPS = 4
EXPERTS_PER_GROUP = 4
N_EXPERTS = N_GROUPS * EXPERTS_PER_GROUP

LANES = 128
SUBLANES = 8
MXU_DIM = 256
LOG2E = 1.4426950408889634
VMEM_LIMIT = 48 * 1024 * 1024

INPROJ_TM = 1024
INPROJ_TN = 512
ATTN_T = MXU_DIM
HGRN_TT = 256
HGRN_C = 32
MERGE_TM = 512
MOE_TM = 512
PLAN_TM = 512
DISPATCH_TM = 512


def _params(sem):
    return pltpu.CompilerParams(dimension_semantics=sem, vmem_limit_bytes=VMEM_LIMIT)


def _split_dot(x, m, passes):
    acc = None
    r = x
    for p in range(passes):
        h = r.astype(BF16)
        term = jnp.dot(h, m, preferred_element_type=F32)
        acc = term if acc is None else acc + term
        if p + 1 < passes:
            r = r - h.astype(F32)
    return acc


def _inproj_kernel(x_ref, g_ref, w_ref, qkv_ref, rest_ref, h_scr, *, n_qkv_blocks, q_scale):
    j = pl.program_id(1)

    @pl.when(j == 0)
    def _():
        x = x_ref[...]
        var = jnp.mean(x * x, axis=-1, keepdims=True)
        h_scr[...] = (x * lax.rsqrt(var + EPS) * g_ref[...]).astype(BF16)

    acc = jnp.dot(h_scr[...], w_ref[...], preferred_element_type=F32)

    @pl.when(j == 0)
    def _():
        qkv_ref[...] = (acc * q_scale).astype(BF16)

    @pl.when(jnp.logical_and(j > 0, j < n_qkv_blocks))
    def _():
        qkv_ref[...] = acc.astype(BF16)

    @pl.when(j >= n_qkv_blocks)
    def _():
        rest_ref[...] = acc


def _inproj(x2, ln_g, w_in_bf16, sb_width):
    n, d = x2.shape
    cols = w_in_bf16.shape[1]
    tm, tn = min(INPROJ_TM, n), INPROJ_TN
    assert sb_width == tn, "q block must be exactly one column tile"
    nq = 3 * sb_width // tn
    nj = cols // tn
    kern = functools.partial(_inproj_kernel, n_qkv_blocks=nq, q_scale=SB_HEAD_DIM ** -0.5)
    return pl.pallas_call(
        kern,
        grid=(n // tm, nj),
        in_specs=[
            pl.BlockSpec((tm, d), lambda i, j: (i, 0)),
            pl.BlockSpec((1, d), lambda i, j: (0, 0)),
            pl.BlockSpec((d, tn), lambda i, j: (0, j)),
        ],
        out_specs=[
            pl.BlockSpec((tm, tn), lambda i, j: (i, jnp.minimum(j, nq - 1))),
            pl.BlockSpec((tm, tn), lambda i, j: (i, jnp.maximum(j - nq, 0))),
        ],
        out_shape=[
            jax.ShapeDtypeStruct((n, nq * tn), BF16),
            jax.ShapeDtypeStruct((n, cols - nq * tn), F32),
        ],
        scratch_shapes=[pltpu.VMEM((tm, d), BF16)],
        compiler_params=_params(("parallel", "arbitrary")),
        name="inproj",
    )(x2, ln_g.reshape(1, d), w_in_bf16)


ATTN_STAGES = 3
MASK_BIAS = -1e30
ATTN_SKIP_LOG2 = 160.0


def _attn_kernel(q_ref, k_ref, v_ref, tri_ref, bias_ref, o_ref,
                 z0, z1, z2, i0, i1, i2, acc_ref, c_ref, *, t, nq):
    zbuf = (z0, z1, z2)
    ibuf = (i0, i1, i2)
    for r in zbuf + ibuf:
        r[...] = jnp.zeros_like(r)
    acc_ref[...] = jnp.zeros_like(acc_ref)
    c_ref[...] = jnp.zeros_like(c_ref)

    lane = lax.broadcasted_iota(jnp.int32, (t, LANES), 1)
    head0 = lane < SB_HEAD_DIM
    nt = (((1,), (1,)), ((), ()))

    def stage_a(qi, kj, slot):
        q = q_ref[pl.ds(pl.multiple_of(qi * t, t), t), :]
        zero = jnp.zeros_like(q)
        q2 = jnp.concatenate([jnp.where(head0, q, zero), jnp.where(head0, zero, q)], axis=0)
        k = k_ref[pl.ds(pl.multiple_of(kj * t, t), t), :]
        z = lax.dot_general(q2, k, nt, preferred_element_type=F32)
        bias = bias_ref[(qi == kj).astype(jnp.int32)]
        zbuf[slot][...] = z * LOG2E + jnp.concatenate([bias, bias], axis=0)

    def stage_b(slot):
        z = zbuf[slot][...]
        p = jnp.maximum(z, 0.0) + jnp.log(1.0 + jnp.exp2(-jnp.abs(z))) * LOG2E
        incl = jnp.dot(p.astype(BF16), tri_ref[...], preferred_element_type=F32)
        ibuf[slot][...] = incl
        return incl[:, 0:1]

    def stage_c(qi, kj, slot):
        first = qi == kj
        incl = ibuf[slot][...]
        c = jnp.where(first, 0.0, c_ref[...])
        a = jnp.exp2(zbuf[slot][...] - incl - c)
        v = v_ref[pl.ds(pl.multiple_of(kj * t, t), t), :]
        pv = jnp.dot(a.astype(BF16), v, preferred_element_type=F32)
        acc = jnp.where(first, pv, acc_ref[...] + pv)
        acc_ref[...] = acc
        c_new = c + incl[:, 0:1]
        c_ref[...] = c_new
        o_ref[pl.ds(pl.multiple_of(qi * t, t), t), :] = (
            jnp.where(head0, acc[0:t], acc[t:2 * t]).astype(o_ref.dtype))
        return c_new

    def block(carry):
        qs, ks, qn, kn, skip_q, drained = carry
        qs, ks = list(qs), list(ks)
        drained = drained + (qn >= nq).astype(jnp.int32)
        for r in range(ATTN_STAGES):
            sc, sb = (r + 1) % ATTN_STAGES, (r + 2) % ATTN_STAGES
            carry_c = stage_c(qs[sc], ks[sc], sc)
            total_b = stage_b(sb)
            carry_b = jnp.where(qs[sb] == ks[sb], 0.0, carry_c) + total_b
            drain = qn >= nq
            qa = jnp.where(drain, 0, qn)
            ka = jnp.where(drain, 0, kn)
            stage_a(qa, ka, r)
            jump = jnp.logical_or(kn == 0, skip_q == qn)
            kn = jnp.where(drain, kn, jnp.where(jump, qn + 1, kn - 1))
            qn = jnp.where(drain, qn, jnp.where(jump, qn + 1, qn))
            skip_q = jnp.where(jnp.min(carry_b) >= ATTN_SKIP_LOG2, qs[sb], skip_q)
            qs[r], ks[r] = qa, ka
        return tuple(qs), tuple(ks), qn, kn, skip_q, drained

    zero = jnp.int32(0)
    init = ((zero,) * ATTN_STAGES, (zero,) * ATTN_STAGES, zero, zero, jnp.int32(-1), zero)
    lax.while_loop(lambda carry: carry[5] < 1, block, init)


def _attn(qkv, bsz, seq, tri, bias):
    n = bsz * seq
    t = min(ATTN_T, seq)
    pairs = SB_HEADS * SB_HEAD_DIM // LANES
    kern = functools.partial(_attn_kernel, t=t, nq=seq // t)
    return pl.pallas_call(
        kern,
        grid=(bsz, pairs),
        in_specs=[
            pl.BlockSpec((seq, LANES), lambda b, p: (b, p)),
            pl.BlockSpec((seq, LANES), lambda b, p: (b, pairs + p)),
            pl.BlockSpec((seq, LANES), lambda b, p: (b, 2 * pairs + p)),
            pl.BlockSpec((t, t), lambda b, p: (0, 0)),
            pl.BlockSpec((2, t, t), lambda b, p: (0, 0, 0)),
        ],
        out_specs=pl.BlockSpec((seq, LANES), lambda b, p: (b, p)),
        out_shape=jax.ShapeDtypeStruct((n, pairs * LANES), BF16),
        scratch_shapes=([pltpu.VMEM((2 * t, t), F32)] * (2 * ATTN_STAGES)
                        + [pltpu.VMEM((2 * t, LANES), F32), pltpu.VMEM((2 * t, 1), F32)]),
        compiler_params=_params(("parallel", "parallel")),
        name="attn",
    )(qkv, qkv, qkv, tri, bias)


def _group_dot(x, bd):
    g = bd.shape[0]
    parts = [jnp.dot(x[:, i:i + g], bd, preferred_element_type=F32)
             for i in range(0, x.shape[1], g)]
    return jnp.concatenate(parts, axis=1)


def _hgrn_kernel(q_ref, f_ref, i_ref, g_ref, lbl_ref, ng_ref, tri_ref, bd_ref, o_ref,
                 st_scr, wpad, vpad, wsh, vsh, *, tt, c, layer):
    ti = pl.program_id(1)
    w = q_ref.shape[1]
    grp = bd_ref.shape[0]
    nch = tt // c

    @pl.when(ti == 0)
    def _():
        st_scr[...] = jnp.zeros_like(st_scr)

    lg = lbl_ref[...]
    e = jnp.exp(lg - jnp.max(lg, axis=0, keepdims=True))
    lb = jnp.sum(e[0:layer + 1], axis=0, keepdims=True) / jnp.sum(e, axis=0, keepdims=True)

    f = lb + (1.0 - lb) * jax.nn.sigmoid(f_ref[...])
    kk = 1.0 - f
    qv = q_ref[...]
    qs = qv * jax.nn.sigmoid(qv)
    v = i_ref[...]
    bd = bd_ref[...]
    cum = _split_dot_left(tri_ref[...], jnp.log(f) * LOG2E, 3)
    wk = jnp.log(kk) * LOG2E - cum

    pos = lax.broadcasted_iota(jnp.int32, (tt, 1), 0) % c
    wpad[0:SUBLANES, :] = jnp.zeros((SUBLANES, w), F32)
    vpad[0:SUBLANES, :] = jnp.zeros((SUBLANES, w), F32)
    wpad[SUBLANES:SUBLANES + tt, :] = wk
    vpad[SUBLANES:SUBLANES + tt, :] = v
    for b in range(SUBLANES):
        wb = wpad[SUBLANES - b:SUBLANES - b + tt, :]
        vb = vpad[SUBLANES - b:SUBLANES - b + tt, :]
        if b:
            wb = jnp.where(pos >= b, wb, -jnp.inf)
        wsh[b] = wb.reshape(nch, c, w)
        vsh[b] = vb.reshape(nch, c, w)

    qs3 = qs.reshape(nch, c, w)
    cum3 = cum.reshape(nch, c, w)
    acc = None
    for a in reversed(range(c // SUBLANES)):
        rows = c - SUBLANES * a
        qa = qs3[:, SUBLANES * a:, :].reshape(nch * rows, w)
        ca = cum3[:, SUBLANES * a:, :].reshape(nch * rows, w)
        acc_a = None
        for b in range(SUBLANES):
            wb = wsh[b, :, 0:rows, :].reshape(nch * rows, w)
            vb = vsh[b, :, 0:rows, :].reshape(nch * rows, w)
            dd = qa * jnp.exp2(ca + wb)
            term = _group_dot(dd.astype(BF16), bd) * vb
            acc_a = term if acc_a is None else acc_a + term
        acc_a = acc_a.reshape(nch, rows, w)
        if acc is not None:
            acc_a = acc_a + jnp.concatenate([jnp.zeros((nch, SUBLANES, w), F32), acc], axis=1)
        acc = acc_a
    acc = acc.reshape(tt, w)

    bdmask = bd != 0
    outs = []
    for ci in range(nch):
        r0 = ci * c
        cum_c = cum[r0:r0 + c]
        last = cum_c[c - 1:c]
        qd = (qs[r0:r0 + c] * jnp.exp2(cum_c)).astype(BF16)
        kd = (kk[r0:r0 + c] * jnp.exp2(last - cum_c)).astype(BF16)
        vc = v[r0:r0 + c].astype(BF16)
        dec = jnp.exp2(last)
        o_parts = []
        for gi in range(w // grp):
            sl = slice(gi * grp, (gi + 1) * grp)
            st = st_scr[gi]
            o_parts.append(lax.dot_general(qd[:, sl], st.astype(BF16), (((1,), (1,)), ((), ())),
                                           preferred_element_type=F32))
            upd = lax.dot_general(vc[:, sl], kd[:, sl], (((0,), (0,)), ((), ())),
                                  preferred_element_type=F32)
            st_scr[gi] = st * dec[:, sl] + jnp.where(bdmask, upd, 0.0)
        outs.append(jnp.concatenate(o_parts, axis=1))
    o = acc + jnp.concatenate(outs, axis=0)

    o2 = o * o
    o2_hi = o2.astype(BF16)
    o2_lo = (o2 - o2_hi.astype(F32)).astype(BF16)
    ms = (_group_dot(o2_hi, bd) + _group_dot(o2_lo, bd)) * (1.0 / HG_DIM)
    gv = g_ref[...]
    o = o * lax.rsqrt(ms + EPS) * ng_ref[...] * (gv * jax.nn.sigmoid(gv))
    o_ref[...] = o.astype(o_ref.dtype)


def _split_dot_left(m, x, passes):
    acc = None
    r = x
    for p in range(passes):
        h = r.astype(BF16)
        term = jnp.dot(m, h, preferred_element_type=F32)
        acc = term if acc is None else acc + term
        if p + 1 < passes:
            r = r - h.astype(F32)
    return acc


def _hgrn(rest, lb_logits, norm_g, bsz, seq, layer, tri, bd):
    n = bsz * seq
    w = HG_HEADS * HG_DIM
    grp = bd.shape[0]
    tt = min(HGRN_TT, seq)
    c = min(HGRN_C, tt)
    nt = seq // tt
    kern = functools.partial(_hgrn_kernel, tt=tt, c=c, layer=layer)
    col = lambda j: pl.BlockSpec((tt, w), lambda b, i, j=j: (b * nt + i, j))
    const = lambda shape: pl.BlockSpec(shape, lambda b, i: (0, 0))
    return pl.pallas_call(
        kern,
        grid=(bsz, nt),
        in_specs=[col(0), col(1), col(2), col(3),
                  const(lb_logits.shape), const((1, w)), const((tt, tt)), const((grp, grp))],
        out_specs=pl.BlockSpec((tt, w), lambda b, i: (b * nt + i, 0)),
        out_shape=jax.ShapeDtypeStruct((n, w), BF16),
        scratch_shapes=([pltpu.VMEM((w // grp, grp, grp), F32)]
                        + [pltpu.VMEM((SUBLANES + tt, w), F32)] * 2
                        + [pltpu.VMEM((SUBLANES, tt // c, c, w), F32)] * 2),
        compiler_params=_params(("parallel", "arbitrary")),
        name="hgrn",
    )(rest, rest, rest, rest, lb_logits, norm_g.reshape(1, w), tri, bd)


ROUTE_I1, ROUTE_I2, ROUTE_W1, ROUTE_W2 = N_EXPERTS, N_EXPERTS + 1, N_EXPERTS + 2, N_EXPERTS + 3


def _route(lg):
    lane = lax.broadcasted_iota(jnp.int32, lg.shape, 1)
    neg = jnp.float32(-jnp.inf)
    big = jnp.int32(LANES)
    gmask = jnp.logical_and(lane >= N_EXPERTS, lane < N_EXPERTS + N_GROUPS)
    gl = jnp.where(gmask, lg, neg)
    gmax = jnp.max(gl, axis=1, keepdims=True)
    gidx = jnp.min(jnp.where(gl == gmax, lane, big), axis=1, keepdims=True) - N_EXPERTS
    w_grp = 1.0 / jnp.sum(jnp.where(gmask, jnp.exp(gl - gmax), 0.0), axis=1, keepdims=True)
    in_grp = jnp.logical_and(lane < N_EXPERTS, lane // EXPERTS_PER_GROUP == gidx)
    l1 = jnp.where(in_grp, lg, neg)
    v1 = jnp.max(l1, axis=1, keepdims=True)
    i1 = jnp.min(jnp.where(l1 == v1, lane, big), axis=1, keepdims=True)
    l2 = jnp.where(jnp.logical_and(in_grp, lane != i1), lg, neg)
    v2 = jnp.max(l2, axis=1, keepdims=True)
    i2 = jnp.min(jnp.where(l2 == v2, lane, big), axis=1, keepdims=True)
    e2 = jnp.exp(v2 - v1)
    p1 = 1.0 / (1.0 + e2)
    p2 = e2 * p1
    out = jnp.where(jnp.logical_or(lane == i1, lane == i2), 1.0, 0.0)
    out = jnp.where(lane == ROUTE_I1, i1.astype(F32), out)
    out = jnp.where(lane == ROUTE_I2, i2.astype(F32), out)
    out = jnp.where(lane == ROUTE_W1, p1 * w_grp, out)
    return jnp.where(lane == ROUTE_W2, p2 * w_grp, out)


def _merge_kernel(x_ref, ysb_ref, yhg_ref, gsb_ref, ghg_ref, wbs_ref, wbh_ref, wo_ref,
                  ln_ref, wr_hi_ref, wr_lo_ref, br_ref, x1_ref, t_ref, route_ref):
    a = jnp.dot(ysb_ref[...], wbs_ref[...], preferred_element_type=F32)
    b = jnp.dot(yhg_ref[...], wbh_ref[...], preferred_element_type=F32)
    merged = jax.nn.sigmoid(gsb_ref[...]) * a + jax.nn.sigmoid(ghg_ref[...]) * b
    x1 = x_ref[...] + jnp.dot(merged.astype(BF16), wo_ref[...], preferred_element_type=F32)
    x1_ref[...] = x1
    var = jnp.mean(x1 * x1, axis=-1, keepdims=True)
    t = x1 * lax.rsqrt(var + EPS) * ln_ref[...]
    t_ref[...] = t
    t_hi = t.astype(BF16)
    t_lo = (t - t_hi.astype(F32)).astype(BF16)
    lg = (jnp.dot(t_hi, wr_hi_ref[...], preferred_element_type=F32)
          + jnp.dot(t_lo, wr_hi_ref[...], preferred_element_type=F32)
          + jnp.dot(t_hi, wr_lo_ref[...], preferred_element_type=F32)) + br_ref[...]
    route_ref[...] = _route(lg)


def _merge(x2, y_sb, y_hg, rest, wbs, wbh, wo, ln_g, wr_hi, wr_lo, br):
    n, d = x2.shape
    tm = min(MERGE_TM, n)
    w_sb, w_hg = y_sb.shape[1], y_hg.shape[1]
    gate_blk = (rest.shape[1] - 2 * d) // d
    row = lambda wdt, j=0: pl.BlockSpec((tm, wdt), lambda i, j=j: (i, j))
    const = lambda shape: pl.BlockSpec(shape, lambda i: (0, 0))
    return pl.pallas_call(
        _merge_kernel,
        grid=(n // tm,),
        in_specs=[row(d), row(w_sb), row(w_hg), row(d, gate_blk), row(d, gate_blk + 1),
                  const(wbs.shape), const(wbh.shape), const(wo.shape), const((1, d)),
                  const(wr_hi.shape), const(wr_lo.shape), const((1, LANES))],
        out_specs=[row(d), row(d), row(LANES)],
        out_shape=[jax.ShapeDtypeStruct((n, d), F32), jax.ShapeDtypeStruct((n, d), F32),
                   jax.ShapeDtypeStruct((n, LANES), F32)],
        compiler_params=_params(("parallel",)),
        name="merge",
    )(x2, y_sb, y_hg, rest, rest, wbs, wbh, wo, ln_g.reshape(1, d), wr_hi, wr_lo, br)


def _plan_kernel(route_ref, ltri_ref, utri_ref, pos_ref, te_ref, meta_ref, tot_scr, run_scr,
                 *, tile_rows):
    ph = pl.program_id(0)
    i = pl.program_id(1)
    route = route_ref[...]
    tm = route.shape[0]
    lane = lax.broadcasted_iota(jnp.int32, (tm, LANES), 1)
    sel = jnp.where(lane < N_EXPERTS, route, 0.0)

    @pl.when(jnp.logical_and(ph == 0, i == 0))
    def _():
        tot_scr[...] = jnp.zeros_like(tot_scr)

    @pl.when(ph == 0)
    def _():
        tot_scr[...] += jnp.sum(sel, axis=0, keepdims=True)

    @pl.when(ph == 1)
    def _():
        tot = tot_scr[...]
        padded = jnp.ceil(tot * (1.0 / tile_rows)) * tile_rows
        off = _split_dot(jnp.broadcast_to(padded, (SUBLANES, LANES)), utri_ref[...], 3)[0:1]

        @pl.when(i == 0)
        def _():
            run_scr[...] = jnp.zeros_like(run_scr)
            lane1 = lax.broadcasted_iota(jnp.int32, (1, LANES), 1)
            is_e = lane1 < N_EXPERTS
            end = off + padded
            n_valid = (jnp.sum(jnp.where(is_e, padded, 0.0), axis=1, keepdims=True)
                       * (1.0 / tile_rows))
            rows = te_ref.shape[0]
            j = lax.broadcasted_iota(jnp.int32, (rows, LANES), 0).astype(F32)
            start = jnp.minimum(j, n_valid - 1.0) * tile_rows
            lane2 = lax.broadcasted_iota(jnp.int32, (rows, LANES), 1)
            done = jnp.where(jnp.logical_and(lane2 < N_EXPERTS, end <= start), 1.0, 0.0)
            te = jnp.sum(done, axis=1, keepdims=True)
            te_ref[...] = jnp.broadcast_to(te, (rows, LANES)).astype(jnp.int32)
            last_tile = jnp.where(jnp.logical_and(is_e, padded > 0),
                                  end * (1.0 / tile_rows) - 1.0, -1.0)
            meta = jnp.where(lane1 == N_EXPERTS, n_valid, last_tile)
            meta_ref[...] = jnp.broadcast_to(meta, meta_ref.shape).astype(jnp.int32)

        rank = jnp.dot(ltri_ref[...], sel.astype(BF16), preferred_element_type=F32) + run_scr[...]
        dest = off + rank
        lane_f = lane.astype(F32)
        p1 = jnp.sum(jnp.where(lane_f == route[:, ROUTE_I1:ROUTE_I1 + 1], dest, 0.0),
                     axis=1, keepdims=True)
        p2 = jnp.sum(jnp.where(lane_f == route[:, ROUTE_I2:ROUTE_I2 + 1], dest, 0.0),
                     axis=1, keepdims=True)
        pos_ref[...] = jnp.where(lane == 0, p1, jnp.where(lane == 1, p2, 0.0)).astype(jnp.int32)
        run_scr[...] += jnp.sum(sel, axis=0, keepdims=True)


def _plan(route, n_tiles_max):
    n = route.shape[0]
    tm = min(PLAN_TM, n)
    r = np.arange(tm)
    ltri = jnp.asarray((r[None, :] < r[:, None]).astype(np.float32), dtype=BF16)
    e = np.arange(LANES)
    utri = jnp.asarray((e[:, None] < e[None, :]).astype(np.float32), dtype=BF16)
    te_rows = -(-n_tiles_max // SUBLANES) * SUBLANES
    kern = functools.partial(_plan_kernel, tile_rows=MOE_TM)
    return pl.pallas_call(
        kern,
        grid=(2, n // tm),
        in_specs=[pl.BlockSpec((tm, LANES), lambda p, i: (i, 0)),
                  pl.BlockSpec((tm, tm), lambda p, i: (0, 0)),
                  pl.BlockSpec((LANES, LANES), lambda p, i: (0, 0))],
        out_specs=[pl.BlockSpec((tm, LANES), lambda p, i: (i * p, 0)),
                   pl.BlockSpec((te_rows, LANES), lambda p, i: (0, 0)),
                   pl.BlockSpec((SUBLANES, LANES), lambda p, i: (0, 0))],
        out_shape=[jax.ShapeDtypeStruct((n, LANES), jnp.int32),
                   jax.ShapeDtypeStruct((te_rows, LANES), jnp.int32),
                   jax.ShapeDtypeStruct((SUBLANES, LANES), jnp.int32)],
        scratch_shapes=[pltpu.VMEM((1, LANES), F32), pltpu.VMEM((1, LANES), F32)],
        compiler_params=_params(("arbitrary", "arbitrary")),
        name="plan",
    )(route, ltri, utri)


def _row_copy(src_ref, r, dst_ref, p, sem):
    return pltpu.make_async_copy(src_ref.at[pl.ds(r, 1)], dst_ref.at[pl.ds(p, 1)], sem)


def _dispatch_kernel(pos1_ref, pos2_ref, meta_ref, t_ref, xs_ref, zero_scr, sem, *, tm, tile_rows):
    i = pl.program_id(0)
    n_tiles = xs_ref.shape[0] // tile_rows

    @pl.when(i == 0)
    def _():
        zero_scr[...] = jnp.zeros_like(zero_scr)
        n_valid = meta_ref[N_EXPERTS]
        clears = [(meta_ref[e], meta_ref[e] >= 0) for e in range(N_EXPERTS)]
        clears += [(n_tiles - 1 - k, n_tiles - 1 - k >= n_valid) for k in range(N_EXPERTS)]

        def clear(tile):
            return pltpu.make_async_copy(
                zero_scr, xs_ref.at[pl.ds(tile * tile_rows, tile_rows)], sem)

        for tile, cond in clears:
            @pl.when(cond)
            def _():
                clear(tile).start()
        for tile, cond in clears:
            @pl.when(cond)
            def _():
                clear(tile).wait()

    base = i * tm

    def start(r, carry):
        _row_copy(t_ref, r, xs_ref, pos1_ref[base + r], sem).start()
        _row_copy(t_ref, r, xs_ref, pos2_ref[base + r], sem).start()
        return carry

    def wait(r, carry):
        _row_copy(t_ref, r, xs_ref, pos1_ref[base + r], sem).wait()
        _row_copy(t_ref, r, xs_ref, pos2_ref[base + r], sem).wait()
        return carry

    lax.fori_loop(0, tm, start, 0)
    lax.fori_loop(0, tm, wait, 0)


def _dispatch(pos1, pos2, meta, t, n_rows):
    n, d = t.shape
    tm = min(DISPATCH_TM, n)
    kern = functools.partial(_dispatch_kernel, tm=tm, tile_rows=MOE_TM)
    grid_spec = pltpu.PrefetchScalarGridSpec(
        num_scalar_prefetch=3,
        grid=(n // tm,),
        in_specs=[pl.BlockSpec((tm, d), lambda i, *_: (i, 0))],
        out_specs=pl.BlockSpec(memory_space=pl.ANY),
        scratch_shapes=[pltpu.VMEM((MOE_TM, d), F32), pltpu.SemaphoreType.DMA(())],
    )
    return pl.pallas_call(
        kern,
        grid_spec=grid_spec,
        out_shape=jax.ShapeDtypeStruct((n_rows, d), F32),
        compiler_params=_params(("arbitrary",)),
        name="dispatch",
    )(pos1, pos2, meta, t)


def _expert_kernel(te_ref, nv_ref, xs_ref, wg_ref, wu_ref, wd_ref, ys_ref):
    in_use = pl.program_id(0) < nv_ref[0]

    @pl.when(in_use)
    def _():
        x = xs_ref[...].astype(BF16)
        hg = jnp.dot(x, wg_ref[...], preferred_element_type=F32)
        hu = jnp.dot(x, wu_ref[...], preferred_element_type=F32)
        a = (hg * jax.nn.sigmoid(hg) * hu).astype(BF16)
        ys_ref[...] = jnp.dot(a, wd_ref[...], preferred_element_type=F32)

    @pl.when(jnp.logical_not(in_use))
    def _():
        ys_ref[...] = jnp.zeros_like(ys_ref)


def _experts(te, nv, xs, wg, wu, wd):
    n_rows, d = xs.shape
    _, _, de = wg.shape
    tile = lambda j, te, nv: (j, 0)
    grid_spec = pltpu.PrefetchScalarGridSpec(
        num_scalar_prefetch=2,
        grid=(n_rows // MOE_TM,),
        in_specs=[pl.BlockSpec((MOE_TM, d), tile),
                  pl.BlockSpec((None, d, de), lambda j, te, nv: (te[j], 0, 0)),
                  pl.BlockSpec((None, d, de), lambda j, te, nv: (te[j], 0, 0)),
                  pl.BlockSpec((None, de, d), lambda j, te, nv: (te[j], 0, 0))],
        out_specs=pl.BlockSpec((MOE_TM, d), tile),
    )
    return pl.pallas_call(
        _expert_kernel,
        grid_spec=grid_spec,
        out_shape=jax.ShapeDtypeStruct((n_rows, d), F32),
        compiler_params=_params(("arbitrary",)),
        name="experts",
    )(te, nv, xs, wg, wu, wd)


def _combine_kernel(pos1_ref, pos2_ref, ys_ref, x1_ref, route_ref, fg_ref, o_ref, g1, g2, sem,
                    *, tm):
    base = pl.program_id(0) * tm

    def start(r, carry):
        _row_copy(ys_ref, pos1_ref[base + r], g1, r, sem).start()
        _row_copy(ys_ref, pos2_ref[base + r], g2, r, sem).start()
        return carry

    def wait(r, carry):
        _row_copy(ys_ref, pos1_ref[base + r], g1, r, sem).wait()
        _row_copy(ys_ref, pos2_ref[base + r], g2, r, sem).wait()
        return carry

    lax.fori_loop(0, tm, start, 0)
    lax.fori_loop(0, tm, wait, 0)
    route = route_ref[...]
    w1 = route[:, ROUTE_W1:ROUTE_W1 + 1]
    w2 = route[:, ROUTE_W2:ROUTE_W2 + 1]
    x2 = x1_ref[...] + (w1 * g1[...] + w2 * g2[...])
    var = jnp.mean(x2 * x2, axis=-1, keepdims=True)
    o_ref[...] = x2 * lax.rsqrt(var + EPS) * fg_ref[...]


def _combine(pos1, pos2, ys, x1, route, final_g):
    n, d = x1.shape
    tm = min(DISPATCH_TM, n)
    kern = functools.partial(_combine_kernel, tm=tm)
    grid_spec = pltpu.PrefetchScalarGridSpec(
        num_scalar_prefetch=2,
        grid=(n // tm,),
        in_specs=[pl.BlockSpec(memory_space=pl.ANY),
                  pl.BlockSpec((tm, d), lambda i, *_: (i, 0)),
                  pl.BlockSpec((tm, LANES), lambda i, *_: (i, 0)),
                  pl.BlockSpec((1, d), lambda i, *_: (0, 0))],
        out_specs=pl.BlockSpec((tm, d), lambda i, *_: (i, 0)),
        scratch_shapes=[pltpu.VMEM((tm, d), F32), pltpu.VMEM((tm, d), F32),
                        pltpu.SemaphoreType.DMA(())],
    )
    return pl.pallas_call(
        kern,
        grid_spec=grid_spec,
        out_shape=jax.ShapeDtypeStruct((n, d), F32),
        compiler_params=_params(("arbitrary",)),
        name="combine",
    )(pos1, pos2, ys, x1, route, final_g.reshape(1, d))


def _moe_sparse(t, route, wg, wu, wd, x1, final_g):
    n = t.shape[0]
    n_tiles_max = 2 * n // MOE_TM + N_EXPERTS
    pos, te, meta = _plan(route, n_tiles_max)
    pos1, pos2 = pos[:, 0], pos[:, 1]
    xs = _dispatch(pos1, pos2, meta[0, :N_EXPERTS + 1], t, n_tiles_max * MOE_TM)
    ys = _experts(te[:n_tiles_max, 0], meta[0, N_EXPERTS:N_EXPERTS + 1], xs, wg, wu, wd)
    return _combine(pos1, pos2, ys, x1, route, final_g)


def _suffix_ones(t):
    j = np.arange(t)[:, None]
    s = np.arange(t)[None, :]
    return jnp.asarray((j >= s).astype(np.float32), dtype=BF16)


def _causal_bias(t):
    row = np.arange(t)[:, None]
    col = np.arange(t)[None, :]
    diag = np.where(col < row, 0.0, MASK_BIAS).astype(np.float32)
    return jnp.asarray(np.stack([np.zeros_like(diag), diag]))


def _chunk_prefix_ones(tt, c):
    t = np.arange(tt)[:, None]
    j = np.arange(tt)[None, :]
    return jnp.asarray(((j <= t) & (t // c == j // c)).astype(np.float32), dtype=BF16)


def _block_diag_ones(w, blk):
    a = np.arange(w)
    return jnp.asarray((a[:, None] // blk == a[None, :] // blk).astype(np.float32), dtype=BF16)


def kernel(x, ln1_g, w_in, w_branch_sb, w_branch_hg, hg_norm_g, hg_lb_logits, w_out, ln2_g,
           w_router_group, b_router_group, w_router_expert, b_router_expert,
           w_exp_gate, w_exp_up, w_exp_down, final_g):
    bsz, seq, d = x.shape
    depth = w_in.shape[0]
    n = bsz * seq
    sb_width = SB_HEADS * SB_HEAD_DIM
    hg_width = HG_HEADS * HG_DIM

    tri_attn = _suffix_ones(min(ATTN_T, seq))
    bias_attn = _causal_bias(min(ATTN_T, seq))
    tt = min(HGRN_TT, seq)
    tri_hg = _chunk_prefix_ones(tt, min(HGRN_C, tt))
    bd = _block_diag_ones(min(MXU_DIM, hg_width), HG_DIM)

    x2 = x.reshape(n, d)
    for l in range(depth):
        qkv, rest = _inproj(x2, ln1_g[l], w_in[l].astype(BF16), sb_width)
        y_sb = _attn(qkv, bsz, seq, tri_attn, bias_attn)
        y_hg = _hgrn(rest, hg_lb_logits, hg_norm_g[l], bsz, seq, l, tri_hg, bd)

        pad = LANES - N_EXPERTS - N_GROUPS
        wr = jnp.concatenate([w_router_expert[l], w_router_group[l],
                              jnp.zeros((d, pad), F32)], axis=1)
        wr_hi = wr.astype(BF16)
        wr_lo = (wr - wr_hi.astype(F32)).astype(BF16)
        br = jnp.concatenate([b_router_expert[l], b_router_group[l],
                              jnp.zeros((pad,), F32)]).reshape(1, LANES)

        last = l == depth - 1
        x1, t, route = _merge(x2, y_sb, y_hg, rest, w_branch_sb[l].astype(BF16),
                              w_branch_hg[l].astype(BF16), w_out[l].astype(BF16), ln2_g[l],
                              wr_hi, wr_lo, br)
        assert last, "final rmsnorm is fused into the last layer's combine kernel"
        x2 = _moe_sparse(t, route, w_exp_gate[l].astype(BF16), w_exp_up[l].astype(BF16),
                         w_exp_down[l].astype(BF16), x1, final_g)
    return x2.reshape(bsz, seq, d)
```

```python
import functools

import jax
import jax.numpy as jnp
import numpy as np
from jax import lax
from jax.experimental import pallas as pl
from jax.experimental.pallas import tpu as pltpu

F32 = jnp.float32
BF16 = jnp.bfloat16

EPS = 1e-6
SB_HEADS = 8
SB_HEAD_DIM = 64
HG_HEADS = 8
HG_DIM = 64
N_GROUPS = 4
EXPERTS_PER_GROUP = 4
N_EXPERTS = N_GROUPS * EXPERTS_PER_GROUP

LANES = 128
SUBLANES = 8
MXU_DIM = 256
LOG2E = 1.4426950408889634
VMEM_LIMIT = 48 * 1024 * 1024

INPROJ_TM = 1024
INPROJ_TN = 512
ATTN_T = MXU_DIM
HGRN_TT = 256
HGRN_C = 32
MERGE_TM = 512
MOE_TM = 512
PLAN_TM = 512
DISPATCH_TM = 512


def _params(sem):
    return pltpu.CompilerParams(dimension_semantics=sem, vmem_limit_bytes=VMEM_LIMIT)


def _split_dot(x, m, passes):
    acc = None
    r = x
    for p in range(passes):
        h = r.astype(BF16)
        term = jnp.dot(h, m, preferred_element_type=F32)
        acc = term if acc is None else acc + term
        if p + 1 < passes:
            r = r - h.astype(F32)
    return acc


def _inproj_kernel(x_ref, g_ref, w_ref, qkv_ref, rest_ref, h_scr, *, n_qkv_blocks, q_scale):
    j = pl.program_id(1)

    @pl.when(j == 0)
    def _():
        x = x_ref[...]
        var = jnp.mean(x * x, axis=-1, keepdims=True)
        h_scr[...] = (x * lax.rsqrt(var + EPS) * g_ref[...]).astype(BF16)

    acc = jnp.dot(h_scr[...], w_ref[...], preferred_element_type=F32)

    @pl.when(j == 0)
    def _():
        qkv_ref[...] = (acc * q_scale).astype(BF16)

    @pl.when(jnp.logical_and(j > 0, j < n_qkv_blocks))
    def _():
        qkv_ref[...] = acc.astype(BF16)

    @pl.when(j >= n_qkv_blocks)
    def _():
        rest_ref[...] = acc


def _inproj(x2, ln_g, w_in_bf16, sb_width):
    n, d = x2.shape
    cols = w_in_bf16.shape[1]
    tm, tn = min(INPROJ_TM, n), INPROJ_TN
    assert sb_width == tn, "q block must be exactly one column tile"
    nq = 3 * sb_width // tn
    nj = cols // tn
    kern = functools.partial(_inproj_kernel, n_qkv_blocks=nq, q_scale=SB_HEAD_DIM ** -0.5)
    return pl.pallas_call(
        kern,
        grid=(n // tm, nj),
        in_specs=[
            pl.BlockSpec((tm, d), lambda i, j: (i, 0)),
            pl.BlockSpec((1, d), lambda i, j: (0, 0)),
            pl.BlockSpec((d, tn), lambda i, j: (0, j)),
        ],
        out_specs=[
            pl.BlockSpec((tm, tn), lambda i, j: (i, jnp.minimum(j, nq - 1))),
            pl.BlockSpec((tm, tn), lambda i, j: (i, jnp.maximum(j - nq, 0))),
        ],
        out_shape=[
            jax.ShapeDtypeStruct((n, nq * tn), BF16),
            jax.ShapeDtypeStruct((n, cols - nq * tn), F32),
        ],
        scratch_shapes=[pltpu.VMEM((tm, d), BF16)],
        compiler_params=_params(("parallel", "arbitrary")),
        name="inproj",
    )(x2, ln_g.reshape(1, d), w_in_bf16)


ATTN_STAGES = 3
MASK_BIAS = -1e30
ATTN_SKIP_LOG2 = 160.0


def _attn_kernel(q_ref, k_ref, v_ref, tri_ref, bias_ref, o_ref,
                 z0, z1, z2, i0, i1, i2, acc_ref, c_ref, *, t, nq):
    zbuf = (z0, z1, z2)
    ibuf = (i0, i1, i2)
    for r in zbuf + ibuf:
        r[...] = jnp.zeros_like(r)
    acc_ref[...] = jnp.zeros_like(acc_ref)
    c_ref[...] = jnp.zeros_like(c_ref)

    lane = lax.broadcasted_iota(jnp.int32, (t, LANES), 1)
    head0 = lane < SB_HEAD_DIM
    nt = (((1,), (1,)), ((), ()))

    def stage_a(qi, kj, slot):
        q = q_ref[pl.ds(pl.multiple_of(qi * t, t), t), :]
        zero = jnp.zeros_like(q)
        q2 = jnp.concatenate([jnp.where(head0, q, zero), jnp.where(head0, zero, q)], axis=0)
        k = k_ref[pl.ds(pl.multiple_of(kj * t, t), t), :]
        z = lax.dot_general(q2, k, nt, preferred_element_type=F32)
        bias = bias_ref[(qi == kj).astype(jnp.int32)]
        zbuf[slot][...] = z * LOG2E + jnp.concatenate([bias, bias], axis=0)

    def stage_b(slot):
        z = zbuf[slot][...]
        p = jnp.maximum(z, 0.0) + jnp.log(1.0 + jnp.exp2(-jnp.abs(z))) * LOG2E
        incl = jnp.dot(p.astype(BF16), tri_ref[...], preferred_element_type=F32)
        ibuf[slot][...] = incl
        return incl[:, 0:1]

    def stage_c(qi, kj, slot):
        first = qi == kj
        incl = ibuf[slot][...]
        c = jnp.where(first, 0.0, c_ref[...])
        a = jnp.exp2(zbuf[slot][...] - incl - c)
        v = v_ref[pl.ds(pl.multiple_of(kj * t, t), t), :]
        pv = jnp.dot(a.astype(BF16), v, preferred_element_type=F32)
        acc = jnp.where(first, pv, acc_ref[...] + pv)
        acc_ref[...] = acc
        c_new = c + incl[:, 0:1]
        c_ref[...] = c_new
        o_ref[pl.ds(pl.multiple_of(qi * t, t), t), :] = (
            jnp.where(head0, acc[0:t], acc[t:2 * t]).astype(o_ref.dtype))
        return c_new

    def block(carry):
        qs, ks, qn, kn, skip_q, drained = carry
        qs, ks = list(qs), list(ks)
        drained = drained + (qn >= nq).astype(jnp.int32)
        for r in range(ATTN_STAGES):
            sc, sb = (r + 1) % ATTN_STAGES, (r + 2) % ATTN_STAGES
            carry_c = stage_c(qs[sc], ks[sc], sc)
            total_b = stage_b(sb)
            carry_b = jnp.where(qs[sb] == ks[sb], 0.0, carry_c) + total_b
            drain = qn >= nq
            qa = jnp.where(drain, 0, qn)
            ka = jnp.where(drain, 0, kn)
            stage_a(qa, ka, r)
            jump = jnp.logical_or(kn == 0, skip_q == qn)
            kn = jnp.where(drain, kn, jnp.where(jump, qn + 1, kn - 1))
            qn = jnp.where(drain, qn, jnp.where(jump, qn + 1, qn))
            skip_q = jnp.where(jnp.min(carry_b) >= ATTN_SKIP_LOG2, qs[sb], skip_q)
            qs[r], ks[r] = qa, ka
        return tuple(qs), tuple(ks), qn, kn, skip_q, drained

    zero = jnp.int32(0)
    init = ((zero,) * ATTN_STAGES, (zero,) * ATTN_STAGES, zero, zero, jnp.int32(-1), zero)
    lax.while_loop(lambda carry: carry[5] < 1, block, init)


def _attn(qkv, bsz, seq, tri, bias):
    n = bsz * seq
    t = min(ATTN_T, seq)
    pairs = SB_HEADS * SB_HEAD_DIM // LANES
    kern = functools.partial(_attn_kernel, t=t, nq=seq // t)
    return pl.pallas_call(
        kern,
        grid=(bsz, pairs),
        in_specs=[
            pl.BlockSpec((seq, LANES), lambda b, p: (b, p)),
            pl.BlockSpec((seq, LANES), lambda b, p: (b, pairs + p)),
            pl.BlockSpec((seq, LANES), lambda b, p: (b, 2 * pairs + p)),
            pl.BlockSpec((t, t), lambda b, p: (0, 0)),
            pl.BlockSpec((2, t, t), lambda b, p: (0, 0, 0)),
        ],
        out_specs=pl.BlockSpec((seq, LANES), lambda b, p: (b, p)),
        out_shape=jax.ShapeDtypeStruct((n, pairs * LANES), BF16),
        scratch_shapes=([pltpu.VMEM((2 * t, t), F32)] * (2 * ATTN_STAGES)
                        + [pltpu.VMEM((2 * t, LANES), F32), pltpu.VMEM((2 * t, 1), F32)]),
        compiler_params=_params(("parallel", "parallel")),
        name="attn",
    )(qkv, qkv, qkv, tri, bias)


def _group_dot(x, bd):
    g = bd.shape[0]
    parts = [jnp.dot(x[:, i:i + g], bd, preferred_element_type=F32)
             for i in range(0, x.shape[1], g)]
    return jnp.concatenate(parts, axis=1)


def _hgrn_kernel(q_ref, f_ref, i_ref, g_ref, lbl_ref, ng_ref, tri_ref, bd_ref, o_ref,
                 st_scr, wpad, vpad, wsh, vsh, *, tt, c, layer):
    ti = pl.program_id(1)
    w = q_ref.shape[1]
    grp = bd_ref.shape[0]
    nch = tt // c

    @pl.when(ti == 0)
    def _():
        st_scr[...] = jnp.zeros_like(st_scr)

    lg = lbl_ref[...]
    e = jnp.exp(lg - jnp.max(lg, axis=0, keepdims=True))
    lb = jnp.sum(e[0:layer + 1], axis=0, keepdims=True) / jnp.sum(e, axis=0, keepdims=True)

    f = lb + (1.0 - lb) * jax.nn.sigmoid(f_ref[...])
    kk = 1.0 - f
    qv = q_ref[...]
    qs = qv * jax.nn.sigmoid(qv)
    v = i_ref[...]
    bd = bd_ref[...]
    cum = _split_dot_left(tri_ref[...], jnp.log(f) * LOG2E, 3)
    wk = jnp.log(kk) * LOG2E - cum

    pos = lax.broadcasted_iota(jnp.int32, (tt, 1), 0) % c
    wpad[0:SUBLANES, :] = jnp.zeros((SUBLANES, w), F32)
    vpad[0:SUBLANES, :] = jnp.zeros((SUBLANES, w), F32)
    wpad[SUBLANES:SUBLANES + tt, :] = wk
    vpad[SUBLANES:SUBLANES + tt, :] = v
    for b in range(SUBLANES):
        wb = wpad[SUBLANES - b:SUBLANES - b + tt, :]
        vb = vpad[SUBLANES - b:SUBLANES - b + tt, :]
        if b:
            wb = jnp.where(pos >= b, wb, -jnp.inf)
        wsh[b] = wb.reshape(nch, c, w)
        vsh[b] = vb.reshape(nch, c, w)

    qs3 = qs.reshape(nch, c, w)
    cum3 = cum.reshape(nch, c, w)
    acc = None
    for a in reversed(range(c // SUBLANES)):
        rows = c - SUBLANES * a
        qa = qs3[:, SUBLANES * a:, :].reshape(nch * rows, w)
        ca = cum3[:, SUBLANES * a:, :].reshape(nch * rows, w)
        acc_a = None
        for b in range(SUBLANES):
            wb = wsh[b, :, 0:rows, :].reshape(nch * rows, w)
            vb = vsh[b, :, 0:rows, :].reshape(nch * rows, w)
            dd = qa * jnp.exp2(ca + wb)
            term = _group_dot(dd.astype(BF16), bd) * vb
            acc_a = term if acc_a is None else acc_a + term
        acc_a = acc_a.reshape(nch, rows, w)
        if acc is not None:
            acc_a = acc_a + jnp.concatenate([jnp.zeros((nch, SUBLANES, w), F32), acc], axis=1)
        acc = acc_a
    acc = acc.reshape(tt, w)

    bdmask = bd != 0
    outs = []
    for ci in range(nch):
        r0 = ci * c
        cum_c = cum[r0:r0 + c]
        last = cum_c[c - 1:c]
        qd = (qs[r0:r0 + c] * jnp.exp2(cum_c)).astype(BF16)
        kd = (kk[r0:r0 + c] * jnp.exp2(last - cum_c)).astype(BF16)
        vc = v[r0:r0 + c].astype(BF16)
        dec = jnp.exp2(last)
        o_parts = []
        for gi in range(w // grp):
            sl = slice(gi * grp, (gi + 1) * grp)
            st = st_scr[gi]
            o_parts.append(lax.dot_general(qd[:, sl], st.astype(BF16), (((1,), (1,)), ((), ())),
                                           preferred_element_type=F32))
            upd = lax.dot_general(vc[:, sl], kd[:, sl], (((0,), (0,)), ((), ())),
                                  preferred_element_type=F32)
            st_scr[gi] = st * dec[:, sl] + jnp.where(bdmask, upd, 0.0)
        outs.append(jnp.concatenate(o_parts, axis=1))
    o = acc + jnp.concatenate(outs, axis=0)

    o2 = o * o
    o2_hi = o2.astype(BF16)
    o2_lo = (o2 - o2_hi.astype(F32)).astype(BF16)
    ms = (_group_dot(o2_hi, bd) + _group_dot(o2_lo, bd)) * (1.0 / HG_DIM)
    gv = g_ref[...]
    o = o * lax.rsqrt(ms + EPS) * ng_ref[...] * (gv * jax.nn.sigmoid(gv))
    o_ref[...] = o.astype(o_ref.dtype)


def _split_dot_left(m, x, passes):
    acc = None
    r = x
    for p in range(passes):
        h = r.astype(BF16)
        term = jnp.dot(m, h, preferred_element_type=F32)
        acc = term if acc is None else acc + term
        if p + 1 < passes:
            r = r - h.astype(F32)
    return acc


def _hgrn(rest, lb_logits, norm_g, bsz, seq, layer, tri, bd):
    n = bsz * seq
    w = HG_HEADS * HG_DIM
    grp = bd.shape[0]
    tt = min(HGRN_TT, seq)
    c = min(HGRN_C, tt)
    nt = seq // tt
    kern = functools.partial(_hgrn_kernel, tt=tt, c=c, layer=layer)
    col = lambda j: pl.BlockSpec((tt, w), lambda b, i, j=j: (b * nt + i, j))
    const = lambda shape: pl.BlockSpec(shape, lambda b, i: (0, 0))
    return pl.pallas_call(
        kern,
        grid=(bsz, nt),
        in_specs=[col(0), col(1), col(2), col(3),
                  const(lb_logits.shape), const((1, w)), const((tt, tt)), const((grp, grp))],
        out_specs=pl.BlockSpec((tt, w), lambda b, i: (b * nt + i, 0)),
        out_shape=jax.ShapeDtypeStruct((n, w), BF16),
        scratch_shapes=([pltpu.VMEM((w // grp, grp, grp), F32)]
                        + [pltpu.VMEM((SUBLANES + tt, w), F32)] * 2
                        + [pltpu.VMEM((SUBLANES, tt // c, c, w), F32)] * 2),
        compiler_params=_params(("parallel", "arbitrary")),
        name="hgrn",
    )(rest, rest, rest, rest, lb_logits, norm_g.reshape(1, w), tri, bd)


ROUTE_I1, ROUTE_I2, ROUTE_W1, ROUTE_W2 = N_EXPERTS, N_EXPERTS + 1, N_EXPERTS + 2, N_EXPERTS + 3


def _route(lg):
    lane = lax.broadcasted_iota(jnp.int32, lg.shape, 1)
    neg = jnp.float32(-jnp.inf)
    big = jnp.int32(LANES)
    gmask = jnp.logical_and(lane >= N_EXPERTS, lane < N_EXPERTS + N_GROUPS)
    gl = jnp.where(gmask, lg, neg)
    gmax = jnp.max(gl, axis=1, keepdims=True)
    gidx = jnp.min(jnp.where(gl == gmax, lane, big), axis=1, keepdims=True) - N_EXPERTS
    w_grp = 1.0 / jnp.sum(jnp.where(gmask, jnp.exp(gl - gmax), 0.0), axis=1, keepdims=True)
    in_grp = jnp.logical_and(lane < N_EXPERTS, lane // EXPERTS_PER_GROUP == gidx)
    l1 = jnp.where(in_grp, lg, neg)
    v1 = jnp.max(l1, axis=1, keepdims=True)
    i1 = jnp.min(jnp.where(l1 == v1, lane, big), axis=1, keepdims=True)
    l2 = jnp.where(jnp.logical_and(in_grp, lane != i1), lg, neg)
    v2 = jnp.max(l2, axis=1, keepdims=True)
    i2 = jnp.min(jnp.where(l2 == v2, lane, big), axis=1, keepdims=True)
    e2 = jnp.exp(v2 - v1)
    p1 = 1.0 / (1.0 + e2)
    p2 = e2 * p1
    out = jnp.where(jnp.logical_or(lane == i1, lane == i2), 1.0, 0.0)
    out = jnp.where(lane == ROUTE_I1, i1.astype(F32), out)
    out = jnp.where(lane == ROUTE_I2, i2.astype(F32), out)
    out = jnp.where(lane == ROUTE_W1, p1 * w_grp, out)
    return jnp.where(lane == ROUTE_W2, p2 * w_grp, out)


def _store_token_tiles(ref, x):
    rows, d = x.shape
    assert d == SUBLANES * LANES
    for s in range(SUBLANES):
        ref[pl.ds(s, rows, stride=SUBLANES), :] = x[:, s * LANES:(s + 1) * LANES]


def _load_token_tiles(ref):
    rows = ref.shape[0] // SUBLANES
    return jnp.concatenate([ref[pl.ds(s, rows, stride=SUBLANES), :] for s in range(SUBLANES)],
                           axis=1)


def _merge_kernel(x_ref, ysb_ref, yhg_ref, gsb_ref, ghg_ref, wbs_ref, wbh_ref, wo_ref,
                  ln_ref, wr_hi_ref, wr_lo_ref, br_ref, x1_ref, t_ref, route_ref):
    a = jnp.dot(ysb_ref[...], wbs_ref[...], preferred_element_type=F32)
    b = jnp.dot(yhg_ref[...], wbh_ref[...], preferred_element_type=F32)
    merged = jax.nn.sigmoid(gsb_ref[...]) * a + jax.nn.sigmoid(ghg_ref[...]) * b
    x1 = x_ref[...] + jnp.dot(merged.astype(BF16), wo_ref[...], preferred_element_type=F32)
    x1_ref[...] = x1
    var = jnp.mean(x1 * x1, axis=-1, keepdims=True)
    t = x1 * lax.rsqrt(var + EPS) * ln_ref[...]
    _store_token_tiles(t_ref, t)
    t_hi = t.astype(BF16)
    t_lo = (t - t_hi.astype(F32)).astype(BF16)
    lg = (jnp.dot(t_hi, wr_hi_ref[...], preferred_element_type=F32)
          + jnp.dot(t_lo, wr_hi_ref[...], preferred_element_type=F32)
          + jnp.dot(t_hi, wr_lo_ref[...], preferred_element_type=F32)) + br_ref[...]
    route_ref[...] = _route(lg)


def _merge(x2, y_sb, y_hg, rest, wbs, wbh, wo, ln_g, wr_hi, wr_lo, br):
    n, d = x2.shape
    tm = min(MERGE_TM, n)
    w_sb, w_hg = y_sb.shape[1], y_hg.shape[1]
    gate_blk = (rest.shape[1] - 2 * d) // d
    row = lambda wdt, j=0: pl.BlockSpec((tm, wdt), lambda i, j=j: (i, j))
    const = lambda shape: pl.BlockSpec(shape, lambda i: (0, 0))
    return pl.pallas_call(
        _merge_kernel,
        grid=(n // tm,),
        in_specs=[row(d), row(w_sb), row(w_hg), row(d, gate_blk), row(d, gate_blk + 1),
                  const(wbs.shape), const(wbh.shape), const(wo.shape), const((1, d)),
                  const(wr_hi.shape), const(wr_lo.shape), const((1, LANES))],
        out_specs=[row(d), pl.BlockSpec((tm * SUBLANES, LANES), lambda i: (i, 0)), row(LANES)],
        out_shape=[jax.ShapeDtypeStruct((n, d), F32),
                   jax.ShapeDtypeStruct((n * SUBLANES, LANES), F32),
                   jax.ShapeDtypeStruct((n, LANES), F32)],
        compiler_params=_params(("parallel",)),
        name="merge",
    )(x2, y_sb, y_hg, rest, rest, wbs, wbh, wo, ln_g.reshape(1, d), wr_hi, wr_lo, br)


def _plan_kernel(route_ref, ltri_ref, utri_ref, pos_ref, te_ref, meta_ref, tot_scr, run_scr,
                 *, tile_rows):
    ph = pl.program_id(0)
    i = pl.program_id(1)
    route = route_ref[...]
    tm = route.shape[0]
    lane = lax.broadcasted_iota(jnp.int32, (tm, LANES), 1)
    sel = jnp.where(lane < N_EXPERTS, route, 0.0)

    @pl.when(jnp.logical_and(ph == 0, i == 0))
    def _():
        tot_scr[...] = jnp.zeros_like(tot_scr)

    @pl.when(ph == 0)
    def _():
        tot_scr[...] += jnp.sum(sel, axis=0, keepdims=True)

    @pl.when(ph == 1)
    def _():
        tot = tot_scr[...]
        padded = jnp.ceil(tot * (1.0 / tile_rows)) * tile_rows
        off = _split_dot(jnp.broadcast_to(padded, (SUBLANES, LANES)), utri_ref[...], 3)[0:1]

        @pl.when(i == 0)
        def _():
            run_scr[...] = jnp.zeros_like(run_scr)
            lane1 = lax.broadcasted_iota(jnp.int32, (1, LANES), 1)
            is_e = lane1 < N_EXPERTS
            end = off + padded
            n_valid = (jnp.sum(jnp.where(is_e, padded, 0.0), axis=1, keepdims=True)
                       * (1.0 / tile_rows))
            rows = te_ref.shape[0]
            j = lax.broadcasted_iota(jnp.int32, (rows, LANES), 0).astype(F32)
            start = jnp.minimum(j, n_valid - 1.0) * tile_rows
            lane2 = lax.broadcasted_iota(jnp.int32, (rows, LANES), 1)
            done = jnp.where(jnp.logical_and(lane2 < N_EXPERTS, end <= start), 1.0, 0.0)
            te = jnp.sum(done, axis=1, keepdims=True)
            te_ref[...] = jnp.broadcast_to(te, (rows, LANES)).astype(jnp.int32)
            last_tile = jnp.where(jnp.logical_and(is_e, padded > 0),
                                  end * (1.0 / tile_rows) - 1.0, -1.0)
            meta = jnp.where(lane1 == N_EXPERTS, n_valid, last_tile)
            meta_ref[...] = jnp.broadcast_to(meta, meta_ref.shape).astype(jnp.int32)

        rank = jnp.dot(ltri_ref[...], sel.astype(BF16), preferred_element_type=F32) + run_scr[...]
        dest = off + rank
        lane_f = lane.astype(F32)
        p1 = jnp.sum(jnp.where(lane_f == route[:, ROUTE_I1:ROUTE_I1 + 1], dest, 0.0),
                     axis=1, keepdims=True)
        p2 = jnp.sum(jnp.where(lane_f == route[:, ROUTE_I2:ROUTE_I2 + 1], dest, 0.0),
                     axis=1, keepdims=True)
        pos_ref[...] = jnp.where(lane == 0, p1, jnp.where(lane == 1, p2, 0.0)).astype(jnp.int32)
        run_scr[...] += jnp.sum(sel, axis=0, keepdims=True)


def _plan(route, n_tiles_max):
    n = route.shape[0]
    tm = min(PLAN_TM, n)
    r = np.arange(tm)
    ltri = jnp.asarray((r[None, :] < r[:, None]).astype(np.float32), dtype=BF16)
    e = np.arange(LANES)
    utri = jnp.asarray((e[:, None] < e[None, :]).astype(np.float32), dtype=BF16)
    te_rows = -(-n_tiles_max // SUBLANES) * SUBLANES
    kern = functools.partial(_plan_kernel, tile_rows=MOE_TM)
    return pl.pallas_call(
        kern,
        grid=(2, n // tm),
        in_specs=[pl.BlockSpec((tm, LANES), lambda p, i: (i, 0)),
                  pl.BlockSpec((tm, tm), lambda p, i: (0, 0)),
                  pl.BlockSpec((LANES, LANES), lambda p, i: (0, 0))],
        out_specs=[pl.BlockSpec((tm, LANES), lambda p, i: (i * p, 0)),
                   pl.BlockSpec((te_rows, LANES), lambda p, i: (0, 0)),
                   pl.BlockSpec((SUBLANES, LANES), lambda p, i: (0, 0))],
        out_shape=[jax.ShapeDtypeStruct((n, LANES), jnp.int32),
                   jax.ShapeDtypeStruct((te_rows, LANES), jnp.int32),
                   jax.ShapeDtypeStruct((SUBLANES, LANES), jnp.int32)],
        scratch_shapes=[pltpu.VMEM((1, LANES), F32), pltpu.VMEM((1, LANES), F32)],
        compiler_params=_params(("arbitrary", "arbitrary")),
        name="plan",
    )(route, ltri, utri)


DMA_UNROLL = 8


def _row_copy(src_ref, r, dst_ref, p, sem):
    return pltpu.make_async_copy(src_ref.at[pl.ds(pl.multiple_of(r * SUBLANES, SUBLANES), SUBLANES)],
                                 dst_ref.at[pl.ds(pl.multiple_of(p * SUBLANES, SUBLANES), SUBLANES)],
                                 sem)


def _dispatch_kernel(pos1_ref, pos2_ref, meta_ref, t_ref, xs_ref, zero_scr, sem, *, tm, tile_rows):
    i = pl.program_id(0)
    tile_rows = tile_rows * SUBLANES
    n_tiles = xs_ref.shape[0] // tile_rows

    @pl.when(i == 0)
    def _():
        zero_scr[...] = jnp.zeros_like(zero_scr)
        n_valid = meta_ref[N_EXPERTS]
        clears = [(meta_ref[e], meta_ref[e] >= 0) for e in range(N_EXPERTS)]
        clears += [(n_tiles - 1 - k, n_tiles - 1 - k >= n_valid) for k in range(N_EXPERTS)]

        def clear(tile):
            return pltpu.make_async_copy(
                zero_scr, xs_ref.at[pl.ds(tile * tile_rows, tile_rows)], sem)

        for tile, cond in clears:
            @pl.when(cond)
            def _():
                clear(tile).start()
        for tile, cond in clears:
            @pl.when(cond)
            def _():
                clear(tile).wait()

    base = i * tm

    def start(r, carry):
        _row_copy(t_ref, r, xs_ref, pos1_ref[base + r], sem).start()
        _row_copy(t_ref, r, xs_ref, pos2_ref[base + r], sem).start()
        return carry

    def wait(r, carry):
        _row_copy(t_ref, r, xs_ref, pos1_ref[base + r], sem).wait()
        _row_copy(t_ref, r, xs_ref, pos2_ref[base + r], sem).wait()
        return carry

    lax.fori_loop(0, tm, start, 0, unroll=DMA_UNROLL)
    lax.fori_loop(0, tm, wait, 0, unroll=DMA_UNROLL)


def _dispatch(pos1, pos2, meta, t, n_rows):
    n = t.shape[0] // SUBLANES
    tm = min(DISPATCH_TM, n)
    kern = functools.partial(_dispatch_kernel, tm=tm, tile_rows=MOE_TM)
    grid_spec = pltpu.PrefetchScalarGridSpec(
        num_scalar_prefetch=3,
        grid=(n // tm,),
        in_specs=[pl.BlockSpec((tm * SUBLANES, LANES), lambda i, *_: (i, 0))],
        out_specs=pl.BlockSpec(memory_space=pl.ANY),
        scratch_shapes=[pltpu.VMEM((MOE_TM * SUBLANES, LANES), F32), pltpu.SemaphoreType.DMA(())],
    )
    return pl.pallas_call(
        kern,
        grid_spec=grid_spec,
        out_shape=jax.ShapeDtypeStruct((n_rows * SUBLANES, LANES), F32),
        compiler_params=_params(("arbitrary",)),
        name="dispatch",
    )(pos1, pos2, meta, t)


def _expert_kernel(te_ref, nv_ref, xs_ref, wg_ref, wu_ref, wd_ref, ys_ref):
    in_use = pl.program_id(0) < nv_ref[0]

    @pl.when(in_use)
    def _():
        x = _load_token_tiles(xs_ref).astype(BF16)
        hg = jnp.dot(x, wg_ref[...], preferred_element_type=F32)
        hu = jnp.dot(x, wu_ref[...], preferred_element_type=F32)
        a = (hg * jax.nn.sigmoid(hg) * hu).astype(BF16)
        _store_token_tiles(ys_ref, jnp.dot(a, wd_ref[...], preferred_element_type=F32))

    @pl.when(jnp.logical_not(in_use))
    def _():
        ys_ref[...] = jnp.zeros_like(ys_ref)


def _experts(te, nv, xs, wg, wu, wd):
    _, d, de = wg.shape
    assert d == SUBLANES * LANES
    tile = lambda j, te, nv: (j, 0)
    grid_spec = pltpu.PrefetchScalarGridSpec(
        num_scalar_prefetch=2,
        grid=(xs.shape[0] // (MOE_TM * SUBLANES),),
        in_specs=[pl.BlockSpec((MOE_TM * SUBLANES, LANES), tile),
                  pl.BlockSpec((None, d, de), lambda j, te, nv: (te[j], 0, 0)),
                  pl.BlockSpec((None, d, de), lambda j, te, nv: (te[j], 0, 0)),
                  pl.BlockSpec((None, de, d), lambda j, te, nv: (te[j], 0, 0))],
        out_specs=pl.BlockSpec((MOE_TM * SUBLANES, LANES), tile),
    )
    return pl.pallas_call(
        _expert_kernel,
        grid_spec=grid_spec,
        out_shape=jax.ShapeDtypeStruct(xs.shape, F32),
        compiler_params=_params(("arbitrary",)),
        name="experts",
    )(te, nv, xs, wg, wu, wd)


def _combine_kernel(pos1_ref, pos2_ref, ys_ref, x1_ref, route_ref, fg_ref, o_ref, g1, g2, sem,
                    *, tm):
    base = pl.program_id(0) * tm

    def start(r, carry):
        _row_copy(ys_ref, pos1_ref[base + r], g1, r, sem).start()
        _row_copy(ys_ref, pos2_ref[base + r], g2, r, sem).start()
        return carry

    def wait(r, carry):
        _row_copy(ys_ref, pos1_ref[base + r], g1, r, sem).wait()
        _row_copy(ys_ref, pos2_ref[base + r], g2, r, sem).wait()
        return carry

    lax.fori_loop(0, tm, start, 0, unroll=DMA_UNROLL)
    lax.fori_loop(0, tm, wait, 0, unroll=DMA_UNROLL)
    route = route_ref[...]
    w1 = route[:, ROUTE_W1:ROUTE_W1 + 1]
    w2 = route[:, ROUTE_W2:ROUTE_W2 + 1]
    x2 = x1_ref[...] + (w1 * _load_token_tiles(g1) + w2 * _load_token_tiles(g2))
    var = jnp.mean(x2 * x2, axis=-1, keepdims=True)
    o_ref[...] = x2 * lax.rsqrt(var + EPS) * fg_ref[...]


def _combine(pos1, pos2, ys, x1, route, final_g):
    n, d = x1.shape
    tm = min(DISPATCH_TM, n)
    kern = functools.partial(_combine_kernel, tm=tm)
    grid_spec = pltpu.PrefetchScalarGridSpec(
        num_scalar_prefetch=2,
        grid=(n // tm,),
        in_specs=[pl.BlockSpec(memory_space=pl.ANY),
                  pl.BlockSpec((tm, d), lambda i, *_: (i, 0)),
                  pl.BlockSpec((tm, LANES), lambda i, *_: (i, 0)),
                  pl.BlockSpec((1, d), lambda i, *_: (0, 0))],
        out_specs=pl.BlockSpec((tm, d), lambda i, *_: (i, 0)),
        scratch_shapes=[pltpu.VMEM((tm * SUBLANES, LANES), F32)] * 2 + [pltpu.SemaphoreType.DMA(())],
    )
    return pl.pallas_call(
        kern,
        grid_spec=grid_spec,
        out_shape=jax.ShapeDtypeStruct((n, d), F32),
        compiler_params=_params(("arbitrary",)),
        name="combine",
    )(pos1, pos2, ys, x1, route, final_g.reshape(1, d))


def _moe_sparse(t, route, wg, wu, wd, x1, final_g):
    n = x1.shape[0]
    n_tiles_max = 2 * n // MOE_TM + N_EXPERTS
    pos, te, meta = _plan(route, n_tiles_max)
    pos1, pos2 = pos[:, 0], pos[:, 1]
    xs = _dispatch(pos1, pos2, meta[0, :N_EXPERTS + 1], t, n_tiles_max * MOE_TM)
    ys = _experts(te[:n_tiles_max, 0], meta[0, N_EXPERTS:N_EXPERTS + 1], xs, wg, wu, wd)
    return _combine(pos1, pos2, ys, x1, route, final_g)


def _suffix_ones(t):
    j = np.arange(t)[:, None]
    s = np.arange(t)[None, :]
    return jnp.asarray((j >= s).astype(np.float32), dtype=BF16)


def _causal_bias(t):
    row = np.arange(t)[:, None]
    col = np.arange(t)[None, :]
    diag = np.where(col < row, 0.0, MASK_BIAS).astype(np.float32)
    return jnp.asarray(np.stack([np.zeros_like(diag), diag]))


def _chunk_prefix_ones(tt, c):
    t = np.arange(tt)[:, None]
    j = np.arange(tt)[None, :]
    return jnp.asarray(((j <= t) & (t // c == j // c)).astype(np.float32), dtype=BF16)


def _block_diag_ones(w, blk):
    a = np.arange(w)
    return jnp.asarray((a[:, None] // blk == a[None, :] // blk).astype(np.float32), dtype=BF16)


def kernel(x, ln1_g, w_in, w_branch_sb, w_branch_hg, hg_norm_g, hg_lb_logits, w_out, ln2_g,
           w_router_group, b_router_group, w_router_expert, b_router_expert,
           w_exp_gate, w_exp_up, w_exp_down, final_g):
    bsz, seq, d = x.shape
    depth = w_in.shape[0]
    n = bsz * seq
    sb_width = SB_HEADS * SB_HEAD_DIM
    hg_width = HG_HEADS * HG_DIM

    tri_attn = _suffix_ones(min(ATTN_T, seq))
    bias_attn = _causal_bias(min(ATTN_T, seq))
    tt = min(HGRN_TT, seq)
    tri_hg = _chunk_prefix_ones(tt, min(HGRN_C, tt))
    bd = _block_diag_ones(min(MXU_DIM, hg_width), HG_DIM)

    x2 = x.reshape(n, d)
    for l in range(depth):
        qkv, rest = _inproj(x2, ln1_g[l], w_in[l].astype(BF16), sb_width)
        y_sb = _attn(qkv, bsz, seq, tri_attn, bias_attn)
        y_hg = _hgrn(rest, hg_lb_logits, hg_norm_g[l], bsz, seq, l, tri_hg, bd)

        pad = LANES - N_EXPERTS - N_GROUPS
        wr = jnp.concatenate([w_router_expert[l], w_router_group[l],
                              jnp.zeros((d, pad), F32)], axis=1)
        wr_hi = wr.astype(BF16)
        wr_lo = (wr - wr_hi.astype(F32)).astype(BF16)
        br = jnp.concatenate([b_router_expert[l], b_router_group[l],
                              jnp.zeros((pad,), F32)]).reshape(1, LANES)

        last = l == depth - 1
        x1, t, route = _merge(x2, y_sb, y_hg, rest, w_branch_sb[l].astype(BF16),
                              w_branch_hg[l].astype(BF16), w_out[l].astype(BF16), ln2_g[l],
                              wr_hi, wr_lo, br)
        assert last, "final rmsnorm is fused into the last layer's combine kernel"
        x2 = _moe_sparse(t, route, w_exp_gate[l].astype(BF16), w_exp_up[l].astype(BF16),
                         w_exp_down[l].astype(BF16), x1, final_g)
    return x2.reshape(bsz, seq, d)
```

```python
import functools

import jax
import jax.numpy as jnp
import numpy as np
from jax import lax
from jax.experimental import pallas as pl
from jax.experimental.pallas import tpu as pltpu

F32 = jnp.float32
BF16 = jnp.bfloat16

EPS = 1e-6
SB_HEADS = 8
SB_HEAD_DIM = 64
HG_HEADS = 8
HG_DIM = 64
N_GROUPS = 4
EXPERTS_PER_GROUP = 4
N_EXPERTS = N_GROUPS * EXPERTS_PER_GROUP

LANES = 128
SUBLANES = 8
MXU_DIM = 256
LOG2E = 1.4426950408889634
VMEM_LIMIT = 48 * 1024 * 1024

INPROJ_TM = 1024
INPROJ_TN = 512
ATTN_T = MXU_DIM
HGRN_TT = 256
HGRN_C = 32
MERGE_TM = 512
MOE_TM = 512
PLAN_TM = 512
DISPATCH_TM = 512


def _params(sem):
    return pltpu.CompilerParams(dimension_semantics=sem, vmem_limit_bytes=VMEM_LIMIT)


def _split_dot(x, m, passes):
    acc = None
    r = x
    for p in range(passes):
        h = r.astype(BF16)
        term = jnp.dot(h, m, preferred_element_type=F32)
        acc = term if acc is None else acc + term
        if p + 1 < passes:
            r = r - h.astype(F32)
    return acc


def _inproj_kernel(x_ref, g_ref, w_ref, qkv_ref, rest_ref, h_scr, *, q_scale):
    j = pl.program_id(1)

    @pl.when(j == 0)
    def _():
        x = x_ref[...]
        var = jnp.mean(x * x, axis=-1, keepdims=True)
        h_scr[...] = (x * lax.rsqrt(var + EPS) * g_ref[...]).astype(BF16)

    acc = jnp.dot(h_scr[...], w_ref[...].astype(BF16), preferred_element_type=F32)
    qkv_ref[...] = (acc * jnp.where(j == 0, q_scale, 1.0)).astype(BF16)
    rest_ref[...] = acc


def _inproj(x2, ln_g, w_in, sb_width):
    n, d = x2.shape
    cols = w_in.shape[1]
    tm, tn = min(INPROJ_TM, n), INPROJ_TN
    assert sb_width == tn, "q block must be exactly one column tile"
    nq = 3 * sb_width // tn
    nj = cols // tn
    kern = functools.partial(_inproj_kernel, q_scale=SB_HEAD_DIM ** -0.5)
    return pl.pallas_call(
        kern,
        grid=(n // tm, nj),
        in_specs=[
            pl.BlockSpec((tm, d), lambda i, j: (i, 0)),
            pl.BlockSpec((1, d), lambda i, j: (0, 0)),
            pl.BlockSpec((d, tn), lambda i, j: (0, j)),
        ],
        out_specs=[
            pl.BlockSpec((tm, tn), lambda i, j: (i, jnp.minimum(j, nq))),
            pl.BlockSpec((tm, tn), lambda i, j: (i, jnp.maximum(j - nq, 0))),
        ],
        out_shape=[
            jax.ShapeDtypeStruct((n, (nq + 1) * tn), BF16),
            jax.ShapeDtypeStruct((n, cols - nq * tn), F32),
        ],
        scratch_shapes=[pltpu.VMEM((tm, d), BF16)],
        compiler_params=_params(("parallel", "arbitrary")),
        name="inproj",
    )(x2, ln_g.reshape(1, d), w_in)


ATTN_STAGES = 3
MASK_BIAS = -1e30
ATTN_SKIP_LOG2 = 160.0


def _attn_kernel(q_ref, k_ref, v_ref, tri_ref, bias_ref, o_ref,
                 z0, z1, z2, i0, i1, i2, acc_ref, c_ref, *, t, nq):
    zbuf = (z0, z1, z2)
    ibuf = (i0, i1, i2)
    for r in zbuf + ibuf:
        r[...] = jnp.zeros_like(r)
    acc_ref[...] = jnp.zeros_like(acc_ref)
    c_ref[...] = jnp.zeros_like(c_ref)

    lane = lax.broadcasted_iota(jnp.int32, (t, LANES), 1)
    head0 = lane < SB_HEAD_DIM
    nt = (((1,), (1,)), ((), ()))

    def stage_a(qi, kj, slot):
        q = q_ref[pl.ds(pl.multiple_of(qi * t, t), t), :]
        zero = jnp.zeros_like(q)
        q2 = jnp.concatenate([jnp.where(head0, q, zero), jnp.where(head0, zero, q)], axis=0)
        k = k_ref[pl.ds(pl.multiple_of(kj * t, t), t), :]
        z = lax.dot_general(q2, k, nt, preferred_element_type=F32)
        bias = bias_ref[(qi == kj).astype(jnp.int32)]
        zbuf[slot][...] = z * LOG2E + jnp.concatenate([bias, bias], axis=0)

    def stage_b(slot):
        z = zbuf[slot][...]
        p = jnp.maximum(z, 0.0) + jnp.log(1.0 + jnp.exp2(-jnp.abs(z))) * LOG2E
        incl = jnp.dot(p.astype(BF16), tri_ref[...], preferred_element_type=F32)
        ibuf[slot][...] = incl
        return incl[:, 0:1]

    def stage_c(qi, kj, slot):
        first = qi == kj
        incl = ibuf[slot][...]
        c = jnp.where(first, 0.0, c_ref[...])
        a = jnp.exp2(zbuf[slot][...] - incl - c)
        v = v_ref[pl.ds(pl.multiple_of(kj * t, t), t), :]
        pv = jnp.dot(a.astype(BF16), v, preferred_element_type=F32)
        acc = jnp.where(first, pv, acc_ref[...] + pv)
        acc_ref[...] = acc
        c_new = c + incl[:, 0:1]
        c_ref[...] = c_new
        o_ref[pl.ds(pl.multiple_of(qi * t, t), t), :] = (
            jnp.where(head0, acc[0:t], acc[t:2 * t]).astype(o_ref.dtype))
        return c_new

    def block(carry):
        qs, ks, qn, kn, skip_q, drained = carry
        qs, ks = list(qs), list(ks)
        drained = drained + (qn >= nq).astype(jnp.int32)
        for r in range(ATTN_STAGES):
            sc, sb = (r + 1) % ATTN_STAGES, (r + 2) % ATTN_STAGES
            carry_c = stage_c(qs[sc], ks[sc], sc)
            total_b = stage_b(sb)
            carry_b = jnp.where(qs[sb] == ks[sb], 0.0, carry_c) + total_b
            drain = qn >= nq
            qa = jnp.where(drain, 0, qn)
            ka = jnp.where(drain, 0, kn)
            stage_a(qa, ka, r)
            jump = jnp.logical_or(kn == 0, skip_q == qn)
            kn = jnp.where(drain, kn, jnp.where(jump, qn + 1, kn - 1))
            qn = jnp.where(drain, qn, jnp.where(jump, qn + 1, qn))
            skip_q = jnp.where(jnp.min(carry_b) >= ATTN_SKIP_LOG2, qs[sb], skip_q)
            qs[r], ks[r] = qa, ka
        return tuple(qs), tuple(ks), qn, kn, skip_q, drained

    zero = jnp.int32(0)
    init = ((zero,) * ATTN_STAGES, (zero,) * ATTN_STAGES, zero, zero, jnp.int32(-1), zero)
    lax.while_loop(lambda carry: carry[5] < 1, block, init)


def _attn(qkv, bsz, seq, tri, bias):
    n = bsz * seq
    t = min(ATTN_T, seq)
    pairs = SB_HEADS * SB_HEAD_DIM // LANES
    kern = functools.partial(_attn_kernel, t=t, nq=seq // t)
    return pl.pallas_call(
        kern,
        grid=(bsz, pairs),
        in_specs=[
            pl.BlockSpec((seq, LANES), lambda b, p: (b, p)),
            pl.BlockSpec((seq, LANES), lambda b, p: (b, pairs + p)),
            pl.BlockSpec((seq, LANES), lambda b, p: (b, 2 * pairs + p)),
            pl.BlockSpec((t, t), lambda b, p: (0, 0)),
            pl.BlockSpec((2, t, t), lambda b, p: (0, 0, 0)),
        ],
        out_specs=pl.BlockSpec((seq, LANES), lambda b, p: (b, p)),
        out_shape=jax.ShapeDtypeStruct((n, pairs * LANES), BF16),
        scratch_shapes=([pltpu.VMEM((2 * t, t), F32)] * (2 * ATTN_STAGES)
                        + [pltpu.VMEM((2 * t, LANES), F32), pltpu.VMEM((2 * t, 1), F32)]),
        compiler_params=_params(("parallel", "parallel")),
        name="attn",
    )(qkv, qkv, qkv, tri, bias)


def _group_dot(x, bd):
    g = bd.shape[0]
    parts = [jnp.dot(x[:, i:i + g], bd, preferred_element_type=F32)
             for i in range(0, x.shape[1], g)]
    return jnp.concatenate(parts, axis=1)


def _hgrn_kernel(q_ref, f_ref, i_ref, g_ref, lbl_ref, ng_ref, tri_ref, bd_ref, o_ref,
                 st_scr, wpad, vpad, wsh, vsh, *, tt, c, layer):
    ti = pl.program_id(1)
    w = q_ref.shape[1]
    grp = bd_ref.shape[0]
    nch = tt // c

    @pl.when(ti == 0)
    def _():
        st_scr[...] = jnp.zeros_like(st_scr)

    lg = lbl_ref[...]
    e = jnp.exp(lg - jnp.max(lg, axis=0, keepdims=True))
    lb = jnp.sum(e[0:layer + 1], axis=0, keepdims=True) / jnp.sum(e, axis=0, keepdims=True)

    f = lb + (1.0 - lb) * jax.nn.sigmoid(f_ref[...])
    kk = 1.0 - f
    qv = q_ref[...]
    qs = qv * jax.nn.sigmoid(qv)
    v = i_ref[...]
    bd = bd_ref[...]
    cum = _split_dot_left(tri_ref[...], jnp.log(f) * LOG2E, 3)
    wk = jnp.log(kk) * LOG2E - cum

    pos = lax.broadcasted_iota(jnp.int32, (tt, 1), 0) % c
    wpad[0:SUBLANES, :] = jnp.zeros((SUBLANES, w), F32)
    vpad[0:SUBLANES, :] = jnp.zeros((SUBLANES, w), F32)
    wpad[SUBLANES:SUBLANES + tt, :] = wk
    vpad[SUBLANES:SUBLANES + tt, :] = v
    for b in range(SUBLANES):
        wb = wpad[SUBLANES - b:SUBLANES - b + tt, :]
        vb = vpad[SUBLANES - b:SUBLANES - b + tt, :]
        if b:
            wb = jnp.where(pos >= b, wb, -jnp.inf)
        wsh[b] = wb.reshape(nch, c, w)
        vsh[b] = vb.reshape(nch, c, w)

    qs3 = qs.reshape(nch, c, w)
    cum3 = cum.reshape(nch, c, w)
    acc = None
    for a in reversed(range(c // SUBLANES)):
        rows = c - SUBLANES * a
        qa = qs3[:, SUBLANES * a:, :].reshape(nch * rows, w)
        ca = cum3[:, SUBLANES * a:, :].reshape(nch * rows, w)
        acc_a = None
        for b in range(SUBLANES):
            wb = wsh[b, :, 0:rows, :].reshape(nch * rows, w)
            vb = vsh[b, :, 0:rows, :].reshape(nch * rows, w)
            dd = qa * jnp.exp2(ca + wb)
            term = _group_dot(dd.astype(BF16), bd) * vb
            acc_a = term if acc_a is None else acc_a + term
        acc_a = acc_a.reshape(nch, rows, w)
        if acc is not None:
            acc_a = acc_a + jnp.concatenate([jnp.zeros((nch, SUBLANES, w), F32), acc], axis=1)
        acc = acc_a
    acc = acc.reshape(tt, w)

    bdmask = bd != 0
    outs = []
    for ci in range(nch):
        r0 = ci * c
        cum_c = cum[r0:r0 + c]
        last = cum_c[c - 1:c]
        qd = (qs[r0:r0 + c] * jnp.exp2(cum_c)).astype(BF16)
        kd = (kk[r0:r0 + c] * jnp.exp2(last - cum_c)).astype(BF16)
        vc = v[r0:r0 + c].astype(BF16)
        dec = jnp.exp2(last)
        o_parts = []
        for gi in range(w // grp):
            sl = slice(gi * grp, (gi + 1) * grp)
            st = st_scr[gi]
            o_parts.append(lax.dot_general(qd[:, sl], st.astype(BF16), (((1,), (1,)), ((), ())),
                                           preferred_element_type=F32))
            upd = lax.dot_general(vc[:, sl], kd[:, sl], (((0,), (0,)), ((), ())),
                                  preferred_element_type=F32)
            st_scr[gi] = st * dec[:, sl] + jnp.where(bdmask, upd, 0.0)
        outs.append(jnp.concatenate(o_parts, axis=1))
    o = acc + jnp.concatenate(outs, axis=0)

    o2 = o * o
    o2_hi = o2.astype(BF16)
    o2_lo = (o2 - o2_hi.astype(F32)).astype(BF16)
    ms = (_group_dot(o2_hi, bd) + _group_dot(o2_lo, bd)) * (1.0 / HG_DIM)
    gv = g_ref[...]
    o = o * lax.rsqrt(ms + EPS) * ng_ref[...] * (gv * jax.nn.sigmoid(gv))
    o_ref[...] = o.astype(o_ref.dtype)


def _split_dot_left(m, x, passes):
    acc = None
    r = x
    for p in range(passes):
        h = r.astype(BF16)
        term = jnp.dot(m, h, preferred_element_type=F32)
        acc = term if acc is None else acc + term
        if p + 1 < passes:
            r = r - h.astype(F32)
    return acc


def _hgrn(rest, lb_logits, norm_g, bsz, seq, layer, tri, bd):
    n = bsz * seq
    w = HG_HEADS * HG_DIM
    grp = bd.shape[0]
    tt = min(HGRN_TT, seq)
    c = min(HGRN_C, tt)
    nt = seq // tt
    kern = functools.partial(_hgrn_kernel, tt=tt, c=c, layer=layer)
    col = lambda j: pl.BlockSpec((tt, w), lambda b, i, j=j: (b * nt + i, j))
    const = lambda shape: pl.BlockSpec(shape, lambda b, i: (0, 0))
    return pl.pallas_call(
        kern,
        grid=(bsz, nt),
        in_specs=[col(0), col(1), col(2), col(3),
                  const(lb_logits.shape), const((1, w)), const((tt, tt)), const((grp, grp))],
        out_specs=pl.BlockSpec((tt, w), lambda b, i: (b * nt + i, 0)),
        out_shape=jax.ShapeDtypeStruct((n, w), BF16),
        scratch_shapes=([pltpu.VMEM((w // grp, grp, grp), F32)]
                        + [pltpu.VMEM((SUBLANES + tt, w), F32)] * 2
                        + [pltpu.VMEM((SUBLANES, tt // c, c, w), F32)] * 2),
        compiler_params=_params(("parallel", "arbitrary")),
        name="hgrn",
    )(rest, rest, rest, rest, lb_logits, norm_g.reshape(1, w), tri, bd)


ROUTE_I1, ROUTE_I2, ROUTE_W1, ROUTE_W2 = N_EXPERTS, N_EXPERTS + 1, N_EXPERTS + 2, N_EXPERTS + 3


def _route(lg):
    lane = lax.broadcasted_iota(jnp.int32, lg.shape, 1)
    neg = jnp.float32(-jnp.inf)
    big = jnp.int32(LANES)
    gmask = jnp.logical_and(lane >= N_EXPERTS, lane < N_EXPERTS + N_GROUPS)
    gl = jnp.where(gmask, lg, neg)
    gmax = jnp.max(gl, axis=1, keepdims=True)
    gidx = jnp.min(jnp.where(gl == gmax, lane, big), axis=1, keepdims=True) - N_EXPERTS
    w_grp = 1.0 / jnp.sum(jnp.where(gmask, jnp.exp(gl - gmax), 0.0), axis=1, keepdims=True)
    in_grp = jnp.logical_and(lane < N_EXPERTS, lane // EXPERTS_PER_GROUP == gidx)
    l1 = jnp.where(in_grp, lg, neg)
    v1 = jnp.max(l1, axis=1, keepdims=True)
    i1 = jnp.min(jnp.where(l1 == v1, lane, big), axis=1, keepdims=True)
    l2 = jnp.where(jnp.logical_and(in_grp, lane != i1), lg, neg)
    v2 = jnp.max(l2, axis=1, keepdims=True)
    i2 = jnp.min(jnp.where(l2 == v2, lane, big), axis=1, keepdims=True)
    e2 = jnp.exp(v2 - v1)
    p1 = 1.0 / (1.0 + e2)
    p2 = e2 * p1
    out = jnp.where(jnp.logical_or(lane == i1, lane == i2), 1.0, 0.0)
    out = jnp.where(lane == ROUTE_I1, i1.astype(F32), out)
    out = jnp.where(lane == ROUTE_I2, i2.astype(F32), out)
    out = jnp.where(lane == ROUTE_W1, p1 * w_grp, out)
    return jnp.where(lane == ROUTE_W2, p2 * w_grp, out)


def _store_token_tiles(ref, x):
    rows, d = x.shape
    assert d == SUBLANES * LANES
    for s in range(SUBLANES):
        ref[pl.ds(s, rows, stride=SUBLANES), :] = x[:, s * LANES:(s + 1) * LANES]


def _load_token_tiles(ref):
    rows = ref.shape[0] // SUBLANES
    return jnp.concatenate([ref[pl.ds(s, rows, stride=SUBLANES), :] for s in range(SUBLANES)],
                           axis=1)


def _merge_kernel(x_ref, ysb_ref, yhg_ref, gsb_ref, ghg_ref, wbs_ref, wbh_ref, wo_ref,
                  ln_ref, wr_hi_ref, wr_lo_ref, br_ref, x1_ref, t_ref, route_ref):
    a = jnp.dot(ysb_ref[...], wbs_ref[...], preferred_element_type=F32)
    b = jnp.dot(yhg_ref[...], wbh_ref[...], preferred_element_type=F32)
    merged = jax.nn.sigmoid(gsb_ref[...]) * a + jax.nn.sigmoid(ghg_ref[...]) * b
    x1 = x_ref[...] + jnp.dot(merged.astype(BF16), wo_ref[...], preferred_element_type=F32)
    x1_ref[...] = x1
    var = jnp.mean(x1 * x1, axis=-1, keepdims=True)
    t = x1 * lax.rsqrt(var + EPS) * ln_ref[...]
    _store_token_tiles(t_ref, t)
    t_hi = t.astype(BF16)
    t_lo = (t - t_hi.astype(F32)).astype(BF16)
    lg = (jnp.dot(t_hi, wr_hi_ref[...], preferred_element_type=F32)
          + jnp.dot(t_lo, wr_hi_ref[...], preferred_element_type=F32)
          + jnp.dot(t_hi, wr_lo_ref[...], preferred_element_type=F32)) + br_ref[...]
    route_ref[...] = _route(lg)


def _merge(x2, y_sb, y_hg, rest, wbs, wbh, wo, ln_g, wr_hi, wr_lo, br):
    n, d = x2.shape
    tm = min(MERGE_TM, n)
    w_sb, w_hg = y_sb.shape[1], y_hg.shape[1]
    gate_blk = (rest.shape[1] - 2 * d) // d
    row = lambda wdt, j=0: pl.BlockSpec((tm, wdt), lambda i, j=j: (i, j))
    const = lambda shape: pl.BlockSpec(shape, lambda i: (0, 0))
    return pl.pallas_call(
        _merge_kernel,
        grid=(n // tm,),
        in_specs=[row(d), row(w_sb), row(w_hg), row(d, gate_blk), row(d, gate_blk + 1),
                  const(wbs.shape), const(wbh.shape), const(wo.shape), const((1, d)),
                  const(wr_hi.shape), const(wr_lo.shape), const((1, LANES))],
        out_specs=[row(d), pl.BlockSpec((tm * SUBLANES, LANES), lambda i: (i, 0)), row(LANES)],
        out_shape=[jax.ShapeDtypeStruct((n, d), F32),
                   jax.ShapeDtypeStruct((n * SUBLANES, LANES), F32),
                   jax.ShapeDtypeStruct((n, LANES), F32)],
        compiler_params=_params(("parallel",)),
        name="merge",
    )(x2, y_sb, y_hg, rest, rest, wbs, wbh, wo, ln_g.reshape(1, d), wr_hi, wr_lo, br)


def _plan_kernel(route_ref, ltri_ref, utri_ref, pos_ref, te_ref, meta_ref, tot_scr, run_scr,
                 *, tile_rows):
    ph = pl.program_id(0)
    i = pl.program_id(1)
    route = route_ref[...]
    tm = route.shape[0]
    lane = lax.broadcasted_iota(jnp.int32, (tm, LANES), 1)
    sel = jnp.where(lane < N_EXPERTS, route, 0.0)

    @pl.when(jnp.logical_and(ph == 0, i == 0))
    def _():
        tot_scr[...] = jnp.zeros_like(tot_scr)

    @pl.when(ph == 0)
    def _():
        tot_scr[...] += jnp.sum(sel, axis=0, keepdims=True)

    @pl.when(ph == 1)
    def _():
        tot = tot_scr[...]
        padded = jnp.ceil(tot * (1.0 / tile_rows)) * tile_rows
        off = _split_dot(jnp.broadcast_to(padded, (SUBLANES, LANES)), utri_ref[...], 3)[0:1]

        @pl.when(i == 0)
        def _():
            run_scr[...] = jnp.zeros_like(run_scr)
            lane1 = lax.broadcasted_iota(jnp.int32, (1, LANES), 1)
            is_e = lane1 < N_EXPERTS
            end = off + padded
            n_valid = (jnp.sum(jnp.where(is_e, padded, 0.0), axis=1, keepdims=True)
                       * (1.0 / tile_rows))
            rows = te_ref.shape[0]
            j = lax.broadcasted_iota(jnp.int32, (rows, LANES), 0).astype(F32)
            start = jnp.minimum(j, n_valid - 1.0) * tile_rows
            lane2 = lax.broadcasted_iota(jnp.int32, (rows, LANES), 1)
            done = jnp.where(jnp.logical_and(lane2 < N_EXPERTS, end <= start), 1.0, 0.0)
            te = jnp.sum(done, axis=1, keepdims=True)
            te_ref[...] = jnp.broadcast_to(te, (rows, LANES)).astype(jnp.int32)
            last_tile = jnp.where(jnp.logical_and(is_e, padded > 0),
                                  end * (1.0 / tile_rows) - 1.0, -1.0)
            meta = jnp.where(lane1 == N_EXPERTS, n_valid, last_tile)
            meta_ref[...] = jnp.broadcast_to(meta, meta_ref.shape).astype(jnp.int32)

        rank = jnp.dot(ltri_ref[...], sel.astype(BF16), preferred_element_type=F32) + run_scr[...]
        dest = off + rank
        lane_f = lane.astype(F32)
        p1 = jnp.sum(jnp.where(lane_f == route[:, ROUTE_I1:ROUTE_I1 + 1], dest, 0.0),
                     axis=1, keepdims=True)
        p2 = jnp.sum(jnp.where(lane_f == route[:, ROUTE_I2:ROUTE_I2 + 1], dest, 0.0),
                     axis=1, keepdims=True)
        pos_ref[...] = jnp.where(lane == 0, p1, jnp.where(lane == 1, p2, 0.0)).astype(jnp.int32)
        run_scr[...] += jnp.sum(sel, axis=0, keepdims=True)


def _plan(route, n_tiles_max):
    n = route.shape[0]
    tm = min(PLAN_TM, n)
    r = np.arange(tm)
    ltri = jnp.asarray((r[None, :] < r[:, None]).astype(np.float32), dtype=BF16)
    e = np.arange(LANES)
    utri = jnp.asarray((e[:, None] < e[None, :]).astype(np.float32), dtype=BF16)
    te_rows = -(-n_tiles_max // SUBLANES) * SUBLANES
    kern = functools.partial(_plan_kernel, tile_rows=MOE_TM)
    return pl.pallas_call(
        kern,
        grid=(2, n // tm),
        in_specs=[pl.BlockSpec((tm, LANES), lambda p, i: (i, 0)),
                  pl.BlockSpec((tm, tm), lambda p, i: (0, 0)),
                  pl.BlockSpec((LANES, LANES), lambda p, i: (0, 0))],
        out_specs=[pl.BlockSpec((tm, LANES), lambda p, i: (i * p, 0)),
                   pl.BlockSpec((te_rows, LANES), lambda p, i: (0, 0)),
                   pl.BlockSpec((SUBLANES, LANES), lambda p, i: (0, 0))],
        out_shape=[jax.ShapeDtypeStruct((n, LANES), jnp.int32),
                   jax.ShapeDtypeStruct((te_rows, LANES), jnp.int32),
                   jax.ShapeDtypeStruct((SUBLANES, LANES), jnp.int32)],
        scratch_shapes=[pltpu.VMEM((1, LANES), F32), pltpu.VMEM((1, LANES), F32)],
        compiler_params=_params(("arbitrary", "arbitrary")),
        name="plan",
    )(route, ltri, utri)


DMA_UNROLL = 8


def _row_copy(src_ref, r, dst_ref, p, sem):
    return pltpu.make_async_copy(src_ref.at[pl.ds(pl.multiple_of(r * SUBLANES, SUBLANES), SUBLANES)],
                                 dst_ref.at[pl.ds(pl.multiple_of(p * SUBLANES, SUBLANES), SUBLANES)],
                                 sem)


def _dispatch_kernel(pos1_ref, pos2_ref, meta_ref, t_ref, xs_ref, zero_scr, sem, *, tm, tile_rows):
    i = pl.program_id(0)
    tile_rows = tile_rows * SUBLANES
    n_tiles = xs_ref.shape[0] // tile_rows

    @pl.when(i == 0)
    def _():
        zero_scr[...] = jnp.zeros_like(zero_scr)
        n_valid = meta_ref[N_EXPERTS]
        clears = [(meta_ref[e], meta_ref[e] >= 0) for e in range(N_EXPERTS)]
        clears += [(n_tiles - 1 - k, n_tiles - 1 - k >= n_valid) for k in range(N_EXPERTS)]

        def clear(tile):
            return pltpu.make_async_copy(
                zero_scr, xs_ref.at[pl.ds(tile * tile_rows, tile_rows)], sem)

        for tile, cond in clears:
            @pl.when(cond)
            def _():
                clear(tile).start()
        for tile, cond in clears:
            @pl.when(cond)
            def _():
                clear(tile).wait()

    base = i * tm

    def start(r, carry):
        _row_copy(t_ref, r, xs_ref, pos1_ref[base + r], sem).start(priority=0)
        _row_copy(t_ref, r, xs_ref, pos2_ref[base + r], sem).start(priority=1)
        return carry

    def wait(r, carry):
        _row_copy(t_ref, r, xs_ref, pos1_ref[base + r], sem).wait()
        _row_copy(t_ref, r, xs_ref, pos2_ref[base + r], sem).wait()
        return carry

    lax.fori_loop(0, tm, start, 0, unroll=DMA_UNROLL)
    lax.fori_loop(0, tm, wait, 0, unroll=DMA_UNROLL)


def _dispatch(pos1, pos2, meta, t, n_rows):
    n = t.shape[0] // SUBLANES
    tm = min(DISPATCH_TM, n)
    kern = functools.partial(_dispatch_kernel, tm=tm, tile_rows=MOE_TM)
    grid_spec = pltpu.PrefetchScalarGridSpec(
        num_scalar_prefetch=3,
        grid=(n // tm,),
        in_specs=[pl.BlockSpec((tm * SUBLANES, LANES), lambda i, *_: (i, 0))],
        out_specs=pl.BlockSpec(memory_space=pl.ANY),
        scratch_shapes=[pltpu.VMEM((MOE_TM * SUBLANES, LANES), F32), pltpu.SemaphoreType.DMA(())],
    )
    return pl.pallas_call(
        kern,
        grid_spec=grid_spec,
        out_shape=jax.ShapeDtypeStruct((n_rows * SUBLANES, LANES), F32),
        compiler_params=_params(("arbitrary",)),
        name="dispatch",
    )(pos1, pos2, meta, t)


def _expert_kernel(te_ref, nv_ref, xs_ref, wg_ref, wu_ref, wd_ref, ys_ref):
    in_use = pl.program_id(0) < nv_ref[0]

    @pl.when(in_use)
    def _():
        x = _load_token_tiles(xs_ref).astype(BF16)
        hg = jnp.dot(x, wg_ref[...].astype(BF16), preferred_element_type=F32)
        hu = jnp.dot(x, wu_ref[...].astype(BF16), preferred_element_type=F32)
        a = (hg * jax.nn.sigmoid(hg) * hu).astype(BF16)
        _store_token_tiles(ys_ref, jnp.dot(a, wd_ref[...].astype(BF16),
                                           preferred_element_type=F32))

    @pl.when(jnp.logical_not(in_use))
    def _():
        ys_ref[...] = jnp.zeros_like(ys_ref)


def _experts(te, nv, xs, wg, wu, wd):
    _, d, de = wg.shape
    assert d == SUBLANES * LANES
    tile = lambda j, te, nv: (j, 0)
    grid_spec = pltpu.PrefetchScalarGridSpec(
        num_scalar_prefetch=2,
        grid=(xs.shape[0] // (MOE_TM * SUBLANES),),
        in_specs=[pl.BlockSpec((MOE_TM * SUBLANES, LANES), tile),
                  pl.BlockSpec((None, d, de), lambda j, te, nv: (te[j], 0, 0)),
                  pl.BlockSpec((None, d, de), lambda j, te, nv: (te[j], 0, 0)),
                  pl.BlockSpec((None, de, d), lambda j, te, nv: (te[j], 0, 0))],
        out_specs=pl.BlockSpec((MOE_TM * SUBLANES, LANES), tile),
    )
    return pl.pallas_call(
        _expert_kernel,
        grid_spec=grid_spec,
        out_shape=jax.ShapeDtypeStruct(xs.shape, F32),
        compiler_params=_params(("arbitrary",)),
        name="experts",
    )(te, nv, xs, wg, wu, wd)


def _combine_kernel(pos1_ref, pos2_ref, ys_ref, x1_ref, route_ref, fg_ref, o_ref, g1, g2, sem,
                    *, tm):
    base = pl.program_id(0) * tm

    def start(r, carry):
        _row_copy(ys_ref, pos1_ref[base + r], g1, r, sem).start(priority=0)
        _row_copy(ys_ref, pos2_ref[base + r], g2, r, sem).start(priority=1)
        return carry

    def wait(r, carry):
        _row_copy(ys_ref, pos1_ref[base + r], g1, r, sem).wait()
        _row_copy(ys_ref, pos2_ref[base + r], g2, r, sem).wait()
        return carry

    lax.fori_loop(0, tm, start, 0, unroll=DMA_UNROLL)
    lax.fori_loop(0, tm, wait, 0, unroll=DMA_UNROLL)
    route = route_ref[...]
    w1 = route[:, ROUTE_W1:ROUTE_W1 + 1]
    w2 = route[:, ROUTE_W2:ROUTE_W2 + 1]
    x2 = x1_ref[...] + (w1 * _load_token_tiles(g1) + w2 * _load_token_tiles(g2))
    var = jnp.mean(x2 * x2, axis=-1, keepdims=True)
    o_ref[...] = x2 * lax.rsqrt(var + EPS) * fg_ref[...]


def _combine(pos1, pos2, ys, x1, route, final_g):
    n, d = x1.shape
    tm = min(DISPATCH_TM, n)
    kern = functools.partial(_combine_kernel, tm=tm)
    grid_spec = pltpu.PrefetchScalarGridSpec(
        num_scalar_prefetch=2,
        grid=(n // tm,),
        in_specs=[pl.BlockSpec(memory_space=pl.ANY),
                  pl.BlockSpec((tm, d), lambda i, *_: (i, 0)),
                  pl.BlockSpec((tm, LANES), lambda i, *_: (i, 0)),
                  pl.BlockSpec((1, d), lambda i, *_: (0, 0))],
        out_specs=pl.BlockSpec((tm, d), lambda i, *_: (i, 0)),
        scratch_shapes=[pltpu.VMEM((tm * SUBLANES, LANES), F32)] * 2 + [pltpu.SemaphoreType.DMA(())],
    )
    return pl.pallas_call(
        kern,
        grid_spec=grid_spec,
        out_shape=jax.ShapeDtypeStruct((n, d), F32),
        compiler_params=_params(("arbitrary",)),
        name="combine",
    )(pos1, pos2, ys, x1, route, final_g.reshape(1, d))


def _moe_sparse(t, route, wg, wu, wd, x1, final_g):
    n = x1.shape[0]
    n_tiles_max = 2 * n // MOE_TM + N_EXPERTS
    pos, te, meta = _plan(route, n_tiles_max)
    pos1, pos2 = pos[:, 0], pos[:, 1]
    xs = _dispatch(pos1, pos2, meta[0, :N_EXPERTS + 1], t, n_tiles_max * MOE_TM)
    ys = _experts(te[:n_tiles_max, 0], meta[0, N_EXPERTS:N_EXPERTS + 1], xs, wg, wu, wd)
    return _combine(pos1, pos2, ys, x1, route, final_g)


def _suffix_ones(t):
    j = np.arange(t)[:, None]
    s = np.arange(t)[None, :]
    return jnp.asarray((j >= s).astype(np.float32), dtype=BF16)


def _causal_bias(t):
    row = np.arange(t)[:, None]
    col = np.arange(t)[None, :]
    diag = np.where(col < row, 0.0, MASK_BIAS).astype(np.float32)
    return jnp.asarray(np.stack([np.zeros_like(diag), diag]))


def _chunk_prefix_ones(tt, c):
    t = np.arange(tt)[:, None]
    j = np.arange(tt)[None, :]
    return jnp.asarray(((j <= t) & (t // c == j // c)).astype(np.float32), dtype=BF16)


def _block_diag_ones(w, blk):
    a = np.arange(w)
    return jnp.asarray((a[:, None] // blk == a[None, :] // blk).astype(np.float32), dtype=BF16)


def kernel(x, ln1_g, w_in, w_branch_sb, w_branch_hg, hg_norm_g, hg_lb_logits, w_out, ln2_g,
           w_router_group, b_router_group, w_router_expert, b_router_expert,
           w_exp_gate, w_exp_up, w_exp_down, final_g):
    bsz, seq, d = x.shape
    depth = w_in.shape[0]
    n = bsz * seq
    sb_width = SB_HEADS * SB_HEAD_DIM
    hg_width = HG_HEADS * HG_DIM

    tri_attn = _suffix_ones(min(ATTN_T, seq))
    bias_attn = _causal_bias(min(ATTN_T, seq))
    tt = min(HGRN_TT, seq)
    tri_hg = _chunk_prefix_ones(tt, min(HGRN_C, tt))
    bd = _block_diag_ones(min(MXU_DIM, hg_width), HG_DIM)

    x2 = x.reshape(n, d)
    for l in range(depth):
        qkv, rest = _inproj(x2, ln1_g[l], w_in[l], sb_width)
        y_sb = _attn(qkv, bsz, seq, tri_attn, bias_attn)
        y_hg = _hgrn(rest, hg_lb_logits, hg_norm_g[l], bsz, seq, l, tri_hg, bd)

        pad = LANES - N_EXPERTS - N_GROUPS
        wr = jnp.concatenate([w_router_expert[l], w_router_group[l],
                              jnp.zeros((d, pad), F32)], axis=1)
        wr_hi = wr.astype(BF16)
        wr_lo = (wr - wr_hi.astype(F32)).astype(BF16)
        br = jnp.concatenate([b_router_expert[l], b_router_group[l],
                              jnp.zeros((pad,), F32)]).reshape(1, LANES)

        last = l == depth - 1
        x1, t, route = _merge(x2, y_sb, y_hg, rest, w_branch_sb[l].astype(BF16),
                              w_branch_hg[l].astype(BF16), w_out[l].astype(BF16), ln2_g[l],
                              wr_hi, wr_lo, br)
        assert last, "final rmsnorm is fused into the last layer's combine kernel"
        x2 = _moe_sparse(t, route, w_exp_gate[l], w_exp_up[l], w_exp_down[l], x1, final_g)
    return x2.reshape(bsz, seq, d)
```

```python
import functools

import jax
import jax.numpy as jnp
import numpy as np
from jax import lax
from jax.experimental import pallas as pl
from jax.experimental.pallas import tpu as pltpu

F32 = jnp.float32
BF16 = jnp.bfloat16

EPS = 1e-6
SB_HEADS = 8
SB_HEAD_DIM = 64
HG_HEADS = 8
HG_DIM = 64
N_GROUPS = 4
EXPERTS_PER_GROUP = 4
N_EXPERTS = N_GROUPS * EXPERTS_PER_GROUP

LANES = 128
SUBLANES = 8
MXU_DIM = 256
LOG2E = 1.4426950408889634
VMEM_LIMIT = 48 * 1024 * 1024

INPROJ_TM = 1024
INPROJ_TN = 512
ATTN_T = MXU_DIM
HGRN_TT = 256
HGRN_C = 32
MERGE_TM = 512
MOE_TM = 512
PLAN_TM = 1024
DISPATCH_TM = 512


def _params(sem):
    return pltpu.CompilerParams(dimension_semantics=sem, vmem_limit_bytes=VMEM_LIMIT)


def _split_dot(x, m, passes):
    acc = None
    r = x
    for p in range(passes):
        h = r.astype(BF16)
        term = jnp.dot(h, m, preferred_element_type=F32)
        acc = term if acc is None else acc + term
        if p + 1 < passes:
            r = r - h.astype(F32)
    return acc


def _inproj_kernel(x_ref, g_ref, w_ref, qkv_ref, rest_ref, h_scr, *, q_scale):
    j = pl.program_id(1)

    @pl.when(j == 0)
    def _():
        x = x_ref[...]
        var = jnp.mean(x * x, axis=-1, keepdims=True)
        h_scr[...] = (x * lax.rsqrt(var + EPS) * g_ref[...]).astype(BF16)

    acc = jnp.dot(h_scr[...], w_ref[...], preferred_element_type=F32)
    qkv_ref[...] = (acc * jnp.where(j == 0, q_scale, 1.0)).astype(BF16)
    rest_ref[...] = acc


def _inproj(x2, ln_g, w_in, sb_width):
    n, d = x2.shape
    cols = w_in.shape[1]
    tm, tn = min(INPROJ_TM, n), INPROJ_TN
    assert sb_width == tn, "q block must be exactly one column tile"
    nq = 3 * sb_width // tn
    nj = cols // tn
    kern = functools.partial(_inproj_kernel, q_scale=SB_HEAD_DIM ** -0.5)
    return pl.pallas_call(
        kern,
        grid=(n // tm, nj),
        in_specs=[
            pl.BlockSpec((tm, d), lambda i, j: (i, 0)),
            pl.BlockSpec((1, d), lambda i, j: (0, 0)),
            pl.BlockSpec((d, tn), lambda i, j: (0, j)),
        ],
        out_specs=[
            pl.BlockSpec((tm, tn), lambda i, j: (i, jnp.minimum(j, nq))),
            pl.BlockSpec((tm, tn), lambda i, j: (i, jnp.maximum(j - nq, 0))),
        ],
        out_shape=[
            jax.ShapeDtypeStruct((n, (nq + 1) * tn), BF16),
            jax.ShapeDtypeStruct((n, cols - nq * tn), F32),
        ],
        scratch_shapes=[pltpu.VMEM((tm, d), BF16)],
        compiler_params=_params(("parallel", "arbitrary")),
        name="inproj",
    )(x2, ln_g.reshape(1, d), w_in)


ATTN_STAGES = 3
MASK_BIAS = -1e30
ATTN_SKIP_LOG2 = 160.0


def _attn_kernel(q_ref, k_ref, v_ref, tri_ref, bias_ref, o_ref,
                 z0, z1, z2, i0, i1, i2, acc_ref, c_ref, *, t, nq):
    zbuf = (z0, z1, z2)
    ibuf = (i0, i1, i2)
    for r in zbuf + ibuf:
        r[...] = jnp.zeros_like(r)
    acc_ref[...] = jnp.zeros_like(acc_ref)
    c_ref[...] = jnp.zeros_like(c_ref)

    lane = lax.broadcasted_iota(jnp.int32, (t, LANES), 1)
    head0 = lane < SB_HEAD_DIM
    nt = (((1,), (1,)), ((), ()))

    def stage_a(qi, kj, slot):
        q = q_ref[pl.ds(pl.multiple_of(qi * t, t), t), :]
        zero = jnp.zeros_like(q)
        q2 = jnp.concatenate([jnp.where(head0, q, zero), jnp.where(head0, zero, q)], axis=0)
        k = k_ref[pl.ds(pl.multiple_of(kj * t, t), t), :]
        z = lax.dot_general(q2, k, nt, preferred_element_type=F32)
        bias = bias_ref[(qi == kj).astype(jnp.int32)]
        zbuf[slot][...] = z * LOG2E + jnp.concatenate([bias, bias], axis=0)

    def stage_b(slot):
        z = zbuf[slot][...]
        p = jnp.maximum(z, 0.0) + jnp.log(1.0 + jnp.exp2(-jnp.abs(z))) * LOG2E
        incl = jnp.dot(p.astype(BF16), tri_ref[...], preferred_element_type=F32)
        ibuf[slot][...] = incl
        return incl[:, 0:1]

    def stage_c(qi, kj, slot):
        first = qi == kj
        incl = ibuf[slot][...]
        c = jnp.where(first, 0.0, c_ref[...])
        a = jnp.exp2(zbuf[slot][...] - incl - c)
        v = v_ref[pl.ds(pl.multiple_of(kj * t, t), t), :]
        pv = jnp.dot(a.astype(BF16), v, preferred_element_type=F32)
        acc = jnp.where(first, pv, acc_ref[...] + pv)
        acc_ref[...] = acc
        c_new = c + incl[:, 0:1]
        c_ref[...] = c_new
        o_ref[pl.ds(pl.multiple_of(qi * t, t), t), :] = (
            jnp.where(head0, acc[0:t], acc[t:2 * t]).astype(o_ref.dtype))
        return c_new

    def block(carry):
        qs, ks, qn, kn, skip_q, drained = carry
        qs, ks = list(qs), list(ks)
        drained = drained + (qn >= nq).astype(jnp.int32)
        for r in range(ATTN_STAGES):
            sc, sb = (r + 1) % ATTN_STAGES, (r + 2) % ATTN_STAGES
            carry_c = stage_c(qs[sc], ks[sc], sc)
            total_b = stage_b(sb)
            carry_b = jnp.where(qs[sb] == ks[sb], 0.0, carry_c) + total_b
            drain = qn >= nq
            qa = jnp.where(drain, 0, qn)
            ka = jnp.where(drain, 0, kn)
            stage_a(qa, ka, r)
            jump = jnp.logical_or(kn == 0, skip_q == qn)
            kn = jnp.where(drain, kn, jnp.where(jump, qn + 1, kn - 1))
            qn = jnp.where(drain, qn, jnp.where(jump, qn + 1, qn))
            skip_q = jnp.where(jnp.min(carry_b) >= ATTN_SKIP_LOG2, qs[sb], skip_q)
            qs[r], ks[r] = qa, ka
        return tuple(qs), tuple(ks), qn, kn, skip_q, drained

    zero = jnp.int32(0)
    init = ((zero,) * ATTN_STAGES, (zero,) * ATTN_STAGES, zero, zero, jnp.int32(-1), zero)
    lax.while_loop(lambda carry: carry[5] < 1, block, init)


def _attn(qkv, bsz, seq, tri, bias):
    n = bsz * seq
    t = min(ATTN_T, seq)
    pairs = SB_HEADS * SB_HEAD_DIM // LANES
    kern = functools.partial(_attn_kernel, t=t, nq=seq // t)
    return pl.pallas_call(
        kern,
        grid=(bsz, pairs),
        in_specs=[
            pl.BlockSpec((seq, LANES), lambda b, p: (b, p)),
            pl.BlockSpec((seq, LANES), lambda b, p: (b, pairs + p)),
            pl.BlockSpec((seq, LANES), lambda b, p: (b, 2 * pairs + p)),
            pl.BlockSpec((t, t), lambda b, p: (0, 0)),
            pl.BlockSpec((2, t, t), lambda b, p: (0, 0, 0)),
        ],
        out_specs=pl.BlockSpec((seq, LANES), lambda b, p: (b, p)),
        out_shape=jax.ShapeDtypeStruct((n, pairs * LANES), BF16),
        scratch_shapes=([pltpu.VMEM((2 * t, t), F32)] * (2 * ATTN_STAGES)
                        + [pltpu.VMEM((2 * t, LANES), F32), pltpu.VMEM((2 * t, 1), F32)]),
        compiler_params=_params(("parallel", "parallel")),
        name="attn",
    )(qkv, qkv, qkv, tri, bias)


def _group_dot(x, bd):
    g = bd.shape[0]
    parts = [jnp.dot(x[:, i:i + g], bd, preferred_element_type=F32)
             for i in range(0, x.shape[1], g)]
    return jnp.concatenate(parts, axis=1)


def _hgrn_kernel(q_ref, f_ref, i_ref, g_ref, lbl_ref, ng_ref, tri_ref, bd_ref, o_ref,
                 st_scr, wpad, vpad, wsh, vsh, *, tt, c, layer):
    ti = pl.program_id(1)
    w = q_ref.shape[1]
    nch = tt // c

    @pl.when(ti == 0)
    def _():
        st_scr[...] = jnp.zeros_like(st_scr)

    lg = lbl_ref[...]
    e = jnp.exp(lg - jnp.max(lg, axis=0, keepdims=True))
    lb = jnp.sum(e[0:layer + 1], axis=0, keepdims=True) / jnp.sum(e, axis=0, keepdims=True)

    f = lb + (1.0 - lb) * jax.nn.sigmoid(f_ref[...])
    kk = 1.0 - f
    qv = q_ref[...]
    qs = qv * jax.nn.sigmoid(qv)
    v = i_ref[...]
    bd = bd_ref[...]
    cum = _split_dot_left(tri_ref[...], jnp.log(f) * LOG2E, 3)
    wk = jnp.log(kk) * LOG2E - cum

    pos = lax.broadcasted_iota(jnp.int32, (tt, 1), 0) % c
    wpad[0:SUBLANES, :] = jnp.zeros((SUBLANES, w), F32)
    vpad[0:SUBLANES, :] = jnp.zeros((SUBLANES, w), F32)
    wpad[SUBLANES:SUBLANES + tt, :] = wk
    vpad[SUBLANES:SUBLANES + tt, :] = v
    for b in range(SUBLANES):
        wb = wpad[SUBLANES - b:SUBLANES - b + tt, :]
        vb = vpad[SUBLANES - b:SUBLANES - b + tt, :]
        if b:
            wb = jnp.where(pos >= b, wb, -jnp.inf)
        wsh[b] = wb.reshape(nch, c, w)
        vsh[b] = vb.reshape(nch, c, w)

    qs3 = qs.reshape(nch, c, w)
    cum3 = cum.reshape(nch, c, w)
    acc = None
    for a in reversed(range(c // SUBLANES)):
        rows = c - SUBLANES * a
        qa = qs3[:, SUBLANES * a:, :].reshape(nch * rows, w)
        ca = cum3[:, SUBLANES * a:, :].reshape(nch * rows, w)
        acc_a = None
        for b in range(SUBLANES):
            wb = wsh[b, :, 0:rows, :].reshape(nch * rows, w)
            vb = vsh[b, :, 0:rows, :].reshape(nch * rows, w)
            dd = qa * jnp.exp2(ca + wb)
            term = _group_dot(dd.astype(BF16), bd) * vb
            acc_a = term if acc_a is None else acc_a + term
        acc_a = acc_a.reshape(nch, rows, w)
        if acc is not None:
            acc_a = acc_a + jnp.concatenate([jnp.zeros((nch, SUBLANES, w), F32), acc], axis=1)
        acc = acc_a
    acc = acc.reshape(tt, w)

    grp = st_scr.shape[1]
    bdmask = bd[0:grp, 0:grp] != 0
    outs = []
    for ci in range(nch):
        r0 = ci * c
        cum_c = cum[r0:r0 + c]
        last = cum_c[c - 1:c]
        qd = (qs[r0:r0 + c] * jnp.exp2(cum_c)).astype(BF16)
        kd = (kk[r0:r0 + c] * jnp.exp2(last - cum_c)).astype(BF16)
        vc = v[r0:r0 + c].astype(BF16)
        dec = jnp.exp2(last)
        o_parts = []
        for gi in range(w // grp):
            sl = slice(gi * grp, (gi + 1) * grp)
            st = st_scr[gi]
            o_parts.append(lax.dot_general(qd[:, sl], st.astype(BF16), (((1,), (1,)), ((), ())),
                                           preferred_element_type=F32))
            upd = lax.dot_general(vc[:, sl], kd[:, sl], (((0,), (0,)), ((), ())),
                                  preferred_element_type=F32)
            st_scr[gi] = st * dec[:, sl] + jnp.where(bdmask, upd, 0.0)
        outs.append(jnp.concatenate(o_parts, axis=1))
    o = acc + jnp.concatenate(outs, axis=0)

    o2 = o * o
    o2_hi = o2.astype(BF16)
    o2_lo = (o2 - o2_hi.astype(F32)).astype(BF16)
    ms = (_group_dot(o2_hi, bd) + _group_dot(o2_lo, bd)) * (1.0 / HG_DIM)
    gv = g_ref[...]
    o = o * lax.rsqrt(ms + EPS) * ng_ref[...] * (gv * jax.nn.sigmoid(gv))
    o_ref[...] = o.astype(o_ref.dtype)


def _split_dot_left(m, x, passes):
    acc = None
    r = x
    for p in range(passes):
        h = r.astype(BF16)
        term = jnp.dot(m, h, preferred_element_type=F32)
        acc = term if acc is None else acc + term
        if p + 1 < passes:
            r = r - h.astype(F32)
    return acc


def _hgrn(rest, lb_logits, norm_g, bsz, seq, layer, tri, bd):
    n = bsz * seq
    w = HG_HEADS * HG_DIM
    grp = bd.shape[0]
    tt = min(HGRN_TT, seq)
    c = min(HGRN_C, tt)
    nt = seq // tt
    kern = functools.partial(_hgrn_kernel, tt=tt, c=c, layer=layer)
    col = lambda j: pl.BlockSpec((tt, w), lambda b, i, j=j: (b * nt + i, j))
    const = lambda shape: pl.BlockSpec(shape, lambda b, i: (0, 0))
    return pl.pallas_call(
        kern,
        grid=(bsz, nt),
        in_specs=[col(0), col(1), col(2), col(3),
                  const(lb_logits.shape), const((1, w)), const((tt, tt)), const((grp, grp))],
        out_specs=pl.BlockSpec((tt, w), lambda b, i: (b * nt + i, 0)),
        out_shape=jax.ShapeDtypeStruct((n, w), BF16),
        scratch_shapes=([pltpu.VMEM((w // LANES, LANES, LANES), F32)]
                        + [pltpu.VMEM((SUBLANES + tt, w), F32)] * 2
                        + [pltpu.VMEM((SUBLANES, tt // c, c, w), F32)] * 2),
        compiler_params=_params(("parallel", "arbitrary")),
        name="hgrn",
    )(rest, rest, rest, rest, lb_logits, norm_g.reshape(1, w), tri, bd)


ROUTE_I1, ROUTE_I2, ROUTE_W1, ROUTE_W2 = N_EXPERTS, N_EXPERTS + 1, N_EXPERTS + 2, N_EXPERTS + 3


def _route(lg):
    lane = lax.broadcasted_iota(jnp.int32, lg.shape, 1)
    neg = jnp.float32(-jnp.inf)
    big = jnp.int32(LANES)
    gmask = jnp.logical_and(lane >= N_EXPERTS, lane < N_EXPERTS + N_GROUPS)
    gl = jnp.where(gmask, lg, neg)
    gmax = jnp.max(gl, axis=1, keepdims=True)
    gidx = jnp.min(jnp.where(gl == gmax, lane, big), axis=1, keepdims=True) - N_EXPERTS
    w_grp = 1.0 / jnp.sum(jnp.where(gmask, jnp.exp(gl - gmax), 0.0), axis=1, keepdims=True)
    in_grp = jnp.logical_and(lane < N_EXPERTS, lane // EXPERTS_PER_GROUP == gidx)
    l1 = jnp.where(in_grp, lg, neg)
    v1 = jnp.max(l1, axis=1, keepdims=True)
    i1 = jnp.min(jnp.where(l1 == v1, lane, big), axis=1, keepdims=True)
    l2 = jnp.where(jnp.logical_and(in_grp, lane != i1), lg, neg)
    v2 = jnp.max(l2, axis=1, keepdims=True)
    i2 = jnp.min(jnp.where(l2 == v2, lane, big), axis=1, keepdims=True)
    e2 = jnp.exp(v2 - v1)
    p1 = 1.0 / (1.0 + e2)
    p2 = e2 * p1
    out = jnp.where(jnp.logical_or(lane == i1, lane == i2), 1.0, 0.0)
    out = jnp.where(lane == ROUTE_I1, i1.astype(F32), out)
    out = jnp.where(lane == ROUTE_I2, i2.astype(F32), out)
    out = jnp.where(lane == ROUTE_W1, p1 * w_grp, out)
    return jnp.where(lane == ROUTE_W2, p2 * w_grp, out)


def _store_token_tiles(ref, x):
    rows, d = x.shape
    assert d == SUBLANES * LANES
    for s in range(SUBLANES):
        ref[pl.ds(s, rows, stride=SUBLANES), :] = x[:, s * LANES:(s + 1) * LANES]


def _load_token_tiles(ref):
    rows = ref.shape[0] // SUBLANES
    return jnp.concatenate([ref[pl.ds(s, rows, stride=SUBLANES), :] for s in range(SUBLANES)],
                           axis=1)


def _merge_kernel(x_ref, ysb_ref, yhg_ref, gsb_ref, ghg_ref, wbs_ref, wbh_ref, wo_ref,
                  ln_ref, wr_hi_ref, wr_lo_ref, br_ref, x1_ref, t_ref, route_ref):
    a = jnp.dot(ysb_ref[...], wbs_ref[...], preferred_element_type=F32)
    b = jnp.dot(yhg_ref[...], wbh_ref[...], preferred_element_type=F32)
    merged = jax.nn.sigmoid(gsb_ref[...]) * a + jax.nn.sigmoid(ghg_ref[...]) * b
    x1 = x_ref[...] + jnp.dot(merged.astype(BF16), wo_ref[...], preferred_element_type=F32)
    x1_ref[...] = x1
    var = jnp.mean(x1 * x1, axis=-1, keepdims=True)
    t = x1 * lax.rsqrt(var + EPS) * ln_ref[...]
    _store_token_tiles(t_ref, t)
    t_hi = t.astype(BF16)
    t_lo = (t - t_hi.astype(F32)).astype(BF16)
    lg = (jnp.dot(t_hi, wr_hi_ref[...], preferred_element_type=F32)
          + jnp.dot(t_lo, wr_hi_ref[...], preferred_element_type=F32)
          + jnp.dot(t_hi, wr_lo_ref[...], preferred_element_type=F32)) + br_ref[...]
    route_ref[...] = _route(lg)


def _merge(x2, y_sb, y_hg, rest, wbs, wbh, wo, ln_g, wr_hi, wr_lo, br):
    n, d = x2.shape
    tm = min(MERGE_TM, n)
    w_sb, w_hg = y_sb.shape[1], y_hg.shape[1]
    gate_blk = (rest.shape[1] - 2 * d) // d
    row = lambda wdt, j=0: pl.BlockSpec((tm, wdt), lambda i, j=j: (i, j))
    const = lambda shape: pl.BlockSpec(shape, lambda i: (0, 0))
    return pl.pallas_call(
        _merge_kernel,
        grid=(n // tm,),
        in_specs=[row(d), row(w_sb), row(w_hg), row(d, gate_blk), row(d, gate_blk + 1),
                  const(wbs.shape), const(wbh.shape), const(wo.shape), const((1, d)),
                  const(wr_hi.shape), const(wr_lo.shape), const((1, LANES))],
        out_specs=[row(d), pl.BlockSpec((tm * SUBLANES, LANES), lambda i: (i, 0)), row(LANES)],
        out_shape=[jax.ShapeDtypeStruct((n, d), F32),
                   jax.ShapeDtypeStruct((n * SUBLANES, LANES), F32),
                   jax.ShapeDtypeStruct((n, LANES), F32)],
        compiler_params=_params(("parallel",)),
        name="merge",
    )(x2, y_sb, y_hg, rest, rest, wbs, wbh, wo, ln_g.reshape(1, d), wr_hi, wr_lo, br)


def _plan_kernel(route_ref, ltri_ref, utri_ref, pos_ref, te_ref, meta_ref, tot_scr, run_scr,
                 *, tile_rows):
    ph = pl.program_id(0)
    i = pl.program_id(1)
    route = route_ref[...]
    tm = route.shape[0]
    lane = lax.broadcasted_iota(jnp.int32, (tm, LANES), 1)
    sel = jnp.where(lane < N_EXPERTS, route, 0.0)

    @pl.when(jnp.logical_and(ph == 0, i == 0))
    def _():
        tot_scr[...] = jnp.zeros_like(tot_scr)

    @pl.when(ph == 0)
    def _():
        tot_scr[...] += jnp.sum(sel, axis=0, keepdims=True)

    @pl.when(ph == 1)
    def _():
        tot = tot_scr[...]
        padded = jnp.ceil(tot * (1.0 / tile_rows)) * tile_rows
        off = _split_dot(jnp.broadcast_to(padded, (SUBLANES, LANES)), utri_ref[...], 3)[0:1]

        @pl.when(i == 0)
        def _():
            run_scr[...] = jnp.zeros_like(run_scr)
            lane1 = lax.broadcasted_iota(jnp.int32, (1, LANES), 1)
            is_e = lane1 < N_EXPERTS
            end = off + padded
            n_valid = (jnp.sum(jnp.where(is_e, padded, 0.0), axis=1, keepdims=True)
                       * (1.0 / tile_rows))
            rows = te_ref.shape[0]
            j = lax.broadcasted_iota(jnp.int32, (rows, LANES), 0).astype(F32)
            start = jnp.minimum(j, n_valid - 1.0) * tile_rows
            lane2 = lax.broadcasted_iota(jnp.int32, (rows, LANES), 1)
            done = jnp.where(jnp.logical_and(lane2 < N_EXPERTS, end <= start), 1.0, 0.0)
            te = jnp.sum(done, axis=1, keepdims=True)
            te_ref[...] = jnp.broadcast_to(te, (rows, LANES)).astype(jnp.int32)
            last_tile = jnp.where(jnp.logical_and(is_e, padded > 0),
                                  end * (1.0 / tile_rows) - 1.0, -1.0)
            meta = jnp.where(lane1 == N_EXPERTS, n_valid, last_tile)
            meta_ref[...] = jnp.broadcast_to(meta, meta_ref.shape).astype(jnp.int32)

        rank = jnp.dot(ltri_ref[...], sel.astype(BF16), preferred_element_type=F32) + run_scr[...]
        dest = off + rank
        lane_f = lane.astype(F32)
        p1 = jnp.sum(jnp.where(lane_f == route[:, ROUTE_I1:ROUTE_I1 + 1], dest, 0.0),
                     axis=1, keepdims=True)
        p2 = jnp.sum(jnp.where(lane_f == route[:, ROUTE_I2:ROUTE_I2 + 1], dest, 0.0),
                     axis=1, keepdims=True)
        pos_ref[...] = jnp.where(lane == 0, p1, jnp.where(lane == 1, p2, 0.0)).astype(jnp.int32)
        run_scr[...] += jnp.sum(sel, axis=0, keepdims=True)


def _plan(route, n_tiles_max):
    n = route.shape[0]
    tm = min(PLAN_TM, n)
    r = np.arange(tm)
    ltri = jnp.asarray((r[None, :] < r[:, None]).astype(np.float32), dtype=BF16)
    e = np.arange(LANES)
    utri = jnp.asarray((e[:, None] < e[None, :]).astype(np.float32), dtype=BF16)
    te_rows = -(-n_tiles_max // SUBLANES) * SUBLANES
    kern = functools.partial(_plan_kernel, tile_rows=MOE_TM)
    return pl.pallas_call(
        kern,
        grid=(2, n // tm),
        in_specs=[pl.BlockSpec((tm, LANES), lambda p, i: (i, 0)),
                  pl.BlockSpec((tm, tm), lambda p, i: (0, 0)),
                  pl.BlockSpec((LANES, LANES), lambda p, i: (0, 0))],
        out_specs=[pl.BlockSpec((tm, LANES), lambda p, i: (i * p, 0)),
                   pl.BlockSpec((te_rows, LANES), lambda p, i: (0, 0)),
                   pl.BlockSpec((SUBLANES, LANES), lambda p, i: (0, 0))],
        out_shape=[jax.ShapeDtypeStruct((n, LANES), jnp.int32),
                   jax.ShapeDtypeStruct((te_rows, LANES), jnp.int32),
                   jax.ShapeDtypeStruct((SUBLANES, LANES), jnp.int32)],
        scratch_shapes=[pltpu.VMEM((1, LANES), F32), pltpu.VMEM((1, LANES), F32)],
        compiler_params=_params(("arbitrary", "arbitrary")),
        name="plan",
    )(route, ltri, utri)


DMA_UNROLL = 8


def _row_copy(src_ref, r, dst_ref, p, sem):
    return pltpu.make_async_copy(src_ref.at[pl.ds(pl.multiple_of(r * SUBLANES, SUBLANES), SUBLANES)],
                                 dst_ref.at[pl.ds(pl.multiple_of(p * SUBLANES, SUBLANES), SUBLANES)],
                                 sem)


def _dispatch_kernel(pos1_ref, pos2_ref, meta_ref, t_ref, xs_ref, zero_scr, sem, *, tm, tile_rows):
    i = pl.program_id(0)
    tile_rows = tile_rows * SUBLANES
    n_tiles = xs_ref.shape[0] // tile_rows

    @pl.when(i == 0)
    def _():
        zero_scr[...] = jnp.zeros_like(zero_scr)
        n_valid = meta_ref[N_EXPERTS]
        clears = [(meta_ref[e], meta_ref[e] >= 0) for e in range(N_EXPERTS)]
        clears += [(n_tiles - 1 - k, n_tiles - 1 - k >= n_valid) for k in range(N_EXPERTS)]

        def clear(tile):
            return pltpu.make_async_copy(
                zero_scr, xs_ref.at[pl.ds(tile * tile_rows, tile_rows)], sem)

        for tile, cond in clears:
            @pl.when(cond)
            def _():
                clear(tile).start()
        for tile, cond in clears:
            @pl.when(cond)
            def _():
                clear(tile).wait()

    base = i * tm

    def start(r, carry):
        _row_copy(t_ref, r, xs_ref, pos1_ref[base + r], sem).start(priority=0)
        _row_copy(t_ref, r, xs_ref, pos2_ref[base + r], sem).start(priority=1)
        return carry

    def wait(r, carry):
        _row_copy(t_ref, r, xs_ref, pos1_ref[base + r], sem).wait()
        _row_copy(t_ref, r, xs_ref, pos2_ref[base + r], sem).wait()
        return carry

    lax.fori_loop(0, tm, start, 0, unroll=DMA_UNROLL)
    lax.fori_loop(0, tm, wait, 0, unroll=DMA_UNROLL)


def _dispatch(pos1, pos2, meta, t, n_rows):
    n = t.shape[0] // SUBLANES
    tm = min(DISPATCH_TM, n)
    kern = functools.partial(_dispatch_kernel, tm=tm, tile_rows=MOE_TM)
    grid_spec = pltpu.PrefetchScalarGridSpec(
        num_scalar_prefetch=3,
        grid=(n // tm,),
        in_specs=[pl.BlockSpec((tm * SUBLANES, LANES), lambda i, *_: (i, 0))],
        out_specs=pl.BlockSpec(memory_space=pl.ANY),
        scratch_shapes=[pltpu.VMEM((MOE_TM * SUBLANES, LANES), F32), pltpu.SemaphoreType.DMA(())],
    )
    return pl.pallas_call(
        kern,
        grid_spec=grid_spec,
        out_shape=jax.ShapeDtypeStruct((n_rows * SUBLANES, LANES), F32),
        compiler_params=_params(("arbitrary",)),
        name="dispatch",
    )(pos1, pos2, meta, t)


def _expert_kernel(te_ref, nv_ref, xs_ref, wg_ref, wu_ref, wd_ref, ys_ref):
    in_use = pl.program_id(0) < nv_ref[0]

    @pl.when(in_use)
    def _():
        x = _load_token_tiles(xs_ref).astype(BF16)
        hg = jnp.dot(x, wg_ref[...].astype(BF16), preferred_element_type=F32)
        hu = jnp.dot(x, wu_ref[...].astype(BF16), preferred_element_type=F32)
        a = (hg * jax.nn.sigmoid(hg) * hu).astype(BF16)
        _store_token_tiles(ys_ref, jnp.dot(a, wd_ref[...].astype(BF16),
                                           preferred_element_type=F32))

    @pl.when(jnp.logical_not(in_use))
    def _():
        ys_ref[...] = jnp.zeros_like(ys_ref)


def _experts(te, nv, xs, wg, wu, wd):
    _, d, de = wg.shape
    assert d == SUBLANES * LANES
    tile = lambda j, te, nv: (j, 0)
    grid_spec = pltpu.PrefetchScalarGridSpec(
        num_scalar_prefetch=2,
        grid=(xs.shape[0] // (MOE_TM * SUBLANES),),
        in_specs=[pl.BlockSpec((MOE_TM * SUBLANES, LANES), tile),
                  pl.BlockSpec((None, d, de), lambda j, te, nv: (te[j], 0, 0)),
                  pl.BlockSpec((None, d, de), lambda j, te, nv: (te[j], 0, 0)),
                  pl.BlockSpec((None, de, d), lambda j, te, nv: (te[j], 0, 0))],
        out_specs=pl.BlockSpec((MOE_TM * SUBLANES, LANES), tile),
    )
    return pl.pallas_call(
        _expert_kernel,
        grid_spec=grid_spec,
        out_shape=jax.ShapeDtypeStruct(xs.shape, F32),
        compiler_params=_params(("arbitrary",)),
        name="experts",
    )(te, nv, xs, wg, wu, wd)


def _combine_kernel(pos1_ref, pos2_ref, ys_ref, x1_ref, route_ref, fg_ref, o_ref, g1, g2, sem,
                    *, tm):
    i = pl.program_id(0)
    slot = i % 2

    def gather(step, buf, wait):
        base = step * tm

        def body(r, carry):
            c1 = _row_copy(ys_ref, pos1_ref[base + r], g1.at[buf], r, sem.at[buf])
            c2 = _row_copy(ys_ref, pos2_ref[base + r], g2.at[buf], r, sem.at[buf])
            if wait:
                c1.wait()
                c2.wait()
            else:
                c1.start(priority=0)
                c2.start(priority=1)
            return carry

        lax.fori_loop(0, tm, body, 0, unroll=DMA_UNROLL)

    @pl.when(i == 0)
    def _():
        gather(0, 0, wait=False)

    @pl.when(i + 1 < pl.num_programs(0))
    def _():
        gather(i + 1, 1 - slot, wait=False)

    gather(i, slot, wait=True)
    route = route_ref[...]
    w1 = route[:, ROUTE_W1:ROUTE_W1 + 1]
    w2 = route[:, ROUTE_W2:ROUTE_W2 + 1]
    x2 = x1_ref[...] + (w1 * _load_token_tiles(g1.at[slot]) + w2 * _load_token_tiles(g2.at[slot]))
    var = jnp.mean(x2 * x2, axis=-1, keepdims=True)
    o_ref[...] = x2 * lax.rsqrt(var + EPS) * fg_ref[...]


def _combine(pos1, pos2, ys, x1, route, final_g):
    n, d = x1.shape
    tm = min(DISPATCH_TM, n)
    kern = functools.partial(_combine_kernel, tm=tm)
    grid_spec = pltpu.PrefetchScalarGridSpec(
        num_scalar_prefetch=2,
        grid=(n // tm,),
        in_specs=[pl.BlockSpec(memory_space=pl.ANY),
                  pl.BlockSpec((tm, d), lambda i, *_: (i, 0)),
                  pl.BlockSpec((tm, LANES), lambda i, *_: (i, 0)),
                  pl.BlockSpec((1, d), lambda i, *_: (0, 0))],
        out_specs=pl.BlockSpec((tm, d), lambda i, *_: (i, 0)),
        scratch_shapes=([pltpu.VMEM((2, tm * SUBLANES, LANES), F32)] * 2
                        + [pltpu.SemaphoreType.DMA((2,))]),
    )
    return pl.pallas_call(
        kern,
        grid_spec=grid_spec,
        out_shape=jax.ShapeDtypeStruct((n, d), F32),
        compiler_params=_params(("arbitrary",)),
        name="combine",
    )(pos1, pos2, ys, x1, route, final_g.reshape(1, d))


def _moe_sparse(t, route, wg, wu, wd, x1, final_g):
    n = x1.shape[0]
    n_tiles_max = 2 * n // MOE_TM + N_EXPERTS
    pos, te, meta = _plan(route, n_tiles_max)
    pos1, pos2 = pos[:, 0], pos[:, 1]
    xs = _dispatch(pos1, pos2, meta[0, :N_EXPERTS + 1], t, n_tiles_max * MOE_TM)
    ys = _experts(te[:n_tiles_max, 0], meta[0, N_EXPERTS:N_EXPERTS + 1], xs, wg, wu, wd)
    return _combine(pos1, pos2, ys, x1, route, final_g)


def _suffix_ones(t):
    j = np.arange(t)[:, None]
    s = np.arange(t)[None, :]
    return jnp.asarray((j >= s).astype(np.float32), dtype=BF16)


def _causal_bias(t):
    row = np.arange(t)[:, None]
    col = np.arange(t)[None, :]
    diag = np.where(col < row, 0.0, MASK_BIAS).astype(np.float32)
    return jnp.asarray(np.stack([np.zeros_like(diag), diag]))


def _chunk_prefix_ones(tt, c):
    t = np.arange(tt)[:, None]
    j = np.arange(tt)[None, :]
    return jnp.asarray(((j <= t) & (t // c == j // c)).astype(np.float32), dtype=BF16)


def _block_diag_ones(w, blk):
    a = np.arange(w)
    return jnp.asarray((a[:, None] // blk == a[None, :] // blk).astype(np.float32), dtype=BF16)


def kernel(x, ln1_g, w_in, w_branch_sb, w_branch_hg, hg_norm_g, hg_lb_logits, w_out, ln2_g,
           w_router_group, b_router_group, w_router_expert, b_router_expert,
           w_exp_gate, w_exp_up, w_exp_down, final_g):
    bsz, seq, d = x.shape
    depth = w_in.shape[0]
    n = bsz * seq
    sb_width = SB_HEADS * SB_HEAD_DIM
    hg_width = HG_HEADS * HG_DIM

    tri_attn = _suffix_ones(min(ATTN_T, seq))
    bias_attn = _causal_bias(min(ATTN_T, seq))
    tt = min(HGRN_TT, seq)
    tri_hg = _chunk_prefix_ones(tt, min(HGRN_C, tt))
    bd = _block_diag_ones(min(MXU_DIM, hg_width), HG_DIM)

    x2 = x.reshape(n, d)
    for l in range(depth):
        qkv, rest = _inproj(x2, ln1_g[l], w_in[l].astype(BF16), sb_width)
        y_sb = _attn(qkv, bsz, seq, tri_attn, bias_attn)
        y_hg = _hgrn(rest, hg_lb_logits, hg_norm_g[l], bsz, seq, l, tri_hg, bd)

        pad = LANES - N_EXPERTS - N_GROUPS
        wr = jnp.concatenate([w_router_expert[l], w_router_group[l],
                              jnp.zeros((d, pad), F32)], axis=1)
        wr_hi = wr.astype(BF16)
        wr_lo = (wr - wr_hi.astype(F32)).astype(BF16)
        br = jnp.concatenate([b_router_expert[l], b_router_group[l],
                              jnp.zeros((pad,), F32)]).reshape(1, LANES)

        last = l == depth - 1
        x1, t, route = _merge(x2, y_sb, y_hg, rest, w_branch_sb[l].astype(BF16),
                              w_branch_hg[l].astype(BF16), w_out[l].astype(BF16), ln2_g[l],
                              wr_hi, wr_lo, br)
        assert last, "final rmsnorm is fused into the last layer's combine kernel"
        x2 = _moe_sparse(t, route, w_exp_gate[l], w_exp_up[l], w_exp_down[l], x1, final_g)
    return x2.reshape(bsz, seq, d)
```

```python
import functools

import jax
import jax.numpy as jnp
import numpy as np
from jax import lax
from jax.experimental import pallas as pl
from jax.experimental.pallas import tpu as pltpu

F32 = jnp.float32
BF16 = jnp.bfloat16

EPS = 1e-6
SB_HEADS = 8
SB_HEAD_DIM = 64
HG_HEADS = 8
HG_DIM = 64
N_GROUPS = 4
EXPERTS_PER_GROUP = 4
N_EXPERTS = N_GROUPS * EXPERTS_PER_GROUP

LANES = 128
SUBLANES = 8
MXU_DIM = 256
LOG2E = 1.4426950408889634
VMEM_LIMIT = 48 * 1024 * 1024

INPROJ_TM = 2048
INPROJ_TN = 512
ATTN_T = MXU_DIM
HGRN_TT = 256
HGRN_C = 32
MERGE_TM = 512
MOE_TM = 512
PLAN_TM = 1024
DISPATCH_TM = 512


def _params(sem):
    return pltpu.CompilerParams(dimension_semantics=sem, vmem_limit_bytes=VMEM_LIMIT)


def _split_dot(x, m, passes):
    acc = None
    r = x
    for p in range(passes):
        h = r.astype(BF16)
        term = jnp.dot(h, m, preferred_element_type=F32)
        acc = term if acc is None else acc + term
        if p + 1 < passes:
            r = r - h.astype(F32)
    return acc


def _inproj_kernel(x_ref, g_ref, w_ref, qkv_ref, rest_ref, h_scr, *, q_scale):
    j = pl.program_id(1)

    @pl.when(j == 0)
    def _():
        x = x_ref[...]
        var = jnp.mean(x * x, axis=-1, keepdims=True)
        h_scr[...] = (x * lax.rsqrt(var + EPS) * g_ref[...]).astype(BF16)

    acc = jnp.dot(h_scr[...], w_ref[...], preferred_element_type=F32)
    qkv_ref[...] = (acc * jnp.where(j == 0, q_scale, 1.0)).astype(BF16)
    rest_ref[...] = acc


def _inproj(x2, ln_g, w_in, sb_width):
    n, d = x2.shape
    cols = w_in.shape[1]
    tm, tn = min(INPROJ_TM, n), INPROJ_TN
    assert sb_width == tn, "q block must be exactly one column tile"
    nq = 3 * sb_width // tn
    nj = cols // tn
    kern = functools.partial(_inproj_kernel, q_scale=SB_HEAD_DIM ** -0.5)
    return pl.pallas_call(
        kern,
        grid=(n // tm, nj),
        in_specs=[
            pl.BlockSpec((tm, d), lambda i, j: (i, 0)),
            pl.BlockSpec((1, d), lambda i, j: (0, 0)),
            pl.BlockSpec((d, tn), lambda i, j: (0, j)),
        ],
        out_specs=[
            pl.BlockSpec((tm, tn), lambda i, j: (i, jnp.minimum(j, nq))),
            pl.BlockSpec((tm, tn), lambda i, j: (i, jnp.maximum(j - nq, 0))),
        ],
        out_shape=[
            jax.ShapeDtypeStruct((n, (nq + 1) * tn), BF16),
            jax.ShapeDtypeStruct((n, cols - nq * tn), F32),
        ],
        scratch_shapes=[pltpu.VMEM((tm, d), BF16)],
        compiler_params=_params(("parallel", "arbitrary")),
        name="inproj",
    )(x2, ln_g.reshape(1, d), w_in)


ATTN_STAGES = 3
ATTN_STREAMS = 2
MASK_BIAS = -1e30
ATTN_SKIP_LOG2 = 160.0


def _attn_kernel(q_ref, k_ref, v_ref, tri_ref, bias_ref, o_ref,
                 z0, z1, z2, i0, i1, i2, acc_ref, c_ref, *, t, nq):
    zbuf = (z0, z1, z2)
    ibuf = (i0, i1, i2)
    for r in zbuf + ibuf:
        r[...] = jnp.zeros_like(r)
    acc_ref[...] = jnp.zeros_like(acc_ref)
    c_ref[...] = jnp.zeros_like(c_ref)

    lane = lax.broadcasted_iota(jnp.int32, (t, LANES), 1)
    head0 = lane < SB_HEAD_DIM
    nt = (((1,), (1,)), ((), ()))

    def stage_a(qi, kj, slot):
        q = q_ref[pl.ds(pl.multiple_of(qi * t, t), t), :]
        zero = jnp.zeros_like(q)
        q2 = jnp.concatenate([jnp.where(head0, q, zero), jnp.where(head0, zero, q)], axis=0)
        k = k_ref[pl.ds(pl.multiple_of(kj * t, t), t), :]
        z = lax.dot_general(q2, k, nt, preferred_element_type=F32)
        bias = bias_ref[(qi == kj).astype(jnp.int32)]
        zbuf[slot][...] = z * LOG2E + jnp.concatenate([bias, bias], axis=0)

    def stage_b(slot):
        z = zbuf[slot][...]
        p = jnp.maximum(z, 0.0) + jnp.log(1.0 + jnp.exp2(-jnp.abs(z))) * LOG2E
        incl = jnp.dot(p.astype(BF16), tri_ref[...], preferred_element_type=F32)
        ibuf[slot][...] = incl
        return incl[:, 0:1]

    def stage_c(qi, kj, slot, stream):
        first = qi == kj
        incl = ibuf[slot][...]
        c = jnp.where(first, 0.0, c_ref[stream])
        a = jnp.exp2(zbuf[slot][...] - incl - c)
        v = v_ref[pl.ds(pl.multiple_of(kj * t, t), t), :]
        pv = jnp.dot(a.astype(BF16), v, preferred_element_type=F32)
        acc = jnp.where(first, pv, acc_ref[stream] + pv)
        acc_ref[stream] = acc
        c_ref[stream] = c + incl[:, 0:1]
        o_ref[pl.ds(pl.multiple_of(qi * t, t), t), :] = (
            jnp.where(head0, acc[0:t], acc[t:2 * t]).astype(o_ref.dtype))

    def block(carry):
        qs, ks, qn, kn, skip_q, drained = carry
        qs, ks, qn, kn, skip_q = list(qs), list(ks), list(qn), list(kn), list(skip_q)
        all_done = qn[0] >= nq
        for p in range(1, ATTN_STREAMS):
            all_done = jnp.logical_and(all_done, qn[p] >= nq)
        drained = drained + all_done.astype(jnp.int32)
        for r in range(ATTN_STAGES * ATTN_STREAMS):
            sa, sc, sb = r % ATTN_STAGES, (r + 1) % ATTN_STAGES, (r + 2) % ATTN_STAGES
            pa, pb = r % ATTN_STREAMS, (r - 1) % ATTN_STREAMS
            stage_c(qs[sc], ks[sc], sc, pa)
            total_b = stage_b(sb)
            carry_b = jnp.where(qs[sb] == ks[sb], 0.0, c_ref[pb]) + total_b
            skip = skip_q[pa] == qn[pa]
            q_cur = jnp.where(skip, qn[pa] + ATTN_STREAMS, qn[pa])
            k_cur = jnp.where(skip, qn[pa] + ATTN_STREAMS, kn[pa])
            drain = q_cur >= nq
            qa = jnp.where(drain, 0, q_cur)
            ka = jnp.where(drain, 0, k_cur)
            stage_a(qa, ka, sa)
            last = k_cur == 0
            qn[pa] = jnp.where(jnp.logical_and(last, jnp.logical_not(drain)),
                               q_cur + ATTN_STREAMS, q_cur)
            kn[pa] = jnp.where(drain, k_cur, jnp.where(last, q_cur + ATTN_STREAMS, k_cur - 1))
            skip_q[pb] = jnp.where(jnp.min(carry_b) >= ATTN_SKIP_LOG2, qs[sb], skip_q[pb])
            qs[sa], ks[sa] = qa, ka
        return tuple(qs), tuple(ks), tuple(qn), tuple(kn), tuple(skip_q), drained

    zero = jnp.int32(0)
    first_q = tuple(jnp.int32(p) for p in range(ATTN_STREAMS))
    init = ((zero,) * ATTN_STAGES, (zero,) * ATTN_STAGES, first_q, first_q,
            (jnp.int32(-1),) * ATTN_STREAMS, zero)
    lax.while_loop(lambda carry: carry[5] < 1, block, init)


def _attn(qkv, bsz, seq, tri, bias):
    n = bsz * seq
    t = min(ATTN_T, seq)
    pairs = SB_HEADS * SB_HEAD_DIM // LANES
    kern = functools.partial(_attn_kernel, t=t, nq=seq // t)
    return pl.pallas_call(
        kern,
        grid=(bsz, pairs),
        in_specs=[
            pl.BlockSpec((seq, LANES), lambda b, p: (b, p)),
            pl.BlockSpec((seq, LANES), lambda b, p: (b, pairs + p)),
            pl.BlockSpec((seq, LANES), lambda b, p: (b, 2 * pairs + p)),
            pl.BlockSpec((t, t), lambda b, p: (0, 0)),
            pl.BlockSpec((2, t, t), lambda b, p: (0, 0, 0)),
        ],
        out_specs=pl.BlockSpec((seq, LANES), lambda b, p: (b, p)),
        out_shape=jax.ShapeDtypeStruct((n, pairs * LANES), BF16),
        scratch_shapes=([pltpu.VMEM((2 * t, t), F32)] * (2 * ATTN_STAGES)
                        + [pltpu.VMEM((ATTN_STREAMS, 2 * t, LANES), F32),
                           pltpu.VMEM((ATTN_STREAMS, 2 * t, 1), F32)]),
        compiler_params=_params(("parallel", "parallel")),
        name="attn",
    )(qkv, qkv, qkv, tri, bias)


def _group_dot(x, bd):
    g = bd.shape[0]
    parts = [jnp.dot(x[:, i:i + g], bd, preferred_element_type=F32)
             for i in range(0, x.shape[1], g)]
    return jnp.concatenate(parts, axis=1)


def _hgrn_kernel(q_ref, f_ref, i_ref, g_ref, lbl_ref, ng_ref, tri_ref, bd_ref, o_ref,
                 st_scr, wpad, vpad, wsh, vsh, *, tt, c, layer):
    ti = pl.program_id(1)
    w = q_ref.shape[1]
    nch = tt // c

    @pl.when(ti == 0)
    def _():
        st_scr[...] = jnp.zeros_like(st_scr)

    lg = lbl_ref[...]
    e = jnp.exp(lg - jnp.max(lg, axis=0, keepdims=True))
    lb = jnp.sum(e[0:layer + 1], axis=0, keepdims=True) / jnp.sum(e, axis=0, keepdims=True)

    f = lb + (1.0 - lb) * jax.nn.sigmoid(f_ref[...])
    kk = 1.0 - f
    qv = q_ref[...]
    qs = qv * jax.nn.sigmoid(qv)
    v = i_ref[...]
    bd = bd_ref[...]
    cum = _split_dot_left(tri_ref[...], jnp.log(f) * LOG2E, 3)
    wk = jnp.log(kk) * LOG2E - cum

    pos = lax.broadcasted_iota(jnp.int32, (tt, 1), 0) % c
    wpad[0:SUBLANES, :] = jnp.zeros((SUBLANES, w), F32)
    vpad[0:SUBLANES, :] = jnp.zeros((SUBLANES, w), F32)
    wpad[SUBLANES:SUBLANES + tt, :] = wk
    vpad[SUBLANES:SUBLANES + tt, :] = v
    for b in range(SUBLANES):
        wb = wpad[SUBLANES - b:SUBLANES - b + tt, :]
        vb = vpad[SUBLANES - b:SUBLANES - b + tt, :]
        if b:
            wb = jnp.where(pos >= b, wb, -jnp.inf)
        wsh[b] = wb.reshape(nch, c, w)
        vsh[b] = vb.reshape(nch, c, w)

    qs3 = qs.reshape(nch, c, w)
    cum3 = cum.reshape(nch, c, w)
    acc = None
    for a in reversed(range(c // SUBLANES)):
        rows = c - SUBLANES * a
        qa = qs3[:, SUBLANES * a:, :].reshape(nch * rows, w)
        ca = cum3[:, SUBLANES * a:, :].reshape(nch * rows, w)
        acc_a = None
        for b in range(SUBLANES):
            wb = wsh[b, :, 0:rows, :].reshape(nch * rows, w)
            vb = vsh[b, :, 0:rows, :].reshape(nch * rows, w)
            dd = qa * jnp.exp2(ca + wb)
            term = _group_dot(dd.astype(BF16), bd) * vb
            acc_a = term if acc_a is None else acc_a + term
        acc_a = acc_a.reshape(nch, rows, w)
        if acc is not None:
            acc_a = acc_a + jnp.concatenate([jnp.zeros((nch, SUBLANES, w), F32), acc], axis=1)
        acc = acc_a
    acc = acc.reshape(tt, w)

    grp = st_scr.shape[1]
    bdmask = bd[0:grp, 0:grp] != 0
    outs = []
    for ci in range(nch):
        r0 = ci * c
        cum_c = cum[r0:r0 + c]
        last = cum_c[c - 1:c]
        qd = (qs[r0:r0 + c] * jnp.exp2(cum_c)).astype(BF16)
        kd = (kk[r0:r0 + c] * jnp.exp2(last - cum_c)).astype(BF16)
        vc = v[r0:r0 + c].astype(BF16)
        dec = jnp.exp2(last)
        o_parts = []
        for gi in range(w // grp):
            sl = slice(gi * grp, (gi + 1) * grp)
            st = st_scr[gi]
            o_parts.append(lax.dot_general(qd[:, sl], st.astype(BF16), (((1,), (1,)), ((), ())),
                                           preferred_element_type=F32))
            upd = lax.dot_general(vc[:, sl], kd[:, sl], (((0,), (0,)), ((), ())),
                                  preferred_element_type=F32)
            st_scr[gi] = st * dec[:, sl] + jnp.where(bdmask, upd, 0.0)
        outs.append(jnp.concatenate(o_parts, axis=1))
    o = acc + jnp.concatenate(outs, axis=0)

    o2 = o * o
    o2_hi = o2.astype(BF16)
    o2_lo = (o2 - o2_hi.astype(F32)).astype(BF16)
    ms = (_group_dot(o2_hi, bd) + _group_dot(o2_lo, bd)) * (1.0 / HG_DIM)
    gv = g_ref[...]
    o = o * lax.rsqrt(ms + EPS) * ng_ref[...] * (gv * jax.nn.sigmoid(gv))
    o_ref[...] = o.astype(o_ref.dtype)


def _split_dot_left(m, x, passes):
    acc = None
    r = x
    for p in range(passes):
        h = r.astype(BF16)
        term = jnp.dot(m, h, preferred_element_type=F32)
        acc = term if acc is None else acc + term
        if p + 1 < passes:
            r = r - h.astype(F32)
    return acc


def _hgrn(rest, lb_logits, norm_g, bsz, seq, layer, tri, bd):
    n = bsz * seq
    w = HG_HEADS * HG_DIM
    grp = bd.shape[0]
    tt = min(HGRN_TT, seq)
    c = min(HGRN_C, tt)
    nt = seq // tt
    kern = functools.partial(_hgrn_kernel, tt=tt, c=c, layer=layer)
    col = lambda j: pl.BlockSpec((tt, w), lambda b, i, j=j: (b * nt + i, j))
    const = lambda shape: pl.BlockSpec(shape, lambda b, i: (0, 0))
    return pl.pallas_call(
        kern,
        grid=(bsz, nt),
        in_specs=[col(0), col(1), col(2), col(3),
                  const(lb_logits.shape), const((1, w)), const((tt, tt)), const((grp, grp))],
        out_specs=pl.BlockSpec((tt, w), lambda b, i: (b * nt + i, 0)),
        out_shape=jax.ShapeDtypeStruct((n, w), BF16),
        scratch_shapes=([pltpu.VMEM((w // LANES, LANES, LANES), F32)]
                        + [pltpu.VMEM((SUBLANES + tt, w), F32)] * 2
                        + [pltpu.VMEM((SUBLANES, tt // c, c, w), F32)] * 2),
        compiler_params=_params(("parallel", "arbitrary")),
        name="hgrn",
    )(rest, rest, rest, rest, lb_logits, norm_g.reshape(1, w), tri, bd)


ROUTE_I1, ROUTE_I2, ROUTE_W1, ROUTE_W2 = N_EXPERTS, N_EXPERTS + 1, N_EXPERTS + 2, N_EXPERTS + 3


def _route(lg):
    lane = lax.broadcasted_iota(jnp.int32, lg.shape, 1)
    neg = jnp.float32(-jnp.inf)
    big = jnp.int32(LANES)
    gmask = jnp.logical_and(lane >= N_EXPERTS, lane < N_EXPERTS + N_GROUPS)
    gl = jnp.where(gmask, lg, neg)
    gmax = jnp.max(gl, axis=1, keepdims=True)
    gidx = jnp.min(jnp.where(gl == gmax, lane, big), axis=1, keepdims=True) - N_EXPERTS
    w_grp = 1.0 / jnp.sum(jnp.where(gmask, jnp.exp(gl - gmax), 0.0), axis=1, keepdims=True)
    in_grp = jnp.logical_and(lane < N_EXPERTS, lane // EXPERTS_PER_GROUP == gidx)
    l1 = jnp.where(in_grp, lg, neg)
    v1 = jnp.max(l1, axis=1, keepdims=True)
    i1 = jnp.min(jnp.where(l1 == v1, lane, big), axis=1, keepdims=True)
    l2 = jnp.where(jnp.logical_and(in_grp, lane != i1), lg, neg)
    v2 = jnp.max(l2, axis=1, keepdims=True)
    i2 = jnp.min(jnp.where(l2 == v2, lane, big), axis=1, keepdims=True)
    e2 = jnp.exp(v2 - v1)
    p1 = 1.0 / (1.0 + e2)
    p2 = e2 * p1
    out = jnp.where(jnp.logical_or(lane == i1, lane == i2), 1.0, 0.0)
    out = jnp.where(lane == ROUTE_I1, i1.astype(F32), out)
    out = jnp.where(lane == ROUTE_I2, i2.astype(F32), out)
    out = jnp.where(lane == ROUTE_W1, p1 * w_grp, out)
    return jnp.where(lane == ROUTE_W2, p2 * w_grp, out)


def _store_token_tiles(ref, x):
    rows, d = x.shape
    assert d == SUBLANES * LANES
    for s in range(SUBLANES):
        ref[pl.ds(s, rows, stride=SUBLANES), :] = x[:, s * LANES:(s + 1) * LANES]


def _load_token_tiles(ref):
    rows = ref.shape[0] // SUBLANES
    return jnp.concatenate([ref[pl.ds(s, rows, stride=SUBLANES), :] for s in range(SUBLANES)],
                           axis=1)


def _merge_kernel(x_ref, ysb_ref, yhg_ref, gsb_ref, ghg_ref, wbs_ref, wbh_ref, wo_ref,
                  ln_ref, wr_hi_ref, wr_lo_ref, br_ref, x1_ref, t_ref, route_ref):
    a = jnp.dot(ysb_ref[...], wbs_ref[...], preferred_element_type=F32)
    b = jnp.dot(yhg_ref[...], wbh_ref[...], preferred_element_type=F32)
    merged = jax.nn.sigmoid(gsb_ref[...]) * a + jax.nn.sigmoid(ghg_ref[...]) * b
    x1 = x_ref[...] + jnp.dot(merged.astype(BF16), wo_ref[...], preferred_element_type=F32)
    x1_ref[...] = x1
    var = jnp.mean(x1 * x1, axis=-1, keepdims=True)
    t = x1 * lax.rsqrt(var + EPS) * ln_ref[...]
    _store_token_tiles(t_ref, t)
    t_hi = t.astype(BF16)
    t_lo = (t - t_hi.astype(F32)).astype(BF16)
    lg = (jnp.dot(t_hi, wr_hi_ref[...], preferred_element_type=F32)
          + jnp.dot(t_lo, wr_hi_ref[...], preferred_element_type=F32)
          + jnp.dot(t_hi, wr_lo_ref[...], preferred_element_type=F32)) + br_ref[...]
    route_ref[...] = _route(lg)


def _merge(x2, y_sb, y_hg, rest, wbs, wbh, wo, ln_g, wr_hi, wr_lo, br):
    n, d = x2.shape
    tm = min(MERGE_TM, n)
    w_sb, w_hg = y_sb.shape[1], y_hg.shape[1]
    gate_blk = (rest.shape[1] - 2 * d) // d
    row = lambda wdt, j=0: pl.BlockSpec((tm, wdt), lambda i, j=j: (i, j))
    const = lambda shape: pl.BlockSpec(shape, lambda i: (0, 0))
    return pl.pallas_call(
        _merge_kernel,
        grid=(n // tm,),
        in_specs=[row(d), row(w_sb), row(w_hg), row(d, gate_blk), row(d, gate_blk + 1),
                  const(wbs.shape), const(wbh.shape), const(wo.shape), const((1, d)),
                  const(wr_hi.shape), const(wr_lo.shape), const((1, LANES))],
        out_specs=[row(d), pl.BlockSpec((tm * SUBLANES, LANES), lambda i: (i, 0)), row(LANES)],
        out_shape=[jax.ShapeDtypeStruct((n, d), F32),
                   jax.ShapeDtypeStruct((n * SUBLANES, LANES), F32),
                   jax.ShapeDtypeStruct((n, LANES), F32)],
        compiler_params=_params(("parallel",)),
        name="merge",
    )(x2, y_sb, y_hg, rest, rest, wbs, wbh, wo, ln_g.reshape(1, d), wr_hi, wr_lo, br)


def _plan_kernel(route_ref, ltri_ref, utri_ref, pos_ref, te_ref, meta_ref, tot_scr, run_scr,
                 *, tile_rows):
    ph = pl.program_id(0)
    i = pl.program_id(1)
    route = route_ref[...]
    tm = route.shape[0]
    lane = lax.broadcasted_iota(jnp.int32, (tm, LANES), 1)
    sel = jnp.where(lane < N_EXPERTS, route, 0.0)

    @pl.when(jnp.logical_and(ph == 0, i == 0))
    def _():
        tot_scr[...] = jnp.zeros_like(tot_scr)

    @pl.when(ph == 0)
    def _():
        tot_scr[...] += jnp.sum(sel, axis=0, keepdims=True)

    @pl.when(ph == 1)
    def _():
        tot = tot_scr[...]
        padded = jnp.ceil(tot * (1.0 / tile_rows)) * tile_rows
        off = _split_dot(jnp.broadcast_to(padded, (SUBLANES, LANES)), utri_ref[...], 3)[0:1]

        @pl.when(i == 0)
        def _():
            run_scr[...] = jnp.zeros_like(run_scr)
            lane1 = lax.broadcasted_iota(jnp.int32, (1, LANES), 1)
            is_e = lane1 < N_EXPERTS
            end = off + padded
            n_valid = (jnp.sum(jnp.where(is_e, padded, 0.0), axis=1, keepdims=True)
                       * (1.0 / tile_rows))
            rows = te_ref.shape[0]
            j = lax.broadcasted_iota(jnp.int32, (rows, LANES), 0).astype(F32)
            start = jnp.minimum(j, n_valid - 1.0) * tile_rows
            lane2 = lax.broadcasted_iota(jnp.int32, (rows, LANES), 1)
            done = jnp.where(jnp.logical_and(lane2 < N_EXPERTS, end <= start), 1.0, 0.0)
            te = jnp.sum(done, axis=1, keepdims=True)
            te_ref[...] = jnp.broadcast_to(te, (rows, LANES)).astype(jnp.int32)
            last_tile = jnp.where(jnp.logical_and(is_e, padded > 0),
                                  end * (1.0 / tile_rows) - 1.0, -1.0)
            meta = jnp.where(lane1 == N_EXPERTS, n_valid, last_tile)
            meta_ref[...] = jnp.broadcast_to(meta, meta_ref.shape).astype(jnp.int32)

        rank = jnp.dot(ltri_ref[...], sel.astype(BF16), preferred_element_type=F32) + run_scr[...]
        dest = off + rank
        lane_f = lane.astype(F32)
        p1 = jnp.sum(jnp.where(lane_f == route[:, ROUTE_I1:ROUTE_I1 + 1], dest, 0.0),
                     axis=1, keepdims=True)
        p2 = jnp.sum(jnp.where(lane_f == route[:, ROUTE_I2:ROUTE_I2 + 1], dest, 0.0),
                     axis=1, keepdims=True)
        pos_ref[...] = jnp.where(lane == 0, p1, jnp.where(lane == 1, p2, 0.0)).astype(jnp.int32)
        run_scr[...] += jnp.sum(sel, axis=0, keepdims=True)


def _plan(route, n_tiles_max):
    n = route.shape[0]
    tm = min(PLAN_TM, n)
    r = np.arange(tm)
    ltri = jnp.asarray((r[None, :] < r[:, None]).astype(np.float32), dtype=BF16)
    e = np.arange(LANES)
    utri = jnp.asarray((e[:, None] < e[None, :]).astype(np.float32), dtype=BF16)
    te_rows = -(-n_tiles_max // SUBLANES) * SUBLANES
    kern = functools.partial(_plan_kernel, tile_rows=MOE_TM)
    return pl.pallas_call(
        kern,
        grid=(2, n // tm),
        in_specs=[pl.BlockSpec((tm, LANES), lambda p, i: (i, 0)),
                  pl.BlockSpec((tm, tm), lambda p, i: (0, 0)),
                  pl.BlockSpec((LANES, LANES), lambda p, i: (0, 0))],
        out_specs=[pl.BlockSpec((tm, LANES), lambda p, i: (i * p, 0)),
                   pl.BlockSpec((te_rows, LANES), lambda p, i: (0, 0)),
                   pl.BlockSpec((SUBLANES, LANES), lambda p, i: (0, 0))],
        out_shape=[jax.ShapeDtypeStruct((n, LANES), jnp.int32),
                   jax.ShapeDtypeStruct((te_rows, LANES), jnp.int32),
                   jax.ShapeDtypeStruct((SUBLANES, LANES), jnp.int32)],
        scratch_shapes=[pltpu.VMEM((1, LANES), F32), pltpu.VMEM((1, LANES), F32)],
        compiler_params=_params(("arbitrary", "arbitrary")),
        name="plan",
    )(route, ltri, utri)


DMA_UNROLL = 8


def _row_copy(src_ref, r, dst_ref, p, sem):
    return pltpu.make_async_copy(src_ref.at[pl.ds(pl.multiple_of(r * SUBLANES, SUBLANES), SUBLANES)],
                                 dst_ref.at[pl.ds(pl.multiple_of(p * SUBLANES, SUBLANES), SUBLANES)],
                                 sem)


def _dispatch_kernel(pos1_ref, pos2_ref, meta_ref, t_ref, xs_ref, zero_scr, sem, *, tm, tile_rows):
    i = pl.program_id(0)
    tile_rows = tile_rows * SUBLANES
    n_tiles = xs_ref.shape[0] // tile_rows

    @pl.when(i == 0)
    def _():
        zero_scr[...] = jnp.zeros_like(zero_scr)
        n_valid = meta_ref[N_EXPERTS]
        clears = [(meta_ref[e], meta_ref[e] >= 0) for e in range(N_EXPERTS)]
        clears += [(n_tiles - 1 - k, n_tiles - 1 - k >= n_valid) for k in range(N_EXPERTS)]

        def clear(tile):
            return pltpu.make_async_copy(
                zero_scr, xs_ref.at[pl.ds(tile * tile_rows, tile_rows)], sem)

        for tile, cond in clears:
            @pl.when(cond)
            def _():
                clear(tile).start()
        for tile, cond in clears:
            @pl.when(cond)
            def _():
                clear(tile).wait()

    base = i * tm

    def start(r, carry):
        _row_copy(t_ref, r, xs_ref, pos1_ref[base + r], sem).start(priority=0)
        _row_copy(t_ref, r, xs_ref, pos2_ref[base + r], sem).start(priority=1)
        return carry

    def wait(r, carry):
        _row_copy(t_ref, r, xs_ref, pos1_ref[base + r], sem).wait()
        _row_copy(t_ref, r, xs_ref, pos2_ref[base + r], sem).wait()
        return carry

    lax.fori_loop(0, tm, start, 0, unroll=DMA_UNROLL)
    lax.fori_loop(0, tm, wait, 0, unroll=DMA_UNROLL)


def _dispatch(pos1, pos2, meta, t, n_rows):
    n = t.shape[0] // SUBLANES
    tm = min(DISPATCH_TM, n)
    kern = functools.partial(_dispatch_kernel, tm=tm, tile_rows=MOE_TM)
    grid_spec = pltpu.PrefetchScalarGridSpec(
        num_scalar_prefetch=3,
        grid=(n // tm,),
        in_specs=[pl.BlockSpec((tm * SUBLANES, LANES), lambda i, *_: (i, 0))],
        out_specs=pl.BlockSpec(memory_space=pl.ANY),
        scratch_shapes=[pltpu.VMEM((MOE_TM * SUBLANES, LANES), F32), pltpu.SemaphoreType.DMA(())],
    )
    return pl.pallas_call(
        kern,
        grid_spec=grid_spec,
        out_shape=jax.ShapeDtypeStruct((n_rows * SUBLANES, LANES), F32),
        compiler_params=_params(("arbitrary",)),
        name="dispatch",
    )(pos1, pos2, meta, t)


def _expert_kernel(te_ref, nv_ref, xs_ref, wg_ref, wu_ref, wd_ref, ys_ref):
    in_use = pl.program_id(0) < nv_ref[0]

    @pl.when(in_use)
    def _():
        x = _load_token_tiles(xs_ref).astype(BF16)
        hg = jnp.dot(x, wg_ref[...].astype(BF16), preferred_element_type=F32)
        hu = jnp.dot(x, wu_ref[...].astype(BF16), preferred_element_type=F32)
        a = (hg * jax.nn.sigmoid(hg) * hu).astype(BF16)
        _store_token_tiles(ys_ref, jnp.dot(a, wd_ref[...].astype(BF16),
                                           preferred_element_type=F32))

    @pl.when(jnp.logical_not(in_use))
    def _():
        ys_ref[...] = jnp.zeros_like(ys_ref)


def _experts(te, nv, xs, wg, wu, wd):
    _, d, de = wg.shape
    assert d == SUBLANES * LANES
    tile = lambda j, te, nv: (j, 0)
    grid_spec = pltpu.PrefetchScalarGridSpec(
        num_scalar_prefetch=2,
        grid=(xs.shape[0] // (MOE_TM * SUBLANES),),
        in_specs=[pl.BlockSpec((MOE_TM * SUBLANES, LANES), tile),
                  pl.BlockSpec((None, d, de), lambda j, te, nv: (te[j], 0, 0)),
                  pl.BlockSpec((None, d, de), lambda j, te, nv: (te[j], 0, 0)),
                  pl.BlockSpec((None, de, d), lambda j, te, nv: (te[j], 0, 0))],
        out_specs=pl.BlockSpec((MOE_TM * SUBLANES, LANES), tile),
    )
    return pl.pallas_call(
        _expert_kernel,
        grid_spec=grid_spec,
        out_shape=jax.ShapeDtypeStruct(xs.shape, F32),
        compiler_params=_params(("arbitrary",)),
        name="experts",
    )(te, nv, xs, wg, wu, wd)


def _combine_kernel(pos1_ref, pos2_ref, ys_ref, x1_ref, route_ref, fg_ref, o_ref, g1, g2, sem,
                    *, tm):
    i = pl.program_id(0)
    slot = i % 2

    def gather(step, buf, wait):
        base = step * tm

        def body(r, carry):
            c1 = _row_copy(ys_ref, pos1_ref[base + r], g1.at[buf], r, sem.at[buf])
            c2 = _row_copy(ys_ref, pos2_ref[base + r], g2.at[buf], r, sem.at[buf])
            if wait:
                c1.wait()
                c2.wait()
            else:
                c1.start(priority=0)
                c2.start(priority=1)
            return carry

        lax.fori_loop(0, tm, body, 0, unroll=DMA_UNROLL)

    @pl.when(i == 0)
    def _():
        gather(0, 0, wait=False)

    @pl.when(i + 1 < pl.num_programs(0))
    def _():
        gather(i + 1, 1 - slot, wait=False)

    gather(i, slot, wait=True)
    route = route_ref[...]
    w1 = route[:, ROUTE_W1:ROUTE_W1 + 1]
    w2 = route[:, ROUTE_W2:ROUTE_W2 + 1]
    x2 = x1_ref[...] + (w1 * _load_token_tiles(g1.at[slot]) + w2 * _load_token_tiles(g2.at[slot]))
    var = jnp.mean(x2 * x2, axis=-1, keepdims=True)
    o_ref[...] = x2 * lax.rsqrt(var + EPS) * fg_ref[...]


def _combine(pos1, pos2, ys, x1, route, final_g):
    n, d = x1.shape
    tm = min(DISPATCH_TM, n)
    kern = functools.partial(_combine_kernel, tm=tm)
    grid_spec = pltpu.PrefetchScalarGridSpec(
        num_scalar_prefetch=2,
        grid=(n // tm,),
        in_specs=[pl.BlockSpec(memory_space=pl.ANY),
                  pl.BlockSpec((tm, d), lambda i, *_: (i, 0)),
                  pl.BlockSpec((tm, LANES), lambda i, *_: (i, 0)),
                  pl.BlockSpec((1, d), lambda i, *_: (0, 0))],
        out_specs=pl.BlockSpec((tm, d), lambda i, *_: (i, 0)),
        scratch_shapes=([pltpu.VMEM((2, tm * SUBLANES, LANES), F32)] * 2
                        + [pltpu.SemaphoreType.DMA((2,))]),
    )
    return pl.pallas_call(
        kern,
        grid_spec=grid_spec,
        out_shape=jax.ShapeDtypeStruct((n, d), F32),
        compiler_params=_params(("arbitrary",)),
        name="combine",
    )(pos1, pos2, ys, x1, route, final_g.reshape(1, d))


def _moe_sparse(t, route, wg, wu, wd, x1, final_g):
    n = x1.shape[0]
    n_tiles_max = 2 * n // MOE_TM + N_EXPERTS
    pos, te, meta = _plan(route, n_tiles_max)
    pos1, pos2 = pos[:, 0], pos[:, 1]
    xs = _dispatch(pos1, pos2, meta[0, :N_EXPERTS + 1], t, n_tiles_max * MOE_TM)
    ys = _experts(te[:n_tiles_max, 0], meta[0, N_EXPERTS:N_EXPERTS + 1], xs, wg, wu, wd)
    return _combine(pos1, pos2, ys, x1, route, final_g)


def _suffix_ones(t):
    j = np.arange(t)[:, None]
    s = np.arange(t)[None, :]
    return jnp.asarray((j >= s).astype(np.float32), dtype=BF16)


def _causal_bias(t):
    row = np.arange(t)[:, None]
    col = np.arange(t)[None, :]
    diag = np.where(col < row, 0.0, MASK_BIAS).astype(np.float32)
    return jnp.asarray(np.stack([np.zeros_like(diag), diag]))


def _chunk_prefix_ones(tt, c):
    t = np.arange(tt)[:, None]
    j = np.arange(tt)[None, :]
    return jnp.asarray(((j <= t) & (t // c == j // c)).astype(np.float32), dtype=BF16)


def _block_diag_ones(w, blk):
    a = np.arange(w)
    return jnp.asarray((a[:, None] // blk == a[None, :] // blk).astype(np.float32), dtype=BF16)


def kernel(x, ln1_g, w_in, w_branch_sb, w_branch_hg, hg_norm_g, hg_lb_logits, w_out, ln2_g,
           w_router_group, b_router_group, w_router_expert, b_router_expert,
           w_exp_gate, w_exp_up, w_exp_down, final_g):
    bsz, seq, d = x.shape
    depth = w_in.shape[0]
    n = bsz * seq
    sb_width = SB_HEADS * SB_HEAD_DIM
    hg_width = HG_HEADS * HG_DIM

    tri_attn = _suffix_ones(min(ATTN_T, seq))
    bias_attn = _causal_bias(min(ATTN_T, seq))
    tt = min(HGRN_TT, seq)
    tri_hg = _chunk_prefix_ones(tt, min(HGRN_C, tt))
    bd = _block_diag_ones(min(MXU_DIM, hg_width), HG_DIM)

    x2 = x.reshape(n, d)
    for l in range(depth):
        qkv, rest = _inproj(x2, ln1_g[l], w_in[l].astype(BF16), sb_width)
        y_sb = _attn(qkv, bsz, seq, tri_attn, bias_attn)
        y_hg = _hgrn(rest, hg_lb_logits, hg_norm_g[l], bsz, seq, l, tri_hg, bd)

        pad = LANES - N_EXPERTS - N_GROUPS
        wr = jnp.concatenate([w_router_expert[l], w_router_group[l],
                              jnp.zeros((d, pad), F32)], axis=1)
        wr_hi = wr.astype(BF16)
        wr_lo = (wr - wr_hi.astype(F32)).astype(BF16)
        br = jnp.concatenate([b_router_expert[l], b_router_group[l],
                              jnp.zeros((pad,), F32)]).reshape(1, LANES)

        last = l == depth - 1
        x1, t, route = _merge(x2, y_sb, y_hg, rest, w_branch_sb[l].astype(BF16),
                              w_branch_hg[l].astype(BF16), w_out[l].astype(BF16), ln2_g[l],
                              wr_hi, wr_lo, br)
        assert last, "final rmsnorm is fused into the last layer's combine kernel"
        x2 = _moe_sparse(t, route, w_exp_gate[l], w_exp_up[l], w_exp_down[l], x1, final_g)
    return x2.reshape(bsz, seq, d)
```

```python
import functools

import jax
import jax.numpy as jnp
import numpy as np
from jax import lax
from jax.experimental import pallas as pl
from jax.experimental.pallas import tpu as pltpu

F32 = jnp.float32
BF16 = jnp.bfloat16

EPS = 1e-6
SB_HEADS = 8
SB_HEAD_DIM = 64
HG_HEADS = 8
HG_DIM = 64
N_GROUPS = 4
EXPERTS_PER_GROUP = 4
N_EXPERTS = N_GROUPS * EXPERTS_PER_GROUP

LANES = 128
SUBLANES = 8
MXU_DIM = 256
LOG2E = 1.4426950408889634
VMEM_LIMIT = 48 * 1024 * 1024

INPROJ_TM = 2048
INPROJ_TN = 512
ATTN_T = MXU_DIM
HGRN_TT = 256
HGRN_C = 32
MERGE_TM = 512
MOE_TM = 512
PLAN_TM = 1024
DISPATCH_TM = 1024


def _params(sem):
    return pltpu.CompilerParams(dimension_semantics=sem, vmem_limit_bytes=VMEM_LIMIT)


def _split_dot(x, m, passes):
    acc = None
    r = x
    for p in range(passes):
        h = r.astype(BF16)
        term = jnp.dot(h, m, preferred_element_type=F32)
        acc = term if acc is None else acc + term
        if p + 1 < passes:
            r = r - h.astype(F32)
    return acc


def _inproj_kernel(x_ref, g_ref, w_ref, qkv_ref, rest_ref, h_scr, *, q_scale):
    j = pl.program_id(1)

    @pl.when(j == 0)
    def _():
        x = x_ref[...]
        var = jnp.mean(x * x, axis=-1, keepdims=True)
        h_scr[...] = (x * lax.rsqrt(var + EPS) * g_ref[...]).astype(BF16)

    acc = jnp.dot(h_scr[...], w_ref[...], preferred_element_type=F32)
    qkv_ref[...] = (acc * jnp.where(j == 0, q_scale, 1.0)).astype(BF16)
    rest_ref[...] = acc


def _inproj(x2, ln_g, w_in, sb_width):
    n, d = x2.shape
    cols = w_in.shape[1]
    tm, tn = min(INPROJ_TM, n), INPROJ_TN
    assert sb_width == tn, "q block must be exactly one column tile"
    nq = 3 * sb_width // tn
    nj = cols // tn
    kern = functools.partial(_inproj_kernel, q_scale=SB_HEAD_DIM ** -0.5)
    return pl.pallas_call(
        kern,
        grid=(n // tm, nj),
        in_specs=[
            pl.BlockSpec((tm, d), lambda i, j: (i, 0)),
            pl.BlockSpec((1, d), lambda i, j: (0, 0)),
            pl.BlockSpec((d, tn), lambda i, j: (0, j)),
        ],
        out_specs=[
            pl.BlockSpec((tm, tn), lambda i, j: (i, jnp.minimum(j, nq))),
            pl.BlockSpec((tm, tn), lambda i, j: (i, jnp.maximum(j - nq, 0))),
        ],
        out_shape=[
            jax.ShapeDtypeStruct((n, (nq + 1) * tn), BF16),
            jax.ShapeDtypeStruct((n, cols - nq * tn), F32),
        ],
        scratch_shapes=[pltpu.VMEM((tm, d), BF16)],
        compiler_params=_params(("parallel", "arbitrary")),
        name="inproj",
    )(x2, ln_g.reshape(1, d), w_in)


ATTN_STAGES = 3
ATTN_STREAMS = 2
MASK_BIAS = -1e30
ATTN_SKIP_LOG2 = 160.0


def _attn_kernel(q_ref, k_ref, v_ref, tri_ref, bias_ref, o_ref,
                 z0, z1, z2, i0, i1, i2, acc_ref, c_ref, *, t, nq):
    zbuf = (z0, z1, z2)
    ibuf = (i0, i1, i2)
    for r in zbuf + ibuf:
        r[...] = jnp.zeros_like(r)
    acc_ref[...] = jnp.zeros_like(acc_ref)
    c_ref[...] = jnp.zeros_like(c_ref)

    lane = lax.broadcasted_iota(jnp.int32, (t, LANES), 1)
    head0 = lane < SB_HEAD_DIM
    nt = (((1,), (1,)), ((), ()))

    def stage_a(qi, kj, slot):
        q = q_ref[pl.ds(pl.multiple_of(qi * t, t), t), :]
        zero = jnp.zeros_like(q)
        q2 = jnp.concatenate([jnp.where(head0, q, zero), jnp.where(head0, zero, q)], axis=0)
        k = k_ref[pl.ds(pl.multiple_of(kj * t, t), t), :]
        z = lax.dot_general(q2, k, nt, preferred_element_type=F32)
        bias = bias_ref[(qi == kj).astype(jnp.int32)]
        zbuf[slot][...] = z * LOG2E + jnp.concatenate([bias, bias], axis=0)

    def stage_b(slot):
        z = zbuf[slot][...]
        p = jnp.maximum(z, 0.0) + jnp.log(1.0 + jnp.exp2(-jnp.abs(z))) * LOG2E
        incl = jnp.dot(p.astype(BF16), tri_ref[...], preferred_element_type=F32)
        ibuf[slot][...] = incl
        return incl[:, 0:1]

    def stage_c(qi, kj, slot, stream):
        first = qi == kj
        incl = ibuf[slot][...]
        c = jnp.where(first, 0.0, c_ref[stream])
        a = jnp.exp2(zbuf[slot][...] - incl - c)
        v = v_ref[pl.ds(pl.multiple_of(kj * t, t), t), :]
        pv = jnp.dot(a.astype(BF16), v, preferred_element_type=F32)
        acc = jnp.where(first, pv, acc_ref[stream] + pv)
        acc_ref[stream] = acc
        c_ref[stream] = c + incl[:, 0:1]
        o_ref[pl.ds(pl.multiple_of(qi * t, t), t), :] = (
            jnp.where(head0, acc[0:t], acc[t:2 * t]).astype(o_ref.dtype))

    def block(carry):
        qs, ks, qn, kn, skip_q, drained = carry
        qs, ks, qn, kn, skip_q = list(qs), list(ks), list(qn), list(kn), list(skip_q)
        all_done = qn[0] >= nq
        for p in range(1, ATTN_STREAMS):
            all_done = jnp.logical_and(all_done, qn[p] >= nq)
        drained = drained + all_done.astype(jnp.int32)
        for r in range(ATTN_STAGES * ATTN_STREAMS):
            sa, sc, sb = r % ATTN_STAGES, (r + 1) % ATTN_STAGES, (r + 2) % ATTN_STAGES
            pa, pb = r % ATTN_STREAMS, (r - 1) % ATTN_STREAMS
            stage_c(qs[sc], ks[sc], sc, pa)
            total_b = stage_b(sb)
            carry_b = jnp.where(qs[sb] == ks[sb], 0.0, c_ref[pb]) + total_b
            skip = skip_q[pa] == qn[pa]
            q_cur = jnp.where(skip, qn[pa] + ATTN_STREAMS, qn[pa])
            k_cur = jnp.where(skip, qn[pa] + ATTN_STREAMS, kn[pa])
            drain = q_cur >= nq
            qa = jnp.where(drain, 0, q_cur)
            ka = jnp.where(drain, 0, k_cur)
            stage_a(qa, ka, sa)
            last = k_cur == 0
            qn[pa] = jnp.where(jnp.logical_and(last, jnp.logical_not(drain)),
                               q_cur + ATTN_STREAMS, q_cur)
            kn[pa] = jnp.where(drain, k_cur, jnp.where(last, q_cur + ATTN_STREAMS, k_cur - 1))
            skip_q[pb] = jnp.where(jnp.min(carry_b) >= ATTN_SKIP_LOG2, qs[sb], skip_q[pb])
            qs[sa], ks[sa] = qa, ka
        return tuple(qs), tuple(ks), tuple(qn), tuple(kn), tuple(skip_q), drained

    zero = jnp.int32(0)
    first_q = tuple(jnp.int32(p) for p in range(ATTN_STREAMS))
    init = ((zero,) * ATTN_STAGES, (zero,) * ATTN_STAGES, first_q, first_q,
            (jnp.int32(-1),) * ATTN_STREAMS, zero)
    lax.while_loop(lambda carry: carry[5] < 1, block, init)


def _attn(qkv, bsz, seq, tri, bias):
    n = bsz * seq
    t = min(ATTN_T, seq)
    pairs = SB_HEADS * SB_HEAD_DIM // LANES
    kern = functools.partial(_attn_kernel, t=t, nq=seq // t)
    return pl.pallas_call(
        kern,
        grid=(bsz, pairs),
        in_specs=[
            pl.BlockSpec((seq, LANES), lambda b, p: (b, p)),
            pl.BlockSpec((seq, LANES), lambda b, p: (b, pairs + p)),
            pl.BlockSpec((seq, LANES), lambda b, p: (b, 2 * pairs + p)),
            pl.BlockSpec((t, t), lambda b, p: (0, 0)),
            pl.BlockSpec((2, t, t), lambda b, p: (0, 0, 0)),
        ],
        out_specs=pl.BlockSpec((seq, LANES), lambda b, p: (b, p)),
        out_shape=jax.ShapeDtypeStruct((n, pairs * LANES), BF16),
        scratch_shapes=([pltpu.VMEM((2 * t, t), F32)] * (2 * ATTN_STAGES)
                        + [pltpu.VMEM((ATTN_STREAMS, 2 * t, LANES), F32),
                           pltpu.VMEM((ATTN_STREAMS, 2 * t, 1), F32)]),
        compiler_params=_params(("parallel", "parallel")),
        name="attn",
    )(qkv, qkv, qkv, tri, bias)


def _group_dot(x, bd):
    g = bd.shape[0]
    parts = [jnp.dot(x[:, i:i + g], bd, preferred_element_type=F32)
             for i in range(0, x.shape[1], g)]
    return jnp.concatenate(parts, axis=1)


def _hgrn_kernel(q_ref, f_ref, i_ref, g_ref, lbl_ref, ng_ref, tri_ref, bd_ref, o_ref,
                 st_scr, wpad, vpad, wsh, vsh, *, tt, c, layer):
    ti = pl.program_id(1)
    w = q_ref.shape[1]
    nch = tt // c

    @pl.when(ti == 0)
    def _():
        st_scr[...] = jnp.zeros_like(st_scr)

    lg = lbl_ref[...]
    e = jnp.exp(lg - jnp.max(lg, axis=0, keepdims=True))
    lb = jnp.sum(e[0:layer + 1], axis=0, keepdims=True) / jnp.sum(e, axis=0, keepdims=True)

    f = lb + (1.0 - lb) * jax.nn.sigmoid(f_ref[...])
    kk = 1.0 - f
    qv = q_ref[...]
    qs = qv * jax.nn.sigmoid(qv)
    v = i_ref[...]
    bd = bd_ref[...]
    cum = _split_dot_left(tri_ref[...], jnp.log(f) * LOG2E, 3)
    wk = jnp.log(kk) * LOG2E - cum

    pos = lax.broadcasted_iota(jnp.int32, (tt, 1), 0) % c
    wpad[0:SUBLANES, :] = jnp.zeros((SUBLANES, w), F32)
    vpad[0:SUBLANES, :] = jnp.zeros((SUBLANES, w), F32)
    wpad[SUBLANES:SUBLANES + tt, :] = wk
    vpad[SUBLANES:SUBLANES + tt, :] = v
    for b in range(SUBLANES):
        wb = wpad[SUBLANES - b:SUBLANES - b + tt, :]
        vb = vpad[SUBLANES - b:SUBLANES - b + tt, :]
        if b:
            wb = jnp.where(pos >= b, wb, -jnp.inf)
        wsh[b] = wb.reshape(nch, c, w)
        vsh[b] = vb.reshape(nch, c, w)

    qs3 = qs.reshape(nch, c, w)
    cum3 = cum.reshape(nch, c, w)
    acc = None
    for a in reversed(range(c // SUBLANES)):
        rows = c - SUBLANES * a
        qa = qs3[:, SUBLANES * a:, :].reshape(nch * rows, w)
        ca = cum3[:, SUBLANES * a:, :].reshape(nch * rows, w)
        acc_a = None
        for b in range(SUBLANES):
            wb = wsh[b, :, 0:rows, :].reshape(nch * rows, w)
            vb = vsh[b, :, 0:rows, :].reshape(nch * rows, w)
            dd = qa * jnp.exp2(ca + wb)
            term = _group_dot(dd.astype(BF16), bd) * vb
            acc_a = term if acc_a is None else acc_a + term
        acc_a = acc_a.reshape(nch, rows, w)
        if acc is not None:
            acc_a = acc_a + jnp.concatenate([jnp.zeros((nch, SUBLANES, w), F32), acc], axis=1)
        acc = acc_a
    acc = acc.reshape(tt, w)

    grp = st_scr.shape[1]
    bdmask = bd[0:grp, 0:grp] != 0
    outs = []
    for ci in range(nch):
        r0 = ci * c
        cum_c = cum[r0:r0 + c]
        last = cum_c[c - 1:c]
        qd = (qs[r0:r0 + c] * jnp.exp2(cum_c)).astype(BF16)
        kd = (kk[r0:r0 + c] * jnp.exp2(last - cum_c)).astype(BF16)
        vc = v[r0:r0 + c].astype(BF16)
        dec = jnp.exp2(last)
        o_parts = []
        for gi in range(w // grp):
            sl = slice(gi * grp, (gi + 1) * grp)
            st = st_scr[gi]
            o_parts.append(lax.dot_general(qd[:, sl], st.astype(BF16), (((1,), (1,)), ((), ())),
                                           preferred_element_type=F32))
            upd = lax.dot_general(vc[:, sl], kd[:, sl], (((0,), (0,)), ((), ())),
                                  preferred_element_type=F32)
            st_scr[gi] = st * dec[:, sl] + jnp.where(bdmask, upd, 0.0)
        outs.append(jnp.concatenate(o_parts, axis=1))
    o = acc + jnp.concatenate(outs, axis=0)

    o2 = o * o
    o2_hi = o2.astype(BF16)
    o2_lo = (o2 - o2_hi.astype(F32)).astype(BF16)
    ms = (_group_dot(o2_hi, bd) + _group_dot(o2_lo, bd)) * (1.0 / HG_DIM)
    gv = g_ref[...]
    o = o * lax.rsqrt(ms + EPS) * ng_ref[...] * (gv * jax.nn.sigmoid(gv))
    o_ref[...] = o.astype(o_ref.dtype)


def _split_dot_left(m, x, passes):
    acc = None
    r = x
    for p in range(passes):
        h = r.astype(BF16)
        term = jnp.dot(m, h, preferred_element_type=F32)
        acc = term if acc is None else acc + term
        if p + 1 < passes:
            r = r - h.astype(F32)
    return acc


def _hgrn(rest, lb_logits, norm_g, bsz, seq, layer, tri, bd):
    n = bsz * seq
    w = HG_HEADS * HG_DIM
    grp = bd.shape[0]
    tt = min(HGRN_TT, seq)
    c = min(HGRN_C, tt)
    nt = seq // tt
    kern = functools.partial(_hgrn_kernel, tt=tt, c=c, layer=layer)
    col = lambda j: pl.BlockSpec((tt, w), lambda b, i, j=j: (b * nt + i, j))
    const = lambda shape: pl.BlockSpec(shape, lambda b, i: (0, 0))
    return pl.pallas_call(
        kern,
        grid=(bsz, nt),
        in_specs=[col(0), col(1), col(2), col(3),
                  const(lb_logits.shape), const((1, w)), const((tt, tt)), const((grp, grp))],
        out_specs=pl.BlockSpec((tt, w), lambda b, i: (b * nt + i, 0)),
        out_shape=jax.ShapeDtypeStruct((n, w), BF16),
        scratch_shapes=([pltpu.VMEM((w // LANES, LANES, LANES), F32)]
                        + [pltpu.VMEM((SUBLANES + tt, w), F32)] * 2
                        + [pltpu.VMEM((SUBLANES, tt // c, c, w), F32)] * 2),
        compiler_params=_params(("parallel", "arbitrary")),
        name="hgrn",
    )(rest, rest, rest, rest, lb_logits, norm_g.reshape(1, w), tri, bd)


ROUTE_I1, ROUTE_I2, ROUTE_W1, ROUTE_W2 = N_EXPERTS, N_EXPERTS + 1, N_EXPERTS + 2, N_EXPERTS + 3


def _route(lg):
    lane = lax.broadcasted_iota(jnp.int32, lg.shape, 1)
    neg = jnp.float32(-jnp.inf)
    big = jnp.int32(LANES)
    gmask = jnp.logical_and(lane >= N_EXPERTS, lane < N_EXPERTS + N_GROUPS)
    gl = jnp.where(gmask, lg, neg)
    gmax = jnp.max(gl, axis=1, keepdims=True)
    gidx = jnp.min(jnp.where(gl == gmax, lane, big), axis=1, keepdims=True) - N_EXPERTS
    w_grp = 1.0 / jnp.sum(jnp.where(gmask, jnp.exp(gl - gmax), 0.0), axis=1, keepdims=True)
    in_grp = jnp.logical_and(lane < N_EXPERTS, lane // EXPERTS_PER_GROUP == gidx)
    l1 = jnp.where(in_grp, lg, neg)
    v1 = jnp.max(l1, axis=1, keepdims=True)
    i1 = jnp.min(jnp.where(l1 == v1, lane, big), axis=1, keepdims=True)
    l2 = jnp.where(jnp.logical_and(in_grp, lane != i1), lg, neg)
    v2 = jnp.max(l2, axis=1, keepdims=True)
    i2 = jnp.min(jnp.where(l2 == v2, lane, big), axis=1, keepdims=True)
    e2 = jnp.exp(v2 - v1)
    p1 = 1.0 / (1.0 + e2)
    p2 = e2 * p1
    out = jnp.where(jnp.logical_or(lane == i1, lane == i2), 1.0, 0.0)
    out = jnp.where(lane == ROUTE_I1, i1.astype(F32), out)
    out = jnp.where(lane == ROUTE_I2, i2.astype(F32), out)
    out = jnp.where(lane == ROUTE_W1, p1 * w_grp, out)
    return jnp.where(lane == ROUTE_W2, p2 * w_grp, out)


def _store_token_tiles(ref, x):
    rows, d = x.shape
    assert d == SUBLANES * LANES
    for s in range(SUBLANES):
        ref[pl.ds(s, rows, stride=SUBLANES), :] = x[:, s * LANES:(s + 1) * LANES]


def _load_token_tiles(ref):
    rows = ref.shape[0] // SUBLANES
    return jnp.concatenate([ref[pl.ds(s, rows, stride=SUBLANES), :] for s in range(SUBLANES)],
                           axis=1)


def _merge_kernel(x_ref, ysb_ref, yhg_ref, gsb_ref, ghg_ref, wbs_ref, wbh_ref, wo_ref,
                  ln_ref, wr_ref, br_ref, x1_ref, t_ref, route_ref):
    a = jnp.dot(ysb_ref[...], wbs_ref[...], preferred_element_type=F32)
    b = jnp.dot(yhg_ref[...], wbh_ref[...], preferred_element_type=F32)
    merged = jax.nn.sigmoid(gsb_ref[...]) * a + jax.nn.sigmoid(ghg_ref[...]) * b
    x1 = x_ref[...] + jnp.dot(merged.astype(BF16), wo_ref[...], preferred_element_type=F32)
    x1_ref[...] = x1
    var = jnp.mean(x1 * x1, axis=-1, keepdims=True)
    t = x1 * lax.rsqrt(var + EPS) * ln_ref[...]
    _store_token_tiles(t_ref, t)
    t_hi = t.astype(BF16)
    t_lo = (t - t_hi.astype(F32)).astype(BF16)
    p_hi = jnp.dot(t_hi, wr_ref[...], preferred_element_type=F32)
    p_lo = jnp.dot(t_lo, wr_ref[...], preferred_element_type=F32)
    lg = (p_hi[:, :LANES] + p_hi[:, LANES:]) + (p_lo[:, :LANES] + p_lo[:, LANES:]) + br_ref[...]
    route_ref[...] = _route(lg)


def _merge(x2, y_sb, y_hg, rest, wbs, wbh, wo, ln_g, wr, br):
    n, d = x2.shape
    tm = min(MERGE_TM, n)
    w_sb, w_hg = y_sb.shape[1], y_hg.shape[1]
    gate_blk = (rest.shape[1] - 2 * d) // d
    row = lambda wdt, j=0: pl.BlockSpec((tm, wdt), lambda i, j=j: (i, j))
    const = lambda shape: pl.BlockSpec(shape, lambda i: (0, 0))
    return pl.pallas_call(
        _merge_kernel,
        grid=(n // tm,),
        in_specs=[row(d), row(w_sb), row(w_hg), row(d, gate_blk), row(d, gate_blk + 1),
                  const(wbs.shape), const(wbh.shape), const(wo.shape), const((1, d)),
                  const(wr.shape), const((1, LANES))],
        out_specs=[row(d), pl.BlockSpec((tm * SUBLANES, LANES), lambda i: (i, 0)), row(LANES)],
        out_shape=[jax.ShapeDtypeStruct((n, d), F32),
                   jax.ShapeDtypeStruct((n * SUBLANES, LANES), F32),
                   jax.ShapeDtypeStruct((n, LANES), F32)],
        compiler_params=_params(("parallel",)),
        name="merge",
    )(x2, y_sb, y_hg, rest, rest, wbs, wbh, wo, ln_g.reshape(1, d), wr, br)


def _plan_kernel(route_ref, ltri_ref, utri_ref, pos_ref, te_ref, meta_ref, tot_scr, run_scr,
                 *, tile_rows):
    ph = pl.program_id(0)
    i = pl.program_id(1)
    route = route_ref[...]
    tm = route.shape[0]
    lane = lax.broadcasted_iota(jnp.int32, (tm, LANES), 1)
    sel = jnp.where(lane < N_EXPERTS, route, 0.0)

    @pl.when(jnp.logical_and(ph == 0, i == 0))
    def _():
        tot_scr[...] = jnp.zeros_like(tot_scr)

    @pl.when(ph == 0)
    def _():
        tot_scr[...] += jnp.sum(sel, axis=0, keepdims=True)

    @pl.when(ph == 1)
    def _():
        tot = tot_scr[...]
        padded = jnp.ceil(tot * (1.0 / tile_rows)) * tile_rows
        off = _split_dot(jnp.broadcast_to(padded, (SUBLANES, LANES)), utri_ref[...], 3)[0:1]

        @pl.when(i == 0)
        def _():
            run_scr[...] = jnp.zeros_like(run_scr)
            lane1 = lax.broadcasted_iota(jnp.int32, (1, LANES), 1)
            is_e = lane1 < N_EXPERTS
            end = off + padded
            n_valid = (jnp.sum(jnp.where(is_e, padded, 0.0), axis=1, keepdims=True)
                       * (1.0 / tile_rows))
            rows = te_ref.shape[0]
            j = lax.broadcasted_iota(jnp.int32, (rows, LANES), 0).astype(F32)
            start = jnp.minimum(j, n_valid - 1.0) * tile_rows
            lane2 = lax.broadcasted_iota(jnp.int32, (rows, LANES), 1)
            done = jnp.where(jnp.logical_and(lane2 < N_EXPERTS, end <= start), 1.0, 0.0)
            te = jnp.sum(done, axis=1, keepdims=True)
            te_ref[...] = jnp.broadcast_to(te, (rows, LANES)).astype(jnp.int32)
            last_tile = jnp.where(jnp.logical_and(is_e, padded > 0),
                                  end * (1.0 / tile_rows) - 1.0, -1.0)
            meta = jnp.where(lane1 == N_EXPERTS, n_valid, last_tile)
            meta_ref[...] = jnp.broadcast_to(meta, meta_ref.shape).astype(jnp.int32)

        rank = jnp.dot(ltri_ref[...], sel.astype(BF16), preferred_element_type=F32) + run_scr[...]
        dest = off + rank
        lane_f = lane.astype(F32)
        p1 = jnp.sum(jnp.where(lane_f == route[:, ROUTE_I1:ROUTE_I1 + 1], dest, 0.0),
                     axis=1, keepdims=True)
        p2 = jnp.sum(jnp.where(lane_f == route[:, ROUTE_I2:ROUTE_I2 + 1], dest, 0.0),
                     axis=1, keepdims=True)
        pos_ref[...] = jnp.where(lane == 0, p1, jnp.where(lane == 1, p2, 0.0)).astype(jnp.int32)
        run_scr[...] += jnp.sum(sel, axis=0, keepdims=True)


def _plan(route, n_tiles_max):
    n = route.shape[0]
    tm = min(PLAN_TM, n)
    r = np.arange(tm)
    ltri = jnp.asarray((r[None, :] < r[:, None]).astype(np.float32), dtype=BF16)
    e = np.arange(LANES)
    utri = jnp.asarray((e[:, None] < e[None, :]).astype(np.float32), dtype=BF16)
    te_rows = -(-n_tiles_max // SUBLANES) * SUBLANES
    kern = functools.partial(_plan_kernel, tile_rows=MOE_TM)
    return pl.pallas_call(
        kern,
        grid=(2, n // tm),
        in_specs=[pl.BlockSpec((tm, LANES), lambda p, i: (i, 0)),
                  pl.BlockSpec((tm, tm), lambda p, i: (0, 0)),
                  pl.BlockSpec((LANES, LANES), lambda p, i: (0, 0))],
        out_specs=[pl.BlockSpec((tm, LANES), lambda p, i: (i * p, 0)),
                   pl.BlockSpec((te_rows, LANES), lambda p, i: (0, 0)),
                   pl.BlockSpec((SUBLANES, LANES), lambda p, i: (0, 0))],
        out_shape=[jax.ShapeDtypeStruct((n, LANES), jnp.int32),
                   jax.ShapeDtypeStruct((te_rows, LANES), jnp.int32),
                   jax.ShapeDtypeStruct((SUBLANES, LANES), jnp.int32)],
        scratch_shapes=[pltpu.VMEM((1, LANES), F32), pltpu.VMEM((1, LANES), F32)],
        compiler_params=_params(("arbitrary", "arbitrary")),
        name="plan",
    )(route, ltri, utri)


DMA_UNROLL = 8


def _row_copy(src_ref, r, dst_ref, p, sem):
    return pltpu.make_async_copy(src_ref.at[pl.ds(pl.multiple_of(r * SUBLANES, SUBLANES), SUBLANES)],
                                 dst_ref.at[pl.ds(pl.multiple_of(p * SUBLANES, SUBLANES), SUBLANES)],
                                 sem)


def _dispatch_kernel(pos1_ref, pos2_ref, meta_ref, t_ref, xs_ref, zero_scr, sem, *, tm, tile_rows):
    i = pl.program_id(0)
    tile_rows = tile_rows * SUBLANES
    n_tiles = xs_ref.shape[0] // tile_rows

    @pl.when(i == 0)
    def _():
        zero_scr[...] = jnp.zeros_like(zero_scr)
        n_valid = meta_ref[N_EXPERTS]
        clears = [(meta_ref[e], meta_ref[e] >= 0) for e in range(N_EXPERTS)]
        clears += [(n_tiles - 1 - k, n_tiles - 1 - k >= n_valid) for k in range(N_EXPERTS)]

        def clear(tile):
            return pltpu.make_async_copy(
                zero_scr, xs_ref.at[pl.ds(tile * tile_rows, tile_rows)], sem)

        for tile, cond in clears:
            @pl.when(cond)
            def _():
                clear(tile).start()
        for tile, cond in clears:
            @pl.when(cond)
            def _():
                clear(tile).wait()

    base = i * tm

    def start(r, carry):
        _row_copy(t_ref, r, xs_ref, pos1_ref[base + r], sem).start(priority=0)
        _row_copy(t_ref, r, xs_ref, pos2_ref[base + r], sem).start(priority=1)
        return carry

    def wait(r, carry):
        _row_copy(t_ref, r, xs_ref, pos1_ref[base + r], sem).wait()
        _row_copy(t_ref, r, xs_ref, pos2_ref[base + r], sem).wait()
        return carry

    lax.fori_loop(0, tm, start, 0, unroll=DMA_UNROLL)
    lax.fori_loop(0, tm, wait, 0, unroll=DMA_UNROLL)


def _dispatch(pos1, pos2, meta, t, n_rows):
    n = t.shape[0] // SUBLANES
    tm = min(DISPATCH_TM, n)
    kern = functools.partial(_dispatch_kernel, tm=tm, tile_rows=MOE_TM)
    grid_spec = pltpu.PrefetchScalarGridSpec(
        num_scalar_prefetch=3,
        grid=(n // tm,),
        in_specs=[pl.BlockSpec((tm * SUBLANES, LANES), lambda i, *_: (i, 0))],
        out_specs=pl.BlockSpec(memory_space=pl.ANY),
        scratch_shapes=[pltpu.VMEM((MOE_TM * SUBLANES, LANES), F32), pltpu.SemaphoreType.DMA(())],
    )
    return pl.pallas_call(
        kern,
        grid_spec=grid_spec,
        out_shape=jax.ShapeDtypeStruct((n_rows * SUBLANES, LANES), F32),
        compiler_params=_params(("arbitrary",)),
        name="dispatch",
    )(pos1, pos2, meta, t)


def _expert_kernel(te_ref, nv_ref, xs_ref, wg_ref, wu_ref, wd_ref, ys_ref):
    in_use = pl.program_id(0) < nv_ref[0]

    @pl.when(in_use)
    def _():
        x = _load_token_tiles(xs_ref).astype(BF16)
        hg = jnp.dot(x, wg_ref[...].astype(BF16), preferred_element_type=F32)
        hu = jnp.dot(x, wu_ref[...].astype(BF16), preferred_element_type=F32)
        a = (hg * jax.nn.sigmoid(hg) * hu).astype(BF16)
        _store_token_tiles(ys_ref, jnp.dot(a, wd_ref[...].astype(BF16),
                                           preferred_element_type=F32))

    @pl.when(jnp.logical_not(in_use))
    def _():
        ys_ref[...] = jnp.zeros_like(ys_ref)


def _experts(te, nv, xs, wg, wu, wd):
    _, d, de = wg.shape
    assert d == SUBLANES * LANES
    tile = lambda j, te, nv: (j, 0)
    grid_spec = pltpu.PrefetchScalarGridSpec(
        num_scalar_prefetch=2,
        grid=(xs.shape[0] // (MOE_TM * SUBLANES),),
        in_specs=[pl.BlockSpec((MOE_TM * SUBLANES, LANES), tile),
                  pl.BlockSpec((None, d, de), lambda j, te, nv: (te[j], 0, 0)),
                  pl.BlockSpec((None, d, de), lambda j, te, nv: (te[j], 0, 0)),
                  pl.BlockSpec((None, de, d), lambda j, te, nv: (te[j], 0, 0))],
        out_specs=pl.BlockSpec((MOE_TM * SUBLANES, LANES), tile),
    )
    return pl.pallas_call(
        _expert_kernel,
        grid_spec=grid_spec,
        out_shape=jax.ShapeDtypeStruct(xs.shape, F32),
        compiler_params=_params(("arbitrary",)),
        name="experts",
    )(te, nv, xs, wg, wu, wd)


def _combine_kernel(pos1_ref, pos2_ref, ys_ref, x1_ref, route_ref, fg_ref, o_ref, g1, g2, sem,
                    *, tm):
    i = pl.program_id(0)
    slot = i % 2

    def gather(step, buf, wait):
        base = step * tm

        def body(r, carry):
            c1 = _row_copy(ys_ref, pos1_ref[base + r], g1.at[buf], r, sem.at[buf])
            c2 = _row_copy(ys_ref, pos2_ref[base + r], g2.at[buf], r, sem.at[buf])
            if wait:
                c1.wait()
                c2.wait()
            else:
                c1.start(priority=0)
                c2.start(priority=1)
            return carry

        lax.fori_loop(0, tm, body, 0, unroll=DMA_UNROLL)

    @pl.when(i == 0)
    def _():
        gather(0, 0, wait=False)

    @pl.when(i + 1 < pl.num_programs(0))
    def _():
        gather(i + 1, 1 - slot, wait=False)

    gather(i, slot, wait=True)
    route = route_ref[...]
    w1 = route[:, ROUTE_W1:ROUTE_W1 + 1]
    w2 = route[:, ROUTE_W2:ROUTE_W2 + 1]
    x2 = x1_ref[...] + (w1 * _load_token_tiles(g1.at[slot]) + w2 * _load_token_tiles(g2.at[slot]))
    var = jnp.mean(x2 * x2, axis=-1, keepdims=True)
    o_ref[...] = x2 * lax.rsqrt(var + EPS) * fg_ref[...]


def _combine(pos1, pos2, ys, x1, route, final_g):
    n, d = x1.shape
    tm = min(DISPATCH_TM, n)
    kern = functools.partial(_combine_kernel, tm=tm)
    grid_spec = pltpu.PrefetchScalarGridSpec(
        num_scalar_prefetch=2,
        grid=(n // tm,),
        in_specs=[pl.BlockSpec(memory_space=pl.ANY),
                  pl.BlockSpec((tm, d), lambda i, *_: (i, 0)),
                  pl.BlockSpec((tm, LANES), lambda i, *_: (i, 0)),
                  pl.BlockSpec((1, d), lambda i, *_: (0, 0))],
        out_specs=pl.BlockSpec((tm, d), lambda i, *_: (i, 0)),
        scratch_shapes=([pltpu.VMEM((2, tm * SUBLANES, LANES), F32)] * 2
                        + [pltpu.SemaphoreType.DMA((2,))]),
    )
    return pl.pallas_call(
        kern,
        grid_spec=grid_spec,
        out_shape=jax.ShapeDtypeStruct((n, d), F32),
        compiler_params=_params(("arbitrary",)),
        name="combine",
    )(pos1, pos2, ys, x1, route, final_g.reshape(1, d))


def _moe_sparse(t, route, wg, wu, wd, x1, final_g):
    n = x1.shape[0]
    n_tiles_max = 2 * n // MOE_TM + N_EXPERTS
    pos, te, meta = _plan(route, n_tiles_max)
    pos1, pos2 = pos[:, 0], pos[:, 1]
    xs = _dispatch(pos1, pos2, meta[0, :N_EXPERTS + 1], t, n_tiles_max * MOE_TM)
    ys = _experts(te[:n_tiles_max, 0], meta[0, N_EXPERTS:N_EXPERTS + 1], xs, wg, wu, wd)
    return _combine(pos1, pos2, ys, x1, route, final_g)


def _suffix_ones(t):
    j = np.arange(t)[:, None]
    s = np.arange(t)[None, :]
    return jnp.asarray((j >= s).astype(np.float32), dtype=BF16)


def _causal_bias(t):
    row = np.arange(t)[:, None]
    col = np.arange(t)[None, :]
    diag = np.where(col < row, 0.0, MASK_BIAS).astype(np.float32)
    return jnp.asarray(np.stack([np.zeros_like(diag), diag]))


def _chunk_prefix_ones(tt, c):
    t = np.arange(tt)[:, None]
    j = np.arange(tt)[None, :]
    return jnp.asarray(((j <= t) & (t // c == j // c)).astype(np.float32), dtype=BF16)


def _block_diag_ones(w, blk):
    a = np.arange(w)
    return jnp.asarray((a[:, None] // blk == a[None, :] // blk).astype(np.float32), dtype=BF16)


def kernel(x, ln1_g, w_in, w_branch_sb, w_branch_hg, hg_norm_g, hg_lb_logits, w_out, ln2_g,
           w_router_group, b_router_group, w_router_expert, b_router_expert,
           w_exp_gate, w_exp_up, w_exp_down, final_g):
    bsz, seq, d = x.shape
    depth = w_in.shape[0]
    n = bsz * seq
    sb_width = SB_HEADS * SB_HEAD_DIM
    hg_width = HG_HEADS * HG_DIM

    tri_attn = _suffix_ones(min(ATTN_T, seq))
    bias_attn = _causal_bias(min(ATTN_T, seq))
    tt = min(HGRN_TT, seq)
    tri_hg = _chunk_prefix_ones(tt, min(HGRN_C, tt))
    bd = _block_diag_ones(min(MXU_DIM, hg_width), HG_DIM)

    x2 = x.reshape(n, d)
    for l in range(depth):
        qkv, rest = _inproj(x2, ln1_g[l], w_in[l].astype(BF16), sb_width)
        y_sb = _attn(qkv, bsz, seq, tri_attn, bias_attn)
        y_hg = _hgrn(rest, hg_lb_logits, hg_norm_g[l], bsz, seq, l, tri_hg, bd)

        pad = LANES - N_EXPERTS - N_GROUPS
        wr = jnp.concatenate([w_router_expert[l], w_router_group[l],
                              jnp.zeros((d, pad), F32)], axis=1)
        wr_hi = wr.astype(BF16)
        wr_lo = (wr - wr_hi.astype(F32)).astype(BF16)
        wr_split = jnp.concatenate([wr_hi, wr_lo], axis=1)
        br = jnp.concatenate([b_router_expert[l], b_router_group[l],
                              jnp.zeros((pad,), F32)]).reshape(1, LANES)

        last = l == depth - 1
        x1, t, route = _merge(x2, y_sb, y_hg, rest, w_branch_sb[l].astype(BF16),
                              w_branch_hg[l].astype(BF16), w_out[l].astype(BF16), ln2_g[l],
                              wr_split, br)
        assert last, "final rmsnorm is fused into the last layer's combine kernel"
        x2 = _moe_sparse(t, route, w_exp_gate[l], w_exp_up[l], w_exp_down[l], x1, final_g)
    return x2.reshape(bsz, seq, d)
```

```python
import functools

import jax
import jax.numpy as jnp
import numpy as np
from jax import lax
from jax.experimental import pallas as pl
from jax.experimental.pallas import tpu as pltpu

F32 = jnp.float32
BF16 = jnp.bfloat16

EPS = 1e-6
SB_HEADS = 8
SB_HEAD_DIM = 64
HG_HEADS = 8
HG_DIM = 64
N_GROUPS = 4
EXPERTS_PER_GROUP = 4
N_EXPERTS = N_GROUPS * EXPERTS_PER_GROUP

LANES = 128
SUBLANES = 8
MXU_DIM = 256
LOG2E = 1.4426950408889634
VMEM_LIMIT = 48 * 1024 * 1024

INPROJ_TM = 2048
INPROJ_TN = 512
ATTN_T = MXU_DIM
HGRN_TT = 256
HGRN_C = 32
MERGE_TM = 512
MOE_TM = 512
PLAN_TM = 1024
DISPATCH_TM = 1024
COMBINE_TM = 512


def _params(sem):
    return pltpu.CompilerParams(dimension_semantics=sem, vmem_limit_bytes=VMEM_LIMIT)


def _split_dot(x, m, passes):
    acc = None
    r = x
    for p in range(passes):
        h = r.astype(BF16)
        term = jnp.dot(h, m, preferred_element_type=F32)
        acc = term if acc is None else acc + term
        if p + 1 < passes:
            r = r - h.astype(F32)
    return acc


def _rmsnorm_bf16(x, g):
    var = jnp.mean(x * x, axis=-1, keepdims=True)
    return (x * lax.rsqrt(var + EPS) * g).astype(BF16)


def _inproj_kernel(x_ref, g_ref, w_ref, qkv_ref, h_scr, *, q_scale):
    j = pl.program_id(1)

    @pl.when(j == 0)
    def _():
        h_scr[...] = _rmsnorm_bf16(x_ref[...], g_ref[...])

    acc = jnp.dot(h_scr[...], w_ref[...], preferred_element_type=F32)
    qkv_ref[...] = (acc * jnp.where(j == 0, q_scale, 1.0)).astype(BF16)


def _inproj(x2, ln_g, w_qkv, sb_width):
    n, d = x2.shape
    tm, tn = min(INPROJ_TM, n), INPROJ_TN
    assert sb_width == tn, "q block must be exactly one column tile"
    kern = functools.partial(_inproj_kernel, q_scale=SB_HEAD_DIM ** -0.5)
    return pl.pallas_call(
        kern,
        grid=(n // tm, w_qkv.shape[1] // tn),
        in_specs=[
            pl.BlockSpec((tm, d), lambda i, j: (i, 0)),
            pl.BlockSpec((1, d), lambda i, j: (0, 0)),
            pl.BlockSpec((d, tn), lambda i, j: (0, j)),
        ],
        out_specs=pl.BlockSpec((tm, tn), lambda i, j: (i, j)),
        out_shape=jax.ShapeDtypeStruct((n, w_qkv.shape[1]), BF16),
        scratch_shapes=[pltpu.VMEM((tm, d), BF16)],
        compiler_params=_params(("parallel", "arbitrary")),
        name="inproj",
    )(x2, ln_g.reshape(1, d), w_qkv)


ATTN_STAGES = 3
ATTN_STREAMS = 2
MASK_BIAS = -1e30
ATTN_SKIP_LOG2 = 160.0


def _attn_kernel(q_ref, k_ref, v_ref, tri_ref, bias_ref, o_ref,
                 z0, z1, z2, i0, i1, i2, acc_ref, c_ref, *, t, nq):
    zbuf = (z0, z1, z2)
    ibuf = (i0, i1, i2)
    for r in zbuf + ibuf:
        r[...] = jnp.zeros_like(r)
    acc_ref[...] = jnp.zeros_like(acc_ref)
    c_ref[...] = jnp.zeros_like(c_ref)

    lane = lax.broadcasted_iota(jnp.int32, (t, LANES), 1)
    head0 = lane < SB_HEAD_DIM
    nt = (((1,), (1,)), ((), ()))

    def stage_a(qi, kj, slot):
        q = q_ref[pl.ds(pl.multiple_of(qi * t, t), t), :]
        zero = jnp.zeros_like(q)
        q2 = jnp.concatenate([jnp.where(head0, q, zero), jnp.where(head0, zero, q)], axis=0)
        k = k_ref[pl.ds(pl.multiple_of(kj * t, t), t), :]
        z = lax.dot_general(q2, k, nt, preferred_element_type=F32)
        bias = bias_ref[(qi == kj).astype(jnp.int32)]
        zbuf[slot][...] = z * LOG2E + jnp.concatenate([bias, bias], axis=0)

    def stage_b(slot):
        z = zbuf[slot][...]
        p = jnp.maximum(z, 0.0) + jnp.log(1.0 + jnp.exp2(-jnp.abs(z))) * LOG2E
        incl = jnp.dot(p.astype(BF16), tri_ref[...], preferred_element_type=F32)
        ibuf[slot][...] = incl
        return incl[:, 0:1]

    def stage_c(qi, kj, slot, stream):
        first = qi == kj
        incl = ibuf[slot][...]
        c = jnp.where(first, 0.0, c_ref[stream])
        a = jnp.exp2(zbuf[slot][...] - incl - c)
        v = v_ref[pl.ds(pl.multiple_of(kj * t, t), t), :]
        pv = jnp.dot(a.astype(BF16), v, preferred_element_type=F32)
        acc = jnp.where(first, pv, acc_ref[stream] + pv)
        acc_ref[stream] = acc
        c_ref[stream] = c + incl[:, 0:1]
        o_ref[pl.ds(pl.multiple_of(qi * t, t), t), :] = (
            jnp.where(head0, acc[0:t], acc[t:2 * t]).astype(o_ref.dtype))

    def block(carry):
        qs, ks, qn, kn, skip_q, drained = carry
        qs, ks, qn, kn, skip_q = list(qs), list(ks), list(qn), list(kn), list(skip_q)
        all_done = qn[0] >= nq
        for p in range(1, ATTN_STREAMS):
            all_done = jnp.logical_and(all_done, qn[p] >= nq)
        drained = drained + all_done.astype(jnp.int32)
        for r in range(ATTN_STAGES * ATTN_STREAMS):
            sa, sc, sb = r % ATTN_STAGES, (r + 1) % ATTN_STAGES, (r + 2) % ATTN_STAGES
            pa, pb = r % ATTN_STREAMS, (r - 1) % ATTN_STREAMS
            stage_c(qs[sc], ks[sc], sc, pa)
            total_b = stage_b(sb)
            carry_b = jnp.where(qs[sb] == ks[sb], 0.0, c_ref[pb]) + total_b
            skip = skip_q[pa] == qn[pa]
            q_cur = jnp.where(skip, qn[pa] + ATTN_STREAMS, qn[pa])
            k_cur = jnp.where(skip, qn[pa] + ATTN_STREAMS, kn[pa])
            drain = q_cur >= nq
            qa = jnp.where(drain, 0, q_cur)
            ka = jnp.where(drain, 0, k_cur)
            stage_a(qa, ka, sa)
            last = k_cur == 0
            qn[pa] = jnp.where(jnp.logical_and(last, jnp.logical_not(drain)),
                               q_cur + ATTN_STREAMS, q_cur)
            kn[pa] = jnp.where(drain, k_cur, jnp.where(last, q_cur + ATTN_STREAMS, k_cur - 1))
            skip_q[pb] = jnp.where(jnp.min(carry_b) >= ATTN_SKIP_LOG2, qs[sb], skip_q[pb])
            qs[sa], ks[sa] = qa, ka
        return tuple(qs), tuple(ks), tuple(qn), tuple(kn), tuple(skip_q), drained

    zero = jnp.int32(0)
    first_q = tuple(jnp.int32(p) for p in range(ATTN_STREAMS))
    init = ((zero,) * ATTN_STAGES, (zero,) * ATTN_STAGES, first_q, first_q,
            (jnp.int32(-1),) * ATTN_STREAMS, zero)
    lax.while_loop(lambda carry: carry[5] < 1, block, init)


def _attn(qkv, bsz, seq, tri, bias):
    n = bsz * seq
    t = min(ATTN_T, seq)
    pairs = SB_HEADS * SB_HEAD_DIM // LANES
    kern = functools.partial(_attn_kernel, t=t, nq=seq // t)
    return pl.pallas_call(
        kern,
        grid=(bsz, pairs),
        in_specs=[
            pl.BlockSpec((seq, LANES), lambda b, p: (b, p)),
            pl.BlockSpec((seq, LANES), lambda b, p: (b, pairs + p)),
            pl.BlockSpec((seq, LANES), lambda b, p: (b, 2 * pairs + p)),
            pl.BlockSpec((t, t), lambda b, p: (0, 0)),
            pl.BlockSpec((2, t, t), lambda b, p: (0, 0, 0)),
        ],
        out_specs=pl.BlockSpec((seq, LANES), lambda b, p: (b, p)),
        out_shape=jax.ShapeDtypeStruct((n, pairs * LANES), BF16),
        scratch_shapes=([pltpu.VMEM((2 * t, t), F32)] * (2 * ATTN_STAGES)
                        + [pltpu.VMEM((ATTN_STREAMS, 2 * t, LANES), F32),
                           pltpu.VMEM((ATTN_STREAMS, 2 * t, 1), F32)]),
        compiler_params=_params(("parallel", "parallel")),
        name="attn",
    )(qkv, qkv, qkv, tri, bias)


def _group_dot(x, bd):
    g = bd.shape[0]
    parts = [jnp.dot(x[:, i:i + g], bd, preferred_element_type=F32)
             for i in range(0, x.shape[1], g)]
    return jnp.concatenate(parts, axis=1)


def _project_ahead(x_ref, xn_ref, ln_ref, w_ref, cur_scr, nxt_scr, first, pieces):
    @pl.when(first)
    def _():
        nxt_scr[...] = jnp.dot(_rmsnorm_bf16(x_ref[...], ln_ref[...]), w_ref[...],
                               preferred_element_type=F32)

    cur_scr[...] = nxt_scr[...]
    h_next = _rmsnorm_bf16(xn_ref[...], ln_ref[...])
    slab = w_ref.shape[1] // pieces

    def piece(j):
        def run():
            cols = slice(j * slab, (j + 1) * slab)
            nxt_scr[:, cols] = jnp.dot(h_next, w_ref[:, cols], preferred_element_type=F32)
        return run

    return [piece(j) for j in range(pieces)]


def _hgrn_kernel(x_ref, xn_ref, ln_ref, w_ref, lbl_ref, ng_ref, tri_ref, bd_ref, o_ref,
                 st_scr, wpad, vpad, wsh, vsh, cur_scr, nxt_scr, *, tt, c, layer):
    ti = pl.program_id(1)
    w = o_ref.shape[1]
    nch = tt // c
    ahead = _project_ahead(x_ref, xn_ref, ln_ref, w_ref, cur_scr, nxt_scr, ti == 0,
                           pieces=2 * (c // SUBLANES))
    q_in, f_in, v, g_in = (cur_scr[:, j * w:(j + 1) * w] for j in range(4))

    @pl.when(ti == 0)
    def _():
        st_scr[...] = jnp.zeros_like(st_scr)

    lg = lbl_ref[...]
    e = jnp.exp(lg - jnp.max(lg, axis=0, keepdims=True))
    lb = jnp.sum(e[0:layer + 1], axis=0, keepdims=True) / jnp.sum(e, axis=0, keepdims=True)

    f = lb + (1.0 - lb) * jax.nn.sigmoid(f_in)
    kk = 1.0 - f
    qs = q_in * jax.nn.sigmoid(q_in)
    bd = bd_ref[...]
    cum = _split_dot_left(tri_ref[...], jnp.log(f) * LOG2E, 3)
    wk = jnp.log(kk) * LOG2E - cum

    pos = lax.broadcasted_iota(jnp.int32, (tt, 1), 0) % c
    wpad[0:SUBLANES, :] = jnp.zeros((SUBLANES, w), F32)
    vpad[0:SUBLANES, :] = jnp.zeros((SUBLANES, w), F32)
    wpad[SUBLANES:SUBLANES + tt, :] = wk
    vpad[SUBLANES:SUBLANES + tt, :] = v
    for b in range(SUBLANES):
        wb = wpad[SUBLANES - b:SUBLANES - b + tt, :]
        vb = vpad[SUBLANES - b:SUBLANES - b + tt, :]
        if b:
            wb = jnp.where(pos >= b, wb, -jnp.inf)
        wsh[b] = wb.reshape(nch, c, w)
        vsh[b] = vb.reshape(nch, c, w)

    qs3 = qs.reshape(nch, c, w)
    cum3 = cum.reshape(nch, c, w)
    acc = None
    for a in reversed(range(c // SUBLANES)):
        rows = c - SUBLANES * a
        qa = qs3[:, SUBLANES * a:, :].reshape(nch * rows, w)
        ca = cum3[:, SUBLANES * a:, :].reshape(nch * rows, w)
        acc_a = None
        for b in range(SUBLANES):
            if b % (SUBLANES // 2) == 0:
                ahead.pop()()
            wb = wsh[b, :, 0:rows, :].reshape(nch * rows, w)
            vb = vsh[b, :, 0:rows, :].reshape(nch * rows, w)
            dd = qa * jnp.exp2(ca + wb)
            term = _group_dot(dd.astype(BF16), bd) * vb
            acc_a = term if acc_a is None else acc_a + term
        acc_a = acc_a.reshape(nch, rows, w)
        if acc is not None:
            acc_a = acc_a + jnp.concatenate([jnp.zeros((nch, SUBLANES, w), F32), acc], axis=1)
        acc = acc_a
    acc = acc.reshape(tt, w)
    assert not ahead

    grp = st_scr.shape[1]
    bdmask = bd[0:grp, 0:grp] != 0
    outs = []
    for ci in range(nch):
        r0 = ci * c
        cum_c = cum[r0:r0 + c]
        last = cum_c[c - 1:c]
        qd = (qs[r0:r0 + c] * jnp.exp2(cum_c)).astype(BF16)
        kd = (kk[r0:r0 + c] * jnp.exp2(last - cum_c)).astype(BF16)
        vc = v[r0:r0 + c].astype(BF16)
        dec = jnp.exp2(last)
        o_parts = []
        for gi in range(w // grp):
            sl = slice(gi * grp, (gi + 1) * grp)
            st = st_scr[gi]
            o_parts.append(lax.dot_general(qd[:, sl], st.astype(BF16), (((1,), (1,)), ((), ())),
                                           preferred_element_type=F32))
            upd = lax.dot_general(vc[:, sl], kd[:, sl], (((0,), (0,)), ((), ())),
                                  preferred_element_type=F32)
            st_scr[gi] = st * dec[:, sl] + jnp.where(bdmask, upd, 0.0)
        outs.append(jnp.concatenate(o_parts, axis=1))
    o = acc + jnp.concatenate(outs, axis=0)

    o2 = o * o
    o2_hi = o2.astype(BF16)
    o2_lo = (o2 - o2_hi.astype(F32)).astype(BF16)
    ms = (_group_dot(o2_hi, bd) + _group_dot(o2_lo, bd)) * (1.0 / HG_DIM)
    o = o * lax.rsqrt(ms + EPS) * ng_ref[...] * (g_in * jax.nn.sigmoid(g_in))
    o_ref[...] = o.astype(o_ref.dtype)


def _split_dot_left(m, x, passes):
    acc = None
    r = x
    for p in range(passes):
        h = r.astype(BF16)
        term = jnp.dot(m, h, preferred_element_type=F32)
        acc = term if acc is None else acc + term
        if p + 1 < passes:
            r = r - h.astype(F32)
    return acc


def _hgrn(x2, ln_g, w_hg, lb_logits, norm_g, bsz, seq, layer, tri, bd):
    n, d = x2.shape
    w = HG_HEADS * HG_DIM
    assert w_hg.shape == (d, 4 * w)
    grp = bd.shape[0]
    tt = min(HGRN_TT, seq)
    c = min(HGRN_C, tt)
    nt = seq // tt
    kern = functools.partial(_hgrn_kernel, tt=tt, c=c, layer=layer)
    const = lambda shape: pl.BlockSpec(shape, lambda b, i: (0, 0))
    return pl.pallas_call(
        kern,
        grid=(bsz, nt),
        in_specs=[pl.BlockSpec((tt, d), lambda b, i: (b * nt + i, 0)),
                  pl.BlockSpec((tt, d), lambda b, i: (b * nt + jnp.minimum(i + 1, nt - 1), 0)),
                  const((1, d)), const(w_hg.shape),
                  const(lb_logits.shape), const((1, w)), const((tt, tt)), const((grp, grp))],
        out_specs=pl.BlockSpec((tt, w), lambda b, i: (b * nt + i, 0)),
        out_shape=jax.ShapeDtypeStruct((n, w), BF16),
        scratch_shapes=([pltpu.VMEM((w // LANES, LANES, LANES), F32)]
                        + [pltpu.VMEM((SUBLANES + tt, w), F32)] * 2
                        + [pltpu.VMEM((SUBLANES, tt // c, c, w), F32)] * 2
                        + [pltpu.VMEM((tt, 4 * w), F32)] * 2),
        compiler_params=_params(("parallel", "arbitrary")),
        name="hgrn",
    )(x2, x2, ln_g.reshape(1, d), w_hg, lb_logits, norm_g.reshape(1, w), tri, bd)


ROUTE_I1, ROUTE_I2, ROUTE_W1, ROUTE_W2 = N_EXPERTS, N_EXPERTS + 1, N_EXPERTS + 2, N_EXPERTS + 3


def _route(lg):
    lane = lax.broadcasted_iota(jnp.int32, lg.shape, 1)
    neg = jnp.float32(-jnp.inf)
    big = jnp.int32(LANES)
    gmask = jnp.logical_and(lane >= N_EXPERTS, lane < N_EXPERTS + N_GROUPS)
    gl = jnp.where(gmask, lg, neg)
    gmax = jnp.max(gl, axis=1, keepdims=True)
    gidx = jnp.min(jnp.where(gl == gmax, lane, big), axis=1, keepdims=True) - N_EXPERTS
    w_grp = 1.0 / jnp.sum(jnp.where(gmask, jnp.exp(gl - gmax), 0.0), axis=1, keepdims=True)
    in_grp = jnp.logical_and(lane < N_EXPERTS, lane // EXPERTS_PER_GROUP == gidx)
    l1 = jnp.where(in_grp, lg, neg)
    v1 = jnp.max(l1, axis=1, keepdims=True)
    i1 = jnp.min(jnp.where(l1 == v1, lane, big), axis=1, keepdims=True)
    l2 = jnp.where(jnp.logical_and(in_grp, lane != i1), lg, neg)
    v2 = jnp.max(l2, axis=1, keepdims=True)
    i2 = jnp.min(jnp.where(l2 == v2, lane, big), axis=1, keepdims=True)
    e2 = jnp.exp(v2 - v1)
    p1 = 1.0 / (1.0 + e2)
    p2 = e2 * p1
    out = jnp.where(jnp.logical_or(lane == i1, lane == i2), 1.0, 0.0)
    out = jnp.where(lane == ROUTE_I1, i1.astype(F32), out)
    out = jnp.where(lane == ROUTE_I2, i2.astype(F32), out)
    out = jnp.where(lane == ROUTE_W1, p1 * w_grp, out)
    return jnp.where(lane == ROUTE_W2, p2 * w_grp, out)


def _store_token_tiles(ref, x):
    rows, d = x.shape
    assert d == SUBLANES * LANES
    for s in range(SUBLANES):
        ref[pl.ds(s, rows, stride=SUBLANES), :] = x[:, s * LANES:(s + 1) * LANES]


def _load_token_tiles(ref):
    rows = ref.shape[0] // SUBLANES
    return jnp.concatenate([ref[pl.ds(s, rows, stride=SUBLANES), :] for s in range(SUBLANES)],
                           axis=1)


def _merge_kernel(x_ref, xn_ref, ysb_ref, yhg_ref, ln1_ref, wgate_ref, wbs_ref, wbh_ref, wo_ref,
                  ln_ref, wr_ref, br_ref, x1_ref, t_ref, route_ref, cur_scr, nxt_scr):
    x = x_ref[...]
    d = x.shape[1]
    ahead = _project_ahead(x_ref, xn_ref, ln1_ref, wgate_ref, cur_scr, nxt_scr,
                           pl.program_id(0) == 0, pieces=4)
    a = jnp.dot(ysb_ref[...], wbs_ref[...], preferred_element_type=F32)
    ahead.pop()()
    b = jnp.dot(yhg_ref[...], wbh_ref[...], preferred_element_type=F32)
    ahead.pop()()
    merged = jax.nn.sigmoid(cur_scr[:, :d]) * a + jax.nn.sigmoid(cur_scr[:, d:]) * b
    x1 = x + jnp.dot(merged.astype(BF16), wo_ref[...], preferred_element_type=F32)
    ahead.pop()()
    x1_ref[...] = x1
    var = jnp.mean(x1 * x1, axis=-1, keepdims=True)
    t = x1 * lax.rsqrt(var + EPS) * ln_ref[...]
    _store_token_tiles(t_ref, t)
    t_hi = t.astype(BF16)
    t_lo = (t - t_hi.astype(F32)).astype(BF16)
    p_hi = jnp.dot(t_hi, wr_ref[...], preferred_element_type=F32)
    p_lo = jnp.dot(t_lo, wr_ref[...], preferred_element_type=F32)
    ahead.pop()()
    assert not ahead
    lg = (p_hi[:, :LANES] + p_hi[:, LANES:]) + (p_lo[:, :LANES] + p_lo[:, LANES:]) + br_ref[...]
    route_ref[...] = _route(lg)


def _merge(x2, y_sb, y_hg, ln1_g, w_gate, wbs, wbh, wo, ln_g, wr, br):
    n, d = x2.shape
    assert w_gate.shape == (d, 2 * d)
    tm = min(MERGE_TM, n)
    w_sb, w_hg = y_sb.shape[1], y_hg.shape[1]
    row = lambda wdt: pl.BlockSpec((tm, wdt), lambda i: (i, 0))
    const = lambda shape: pl.BlockSpec(shape, lambda i: (0, 0))
    steps = n // tm
    return pl.pallas_call(
        _merge_kernel,
        grid=(steps,),
        in_specs=[row(d), pl.BlockSpec((tm, d), lambda i: (jnp.minimum(i + 1, steps - 1), 0)),
                  row(w_sb), row(w_hg), const((1, d)), const(w_gate.shape),
                  const(wbs.shape), const(wbh.shape), const(wo.shape), const((1, d)),
                  const(wr.shape), const((1, LANES))],
        out_specs=[row(d), pl.BlockSpec((tm * SUBLANES, LANES), lambda i: (i, 0)), row(LANES)],
        out_shape=[jax.ShapeDtypeStruct((n, d), F32),
                   jax.ShapeDtypeStruct((n * SUBLANES, LANES), F32),
                   jax.ShapeDtypeStruct((n, LANES), F32)],
        scratch_shapes=[pltpu.VMEM((tm, 2 * d), F32)] * 2,
        compiler_params=_params(("arbitrary",)),
        name="merge",
    )(x2, x2, y_sb, y_hg, ln1_g.reshape(1, d), w_gate, wbs, wbh, wo, ln_g.reshape(1, d), wr, br)


def _plan_kernel(route_ref, ltri_ref, utri_ref, pos_ref, te_ref, meta_ref, tot_scr, run_scr,
                 *, tile_rows):
    ph = pl.program_id(0)
    i = pl.program_id(1)
    route = route_ref[...]
    tm = route.shape[0]
    lane = lax.broadcasted_iota(jnp.int32, (tm, LANES), 1)
    sel = jnp.where(lane < N_EXPERTS, route, 0.0)

    @pl.when(jnp.logical_and(ph == 0, i == 0))
    def _():
        tot_scr[...] = jnp.zeros_like(tot_scr)

    @pl.when(ph == 0)
    def _():
        tot_scr[...] += jnp.sum(sel, axis=0, keepdims=True)

    @pl.when(ph == 1)
    def _():
        tot = tot_scr[...]
        padded = jnp.ceil(tot * (1.0 / tile_rows)) * tile_rows
        off = _split_dot(jnp.broadcast_to(padded, (SUBLANES, LANES)), utri_ref[...], 3)[0:1]

        @pl.when(i == 0)
        def _():
            run_scr[...] = jnp.zeros_like(run_scr)
            lane1 = lax.broadcasted_iota(jnp.int32, (1, LANES), 1)
            is_e = lane1 < N_EXPERTS
            end = off + padded
            n_valid = (jnp.sum(jnp.where(is_e, padded, 0.0), axis=1, keepdims=True)
                       * (1.0 / tile_rows))
            rows = te_ref.shape[0]
            j = lax.broadcasted_iota(jnp.int32, (rows, LANES), 0).astype(F32)
            start = jnp.minimum(j, n_valid - 1.0) * tile_rows
            lane2 = lax.broadcasted_iota(jnp.int32, (rows, LANES), 1)
            done = jnp.where(jnp.logical_and(lane2 < N_EXPERTS, end <= start), 1.0, 0.0)
            te = jnp.sum(done, axis=1, keepdims=True)
            te_ref[...] = jnp.broadcast_to(te, (rows, LANES)).astype(jnp.int32)
            last_tile = jnp.where(jnp.logical_and(is_e, padded > 0),
                                  end * (1.0 / tile_rows) - 1.0, -1.0)
            meta = jnp.where(lane1 == N_EXPERTS, n_valid, last_tile)
            meta_ref[...] = jnp.broadcast_to(meta, meta_ref.shape).astype(jnp.int32)

        rank = jnp.dot(ltri_ref[...], sel.astype(BF16), preferred_element_type=F32) + run_scr[...]
        dest = off + rank
        lane_f = lane.astype(F32)
        p1 = jnp.sum(jnp.where(lane_f == route[:, ROUTE_I1:ROUTE_I1 + 1], dest, 0.0),
                     axis=1, keepdims=True)
        p2 = jnp.sum(jnp.where(lane_f == route[:, ROUTE_I2:ROUTE_I2 + 1], dest, 0.0),
                     axis=1, keepdims=True)
        pos_ref[...] = jnp.where(lane == 0, p1, jnp.where(lane == 1, p2, 0.0)).astype(jnp.int32)
        run_scr[...] += jnp.sum(sel, axis=0, keepdims=True)


def _plan(route, n_tiles_max):
    n = route.shape[0]
    tm = min(PLAN_TM, n)
    r = np.arange(tm)
    ltri = jnp.asarray((r[None, :] < r[:, None]).astype(np.float32), dtype=BF16)
    e = np.arange(LANES)
    utri = jnp.asarray((e[:, None] < e[None, :]).astype(np.float32), dtype=BF16)
    te_rows = -(-n_tiles_max // SUBLANES) * SUBLANES
    kern = functools.partial(_plan_kernel, tile_rows=MOE_TM)
    return pl.pallas_call(
        kern,
        grid=(2, n // tm),
        in_specs=[pl.BlockSpec((tm, LANES), lambda p, i: (i, 0)),
                  pl.BlockSpec((tm, tm), lambda p, i: (0, 0)),
                  pl.BlockSpec((LANES, LANES), lambda p, i: (0, 0))],
        out_specs=[pl.BlockSpec((tm, LANES), lambda p, i: (i * p, 0)),
                   pl.BlockSpec((te_rows, LANES), lambda p, i: (0, 0)),
                   pl.BlockSpec((SUBLANES, LANES), lambda p, i: (0, 0))],
        out_shape=[jax.ShapeDtypeStruct((n, LANES), jnp.int32),
                   jax.ShapeDtypeStruct((te_rows, LANES), jnp.int32),
                   jax.ShapeDtypeStruct((SUBLANES, LANES), jnp.int32)],
        scratch_shapes=[pltpu.VMEM((1, LANES), F32), pltpu.VMEM((1, LANES), F32)],
        compiler_params=_params(("arbitrary", "arbitrary")),
        name="plan",
    )(route, ltri, utri)


DMA_UNROLL = 8


def _row_copy(src_ref, r, dst_ref, p, sem):
    return pltpu.make_async_copy(src_ref.at[pl.ds(pl.multiple_of(r * SUBLANES, SUBLANES), SUBLANES)],
                                 dst_ref.at[pl.ds(pl.multiple_of(p * SUBLANES, SUBLANES), SUBLANES)],
                                 sem)


def _dispatch_kernel(pos1_ref, pos2_ref, meta_ref, t_ref, xs_ref, zero_scr, sem, *, tm, tile_rows):
    i = pl.program_id(0)
    tile_rows = tile_rows * SUBLANES
    n_tiles = xs_ref.shape[0] // tile_rows

    @pl.when(i == 0)
    def _():
        zero_scr[...] = jnp.zeros_like(zero_scr)
        n_valid = meta_ref[N_EXPERTS]
        clears = [(meta_ref[e], meta_ref[e] >= 0) for e in range(N_EXPERTS)]
        clears += [(n_tiles - 1 - k, n_tiles - 1 - k >= n_valid) for k in range(N_EXPERTS)]

        def clear(tile):
            return pltpu.make_async_copy(
                zero_scr, xs_ref.at[pl.ds(tile * tile_rows, tile_rows)], sem)

        for tile, cond in clears:
            @pl.when(cond)
            def _():
                clear(tile).start()
        for tile, cond in clears:
            @pl.when(cond)
            def _():
                clear(tile).wait()

    base = i * tm

    def start(r, carry):
        _row_copy(t_ref, r, xs_ref, pos1_ref[base + r], sem).start(priority=0)
        _row_copy(t_ref, r, xs_ref, pos2_ref[base + r], sem).start(priority=1)
        return carry

    def wait(r, carry):
        _row_copy(t_ref, r, xs_ref, pos1_ref[base + r], sem).wait()
        _row_copy(t_ref, r, xs_ref, pos2_ref[base + r], sem).wait()
        return carry

    lax.fori_loop(0, tm, start, 0, unroll=DMA_UNROLL)
    lax.fori_loop(0, tm, wait, 0, unroll=DMA_UNROLL)


def _dispatch(pos1, pos2, meta, t, n_rows):
    n = t.shape[0] // SUBLANES
    tm = min(DISPATCH_TM, n)
    kern = functools.partial(_dispatch_kernel, tm=tm, tile_rows=MOE_TM)
    grid_spec = pltpu.PrefetchScalarGridSpec(
        num_scalar_prefetch=3,
        grid=(n // tm,),
        in_specs=[pl.BlockSpec((tm * SUBLANES, LANES), lambda i, *_: (i, 0))],
        out_specs=pl.BlockSpec(memory_space=pl.ANY),
        scratch_shapes=[pltpu.VMEM((MOE_TM * SUBLANES, LANES), F32), pltpu.SemaphoreType.DMA(())],
    )
    return pl.pallas_call(
        kern,
        grid_spec=grid_spec,
        out_shape=jax.ShapeDtypeStruct((n_rows * SUBLANES, LANES), F32),
        compiler_params=_params(("arbitrary",)),
        name="dispatch",
    )(pos1, pos2, meta, t)


def _expert_kernel(te_ref, nv_ref, xs_ref, wg_ref, wu_ref, wd_ref, ys_ref):
    in_use = pl.program_id(0) < nv_ref[0]

    @pl.when(in_use)
    def _():
        x = _load_token_tiles(xs_ref).astype(BF16)
        hg = jnp.dot(x, wg_ref[...].astype(BF16), preferred_element_type=F32)
        hu = jnp.dot(x, wu_ref[...].astype(BF16), preferred_element_type=F32)
        a = (hg * jax.nn.sigmoid(hg) * hu).astype(BF16)
        _store_token_tiles(ys_ref, jnp.dot(a, wd_ref[...].astype(BF16),
                                           preferred_element_type=F32))

    @pl.when(jnp.logical_not(in_use))
    def _():
        ys_ref[...] = jnp.zeros_like(ys_ref)


def _experts(te, nv, xs, wg, wu, wd):
    _, d, de = wg.shape
    assert d == SUBLANES * LANES
    tile = lambda j, te, nv: (j, 0)
    grid_spec = pltpu.PrefetchScalarGridSpec(
        num_scalar_prefetch=2,
        grid=(xs.shape[0] // (MOE_TM * SUBLANES),),
        in_specs=[pl.BlockSpec((MOE_TM * SUBLANES, LANES), tile),
                  pl.BlockSpec((None, d, de), lambda j, te, nv: (te[j], 0, 0)),
                  pl.BlockSpec((None, d, de), lambda j, te, nv: (te[j], 0, 0)),
                  pl.BlockSpec((None, de, d), lambda j, te, nv: (te[j], 0, 0))],
        out_specs=pl.BlockSpec((MOE_TM * SUBLANES, LANES), tile),
    )
    return pl.pallas_call(
        _expert_kernel,
        grid_spec=grid_spec,
        out_shape=jax.ShapeDtypeStruct(xs.shape, F32),
        compiler_params=_params(("arbitrary",)),
        name="experts",
    )(te, nv, xs, wg, wu, wd)


def _combine_kernel(pos1_ref, pos2_ref, ys_ref, x1_ref, route_ref, fg_ref, o_ref, g1, g2, sem,
                    *, tm):
    i = pl.program_id(0)
    slot = i % 2

    def gather(step, buf, wait):
        base = step * tm

        def body(r, carry):
            c1 = _row_copy(ys_ref, pos1_ref[base + r], g1.at[buf], r, sem.at[buf])
            c2 = _row_copy(ys_ref, pos2_ref[base + r], g2.at[buf], r, sem.at[buf])
            if wait:
                c1.wait()
                c2.wait()
            else:
                c1.start(priority=0)
                c2.start(priority=1)
            return carry

        lax.fori_loop(0, tm, body, 0, unroll=DMA_UNROLL)

    @pl.when(i == 0)
    def _():
        gather(0, 0, wait=False)

    @pl.when(i + 1 < pl.num_programs(0))
    def _():
        gather(i + 1, 1 - slot, wait=False)

    gather(i, slot, wait=True)
    route = route_ref[...]
    w1 = route[:, ROUTE_W1:ROUTE_W1 + 1]
    w2 = route[:, ROUTE_W2:ROUTE_W2 + 1]
    x2 = x1_ref[...] + (w1 * _load_token_tiles(g1.at[slot]) + w2 * _load_token_tiles(g2.at[slot]))
    var = jnp.mean(x2 * x2, axis=-1, keepdims=True)
    o_ref[...] = x2 * lax.rsqrt(var + EPS) * fg_ref[...]


def _combine(pos1, pos2, ys, x1, route, final_g):
    n, d = x1.shape
    tm = min(COMBINE_TM, n)
    kern = functools.partial(_combine_kernel, tm=tm)
    grid_spec = pltpu.PrefetchScalarGridSpec(
        num_scalar_prefetch=2,
        grid=(n // tm,),
        in_specs=[pl.BlockSpec(memory_space=pl.ANY),
                  pl.BlockSpec((tm, d), lambda i, *_: (i, 0)),
                  pl.BlockSpec((tm, LANES), lambda i, *_: (i, 0)),
                  pl.BlockSpec((1, d), lambda i, *_: (0, 0))],
        out_specs=pl.BlockSpec((tm, d), lambda i, *_: (i, 0)),
        scratch_shapes=([pltpu.VMEM((2, tm * SUBLANES, LANES), F32)] * 2
                        + [pltpu.SemaphoreType.DMA((2,))]),
    )
    return pl.pallas_call(
        kern,
        grid_spec=grid_spec,
        out_shape=jax.ShapeDtypeStruct((n, d), F32),
        compiler_params=_params(("arbitrary",)),
        name="combine",
    )(pos1, pos2, ys, x1, route, final_g.reshape(1, d))


def _moe_sparse(t, route, wg, wu, wd, x1, final_g):
    n = x1.shape[0]
    n_tiles_max = 2 * n // MOE_TM + N_EXPERTS
    pos, te, meta = _plan(route, n_tiles_max)
    pos1, pos2 = pos[:, 0], pos[:, 1]
    xs = _dispatch(pos1, pos2, meta[0, :N_EXPERTS + 1], t, n_tiles_max * MOE_TM)
    ys = _experts(te[:n_tiles_max, 0], meta[0, N_EXPERTS:N_EXPERTS + 1], xs, wg, wu, wd)
    return _combine(pos1, pos2, ys, x1, route, final_g)


def _suffix_ones(t):
    j = np.arange(t)[:, None]
    s = np.arange(t)[None, :]
    return jnp.asarray((j >= s).astype(np.float32), dtype=BF16)


def _causal_bias(t):
    row = np.arange(t)[:, None]
    col = np.arange(t)[None, :]
    diag = np.where(col < row, 0.0, MASK_BIAS).astype(np.float32)
    return jnp.asarray(np.stack([np.zeros_like(diag), diag]))


def _chunk_prefix_ones(tt, c):
    t = np.arange(tt)[:, None]
    j = np.arange(tt)[None, :]
    return jnp.asarray(((j <= t) & (t // c == j // c)).astype(np.float32), dtype=BF16)


def _block_diag_ones(w, blk):
    a = np.arange(w)
    return jnp.asarray((a[:, None] // blk == a[None, :] // blk).astype(np.float32), dtype=BF16)


def kernel(x, ln1_g, w_in, w_branch_sb, w_branch_hg, hg_norm_g, hg_lb_logits, w_out, ln2_g,
           w_router_group, b_router_group, w_router_expert, b_router_expert,
           w_exp_gate, w_exp_up, w_exp_down, final_g):
    bsz, seq, d = x.shape
    depth = w_in.shape[0]
    n = bsz * seq
    sb_width = SB_HEADS * SB_HEAD_DIM
    hg_width = HG_HEADS * HG_DIM

    tri_attn = _suffix_ones(min(ATTN_T, seq))
    bias_attn = _causal_bias(min(ATTN_T, seq))
    tt = min(HGRN_TT, seq)
    tri_hg = _chunk_prefix_ones(tt, min(HGRN_C, tt))
    bd = _block_diag_ones(min(MXU_DIM, hg_width), HG_DIM)

    x2 = x.reshape(n, d)
    for l in range(depth):
        w_in_l = w_in[l].astype(BF16)
        c_hg, c_gate = 3 * sb_width, 3 * sb_width + 4 * hg_width
        assert w_in_l.shape[1] == c_gate + 2 * d
        qkv = _inproj(x2, ln1_g[l], w_in_l[:, :c_hg], sb_width)
        y_sb = _attn(qkv, bsz, seq, tri_attn, bias_attn)
        y_hg = _hgrn(x2, ln1_g[l], w_in_l[:, c_hg:c_gate], hg_lb_logits, hg_norm_g[l],
                     bsz, seq, l, tri_hg, bd)

        pad = LANES - N_EXPERTS - N_GROUPS
        wr = jnp.concatenate([w_router_expert[l], w_router_group[l],
                              jnp.zeros((d, pad), F32)], axis=1)
        wr_hi = wr.astype(BF16)
        wr_lo = (wr - wr_hi.astype(F32)).astype(BF16)
        wr_split = jnp.concatenate([wr_hi, wr_lo], axis=1)
        br = jnp.concatenate([b_router_expert[l], b_router_group[l],
                              jnp.zeros((pad,), F32)]).reshape(1, LANES)

        last = l == depth - 1
        x1, t, route = _merge(x2, y_sb, y_hg, ln1_g[l], w_in_l[:, c_gate:],
                              w_branch_sb[l].astype(BF16), w_branch_hg[l].astype(BF16),
                              w_out[l].astype(BF16), ln2_g[l], wr_split, br)
        assert last, "final rmsnorm is fused into the last layer's combine kernel"
        x2 = _moe_sparse(t, route, w_exp_gate[l], w_exp_up[l], w_exp_down[l], x1, final_g)
    return x2.reshape(bsz, seq, d)
```

```python
import functools

import jax
import jax.numpy as jnp
import numpy as np
from jax import lax
from jax.experimental import pallas as pl
from jax.experimental.pallas import tpu as pltpu

F32 = jnp.float32
BF16 = jnp.bfloat16

EPS = 1e-6
SB_HEADS = 8
SB_HEAD_DIM = 64
HG_HEADS = 8
HG_DIM = 64
N_GROUPS = 4
EXPERTS_PER_GROUP = 4
N_EXPERTS = N_GROUPS * EXPERTS_PER_GROUP

LANES = 128
SUBLANES = 8
MXU_DIM = 256
LOG2E = 1.4426950408889634
VMEM_LIMIT = 48 * 1024 * 1024

INPROJ_TM = 2048
INPROJ_TN = 512
ATTN_T = MXU_DIM
HGRN_TT = 256
HGRN_C = 32
MERGE_TM = 512
MOE_TM = 512
PLAN_TM = 1024
DISPATCH_TM = 1024
COMBINE_TM = 512


def _params(sem):
    return pltpu.CompilerParams(dimension_semantics=sem, vmem_limit_bytes=VMEM_LIMIT)


def _split_dot(x, m, passes):
    acc = None
    r = x
    for p in range(passes):
        h = r.astype(BF16)
        term = jnp.dot(h, m, preferred_element_type=F32)
        acc = term if acc is None else acc + term
        if p + 1 < passes:
            r = r - h.astype(F32)
    return acc


def _rmsnorm_bf16(x, g):
    var = jnp.mean(x * x, axis=-1, keepdims=True)
    return (x * lax.rsqrt(var + EPS) * g).astype(BF16)


def _inproj_kernel(x_ref, g_ref, w_ref, qkv_ref, rest_ref, h_scr, *, q_scale):
    j = pl.program_id(1)

    @pl.when(j == 0)
    def _():
        h_scr[...] = _rmsnorm_bf16(x_ref[...], g_ref[...])

    acc = jnp.dot(h_scr[...], w_ref[...], preferred_element_type=F32)
    qkv_ref[...] = (acc * jnp.where(j == 0, q_scale, 1.0)).astype(BF16)
    rest_ref[...] = acc


def _inproj(x2, ln_g, w, sb_width):
    n, d = x2.shape
    cols = w.shape[1]
    tm, tn = min(INPROJ_TM, n), INPROJ_TN
    assert sb_width == tn, "q block must be exactly one column tile"
    nq = 3 * sb_width // tn
    kern = functools.partial(_inproj_kernel, q_scale=SB_HEAD_DIM ** -0.5)
    return pl.pallas_call(
        kern,
        grid=(n // tm, cols // tn),
        in_specs=[
            pl.BlockSpec((tm, d), lambda i, j: (i, 0)),
            pl.BlockSpec((1, d), lambda i, j: (0, 0)),
            pl.BlockSpec((d, tn), lambda i, j: (0, j)),
        ],
        out_specs=[
            pl.BlockSpec((tm, tn), lambda i, j: (i, jnp.minimum(j, nq))),
            pl.BlockSpec((tm, tn), lambda i, j: (i, jnp.maximum(j - nq, 0))),
        ],
        out_shape=[
            jax.ShapeDtypeStruct((n, (nq + 1) * tn), BF16),
            jax.ShapeDtypeStruct((n, cols - nq * tn), F32),
        ],
        scratch_shapes=[pltpu.VMEM((tm, d), BF16)],
        compiler_params=_params(("parallel", "arbitrary")),
        name="inproj",
    )(x2, ln_g.reshape(1, d), w)


ATTN_STAGES = 3
ATTN_STREAMS = 2
MASK_BIAS = -1e30
ATTN_SKIP_LOG2 = 160.0


def _attn_kernel(q_ref, k_ref, v_ref, tri_ref, bias_ref, o_ref,
                 z0, z1, z2, i0, i1, i2, acc_ref, c_ref, *, t, nq):
    zbuf = (z0, z1, z2)
    ibuf = (i0, i1, i2)
    for r in zbuf + ibuf:
        r[...] = jnp.zeros_like(r)
    acc_ref[...] = jnp.zeros_like(acc_ref)
    c_ref[...] = jnp.zeros_like(c_ref)

    lane = lax.broadcasted_iota(jnp.int32, (t, LANES), 1)
    head0 = lane < SB_HEAD_DIM
    nt = (((1,), (1,)), ((), ()))

    def stage_a(qi, kj, slot):
        q = q_ref[pl.ds(pl.multiple_of(qi * t, t), t), :]
        zero = jnp.zeros_like(q)
        q2 = jnp.concatenate([jnp.where(head0, q, zero), jnp.where(head0, zero, q)], axis=0)
        k = k_ref[pl.ds(pl.multiple_of(kj * t, t), t), :]
        z = lax.dot_general(q2, k, nt, preferred_element_type=F32)
        bias = bias_ref[(qi == kj).astype(jnp.int32)]
        zbuf[slot][...] = z * LOG2E + jnp.concatenate([bias, bias], axis=0)

    def stage_b(slot):
        z = zbuf[slot][...]
        p = jnp.maximum(z, 0.0) + jnp.log(1.0 + jnp.exp2(-jnp.abs(z))) * LOG2E
        incl = jnp.dot(p.astype(BF16), tri_ref[...], preferred_element_type=F32)
        ibuf[slot][...] = incl
        return incl[:, 0:1]

    def stage_c(qi, kj, slot, stream):
        first = qi == kj
        incl = ibuf[slot][...]
        c = jnp.where(first, 0.0, c_ref[stream])
        a = jnp.exp2(zbuf[slot][...] - incl - c)
        v = v_ref[pl.ds(pl.multiple_of(kj * t, t), t), :]
        pv = jnp.dot(a.astype(BF16), v, preferred_element_type=F32)
        acc = jnp.where(first, pv, acc_ref[stream] + pv)
        acc_ref[stream] = acc
        c_ref[stream] = c + incl[:, 0:1]
        o_ref[pl.ds(pl.multiple_of(qi * t, t), t), :] = (
            jnp.where(head0, acc[0:t], acc[t:2 * t]).astype(o_ref.dtype))

    def block(carry):
        qs, ks, qn, kn, skip_q, drained = carry
        qs, ks, qn, kn, skip_q = list(qs), list(ks), list(qn), list(kn), list(skip_q)
        all_done = qn[0] >= nq
        for p in range(1, ATTN_STREAMS):
            all_done = jnp.logical_and(all_done, qn[p] >= nq)
        drained = drained + all_done.astype(jnp.int32)
        for r in range(ATTN_STAGES * ATTN_STREAMS):
            sa, sc, sb = r % ATTN_STAGES, (r + 1) % ATTN_STAGES, (r + 2) % ATTN_STAGES
            pa, pb = r % ATTN_STREAMS, (r - 1) % ATTN_STREAMS
            stage_c(qs[sc], ks[sc], sc, pa)
            total_b = stage_b(sb)
            carry_b = jnp.where(qs[sb] == ks[sb], 0.0, c_ref[pb]) + total_b
            skip = skip_q[pa] == qn[pa]
            q_cur = jnp.where(skip, qn[pa] + ATTN_STREAMS, qn[pa])
            k_cur = jnp.where(skip, qn[pa] + ATTN_STREAMS, kn[pa])
            drain = q_cur >= nq
            qa = jnp.where(drain, 0, q_cur)
            ka = jnp.where(drain, 0, k_cur)
            stage_a(qa, ka, sa)
            last = k_cur == 0
            qn[pa] = jnp.where(jnp.logical_and(last, jnp.logical_not(drain)),
                               q_cur + ATTN_STREAMS, q_cur)
            kn[pa] = jnp.where(drain, k_cur, jnp.where(last, q_cur + ATTN_STREAMS, k_cur - 1))
            skip_q[pb] = jnp.where(jnp.min(carry_b) >= ATTN_SKIP_LOG2, qs[sb], skip_q[pb])
            qs[sa], ks[sa] = qa, ka
        return tuple(qs), tuple(ks), tuple(qn), tuple(kn), tuple(skip_q), drained

    zero = jnp.int32(0)
    first_q = tuple(jnp.int32(p) for p in range(ATTN_STREAMS))
    init = ((zero,) * ATTN_STAGES, (zero,) * ATTN_STAGES, first_q, first_q,
            (jnp.int32(-1),) * ATTN_STREAMS, zero)
    lax.while_loop(lambda carry: carry[5] < 1, block, init)


def _attn(qkv, bsz, seq, tri, bias):
    n = bsz * seq
    t = min(ATTN_T, seq)
    pairs = SB_HEADS * SB_HEAD_DIM // LANES
    kern = functools.partial(_attn_kernel, t=t, nq=seq // t)
    return pl.pallas_call(
        kern,
        grid=(bsz, pairs),
        in_specs=[
            pl.BlockSpec((seq, LANES), lambda b, p: (b, p)),
            pl.BlockSpec((seq, LANES), lambda b, p: (b, pairs + p)),
            pl.BlockSpec((seq, LANES), lambda b, p: (b, 2 * pairs + p)),
            pl.BlockSpec((t, t), lambda b, p: (0, 0)),
            pl.BlockSpec((2, t, t), lambda b, p: (0, 0, 0)),
        ],
        out_specs=pl.BlockSpec((seq, LANES), lambda b, p: (b, p)),
        out_shape=jax.ShapeDtypeStruct((n, pairs * LANES), BF16),
        scratch_shapes=([pltpu.VMEM((2 * t, t), F32)] * (2 * ATTN_STAGES)
                        + [pltpu.VMEM((ATTN_STREAMS, 2 * t, LANES), F32),
                           pltpu.VMEM((ATTN_STREAMS, 2 * t, 1), F32)]),
        compiler_params=_params(("parallel", "parallel")),
        name="attn",
    )(qkv, qkv, qkv, tri, bias)


def _group_dot(x, bd):
    g = bd.shape[0]
    parts = [jnp.dot(x[:, i:i + g], bd, preferred_element_type=F32)
             for i in range(0, x.shape[1], g)]
    return jnp.concatenate(parts, axis=1)


def _project_ahead(x_ref, xn_ref, ln_ref, w_ref, cur_scr, nxt_scr, first, pieces):
    @pl.when(first)
    def _():
        nxt_scr[...] = jnp.dot(_rmsnorm_bf16(x_ref[...], ln_ref[...]), w_ref[...],
                               preferred_element_type=F32)

    cur_scr[...] = nxt_scr[...]
    h_next = _rmsnorm_bf16(xn_ref[...], ln_ref[...])
    slab = w_ref.shape[1] // pieces

    def piece(j):
        def run():
            cols = slice(j * slab, (j + 1) * slab)
            nxt_scr[:, cols] = jnp.dot(h_next, w_ref[:, cols], preferred_element_type=F32)
        return run

    return [piece(j) for j in range(pieces)]


def _hgrn_kernel(q_ref, f_ref, i_ref, g_ref, lbl_ref, ng_ref, tri_ref, bd_ref, o_ref,
                 st_scr, wpad, vpad, wsh, vsh, *, tt, c, layer):
    ti = pl.program_id(1)
    w = o_ref.shape[1]
    nch = tt // c
    q_in, f_in, v, g_in = q_ref[...], f_ref[...], i_ref[...], g_ref[...]

    @pl.when(ti == 0)
    def _():
        st_scr[...] = jnp.zeros_like(st_scr)

    lg = lbl_ref[...]
    e = jnp.exp(lg - jnp.max(lg, axis=0, keepdims=True))
    lb = jnp.sum(e[0:layer + 1], axis=0, keepdims=True) / jnp.sum(e, axis=0, keepdims=True)

    f = lb + (1.0 - lb) * jax.nn.sigmoid(f_in)
    kk = 1.0 - f
    qs = q_in * jax.nn.sigmoid(q_in)
    bd = bd_ref[...]
    cum = _split_dot_left(tri_ref[...], jnp.log(f) * LOG2E, 3)
    wk = jnp.log(kk) * LOG2E - cum

    pos = lax.broadcasted_iota(jnp.int32, (tt, 1), 0) % c
    wpad[0:SUBLANES, :] = jnp.zeros((SUBLANES, w), F32)
    vpad[0:SUBLANES, :] = jnp.zeros((SUBLANES, w), F32)
    wpad[SUBLANES:SUBLANES + tt, :] = wk
    vpad[SUBLANES:SUBLANES + tt, :] = v
    for b in range(SUBLANES):
        wb = wpad[SUBLANES - b:SUBLANES - b + tt, :]
        vb = vpad[SUBLANES - b:SUBLANES - b + tt, :]
        if b:
            wb = jnp.where(pos >= b, wb, -jnp.inf)
        wsh[b] = wb.reshape(nch, c, w)
        vsh[b] = vb.reshape(nch, c, w)

    qs3 = qs.reshape(nch, c, w)
    cum3 = cum.reshape(nch, c, w)
    acc = None
    for a in reversed(range(c // SUBLANES)):
        rows = c - SUBLANES * a
        qa = qs3[:, SUBLANES * a:, :].reshape(nch * rows, w)
        ca = cum3[:, SUBLANES * a:, :].reshape(nch * rows, w)
        acc_a = None
        for b in range(SUBLANES):
            wb = wsh[b, :, 0:rows, :].reshape(nch * rows, w)
            vb = vsh[b, :, 0:rows, :].reshape(nch * rows, w)
            dd = qa * jnp.exp2(ca + wb)
            term = _group_dot(dd.astype(BF16), bd) * vb
            acc_a = term if acc_a is None else acc_a + term
        acc_a = acc_a.reshape(nch, rows, w)
        if acc is not None:
            acc_a = acc_a + jnp.concatenate([jnp.zeros((nch, SUBLANES, w), F32), acc], axis=1)
        acc = acc_a
    acc = acc.reshape(tt, w)

    grp = st_scr.shape[1]
    bdmask = bd[0:grp, 0:grp] != 0
    outs = []
    for ci in range(nch):
        r0 = ci * c
        cum_c = cum[r0:r0 + c]
        last = cum_c[c - 1:c]
        qd = (qs[r0:r0 + c] * jnp.exp2(cum_c)).astype(BF16)
        kd = (kk[r0:r0 + c] * jnp.exp2(last - cum_c)).astype(BF16)
        vc = v[r0:r0 + c].astype(BF16)
        dec = jnp.exp2(last)
        o_parts = []
        for gi in range(w // grp):
            sl = slice(gi * grp, (gi + 1) * grp)
            st = st_scr[gi]
            o_parts.append(lax.dot_general(qd[:, sl], st.astype(BF16), (((1,), (1,)), ((), ())),
                                           preferred_element_type=F32))
            upd = lax.dot_general(vc[:, sl], kd[:, sl], (((0,), (0,)), ((), ())),
                                  preferred_element_type=F32)
            st_scr[gi] = st * dec[:, sl] + jnp.where(bdmask, upd, 0.0)
        outs.append(jnp.concatenate(o_parts, axis=1))
    o = acc + jnp.concatenate(outs, axis=0)

    o2 = o * o
    o2_hi = o2.astype(BF16)
    o2_lo = (o2 - o2_hi.astype(F32)).astype(BF16)
    ms = (_group_dot(o2_hi, bd) + _group_dot(o2_lo, bd)) * (1.0 / HG_DIM)
    o = o * lax.rsqrt(ms + EPS) * ng_ref[...] * (g_in * jax.nn.sigmoid(g_in))
    o_ref[...] = o.astype(o_ref.dtype)


def _split_dot_left(m, x, passes):
    acc = None
    r = x
    for p in range(passes):
        h = r.astype(BF16)
        term = jnp.dot(m, h, preferred_element_type=F32)
        acc = term if acc is None else acc + term
        if p + 1 < passes:
            r = r - h.astype(F32)
    return acc


def _hgrn(rest, lb_logits, norm_g, bsz, seq, layer, tri, bd):
    n = bsz * seq
    w = HG_HEADS * HG_DIM
    assert rest.shape == (n, 4 * w)
    grp = bd.shape[0]
    tt = min(HGRN_TT, seq)
    c = min(HGRN_C, tt)
    nt = seq // tt
    kern = functools.partial(_hgrn_kernel, tt=tt, c=c, layer=layer)
    col = lambda j: pl.BlockSpec((tt, w), lambda b, i, j=j: (b * nt + i, j))
    const = lambda shape: pl.BlockSpec(shape, lambda b, i: (0, 0))
    return pl.pallas_call(
        kern,
        grid=(bsz, nt),
        in_specs=[col(0), col(1), col(2), col(3),
                  const(lb_logits.shape), const((1, w)), const((tt, tt)), const((grp, grp))],
        out_specs=pl.BlockSpec((tt, w), lambda b, i: (b * nt + i, 0)),
        out_shape=jax.ShapeDtypeStruct((n, w), BF16),
        scratch_shapes=([pltpu.VMEM((w // LANES, LANES, LANES), F32)]
                        + [pltpu.VMEM((SUBLANES + tt, w), F32)] * 2
                        + [pltpu.VMEM((SUBLANES, tt // c, c, w), F32)] * 2),
        compiler_params=_params(("parallel", "arbitrary")),
        name="hgrn",
    )(rest, rest, rest, rest, lb_logits, norm_g.reshape(1, w), tri, bd)


ROUTE_I1, ROUTE_I2, ROUTE_W1, ROUTE_W2 = N_EXPERTS, N_EXPERTS + 1, N_EXPERTS + 2, N_EXPERTS + 3


def _route(lg):
    lane = lax.broadcasted_iota(jnp.int32, lg.shape, 1)
    neg = jnp.float32(-jnp.inf)
    big = jnp.int32(LANES)
    gmask = jnp.logical_and(lane >= N_EXPERTS, lane < N_EXPERTS + N_GROUPS)
    gl = jnp.where(gmask, lg, neg)
    gmax = jnp.max(gl, axis=1, keepdims=True)
    gidx = jnp.min(jnp.where(gl == gmax, lane, big), axis=1, keepdims=True) - N_EXPERTS
    w_grp = 1.0 / jnp.sum(jnp.where(gmask, jnp.exp(gl - gmax), 0.0), axis=1, keepdims=True)
    in_grp = jnp.logical_and(lane < N_EXPERTS, lane // EXPERTS_PER_GROUP == gidx)
    l1 = jnp.where(in_grp, lg, neg)
    v1 = jnp.max(l1, axis=1, keepdims=True)
    i1 = jnp.min(jnp.where(l1 == v1, lane, big), axis=1, keepdims=True)
    l2 = jnp.where(jnp.logical_and(in_grp, lane != i1), lg, neg)
    v2 = jnp.max(l2, axis=1, keepdims=True)
    i2 = jnp.min(jnp.where(l2 == v2, lane, big), axis=1, keepdims=True)
    e2 = jnp.exp(v2 - v1)
    p1 = 1.0 / (1.0 + e2)
    p2 = e2 * p1
    out = jnp.where(jnp.logical_or(lane == i1, lane == i2), 1.0, 0.0)
    out = jnp.where(lane == ROUTE_I1, i1.astype(F32), out)
    out = jnp.where(lane == ROUTE_I2, i2.astype(F32), out)
    out = jnp.where(lane == ROUTE_W1, p1 * w_grp, out)
    return jnp.where(lane == ROUTE_W2, p2 * w_grp, out)


def _store_token_tiles(ref, x):
    rows, d = x.shape
    assert d == SUBLANES * LANES
    for s in range(SUBLANES):
        ref[pl.ds(s, rows, stride=SUBLANES), :] = x[:, s * LANES:(s + 1) * LANES]


def _load_token_tiles(ref):
    rows = ref.shape[0] // SUBLANES
    return jnp.concatenate([ref[pl.ds(s, rows, stride=SUBLANES), :] for s in range(SUBLANES)],
                           axis=1)


def _merge_kernel(x_ref, xn_ref, ysb_ref, yhg_ref, ln1_ref, wgate_ref, wbs_ref, wbh_ref, wo_ref,
                  ln_ref, wr_ref, br_ref, x1_ref, t_ref, route_ref, cur_scr, nxt_scr):
    x = x_ref[...]
    d = x.shape[1]
    ahead = _project_ahead(x_ref, xn_ref, ln1_ref, wgate_ref, cur_scr, nxt_scr,
                           pl.program_id(0) == 0, pieces=4)
    a = jnp.dot(ysb_ref[...], wbs_ref[...], preferred_element_type=F32)
    ahead.pop()()
    b = jnp.dot(yhg_ref[...], wbh_ref[...], preferred_element_type=F32)
    ahead.pop()()
    merged = jax.nn.sigmoid(cur_scr[:, :d]) * a + jax.nn.sigmoid(cur_scr[:, d:]) * b
    x1 = x + jnp.dot(merged.astype(BF16), wo_ref[...], preferred_element_type=F32)
    ahead.pop()()
    x1_ref[...] = x1
    var = jnp.mean(x1 * x1, axis=-1, keepdims=True)
    t = x1 * lax.rsqrt(var + EPS) * ln_ref[...]
    _store_token_tiles(t_ref, t)
    t_hi = t.astype(BF16)
    t_lo = (t - t_hi.astype(F32)).astype(BF16)
    p_hi = jnp.dot(t_hi, wr_ref[...], preferred_element_type=F32)
    p_lo = jnp.dot(t_lo, wr_ref[...], preferred_element_type=F32)
    ahead.pop()()
    assert not ahead
    lg = (p_hi[:, :LANES] + p_hi[:, LANES:]) + (p_lo[:, :LANES] + p_lo[:, LANES:]) + br_ref[...]
    route_ref[...] = _route(lg)


def _merge(x2, y_sb, y_hg, ln1_g, w_gate, wbs, wbh, wo, ln_g, wr, br):
    n, d = x2.shape
    assert w_gate.shape == (d, 2 * d)
    tm = min(MERGE_TM, n)
    w_sb, w_hg = y_sb.shape[1], y_hg.shape[1]
    row = lambda wdt: pl.BlockSpec((tm, wdt), lambda i: (i, 0))
    const = lambda shape: pl.BlockSpec(shape, lambda i: (0, 0))
    steps = n // tm
    return pl.pallas_call(
        _merge_kernel,
        grid=(steps,),
        in_specs=[row(d), pl.BlockSpec((tm, d), lambda i: (jnp.minimum(i + 1, steps - 1), 0)),
                  row(w_sb), row(w_hg), const((1, d)), const(w_gate.shape),
                  const(wbs.shape), const(wbh.shape), const(wo.shape), const((1, d)),
                  const(wr.shape), const((1, LANES))],
        out_specs=[row(d), pl.BlockSpec((tm * SUBLANES, LANES), lambda i: (i, 0)), row(LANES)],
        out_shape=[jax.ShapeDtypeStruct((n, d), F32),
                   jax.ShapeDtypeStruct((n * SUBLANES, LANES), F32),
                   jax.ShapeDtypeStruct((n, LANES), F32)],
        scratch_shapes=[pltpu.VMEM((tm, 2 * d), F32)] * 2,
        compiler_params=_params(("arbitrary",)),
        name="merge",
    )(x2, x2, y_sb, y_hg, ln1_g.reshape(1, d), w_gate, wbs, wbh, wo, ln_g.reshape(1, d), wr, br)


def _plan_kernel(route_ref, ltri_ref, utri_ref, pos_ref, te_ref, meta_ref, tot_scr, run_scr,
                 *, tile_rows):
    ph = pl.program_id(0)
    i = pl.program_id(1)
    route = route_ref[...]
    tm = route.shape[0]
    lane = lax.broadcasted_iota(jnp.int32, (tm, LANES), 1)
    sel = jnp.where(lane < N_EXPERTS, route, 0.0)

    @pl.when(jnp.logical_and(ph == 0, i == 0))
    def _():
        tot_scr[...] = jnp.zeros_like(tot_scr)

    @pl.when(ph == 0)
    def _():
        tot_scr[...] += jnp.sum(sel, axis=0, keepdims=True)

    @pl.when(ph == 1)
    def _():
        tot = tot_scr[...]
        padded = jnp.ceil(tot * (1.0 / tile_rows)) * tile_rows
        off = _split_dot(jnp.broadcast_to(padded, (SUBLANES, LANES)), utri_ref[...], 3)[0:1]

        @pl.when(i == 0)
        def _():
            run_scr[...] = jnp.zeros_like(run_scr)
            lane1 = lax.broadcasted_iota(jnp.int32, (1, LANES), 1)
            is_e = lane1 < N_EXPERTS
            end = off + padded
            n_valid = (jnp.sum(jnp.where(is_e, padded, 0.0), axis=1, keepdims=True)
                       * (1.0 / tile_rows))
            rows = te_ref.shape[0]
            j = lax.broadcasted_iota(jnp.int32, (rows, LANES), 0).astype(F32)
            start = jnp.minimum(j, n_valid - 1.0) * tile_rows
            lane2 = lax.broadcasted_iota(jnp.int32, (rows, LANES), 1)
            done = jnp.where(jnp.logical_and(lane2 < N_EXPERTS, end <= start), 1.0, 0.0)
            te = jnp.sum(done, axis=1, keepdims=True)
            te_ref[...] = jnp.broadcast_to(te, (rows, LANES)).astype(jnp.int32)
            last_tile = jnp.where(jnp.logical_and(is_e, padded > 0),
                                  end * (1.0 / tile_rows) - 1.0, -1.0)
            meta = jnp.where(lane1 == N_EXPERTS, n_valid, last_tile)
            meta_ref[...] = jnp.broadcast_to(meta, meta_ref.shape).astype(jnp.int32)

        rank = jnp.dot(ltri_ref[...], sel.astype(BF16), preferred_element_type=F32) + run_scr[...]
        dest = off + rank
        lane_f = lane.astype(F32)
        p1 = jnp.sum(jnp.where(lane_f == route[:, ROUTE_I1:ROUTE_I1 + 1], dest, 0.0),
                     axis=1, keepdims=True)
        p2 = jnp.sum(jnp.where(lane_f == route[:, ROUTE_I2:ROUTE_I2 + 1], dest, 0.0),
                     axis=1, keepdims=True)
        pos_ref[...] = jnp.where(lane == 0, p1, jnp.where(lane == 1, p2, 0.0)).astype(jnp.int32)
        run_scr[...] += jnp.sum(sel, axis=0, keepdims=True)


def _plan(route, n_tiles_max):
    n = route.shape[0]
    tm = min(PLAN_TM, n)
    r = np.arange(tm)
    ltri = jnp.asarray((r[None, :] < r[:, None]).astype(np.float32), dtype=BF16)
    e = np.arange(LANES)
    utri = jnp.asarray((e[:, None] < e[None, :]).astype(np.float32), dtype=BF16)
    te_rows = -(-n_tiles_max // SUBLANES) * SUBLANES
    kern = functools.partial(_plan_kernel, tile_rows=MOE_TM)
    return pl.pallas_call(
        kern,
        grid=(2, n // tm),
        in_specs=[pl.BlockSpec((tm, LANES), lambda p, i: (i, 0)),
                  pl.BlockSpec((tm, tm), lambda p, i: (0, 0)),
                  pl.BlockSpec((LANES, LANES), lambda p, i: (0, 0))],
        out_specs=[pl.BlockSpec((tm, LANES), lambda p, i: (i * p, 0)),
                   pl.BlockSpec((te_rows, LANES), lambda p, i: (0, 0)),
                   pl.BlockSpec((SUBLANES, LANES), lambda p, i: (0, 0))],
        out_shape=[jax.ShapeDtypeStruct((n, LANES), jnp.int32),
                   jax.ShapeDtypeStruct((te_rows, LANES), jnp.int32),
                   jax.ShapeDtypeStruct((SUBLANES, LANES), jnp.int32)],
        scratch_shapes=[pltpu.VMEM((1, LANES), F32), pltpu.VMEM((1, LANES), F32)],
        compiler_params=_params(("arbitrary", "arbitrary")),
        name="plan",
    )(route, ltri, utri)


DMA_UNROLL = 8


def _row_copy(src_ref, r, dst_ref, p, sem):
    return pltpu.make_async_copy(src_ref.at[pl.ds(pl.multiple_of(r * SUBLANES, SUBLANES), SUBLANES)],
                                 dst_ref.at[pl.ds(pl.multiple_of(p * SUBLANES, SUBLANES), SUBLANES)],
                                 sem)


def _dispatch_kernel(pos1_ref, pos2_ref, meta_ref, t_ref, xs_ref, zero_scr, sem, *, tm, tile_rows):
    i = pl.program_id(0)
    tile_rows = tile_rows * SUBLANES
    n_tiles = xs_ref.shape[0] // tile_rows

    @pl.when(i == 0)
    def _():
        zero_scr[...] = jnp.zeros_like(zero_scr)
        n_valid = meta_ref[N_EXPERTS]
        clears = [(meta_ref[e], meta_ref[e] >= 0) for e in range(N_EXPERTS)]
        clears += [(n_tiles - 1 - k, n_tiles - 1 - k >= n_valid) for k in range(N_EXPERTS)]

        def clear(tile):
            return pltpu.make_async_copy(
                zero_scr, xs_ref.at[pl.ds(tile * tile_rows, tile_rows)], sem)

        for tile, cond in clears:
            @pl.when(cond)
            def _():
                clear(tile).start()
        for tile, cond in clears:
            @pl.when(cond)
            def _():
                clear(tile).wait()

    base = i * tm

    def start(r, carry):
        _row_copy(t_ref, r, xs_ref, pos1_ref[base + r], sem).start(priority=0)
        _row_copy(t_ref, r, xs_ref, pos2_ref[base + r], sem).start(priority=1)
        return carry

    def wait(r, carry):
        _row_copy(t_ref, r, xs_ref, pos1_ref[base + r], sem).wait()
        _row_copy(t_ref, r, xs_ref, pos2_ref[base + r], sem).wait()
        return carry

    lax.fori_loop(0, tm, start, 0, unroll=DMA_UNROLL)
    lax.fori_loop(0, tm, wait, 0, unroll=DMA_UNROLL)


def _dispatch(pos1, pos2, meta, t, n_rows):
    n = t.shape[0] // SUBLANES
    tm = min(DISPATCH_TM, n)
    kern = functools.partial(_dispatch_kernel, tm=tm, tile_rows=MOE_TM)
    grid_spec = pltpu.PrefetchScalarGridSpec(
        num_scalar_prefetch=3,
        grid=(n // tm,),
        in_specs=[pl.BlockSpec((tm * SUBLANES, LANES), lambda i, *_: (i, 0))],
        out_specs=pl.BlockSpec(memory_space=pl.ANY),
        scratch_shapes=[pltpu.VMEM((MOE_TM * SUBLANES, LANES), F32), pltpu.SemaphoreType.DMA(())],
    )
    return pl.pallas_call(
        kern,
        grid_spec=grid_spec,
        out_shape=jax.ShapeDtypeStruct((n_rows * SUBLANES, LANES), F32),
        compiler_params=_params(("arbitrary",)),
        name="dispatch",
    )(pos1, pos2, meta, t)


def _expert_kernel(te_ref, nv_ref, xs_ref, wg_ref, wu_ref, wd_ref, ys_ref):
    in_use = pl.program_id(0) < nv_ref[0]

    @pl.when(in_use)
    def _():
        x = _load_token_tiles(xs_ref).astype(BF16)
        hg = jnp.dot(x, wg_ref[...].astype(BF16), preferred_element_type=F32)
        hu = jnp.dot(x, wu_ref[...].astype(BF16), preferred_element_type=F32)
        a = (hg * jax.nn.sigmoid(hg) * hu).astype(BF16)
        _store_token_tiles(ys_ref, jnp.dot(a, wd_ref[...].astype(BF16),
                                           preferred_element_type=F32))

    @pl.when(jnp.logical_not(in_use))
    def _():
        ys_ref[...] = jnp.zeros_like(ys_ref)


def _experts(te, nv, xs, wg, wu, wd):
    _, d, de = wg.shape
    assert d == SUBLANES * LANES
    tile = lambda j, te, nv: (j, 0)
    grid_spec = pltpu.PrefetchScalarGridSpec(
        num_scalar_prefetch=2,
        grid=(xs.shape[0] // (MOE_TM * SUBLANES),),
        in_specs=[pl.BlockSpec((MOE_TM * SUBLANES, LANES), tile),
                  pl.BlockSpec((None, d, de), lambda j, te, nv: (te[j], 0, 0)),
                  pl.BlockSpec((None, d, de), lambda j, te, nv: (te[j], 0, 0)),
                  pl.BlockSpec((None, de, d), lambda j, te, nv: (te[j], 0, 0))],
        out_specs=pl.BlockSpec((MOE_TM * SUBLANES, LANES), tile),
    )
    return pl.pallas_call(
        _expert_kernel,
        grid_spec=grid_spec,
        out_shape=jax.ShapeDtypeStruct(xs.shape, F32),
        compiler_params=_params(("arbitrary",)),
        name="experts",
    )(te, nv, xs, wg, wu, wd)


def _combine_kernel(pos1_ref, pos2_ref, ys_ref, x1_ref, route_ref, fg_ref, o_ref, g1, g2, sem,
                    *, tm):
    i = pl.program_id(0)
    slot = i % 2

    def gather(step, buf, wait):
        base = step * tm

        def body(r, carry):
            c1 = _row_copy(ys_ref, pos1_ref[base + r], g1.at[buf], r, sem.at[buf])
            c2 = _row_copy(ys_ref, pos2_ref[base + r], g2.at[buf], r, sem.at[buf])
            if wait:
                c1.wait()
                c2.wait()
            else:
                c1.start(priority=0)
                c2.start(priority=1)
            return carry

        lax.fori_loop(0, tm, body, 0, unroll=DMA_UNROLL)

    @pl.when(i == 0)
    def _():
        gather(0, 0, wait=False)

    @pl.when(i + 1 < pl.num_programs(0))
    def _():
        gather(i + 1, 1 - slot, wait=False)

    gather(i, slot, wait=True)
    route = route_ref[...]
    w1 = route[:, ROUTE_W1:ROUTE_W1 + 1]
    w2 = route[:, ROUTE_W2:ROUTE_W2 + 1]
    x2 = x1_ref[...] + (w1 * _load_token_tiles(g1.at[slot]) + w2 * _load_token_tiles(g2.at[slot]))
    var = jnp.mean(x2 * x2, axis=-1, keepdims=True)
    o_ref[...] = x2 * lax.rsqrt(var + EPS) * fg_ref[...]


def _combine(pos1, pos2, ys, x1, route, final_g):
    n, d = x1.shape
    tm = min(COMBINE_TM, n)
    kern = functools.partial(_combine_kernel, tm=tm)
    grid_spec = pltpu.PrefetchScalarGridSpec(
        num_scalar_prefetch=2,
        grid=(n // tm,),
        in_specs=[pl.BlockSpec(memory_space=pl.ANY),
                  pl.BlockSpec((tm, d), lambda i, *_: (i, 0)),
                  pl.BlockSpec((tm, LANES), lambda i, *_: (i, 0)),
                  pl.BlockSpec((1, d), lambda i, *_: (0, 0))],
        out_specs=pl.BlockSpec((tm, d), lambda i, *_: (i, 0)),
        scratch_shapes=([pltpu.VMEM((2, tm * SUBLANES, LANES), F32)] * 2
                        + [pltpu.SemaphoreType.DMA((2,))]),
    )
    return pl.pallas_call(
        kern,
        grid_spec=grid_spec,
        out_shape=jax.ShapeDtypeStruct((n, d), F32),
        compiler_params=_params(("arbitrary",)),
        name="combine",
    )(pos1, pos2, ys, x1, route, final_g.reshape(1, d))


def _moe_sparse(t, route, wg, wu, wd, x1, final_g):
    n = x1.shape[0]
    n_tiles_max = 2 * n // MOE_TM + N_EXPERTS
    pos, te, meta = _plan(route, n_tiles_max)
    pos1, pos2 = pos[:, 0], pos[:, 1]
    xs = _dispatch(pos1, pos2, meta[0, :N_EXPERTS + 1], t, n_tiles_max * MOE_TM)
    ys = _experts(te[:n_tiles_max, 0], meta[0, N_EXPERTS:N_EXPERTS + 1], xs, wg, wu, wd)
    return _combine(pos1, pos2, ys, x1, route, final_g)


def _suffix_ones(t):
    j = np.arange(t)[:, None]
    s = np.arange(t)[None, :]
    return jnp.asarray((j >= s).astype(np.float32), dtype=BF16)


def _causal_bias(t):
    row = np.arange(t)[:, None]
    col = np.arange(t)[None, :]
    diag = np.where(col < row, 0.0, MASK_BIAS).astype(np.float32)
    return jnp.asarray(np.stack([np.zeros_like(diag), diag]))


def _chunk_prefix_ones(tt, c):
    t = np.arange(tt)[:, None]
    j = np.arange(tt)[None, :]
    return jnp.asarray(((j <= t) & (t // c == j // c)).astype(np.float32), dtype=BF16)


def _block_diag_ones(w, blk):
    a = np.arange(w)
    return jnp.asarray((a[:, None] // blk == a[None, :] // blk).astype(np.float32), dtype=BF16)


def kernel(x, ln1_g, w_in, w_branch_sb, w_branch_hg, hg_norm_g, hg_lb_logits, w_out, ln2_g,
           w_router_group, b_router_group, w_router_expert, b_router_expert,
           w_exp_gate, w_exp_up, w_exp_down, final_g):
    bsz, seq, d = x.shape
    depth = w_in.shape[0]
    n = bsz * seq
    sb_width = SB_HEADS * SB_HEAD_DIM
    hg_width = HG_HEADS * HG_DIM

    tri_attn = _suffix_ones(min(ATTN_T, seq))
    bias_attn = _causal_bias(min(ATTN_T, seq))
    tt = min(HGRN_TT, seq)
    tri_hg = _chunk_prefix_ones(tt, min(HGRN_C, tt))
    bd = _block_diag_ones(min(MXU_DIM, hg_width), HG_DIM)

    x2 = x.reshape(n, d)
    for l in range(depth):
        w_in_l = w_in[l].astype(BF16)
        c_gate = 3 * sb_width + 4 * hg_width
        assert w_in_l.shape[1] == c_gate + 2 * d
        qkv, rest = _inproj(x2, ln1_g[l], w_in_l[:, :c_gate], sb_width)
        y_sb = _attn(qkv, bsz, seq, tri_attn, bias_attn)
        y_hg = _hgrn(rest, hg_lb_logits, hg_norm_g[l], bsz, seq, l, tri_hg, bd)

        pad = LANES - N_EXPERTS - N_GROUPS
        wr = jnp.concatenate([w_router_expert[l], w_router_group[l],
                              jnp.zeros((d, pad), F32)], axis=1)
        wr_hi = wr.astype(BF16)
        wr_lo = (wr - wr_hi.astype(F32)).astype(BF16)
        wr_split = jnp.concatenate([wr_hi, wr_lo], axis=1)
        br = jnp.concatenate([b_router_expert[l], b_router_group[l],
                              jnp.zeros((pad,), F32)]).reshape(1, LANES)

        last = l == depth - 1
        x1, t, route = _merge(x2, y_sb, y_hg, ln1_g[l], w_in_l[:, c_gate:],
                              w_branch_sb[l].astype(BF16), w_branch_hg[l].astype(BF16),
                              w_out[l].astype(BF16), ln2_g[l], wr_split, br)
        assert last, "final rmsnorm is fused into the last layer's combine kernel"
        x2 = _moe_sparse(t, route, w_exp_gate[l], w_exp_up[l], w_exp_down[l], x1, final_g)
    return x2.reshape(bsz, seq, d)
```

```python
import functools

import jax
import jax.numpy as jnp
import numpy as np
from jax import lax
from jax.experimental import pallas as pl
from jax.experimental.pallas import tpu as pltpu

F32 = jnp.float32
BF16 = jnp.bfloat16

EPS = 1e-6
SB_HEADS = 8
SB_HEAD_DIM = 64
HG_HEADS = 8
HG_DIM = 64
N_GROUPS = 4
EXPERTS_PER_GROUP = 4
N_EXPERTS = N_GROUPS * EXPERTS_PER_GROUP

LANES = 128
SUBLANES = 8
MXU_DIM = 256
LOG2E = 1.4426950408889634
VMEM_LIMIT = 48 * 1024 * 1024

INPROJ_TM = 2048
INPROJ_TN = 512
ATTN_T = MXU_DIM
HGRN_TT = 256
HGRN_C = 32
MERGE_TM = 512
MOE_TM = 512
PLAN_TM = 1024
DISPATCH_TM = 1024
COMBINE_TM = 512


def _params(sem):
    return pltpu.CompilerParams(dimension_semantics=sem, vmem_limit_bytes=VMEM_LIMIT)


def _split_dot(x, m, passes):
    acc = None
    r = x
    for p in range(passes):
        h = r.astype(BF16)
        term = jnp.dot(h, m, preferred_element_type=F32)
        acc = term if acc is None else acc + term
        if p + 1 < passes:
            r = r - h.astype(F32)
    return acc


def _inproj_kernel(x_ref, g_ref, w_ref, qkv_ref, rest_ref, h_scr, *, q_scale):
    j = pl.program_id(1)

    @pl.when(j == 0)
    def _():
        x = x_ref[...]
        var = jnp.mean(x * x, axis=-1, keepdims=True)
        h_scr[...] = (x * lax.rsqrt(var + EPS) * g_ref[...]).astype(BF16)

    acc = jnp.dot(h_scr[...], w_ref[...], preferred_element_type=F32)
    qkv_ref[...] = (acc * jnp.where(j == 0, q_scale, 1.0)).astype(BF16)
    rest_ref[...] = acc


def _inproj(x2, ln_g, w_in, sb_width):
    n, d = x2.shape
    cols = w_in.shape[1]
    tm, tn = min(INPROJ_TM, n), INPROJ_TN
    assert sb_width == tn, "q block must be exactly one column tile"
    nq = 3 * sb_width // tn
    nj = cols // tn
    kern = functools.partial(_inproj_kernel, q_scale=SB_HEAD_DIM ** -0.5)
    return pl.pallas_call(
        kern,
        grid=(n // tm, nj),
        in_specs=[
            pl.BlockSpec((tm, d), lambda i, j: (i, 0)),
            pl.BlockSpec((1, d), lambda i, j: (0, 0)),
            pl.BlockSpec((d, tn), lambda i, j: (0, j)),
        ],
        out_specs=[
            pl.BlockSpec((tm, tn), lambda i, j: (i, jnp.minimum(j, nq))),
            pl.BlockSpec((tm, tn), lambda i, j: (i, jnp.maximum(j - nq, 0))),
        ],
        out_shape=[
            jax.ShapeDtypeStruct((n, (nq + 1) * tn), BF16),
            jax.ShapeDtypeStruct((n, cols - nq * tn), F32),
        ],
        scratch_shapes=[pltpu.VMEM((tm, d), BF16)],
        compiler_params=_params(("parallel", "arbitrary")),
        name="inproj",
    )(x2, ln_g.reshape(1, d), w_in)


ATTN_STAGES = 3
ATTN_STREAMS = 2
MASK_BIAS = -1e30
ATTN_SKIP = 111.0


def _attn_kernel(q_ref, k_ref, v_ref, tri_ref, bias_ref, o_ref,
                 z0, z1, z2, i0, i1, i2, acc_ref, c_ref, *, t, nq):
    zbuf = (z0, z1, z2)
    ibuf = (i0, i1, i2)
    for r in zbuf + ibuf:
        r[...] = jnp.zeros_like(r)
    acc_ref[...] = jnp.zeros_like(acc_ref)
    c_ref[...] = jnp.zeros_like(c_ref)

    lane = lax.broadcasted_iota(jnp.int32, (t, LANES), 1)
    head0 = lane < SB_HEAD_DIM
    nt = (((1,), (1,)), ((), ()))

    def stage_a(qi, kj, slot):
        q = q_ref[pl.ds(pl.multiple_of(qi * t, t), t), :]
        zero = jnp.zeros_like(q)
        q2 = jnp.concatenate([jnp.where(head0, q, zero), jnp.where(head0, zero, q)], axis=0)
        k = k_ref[pl.ds(pl.multiple_of(kj * t, t), t), :]
        z = lax.dot_general(q2, k, nt, preferred_element_type=F32)
        bias = bias_ref[(qi == kj).astype(jnp.int32)]
        zbuf[slot][...] = z + jnp.concatenate([bias, bias], axis=0)

    def stage_b(slot):
        z = zbuf[slot][...]
        p = jnp.maximum(z, 0.0) + jnp.log(1.0 + jnp.exp2(jnp.abs(z) * (-LOG2E)))
        incl = jnp.dot(p.astype(BF16), tri_ref[...], preferred_element_type=F32)
        ibuf[slot][...] = incl
        return incl[:, 0:1]

    def stage_c(qi, kj, slot, stream):
        first = qi == kj
        incl = ibuf[slot][...]
        c = jnp.where(first, 0.0, c_ref[stream])
        a = jnp.exp2((zbuf[slot][...] - incl - c) * LOG2E)
        v = v_ref[pl.ds(pl.multiple_of(kj * t, t), t), :]
        pv = jnp.dot(a.astype(BF16), v, preferred_element_type=F32)
        acc = jnp.where(first, pv, acc_ref[stream] + pv)
        acc_ref[stream] = acc
        c_ref[stream] = c + incl[:, 0:1]
        o_ref[pl.ds(pl.multiple_of(qi * t, t), t), :] = (
            jnp.where(head0, acc[0:t], acc[t:2 * t]).astype(o_ref.dtype))

    def block(carry):
        qs, ks, qn, kn, skip_q, drained = carry
        qs, ks, qn, kn, skip_q = list(qs), list(ks), list(qn), list(kn), list(skip_q)
        all_done = qn[0] >= nq
        for p in range(1, ATTN_STREAMS):
            all_done = jnp.logical_and(all_done, qn[p] >= nq)
        drained = drained + all_done.astype(jnp.int32)
        for r in range(ATTN_STAGES * ATTN_STREAMS):
            sa, sc, sb = r % ATTN_STAGES, (r + 1) % ATTN_STAGES, (r + 2) % ATTN_STAGES
            pa, pb = r % ATTN_STREAMS, (r - 1) % ATTN_STREAMS
            stage_c(qs[sc], ks[sc], sc, pa)
            total_b = stage_b(sb)
            carry_b = jnp.where(qs[sb] == ks[sb], 0.0, c_ref[pb]) + total_b
            skip = skip_q[pa] == qn[pa]
            q_cur = jnp.where(skip, qn[pa] + ATTN_STREAMS, qn[pa])
            k_cur = jnp.where(skip, qn[pa] + ATTN_STREAMS, kn[pa])
            drain = q_cur >= nq
            qa = jnp.where(drain, 0, q_cur)
            ka = jnp.where(drain, 0, k_cur)
            stage_a(qa, ka, sa)
            last = k_cur == 0
            qn[pa] = jnp.where(jnp.logical_and(last, jnp.logical_not(drain)),
                               q_cur + ATTN_STREAMS, q_cur)
            kn[pa] = jnp.where(drain, k_cur, jnp.where(last, q_cur + ATTN_STREAMS, k_cur - 1))
            skip_q[pb] = jnp.where(jnp.min(carry_b) >= ATTN_SKIP, qs[sb], skip_q[pb])
            qs[sa], ks[sa] = qa, ka
        return tuple(qs), tuple(ks), tuple(qn), tuple(kn), tuple(skip_q), drained

    zero = jnp.int32(0)
    first_q = tuple(jnp.int32(p) for p in range(ATTN_STREAMS))
    init = ((zero,) * ATTN_STAGES, (zero,) * ATTN_STAGES, first_q, first_q,
            (jnp.int32(-1),) * ATTN_STREAMS, zero)
    lax.while_loop(lambda carry: carry[5] < 1, block, init)


def _attn(qkv, bsz, seq, tri, bias):
    n = bsz * seq
    t = min(ATTN_T, seq)
    pairs = SB_HEADS * SB_HEAD_DIM // LANES
    kern = functools.partial(_attn_kernel, t=t, nq=seq // t)
    return pl.pallas_call(
        kern,
        grid=(bsz, pairs),
        in_specs=[
            pl.BlockSpec((seq, LANES), lambda b, p: (b, p)),
            pl.BlockSpec((seq, LANES), lambda b, p: (b, pairs + p)),
            pl.BlockSpec((seq, LANES), lambda b, p: (b, 2 * pairs + p)),
            pl.BlockSpec((t, t), lambda b, p: (0, 0)),
            pl.BlockSpec((2, t, t), lambda b, p: (0, 0, 0)),
        ],
        out_specs=pl.BlockSpec((seq, LANES), lambda b, p: (b, p)),
        out_shape=jax.ShapeDtypeStruct((n, pairs * LANES), BF16),
        scratch_shapes=([pltpu.VMEM((2 * t, t), F32)] * (2 * ATTN_STAGES)
                        + [pltpu.VMEM((ATTN_STREAMS, 2 * t, LANES), F32),
                           pltpu.VMEM((ATTN_STREAMS, 2 * t, 1), F32)]),
        compiler_params=_params(("parallel", "parallel")),
        name="attn",
    )(qkv, qkv, qkv, tri, bias)


def _group_dot(x, bd):
    g = bd.shape[0]
    parts = [jnp.dot(x[:, i:i + g], bd, preferred_element_type=F32)
             for i in range(0, x.shape[1], g)]
    return jnp.concatenate(parts, axis=1)


def _hgrn_kernel(q_ref, f_ref, i_ref, g_ref, lbl_ref, ng_ref, tri_ref, bd_ref, o_ref,
                 st_scr, wpad, vpad, wsh, vsh, *, tt, c, layer):
    ti = pl.program_id(1)
    w = q_ref.shape[1]
    nch = tt // c

    @pl.when(ti == 0)
    def _():
        st_scr[...] = jnp.zeros_like(st_scr)

    lg = lbl_ref[...]
    e = jnp.exp(lg - jnp.max(lg, axis=0, keepdims=True))
    lb = jnp.sum(e[0:layer + 1], axis=0, keepdims=True) / jnp.sum(e, axis=0, keepdims=True)

    f = lb + (1.0 - lb) * jax.nn.sigmoid(f_ref[...])
    kk = 1.0 - f
    qv = q_ref[...]
    qs = qv * jax.nn.sigmoid(qv)
    v = i_ref[...]
    bd = bd_ref[...]
    cum = _split_dot_left(tri_ref[...], jnp.log(f) * LOG2E, 3)
    wk = jnp.log(kk) * LOG2E - cum

    pos = lax.broadcasted_iota(jnp.int32, (tt, 1), 0) % c
    wpad[0:SUBLANES, :] = jnp.zeros((SUBLANES, w), F32)
    vpad[0:SUBLANES, :] = jnp.zeros((SUBLANES, w), F32)
    wpad[SUBLANES:SUBLANES + tt, :] = wk
    vpad[SUBLANES:SUBLANES + tt, :] = v
    for b in range(SUBLANES):
        wb = wpad[SUBLANES - b:SUBLANES - b + tt, :]
        vb = vpad[SUBLANES - b:SUBLANES - b + tt, :]
        if b:
            wb = jnp.where(pos >= b, wb, -jnp.inf)
        wsh[b] = wb.reshape(nch, c, w)
        vsh[b] = vb.reshape(nch, c, w)

    qs3 = qs.reshape(nch, c, w)
    cum3 = cum.reshape(nch, c, w)
    acc = None
    for a in reversed(range(c // SUBLANES)):
        rows = c - SUBLANES * a
        qa = qs3[:, SUBLANES * a:, :].reshape(nch * rows, w)
        ca = cum3[:, SUBLANES * a:, :].reshape(nch * rows, w)
        acc_a = None
        for b in range(SUBLANES):
            wb = wsh[b, :, 0:rows, :].reshape(nch * rows, w)
            vb = vsh[b, :, 0:rows, :].reshape(nch * rows, w)
            dd = qa * jnp.exp2(ca + wb)
            term = _group_dot(dd.astype(BF16), bd) * vb
            acc_a = term if acc_a is None else acc_a + term
        acc_a = acc_a.reshape(nch, rows, w)
        if acc is not None:
            acc_a = acc_a + jnp.concatenate([jnp.zeros((nch, SUBLANES, w), F32), acc], axis=1)
        acc = acc_a
    acc = acc.reshape(tt, w)

    grp = st_scr.shape[1]
    bdmask = bd[0:grp, 0:grp] != 0
    outs = []
    for ci in range(nch):
        r0 = ci * c
        cum_c = cum[r0:r0 + c]
        last = cum_c[c - 1:c]
        qd = (qs[r0:r0 + c] * jnp.exp2(cum_c)).astype(BF16)
        kd = (kk[r0:r0 + c] * jnp.exp2(last - cum_c)).astype(BF16)
        vc = v[r0:r0 + c].astype(BF16)
        dec = jnp.exp2(last)
        o_parts = []
        for gi in range(w // grp):
            sl = slice(gi * grp, (gi + 1) * grp)
            st = st_scr[gi]
            o_parts.append(lax.dot_general(qd[:, sl], st.astype(BF16), (((1,), (1,)), ((), ())),
                                           preferred_element_type=F32))
            upd = lax.dot_general(vc[:, sl], kd[:, sl], (((0,), (0,)), ((), ())),
                                  preferred_element_type=F32)
            st_scr[gi] = st * dec[:, sl] + jnp.where(bdmask, upd, 0.0)
        outs.append(jnp.concatenate(o_parts, axis=1))
    o = acc + jnp.concatenate(outs, axis=0)

    o2 = o * o
    o2_hi = o2.astype(BF16)
    o2_lo = (o2 - o2_hi.astype(F32)).astype(BF16)
    ms = (_group_dot(o2_hi, bd) + _group_dot(o2_lo, bd)) * (1.0 / HG_DIM)
    gv = g_ref[...]
    o = o * lax.rsqrt(ms + EPS) * ng_ref[...] * (gv * jax.nn.sigmoid(gv))
    o_ref[...] = o.astype(o_ref.dtype)


def _split_dot_left(m, x, passes):
    acc = None
    r = x
    for p in range(passes):
        h = r.astype(BF16)
        term = jnp.dot(m, h, preferred_element_type=F32)
        acc = term if acc is None else acc + term
        if p + 1 < passes:
            r = r - h.astype(F32)
    return acc


def _hgrn(rest, lb_logits, norm_g, bsz, seq, layer, tri, bd):
    n = bsz * seq
    w = HG_HEADS * HG_DIM
    grp = bd.shape[0]
    tt = min(HGRN_TT, seq)
    c = min(HGRN_C, tt)
    nt = seq // tt
    kern = functools.partial(_hgrn_kernel, tt=tt, c=c, layer=layer)
    col = lambda j: pl.BlockSpec((tt, w), lambda b, i, j=j: (b * nt + i, j))
    const = lambda shape: pl.BlockSpec(shape, lambda b, i: (0, 0))
    return pl.pallas_call(
        kern,
        grid=(bsz, nt),
        in_specs=[col(0), col(1), col(2), col(3),
                  const(lb_logits.shape), const((1, w)), const((tt, tt)), const((grp, grp))],
        out_specs=pl.BlockSpec((tt, w), lambda b, i: (b * nt + i, 0)),
        out_shape=jax.ShapeDtypeStruct((n, w), BF16),
        scratch_shapes=([pltpu.VMEM((w // LANES, LANES, LANES), F32)]
                        + [pltpu.VMEM((SUBLANES + tt, w), F32)] * 2
                        + [pltpu.VMEM((SUBLANES, tt // c, c, w), F32)] * 2),
        compiler_params=_params(("parallel", "arbitrary")),
        name="hgrn",
    )(rest, rest, rest, rest, lb_logits, norm_g.reshape(1, w), tri, bd)


ROUTE_I1, ROUTE_I2, ROUTE_W1, ROUTE_W2 = N_EXPERTS, N_EXPERTS + 1, N_EXPERTS + 2, N_EXPERTS + 3


def _route(lg):
    lane = lax.broadcasted_iota(jnp.int32, lg.shape, 1)
    neg = jnp.float32(-jnp.inf)
    big = jnp.int32(LANES)
    gmask = jnp.logical_and(lane >= N_EXPERTS, lane < N_EXPERTS + N_GROUPS)
    gl = jnp.where(gmask, lg, neg)
    gmax = jnp.max(gl, axis=1, keepdims=True)
    gidx = jnp.min(jnp.where(gl == gmax, lane, big), axis=1, keepdims=True) - N_EXPERTS
    w_grp = 1.0 / jnp.sum(jnp.where(gmask, jnp.exp(gl - gmax), 0.0), axis=1, keepdims=True)
    in_grp = jnp.logical_and(lane < N_EXPERTS, lane // EXPERTS_PER_GROUP == gidx)
    l1 = jnp.where(in_grp, lg, neg)
    v1 = jnp.max(l1, axis=1, keepdims=True)
    i1 = jnp.min(jnp.where(l1 == v1, lane, big), axis=1, keepdims=True)
    l2 = jnp.where(jnp.logical_and(in_grp, lane != i1), lg, neg)
    v2 = jnp.max(l2, axis=1, keepdims=True)
    i2 = jnp.min(jnp.where(l2 == v2, lane, big), axis=1, keepdims=True)
    e2 = jnp.exp(v2 - v1)
    p1 = 1.0 / (1.0 + e2)
    p2 = e2 * p1
    out = jnp.where(jnp.logical_or(lane == i1, lane == i2), 1.0, 0.0)
    out = jnp.where(lane == ROUTE_I1, i1.astype(F32), out)
    out = jnp.where(lane == ROUTE_I2, i2.astype(F32), out)
    out = jnp.where(lane == ROUTE_W1, p1 * w_grp, out)
    return jnp.where(lane == ROUTE_W2, p2 * w_grp, out)


def _store_token_tiles(ref, x):
    rows, d = x.shape
    assert d == SUBLANES * LANES
    for s in range(SUBLANES):
        ref[pl.ds(s, rows, stride=SUBLANES), :] = x[:, s * LANES:(s + 1) * LANES]


def _load_token_tiles(ref):
    rows = ref.shape[0] // SUBLANES
    return jnp.concatenate([ref[pl.ds(s, rows, stride=SUBLANES), :] for s in range(SUBLANES)],
                           axis=1)


def _merge_kernel(x_ref, ysb_ref, yhg_ref, gsb_ref, ghg_ref, wbs_ref, wbh_ref, wo_ref,
                  ln_ref, wr_ref, br_ref, x1_ref, t_ref, route_ref):
    a = jnp.dot(ysb_ref[...], wbs_ref[...], preferred_element_type=F32)
    b = jnp.dot(yhg_ref[...], wbh_ref[...], preferred_element_type=F32)
    merged = jax.nn.sigmoid(gsb_ref[...]) * a + jax.nn.sigmoid(ghg_ref[...]) * b
    x1 = x_ref[...] + jnp.dot(merged.astype(BF16), wo_ref[...], preferred_element_type=F32)
    x1_ref[...] = x1
    var = jnp.mean(x1 * x1, axis=-1, keepdims=True)
    t = x1 * lax.rsqrt(var + EPS) * ln_ref[...]
    _store_token_tiles(t_ref, t)
    t_hi = t.astype(BF16)
    t_lo = (t - t_hi.astype(F32)).astype(BF16)
    p_hi = jnp.dot(t_hi, wr_ref[...], preferred_element_type=F32)
    p_lo = jnp.dot(t_lo, wr_ref[...], preferred_element_type=F32)
    lg = (p_hi[:, :LANES] + p_hi[:, LANES:]) + (p_lo[:, :LANES] + p_lo[:, LANES:]) + br_ref[...]
    route_ref[...] = _route(lg)


def _merge(x2, y_sb, y_hg, rest, wbs, wbh, wo, ln_g, wr, br):
    n, d = x2.shape
    tm = min(MERGE_TM, n)
    w_sb, w_hg = y_sb.shape[1], y_hg.shape[1]
    gate_blk = (rest.shape[1] - 2 * d) // d
    row = lambda wdt, j=0: pl.BlockSpec((tm, wdt), lambda i, j=j: (i, j))
    const = lambda shape: pl.BlockSpec(shape, lambda i: (0, 0))
    return pl.pallas_call(
        _merge_kernel,
        grid=(n // tm,),
        in_specs=[row(d), row(w_sb), row(w_hg), row(d, gate_blk), row(d, gate_blk + 1),
                  const(wbs.shape), const(wbh.shape), const(wo.shape), const((1, d)),
                  const(wr.shape), const((1, LANES))],
        out_specs=[row(d), pl.BlockSpec((tm * SUBLANES, LANES), lambda i: (i, 0)), row(LANES)],
        out_shape=[jax.ShapeDtypeStruct((n, d), F32),
                   jax.ShapeDtypeStruct((n * SUBLANES, LANES), F32),
                   jax.ShapeDtypeStruct((n, LANES), F32)],
        compiler_params=_params(("parallel",)),
        name="merge",
    )(x2, y_sb, y_hg, rest, rest, wbs, wbh, wo, ln_g.reshape(1, d), wr, br)


def _plan_kernel(route_ref, ltri_ref, utri_ref, pos_ref, te_ref, meta_ref, tot_scr, run_scr,
                 *, tile_rows):
    ph = pl.program_id(0)
    i = pl.program_id(1)
    route = route_ref[...]
    tm = route.shape[0]
    lane = lax.broadcasted_iota(jnp.int32, (tm, LANES), 1)
    sel = jnp.where(lane < N_EXPERTS, route, 0.0)

    @pl.when(jnp.logical_and(ph == 0, i == 0))
    def _():
        tot_scr[...] = jnp.zeros_like(tot_scr)

    @pl.when(ph == 0)
    def _():
        tot_scr[...] += jnp.sum(sel, axis=0, keepdims=True)

    @pl.when(ph == 1)
    def _():
        tot = tot_scr[...]
        padded = jnp.ceil(tot * (1.0 / tile_rows)) * tile_rows
        off = _split_dot(jnp.broadcast_to(padded, (SUBLANES, LANES)), utri_ref[...], 3)[0:1]

        @pl.when(i == 0)
        def _():
            run_scr[...] = jnp.zeros_like(run_scr)
            lane1 = lax.broadcasted_iota(jnp.int32, (1, LANES), 1)
            is_e = lane1 < N_EXPERTS
            end = off + padded
            n_valid = (jnp.sum(jnp.where(is_e, padded, 0.0), axis=1, keepdims=True)
                       * (1.0 / tile_rows))
            rows = te_ref.shape[0]
            j = lax.broadcasted_iota(jnp.int32, (rows, LANES), 0).astype(F32)
            start = jnp.minimum(j, n_valid - 1.0) * tile_rows
            lane2 = lax.broadcasted_iota(jnp.int32, (rows, LANES), 1)
            done = jnp.where(jnp.logical_and(lane2 < N_EXPERTS, end <= start), 1.0, 0.0)
            te = jnp.sum(done, axis=1, keepdims=True)
            te_ref[...] = jnp.broadcast_to(te, (rows, LANES)).astype(jnp.int32)
            last_tile = jnp.where(jnp.logical_and(is_e, padded > 0),
                                  end * (1.0 / tile_rows) - 1.0, -1.0)
            meta = jnp.where(lane1 == N_EXPERTS, n_valid, last_tile)
            meta_ref[...] = jnp.broadcast_to(meta, meta_ref.shape).astype(jnp.int32)

        rank = jnp.dot(ltri_ref[...], sel.astype(BF16), preferred_element_type=F32) + run_scr[...]
        dest = off + rank
        lane_f = lane.astype(F32)
        p1 = jnp.sum(jnp.where(lane_f == route[:, ROUTE_I1:ROUTE_I1 + 1], dest, 0.0),
                     axis=1, keepdims=True)
        p2 = jnp.sum(jnp.where(lane_f == route[:, ROUTE_I2:ROUTE_I2 + 1], dest, 0.0),
                     axis=1, keepdims=True)
        pos_ref[...] = jnp.where(lane == 0, p1, jnp.where(lane == 1, p2, 0.0)).astype(jnp.int32)
        run_scr[...] += jnp.sum(sel, axis=0, keepdims=True)


def _plan(route, n_tiles_max):
    n = route.shape[0]
    tm = min(PLAN_TM, n)
    r = np.arange(tm)
    ltri = jnp.asarray((r[None, :] < r[:, None]).astype(np.float32), dtype=BF16)
    e = np.arange(LANES)
    utri = jnp.asarray((e[:, None] < e[None, :]).astype(np.float32), dtype=BF16)
    te_rows = -(-n_tiles_max // SUBLANES) * SUBLANES
    kern = functools.partial(_plan_kernel, tile_rows=MOE_TM)
    return pl.pallas_call(
        kern,
        grid=(2, n // tm),
        in_specs=[pl.BlockSpec((tm, LANES), lambda p, i: (i, 0)),
                  pl.BlockSpec((tm, tm), lambda p, i: (0, 0)),
                  pl.BlockSpec((LANES, LANES), lambda p, i: (0, 0))],
        out_specs=[pl.BlockSpec((tm, LANES), lambda p, i: (i * p, 0)),
                   pl.BlockSpec((te_rows, LANES), lambda p, i: (0, 0)),
                   pl.BlockSpec((SUBLANES, LANES), lambda p, i: (0, 0))],
        out_shape=[jax.ShapeDtypeStruct((n, LANES), jnp.int32),
                   jax.ShapeDtypeStruct((te_rows, LANES), jnp.int32),
                   jax.ShapeDtypeStruct((SUBLANES, LANES), jnp.int32)],
        scratch_shapes=[pltpu.VMEM((1, LANES), F32), pltpu.VMEM((1, LANES), F32)],
        compiler_params=_params(("arbitrary", "arbitrary")),
        name="plan",
    )(route, ltri, utri)


DMA_UNROLL = 8


def _row_copy(src_ref, r, dst_ref, p, sem):
    return pltpu.make_async_copy(src_ref.at[pl.ds(pl.multiple_of(r * SUBLANES, SUBLANES), SUBLANES)],
                                 dst_ref.at[pl.ds(pl.multiple_of(p * SUBLANES, SUBLANES), SUBLANES)],
                                 sem)


def _dispatch_kernel(pos1_ref, pos2_ref, meta_ref, t_ref, xs_ref, zero_scr, sem, *, tm, tile_rows):
    i = pl.program_id(0)
    tile_rows = tile_rows * SUBLANES
    n_tiles = xs_ref.shape[0] // tile_rows

    @pl.when(i == 0)
    def _():
        zero_scr[...] = jnp.zeros_like(zero_scr)
        n_valid = meta_ref[N_EXPERTS]
        clears = [(meta_ref[e], meta_ref[e] >= 0) for e in range(N_EXPERTS)]
        clears += [(n_tiles - 1 - k, n_tiles - 1 - k >= n_valid) for k in range(N_EXPERTS)]

        def clear(tile):
            return pltpu.make_async_copy(
                zero_scr, xs_ref.at[pl.ds(tile * tile_rows, tile_rows)], sem)

        for tile, cond in clears:
            @pl.when(cond)
            def _():
                clear(tile).start()
        for tile, cond in clears:
            @pl.when(cond)
            def _():
                clear(tile).wait()

    base = i * tm

    def start(r, carry):
        _row_copy(t_ref, r, xs_ref, pos1_ref[base + r], sem).start(priority=0)
        _row_copy(t_ref, r, xs_ref, pos2_ref[base + r], sem).start(priority=1)
        return carry

    def wait(r, carry):
        _row_copy(t_ref, r, xs_ref, pos1_ref[base + r], sem).wait()
        _row_copy(t_ref, r, xs_ref, pos2_ref[base + r], sem).wait()
        return carry

    lax.fori_loop(0, tm, start, 0, unroll=DMA_UNROLL)
    lax.fori_loop(0, tm, wait, 0, unroll=DMA_UNROLL)


def _dispatch(pos1, pos2, meta, t, n_rows):
    n = t.shape[0] // SUBLANES
    tm = min(DISPATCH_TM, n)
    kern = functools.partial(_dispatch_kernel, tm=tm, tile_rows=MOE_TM)
    grid_spec = pltpu.PrefetchScalarGridSpec(
        num_scalar_prefetch=3,
        grid=(n // tm,),
        in_specs=[pl.BlockSpec((tm * SUBLANES, LANES), lambda i, *_: (i, 0))],
        out_specs=pl.BlockSpec(memory_space=pl.ANY),
        scratch_shapes=[pltpu.VMEM((MOE_TM * SUBLANES, LANES), F32), pltpu.SemaphoreType.DMA(())],
    )
    return pl.pallas_call(
        kern,
        grid_spec=grid_spec,
        out_shape=jax.ShapeDtypeStruct((n_rows * SUBLANES, LANES), F32),
        compiler_params=_params(("arbitrary",)),
        name="dispatch",
    )(pos1, pos2, meta, t)


def _expert_kernel(te_ref, nv_ref, xs_ref, wg_ref, wu_ref, wd_ref, ys_ref):
    in_use = pl.program_id(0) < nv_ref[0]

    @pl.when(in_use)
    def _():
        x = _load_token_tiles(xs_ref).astype(BF16)
        hg = jnp.dot(x, wg_ref[...].astype(BF16), preferred_element_type=F32)
        hu = jnp.dot(x, wu_ref[...].astype(BF16), preferred_element_type=F32)
        a = (hg * jax.nn.sigmoid(hg) * hu).astype(BF16)
        _store_token_tiles(ys_ref, jnp.dot(a, wd_ref[...].astype(BF16),
                                           preferred_element_type=F32))

    @pl.when(jnp.logical_not(in_use))
    def _():
        ys_ref[...] = jnp.zeros_like(ys_ref)


def _experts(te, nv, xs, wg, wu, wd):
    _, d, de = wg.shape
    assert d == SUBLANES * LANES
    tile = lambda j, te, nv: (j, 0)
    grid_spec = pltpu.PrefetchScalarGridSpec(
        num_scalar_prefetch=2,
        grid=(xs.shape[0] // (MOE_TM * SUBLANES),),
        in_specs=[pl.BlockSpec((MOE_TM * SUBLANES, LANES), tile),
                  pl.BlockSpec((None, d, de), lambda j, te, nv: (te[j], 0, 0)),
                  pl.BlockSpec((None, d, de), lambda j, te, nv: (te[j], 0, 0)),
                  pl.BlockSpec((None, de, d), lambda j, te, nv: (te[j], 0, 0))],
        out_specs=pl.BlockSpec((MOE_TM * SUBLANES, LANES), tile),
    )
    return pl.pallas_call(
        _expert_kernel,
        grid_spec=grid_spec,
        out_shape=jax.ShapeDtypeStruct(xs.shape, F32),
        compiler_params=_params(("arbitrary",)),
        name="experts",
    )(te, nv, xs, wg, wu, wd)


def _combine_kernel(pos1_ref, pos2_ref, ys_ref, x1_ref, route_ref, fg_ref, o_ref, g1, g2, sem,
                    *, tm):
    i = pl.program_id(0)
    slot = i % 2

    def gather(step, buf, wait):
        base = step * tm

        def body(r, carry):
            c1 = _row_copy(ys_ref, pos1_ref[base + r], g1.at[buf], r, sem.at[buf])
            c2 = _row_copy(ys_ref, pos2_ref[base + r], g2.at[buf], r, sem.at[buf])
            if wait:
                c1.wait()
                c2.wait()
            else:
                c1.start(priority=0)
                c2.start(priority=1)
            return carry

        lax.fori_loop(0, tm, body, 0, unroll=DMA_UNROLL)

    @pl.when(i == 0)
    def _():
        gather(0, 0, wait=False)

    @pl.when(i + 1 < pl.num_programs(0))
    def _():
        gather(i + 1, 1 - slot, wait=False)

    gather(i, slot, wait=True)
    route = route_ref[...]
    w1 = route[:, ROUTE_W1:ROUTE_W1 + 1]
    w2 = route[:, ROUTE_W2:ROUTE_W2 + 1]
    x2 = x1_ref[...] + (w1 * _load_token_tiles(g1.at[slot]) + w2 * _load_token_tiles(g2.at[slot]))
    var = jnp.mean(x2 * x2, axis=-1, keepdims=True)
    o_ref[...] = x2 * lax.rsqrt(var + EPS) * fg_ref[...]


def _combine(pos1, pos2, ys, x1, route, final_g):
    n, d = x1.shape
    tm = min(COMBINE_TM, n)
    kern = functools.partial(_combine_kernel, tm=tm)
    grid_spec = pltpu.PrefetchScalarGridSpec(
        num_scalar_prefetch=2,
        grid=(n // tm,),
        in_specs=[pl.BlockSpec(memory_space=pl.ANY),
                  pl.BlockSpec((tm, d), lambda i, *_: (i, 0)),
                  pl.BlockSpec((tm, LANES), lambda i, *_: (i, 0)),
                  pl.BlockSpec((1, d), lambda i, *_: (0, 0))],
        out_specs=pl.BlockSpec((tm, d), lambda i, *_: (i, 0)),
        scratch_shapes=([pltpu.VMEM((2, tm * SUBLANES, LANES), F32)] * 2
                        + [pltpu.SemaphoreType.DMA((2,))]),
    )
    return pl.pallas_call(
        kern,
        grid_spec=grid_spec,
        out_shape=jax.ShapeDtypeStruct((n, d), F32),
        compiler_params=_params(("arbitrary",)),
        name="combine",
    )(pos1, pos2, ys, x1, route, final_g.reshape(1, d))


def _moe_sparse(t, route, wg, wu, wd, x1, final_g):
    n = x1.shape[0]
    n_tiles_max = 2 * n // MOE_TM + N_EXPERTS
    pos, te, meta = _plan(route, n_tiles_max)
    pos1, pos2 = pos[:, 0], pos[:, 1]
    xs = _dispatch(pos1, pos2, meta[0, :N_EXPERTS + 1], t, n_tiles_max * MOE_TM)
    ys = _experts(te[:n_tiles_max, 0], meta[0, N_EXPERTS:N_EXPERTS + 1], xs, wg, wu, wd)
    return _combine(pos1, pos2, ys, x1, route, final_g)


def _suffix_ones(t):
    j = np.arange(t)[:, None]
    s = np.arange(t)[None, :]
    return jnp.asarray((j >= s).astype(np.float32), dtype=BF16)


def _causal_bias(t):
    row = np.arange(t)[:, None]
    col = np.arange(t)[None, :]
    diag = np.where(col < row, 0.0, MASK_BIAS).astype(np.float32)
    return jnp.asarray(np.stack([np.zeros_like(diag), diag]))


def _chunk_prefix_ones(tt, c):
    t = np.arange(tt)[:, None]
    j = np.arange(tt)[None, :]
    return jnp.asarray(((j <= t) & (t // c == j // c)).astype(np.float32), dtype=BF16)


def _block_diag_ones(w, blk):
    a = np.arange(w)
    return jnp.asarray((a[:, None] // blk == a[None, :] // blk).astype(np.float32), dtype=BF16)


def kernel(x, ln1_g, w_in, w_branch_sb, w_branch_hg, hg_norm_g, hg_lb_logits, w_out, ln2_g,
           w_router_group, b_router_group, w_router_expert, b_router_expert,
           w_exp_gate, w_exp_up, w_exp_down, final_g):
    bsz, seq, d = x.shape
    depth = w_in.shape[0]
    n = bsz * seq
    sb_width = SB_HEADS * SB_HEAD_DIM
    hg_width = HG_HEADS * HG_DIM

    tri_attn = _suffix_ones(min(ATTN_T, seq))
    bias_attn = _causal_bias(min(ATTN_T, seq))
    tt = min(HGRN_TT, seq)
    tri_hg = _chunk_prefix_ones(tt, min(HGRN_C, tt))
    bd = _block_diag_ones(min(MXU_DIM, hg_width), HG_DIM)

    x2 = x.reshape(n, d)
    for l in range(depth):
        qkv, rest = _inproj(x2, ln1_g[l], w_in[l].astype(BF16), sb_width)
        y_sb = _attn(qkv, bsz, seq, tri_attn, bias_attn)
        y_hg = _hgrn(rest, hg_lb_logits, hg_norm_g[l], bsz, seq, l, tri_hg, bd)

        pad = LANES - N_EXPERTS - N_GROUPS
        wr = jnp.concatenate([w_router_expert[l], w_router_group[l],
                              jnp.zeros((d, pad), F32)], axis=1)
        wr_hi = wr.astype(BF16)
        wr_lo = (wr - wr_hi.astype(F32)).astype(BF16)
        wr_split = jnp.concatenate([wr_hi, wr_lo], axis=1)
        br = jnp.concatenate([b_router_expert[l], b_router_group[l],
                              jnp.zeros((pad,), F32)]).reshape(1, LANES)

        last = l == depth - 1
        x1, t, route = _merge(x2, y_sb, y_hg, rest, w_branch_sb[l].astype(BF16),
                              w_branch_hg[l].astype(BF16), w_out[l].astype(BF16), ln2_g[l],
                              wr_split, br)
        assert last, "final rmsnorm is fused into the last layer's combine kernel"
        x2 = _moe_sparse(t, route, w_exp_gate[l], w_exp_up[l], w_exp_down[l], x1, final_g)
    return x2.reshape(bsz, seq, d)
```

```python
import functools

import jax
import jax.numpy as jnp
import numpy as np
from jax import lax
from jax.experimental import pallas as pl
from jax.experimental.pallas import tpu as pltpu

F32 = jnp.float32
BF16 = jnp.bfloat16

EPS = 1e-6
SB_HEADS = 8
SB_HEAD_DIM = 64
HG_HEADS = 8
HG_DIM = 64
N_GROUPS = 4
EXPERTS_PER_GROUP = 4
N_EXPERTS = N_GROUPS * EXPERTS_PER_GROUP

LANES = 128
SUBLANES = 8
MXU_DIM = 256
LOG2E = 1.4426950408889634
VMEM_LIMIT = 48 * 1024 * 1024

INPROJ_TM = 2048
INPROJ_TN = 512
ATTN_T = MXU_DIM
HGRN_TT = 256
HGRN_C = 32
MERGE_TM = 512
MOE_TM = 256
PLAN_TM = 1024
DISPATCH_TM = 1024
COMBINE_TM = 512


def _params(sem):
    return pltpu.CompilerParams(dimension_semantics=sem, vmem_limit_bytes=VMEM_LIMIT)


def _split_dot(x, m, passes):
    acc = None
    r = x
    for p in range(passes):
        h = r.astype(BF16)
        term = jnp.dot(h, m, preferred_element_type=F32)
        acc = term if acc is None else acc + term
        if p + 1 < passes:
            r = r - h.astype(F32)
    return acc


def _inproj_kernel(x_ref, g_ref, w_ref, qkv_ref, rest_ref, h_scr, *, q_scale):
    j = pl.program_id(1)

    @pl.when(j == 0)
    def _():
        x = x_ref[...]
        var = jnp.mean(x * x, axis=-1, keepdims=True)
        h_scr[...] = (x * lax.rsqrt(var + EPS) * g_ref[...]).astype(BF16)

    acc = jnp.dot(h_scr[...], w_ref[...], preferred_element_type=F32)
    qkv_ref[...] = (acc * jnp.where(j == 0, q_scale, 1.0)).astype(BF16)
    rest_ref[...] = acc


def _inproj(x2, ln_g, w_in, sb_width):
    n, d = x2.shape
    cols = w_in.shape[1]
    tm, tn = min(INPROJ_TM, n), INPROJ_TN
    assert sb_width == tn, "q block must be exactly one column tile"
    nq = 3 * sb_width // tn
    nj = cols // tn
    kern = functools.partial(_inproj_kernel, q_scale=SB_HEAD_DIM ** -0.5)
    return pl.pallas_call(
        kern,
        grid=(n // tm, nj),
        in_specs=[
            pl.BlockSpec((tm, d), lambda i, j: (i, 0)),
            pl.BlockSpec((1, d), lambda i, j: (0, 0)),
            pl.BlockSpec((d, tn), lambda i, j: (0, j)),
        ],
        out_specs=[
            pl.BlockSpec((tm, tn), lambda i, j: (i, jnp.minimum(j, nq))),
            pl.BlockSpec((tm, tn), lambda i, j: (i, jnp.maximum(j - nq, 0))),
        ],
        out_shape=[
            jax.ShapeDtypeStruct((n, (nq + 1) * tn), BF16),
            jax.ShapeDtypeStruct((n, cols - nq * tn), F32),
        ],
        scratch_shapes=[pltpu.VMEM((tm, d), BF16)],
        compiler_params=_params(("parallel", "arbitrary")),
        name="inproj",
    )(x2, ln_g.reshape(1, d), w_in)


ATTN_STAGES = 3
ATTN_STREAMS = 2
MASK_BIAS = -1e30
ATTN_SKIP = 111.0


def _attn_kernel(q_ref, k_ref, v_ref, tri_ref, bias_ref, o_ref,
                 z0, z1, z2, i0, i1, i2, acc_ref, c_ref, *, t, nq):
    zbuf = (z0, z1, z2)
    ibuf = (i0, i1, i2)
    for r in zbuf + ibuf:
        r[...] = jnp.zeros_like(r)
    acc_ref[...] = jnp.zeros_like(acc_ref)
    c_ref[...] = jnp.zeros_like(c_ref)

    lane = lax.broadcasted_iota(jnp.int32, (t, LANES), 1)
    head0 = lane < SB_HEAD_DIM
    nt = (((1,), (1,)), ((), ()))

    def stage_a(qi, kj, slot):
        q = q_ref[pl.ds(pl.multiple_of(qi * t, t), t), :]
        zero = jnp.zeros_like(q)
        q2 = jnp.concatenate([jnp.where(head0, q, zero), jnp.where(head0, zero, q)], axis=0)
        k = k_ref[pl.ds(pl.multiple_of(kj * t, t), t), :]
        z = lax.dot_general(q2, k, nt, preferred_element_type=F32)
        bias = bias_ref[(qi == kj).astype(jnp.int32)]
        zbuf[slot][...] = z + jnp.concatenate([bias, bias], axis=0)

    def stage_b(slot):
        z = zbuf[slot][...]
        p = jnp.maximum(z, 0.0) + jnp.log(1.0 + jnp.exp2(jnp.abs(z) * (-LOG2E)))
        incl = jnp.dot(p.astype(BF16), tri_ref[...], preferred_element_type=F32)
        ibuf[slot][...] = incl
        return incl[:, 0:1]

    def stage_c(qi, kj, slot, stream):
        first = qi == kj
        incl = ibuf[slot][...]
        c = jnp.where(first, 0.0, c_ref[stream])
        a = jnp.exp2((zbuf[slot][...] - incl - c) * LOG2E)
        v = v_ref[pl.ds(pl.multiple_of(kj * t, t), t), :]
        pv = jnp.dot(a.astype(BF16), v, preferred_element_type=F32)
        acc = jnp.where(first, pv, acc_ref[stream] + pv)
        acc_ref[stream] = acc
        c_ref[stream] = c + incl[:, 0:1]
        o_ref[pl.ds(pl.multiple_of(qi * t, t), t), :] = (
            jnp.where(head0, acc[0:t], acc[t:2 * t]).astype(o_ref.dtype))

    def block(carry):
        qs, ks, qn, kn, skip_q, drained = carry
        qs, ks, qn, kn, skip_q = list(qs), list(ks), list(qn), list(kn), list(skip_q)
        all_done = qn[0] >= nq
        for p in range(1, ATTN_STREAMS):
            all_done = jnp.logical_and(all_done, qn[p] >= nq)
        drained = drained + all_done.astype(jnp.int32)
        for r in range(ATTN_STAGES * ATTN_STREAMS):
            sa, sc, sb = r % ATTN_STAGES, (r + 1) % ATTN_STAGES, (r + 2) % ATTN_STAGES
            pa, pb = r % ATTN_STREAMS, (r - 1) % ATTN_STREAMS
            stage_c(qs[sc], ks[sc], sc, pa)
            total_b = stage_b(sb)
            carry_b = jnp.where(qs[sb] == ks[sb], 0.0, c_ref[pb]) + total_b
            skip = skip_q[pa] == qn[pa]
            q_cur = jnp.where(skip, qn[pa] + ATTN_STREAMS, qn[pa])
            k_cur = jnp.where(skip, qn[pa] + ATTN_STREAMS, kn[pa])
            drain = q_cur >= nq
            qa = jnp.where(drain, 0, q_cur)
            ka = jnp.where(drain, 0, k_cur)
            stage_a(qa, ka, sa)
            last = k_cur == 0
            qn[pa] = jnp.where(jnp.logical_and(last, jnp.logical_not(drain)),
                               q_cur + ATTN_STREAMS, q_cur)
            kn[pa] = jnp.where(drain, k_cur, jnp.where(last, q_cur + ATTN_STREAMS, k_cur - 1))
            skip_q[pb] = jnp.where(jnp.min(carry_b) >= ATTN_SKIP, qs[sb], skip_q[pb])
            qs[sa], ks[sa] = qa, ka
        return tuple(qs), tuple(ks), tuple(qn), tuple(kn), tuple(skip_q), drained

    zero = jnp.int32(0)
    first_q = tuple(jnp.int32(p) for p in range(ATTN_STREAMS))
    init = ((zero,) * ATTN_STAGES, (zero,) * ATTN_STAGES, first_q, first_q,
            (jnp.int32(-1),) * ATTN_STREAMS, zero)
    lax.while_loop(lambda carry: carry[5] < 1, block, init)


def _attn(qkv, bsz, seq, tri, bias):
    n = bsz * seq
    t = min(ATTN_T, seq)
    pairs = SB_HEADS * SB_HEAD_DIM // LANES
    kern = functools.partial(_attn_kernel, t=t, nq=seq // t)
    return pl.pallas_call(
        kern,
        grid=(bsz, pairs),
        in_specs=[
            pl.BlockSpec((seq, LANES), lambda b, p: (b, p)),
            pl.BlockSpec((seq, LANES), lambda b, p: (b, pairs + p)),
            pl.BlockSpec((seq, LANES), lambda b, p: (b, 2 * pairs + p)),
            pl.BlockSpec((t, t), lambda b, p: (0, 0)),
            pl.BlockSpec((2, t, t), lambda b, p: (0, 0, 0)),
        ],
        out_specs=pl.BlockSpec((seq, LANES), lambda b, p: (b, p)),
        out_shape=jax.ShapeDtypeStruct((n, pairs * LANES), BF16),
        scratch_shapes=([pltpu.VMEM((2 * t, t), F32)] * (2 * ATTN_STAGES)
                        + [pltpu.VMEM((ATTN_STREAMS, 2 * t, LANES), F32),
                           pltpu.VMEM((ATTN_STREAMS, 2 * t, 1), F32)]),
        compiler_params=_params(("parallel", "parallel")),
        name="attn",
    )(qkv, qkv, qkv, tri, bias)


def _group_dot(x, bd):
    g = bd.shape[0]
    parts = [jnp.dot(x[:, i:i + g], bd, preferred_element_type=F32)
             for i in range(0, x.shape[1], g)]
    return jnp.concatenate(parts, axis=1)


def _hgrn_kernel(q_ref, f_ref, i_ref, g_ref, lbl_ref, ng_ref, tri_ref, bd_ref, o_ref,
                 st_scr, wpad, vpad, wsh, vsh, *, tt, c, layer):
    ti = pl.program_id(1)
    w = q_ref.shape[1]
    nch = tt // c

    @pl.when(ti == 0)
    def _():
        st_scr[...] = jnp.zeros_like(st_scr)

    lg = lbl_ref[...]
    e = jnp.exp(lg - jnp.max(lg, axis=0, keepdims=True))
    lb = jnp.sum(e[0:layer + 1], axis=0, keepdims=True) / jnp.sum(e, axis=0, keepdims=True)

    f = lb + (1.0 - lb) * jax.nn.sigmoid(f_ref[...])
    kk = 1.0 - f
    qv = q_ref[...]
    qs = qv * jax.nn.sigmoid(qv)
    v = i_ref[...]
    bd = bd_ref[...]
    cum = _split_dot_left(tri_ref[...], jnp.log(f) * LOG2E, 3)
    wk = jnp.log(kk) * LOG2E - cum

    pos = lax.broadcasted_iota(jnp.int32, (tt, 1), 0) % c
    wpad[0:SUBLANES, :] = jnp.zeros((SUBLANES, w), F32)
    vpad[0:SUBLANES, :] = jnp.zeros((SUBLANES, w), F32)
    wpad[SUBLANES:SUBLANES + tt, :] = wk
    vpad[SUBLANES:SUBLANES + tt, :] = v
    for b in range(SUBLANES):
        wb = wpad[SUBLANES - b:SUBLANES - b + tt, :]
        vb = vpad[SUBLANES - b:SUBLANES - b + tt, :]
        if b:
            wb = jnp.where(pos >= b, wb, -jnp.inf)
        wsh[b] = wb.reshape(nch, c, w)
        vsh[b] = vb.reshape(nch, c, w)

    qs3 = qs.reshape(nch, c, w)
    cum3 = cum.reshape(nch, c, w)
    acc = None
    for a in reversed(range(c // SUBLANES)):
        rows = c - SUBLANES * a
        qa = qs3[:, SUBLANES * a:, :].reshape(nch * rows, w)
        ca = cum3[:, SUBLANES * a:, :].reshape(nch * rows, w)
        acc_a = None
        for b in range(SUBLANES):
            wb = wsh[b, :, 0:rows, :].reshape(nch * rows, w)
            vb = vsh[b, :, 0:rows, :].reshape(nch * rows, w)
            dd = qa * jnp.exp2(ca + wb)
            term = _group_dot(dd.astype(BF16), bd) * vb
            acc_a = term if acc_a is None else acc_a + term
        acc_a = acc_a.reshape(nch, rows, w)
        if acc is not None:
            acc_a = acc_a + jnp.concatenate([jnp.zeros((nch, SUBLANES, w), F32), acc], axis=1)
        acc = acc_a
    acc = acc.reshape(tt, w)

    grp = st_scr.shape[1]
    bdmask = bd[0:grp, 0:grp] != 0
    outs = []
    for ci in range(nch):
        r0 = ci * c
        cum_c = cum[r0:r0 + c]
        last = cum_c[c - 1:c]
        qd = (qs[r0:r0 + c] * jnp.exp2(cum_c)).astype(BF16)
        kd = (kk[r0:r0 + c] * jnp.exp2(last - cum_c)).astype(BF16)
        vc = v[r0:r0 + c].astype(BF16)
        dec = jnp.exp2(last)
        o_parts = []
        for gi in range(w // grp):
            sl = slice(gi * grp, (gi + 1) * grp)
            st = st_scr[gi]
            o_parts.append(lax.dot_general(qd[:, sl], st.astype(BF16), (((1,), (1,)), ((), ())),
                                           preferred_element_type=F32))
            upd = lax.dot_general(vc[:, sl], kd[:, sl], (((0,), (0,)), ((), ())),
                                  preferred_element_type=F32)
            st_scr[gi] = st * dec[:, sl] + jnp.where(bdmask, upd, 0.0)
        outs.append(jnp.concatenate(o_parts, axis=1))
    o = acc + jnp.concatenate(outs, axis=0)

    o2 = o * o
    o2_hi = o2.astype(BF16)
    o2_lo = (o2 - o2_hi.astype(F32)).astype(BF16)
    ms = (_group_dot(o2_hi, bd) + _group_dot(o2_lo, bd)) * (1.0 / HG_DIM)
    gv = g_ref[...]
    o = o * lax.rsqrt(ms + EPS) * ng_ref[...] * (gv * jax.nn.sigmoid(gv))
    o_ref[...] = o.astype(o_ref.dtype)


def _split_dot_left(m, x, passes):
    acc = None
    r = x
    for p in range(passes):
        h = r.astype(BF16)
        term = jnp.dot(m, h, preferred_element_type=F32)
        acc = term if acc is None else acc + term
        if p + 1 < passes:
            r = r - h.astype(F32)
    return acc


def _hgrn(rest, lb_logits, norm_g, bsz, seq, layer, tri, bd):
    n = bsz * seq
    w = HG_HEADS * HG_DIM
    grp = bd.shape[0]
    tt = min(HGRN_TT, seq)
    c = min(HGRN_C, tt)
    nt = seq // tt
    kern = functools.partial(_hgrn_kernel, tt=tt, c=c, layer=layer)
    col = lambda j: pl.BlockSpec((tt, w), lambda b, i, j=j: (b * nt + i, j))
    const = lambda shape: pl.BlockSpec(shape, lambda b, i: (0, 0))
    return pl.pallas_call(
        kern,
        grid=(bsz, nt),
        in_specs=[col(0), col(1), col(2), col(3),
                  const(lb_logits.shape), const((1, w)), const((tt, tt)), const((grp, grp))],
        out_specs=pl.BlockSpec((tt, w), lambda b, i: (b * nt + i, 0)),
        out_shape=jax.ShapeDtypeStruct((n, w), BF16),
        scratch_shapes=([pltpu.VMEM((w // LANES, LANES, LANES), F32)]
                        + [pltpu.VMEM((SUBLANES + tt, w), F32)] * 2
                        + [pltpu.VMEM((SUBLANES, tt // c, c, w), F32)] * 2),
        compiler_params=_params(("parallel", "arbitrary")),
        name="hgrn",
    )(rest, rest, rest, rest, lb_logits, norm_g.reshape(1, w), tri, bd)


ROUTE_I1, ROUTE_I2, ROUTE_W1, ROUTE_W2 = N_EXPERTS, N_EXPERTS + 1, N_EXPERTS + 2, N_EXPERTS + 3


def _route(lg):
    lane = lax.broadcasted_iota(jnp.int32, lg.shape, 1)
    neg = jnp.float32(-jnp.inf)
    big = jnp.int32(LANES)
    gmask = jnp.logical_and(lane >= N_EXPERTS, lane < N_EXPERTS + N_GROUPS)
    gl = jnp.where(gmask, lg, neg)
    gmax = jnp.max(gl, axis=1, keepdims=True)
    gidx = jnp.min(jnp.where(gl == gmax, lane, big), axis=1, keepdims=True) - N_EXPERTS
    w_grp = 1.0 / jnp.sum(jnp.where(gmask, jnp.exp(gl - gmax), 0.0), axis=1, keepdims=True)
    in_grp = jnp.logical_and(lane < N_EXPERTS, lane // EXPERTS_PER_GROUP == gidx)
    l1 = jnp.where(in_grp, lg, neg)
    v1 = jnp.max(l1, axis=1, keepdims=True)
    i1 = jnp.min(jnp.where(l1 == v1, lane, big), axis=1, keepdims=True)
    l2 = jnp.where(jnp.logical_and(in_grp, lane != i1), lg, neg)
    v2 = jnp.max(l2, axis=1, keepdims=True)
    i2 = jnp.min(jnp.where(l2 == v2, lane, big), axis=1, keepdims=True)
    e2 = jnp.exp(v2 - v1)
    p1 = 1.0 / (1.0 + e2)
    p2 = e2 * p1
    out = jnp.where(jnp.logical_or(lane == i1, lane == i2), 1.0, 0.0)
    out = jnp.where(lane == ROUTE_I1, i1.astype(F32), out)
    out = jnp.where(lane == ROUTE_I2, i2.astype(F32), out)
    out = jnp.where(lane == ROUTE_W1, p1 * w_grp, out)
    return jnp.where(lane == ROUTE_W2, p2 * w_grp, out)


def _store_token_tiles(ref, x):
    rows, d = x.shape
    assert d == SUBLANES * LANES
    for s in range(SUBLANES):
        ref[pl.ds(s, rows, stride=SUBLANES), :] = x[:, s * LANES:(s + 1) * LANES]


def _load_token_tiles(ref):
    rows = ref.shape[0] // SUBLANES
    return jnp.concatenate([ref[pl.ds(s, rows, stride=SUBLANES), :] for s in range(SUBLANES)],
                           axis=1)


def _merge_kernel(x_ref, ysb_ref, yhg_ref, gsb_ref, ghg_ref, wbs_ref, wbh_ref, wo_ref,
                  ln_ref, wr_ref, br_ref, x1_ref, t_ref, route_ref):
    a = jnp.dot(ysb_ref[...], wbs_ref[...], preferred_element_type=F32)
    b = jnp.dot(yhg_ref[...], wbh_ref[...], preferred_element_type=F32)
    merged = jax.nn.sigmoid(gsb_ref[...]) * a + jax.nn.sigmoid(ghg_ref[...]) * b
    x1 = x_ref[...] + jnp.dot(merged.astype(BF16), wo_ref[...], preferred_element_type=F32)
    x1_ref[...] = x1
    var = jnp.mean(x1 * x1, axis=-1, keepdims=True)
    t = x1 * lax.rsqrt(var + EPS) * ln_ref[...]
    _store_token_tiles(t_ref, t)
    t_hi = t.astype(BF16)
    t_lo = (t - t_hi.astype(F32)).astype(BF16)
    p_hi = jnp.dot(t_hi, wr_ref[...], preferred_element_type=F32)
    p_lo = jnp.dot(t_lo, wr_ref[...], preferred_element_type=F32)
    lg = (p_hi[:, :LANES] + p_hi[:, LANES:]) + (p_lo[:, :LANES] + p_lo[:, LANES:]) + br_ref[...]
    route_ref[...] = _route(lg)


def _merge(x2, y_sb, y_hg, rest, wbs, wbh, wo, ln_g, wr, br):
    n, d = x2.shape
    tm = min(MERGE_TM, n)
    w_sb, w_hg = y_sb.shape[1], y_hg.shape[1]
    gate_blk = (rest.shape[1] - 2 * d) // d
    row = lambda wdt, j=0: pl.BlockSpec((tm, wdt), lambda i, j=j: (i, j))
    const = lambda shape: pl.BlockSpec(shape, lambda i: (0, 0))
    return pl.pallas_call(
        _merge_kernel,
        grid=(n // tm,),
        in_specs=[row(d), row(w_sb), row(w_hg), row(d, gate_blk), row(d, gate_blk + 1),
                  const(wbs.shape), const(wbh.shape), const(wo.shape), const((1, d)),
                  const(wr.shape), const((1, LANES))],
        out_specs=[row(d), pl.BlockSpec((tm * SUBLANES, LANES), lambda i: (i, 0)), row(LANES)],
        out_shape=[jax.ShapeDtypeStruct((n, d), F32),
                   jax.ShapeDtypeStruct((n * SUBLANES, LANES), F32),
                   jax.ShapeDtypeStruct((n, LANES), F32)],
        compiler_params=_params(("parallel",)),
        name="merge",
    )(x2, y_sb, y_hg, rest, rest, wbs, wbh, wo, ln_g.reshape(1, d), wr, br)


PAIR_A = (0, 0, 0, 1, 1, 3)
PAIR_B = (1, 2, 3, 3, 2, 2)
N_PAIRS = len(PAIR_A)
N_CLASSES = N_GROUPS * N_PAIRS
TOKEN_ROWS_OUT = 2 * SUBLANES


def _pair_slots(pidx):
    a = jnp.where(pidx < 3.0, 0.0, jnp.where(pidx < 5.0, 1.0, 3.0))
    b = jnp.where(pidx == 0.0, 1.0, jnp.where(pidx == 1.0, 2.0, jnp.where(pidx < 4.0, 3.0, 2.0)))
    return a, b


def _plan_kernel(route_ref, ltri_ref, utri_ref, pos_ref, wab_ref, te_ref, meta_ref,
                 tot_scr, run_scr, *, tile_rows):
    ph = pl.program_id(0)
    i = pl.program_id(1)
    route = route_ref[...]
    tm = route.shape[0]
    lane = lax.broadcasted_iota(jnp.int32, (tm, LANES), 1)
    lane_f = lane.astype(F32)
    i1 = route[:, ROUTE_I1:ROUTE_I1 + 1]
    i2 = route[:, ROUTE_I2:ROUTE_I2 + 1]
    grp = jnp.floor(i1 * (1.0 / EXPERTS_PER_GROUP))
    l1 = i1 - grp * EXPERTS_PER_GROUP
    l2 = i2 - grp * EXPERTS_PER_GROUP
    lo, hi = jnp.minimum(l1, l2), jnp.maximum(l1, l2)
    pidx = jnp.where(lo == 0.0, hi - 1.0, jnp.where(lo == 1.0, jnp.where(hi == 3.0, 3.0, 4.0), 5.0))
    cls = grp * N_PAIRS + pidx
    sel = jnp.where(lane_f == cls, 1.0, 0.0)

    @pl.when(jnp.logical_and(ph == 0, i == 0))
    def _():
        tot_scr[...] = jnp.zeros_like(tot_scr)

    @pl.when(ph == 0)
    def _():
        tot_scr[...] += jnp.sum(sel, axis=0, keepdims=True)

    @pl.when(ph == 1)
    def _():
        tot = tot_scr[...]
        padded = jnp.ceil(tot * (1.0 / tile_rows)) * tile_rows
        off = _split_dot(jnp.broadcast_to(padded, (SUBLANES, LANES)), utri_ref[...], 3)[0:1]

        @pl.when(i == 0)
        def _():
            run_scr[...] = jnp.zeros_like(run_scr)
            lane1 = lax.broadcasted_iota(jnp.int32, (1, LANES), 1)
            is_c = lane1 < N_CLASSES
            end = off + padded
            n_valid = (jnp.sum(jnp.where(is_c, padded, 0.0), axis=1, keepdims=True)
                       * (1.0 / tile_rows))
            rows = te_ref.shape[0]
            j = lax.broadcasted_iota(jnp.int32, (rows, LANES), 0).astype(F32)
            start = jnp.minimum(j, n_valid - 1.0) * tile_rows
            lane2 = lax.broadcasted_iota(jnp.int32, (rows, LANES), 1)
            done = jnp.where(jnp.logical_and(lane2 < N_CLASSES, end <= start), 1.0, 0.0)
            tc = jnp.sum(done, axis=1, keepdims=True)
            tg = jnp.floor(tc * (1.0 / N_PAIRS))
            ta, tb = _pair_slots(tc - tg * N_PAIRS)
            te = jnp.where(lane2 == 0, tg * EXPERTS_PER_GROUP + ta, tg * EXPERTS_PER_GROUP + tb)
            te_ref[...] = te.astype(jnp.int32)
            last_tile = jnp.where(jnp.logical_and(is_c, padded > 0),
                                  end * (1.0 / tile_rows) - 1.0, -1.0)
            meta = jnp.where(lane1 == N_CLASSES, n_valid, last_tile)
            meta_ref[...] = jnp.broadcast_to(meta, meta_ref.shape).astype(jnp.int32)

        rank = jnp.dot(ltri_ref[...], sel.astype(BF16), preferred_element_type=F32) + run_scr[...]
        dest = jnp.sum(jnp.where(lane_f == cls, off + rank, 0.0), axis=1, keepdims=True)
        pos_ref[...] = jnp.where(lane == 0, dest, 0.0).astype(jnp.int32)
        a_loc, _ = _pair_slots(pidx)
        first_is_a = l1 == a_loc
        w1 = route[:, ROUTE_W1:ROUTE_W1 + 1]
        w2 = route[:, ROUTE_W2:ROUTE_W2 + 1]
        wab_ref[...] = jnp.where(lane == 0, jnp.where(first_is_a, w1, w2),
                                 jnp.where(lane == 1, jnp.where(first_is_a, w2, w1), 0.0))
        run_scr[...] += jnp.sum(sel, axis=0, keepdims=True)


def _plan(route, n_tiles_max):
    n = route.shape[0]
    tm = min(PLAN_TM, n)
    r = np.arange(tm)
    ltri = jnp.asarray((r[None, :] < r[:, None]).astype(np.float32), dtype=BF16)
    e = np.arange(LANES)
    utri = jnp.asarray((e[:, None] < e[None, :]).astype(np.float32), dtype=BF16)
    te_rows = -(-n_tiles_max // SUBLANES) * SUBLANES
    kern = functools.partial(_plan_kernel, tile_rows=MOE_TM)
    return pl.pallas_call(
        kern,
        grid=(2, n // tm),
        in_specs=[pl.BlockSpec((tm, LANES), lambda p, i: (i, 0)),
                  pl.BlockSpec((tm, tm), lambda p, i: (0, 0)),
                  pl.BlockSpec((LANES, LANES), lambda p, i: (0, 0))],
        out_specs=[pl.BlockSpec((tm, LANES), lambda p, i: (i * p, 0)),
                   pl.BlockSpec((tm, LANES), lambda p, i: (i * p, 0)),
                   pl.BlockSpec((te_rows, LANES), lambda p, i: (0, 0)),
                   pl.BlockSpec((SUBLANES, LANES), lambda p, i: (0, 0))],
        out_shape=[jax.ShapeDtypeStruct((n, LANES), jnp.int32),
                   jax.ShapeDtypeStruct((n, LANES), F32),
                   jax.ShapeDtypeStruct((te_rows, LANES), jnp.int32),
                   jax.ShapeDtypeStruct((SUBLANES, LANES), jnp.int32)],
        scratch_shapes=[pltpu.VMEM((1, LANES), F32), pltpu.VMEM((1, LANES), F32)],
        compiler_params=_params(("arbitrary", "arbitrary")),
        name="plan",
    )(route, ltri, utri)


DMA_UNROLL = 8


def _token_copy(src_ref, r, dst_ref, p, sem, rows):
    return pltpu.make_async_copy(src_ref.at[pl.ds(pl.multiple_of(r * rows, rows), rows)],
                                 dst_ref.at[pl.ds(pl.multiple_of(p * rows, rows), rows)], sem)


def _dispatch_kernel(pos_ref, meta_ref, t_ref, xs_ref, zero_scr, sem, *, tm, tile_rows):
    i = pl.program_id(0)
    tile_rows = tile_rows * SUBLANES
    n_tiles = xs_ref.shape[0] // tile_rows

    @pl.when(i == 0)
    def _():
        zero_scr[...] = jnp.zeros_like(zero_scr)
        n_valid = meta_ref[N_CLASSES]
        clears = [(meta_ref[c], meta_ref[c] >= 0) for c in range(N_CLASSES)]
        clears += [(n_tiles - 1 - k, n_tiles - 1 - k >= n_valid) for k in range(N_CLASSES)]

        def clear(tile):
            return pltpu.make_async_copy(
                zero_scr, xs_ref.at[pl.ds(tile * tile_rows, tile_rows)], sem)

        for tile, cond in clears:
            @pl.when(cond)
            def _():
                clear(tile).start()
        for tile, cond in clears:
            @pl.when(cond)
            def _():
                clear(tile).wait()

    base = i * tm

    def start(r2, carry):
        for k in range(2):
            r = 2 * r2 + k
            _token_copy(t_ref, r, xs_ref, pos_ref[base + r], sem, SUBLANES).start(priority=k)
        return carry

    def wait(r, carry):
        _token_copy(t_ref, r, xs_ref, pos_ref[base + r], sem, SUBLANES).wait()
        return carry

    lax.fori_loop(0, tm // 2, start, 0, unroll=DMA_UNROLL)
    lax.fori_loop(0, tm, wait, 0, unroll=DMA_UNROLL)


def _dispatch(pos, meta, t, n_rows):
    n = t.shape[0] // SUBLANES
    tm = min(DISPATCH_TM, n)
    kern = functools.partial(_dispatch_kernel, tm=tm, tile_rows=MOE_TM)
    grid_spec = pltpu.PrefetchScalarGridSpec(
        num_scalar_prefetch=2,
        grid=(n // tm,),
        in_specs=[pl.BlockSpec((tm * SUBLANES, LANES), lambda i, *_: (i, 0))],
        out_specs=pl.BlockSpec(memory_space=pl.ANY),
        scratch_shapes=[pltpu.VMEM((MOE_TM * SUBLANES, LANES), F32), pltpu.SemaphoreType.DMA(())],
    )
    return pl.pallas_call(
        kern,
        grid_spec=grid_spec,
        out_shape=jax.ShapeDtypeStruct((n_rows * SUBLANES, LANES), F32),
        compiler_params=_params(("arbitrary",)),
        name="dispatch",
    )(pos, meta, t)


def _expert_kernel(tea_ref, teb_ref, nv_ref, xs_ref, wga, wua, wda, wgb, wub, wdb, ys_ref):
    in_use = pl.program_id(0) < nv_ref[0]
    rows = xs_ref.shape[0] // SUBLANES

    @pl.when(in_use)
    def _():
        x = _load_token_tiles(xs_ref).astype(BF16)
        for slot, (wg, wu, wd) in enumerate(((wga, wua, wda), (wgb, wub, wdb))):
            hg = jnp.dot(x, wg[...].astype(BF16), preferred_element_type=F32)
            hu = jnp.dot(x, wu[...].astype(BF16), preferred_element_type=F32)
            a = (hg * jax.nn.sigmoid(hg) * hu).astype(BF16)
            y = jnp.dot(a, wd[...].astype(BF16), preferred_element_type=F32)
            for s in range(SUBLANES):
                ys_ref[pl.ds(slot * SUBLANES + s, rows, stride=TOKEN_ROWS_OUT), :] = (
                    y[:, s * LANES:(s + 1) * LANES])

    @pl.when(jnp.logical_not(in_use))
    def _():
        ys_ref[...] = jnp.zeros_like(ys_ref)


def _experts(tea, teb, nv, xs, wg, wu, wd):
    _, d, de = wg.shape
    assert d == SUBLANES * LANES
    n_tiles = xs.shape[0] // (MOE_TM * SUBLANES)
    wa = lambda shape: pl.BlockSpec(shape, lambda j, tea, teb, nv: (tea[j], 0, 0))
    wb = lambda shape: pl.BlockSpec(shape, lambda j, tea, teb, nv: (teb[j], 0, 0))
    grid_spec = pltpu.PrefetchScalarGridSpec(
        num_scalar_prefetch=3,
        grid=(n_tiles,),
        in_specs=[pl.BlockSpec((MOE_TM * SUBLANES, LANES), lambda j, *_: (j, 0)),
                  wa((None, d, de)), wa((None, d, de)), wa((None, de, d)),
                  wb((None, d, de)), wb((None, d, de)), wb((None, de, d))],
        out_specs=pl.BlockSpec((MOE_TM * TOKEN_ROWS_OUT, LANES), lambda j, *_: (j, 0)),
    )
    return pl.pallas_call(
        _expert_kernel,
        grid_spec=grid_spec,
        out_shape=jax.ShapeDtypeStruct((n_tiles * MOE_TM * TOKEN_ROWS_OUT, LANES), F32),
        compiler_params=_params(("arbitrary",)),
        name="experts",
    )(tea, teb, nv, xs, wg, wu, wd, wg, wu, wd)


def _combine_kernel(pos_ref, ys_ref, x1_ref, wab_ref, fg_ref, o_ref, g, sem, *, tm):
    i = pl.program_id(0)
    slot = i % 2

    def gather(step, buf, wait):
        base = step * tm

        def body(r2, carry):
            for k in range(2):
                r = 2 * r2 + k
                c = _token_copy(ys_ref, pos_ref[base + r], g.at[buf], r, sem.at[buf],
                                TOKEN_ROWS_OUT)
                if wait:
                    c.wait()
                else:
                    c.start(priority=k)
            return carry

        lax.fori_loop(0, tm // 2, body, 0, unroll=DMA_UNROLL)

    @pl.when(i == 0)
    def _():
        gather(0, 0, wait=False)

    @pl.when(i + 1 < pl.num_programs(0))
    def _():
        gather(i + 1, 1 - slot, wait=False)

    gather(i, slot, wait=True)
    gs = g.at[slot]
    ya, yb = (jnp.concatenate([gs[pl.ds(off + s, tm, stride=TOKEN_ROWS_OUT), :]
                               for s in range(SUBLANES)], axis=1) for off in (0, SUBLANES))
    wab = wab_ref[...]
    x2 = x1_ref[...] + (wab[:, 0:1] * ya + wab[:, 1:2] * yb)
    var = jnp.mean(x2 * x2, axis=-1, keepdims=True)
    o_ref[...] = x2 * lax.rsqrt(var + EPS) * fg_ref[...]


def _combine(pos, ys, x1, wab, final_g):
    n, d = x1.shape
    tm = min(COMBINE_TM, n)
    kern = functools.partial(_combine_kernel, tm=tm)
    grid_spec = pltpu.PrefetchScalarGridSpec(
        num_scalar_prefetch=1,
        grid=(n // tm,),
        in_specs=[pl.BlockSpec(memory_space=pl.ANY),
                  pl.BlockSpec((tm, d), lambda i, *_: (i, 0)),
                  pl.BlockSpec((tm, LANES), lambda i, *_: (i, 0)),
                  pl.BlockSpec((1, d), lambda i, *_: (0, 0))],
        out_specs=pl.BlockSpec((tm, d), lambda i, *_: (i, 0)),
        scratch_shapes=[pltpu.VMEM((2, tm * TOKEN_ROWS_OUT, LANES), F32),
                        pltpu.SemaphoreType.DMA((2,))],
    )
    return pl.pallas_call(
        kern,
        grid_spec=grid_spec,
        out_shape=jax.ShapeDtypeStruct((n, d), F32),
        compiler_params=_params(("arbitrary",)),
        name="combine",
    )(pos, ys, x1, wab, final_g.reshape(1, d))


def _moe_sparse(t, route, wg, wu, wd, x1, final_g):
    n = x1.shape[0]
    n_tiles_max = n // MOE_TM + N_CLASSES
    pos, wab, te, meta = _plan(route, n_tiles_max)
    pos = pos[:, 0]
    xs = _dispatch(pos, meta[0, :N_CLASSES + 1], t, n_tiles_max * MOE_TM)
    ys = _experts(te[:n_tiles_max, 0], te[:n_tiles_max, 1], meta[0, N_CLASSES:N_CLASSES + 1],
                  xs, wg, wu, wd)
    return _combine(pos, ys, x1, wab, final_g)


def _suffix_ones(t):
    j = np.arange(t)[:, None]
    s = np.arange(t)[None, :]
    return jnp.asarray((j >= s).astype(np.float32), dtype=BF16)


def _causal_bias(t):
    row = np.arange(t)[:, None]
    col = np.arange(t)[None, :]
    diag = np.where(col < row, 0.0, MASK_BIAS).astype(np.float32)
    return jnp.asarray(np.stack([np.zeros_like(diag), diag]))


def _chunk_prefix_ones(tt, c):
    t = np.arange(tt)[:, None]
    j = np.arange(tt)[None, :]
    return jnp.asarray(((j <= t) & (t // c == j // c)).astype(np.float32), dtype=BF16)


def _block_diag_ones(w, blk):
    a = np.arange(w)
    return jnp.asarray((a[:, None] // blk == a[None, :] // blk).astype(np.float32), dtype=BF16)


def kernel(x, ln1_g, w_in, w_branch_sb, w_branch_hg, hg_norm_g, hg_lb_logits, w_out, ln2_g,
           w_router_group, b_router_group, w_router_expert, b_router_expert,
           w_exp_gate, w_exp_up, w_exp_down, final_g):
    bsz, seq, d = x.shape
    depth = w_in.shape[0]
    n = bsz * seq
    sb_width = SB_HEADS * SB_HEAD_DIM
    hg_width = HG_HEADS * HG_DIM

    tri_attn = _suffix_ones(min(ATTN_T, seq))
    bias_attn = _causal_bias(min(ATTN_T, seq))
    tt = min(HGRN_TT, seq)
    tri_hg = _chunk_prefix_ones(tt, min(HGRN_C, tt))
    bd = _block_diag_ones(min(MXU_DIM, hg_width), HG_DIM)

    x2 = x.reshape(n, d)
    for l in range(depth):
        qkv, rest = _inproj(x2, ln1_g[l], w_in[l].astype(BF16), sb_width)
        y_sb = _attn(qkv, bsz, seq, tri_attn, bias_attn)
        y_hg = _hgrn(rest, hg_lb_logits, hg_norm_g[l], bsz, seq, l, tri_hg, bd)

        pad = LANES - N_EXPERTS - N_GROUPS
        wr = jnp.concatenate([w_router_expert[l], w_router_group[l],
                              jnp.zeros((d, pad), F32)], axis=1)
        wr_hi = wr.astype(BF16)
        wr_lo = (wr - wr_hi.astype(F32)).astype(BF16)
        wr_split = jnp.concatenate([wr_hi, wr_lo], axis=1)
        br = jnp.concatenate([b_router_expert[l], b_router_group[l],
                              jnp.zeros((pad,), F32)]).reshape(1, LANES)

        last = l == depth - 1
        x1, t, route = _merge(x2, y_sb, y_hg, rest, w_branch_sb[l].astype(BF16),
                              w_branch_hg[l].astype(BF16), w_out[l].astype(BF16), ln2_g[l],
                              wr_split, br)
        assert last, "final rmsnorm is fused into the last layer's combine kernel"
        x2 = _moe_sparse(t, route, w_exp_gate[l], w_exp_up[l], w_exp_down[l], x1, final_g)
    return x2.reshape(bsz, seq, d)
```

```python
import functools

import jax
import jax.numpy as jnp
import numpy as np
from jax import lax
from jax.experimental import pallas as pl
from jax.experimental.pallas import tpu as pltpu

F32 = jnp.float32
BF16 = jnp.bfloat16

EPS = 1e-6
SB_HEADS = 8
SB_HEAD_DIM = 64
HG_HEADS = 8
HG_DIM = 64
N_GROUPS = 4
EXPERTS_PER_GROUP = 4
N_EXPERTS = N_GROUPS * EXPERTS_PER_GROUP

LANES = 128
SUBLANES = 8
MXU_DIM = 256
LOG2E = 1.4426950408889634
VMEM_LIMIT = 48 * 1024 * 1024

INPROJ_TM = 2048
INPROJ_TN = 512
ATTN_T = MXU_DIM
HGRN_TT = 256
HGRN_C = 32
MERGE_TM = 512
MOE_TM = 256
PLAN_TM = 1024
DISPATCH_TM = 1024
COMBINE_TM = 512


def _params(sem):
    return pltpu.CompilerParams(dimension_semantics=sem, vmem_limit_bytes=VMEM_LIMIT)


def _split_dot(x, m, passes):
    acc = None
    r = x
    for p in range(passes):
        h = r.astype(BF16)
        term = jnp.dot(h, m, preferred_element_type=F32)
        acc = term if acc is None else acc + term
        if p + 1 < passes:
            r = r - h.astype(F32)
    return acc


def _inproj_kernel(x_ref, g_ref, w_ref, qkv_ref, rest_ref, h_scr, *, q_scale):
    j = pl.program_id(1)

    @pl.when(j == 0)
    def _():
        x = x_ref[...]
        var = jnp.mean(x * x, axis=-1, keepdims=True)
        h_scr[...] = (x * lax.rsqrt(var + EPS) * g_ref[...]).astype(BF16)

    acc = jnp.dot(h_scr[...], w_ref[...], preferred_element_type=F32)
    qkv_ref[...] = (acc * jnp.where(j == 0, q_scale, 1.0)).astype(BF16)
    rest_ref[...] = acc


def _inproj(x2, ln_g, w_in, sb_width):
    n, d = x2.shape
    cols = w_in.shape[1]
    tm, tn = min(INPROJ_TM, n), INPROJ_TN
    assert sb_width == tn, "q block must be exactly one column tile"
    nq = 3 * sb_width // tn
    nj = cols // tn
    kern = functools.partial(_inproj_kernel, q_scale=SB_HEAD_DIM ** -0.5)
    return pl.pallas_call(
        kern,
        grid=(n // tm, nj),
        in_specs=[
            pl.BlockSpec((tm, d), lambda i, j: (i, 0)),
            pl.BlockSpec((1, d), lambda i, j: (0, 0)),
            pl.BlockSpec((d, tn), lambda i, j: (0, j)),
        ],
        out_specs=[
            pl.BlockSpec((tm, tn), lambda i, j: (i, jnp.minimum(j, nq))),
            pl.BlockSpec((tm, tn), lambda i, j: (i, jnp.maximum(j - nq, 0))),
        ],
        out_shape=[
            jax.ShapeDtypeStruct((n, (nq + 1) * tn), BF16),
            jax.ShapeDtypeStruct((n, cols - nq * tn), F32),
        ],
        scratch_shapes=[pltpu.VMEM((tm, d), BF16)],
        compiler_params=_params(("parallel", "arbitrary")),
        name="inproj",
    )(x2, ln_g.reshape(1, d), w_in)


ATTN_STAGES = 3
ATTN_STREAMS = 2
MASK_BIAS = -1e30
ATTN_SKIP = 111.0


def _attn_kernel(q_ref, k_ref, v_ref, tri_ref, bias_ref, o_ref,
                 z0, z1, z2, i0, i1, i2, acc_ref, c_ref, *, t, nq):
    zbuf = (z0, z1, z2)
    ibuf = (i0, i1, i2)
    for r in zbuf + ibuf:
        r[...] = jnp.zeros_like(r)
    acc_ref[...] = jnp.zeros_like(acc_ref)
    c_ref[...] = jnp.zeros_like(c_ref)

    lane = lax.broadcasted_iota(jnp.int32, (t, LANES), 1)
    head0 = lane < SB_HEAD_DIM
    nt = (((1,), (1,)), ((), ()))

    def stage_a(qi, kj, slot):
        q = q_ref[pl.ds(pl.multiple_of(qi * t, t), t), :]
        zero = jnp.zeros_like(q)
        q2 = jnp.concatenate([jnp.where(head0, q, zero), jnp.where(head0, zero, q)], axis=0)
        k = k_ref[pl.ds(pl.multiple_of(kj * t, t), t), :]
        z = lax.dot_general(q2, k, nt, preferred_element_type=F32)
        bias = bias_ref[(qi == kj).astype(jnp.int32)]
        zbuf[slot][...] = z + jnp.concatenate([bias, bias], axis=0)

    def stage_b(slot):
        z = zbuf[slot][...]
        p = jnp.maximum(z, 0.0) + jnp.log(1.0 + jnp.exp2(jnp.abs(z) * (-LOG2E)))
        incl = jnp.dot(p.astype(BF16), tri_ref[...], preferred_element_type=F32)
        ibuf[slot][...] = incl
        return incl[:, 0:1]

    def stage_c(qi, kj, slot, stream):
        first = qi == kj
        incl = ibuf[slot][...]
        c = jnp.where(first, 0.0, c_ref[stream])
        a = jnp.exp2((zbuf[slot][...] - incl - c) * LOG2E)
        v = v_ref[pl.ds(pl.multiple_of(kj * t, t), t), :]
        pv = jnp.dot(a.astype(BF16), v, preferred_element_type=F32)
        acc = jnp.where(first, pv, acc_ref[stream] + pv)
        acc_ref[stream] = acc
        c_ref[stream] = c + incl[:, 0:1]
        o_ref[pl.ds(pl.multiple_of(qi * t, t), t), :] = (
            jnp.where(head0, acc[0:t], acc[t:2 * t]).astype(o_ref.dtype))

    def block(carry):
        qs, ks, qn, kn, skip_q, drained = carry
        qs, ks, qn, kn, skip_q = list(qs), list(ks), list(qn), list(kn), list(skip_q)
        all_done = qn[0] >= nq
        for p in range(1, ATTN_STREAMS):
            all_done = jnp.logical_and(all_done, qn[p] >= nq)
        drained = drained + all_done.astype(jnp.int32)
        for r in range(ATTN_STAGES * ATTN_STREAMS):
            sa, sc, sb = r % ATTN_STAGES, (r + 1) % ATTN_STAGES, (r + 2) % ATTN_STAGES
            pa, pb = r % ATTN_STREAMS, (r - 1) % ATTN_STREAMS
            stage_c(qs[sc], ks[sc], sc, pa)
            total_b = stage_b(sb)
            carry_b = jnp.where(qs[sb] == ks[sb], 0.0, c_ref[pb]) + total_b
            skip = skip_q[pa] == qn[pa]
            q_cur = jnp.where(skip, qn[pa] + ATTN_STREAMS, qn[pa])
            k_cur = jnp.where(skip, qn[pa] + ATTN_STREAMS, kn[pa])
            drain = q_cur >= nq
            qa = jnp.where(drain, 0, q_cur)
            ka = jnp.where(drain, 0, k_cur)
            stage_a(qa, ka, sa)
            last = k_cur == 0
            qn[pa] = jnp.where(jnp.logical_and(last, jnp.logical_not(drain)),
                               q_cur + ATTN_STREAMS, q_cur)
            kn[pa] = jnp.where(drain, k_cur, jnp.where(last, q_cur + ATTN_STREAMS, k_cur - 1))
            skip_q[pb] = jnp.where(jnp.min(carry_b) >= ATTN_SKIP, qs[sb], skip_q[pb])
            qs[sa], ks[sa] = qa, ka
        return tuple(qs), tuple(ks), tuple(qn), tuple(kn), tuple(skip_q), drained

    zero = jnp.int32(0)
    first_q = tuple(jnp.int32(p) for p in range(ATTN_STREAMS))
    init = ((zero,) * ATTN_STAGES, (zero,) * ATTN_STAGES, first_q, first_q,
            (jnp.int32(-1),) * ATTN_STREAMS, zero)
    lax.while_loop(lambda carry: carry[5] < 1, block, init)


def _attn(qkv, bsz, seq, tri, bias):
    n = bsz * seq
    t = min(ATTN_T, seq)
    pairs = SB_HEADS * SB_HEAD_DIM // LANES
    kern = functools.partial(_attn_kernel, t=t, nq=seq // t)
    return pl.pallas_call(
        kern,
        grid=(bsz, pairs),
        in_specs=[
            pl.BlockSpec((seq, LANES), lambda b, p: (b, p)),
            pl.BlockSpec((seq, LANES), lambda b, p: (b, pairs + p)),
            pl.BlockSpec((seq, LANES), lambda b, p: (b, 2 * pairs + p)),
            pl.BlockSpec((t, t), lambda b, p: (0, 0)),
            pl.BlockSpec((2, t, t), lambda b, p: (0, 0, 0)),
        ],
        out_specs=pl.BlockSpec((seq, LANES), lambda b, p: (b, p)),
        out_shape=jax.ShapeDtypeStruct((n, pairs * LANES), BF16),
        scratch_shapes=([pltpu.VMEM((2 * t, t), F32)] * (2 * ATTN_STAGES)
                        + [pltpu.VMEM((ATTN_STREAMS, 2 * t, LANES), F32),
                           pltpu.VMEM((ATTN_STREAMS, 2 * t, 1), F32)]),
        compiler_params=_params(("parallel", "parallel")),
        name="attn",
    )(qkv, qkv, qkv, tri, bias)


def _group_dot(x, bd):
    g = bd.shape[0]
    parts = [jnp.dot(x[:, i:i + g], bd, preferred_element_type=F32)
             for i in range(0, x.shape[1], g)]
    return jnp.concatenate(parts, axis=1)


def _hgrn_kernel(q_ref, f_ref, i_ref, g_ref, lbl_ref, ng_ref, tri_ref, bd_ref, o_ref,
                 st_scr, wpad, vpad, wsh, vsh, *, tt, c, layer):
    ti = pl.program_id(1)
    w = q_ref.shape[1]
    nch = tt // c

    @pl.when(ti == 0)
    def _():
        st_scr[...] = jnp.zeros_like(st_scr)

    lg = lbl_ref[...]
    e = jnp.exp(lg - jnp.max(lg, axis=0, keepdims=True))
    lb = jnp.sum(e[0:layer + 1], axis=0, keepdims=True) / jnp.sum(e, axis=0, keepdims=True)

    f = lb + (1.0 - lb) * jax.nn.sigmoid(f_ref[...])
    kk = 1.0 - f
    qv = q_ref[...]
    qs = qv * jax.nn.sigmoid(qv)
    v = i_ref[...]
    bd = bd_ref[...]
    cum = _split_dot_left(tri_ref[...], jnp.log(f) * LOG2E, 3)
    wk = jnp.log(kk) * LOG2E - cum

    pos = lax.broadcasted_iota(jnp.int32, (tt, 1), 0) % c
    wpad[0:SUBLANES, :] = jnp.zeros((SUBLANES, w), F32)
    vpad[0:SUBLANES, :] = jnp.zeros((SUBLANES, w), F32)
    wpad[SUBLANES:SUBLANES + tt, :] = wk
    vpad[SUBLANES:SUBLANES + tt, :] = v
    for b in range(SUBLANES):
        wb = wpad[SUBLANES - b:SUBLANES - b + tt, :]
        vb = vpad[SUBLANES - b:SUBLANES - b + tt, :]
        if b:
            wb = jnp.where(pos >= b, wb, -jnp.inf)
        wsh[b] = wb.reshape(nch, c, w)
        vsh[b] = vb.reshape(nch, c, w)

    qs3 = qs.reshape(nch, c, w)
    cum3 = cum.reshape(nch, c, w)
    acc = None
    for a in reversed(range(c // SUBLANES)):
        rows = c - SUBLANES * a
        qa = qs3[:, SUBLANES * a:, :].reshape(nch * rows, w)
        ca = cum3[:, SUBLANES * a:, :].reshape(nch * rows, w)
        acc_a = None
        for b in range(SUBLANES):
            wb = wsh[b, :, 0:rows, :].reshape(nch * rows, w)
            vb = vsh[b, :, 0:rows, :].reshape(nch * rows, w)
            dd = qa * jnp.exp2(ca + wb)
            term = _group_dot(dd.astype(BF16), bd) * vb
            acc_a = term if acc_a is None else acc_a + term
        acc_a = acc_a.reshape(nch, rows, w)
        if acc is not None:
            acc_a = acc_a + jnp.concatenate([jnp.zeros((nch, SUBLANES, w), F32), acc], axis=1)
        acc = acc_a
    acc = acc.reshape(tt, w)

    grp = st_scr.shape[1]
    bdmask = bd[0:grp, 0:grp] != 0
    outs = []
    for ci in range(nch):
        r0 = ci * c
        cum_c = cum[r0:r0 + c]
        last = cum_c[c - 1:c]
        qd = (qs[r0:r0 + c] * jnp.exp2(cum_c)).astype(BF16)
        kd = (kk[r0:r0 + c] * jnp.exp2(last - cum_c)).astype(BF16)
        vc = v[r0:r0 + c].astype(BF16)
        dec = jnp.exp2(last)
        o_parts = []
        for gi in range(w // grp):
            sl = slice(gi * grp, (gi + 1) * grp)
            st = st_scr[gi]
            o_parts.append(lax.dot_general(qd[:, sl], st.astype(BF16), (((1,), (1,)), ((), ())),
                                           preferred_element_type=F32))
            upd = lax.dot_general(vc[:, sl], kd[:, sl], (((0,), (0,)), ((), ())),
                                  preferred_element_type=F32)
            st_scr[gi] = st * dec[:, sl] + jnp.where(bdmask, upd, 0.0)
        outs.append(jnp.concatenate(o_parts, axis=1))
    o = acc + jnp.concatenate(outs, axis=0)

    o2 = o * o
    o2_hi = o2.astype(BF16)
    o2_lo = (o2 - o2_hi.astype(F32)).astype(BF16)
    ms = (_group_dot(o2_hi, bd) + _group_dot(o2_lo, bd)) * (1.0 / HG_DIM)
    gv = g_ref[...]
    o = o * lax.rsqrt(ms + EPS) * ng_ref[...] * (gv * jax.nn.sigmoid(gv))
    o_ref[...] = o.astype(o_ref.dtype)


def _split_dot_left(m, x, passes):
    acc = None
    r = x
    for p in range(passes):
        h = r.astype(BF16)
        term = jnp.dot(m, h, preferred_element_type=F32)
        acc = term if acc is None else acc + term
        if p + 1 < passes:
            r = r - h.astype(F32)
    return acc


def _hgrn(rest, lb_logits, norm_g, bsz, seq, layer, tri, bd):
    n = bsz * seq
    w = HG_HEADS * HG_DIM
    grp = bd.shape[0]
    tt = min(HGRN_TT, seq)
    c = min(HGRN_C, tt)
    nt = seq // tt
    kern = functools.partial(_hgrn_kernel, tt=tt, c=c, layer=layer)
    col = lambda j: pl.BlockSpec((tt, w), lambda b, i, j=j: (b * nt + i, j))
    const = lambda shape: pl.BlockSpec(shape, lambda b, i: (0, 0))
    return pl.pallas_call(
        kern,
        grid=(bsz, nt),
        in_specs=[col(0), col(1), col(2), col(3),
                  const(lb_logits.shape), const((1, w)), const((tt, tt)), const((grp, grp))],
        out_specs=pl.BlockSpec((tt, w), lambda b, i: (b * nt + i, 0)),
        out_shape=jax.ShapeDtypeStruct((n, w), BF16),
        scratch_shapes=([pltpu.VMEM((w // LANES, LANES, LANES), F32)]
                        + [pltpu.VMEM((SUBLANES + tt, w), F32)] * 2
                        + [pltpu.VMEM((SUBLANES, tt // c, c, w), F32)] * 2),
        compiler_params=_params(("parallel", "arbitrary")),
        name="hgrn",
    )(rest, rest, rest, rest, lb_logits, norm_g.reshape(1, w), tri, bd)


PAIR_A = (0, 0, 0, 1, 1, 3)
PAIR_B = (1, 2, 3, 3, 2, 2)
N_PAIRS = len(PAIR_A)
N_CLASSES = N_GROUPS * N_PAIRS
ROUTE_CLS, ROUTE_WA, ROUTE_WB = 0, 1, 2


def _pair_slots(pidx):
    a = jnp.where(pidx < 3.0, 0.0, jnp.where(pidx < 5.0, 1.0, 3.0))
    b = jnp.where(pidx == 0.0, 1.0, jnp.where(pidx == 1.0, 2.0, jnp.where(pidx < 4.0, 3.0, 2.0)))
    return a, b


def _route(lg):
    lane = lax.broadcasted_iota(jnp.int32, lg.shape, 1)
    neg = jnp.float32(-jnp.inf)
    big = jnp.int32(LANES)
    gmask = jnp.logical_and(lane >= N_EXPERTS, lane < N_EXPERTS + N_GROUPS)
    gl = jnp.where(gmask, lg, neg)
    gmax = jnp.max(gl, axis=1, keepdims=True)
    gidx = jnp.min(jnp.where(gl == gmax, lane, big), axis=1, keepdims=True) - N_EXPERTS
    w_grp = 1.0 / jnp.sum(jnp.where(gmask, jnp.exp(gl - gmax), 0.0), axis=1, keepdims=True)
    in_grp = jnp.logical_and(lane < N_EXPERTS, lane // EXPERTS_PER_GROUP == gidx)
    l1 = jnp.where(in_grp, lg, neg)
    v1 = jnp.max(l1, axis=1, keepdims=True)
    i1 = jnp.min(jnp.where(l1 == v1, lane, big), axis=1, keepdims=True)
    l2 = jnp.where(jnp.logical_and(in_grp, lane != i1), lg, neg)
    v2 = jnp.max(l2, axis=1, keepdims=True)
    i2 = jnp.min(jnp.where(l2 == v2, lane, big), axis=1, keepdims=True)
    e2 = jnp.exp(v2 - v1)
    p1 = 1.0 / (1.0 + e2)
    p2 = e2 * p1
    loc1 = (i1 - gidx * EXPERTS_PER_GROUP).astype(F32)
    loc2 = (i2 - gidx * EXPERTS_PER_GROUP).astype(F32)
    lo, hi = jnp.minimum(loc1, loc2), jnp.maximum(loc1, loc2)
    pidx = jnp.where(lo == 0.0, hi - 1.0, jnp.where(lo == 1.0, jnp.where(hi == 3.0, 3.0, 4.0), 5.0))
    cls = gidx.astype(F32) * N_PAIRS + pidx
    a_loc, _ = _pair_slots(pidx)
    first_is_a = loc1 == a_loc
    wa = jnp.where(first_is_a, p1, p2) * w_grp
    wb = jnp.where(first_is_a, p2, p1) * w_grp
    return jnp.where(lane == ROUTE_CLS, cls,
                     jnp.where(lane == ROUTE_WA, wa, jnp.where(lane == ROUTE_WB, wb, 0.0)))


def _store_token_tiles(ref, x):
    rows, d = x.shape
    assert d == SUBLANES * LANES
    for s in range(SUBLANES):
        ref[pl.ds(s, rows, stride=SUBLANES), :] = x[:, s * LANES:(s + 1) * LANES]


def _load_token_tiles(ref):
    rows = ref.shape[0] // SUBLANES
    return jnp.concatenate([ref[pl.ds(s, rows, stride=SUBLANES), :] for s in range(SUBLANES)],
                           axis=1)


def _merge_kernel(x_ref, ysb_ref, yhg_ref, gsb_ref, ghg_ref, wbs_ref, wbh_ref, wo_ref,
                  ln_ref, wr_ref, br_ref, x1_ref, t_ref, route_ref):
    a = jnp.dot(ysb_ref[...], wbs_ref[...], preferred_element_type=F32)
    b = jnp.dot(yhg_ref[...], wbh_ref[...], preferred_element_type=F32)
    merged = jax.nn.sigmoid(gsb_ref[...]) * a + jax.nn.sigmoid(ghg_ref[...]) * b
    x1 = x_ref[...] + jnp.dot(merged.astype(BF16), wo_ref[...], preferred_element_type=F32)
    x1_ref[...] = x1
    var = jnp.mean(x1 * x1, axis=-1, keepdims=True)
    t = x1 * lax.rsqrt(var + EPS) * ln_ref[...]
    _store_token_tiles(t_ref, t)
    t_hi = t.astype(BF16)
    t_lo = (t - t_hi.astype(F32)).astype(BF16)
    p_hi = jnp.dot(t_hi, wr_ref[...], preferred_element_type=F32)
    p_lo = jnp.dot(t_lo, wr_ref[...], preferred_element_type=F32)
    lg = (p_hi[:, :LANES] + p_hi[:, LANES:]) + (p_lo[:, :LANES] + p_lo[:, LANES:]) + br_ref[...]
    route_ref[...] = _route(lg)


def _merge(x2, y_sb, y_hg, rest, wbs, wbh, wo, ln_g, wr, br):
    n, d = x2.shape
    tm = min(MERGE_TM, n)
    w_sb, w_hg = y_sb.shape[1], y_hg.shape[1]
    gate_blk = (rest.shape[1] - 2 * d) // d
    row = lambda wdt, j=0: pl.BlockSpec((tm, wdt), lambda i, j=j: (i, j))
    const = lambda shape: pl.BlockSpec(shape, lambda i: (0, 0))
    return pl.pallas_call(
        _merge_kernel,
        grid=(n // tm,),
        in_specs=[row(d), row(w_sb), row(w_hg), row(d, gate_blk), row(d, gate_blk + 1),
                  const(wbs.shape), const(wbh.shape), const(wo.shape), const((1, d)),
                  const(wr.shape), const((1, LANES))],
        out_specs=[row(d), pl.BlockSpec((tm * SUBLANES, LANES), lambda i: (i, 0)), row(LANES)],
        out_shape=[jax.ShapeDtypeStruct((n, d), F32),
                   jax.ShapeDtypeStruct((n * SUBLANES, LANES), F32),
                   jax.ShapeDtypeStruct((n, LANES), F32)],
        compiler_params=_params(("parallel",)),
        name="merge",
    )(x2, y_sb, y_hg, rest, rest, wbs, wbh, wo, ln_g.reshape(1, d), wr, br)


TOKEN_ROWS_OUT = 2 * SUBLANES


def _plan_kernel(route_ref, ltri_ref, utri_ref, pos_ref, te_ref, meta_ref,
                 tot_scr, run_scr, *, tile_rows):
    ph = pl.program_id(0)
    i = pl.program_id(1)
    route = route_ref[...]
    tm = route.shape[0]
    lane = lax.broadcasted_iota(jnp.int32, (tm, LANES), 1)
    is_cls = lane.astype(F32) == route[:, ROUTE_CLS:ROUTE_CLS + 1]
    sel = jnp.where(is_cls, 1.0, 0.0)

    @pl.when(jnp.logical_and(ph == 0, i == 0))
    def _():
        tot_scr[...] = jnp.zeros_like(tot_scr)

    @pl.when(ph == 0)
    def _():
        tot_scr[...] += jnp.sum(sel, axis=0, keepdims=True)

    @pl.when(ph == 1)
    def _():
        tot = tot_scr[...]
        padded = jnp.ceil(tot * (1.0 / tile_rows)) * tile_rows
        off = _split_dot(jnp.broadcast_to(padded, (SUBLANES, LANES)), utri_ref[...], 3)[0:1]

        @pl.when(i == 0)
        def _():
            run_scr[...] = jnp.zeros_like(run_scr)
            lane1 = lax.broadcasted_iota(jnp.int32, (1, LANES), 1)
            is_c = lane1 < N_CLASSES
            end = off + padded
            n_valid = (jnp.sum(jnp.where(is_c, padded, 0.0), axis=1, keepdims=True)
                       * (1.0 / tile_rows))
            rows = te_ref.shape[0]
            j = lax.broadcasted_iota(jnp.int32, (rows, LANES), 0).astype(F32)
            start = jnp.minimum(j, n_valid - 1.0) * tile_rows
            lane2 = lax.broadcasted_iota(jnp.int32, (rows, LANES), 1)
            done = jnp.where(jnp.logical_and(lane2 < N_CLASSES, end <= start), 1.0, 0.0)
            tc = jnp.sum(done, axis=1, keepdims=True)
            tg = jnp.floor(tc * (1.0 / N_PAIRS))
            ta, tb = _pair_slots(tc - tg * N_PAIRS)
            te = jnp.where(lane2 == 0, tg * EXPERTS_PER_GROUP + ta, tg * EXPERTS_PER_GROUP + tb)
            te_ref[...] = te.astype(jnp.int32)
            last_tile = jnp.where(jnp.logical_and(is_c, padded > 0),
                                  end * (1.0 / tile_rows) - 1.0, -1.0)
            meta = jnp.where(lane1 == N_CLASSES, n_valid, last_tile)
            meta_ref[...] = jnp.broadcast_to(meta, meta_ref.shape).astype(jnp.int32)

        rank = jnp.dot(ltri_ref[...], sel.astype(BF16), preferred_element_type=F32) + run_scr[...]
        dest = jnp.sum(jnp.where(is_cls, off + rank, 0.0), axis=1, keepdims=True)
        pos_ref[...] = jnp.where(lane == 0, dest, 0.0).astype(jnp.int32)
        run_scr[...] += jnp.sum(sel, axis=0, keepdims=True)


def _plan(route, n_tiles_max):
    n = route.shape[0]
    tm = min(PLAN_TM, n)
    r = np.arange(tm)
    ltri = jnp.asarray((r[None, :] < r[:, None]).astype(np.float32), dtype=BF16)
    e = np.arange(LANES)
    utri = jnp.asarray((e[:, None] < e[None, :]).astype(np.float32), dtype=BF16)
    te_rows = -(-n_tiles_max // SUBLANES) * SUBLANES
    kern = functools.partial(_plan_kernel, tile_rows=MOE_TM)
    return pl.pallas_call(
        kern,
        grid=(2, n // tm),
        in_specs=[pl.BlockSpec((tm, LANES), lambda p, i: (i, 0)),
                  pl.BlockSpec((tm, tm), lambda p, i: (0, 0)),
                  pl.BlockSpec((LANES, LANES), lambda p, i: (0, 0))],
        out_specs=[pl.BlockSpec((tm, LANES), lambda p, i: (i * p, 0)),
                   pl.BlockSpec((te_rows, LANES), lambda p, i: (0, 0)),
                   pl.BlockSpec((SUBLANES, LANES), lambda p, i: (0, 0))],
        out_shape=[jax.ShapeDtypeStruct((n, LANES), jnp.int32),
                   jax.ShapeDtypeStruct((te_rows, LANES), jnp.int32),
                   jax.ShapeDtypeStruct((SUBLANES, LANES), jnp.int32)],
        scratch_shapes=[pltpu.VMEM((1, LANES), F32), pltpu.VMEM((1, LANES), F32)],
        compiler_params=_params(("arbitrary", "arbitrary")),
        name="plan",
    )(route, ltri, utri)


DMA_UNROLL = 8


def _token_copy(src_ref, r, dst_ref, p, sem, rows):
    return pltpu.make_async_copy(src_ref.at[pl.ds(pl.multiple_of(r * rows, rows), rows)],
                                 dst_ref.at[pl.ds(pl.multiple_of(p * rows, rows), rows)], sem)


def _dispatch_kernel(pos_ref, meta_ref, t_ref, xs_ref, zero_scr, sem, *, tm, tile_rows):
    i = pl.program_id(0)
    tile_rows = tile_rows * SUBLANES
    n_tiles = xs_ref.shape[0] // tile_rows

    @pl.when(i == 0)
    def _():
        zero_scr[...] = jnp.zeros_like(zero_scr)
        n_valid = meta_ref[N_CLASSES]
        clears = [(meta_ref[c], meta_ref[c] >= 0) for c in range(N_CLASSES)]
        clears += [(n_tiles - 1 - k, n_tiles - 1 - k >= n_valid) for k in range(N_CLASSES)]

        def clear(tile):
            return pltpu.make_async_copy(
                zero_scr, xs_ref.at[pl.ds(tile * tile_rows, tile_rows)], sem)

        for tile, cond in clears:
            @pl.when(cond)
            def _():
                clear(tile).start()
        for tile, cond in clears:
            @pl.when(cond)
            def _():
                clear(tile).wait()

    base = i * tm

    def start(r2, carry):
        for k in range(2):
            r = 2 * r2 + k
            _token_copy(t_ref, r, xs_ref, pos_ref[base + r], sem, SUBLANES).start(priority=k)
        return carry

    def wait(r, carry):
        _token_copy(t_ref, r, xs_ref, pos_ref[base + r], sem, SUBLANES).wait()
        return carry

    lax.fori_loop(0, tm // 2, start, 0, unroll=DMA_UNROLL)
    lax.fori_loop(0, tm, wait, 0, unroll=DMA_UNROLL)


def _dispatch(pos, meta, t, n_rows):
    n = t.shape[0] // SUBLANES
    tm = min(DISPATCH_TM, n)
    kern = functools.partial(_dispatch_kernel, tm=tm, tile_rows=MOE_TM)
    grid_spec = pltpu.PrefetchScalarGridSpec(
        num_scalar_prefetch=2,
        grid=(n // tm,),
        in_specs=[pl.BlockSpec((tm * SUBLANES, LANES), lambda i, *_: (i, 0))],
        out_specs=pl.BlockSpec(memory_space=pl.ANY),
        scratch_shapes=[pltpu.VMEM((MOE_TM * SUBLANES, LANES), F32), pltpu.SemaphoreType.DMA(())],
    )
    return pl.pallas_call(
        kern,
        grid_spec=grid_spec,
        out_shape=jax.ShapeDtypeStruct((n_rows * SUBLANES, LANES), F32),
        compiler_params=_params(("arbitrary",)),
        name="dispatch",
    )(pos, meta, t)


def _expert_kernel(tea_ref, teb_ref, nv_ref, xs_ref, wga, wua, wda, wgb, wub, wdb, ys_ref):
    in_use = pl.program_id(0) < nv_ref[0]
    rows = xs_ref.shape[0] // SUBLANES

    @pl.when(in_use)
    def _():
        x = _load_token_tiles(xs_ref).astype(BF16)
        for slot, (wg, wu, wd) in enumerate(((wga, wua, wda), (wgb, wub, wdb))):
            hg = jnp.dot(x, wg[...].astype(BF16), preferred_element_type=F32)
            hu = jnp.dot(x, wu[...].astype(BF16), preferred_element_type=F32)
            a = (hg * jax.nn.sigmoid(hg) * hu).astype(BF16)
            y = jnp.dot(a, wd[...].astype(BF16), preferred_element_type=F32)
            for s in range(SUBLANES):
                ys_ref[pl.ds(slot * SUBLANES + s, rows, stride=TOKEN_ROWS_OUT), :] = (
                    y[:, s * LANES:(s + 1) * LANES])

    @pl.when(jnp.logical_not(in_use))
    def _():
        ys_ref[...] = jnp.zeros_like(ys_ref)


def _experts(tea, teb, nv, xs, wg, wu, wd):
    _, d, de = wg.shape
    assert d == SUBLANES * LANES
    n_tiles = xs.shape[0] // (MOE_TM * SUBLANES)
    wa = lambda shape: pl.BlockSpec(shape, lambda j, tea, teb, nv: (tea[j], 0, 0))
    wb = lambda shape: pl.BlockSpec(shape, lambda j, tea, teb, nv: (teb[j], 0, 0))
    grid_spec = pltpu.PrefetchScalarGridSpec(
        num_scalar_prefetch=3,
        grid=(n_tiles,),
        in_specs=[pl.BlockSpec((MOE_TM * SUBLANES, LANES), lambda j, *_: (j, 0)),
                  wa((None, d, de)), wa((None, d, de)), wa((None, de, d)),
                  wb((None, d, de)), wb((None, d, de)), wb((None, de, d))],
        out_specs=pl.BlockSpec((MOE_TM * TOKEN_ROWS_OUT, LANES), lambda j, *_: (j, 0)),
    )
    return pl.pallas_call(
        _expert_kernel,
        grid_spec=grid_spec,
        out_shape=jax.ShapeDtypeStruct((n_tiles * MOE_TM * TOKEN_ROWS_OUT, LANES), F32),
        compiler_params=_params(("arbitrary",)),
        name="experts",
    )(tea, teb, nv, xs, wg, wu, wd, wg, wu, wd)


def _combine_kernel(pos_ref, ys_ref, x1_ref, route_ref, fg_ref, o_ref, g, sem, *, tm):
    i = pl.program_id(0)
    slot = i % 2

    def gather(step, buf, wait):
        base = step * tm

        def body(r2, carry):
            for k in range(2):
                r = 2 * r2 + k
                c = _token_copy(ys_ref, pos_ref[base + r], g.at[buf], r, sem.at[buf],
                                TOKEN_ROWS_OUT)
                if wait:
                    c.wait()
                else:
                    c.start(priority=k)
            return carry

        lax.fori_loop(0, tm // 2, body, 0, unroll=DMA_UNROLL)

    @pl.when(i == 0)
    def _():
        gather(0, 0, wait=False)

    @pl.when(i + 1 < pl.num_programs(0))
    def _():
        gather(i + 1, 1 - slot, wait=False)

    gather(i, slot, wait=True)
    gs = g.at[slot]
    ya, yb = (jnp.concatenate([gs[pl.ds(off + s, tm, stride=TOKEN_ROWS_OUT), :]
                               for s in range(SUBLANES)], axis=1) for off in (0, SUBLANES))
    route = route_ref[...]
    x2 = x1_ref[...] + (route[:, ROUTE_WA:ROUTE_WA + 1] * ya + route[:, ROUTE_WB:ROUTE_WB + 1] * yb)
    var = jnp.mean(x2 * x2, axis=-1, keepdims=True)
    o_ref[...] = x2 * lax.rsqrt(var + EPS) * fg_ref[...]


def _combine(pos, ys, x1, route, final_g):
    n, d = x1.shape
    tm = min(COMBINE_TM, n)
    kern = functools.partial(_combine_kernel, tm=tm)
    grid_spec = pltpu.PrefetchScalarGridSpec(
        num_scalar_prefetch=1,
        grid=(n // tm,),
        in_specs=[pl.BlockSpec(memory_space=pl.ANY),
                  pl.BlockSpec((tm, d), lambda i, *_: (i, 0)),
                  pl.BlockSpec((tm, LANES), lambda i, *_: (i, 0)),
                  pl.BlockSpec((1, d), lambda i, *_: (0, 0))],
        out_specs=pl.BlockSpec((tm, d), lambda i, *_: (i, 0)),
        scratch_shapes=[pltpu.VMEM((2, tm * TOKEN_ROWS_OUT, LANES), F32),
                        pltpu.SemaphoreType.DMA((2,))],
    )
    return pl.pallas_call(
        kern,
        grid_spec=grid_spec,
        out_shape=jax.ShapeDtypeStruct((n, d), F32),
        compiler_params=_params(("arbitrary",)),
        name="combine",
    )(pos, ys, x1, route, final_g.reshape(1, d))


def _moe_sparse(t, route, wg, wu, wd, x1, final_g):
    n = x1.shape[0]
    n_tiles_max = n // MOE_TM + N_CLASSES
    pos, te, meta = _plan(route, n_tiles_max)
    pos = pos[:, 0]
    xs = _dispatch(pos, meta[0, :N_CLASSES + 1], t, n_tiles_max * MOE_TM)
    ys = _experts(te[:n_tiles_max, 0], te[:n_tiles_max, 1], meta[0, N_CLASSES:N_CLASSES + 1],
                  xs, wg, wu, wd)
    return _combine(pos, ys, x1, route, final_g)


def _suffix_ones(t):
    j = np.arange(t)[:, None]
    s = np.arange(t)[None, :]
    return jnp.asarray((j >= s).astype(np.float32), dtype=BF16)


def _causal_bias(t):
    row = np.arange(t)[:, None]
    col = np.arange(t)[None, :]
    diag = np.where(col < row, 0.0, MASK_BIAS).astype(np.float32)
    return jnp.asarray(np.stack([np.zeros_like(diag), diag]))


def _chunk_prefix_ones(tt, c):
    t = np.arange(tt)[:, None]
    j = np.arange(tt)[None, :]
    return jnp.asarray(((j <= t) & (t // c == j // c)).astype(np.float32), dtype=BF16)


def _block_diag_ones(w, blk):
    a = np.arange(w)
    return jnp.asarray((a[:, None] // blk == a[None, :] // blk).astype(np.float32), dtype=BF16)


def kernel(x, ln1_g, w_in, w_branch_sb, w_branch_hg, hg_norm_g, hg_lb_logits, w_out, ln2_g,
           w_router_group, b_router_group, w_router_expert, b_router_expert,
           w_exp_gate, w_exp_up, w_exp_down, final_g):
    bsz, seq, d = x.shape
    depth = w_in.shape[0]
    n = bsz * seq
    sb_width = SB_HEADS * SB_HEAD_DIM
    hg_width = HG_HEADS * HG_DIM

    tri_attn = _suffix_ones(min(ATTN_T, seq))
    bias_attn = _causal_bias(min(ATTN_T, seq))
    tt = min(HGRN_TT, seq)
    tri_hg = _chunk_prefix_ones(tt, min(HGRN_C, tt))
    bd = _block_diag_ones(min(MXU_DIM, hg_width), HG_DIM)

    x2 = x.reshape(n, d)
    for l in range(depth):
        qkv, rest = _inproj(x2, ln1_g[l], w_in[l].astype(BF16), sb_width)
        y_sb = _attn(qkv, bsz, seq, tri_attn, bias_attn)
        y_hg = _hgrn(rest, hg_lb_logits, hg_norm_g[l], bsz, seq, l, tri_hg, bd)

        pad = LANES - N_EXPERTS - N_GROUPS
        wr = jnp.concatenate([w_router_expert[l], w_router_group[l],
                              jnp.zeros((d, pad), F32)], axis=1)
        wr_hi = wr.astype(BF16)
        wr_lo = (wr - wr_hi.astype(F32)).astype(BF16)
        wr_split = jnp.concatenate([wr_hi, wr_lo], axis=1)
        br = jnp.concatenate([b_router_expert[l], b_router_group[l],
                              jnp.zeros((pad,), F32)]).reshape(1, LANES)

        last = l == depth - 1
        x1, t, route = _merge(x2, y_sb, y_hg, rest, w_branch_sb[l].astype(BF16),
                              w_branch_hg[l].astype(BF16), w_out[l].astype(BF16), ln2_g[l],
                              wr_split, br)
        assert last, "final rmsnorm is fused into the last layer's combine kernel"
        x2 = _moe_sparse(t, route, w_exp_gate[l], w_exp_up[l], w_exp_down[l], x1, final_g)
    return x2.reshape(bsz, seq, d)
```

```python
import functools

import jax
import jax.numpy as jnp
import numpy as np
from jax import lax
from jax.experimental import pallas as pl
from jax.experimental.pallas import tpu as pltpu

F32 = jnp.float32
BF16 = jnp.bfloat16

EPS = 1e-6
SB_HEADS = 8
SB_HEAD_DIM = 64
HG_HEADS = 8
HG_DIM = 64
N_GROUPS = 4
EXPERTS_PER_GROUP = 4
N_EXPERTS = N_GROUPS * EXPERTS_PER_GROUP

LANES = 128
SUBLANES = 8
MXU_DIM = 256
LOG2E = 1.4426950408889634
VMEM_LIMIT = 48 * 1024 * 1024

INPROJ_TM = 2048
INPROJ_TN = 512
ATTN_T = MXU_DIM
HGRN_TT = 256
HGRN_C = 32
MERGE_TM = 512
MOE_TM = 384
PLAN_TM = 1024
DISPATCH_TM = 1024
COMBINE_TM = 512


def _params(sem):
    return pltpu.CompilerParams(dimension_semantics=sem, vmem_limit_bytes=VMEM_LIMIT)


def _split_dot(x, m, passes):
    acc = None
    r = x
    for p in range(passes):
        h = r.astype(BF16)
        term = jnp.dot(h, m, preferred_element_type=F32)
        acc = term if acc is None else acc + term
        if p + 1 < passes:
            r = r - h.astype(F32)
    return acc


def _inproj_kernel(x_ref, g_ref, w_ref, qkv_ref, rest_ref, h_scr, *, q_scale):
    j = pl.program_id(1)

    @pl.when(j == 0)
    def _():
        x = x_ref[...]
        var = jnp.mean(x * x, axis=-1, keepdims=True)
        h_scr[...] = (x * lax.rsqrt(var + EPS) * g_ref[...]).astype(BF16)

    acc = jnp.dot(h_scr[...], w_ref[...], preferred_element_type=F32)
    qkv_ref[...] = (acc * jnp.where(j == 0, q_scale, 1.0)).astype(BF16)
    rest_ref[...] = acc


def _inproj(x2, ln_g, w_in, sb_width):
    n, d = x2.shape
    cols = w_in.shape[1]
    tm, tn = min(INPROJ_TM, n), INPROJ_TN
    assert sb_width == tn, "q block must be exactly one column tile"
    nq = 3 * sb_width // tn
    nj = cols // tn
    kern = functools.partial(_inproj_kernel, q_scale=SB_HEAD_DIM ** -0.5)
    return pl.pallas_call(
        kern,
        grid=(n // tm, nj),
        in_specs=[
            pl.BlockSpec((tm, d), lambda i, j: (i, 0)),
            pl.BlockSpec((1, d), lambda i, j: (0, 0)),
            pl.BlockSpec((d, tn), lambda i, j: (0, j)),
        ],
        out_specs=[
            pl.BlockSpec((tm, tn), lambda i, j: (i, jnp.minimum(j, nq))),
            pl.BlockSpec((tm, tn), lambda i, j: (i, jnp.maximum(j - nq, 0))),
        ],
        out_shape=[
            jax.ShapeDtypeStruct((n, (nq + 1) * tn), BF16),
            jax.ShapeDtypeStruct((n, cols - nq * tn), F32),
        ],
        scratch_shapes=[pltpu.VMEM((tm, d), BF16)],
        compiler_params=_params(("parallel", "arbitrary")),
        name="inproj",
    )(x2, ln_g.reshape(1, d), w_in)


ATTN_STAGES = 3
ATTN_STREAMS = 2
MASK_BIAS = -1e30
ATTN_SKIP = 111.0


def _attn_kernel(q_ref, k_ref, v_ref, tri_ref, bias_ref, o_ref,
                 z0, z1, z2, i0, i1, i2, acc_ref, c_ref, *, t, nq):
    zbuf = (z0, z1, z2)
    ibuf = (i0, i1, i2)
    for r in zbuf + ibuf:
        r[...] = jnp.zeros_like(r)
    acc_ref[...] = jnp.zeros_like(acc_ref)
    c_ref[...] = jnp.zeros_like(c_ref)

    lane = lax.broadcasted_iota(jnp.int32, (t, LANES), 1)
    head0 = lane < SB_HEAD_DIM
    nt = (((1,), (1,)), ((), ()))

    def stage_a(qi, kj, slot):
        q = q_ref[pl.ds(pl.multiple_of(qi * t, t), t), :]
        zero = jnp.zeros_like(q)
        q2 = jnp.concatenate([jnp.where(head0, q, zero), jnp.where(head0, zero, q)], axis=0)
        k = k_ref[pl.ds(pl.multiple_of(kj * t, t), t), :]
        z = lax.dot_general(q2, k, nt, preferred_element_type=F32)
        bias = bias_ref[(qi == kj).astype(jnp.int32)]
        zbuf[slot][...] = z + jnp.concatenate([bias, bias], axis=0)

    def stage_b(slot):
        z = zbuf[slot][...]
        p = jnp.maximum(z, 0.0) + jnp.log(1.0 + jnp.exp2(jnp.abs(z) * (-LOG2E)))
        incl = jnp.dot(p.astype(BF16), tri_ref[...], preferred_element_type=F32)
        ibuf[slot][...] = incl
        return incl[:, 0:1]

    def stage_c(qi, kj, slot, stream):
        first = qi == kj
        incl = ibuf[slot][...]
        c = jnp.where(first, 0.0, c_ref[stream])
        a = jnp.exp2((zbuf[slot][...] - incl - c) * LOG2E)
        v = v_ref[pl.ds(pl.multiple_of(kj * t, t), t), :]
        pv = jnp.dot(a.astype(BF16), v, preferred_element_type=F32)
        acc = jnp.where(first, pv, acc_ref[stream] + pv)
        acc_ref[stream] = acc
        c_ref[stream] = c + incl[:, 0:1]
        o_ref[pl.ds(pl.multiple_of(qi * t, t), t), :] = (
            jnp.where(head0, acc[0:t], acc[t:2 * t]).astype(o_ref.dtype))

    def block(carry):
        qs, ks, qn, kn, skip_q, drained = carry
        qs, ks, qn, kn, skip_q = list(qs), list(ks), list(qn), list(kn), list(skip_q)
        all_done = qn[0] >= nq
        for p in range(1, ATTN_STREAMS):
            all_done = jnp.logical_and(all_done, qn[p] >= nq)
        drained = drained + all_done.astype(jnp.int32)
        for r in range(ATTN_STAGES * ATTN_STREAMS):
            sa, sc, sb = r % ATTN_STAGES, (r + 1) % ATTN_STAGES, (r + 2) % ATTN_STAGES
            pa, pb = r % ATTN_STREAMS, (r - 1) % ATTN_STREAMS
            stage_c(qs[sc], ks[sc], sc, pa)
            total_b = stage_b(sb)
            carry_b = jnp.where(qs[sb] == ks[sb], 0.0, c_ref[pb]) + total_b
            skip = skip_q[pa] == qn[pa]
            q_cur = jnp.where(skip, qn[pa] + ATTN_STREAMS, qn[pa])
            k_cur = jnp.where(skip, qn[pa] + ATTN_STREAMS, kn[pa])
            drain = q_cur >= nq
            qa = jnp.where(drain, 0, q_cur)
            ka = jnp.where(drain, 0, k_cur)
            stage_a(qa, ka, sa)
            last = k_cur == 0
            qn[pa] = jnp.where(jnp.logical_and(last, jnp.logical_not(drain)),
                               q_cur + ATTN_STREAMS, q_cur)
            kn[pa] = jnp.where(drain, k_cur, jnp.where(last, q_cur + ATTN_STREAMS, k_cur - 1))
            skip_q[pb] = jnp.where(jnp.min(carry_b) >= ATTN_SKIP, qs[sb], skip_q[pb])
            qs[sa], ks[sa] = qa, ka
        return tuple(qs), tuple(ks), tuple(qn), tuple(kn), tuple(skip_q), drained

    zero = jnp.int32(0)
    first_q = tuple(jnp.int32(p) for p in range(ATTN_STREAMS))
    init = ((zero,) * ATTN_STAGES, (zero,) * ATTN_STAGES, first_q, first_q,
            (jnp.int32(-1),) * ATTN_STREAMS, zero)
    lax.while_loop(lambda carry: carry[5] < 1, block, init)


def _attn(qkv, bsz, seq, tri, bias):
    n = bsz * seq
    t = min(ATTN_T, seq)
    pairs = SB_HEADS * SB_HEAD_DIM // LANES
    kern = functools.partial(_attn_kernel, t=t, nq=seq // t)
    return pl.pallas_call(
        kern,
        grid=(bsz, pairs),
        in_specs=[
            pl.BlockSpec((seq, LANES), lambda b, p: (b, p)),
            pl.BlockSpec((seq, LANES), lambda b, p: (b, pairs + p)),
            pl.BlockSpec((seq, LANES), lambda b, p: (b, 2 * pairs + p)),
            pl.BlockSpec((t, t), lambda b, p: (0, 0)),
            pl.BlockSpec((2, t, t), lambda b, p: (0, 0, 0)),
        ],
        out_specs=pl.BlockSpec((seq, LANES), lambda b, p: (b, p)),
        out_shape=jax.ShapeDtypeStruct((n, pairs * LANES), BF16),
        scratch_shapes=([pltpu.VMEM((2 * t, t), F32)] * (2 * ATTN_STAGES)
                        + [pltpu.VMEM((ATTN_STREAMS, 2 * t, LANES), F32),
                           pltpu.VMEM((ATTN_STREAMS, 2 * t, 1), F32)]),
        compiler_params=_params(("parallel", "parallel")),
        name="attn",
    )(qkv, qkv, qkv, tri, bias)


def _group_dot(x, bd):
    g = bd.shape[0]
    parts = [jnp.dot(x[:, i:i + g], bd, preferred_element_type=F32)
             for i in range(0, x.shape[1], g)]
    return jnp.concatenate(parts, axis=1)


def _hgrn_kernel(q_ref, f_ref, i_ref, g_ref, lbl_ref, ng_ref, tri_ref, bd_ref, o_ref,
                 st_scr, wpad, vpad, wsh, vsh, *, tt, c, layer):
    ti = pl.program_id(1)
    w = q_ref.shape[1]
    nch = tt // c

    @pl.when(ti == 0)
    def _():
        st_scr[...] = jnp.zeros_like(st_scr)

    lg = lbl_ref[...]
    e = jnp.exp(lg - jnp.max(lg, axis=0, keepdims=True))
    lb = jnp.sum(e[0:layer + 1], axis=0, keepdims=True) / jnp.sum(e, axis=0, keepdims=True)

    f = lb + (1.0 - lb) * jax.nn.sigmoid(f_ref[...])
    kk = 1.0 - f
    qv = q_ref[...]
    qs = qv * jax.nn.sigmoid(qv)
    v = i_ref[...]
    bd = bd_ref[...]
    cum = _split_dot_left(tri_ref[...], jnp.log(f) * LOG2E, 3)
    wk = jnp.log(kk) * LOG2E - cum

    pos = lax.broadcasted_iota(jnp.int32, (tt, 1), 0) % c
    wpad[0:SUBLANES, :] = jnp.zeros((SUBLANES, w), F32)
    vpad[0:SUBLANES, :] = jnp.zeros((SUBLANES, w), F32)
    wpad[SUBLANES:SUBLANES + tt, :] = wk
    vpad[SUBLANES:SUBLANES + tt, :] = v
    for b in range(SUBLANES):
        wb = wpad[SUBLANES - b:SUBLANES - b + tt, :]
        vb = vpad[SUBLANES - b:SUBLANES - b + tt, :]
        if b:
            wb = jnp.where(pos >= b, wb, -jnp.inf)
        wsh[b] = wb.reshape(nch, c, w)
        vsh[b] = vb.reshape(nch, c, w)

    qs3 = qs.reshape(nch, c, w)
    cum3 = cum.reshape(nch, c, w)
    acc = None
    for a in reversed(range(c // SUBLANES)):
        rows = c - SUBLANES * a
        qa = qs3[:, SUBLANES * a:, :].reshape(nch * rows, w)
        ca = cum3[:, SUBLANES * a:, :].reshape(nch * rows, w)
        acc_a = None
        for b in range(SUBLANES):
            wb = wsh[b, :, 0:rows, :].reshape(nch * rows, w)
            vb = vsh[b, :, 0:rows, :].reshape(nch * rows, w)
            dd = qa * jnp.exp2(ca + wb)
            term = _group_dot(dd.astype(BF16), bd) * vb
            acc_a = term if acc_a is None else acc_a + term
        acc_a = acc_a.reshape(nch, rows, w)
        if acc is not None:
            acc_a = acc_a + jnp.concatenate([jnp.zeros((nch, SUBLANES, w), F32), acc], axis=1)
        acc = acc_a
    acc = acc.reshape(tt, w)

    grp = st_scr.shape[1]
    bdmask = bd[0:grp, 0:grp] != 0
    outs = []
    for ci in range(nch):
        r0 = ci * c
        cum_c = cum[r0:r0 + c]
        last = cum_c[c - 1:c]
        qd = (qs[r0:r0 + c] * jnp.exp2(cum_c)).astype(BF16)
        kd = (kk[r0:r0 + c] * jnp.exp2(last - cum_c)).astype(BF16)
        vc = v[r0:r0 + c].astype(BF16)
        dec = jnp.exp2(last)
        o_parts = []
        for gi in range(w // grp):
            sl = slice(gi * grp, (gi + 1) * grp)
            st = st_scr[gi]
            o_parts.append(lax.dot_general(qd[:, sl], st.astype(BF16), (((1,), (1,)), ((), ())),
                                           preferred_element_type=F32))
            upd = lax.dot_general(vc[:, sl], kd[:, sl], (((0,), (0,)), ((), ())),
                                  preferred_element_type=F32)
            st_scr[gi] = st * dec[:, sl] + jnp.where(bdmask, upd, 0.0)
        outs.append(jnp.concatenate(o_parts, axis=1))
    o = acc + jnp.concatenate(outs, axis=0)

    o2 = o * o
    o2_hi = o2.astype(BF16)
    o2_lo = (o2 - o2_hi.astype(F32)).astype(BF16)
    ms = (_group_dot(o2_hi, bd) + _group_dot(o2_lo, bd)) * (1.0 / HG_DIM)
    gv = g_ref[...]
    o = o * lax.rsqrt(ms + EPS) * ng_ref[...] * (gv * jax.nn.sigmoid(gv))
    o_ref[...] = o.astype(o_ref.dtype)


def _split_dot_left(m, x, passes):
    acc = None
    r = x
    for p in range(passes):
        h = r.astype(BF16)
        term = jnp.dot(m, h, preferred_element_type=F32)
        acc = term if acc is None else acc + term
        if p + 1 < passes:
            r = r - h.astype(F32)
    return acc


def _hgrn(rest, lb_logits, norm_g, bsz, seq, layer, tri, bd):
    n = bsz * seq
    w = HG_HEADS * HG_DIM
    grp = bd.shape[0]
    tt = min(HGRN_TT, seq)
    c = min(HGRN_C, tt)
    nt = seq // tt
    kern = functools.partial(_hgrn_kernel, tt=tt, c=c, layer=layer)
    col = lambda j: pl.BlockSpec((tt, w), lambda b, i, j=j: (b * nt + i, j))
    const = lambda shape: pl.BlockSpec(shape, lambda b, i: (0, 0))
    return pl.pallas_call(
        kern,
        grid=(bsz, nt),
        in_specs=[col(0), col(1), col(2), col(3),
                  const(lb_logits.shape), const((1, w)), const((tt, tt)), const((grp, grp))],
        out_specs=pl.BlockSpec((tt, w), lambda b, i: (b * nt + i, 0)),
        out_shape=jax.ShapeDtypeStruct((n, w), BF16),
        scratch_shapes=([pltpu.VMEM((w // LANES, LANES, LANES), F32)]
                        + [pltpu.VMEM((SUBLANES + tt, w), F32)] * 2
                        + [pltpu.VMEM((SUBLANES, tt // c, c, w), F32)] * 2),
        compiler_params=_params(("parallel", "arbitrary")),
        name="hgrn",
    )(rest, rest, rest, rest, lb_logits, norm_g.reshape(1, w), tri, bd)


PAIR_A = (0, 0, 0, 1, 1, 3)
PAIR_B = (1, 2, 3, 3, 2, 2)
N_PAIRS = len(PAIR_A)
N_CLASSES = N_GROUPS * N_PAIRS
ROUTE_CLS, ROUTE_WA, ROUTE_WB = 0, 1, 2


def _pair_slots(pidx):
    a = jnp.where(pidx < 3.0, 0.0, jnp.where(pidx < 5.0, 1.0, 3.0))
    b = jnp.where(pidx == 0.0, 1.0, jnp.where(pidx == 1.0, 2.0, jnp.where(pidx < 4.0, 3.0, 2.0)))
    return a, b


def _route(lg):
    lane = lax.broadcasted_iota(jnp.int32, lg.shape, 1)
    neg = jnp.float32(-jnp.inf)
    big = jnp.int32(LANES)
    gmask = jnp.logical_and(lane >= N_EXPERTS, lane < N_EXPERTS + N_GROUPS)
    gl = jnp.where(gmask, lg, neg)
    gmax = jnp.max(gl, axis=1, keepdims=True)
    gidx = jnp.min(jnp.where(gl == gmax, lane, big), axis=1, keepdims=True) - N_EXPERTS
    w_grp = 1.0 / jnp.sum(jnp.where(gmask, jnp.exp(gl - gmax), 0.0), axis=1, keepdims=True)
    in_grp = jnp.logical_and(lane < N_EXPERTS, lane // EXPERTS_PER_GROUP == gidx)
    l1 = jnp.where(in_grp, lg, neg)
    v1 = jnp.max(l1, axis=1, keepdims=True)
    i1 = jnp.min(jnp.where(l1 == v1, lane, big), axis=1, keepdims=True)
    l2 = jnp.where(jnp.logical_and(in_grp, lane != i1), lg, neg)
    v2 = jnp.max(l2, axis=1, keepdims=True)
    i2 = jnp.min(jnp.where(l2 == v2, lane, big), axis=1, keepdims=True)
    e2 = jnp.exp(v2 - v1)
    p1 = 1.0 / (1.0 + e2)
    p2 = e2 * p1
    loc1 = (i1 - gidx * EXPERTS_PER_GROUP).astype(F32)
    loc2 = (i2 - gidx * EXPERTS_PER_GROUP).astype(F32)
    lo, hi = jnp.minimum(loc1, loc2), jnp.maximum(loc1, loc2)
    pidx = jnp.where(lo == 0.0, hi - 1.0, jnp.where(lo == 1.0, jnp.where(hi == 3.0, 3.0, 4.0), 5.0))
    cls = gidx.astype(F32) * N_PAIRS + pidx
    a_loc, _ = _pair_slots(pidx)
    first_is_a = loc1 == a_loc
    wa = jnp.where(first_is_a, p1, p2) * w_grp
    wb = jnp.where(first_is_a, p2, p1) * w_grp
    return jnp.where(lane == ROUTE_CLS, cls,
                     jnp.where(lane == ROUTE_WA, wa, jnp.where(lane == ROUTE_WB, wb, 0.0)))


def _store_token_tiles(ref, x):
    rows, d = x.shape
    assert d == SUBLANES * LANES
    for s in range(SUBLANES):
        ref[pl.ds(s, rows, stride=SUBLANES), :] = x[:, s * LANES:(s + 1) * LANES]


def _load_token_tiles(ref):
    rows = ref.shape[0] // SUBLANES
    return jnp.concatenate([ref[pl.ds(s, rows, stride=SUBLANES), :] for s in range(SUBLANES)],
                           axis=1)


def _merge_kernel(x_ref, ysb_ref, yhg_ref, gsb_ref, ghg_ref, wbs_ref, wbh_ref, wo_ref,
                  ln_ref, wr_ref, br_ref, x1_ref, t_ref, route_ref):
    a = jnp.dot(ysb_ref[...], wbs_ref[...], preferred_element_type=F32)
    b = jnp.dot(yhg_ref[...], wbh_ref[...], preferred_element_type=F32)
    merged = jax.nn.sigmoid(gsb_ref[...]) * a + jax.nn.sigmoid(ghg_ref[...]) * b
    x1 = x_ref[...] + jnp.dot(merged.astype(BF16), wo_ref[...], preferred_element_type=F32)
    x1_ref[...] = x1
    var = jnp.mean(x1 * x1, axis=-1, keepdims=True)
    t = x1 * lax.rsqrt(var + EPS) * ln_ref[...]
    _store_token_tiles(t_ref, t)
    t_hi = t.astype(BF16)
    t_lo = (t - t_hi.astype(F32)).astype(BF16)
    p_hi = jnp.dot(t_hi, wr_ref[...], preferred_element_type=F32)
    p_lo = jnp.dot(t_lo, wr_ref[...], preferred_element_type=F32)
    lg = (p_hi[:, :LANES] + p_hi[:, LANES:]) + (p_lo[:, :LANES] + p_lo[:, LANES:]) + br_ref[...]
    route_ref[...] = _route(lg)


def _merge(x2, y_sb, y_hg, rest, wbs, wbh, wo, ln_g, wr, br):
    n, d = x2.shape
    tm = min(MERGE_TM, n)
    w_sb, w_hg = y_sb.shape[1], y_hg.shape[1]
    gate_blk = (rest.shape[1] - 2 * d) // d
    row = lambda wdt, j=0: pl.BlockSpec((tm, wdt), lambda i, j=j: (i, j))
    const = lambda shape: pl.BlockSpec(shape, lambda i: (0, 0))
    return pl.pallas_call(
        _merge_kernel,
        grid=(n // tm,),
        in_specs=[row(d), row(w_sb), row(w_hg), row(d, gate_blk), row(d, gate_blk + 1),
                  const(wbs.shape), const(wbh.shape), const(wo.shape), const((1, d)),
                  const(wr.shape), const((1, LANES))],
        out_specs=[row(d), pl.BlockSpec((tm * SUBLANES, LANES), lambda i: (i, 0)), row(LANES)],
        out_shape=[jax.ShapeDtypeStruct((n, d), F32),
                   jax.ShapeDtypeStruct((n * SUBLANES, LANES), F32),
                   jax.ShapeDtypeStruct((n, LANES), F32)],
        compiler_params=_params(("parallel",)),
        name="merge",
    )(x2, y_sb, y_hg, rest, rest, wbs, wbh, wo, ln_g.reshape(1, d), wr, br)


TOKEN_ROWS_OUT = 2 * SUBLANES


def _plan_kernel(route_ref, ltri_ref, utri_ref, pos_ref, te_ref, meta_ref,
                 tot_scr, run_scr, *, tile_rows):
    ph = pl.program_id(0)
    i = pl.program_id(1)
    route = route_ref[...]
    tm = route.shape[0]
    lane = lax.broadcasted_iota(jnp.int32, (tm, LANES), 1)
    is_cls = lane.astype(F32) == route[:, ROUTE_CLS:ROUTE_CLS + 1]
    sel = jnp.where(is_cls, 1.0, 0.0)

    @pl.when(jnp.logical_and(ph == 0, i == 0))
    def _():
        tot_scr[...] = jnp.zeros_like(tot_scr)

    @pl.when(ph == 0)
    def _():
        tot_scr[...] += jnp.sum(sel, axis=0, keepdims=True)

    @pl.when(ph == 1)
    def _():
        tot = tot_scr[...]
        top = tot + (tile_rows - 1.0)
        tiles = jnp.floor(top * (1.0 / tile_rows))
        tiles = jnp.where(tiles * tile_rows > top, tiles - 1.0, tiles)
        tiles = jnp.where((tiles + 1.0) * tile_rows <= top, tiles + 1.0, tiles)
        first_tile = _split_dot(jnp.broadcast_to(tiles, (SUBLANES, LANES)), utri_ref[...], 3)[0:1]

        @pl.when(i == 0)
        def _():
            run_scr[...] = jnp.zeros_like(run_scr)
            lane1 = lax.broadcasted_iota(jnp.int32, (1, LANES), 1)
            is_c = lane1 < N_CLASSES
            end_tile = first_tile + tiles
            n_valid = jnp.sum(jnp.where(is_c, tiles, 0.0), axis=1, keepdims=True)
            rows = te_ref.shape[0]
            j = lax.broadcasted_iota(jnp.int32, (rows, LANES), 0).astype(F32)
            j = jnp.minimum(j, n_valid - 1.0)
            lane2 = lax.broadcasted_iota(jnp.int32, (rows, LANES), 1)
            done = jnp.where(jnp.logical_and(lane2 < N_CLASSES, end_tile <= j), 1.0, 0.0)
            tc = jnp.sum(done, axis=1, keepdims=True)
            tg = jnp.floor((tc + 0.5) * (1.0 / N_PAIRS))
            ta, tb = _pair_slots(tc - tg * N_PAIRS)
            te = jnp.where(lane2 == 0, tg * EXPERTS_PER_GROUP + ta, tg * EXPERTS_PER_GROUP + tb)
            te_ref[...] = te.astype(jnp.int32)
            last_tile = jnp.where(jnp.logical_and(is_c, tiles > 0), end_tile - 1.0, -1.0)
            meta = jnp.where(lane1 == N_CLASSES, n_valid, last_tile)
            meta_ref[...] = jnp.broadcast_to(meta, meta_ref.shape).astype(jnp.int32)

        rank = jnp.dot(ltri_ref[...], sel.astype(BF16), preferred_element_type=F32) + run_scr[...]
        dest = jnp.sum(jnp.where(is_cls, first_tile * tile_rows + rank, 0.0),
                       axis=1, keepdims=True)
        pos_ref[...] = jnp.where(lane == 0, dest, 0.0).astype(jnp.int32)
        run_scr[...] += jnp.sum(sel, axis=0, keepdims=True)


def _plan(route, n_tiles_max):
    n = route.shape[0]
    tm = min(PLAN_TM, n)
    r = np.arange(tm)
    ltri = jnp.asarray((r[None, :] < r[:, None]).astype(np.float32), dtype=BF16)
    e = np.arange(LANES)
    utri = jnp.asarray((e[:, None] < e[None, :]).astype(np.float32), dtype=BF16)
    te_rows = -(-n_tiles_max // SUBLANES) * SUBLANES
    kern = functools.partial(_plan_kernel, tile_rows=MOE_TM)
    return pl.pallas_call(
        kern,
        grid=(2, n // tm),
        in_specs=[pl.BlockSpec((tm, LANES), lambda p, i: (i, 0)),
                  pl.BlockSpec((tm, tm), lambda p, i: (0, 0)),
                  pl.BlockSpec((LANES, LANES), lambda p, i: (0, 0))],
        out_specs=[pl.BlockSpec((tm, LANES), lambda p, i: (i * p, 0)),
                   pl.BlockSpec((te_rows, LANES), lambda p, i: (0, 0)),
                   pl.BlockSpec((SUBLANES, LANES), lambda p, i: (0, 0))],
        out_shape=[jax.ShapeDtypeStruct((n, LANES), jnp.int32),
                   jax.ShapeDtypeStruct((te_rows, LANES), jnp.int32),
                   jax.ShapeDtypeStruct((SUBLANES, LANES), jnp.int32)],
        scratch_shapes=[pltpu.VMEM((1, LANES), F32), pltpu.VMEM((1, LANES), F32)],
        compiler_params=_params(("arbitrary", "arbitrary")),
        name="plan",
    )(route, ltri, utri)


DMA_UNROLL = 8


def _token_copy(src_ref, r, dst_ref, p, sem, rows):
    return pltpu.make_async_copy(src_ref.at[pl.ds(pl.multiple_of(r * rows, rows), rows)],
                                 dst_ref.at[pl.ds(pl.multiple_of(p * rows, rows), rows)], sem)


def _dispatch_kernel(pos_ref, meta_ref, t_ref, xs_ref, zero_scr, sem, *, tm, tile_rows):
    i = pl.program_id(0)
    tile_rows = tile_rows * SUBLANES
    n_tiles = xs_ref.shape[0] // tile_rows

    @pl.when(i == 0)
    def _():
        zero_scr[...] = jnp.zeros_like(zero_scr)
        n_valid = meta_ref[N_CLASSES]
        clears = [(meta_ref[c], meta_ref[c] >= 0) for c in range(N_CLASSES)]
        clears += [(n_tiles - 1 - k, n_tiles - 1 - k >= n_valid) for k in range(N_CLASSES)]

        def clear(tile):
            return pltpu.make_async_copy(
                zero_scr, xs_ref.at[pl.ds(tile * tile_rows, tile_rows)], sem)

        for tile, cond in clears:
            @pl.when(cond)
            def _():
                clear(tile).start()
        for tile, cond in clears:
            @pl.when(cond)
            def _():
                clear(tile).wait()

    base = i * tm

    def start(r2, carry):
        for k in range(2):
            r = 2 * r2 + k
            _token_copy(t_ref, r, xs_ref, pos_ref[base + r], sem, SUBLANES).start(priority=k)
        return carry

    def wait(r, carry):
        _token_copy(t_ref, r, xs_ref, pos_ref[base + r], sem, SUBLANES).wait()
        return carry

    lax.fori_loop(0, tm // 2, start, 0, unroll=DMA_UNROLL)
    lax.fori_loop(0, tm, wait, 0, unroll=DMA_UNROLL)


def _dispatch(pos, meta, t, n_rows):
    n = t.shape[0] // SUBLANES
    tm = min(DISPATCH_TM, n)
    kern = functools.partial(_dispatch_kernel, tm=tm, tile_rows=MOE_TM)
    grid_spec = pltpu.PrefetchScalarGridSpec(
        num_scalar_prefetch=2,
        grid=(n // tm,),
        in_specs=[pl.BlockSpec((tm * SUBLANES, LANES), lambda i, *_: (i, 0))],
        out_specs=pl.BlockSpec(memory_space=pl.ANY),
        scratch_shapes=[pltpu.VMEM((MOE_TM * SUBLANES, LANES), F32), pltpu.SemaphoreType.DMA(())],
    )
    return pl.pallas_call(
        kern,
        grid_spec=grid_spec,
        out_shape=jax.ShapeDtypeStruct((n_rows * SUBLANES, LANES), F32),
        compiler_params=_params(("arbitrary",)),
        name="dispatch",
    )(pos, meta, t)


def _expert_kernel(tea_ref, teb_ref, nv_ref, xs_ref, wga, wua, wda, wgb, wub, wdb, ys_ref):
    in_use = pl.program_id(0) < nv_ref[0]
    rows = xs_ref.shape[0] // SUBLANES

    @pl.when(in_use)
    def _():
        x = _load_token_tiles(xs_ref).astype(BF16)
        for slot, (wg, wu, wd) in enumerate(((wga, wua, wda), (wgb, wub, wdb))):
            hg = jnp.dot(x, wg[...].astype(BF16), preferred_element_type=F32)
            hu = jnp.dot(x, wu[...].astype(BF16), preferred_element_type=F32)
            a = (hg * jax.nn.sigmoid(hg) * hu).astype(BF16)
            y = jnp.dot(a, wd[...].astype(BF16), preferred_element_type=F32)
            for s in range(SUBLANES):
                ys_ref[pl.ds(slot * SUBLANES + s, rows, stride=TOKEN_ROWS_OUT), :] = (
                    y[:, s * LANES:(s + 1) * LANES])

    @pl.when(jnp.logical_not(in_use))
    def _():
        ys_ref[...] = jnp.zeros_like(ys_ref)


def _experts(tea, teb, nv, xs, wg, wu, wd):
    _, d, de = wg.shape
    assert d == SUBLANES * LANES
    n_tiles = xs.shape[0] // (MOE_TM * SUBLANES)
    wa = lambda shape: pl.BlockSpec(shape, lambda j, tea, teb, nv: (tea[j], 0, 0))
    wb = lambda shape: pl.BlockSpec(shape, lambda j, tea, teb, nv: (teb[j], 0, 0))
    grid_spec = pltpu.PrefetchScalarGridSpec(
        num_scalar_prefetch=3,
        grid=(n_tiles,),
        in_specs=[pl.BlockSpec((MOE_TM * SUBLANES, LANES), lambda j, *_: (j, 0)),
                  wa((None, d, de)), wa((None, d, de)), wa((None, de, d)),
                  wb((None, d, de)), wb((None, d, de)), wb((None, de, d))],
        out_specs=pl.BlockSpec((MOE_TM * TOKEN_ROWS_OUT, LANES), lambda j, *_: (j, 0)),
    )
    return pl.pallas_call(
        _expert_kernel,
        grid_spec=grid_spec,
        out_shape=jax.ShapeDtypeStruct((n_tiles * MOE_TM * TOKEN_ROWS_OUT, LANES), F32),
        compiler_params=_params(("arbitrary",)),
        name="experts",
    )(tea, teb, nv, xs, wg, wu, wd, wg, wu, wd)


def _combine_kernel(pos_ref, ys_ref, x1_ref, route_ref, fg_ref, o_ref, g, sem, *, tm):
    i = pl.program_id(0)
    slot = i % 2

    def gather(step, buf, wait):
        base = step * tm

        def body(r2, carry):
            for k in range(2):
                r = 2 * r2 + k
                c = _token_copy(ys_ref, pos_ref[base + r], g.at[buf], r, sem.at[buf],
                                TOKEN_ROWS_OUT)
                if wait:
                    c.wait()
                else:
                    c.start(priority=k)
            return carry

        lax.fori_loop(0, tm // 2, body, 0, unroll=DMA_UNROLL)

    @pl.when(i == 0)
    def _():
        gather(0, 0, wait=False)

    @pl.when(i + 1 < pl.num_programs(0))
    def _():
        gather(i + 1, 1 - slot, wait=False)

    gather(i, slot, wait=True)
    gs = g.at[slot]
    ya, yb = (jnp.concatenate([gs[pl.ds(off + s, tm, stride=TOKEN_ROWS_OUT), :]
                               for s in range(SUBLANES)], axis=1) for off in (0, SUBLANES))
    route = route_ref[...]
    x2 = x1_ref[...] + (route[:, ROUTE_WA:ROUTE_WA + 1] * ya + route[:, ROUTE_WB:ROUTE_WB + 1] * yb)
    var = jnp.mean(x2 * x2, axis=-1, keepdims=True)
    o_ref[...] = x2 * lax.rsqrt(var + EPS) * fg_ref[...]


def _combine(pos, ys, x1, route, final_g):
    n, d = x1.shape
    tm = min(COMBINE_TM, n)
    kern = functools.partial(_combine_kernel, tm=tm)
    grid_spec = pltpu.PrefetchScalarGridSpec(
        num_scalar_prefetch=1,
        grid=(n // tm,),
        in_specs=[pl.BlockSpec(memory_space=pl.ANY),
                  pl.BlockSpec((tm, d), lambda i, *_: (i, 0)),
                  pl.BlockSpec((tm, LANES), lambda i, *_: (i, 0)),
                  pl.BlockSpec((1, d), lambda i, *_: (0, 0))],
        out_specs=pl.BlockSpec((tm, d), lambda i, *_: (i, 0)),
        scratch_shapes=[pltpu.VMEM((2, tm * TOKEN_ROWS_OUT, LANES), F32),
                        pltpu.SemaphoreType.DMA((2,))],
    )
    return pl.pallas_call(
        kern,
        grid_spec=grid_spec,
        out_shape=jax.ShapeDtypeStruct((n, d), F32),
        compiler_params=_params(("arbitrary",)),
        name="combine",
    )(pos, ys, x1, route, final_g.reshape(1, d))


def _moe_sparse(t, route, wg, wu, wd, x1, final_g):
    n = x1.shape[0]
    n_tiles_max = (n + N_CLASSES * (MOE_TM - 1)) // MOE_TM
    pos, te, meta = _plan(route, n_tiles_max)
    pos = pos[:, 0]
    xs = _dispatch(pos, meta[0, :N_CLASSES + 1], t, n_tiles_max * MOE_TM)
    ys = _experts(te[:n_tiles_max, 0], te[:n_tiles_max, 1], meta[0, N_CLASSES:N_CLASSES + 1],
                  xs, wg, wu, wd)
    return _combine(pos, ys, x1, route, final_g)


def _suffix_ones(t):
    j = np.arange(t)[:, None]
    s = np.arange(t)[None, :]
    return jnp.asarray((j >= s).astype(np.float32), dtype=BF16)


def _causal_bias(t):
    row = np.arange(t)[:, None]
    col = np.arange(t)[None, :]
    diag = np.where(col < row, 0.0, MASK_BIAS).astype(np.float32)
    return jnp.asarray(np.stack([np.zeros_like(diag), diag]))


def _chunk_prefix_ones(tt, c):
    t = np.arange(tt)[:, None]
    j = np.arange(tt)[None, :]
    return jnp.asarray(((j <= t) & (t // c == j // c)).astype(np.float32), dtype=BF16)


def _block_diag_ones(w, blk):
    a = np.arange(w)
    return jnp.asarray((a[:, None] // blk == a[None, :] // blk).astype(np.float32), dtype=BF16)


def kernel(x, ln1_g, w_in, w_branch_sb, w_branch_hg, hg_norm_g, hg_lb_logits, w_out, ln2_g,
           w_router_group, b_router_group, w_router_expert, b_router_expert,
           w_exp_gate, w_exp_up, w_exp_down, final_g):
    bsz, seq, d = x.shape
    depth = w_in.shape[0]
    n = bsz * seq
    sb_width = SB_HEADS * SB_HEAD_DIM
    hg_width = HG_HEADS * HG_DIM

    tri_attn = _suffix_ones(min(ATTN_T, seq))
    bias_attn = _causal_bias(min(ATTN_T, seq))
    tt = min(HGRN_TT, seq)
    tri_hg = _chunk_prefix_ones(tt, min(HGRN_C, tt))
    bd = _block_diag_ones(min(MXU_DIM, hg_width), HG_DIM)

    x2 = x.reshape(n, d)
    for l in range(depth):
        qkv, rest = _inproj(x2, ln1_g[l], w_in[l].astype(BF16), sb_width)
        y_sb = _attn(qkv, bsz, seq, tri_attn, bias_attn)
        y_hg = _hgrn(rest, hg_lb_logits, hg_norm_g[l], bsz, seq, l, tri_hg, bd)

        pad = LANES - N_EXPERTS - N_GROUPS
        wr = jnp.concatenate([w_router_expert[l], w_router_group[l],
                              jnp.zeros((d, pad), F32)], axis=1)
        wr_hi = wr.astype(BF16)
        wr_lo = (wr - wr_hi.astype(F32)).astype(BF16)
        wr_split = jnp.concatenate([wr_hi, wr_lo], axis=1)
        br = jnp.concatenate([b_router_expert[l], b_router_group[l],
                              jnp.zeros((pad,), F32)]).reshape(1, LANES)

        last = l == depth - 1
        x1, t, route = _merge(x2, y_sb, y_hg, rest, w_branch_sb[l].astype(BF16),
                              w_branch_hg[l].astype(BF16), w_out[l].astype(BF16), ln2_g[l],
                              wr_split, br)
        assert last, "final rmsnorm is fused into the last layer's combine kernel"
        x2 = _moe_sparse(t, route, w_exp_gate[l], w_exp_up[l], w_exp_down[l], x1, final_g)
    return x2.reshape(bsz, seq, d)
```

```python
import functools

import jax
import jax.numpy as jnp
import numpy as np
from jax import lax
from jax.experimental import pallas as pl
from jax.experimental.pallas import tpu as pltpu

F32 = jnp.float32
BF16 = jnp.bfloat16

EPS = 1e-6
SB_HEADS = 8
SB_HEAD_DIM = 64
HG_HEADS = 8
HG_DIM = 64
N_GROUPS = 4
EXPERTS_PER_GROUP = 4
N_EXPERTS = N_GROUPS * EXPERTS_PER_GROUP

LANES = 128
SUBLANES = 8
MXU_DIM = 256
LOG2E = 1.4426950408889634
VMEM_LIMIT = 48 * 1024 * 1024

INPROJ_TM = 2048
INPROJ_TN = 512
ATTN_T = MXU_DIM
HGRN_TT = 256
HGRN_C = 32
MERGE_TM = 512
MOE_TM = 384
PLAN_TM = 1024
DISPATCH_TM = 1024
COMBINE_TM = 512


def _params(sem):
    return pltpu.CompilerParams(dimension_semantics=sem, vmem_limit_bytes=VMEM_LIMIT)


def _split_dot(x, m, passes):
    acc = None
    r = x
    for p in range(passes):
        h = r.astype(BF16)
        term = jnp.dot(h, m, preferred_element_type=F32)
        acc = term if acc is None else acc + term
        if p + 1 < passes:
            r = r - h.astype(F32)
    return acc


def _inproj_kernel(x_ref, g_ref, w_ref, qkv_ref, rest_ref, h_scr, *, q_scale):
    j = pl.program_id(1)

    @pl.when(j == 0)
    def _():
        x = x_ref[...]
        var = jnp.mean(x * x, axis=-1, keepdims=True)
        h_scr[...] = (x * lax.rsqrt(var + EPS) * g_ref[...]).astype(BF16)

    acc = jnp.dot(h_scr[...], w_ref[...], preferred_element_type=F32)
    qkv_ref[...] = (acc * jnp.where(j == 0, q_scale, 1.0)).astype(BF16)
    rest_ref[...] = acc


def _inproj(x2, ln_g, w_in, sb_width):
    n, d = x2.shape
    cols = w_in.shape[1]
    tm, tn = min(INPROJ_TM, n), INPROJ_TN
    assert sb_width == tn, "q block must be exactly one column tile"
    nq = 3 * sb_width // tn
    nj = cols // tn
    kern = functools.partial(_inproj_kernel, q_scale=SB_HEAD_DIM ** -0.5)
    return pl.pallas_call(
        kern,
        grid=(n // tm, nj),
        in_specs=[
            pl.BlockSpec((tm, d), lambda i, j: (i, 0)),
            pl.BlockSpec((1, d), lambda i, j: (0, 0)),
            pl.BlockSpec((d, tn), lambda i, j: (0, j)),
        ],
        out_specs=[
            pl.BlockSpec((tm, tn), lambda i, j: (i, jnp.minimum(j, nq))),
            pl.BlockSpec((tm, tn), lambda i, j: (i, jnp.maximum(j - nq, 0))),
        ],
        out_shape=[
            jax.ShapeDtypeStruct((n, (nq + 1) * tn), BF16),
            jax.ShapeDtypeStruct((n, cols - nq * tn), F32),
        ],
        scratch_shapes=[pltpu.VMEM((tm, d), BF16)],
        compiler_params=_params(("parallel", "arbitrary")),
        name="inproj",
    )(x2, ln_g.reshape(1, d), w_in)


ATTN_STAGES = 3
ATTN_STREAMS = 2
MASK_BIAS = -1e30
ATTN_SKIP = 111.0


def _attn_kernel(q_ref, k_ref, v_ref, tri_ref, bias_ref, o_ref,
                 z0, z1, z2, i0, i1, i2, acc_ref, c_ref, *, t, nq):
    zbuf = (z0, z1, z2)
    ibuf = (i0, i1, i2)
    for r in zbuf + ibuf:
        r[...] = jnp.zeros_like(r)
    acc_ref[...] = jnp.zeros_like(acc_ref)
    c_ref[...] = jnp.zeros_like(c_ref)

    lane = lax.broadcasted_iota(jnp.int32, (t, LANES), 1)
    head0 = lane < SB_HEAD_DIM
    nt = (((1,), (1,)), ((), ()))

    def stage_a(qi, kj, slot):
        q = q_ref[pl.ds(pl.multiple_of(qi * t, t), t), :]
        zero = jnp.zeros_like(q)
        q2 = jnp.concatenate([jnp.where(head0, q, zero), jnp.where(head0, zero, q)], axis=0)
        k = k_ref[pl.ds(pl.multiple_of(kj * t, t), t), :]
        z = lax.dot_general(q2, k, nt, preferred_element_type=F32)
        bias = bias_ref[(qi == kj).astype(jnp.int32)]
        zbuf[slot][...] = z + jnp.concatenate([bias, bias], axis=0)

    def stage_b(slot):
        z = zbuf[slot][...]
        p = jnp.maximum(z, 0.0) + jnp.log(1.0 + jnp.exp2(jnp.abs(z) * (-LOG2E)))
        incl = jnp.dot(p.astype(BF16), tri_ref[...], preferred_element_type=F32)
        ibuf[slot][...] = incl
        return incl[:, 0:1]

    def stage_c(qi, kj, slot, stream):
        first = qi == kj
        incl = ibuf[slot][...]
        c = jnp.where(first, 0.0, c_ref[stream])
        a = jnp.exp2((zbuf[slot][...] - incl - c) * LOG2E)
        v = v_ref[pl.ds(pl.multiple_of(kj * t, t), t), :]
        pv = jnp.dot(a.astype(BF16), v, preferred_element_type=F32)
        acc = jnp.where(first, pv, acc_ref[stream] + pv)
        acc_ref[stream] = acc
        c_ref[stream] = c + incl[:, 0:1]
        o_ref[pl.ds(pl.multiple_of(qi * t, t), t), :] = (
            jnp.where(head0, acc[0:t], acc[t:2 * t]).astype(o_ref.dtype))

    def block(carry):
        qs, ks, qn, kn, skip_q, drained = carry
        qs, ks, qn, kn, skip_q = list(qs), list(ks), list(qn), list(kn), list(skip_q)
        all_done = qn[0] >= nq
        for p in range(1, ATTN_STREAMS):
            all_done = jnp.logical_and(all_done, qn[p] >= nq)
        drained = drained + all_done.astype(jnp.int32)
        for r in range(ATTN_STAGES * ATTN_STREAMS):
            sa, sc, sb = r % ATTN_STAGES, (r + 1) % ATTN_STAGES, (r + 2) % ATTN_STAGES
            pa, pb = r % ATTN_STREAMS, (r - 1) % ATTN_STREAMS
            stage_c(qs[sc], ks[sc], sc, pa)
            total_b = stage_b(sb)
            carry_b = jnp.where(qs[sb] == ks[sb], 0.0, c_ref[pb]) + total_b
            skip = skip_q[pa] == qn[pa]
            q_cur = jnp.where(skip, qn[pa] + ATTN_STREAMS, qn[pa])
            k_cur = jnp.where(skip, qn[pa] + ATTN_STREAMS, kn[pa])
            drain = q_cur >= nq
            qa = jnp.where(drain, 0, q_cur)
            ka = jnp.where(drain, 0, k_cur)
            stage_a(qa, ka, sa)
            last = k_cur == 0
            qn[pa] = jnp.where(jnp.logical_and(last, jnp.logical_not(drain)),
                               q_cur + ATTN_STREAMS, q_cur)
            kn[pa] = jnp.where(drain, k_cur, jnp.where(last, q_cur + ATTN_STREAMS, k_cur - 1))
            skip_q[pb] = jnp.where(jnp.min(carry_b) >= ATTN_SKIP, qs[sb], skip_q[pb])
            qs[sa], ks[sa] = qa, ka
        return tuple(qs), tuple(ks), tuple(qn), tuple(kn), tuple(skip_q), drained

    zero = jnp.int32(0)
    first_q = tuple(jnp.int32(p) for p in range(ATTN_STREAMS))
    init = ((zero,) * ATTN_STAGES, (zero,) * ATTN_STAGES, first_q, first_q,
            (jnp.int32(-1),) * ATTN_STREAMS, zero)
    lax.while_loop(lambda carry: carry[5] < 1, block, init)


def _attn(qkv, bsz, seq, tri, bias):
    n = bsz * seq
    t = min(ATTN_T, seq)
    pairs = SB_HEADS * SB_HEAD_DIM // LANES
    kern = functools.partial(_attn_kernel, t=t, nq=seq // t)
    return pl.pallas_call(
        kern,
        grid=(bsz, pairs),
        in_specs=[
            pl.BlockSpec((seq, LANES), lambda b, p: (b, p)),
            pl.BlockSpec((seq, LANES), lambda b, p: (b, pairs + p)),
            pl.BlockSpec((seq, LANES), lambda b, p: (b, 2 * pairs + p)),
            pl.BlockSpec((t, t), lambda b, p: (0, 0)),
            pl.BlockSpec((2, t, t), lambda b, p: (0, 0, 0)),
        ],
        out_specs=pl.BlockSpec((seq, LANES), lambda b, p: (b, p)),
        out_shape=jax.ShapeDtypeStruct((n, pairs * LANES), BF16),
        scratch_shapes=([pltpu.VMEM((2 * t, t), F32)] * (2 * ATTN_STAGES)
                        + [pltpu.VMEM((ATTN_STREAMS, 2 * t, LANES), F32),
                           pltpu.VMEM((ATTN_STREAMS, 2 * t, 1), F32)]),
        compiler_params=_params(("parallel", "parallel")),
        name="attn",
    )(qkv, qkv, qkv, tri, bias)


def _group_dot(x, bd):
    g = bd.shape[0]
    parts = [jnp.dot(x[:, i:i + g], bd, preferred_element_type=F32)
             for i in range(0, x.shape[1], g)]
    return jnp.concatenate(parts, axis=1)


def _hgrn_kernel(q_ref, f_ref, i_ref, g_ref, lbl_ref, ng_ref, tri_ref, bd_ref, o_ref,
                 st_scr, wpad, vpad, wsh, vsh, *, tt, c, layer):
    ti = pl.program_id(1)
    w = q_ref.shape[1]
    nch = tt // c

    @pl.when(ti == 0)
    def _():
        st_scr[...] = jnp.zeros_like(st_scr)

    lg = lbl_ref[...]
    e = jnp.exp(lg - jnp.max(lg, axis=0, keepdims=True))
    lb = jnp.sum(e[0:layer + 1], axis=0, keepdims=True) / jnp.sum(e, axis=0, keepdims=True)

    f = lb + (1.0 - lb) * jax.nn.sigmoid(f_ref[...])
    kk = 1.0 - f
    qv = q_ref[...]
    qs = qv * jax.nn.sigmoid(qv)
    v = i_ref[...]
    bd = bd_ref[...]
    cum = _split_dot_left(tri_ref[...], jnp.log(f) * LOG2E, 3)
    wk = jnp.log(kk) * LOG2E - cum

    pos = lax.broadcasted_iota(jnp.int32, (tt, 1), 0) % c
    wpad[0:SUBLANES, :] = jnp.zeros((SUBLANES, w), F32)
    vpad[0:SUBLANES, :] = jnp.zeros((SUBLANES, w), F32)
    wpad[SUBLANES:SUBLANES + tt, :] = wk
    vpad[SUBLANES:SUBLANES + tt, :] = v
    for b in range(SUBLANES):
        wb = wpad[SUBLANES - b:SUBLANES - b + tt, :]
        vb = vpad[SUBLANES - b:SUBLANES - b + tt, :]
        if b:
            wb = jnp.where(pos >= b, wb, -jnp.inf)
        wsh[b] = wb.reshape(nch, c, w)
        vsh[b] = vb.reshape(nch, c, w)

    qs3 = qs.reshape(nch, c, w)
    cum3 = cum.reshape(nch, c, w)
    acc = None
    for a in reversed(range(c // SUBLANES)):
        rows = c - SUBLANES * a
        qa = qs3[:, SUBLANES * a:, :].reshape(nch * rows, w)
        ca = cum3[:, SUBLANES * a:, :].reshape(nch * rows, w)
        acc_a = None
        for b in range(SUBLANES):
            wb = wsh[b, :, 0:rows, :].reshape(nch * rows, w)
            vb = vsh[b, :, 0:rows, :].reshape(nch * rows, w)
            dd = qa * jnp.exp2(ca + wb)
            term = _group_dot(dd.astype(BF16), bd) * vb
            acc_a = term if acc_a is None else acc_a + term
        acc_a = acc_a.reshape(nch, rows, w)
        if acc is not None:
            acc_a = acc_a + jnp.concatenate([jnp.zeros((nch, SUBLANES, w), F32), acc], axis=1)
        acc = acc_a
    acc = acc.reshape(tt, w)

    grp = st_scr.shape[1]
    bdmask = bd[0:grp, 0:grp] != 0
    outs = []
    for ci in range(nch):
        r0 = ci * c
        cum_c = cum[r0:r0 + c]
        last = cum_c[c - 1:c]
        qd = (qs[r0:r0 + c] * jnp.exp2(cum_c)).astype(BF16)
        kd = (kk[r0:r0 + c] * jnp.exp2(last - cum_c)).astype(BF16)
        vc = v[r0:r0 + c].astype(BF16)
        dec = jnp.exp2(last)
        o_parts = []
        for gi in range(w // grp):
            sl = slice(gi * grp, (gi + 1) * grp)
            st = st_scr[gi]
            o_parts.append(lax.dot_general(qd[:, sl], st.astype(BF16), (((1,), (1,)), ((), ())),
                                           preferred_element_type=F32))
            upd = lax.dot_general(vc[:, sl], kd[:, sl], (((0,), (0,)), ((), ())),
                                  preferred_element_type=F32)
            st_scr[gi] = st * dec[:, sl] + jnp.where(bdmask, upd, 0.0)
        outs.append(jnp.concatenate(o_parts, axis=1))
    o = acc + jnp.concatenate(outs, axis=0)

    o2 = o * o
    o2_hi = o2.astype(BF16)
    o2_lo = (o2 - o2_hi.astype(F32)).astype(BF16)
    ms = (_group_dot(o2_hi, bd) + _group_dot(o2_lo, bd)) * (1.0 / HG_DIM)
    gv = g_ref[...]
    o = o * lax.rsqrt(ms + EPS) * ng_ref[...] * (gv * jax.nn.sigmoid(gv))
    o_ref[...] = o.astype(o_ref.dtype)


def _split_dot_left(m, x, passes):
    acc = None
    r = x
    for p in range(passes):
        h = r.astype(BF16)
        term = jnp.dot(m, h, preferred_element_type=F32)
        acc = term if acc is None else acc + term
        if p + 1 < passes:
            r = r - h.astype(F32)
    return acc


def _hgrn(rest, lb_logits, norm_g, bsz, seq, layer, tri, bd):
    n = bsz * seq
    w = HG_HEADS * HG_DIM
    grp = bd.shape[0]
    tt = min(HGRN_TT, seq)
    c = min(HGRN_C, tt)
    nt = seq // tt
    kern = functools.partial(_hgrn_kernel, tt=tt, c=c, layer=layer)
    col = lambda j: pl.BlockSpec((tt, w), lambda b, i, j=j: (b * nt + i, j))
    const = lambda shape: pl.BlockSpec(shape, lambda b, i: (0, 0))
    return pl.pallas_call(
        kern,
        grid=(bsz, nt),
        in_specs=[col(0), col(1), col(2), col(3),
                  const(lb_logits.shape), const((1, w)), const((tt, tt)), const((grp, grp))],
        out_specs=pl.BlockSpec((tt, w), lambda b, i: (b * nt + i, 0)),
        out_shape=jax.ShapeDtypeStruct((n, w), BF16),
        scratch_shapes=([pltpu.VMEM((w // LANES, LANES, LANES), F32)]
                        + [pltpu.VMEM((SUBLANES + tt, w), F32)] * 2
                        + [pltpu.VMEM((SUBLANES, tt // c, c, w), F32)] * 2),
        compiler_params=_params(("parallel", "arbitrary")),
        name="hgrn",
    )(rest, rest, rest, rest, lb_logits, norm_g.reshape(1, w), tri, bd)


PAIR_A = (0, 0, 0, 1, 1, 3)
PAIR_B = (1, 2, 3, 3, 2, 2)
N_PAIRS = len(PAIR_A)
N_CLASSES = N_GROUPS * N_PAIRS
ROUTE_CLS, ROUTE_WA, ROUTE_WB = 0, 1, 2


def _pair_slots(pidx):
    a = jnp.where(pidx < 3.0, 0.0, jnp.where(pidx < 5.0, 1.0, 3.0))
    b = jnp.where(pidx == 0.0, 1.0, jnp.where(pidx == 1.0, 2.0, jnp.where(pidx < 4.0, 3.0, 2.0)))
    return a, b


def _route(lg):
    lane = lax.broadcasted_iota(jnp.int32, lg.shape, 1)
    neg = jnp.float32(-jnp.inf)
    big = jnp.int32(LANES)
    gmask = jnp.logical_and(lane >= N_EXPERTS, lane < N_EXPERTS + N_GROUPS)
    gl = jnp.where(gmask, lg, neg)
    gmax = jnp.max(gl, axis=1, keepdims=True)
    gidx = jnp.min(jnp.where(gl == gmax, lane, big), axis=1, keepdims=True) - N_EXPERTS
    w_grp = 1.0 / jnp.sum(jnp.where(gmask, jnp.exp(gl - gmax), 0.0), axis=1, keepdims=True)
    in_grp = jnp.logical_and(lane < N_EXPERTS, lane // EXPERTS_PER_GROUP == gidx)
    l1 = jnp.where(in_grp, lg, neg)
    v1 = jnp.max(l1, axis=1, keepdims=True)
    i1 = jnp.min(jnp.where(l1 == v1, lane, big), axis=1, keepdims=True)
    l2 = jnp.where(jnp.logical_and(in_grp, lane != i1), lg, neg)
    v2 = jnp.max(l2, axis=1, keepdims=True)
    i2 = jnp.min(jnp.where(l2 == v2, lane, big), axis=1, keepdims=True)
    e2 = jnp.exp(v2 - v1)
    p1 = 1.0 / (1.0 + e2)
    p2 = e2 * p1
    loc1 = (i1 - gidx * EXPERTS_PER_GROUP).astype(F32)
    loc2 = (i2 - gidx * EXPERTS_PER_GROUP).astype(F32)
    lo, hi = jnp.minimum(loc1, loc2), jnp.maximum(loc1, loc2)
    pidx = jnp.where(lo == 0.0, hi - 1.0, jnp.where(lo == 1.0, jnp.where(hi == 3.0, 3.0, 4.0), 5.0))
    cls = gidx.astype(F32) * N_PAIRS + pidx
    a_loc, _ = _pair_slots(pidx)
    first_is_a = loc1 == a_loc
    wa = jnp.where(first_is_a, p1, p2) * w_grp
    wb = jnp.where(first_is_a, p2, p1) * w_grp
    return jnp.where(lane == ROUTE_CLS, cls,
                     jnp.where(lane == ROUTE_WA, wa, jnp.where(lane == ROUTE_WB, wb, 0.0)))


def _store_token_tiles(ref, x):
    rows, d = x.shape
    assert d == SUBLANES * LANES
    for s in range(SUBLANES):
        ref[pl.ds(s, rows, stride=SUBLANES), :] = x[:, s * LANES:(s + 1) * LANES]


def _load_token_tiles(ref):
    rows = ref.shape[0] // SUBLANES
    return jnp.concatenate([ref[pl.ds(s, rows, stride=SUBLANES), :] for s in range(SUBLANES)],
                           axis=1)


def _merge_kernel(x_ref, ysb_ref, yhg_ref, gsb_ref, ghg_ref, wbs_ref, wbh_ref, wo_ref,
                  ln_ref, wr_ref, br_ref, x1_ref, t_ref, route_ref):
    a = jnp.dot(ysb_ref[...], wbs_ref[...], preferred_element_type=F32)
    b = jnp.dot(yhg_ref[...], wbh_ref[...], preferred_element_type=F32)
    merged = jax.nn.sigmoid(gsb_ref[...]) * a + jax.nn.sigmoid(ghg_ref[...]) * b
    x1 = x_ref[...] + jnp.dot(merged.astype(BF16), wo_ref[...], preferred_element_type=F32)
    x1_ref[...] = x1
    var = jnp.mean(x1 * x1, axis=-1, keepdims=True)
    t = x1 * lax.rsqrt(var + EPS) * ln_ref[...]
    _store_token_tiles(t_ref, t)
    t_hi = t.astype(BF16)
    t_lo = (t - t_hi.astype(F32)).astype(BF16)
    p_hi = jnp.dot(t_hi, wr_ref[...], preferred_element_type=F32)
    p_lo = jnp.dot(t_lo, wr_ref[...], preferred_element_type=F32)
    lg = (p_hi[:, :LANES] + p_hi[:, LANES:]) + (p_lo[:, :LANES] + p_lo[:, LANES:]) + br_ref[...]
    route_ref[...] = _route(lg)


def _merge(x2, y_sb, y_hg, rest, wbs, wbh, wo, ln_g, wr, br):
    n, d = x2.shape
    tm = min(MERGE_TM, n)
    w_sb, w_hg = y_sb.shape[1], y_hg.shape[1]
    gate_blk = (rest.shape[1] - 2 * d) // d
    row = lambda wdt, j=0: pl.BlockSpec((tm, wdt), lambda i, j=j: (i, j))
    const = lambda shape: pl.BlockSpec(shape, lambda i: (0, 0))
    return pl.pallas_call(
        _merge_kernel,
        grid=(n // tm,),
        in_specs=[row(d), row(w_sb), row(w_hg), row(d, gate_blk), row(d, gate_blk + 1),
                  const(wbs.shape), const(wbh.shape), const(wo.shape), const((1, d)),
                  const(wr.shape), const((1, LANES))],
        out_specs=[row(d), pl.BlockSpec((tm * SUBLANES, LANES), lambda i: (i, 0)), row(LANES)],
        out_shape=[jax.ShapeDtypeStruct((n, d), F32),
                   jax.ShapeDtypeStruct((n * SUBLANES, LANES), F32),
                   jax.ShapeDtypeStruct((n, LANES), F32)],
        compiler_params=_params(("parallel",)),
        name="merge",
    )(x2, y_sb, y_hg, rest, rest, wbs, wbh, wo, ln_g.reshape(1, d), wr, br)


TOKEN_ROWS_OUT = 2 * SUBLANES


def _plan_kernel(route_ref, ltri_ref, utri_ref, pos_ref, te_ref, meta_ref,
                 tot_scr, run_scr, *, tile_rows):
    ph = pl.program_id(0)
    i = pl.program_id(1)
    route = route_ref[...]
    tm = route.shape[0]
    lane = lax.broadcasted_iota(jnp.int32, (tm, LANES), 1)
    is_cls = lane.astype(F32) == route[:, ROUTE_CLS:ROUTE_CLS + 1]
    sel = jnp.where(is_cls, 1.0, 0.0)

    @pl.when(jnp.logical_and(ph == 0, i == 0))
    def _():
        tot_scr[...] = jnp.zeros_like(tot_scr)

    @pl.when(ph == 0)
    def _():
        tot_scr[...] += jnp.sum(sel, axis=0, keepdims=True)

    @pl.when(ph == 1)
    def _():
        tot = tot_scr[...]
        top = tot + (tile_rows - 1.0)
        tiles = jnp.floor(top * (1.0 / tile_rows))
        tiles = jnp.where(tiles * tile_rows > top, tiles - 1.0, tiles)
        tiles = jnp.where((tiles + 1.0) * tile_rows <= top, tiles + 1.0, tiles)
        first_tile = _split_dot(jnp.broadcast_to(tiles, (SUBLANES, LANES)), utri_ref[...], 3)[0:1]

        @pl.when(i == 0)
        def _():
            run_scr[...] = jnp.zeros_like(run_scr)
            lane1 = lax.broadcasted_iota(jnp.int32, (1, LANES), 1)
            is_c = lane1 < N_CLASSES
            end_tile = first_tile + tiles
            n_valid = jnp.sum(jnp.where(is_c, tiles, 0.0), axis=1, keepdims=True)
            rows = te_ref.shape[0]
            j = lax.broadcasted_iota(jnp.int32, (rows, LANES), 0).astype(F32)
            j = jnp.minimum(j, n_valid - 1.0)
            lane2 = lax.broadcasted_iota(jnp.int32, (rows, LANES), 1)
            done = jnp.where(jnp.logical_and(lane2 < N_CLASSES, end_tile <= j), 1.0, 0.0)
            tc = jnp.sum(done, axis=1, keepdims=True)
            tg = jnp.floor((tc + 0.5) * (1.0 / N_PAIRS))
            ta, tb = _pair_slots(tc - tg * N_PAIRS)
            te = jnp.where(lane2 == 0, tg * EXPERTS_PER_GROUP + ta, tg * EXPERTS_PER_GROUP + tb)
            te_ref[...] = te.astype(jnp.int32)
            last_tile = jnp.where(jnp.logical_and(is_c, tiles > 0), end_tile - 1.0, -1.0)
            meta = jnp.where(lane1 == N_CLASSES, n_valid, last_tile)
            meta_ref[...] = jnp.broadcast_to(meta, meta_ref.shape).astype(jnp.int32)

        rank = jnp.dot(ltri_ref[...], sel.astype(BF16), preferred_element_type=F32) + run_scr[...]
        dest = jnp.sum(jnp.where(is_cls, first_tile * tile_rows + rank, 0.0),
                       axis=1, keepdims=True)
        pos_ref[...] = jnp.where(lane == 0, dest, 0.0).astype(jnp.int32)
        run_scr[...] += jnp.sum(sel, axis=0, keepdims=True)


def _plan(route, n_tiles_max):
    n = route.shape[0]
    tm = min(PLAN_TM, n)
    r = np.arange(tm)
    ltri = jnp.asarray((r[None, :] < r[:, None]).astype(np.float32), dtype=BF16)
    e = np.arange(LANES)
    utri = jnp.asarray((e[:, None] < e[None, :]).astype(np.float32), dtype=BF16)
    te_rows = -(-n_tiles_max // SUBLANES) * SUBLANES
    kern = functools.partial(_plan_kernel, tile_rows=MOE_TM)
    return pl.pallas_call(
        kern,
        grid=(2, n // tm),
        in_specs=[pl.BlockSpec((tm, LANES), lambda p, i: (i, 0)),
                  pl.BlockSpec((tm, tm), lambda p, i: (0, 0)),
                  pl.BlockSpec((LANES, LANES), lambda p, i: (0, 0))],
        out_specs=[pl.BlockSpec((tm, LANES), lambda p, i: (i * p, 0)),
                   pl.BlockSpec((te_rows, LANES), lambda p, i: (0, 0)),
                   pl.BlockSpec((SUBLANES, LANES), lambda p, i: (0, 0))],
        out_shape=[jax.ShapeDtypeStruct((n, LANES), jnp.int32),
                   jax.ShapeDtypeStruct((te_rows, LANES), jnp.int32),
                   jax.ShapeDtypeStruct((SUBLANES, LANES), jnp.int32)],
        scratch_shapes=[pltpu.VMEM((1, LANES), F32), pltpu.VMEM((1, LANES), F32)],
        compiler_params=_params(("arbitrary", "arbitrary")),
        name="plan",
    )(route, ltri, utri)


DMA_UNROLL = 8


def _token_copy(src_ref, r, dst_ref, p, sem, rows):
    return pltpu.make_async_copy(src_ref.at[pl.ds(pl.multiple_of(r * rows, rows), rows)],
                                 dst_ref.at[pl.ds(pl.multiple_of(p * rows, rows), rows)], sem)


def _dispatch_kernel(pos_ref, meta_ref, t_ref, xs_ref, zero_scr, sem, *, tm, tile_rows):
    i = pl.program_id(0)
    tile_rows = tile_rows * SUBLANES
    n_tiles = xs_ref.shape[0] // tile_rows

    @pl.when(i == 0)
    def _():
        zero_scr[...] = jnp.zeros_like(zero_scr)
        n_valid = meta_ref[N_CLASSES]
        clears = [(meta_ref[c], meta_ref[c] >= 0) for c in range(N_CLASSES)]
        clears += [(n_tiles - 1 - k, n_tiles - 1 - k >= n_valid) for k in range(N_CLASSES)]

        def clear(tile):
            return pltpu.make_async_copy(
                zero_scr, xs_ref.at[pl.ds(tile * tile_rows, tile_rows)], sem)

        for tile, cond in clears:
            @pl.when(cond)
            def _():
                clear(tile).start()
        for tile, cond in clears:
            @pl.when(cond)
            def _():
                clear(tile).wait()

    base = i * tm

    def start(r2, carry):
        for k in range(2):
            r = 2 * r2 + k
            _token_copy(t_ref, r, xs_ref, pos_ref[base + r], sem, SUBLANES).start(priority=k)
        return carry

    def wait(r, carry):
        _token_copy(t_ref, r, xs_ref, pos_ref[base + r], sem, SUBLANES).wait()
        return carry

    lax.fori_loop(0, tm // 2, start, 0, unroll=DMA_UNROLL)
    lax.fori_loop(0, tm, wait, 0, unroll=DMA_UNROLL)


def _dispatch(pos, meta, t, n_rows):
    n = t.shape[0] // SUBLANES
    tm = min(DISPATCH_TM, n)
    kern = functools.partial(_dispatch_kernel, tm=tm, tile_rows=MOE_TM)
    grid_spec = pltpu.PrefetchScalarGridSpec(
        num_scalar_prefetch=2,
        grid=(n // tm,),
        in_specs=[pl.BlockSpec((tm * SUBLANES, LANES), lambda i, *_: (i, 0))],
        out_specs=pl.BlockSpec(memory_space=pl.ANY),
        scratch_shapes=[pltpu.VMEM((MOE_TM * SUBLANES, LANES), F32), pltpu.SemaphoreType.DMA(())],
    )
    return pl.pallas_call(
        kern,
        grid_spec=grid_spec,
        out_shape=jax.ShapeDtypeStruct((n_rows * SUBLANES, LANES), F32),
        compiler_params=_params(("arbitrary",)),
        name="dispatch",
    )(pos, meta, t)


def _expert_kernel(tea_ref, teb_ref, nv_ref, xs_ref, wga, wua, wda, wgb, wub, wdb, ys_ref,
                   bga, bua, bda, bgb, bub, bdb):
    j = pl.program_id(0)
    in_use = j < nv_ref[0]
    rows = xs_ref.shape[0] // SUBLANES
    prev = jnp.maximum(j - 1, 0)

    for te_ref, f32_refs, bf16_refs in ((tea_ref, (wga, wua, wda), (bga, bua, bda)),
                                        (teb_ref, (wgb, wub, wdb), (bgb, bub, bdb))):
        @pl.when(jnp.logical_or(j == 0, te_ref[j] != te_ref[prev]))
        def _():
            for src, dst in zip(f32_refs, bf16_refs):
                dst[...] = src[...].astype(BF16)

    @pl.when(in_use)
    def _():
        x = _load_token_tiles(xs_ref).astype(BF16)
        for slot, (wg, wu, wd) in enumerate(((bga, bua, bda), (bgb, bub, bdb))):
            hg = jnp.dot(x, wg[...], preferred_element_type=F32)
            hu = jnp.dot(x, wu[...], preferred_element_type=F32)
            a = (hg * jax.nn.sigmoid(hg) * hu).astype(BF16)
            y = jnp.dot(a, wd[...], preferred_element_type=F32)
            for s in range(SUBLANES):
                ys_ref[pl.ds(slot * SUBLANES + s, rows, stride=TOKEN_ROWS_OUT), :] = (
                    y[:, s * LANES:(s + 1) * LANES])

    @pl.when(jnp.logical_not(in_use))
    def _():
        ys_ref[...] = jnp.zeros_like(ys_ref)


def _experts(tea, teb, nv, xs, wg, wu, wd):
    _, d, de = wg.shape
    assert d == SUBLANES * LANES
    n_tiles = xs.shape[0] // (MOE_TM * SUBLANES)
    wa = lambda shape: pl.BlockSpec(shape, lambda j, tea, teb, nv: (tea[j], 0, 0))
    wb = lambda shape: pl.BlockSpec(shape, lambda j, tea, teb, nv: (teb[j], 0, 0))
    grid_spec = pltpu.PrefetchScalarGridSpec(
        num_scalar_prefetch=3,
        grid=(n_tiles,),
        in_specs=[pl.BlockSpec((MOE_TM * SUBLANES, LANES), lambda j, *_: (j, 0)),
                  wa((None, d, de)), wa((None, d, de)), wa((None, de, d)),
                  wb((None, d, de)), wb((None, d, de)), wb((None, de, d))],
        out_specs=pl.BlockSpec((MOE_TM * TOKEN_ROWS_OUT, LANES), lambda j, *_: (j, 0)),
        scratch_shapes=[pltpu.VMEM((d, de), BF16), pltpu.VMEM((d, de), BF16),
                        pltpu.VMEM((de, d), BF16)] * 2,
    )
    return pl.pallas_call(
        _expert_kernel,
        grid_spec=grid_spec,
        out_shape=jax.ShapeDtypeStruct((n_tiles * MOE_TM * TOKEN_ROWS_OUT, LANES), F32),
        compiler_params=_params(("arbitrary",)),
        name="experts",
    )(tea, teb, nv, xs, wg, wu, wd, wg, wu, wd)


def _combine_kernel(pos_ref, ys_ref, x1_ref, route_ref, fg_ref, o_ref, g, sem, *, tm):
    i = pl.program_id(0)
    slot = i % 2

    def gather(step, buf, wait):
        base = step * tm

        def body(r2, carry):
            for k in range(2):
                r = 2 * r2 + k
                c = _token_copy(ys_ref, pos_ref[base + r], g.at[buf], r, sem.at[buf],
                                TOKEN_ROWS_OUT)
                if wait:
                    c.wait()
                else:
                    c.start(priority=k)
            return carry

        lax.fori_loop(0, tm // 2, body, 0, unroll=DMA_UNROLL)

    @pl.when(i == 0)
    def _():
        gather(0, 0, wait=False)

    @pl.when(i + 1 < pl.num_programs(0))
    def _():
        gather(i + 1, 1 - slot, wait=False)

    gather(i, slot, wait=True)
    gs = g.at[slot]
    ya, yb = (jnp.concatenate([gs[pl.ds(off + s, tm, stride=TOKEN_ROWS_OUT), :]
                               for s in range(SUBLANES)], axis=1) for off in (0, SUBLANES))
    route = route_ref[...]
    x2 = x1_ref[...] + (route[:, ROUTE_WA:ROUTE_WA + 1] * ya + route[:, ROUTE_WB:ROUTE_WB + 1] * yb)
    var = jnp.mean(x2 * x2, axis=-1, keepdims=True)
    o_ref[...] = x2 * lax.rsqrt(var + EPS) * fg_ref[...]


def _combine(pos, ys, x1, route, final_g):
    n, d = x1.shape
    tm = min(COMBINE_TM, n)
    kern = functools.partial(_combine_kernel, tm=tm)
    grid_spec = pltpu.PrefetchScalarGridSpec(
        num_scalar_prefetch=1,
        grid=(n // tm,),
        in_specs=[pl.BlockSpec(memory_space=pl.ANY),
                  pl.BlockSpec((tm, d), lambda i, *_: (i, 0)),
                  pl.BlockSpec((tm, LANES), lambda i, *_: (i, 0)),
                  pl.BlockSpec((1, d), lambda i, *_: (0, 0))],
        out_specs=pl.BlockSpec((tm, d), lambda i, *_: (i, 0)),
        scratch_shapes=[pltpu.VMEM((2, tm * TOKEN_ROWS_OUT, LANES), F32),
                        pltpu.SemaphoreType.DMA((2,))],
    )
    return pl.pallas_call(
        kern,
        grid_spec=grid_spec,
        out_shape=jax.ShapeDtypeStruct((n, d), F32),
        compiler_params=_params(("arbitrary",)),
        name="combine",
    )(pos, ys, x1, route, final_g.reshape(1, d))


def _moe_sparse(t, route, wg, wu, wd, x1, final_g):
    n = x1.shape[0]
    n_tiles_max = (n + N_CLASSES * (MOE_TM - 1)) // MOE_TM
    pos, te, meta = _plan(route, n_tiles_max)
    pos = pos[:, 0]
    xs = _dispatch(pos, meta[0, :N_CLASSES + 1], t, n_tiles_max * MOE_TM)
    ys = _experts(te[:n_tiles_max, 0], te[:n_tiles_max, 1], meta[0, N_CLASSES:N_CLASSES + 1],
                  xs, wg, wu, wd)
    return _combine(pos, ys, x1, route, final_g)


def _suffix_ones(t):
    j = np.arange(t)[:, None]
    s = np.arange(t)[None, :]
    return jnp.asarray((j >= s).astype(np.float32), dtype=BF16)


def _causal_bias(t):
    row = np.arange(t)[:, None]
    col = np.arange(t)[None, :]
    diag = np.where(col < row, 0.0, MASK_BIAS).astype(np.float32)
    return jnp.asarray(np.stack([np.zeros_like(diag), diag]))


def _chunk_prefix_ones(tt, c):
    t = np.arange(tt)[:, None]
    j = np.arange(tt)[None, :]
    return jnp.asarray(((j <= t) & (t // c == j // c)).astype(np.float32), dtype=BF16)


def _block_diag_ones(w, blk):
    a = np.arange(w)
    return jnp.asarray((a[:, None] // blk == a[None, :] // blk).astype(np.float32), dtype=BF16)


def kernel(x, ln1_g, w_in, w_branch_sb, w_branch_hg, hg_norm_g, hg_lb_logits, w_out, ln2_g,
           w_router_group, b_router_group, w_router_expert, b_router_expert,
           w_exp_gate, w_exp_up, w_exp_down, final_g):
    bsz, seq, d = x.shape
    depth = w_in.shape[0]
    n = bsz * seq
    sb_width = SB_HEADS * SB_HEAD_DIM
    hg_width = HG_HEADS * HG_DIM

    tri_attn = _suffix_ones(min(ATTN_T, seq))
    bias_attn = _causal_bias(min(ATTN_T, seq))
    tt = min(HGRN_TT, seq)
    tri_hg = _chunk_prefix_ones(tt, min(HGRN_C, tt))
    bd = _block_diag_ones(min(MXU_DIM, hg_width), HG_DIM)

    x2 = x.reshape(n, d)
    for l in range(depth):
        qkv, rest = _inproj(x2, ln1_g[l], w_in[l].astype(BF16), sb_width)
        y_sb = _attn(qkv, bsz, seq, tri_attn, bias_attn)
        y_hg = _hgrn(rest, hg_lb_logits, hg_norm_g[l], bsz, seq, l, tri_hg, bd)

        pad = LANES - N_EXPERTS - N_GROUPS
        wr = jnp.concatenate([w_router_expert[l], w_router_group[l],
                              jnp.zeros((d, pad), F32)], axis=1)
        wr_hi = wr.astype(BF16)
        wr_lo = (wr - wr_hi.astype(F32)).astype(BF16)
        wr_split = jnp.concatenate([wr_hi, wr_lo], axis=1)
        br = jnp.concatenate([b_router_expert[l], b_router_group[l],
                              jnp.zeros((pad,), F32)]).reshape(1, LANES)

        last = l == depth - 1
        x1, t, route = _merge(x2, y_sb, y_hg, rest, w_branch_sb[l].astype(BF16),
                              w_branch_hg[l].astype(BF16), w_out[l].astype(BF16), ln2_g[l],
                              wr_split, br)
        assert last, "final rmsnorm is fused into the last layer's combine kernel"
        x2 = _moe_sparse(t, route, w_exp_gate[l], w_exp_up[l], w_exp_down[l], x1, final_g)
    return x2.reshape(bsz, seq, d)
```

```python
import functools

import jax
import jax.numpy as jnp
import numpy as np
from jax import lax
from jax.experimental import pallas as pl
from jax.experimental.pallas import tpu as pltpu

F32 = jnp.float32
BF16 = jnp.bfloat16

EPS = 1e-6
SB_HEADS = 8
SB_HEAD_DIM = 64
HG_HEADS = 8
HG_DIM = 64
N_GROUPS = 4
EXPERTS_PER_GROUP = 4
N_EXPERTS = N_GROUPS * EXPERTS_PER_GROUP

LANES = 128
SUBLANES = 8
MXU_DIM = 256
LOG2E = 1.4426950408889634
VMEM_LIMIT = 48 * 1024 * 1024

INPROJ_TM = 2048
INPROJ_TN = 512
ATTN_T = MXU_DIM
HGRN_TT = 256
HGRN_C = 32
MERGE_TM = 512
MOE_TM = 384
PLAN_TM = 1024
DISPATCH_TM = 1024
COMBINE_TM = 512


def _params(sem):
    return pltpu.CompilerParams(dimension_semantics=sem, vmem_limit_bytes=VMEM_LIMIT)


def _split_dot(x, m, passes):
    acc = None
    r = x
    for p in range(passes):
        h = r.astype(BF16)
        term = jnp.dot(h, m, preferred_element_type=F32)
        acc = term if acc is None else acc + term
        if p + 1 < passes:
            r = r - h.astype(F32)
    return acc


def _inproj_kernel(x_ref, g_ref, w_ref, qkv_ref, rest_ref, h_scr, *, q_scale):
    j = pl.program_id(1)

    @pl.when(j == 0)
    def _():
        x = x_ref[...]
        var = jnp.mean(x * x, axis=-1, keepdims=True)
        h_scr[...] = (x * lax.rsqrt(var + EPS) * g_ref[...]).astype(BF16)

    acc = jnp.dot(h_scr[...], w_ref[...], preferred_element_type=F32)
    qkv_ref[...] = (acc * jnp.where(j == 0, q_scale, 1.0)).astype(BF16)
    rest_ref[...] = acc


def _inproj(x2, ln_g, w_in, sb_width):
    n, d = x2.shape
    cols = w_in.shape[1]
    tm, tn = min(INPROJ_TM, n), INPROJ_TN
    assert sb_width == tn, "q block must be exactly one column tile"
    nq = 3 * sb_width // tn
    nj = cols // tn
    kern = functools.partial(_inproj_kernel, q_scale=SB_HEAD_DIM ** -0.5)
    return pl.pallas_call(
        kern,
        grid=(n // tm, nj),
        in_specs=[
            pl.BlockSpec((tm, d), lambda i, j: (i, 0)),
            pl.BlockSpec((1, d), lambda i, j: (0, 0)),
            pl.BlockSpec((d, tn), lambda i, j: (0, j)),
        ],
        out_specs=[
            pl.BlockSpec((tm, tn), lambda i, j: (i, jnp.minimum(j, nq))),
            pl.BlockSpec((tm, tn), lambda i, j: (i, jnp.maximum(j - nq, 0))),
        ],
        out_shape=[
            jax.ShapeDtypeStruct((n, (nq + 1) * tn), BF16),
            jax.ShapeDtypeStruct((n, cols - nq * tn), F32),
        ],
        scratch_shapes=[pltpu.VMEM((tm, d), BF16)],
        compiler_params=_params(("parallel", "arbitrary")),
        name="inproj",
    )(x2, ln_g.reshape(1, d), w_in)


ATTN_STAGES = 3
ATTN_STREAMS = 2
MASK_BIAS = -1e30
ATTN_SKIP = 111.0


def _attn_kernel(q_ref, k_ref, v_ref, tri_ref, bias_ref, o_ref,
                 z0, z1, z2, i0, i1, i2, acc_ref, c_ref, *, t, nq):
    zbuf = (z0, z1, z2)
    ibuf = (i0, i1, i2)
    for r in zbuf + ibuf:
        r[...] = jnp.zeros_like(r)
    acc_ref[...] = jnp.zeros_like(acc_ref)
    c_ref[...] = jnp.zeros_like(c_ref)

    lane = lax.broadcasted_iota(jnp.int32, (t, LANES), 1)
    head0 = lane < SB_HEAD_DIM
    nt = (((1,), (1,)), ((), ()))

    def stage_a(qi, kj, slot):
        q = q_ref[pl.ds(pl.multiple_of(qi * t, t), t), :]
        zero = jnp.zeros_like(q)
        q2 = jnp.concatenate([jnp.where(head0, q, zero), jnp.where(head0, zero, q)], axis=0)
        k = k_ref[pl.ds(pl.multiple_of(kj * t, t), t), :]
        z = lax.dot_general(q2, k, nt, preferred_element_type=F32)
        bias = bias_ref[(qi == kj).astype(jnp.int32)]
        zbuf[slot][...] = z + jnp.concatenate([bias, bias], axis=0)

    def stage_b(slot):
        z = zbuf[slot][...]
        p = jnp.maximum(z, 0.0) + jnp.log(1.0 + jnp.exp2(jnp.abs(z) * (-LOG2E)))
        incl = jnp.dot(p.astype(BF16), tri_ref[...], preferred_element_type=F32)
        ibuf[slot][...] = incl
        return incl[:, 0:1]

    def stage_c(qi, kj, slot, stream):
        first = qi == kj
        incl = ibuf[slot][...]
        c = jnp.where(first, 0.0, c_ref[stream])
        a = jnp.exp2((zbuf[slot][...] - incl - c) * LOG2E)
        v = v_ref[pl.ds(pl.multiple_of(kj * t, t), t), :]
        pv = jnp.dot(a.astype(BF16), v, preferred_element_type=F32)
        acc = jnp.where(first, pv, acc_ref[stream] + pv)
        acc_ref[stream] = acc
        c_ref[stream] = c + incl[:, 0:1]
        o_ref[pl.ds(pl.multiple_of(qi * t, t), t), :] = (
            jnp.where(head0, acc[0:t], acc[t:2 * t]).astype(o_ref.dtype))

    def block(carry):
        qs, ks, qn, kn, skip_q, drained = carry
        qs, ks, qn, kn, skip_q = list(qs), list(ks), list(qn), list(kn), list(skip_q)
        all_done = qn[0] >= nq
        for p in range(1, ATTN_STREAMS):
            all_done = jnp.logical_and(all_done, qn[p] >= nq)
        drained = drained + all_done.astype(jnp.int32)
        for r in range(ATTN_STAGES * ATTN_STREAMS):
            sa, sc, sb = r % ATTN_STAGES, (r + 1) % ATTN_STAGES, (r + 2) % ATTN_STAGES
            pa, pb = r % ATTN_STREAMS, (r - 1) % ATTN_STREAMS
            stage_c(qs[sc], ks[sc], sc, pa)
            total_b = stage_b(sb)
            carry_b = jnp.where(qs[sb] == ks[sb], 0.0, c_ref[pb]) + total_b
            skip = skip_q[pa] == qn[pa]
            q_cur = jnp.where(skip, qn[pa] + ATTN_STREAMS, qn[pa])
            k_cur = jnp.where(skip, qn[pa] + ATTN_STREAMS, kn[pa])
            drain = q_cur >= nq
            qa = jnp.where(drain, 0, q_cur)
            ka = jnp.where(drain, 0, k_cur)
            stage_a(qa, ka, sa)
            last = k_cur == 0
            qn[pa] = jnp.where(jnp.logical_and(last, jnp.logical_not(drain)),
                               q_cur + ATTN_STREAMS, q_cur)
            kn[pa] = jnp.where(drain, k_cur, jnp.where(last, q_cur + ATTN_STREAMS, k_cur - 1))
            skip_q[pb] = jnp.where(jnp.min(carry_b) >= ATTN_SKIP, qs[sb], skip_q[pb])
            qs[sa], ks[sa] = qa, ka
        return tuple(qs), tuple(ks), tuple(qn), tuple(kn), tuple(skip_q), drained

    zero = jnp.int32(0)
    first_q = tuple(jnp.int32(p) for p in range(ATTN_STREAMS))
    init = ((zero,) * ATTN_STAGES, (zero,) * ATTN_STAGES, first_q, first_q,
            (jnp.int32(-1),) * ATTN_STREAMS, zero)
    lax.while_loop(lambda carry: carry[5] < 1, block, init)


def _attn(qkv, bsz, seq, tri, bias):
    n = bsz * seq
    t = min(ATTN_T, seq)
    pairs = SB_HEADS * SB_HEAD_DIM // LANES
    kern = functools.partial(_attn_kernel, t=t, nq=seq // t)
    return pl.pallas_call(
        kern,
        grid=(bsz, pairs),
        in_specs=[
            pl.BlockSpec((seq, LANES), lambda b, p: (b, p)),
            pl.BlockSpec((seq, LANES), lambda b, p: (b, pairs + p)),
            pl.BlockSpec((seq, LANES), lambda b, p: (b, 2 * pairs + p)),
            pl.BlockSpec((t, t), lambda b, p: (0, 0)),
            pl.BlockSpec((2, t, t), lambda b, p: (0, 0, 0)),
        ],
        out_specs=pl.BlockSpec((seq, LANES), lambda b, p: (b, p)),
        out_shape=jax.ShapeDtypeStruct((n, pairs * LANES), BF16),
        scratch_shapes=([pltpu.VMEM((2 * t, t), F32)] * (2 * ATTN_STAGES)
                        + [pltpu.VMEM((ATTN_STREAMS, 2 * t, LANES), F32),
                           pltpu.VMEM((ATTN_STREAMS, 2 * t, 1), F32)]),
        compiler_params=_params(("parallel", "parallel")),
        name="attn",
    )(qkv, qkv, qkv, tri, bias)


def _group_dot(x, bd):
    g = bd.shape[0]
    parts = [jnp.dot(x[:, i:i + g], bd, preferred_element_type=F32)
             for i in range(0, x.shape[1], g)]
    return jnp.concatenate(parts, axis=1)


def _hgrn_kernel(q_ref, f_ref, i_ref, g_ref, lbl_ref, ng_ref, tri_ref, bd_ref, o_ref,
                 st_scr, wpad, vpad, wsh, vsh, *, tt, c, layer):
    ti = pl.program_id(1)
    w = q_ref.shape[1]
    nch = tt // c

    @pl.when(ti == 0)
    def _():
        st_scr[...] = jnp.zeros_like(st_scr)

    lg = lbl_ref[...]
    e = jnp.exp(lg - jnp.max(lg, axis=0, keepdims=True))
    lb = jnp.sum(e[0:layer + 1], axis=0, keepdims=True) / jnp.sum(e, axis=0, keepdims=True)

    f = lb + (1.0 - lb) * jax.nn.sigmoid(f_ref[...])
    kk = 1.0 - f
    qv = q_ref[...]
    qs = qv * jax.nn.sigmoid(qv)
    v = i_ref[...]
    bd = bd_ref[...]
    cum = _split_dot_left(tri_ref[...], jnp.log(f) * LOG2E, 3)
    wk = jnp.log(kk) * LOG2E - cum

    pos = lax.broadcasted_iota(jnp.int32, (tt, 1), 0) % c
    wpad[0:SUBLANES, :] = jnp.zeros((SUBLANES, w), F32)
    vpad[0:SUBLANES, :] = jnp.zeros((SUBLANES, w), F32)
    wpad[SUBLANES:SUBLANES + tt, :] = wk
    vpad[SUBLANES:SUBLANES + tt, :] = v
    for b in range(SUBLANES):
        wb = wpad[SUBLANES - b:SUBLANES - b + tt, :]
        vb = vpad[SUBLANES - b:SUBLANES - b + tt, :]
        if b:
            wb = jnp.where(pos >= b, wb, -jnp.inf)
        wsh[b] = wb.reshape(nch, c, w)
        vsh[b] = vb.reshape(nch, c, w)

    qs3 = qs.reshape(nch, c, w)
    cum3 = cum.reshape(nch, c, w)
    acc = None
    for a in reversed(range(c // SUBLANES)):
        rows = c - SUBLANES * a
        qa = qs3[:, SUBLANES * a:, :].reshape(nch * rows, w)
        ca = cum3[:, SUBLANES * a:, :].reshape(nch * rows, w)
        acc_a = None
        for b in range(SUBLANES):
            wb = wsh[b, :, 0:rows, :].reshape(nch * rows, w)
            vb = vsh[b, :, 0:rows, :].reshape(nch * rows, w)
            dd = qa * jnp.exp2(ca + wb)
            term = _group_dot(dd.astype(BF16), bd) * vb
            acc_a = term if acc_a is None else acc_a + term
        acc_a = acc_a.reshape(nch, rows, w)
        if acc is not None:
            acc_a = acc_a + jnp.concatenate([jnp.zeros((nch, SUBLANES, w), F32), acc], axis=1)
        acc = acc_a
    acc = acc.reshape(tt, w)

    grp = st_scr.shape[1]
    bdmask = bd[0:grp, 0:grp] != 0
    outs = []
    for ci in range(nch):
        r0 = ci * c
        cum_c = cum[r0:r0 + c]
        last = cum_c[c - 1:c]
        qd = (qs[r0:r0 + c] * jnp.exp2(cum_c)).astype(BF16)
        kd = (kk[r0:r0 + c] * jnp.exp2(last - cum_c)).astype(BF16)
        vc = v[r0:r0 + c].astype(BF16)
        dec = jnp.exp2(last)
        o_parts = []
        for gi in range(w // grp):
            sl = slice(gi * grp, (gi + 1) * grp)
            st = st_scr[gi]
            o_parts.append(lax.dot_general(qd[:, sl], st.astype(BF16), (((1,), (1,)), ((), ())),
                                           preferred_element_type=F32))
            upd = lax.dot_general(vc[:, sl], kd[:, sl], (((0,), (0,)), ((), ())),
                                  preferred_element_type=F32)
            st_scr[gi] = st * dec[:, sl] + jnp.where(bdmask, upd, 0.0)
        outs.append(jnp.concatenate(o_parts, axis=1))
    o = acc + jnp.concatenate(outs, axis=0)

    o2 = o * o
    o2_hi = o2.astype(BF16)
    o2_lo = (o2 - o2_hi.astype(F32)).astype(BF16)
    ms = (_group_dot(o2_hi, bd) + _group_dot(o2_lo, bd)) * (1.0 / HG_DIM)
    gv = g_ref[...]
    o = o * lax.rsqrt(ms + EPS) * ng_ref[...] * (gv * jax.nn.sigmoid(gv))
    o_ref[...] = o.astype(o_ref.dtype)


def _split_dot_left(m, x, passes):
    acc = None
    r = x
    for p in range(passes):
        h = r.astype(BF16)
        term = jnp.dot(m, h, preferred_element_type=F32)
        acc = term if acc is None else acc + term
        if p + 1 < passes:
            r = r - h.astype(F32)
    return acc


def _hgrn(rest, lb_logits, norm_g, bsz, seq, layer, tri, bd):
    n = bsz * seq
    w = HG_HEADS * HG_DIM
    grp = bd.shape[0]
    tt = min(HGRN_TT, seq)
    c = min(HGRN_C, tt)
    nt = seq // tt
    kern = functools.partial(_hgrn_kernel, tt=tt, c=c, layer=layer)
    col = lambda j: pl.BlockSpec((tt, w), lambda b, i, j=j: (b * nt + i, j))
    const = lambda shape: pl.BlockSpec(shape, lambda b, i: (0, 0))
    return pl.pallas_call(
        kern,
        grid=(bsz, nt),
        in_specs=[col(0), col(1), col(2), col(3),
                  const(lb_logits.shape), const((1, w)), const((tt, tt)), const((grp, grp))],
        out_specs=pl.BlockSpec((tt, w), lambda b, i: (b * nt + i, 0)),
        out_shape=jax.ShapeDtypeStruct((n, w), BF16),
        scratch_shapes=([pltpu.VMEM((w // LANES, LANES, LANES), F32)]
                        + [pltpu.VMEM((SUBLANES + tt, w), F32)] * 2
                        + [pltpu.VMEM((SUBLANES, tt // c, c, w), F32)] * 2),
        compiler_params=_params(("parallel", "arbitrary")),
        name="hgrn",
    )(rest, rest, rest, rest, lb_logits, norm_g.reshape(1, w), tri, bd)


PAIR_A = (0, 0, 0, 1, 1, 3)
PAIR_B = (1, 2, 3, 3, 2, 2)
N_PAIRS = len(PAIR_A)
N_CLASSES = N_GROUPS * N_PAIRS
assert sorted(tuple(sorted(p)) for p in zip(PAIR_A, PAIR_B)) == [
    (a, b) for a in range(EXPERTS_PER_GROUP) for b in range(a + 1, EXPERTS_PER_GROUP)]
ROUTE_CLS, ROUTE_WA, ROUTE_WB = 0, 1, 2


def _pair_slots(pidx):
    a = b = jnp.zeros_like(pidx)
    for p in range(N_PAIRS):
        a = jnp.where(pidx == p, float(PAIR_A[p]), a)
        b = jnp.where(pidx == p, float(PAIR_B[p]), b)
    return a, b


def _pair_index(lo, hi):
    pidx = jnp.zeros_like(lo)
    for p in range(N_PAIRS):
        is_p = jnp.logical_and(lo == min(PAIR_A[p], PAIR_B[p]), hi == max(PAIR_A[p], PAIR_B[p]))
        pidx = jnp.where(is_p, float(p), pidx)
    return pidx


def _route(lg):
    lane = lax.broadcasted_iota(jnp.int32, lg.shape, 1)
    neg = jnp.float32(-jnp.inf)
    big = jnp.int32(LANES)
    gmask = jnp.logical_and(lane >= N_EXPERTS, lane < N_EXPERTS + N_GROUPS)
    gl = jnp.where(gmask, lg, neg)
    gmax = jnp.max(gl, axis=1, keepdims=True)
    gidx = jnp.min(jnp.where(gl == gmax, lane, big), axis=1, keepdims=True) - N_EXPERTS
    w_grp = 1.0 / jnp.sum(jnp.where(gmask, jnp.exp(gl - gmax), 0.0), axis=1, keepdims=True)
    in_grp = jnp.logical_and(lane < N_EXPERTS, lane // EXPERTS_PER_GROUP == gidx)
    l1 = jnp.where(in_grp, lg, neg)
    v1 = jnp.max(l1, axis=1, keepdims=True)
    i1 = jnp.min(jnp.where(l1 == v1, lane, big), axis=1, keepdims=True)
    l2 = jnp.where(jnp.logical_and(in_grp, lane != i1), lg, neg)
    v2 = jnp.max(l2, axis=1, keepdims=True)
    i2 = jnp.min(jnp.where(l2 == v2, lane, big), axis=1, keepdims=True)
    e2 = jnp.exp(v2 - v1)
    p1 = 1.0 / (1.0 + e2)
    p2 = e2 * p1
    loc1 = (i1 - gidx * EXPERTS_PER_GROUP).astype(F32)
    loc2 = (i2 - gidx * EXPERTS_PER_GROUP).astype(F32)
    pidx = _pair_index(jnp.minimum(loc1, loc2), jnp.maximum(loc1, loc2))
    cls = gidx.astype(F32) * N_PAIRS + pidx
    a_loc, _ = _pair_slots(pidx)
    first_is_a = loc1 == a_loc
    wa = jnp.where(first_is_a, p1, p2) * w_grp
    wb = jnp.where(first_is_a, p2, p1) * w_grp
    return jnp.where(lane == ROUTE_CLS, cls,
                     jnp.where(lane == ROUTE_WA, wa, jnp.where(lane == ROUTE_WB, wb, 0.0)))


def _store_token_tiles(ref, x):
    rows, d = x.shape
    assert d == SUBLANES * LANES
    for s in range(SUBLANES):
        ref[pl.ds(s, rows, stride=SUBLANES), :] = x[:, s * LANES:(s + 1) * LANES]


def _load_token_tiles(ref):
    rows = ref.shape[0] // SUBLANES
    return jnp.concatenate([ref[pl.ds(s, rows, stride=SUBLANES), :] for s in range(SUBLANES)],
                           axis=1)


def _merge_kernel(x_ref, ysb_ref, yhg_ref, gsb_ref, ghg_ref, wbs_ref, wbh_ref, wo_ref,
                  ln_ref, wr_ref, br_ref, x1_ref, t_ref, route_ref):
    a = jnp.dot(ysb_ref[...], wbs_ref[...], preferred_element_type=F32)
    b = jnp.dot(yhg_ref[...], wbh_ref[...], preferred_element_type=F32)
    merged = jax.nn.sigmoid(gsb_ref[...]) * a + jax.nn.sigmoid(ghg_ref[...]) * b
    x1 = x_ref[...] + jnp.dot(merged.astype(BF16), wo_ref[...], preferred_element_type=F32)
    x1_ref[...] = x1
    var = jnp.mean(x1 * x1, axis=-1, keepdims=True)
    t = x1 * lax.rsqrt(var + EPS) * ln_ref[...]
    _store_token_tiles(t_ref, t)
    t_hi = t.astype(BF16)
    t_lo = (t - t_hi.astype(F32)).astype(BF16)
    p_hi = jnp.dot(t_hi, wr_ref[...], preferred_element_type=F32)
    p_lo = jnp.dot(t_lo, wr_ref[...], preferred_element_type=F32)
    lg = (p_hi[:, :LANES] + p_hi[:, LANES:]) + (p_lo[:, :LANES] + p_lo[:, LANES:]) + br_ref[...]
    route_ref[...] = _route(lg)


def _merge(x2, y_sb, y_hg, rest, wbs, wbh, wo, ln_g, wr, br):
    n, d = x2.shape
    tm = min(MERGE_TM, n)
    w_sb, w_hg = y_sb.shape[1], y_hg.shape[1]
    gate_blk = (rest.shape[1] - 2 * d) // d
    row = lambda wdt, j=0: pl.BlockSpec((tm, wdt), lambda i, j=j: (i, j))
    const = lambda shape: pl.BlockSpec(shape, lambda i: (0, 0))
    return pl.pallas_call(
        _merge_kernel,
        grid=(n // tm,),
        in_specs=[row(d), row(w_sb), row(w_hg), row(d, gate_blk), row(d, gate_blk + 1),
                  const(wbs.shape), const(wbh.shape), const(wo.shape), const((1, d)),
                  const(wr.shape), const((1, LANES))],
        out_specs=[row(d), pl.BlockSpec((tm * SUBLANES, LANES), lambda i: (i, 0)), row(LANES)],
        out_shape=[jax.ShapeDtypeStruct((n, d), F32),
                   jax.ShapeDtypeStruct((n * SUBLANES, LANES), F32),
                   jax.ShapeDtypeStruct((n, LANES), F32)],
        compiler_params=_params(("parallel",)),
        name="merge",
    )(x2, y_sb, y_hg, rest, rest, wbs, wbh, wo, ln_g.reshape(1, d), wr, br)


TOKEN_ROWS_OUT = 2 * SUBLANES


def _plan_kernel(route_ref, ltri_ref, utri_ref, pos_ref, te_ref, meta_ref,
                 tot_scr, run_scr, *, tile_rows):
    ph = pl.program_id(0)
    i = pl.program_id(1)
    route = route_ref[...]
    tm = route.shape[0]
    lane = lax.broadcasted_iota(jnp.int32, (tm, LANES), 1)
    is_cls = lane.astype(F32) == route[:, ROUTE_CLS:ROUTE_CLS + 1]
    sel = jnp.where(is_cls, 1.0, 0.0)

    @pl.when(jnp.logical_and(ph == 0, i == 0))
    def _():
        tot_scr[...] = jnp.zeros_like(tot_scr)

    @pl.when(ph == 0)
    def _():
        tot_scr[...] += jnp.sum(sel, axis=0, keepdims=True)

    @pl.when(ph == 1)
    def _():
        tot = tot_scr[...]
        top = tot + (tile_rows - 1.0)
        tiles = jnp.floor(top * (1.0 / tile_rows))
        tiles = jnp.where(tiles * tile_rows > top, tiles - 1.0, tiles)
        tiles = jnp.where((tiles + 1.0) * tile_rows <= top, tiles + 1.0, tiles)
        first_tile = _split_dot(jnp.broadcast_to(tiles, (SUBLANES, LANES)), utri_ref[...], 3)[0:1]

        @pl.when(i == 0)
        def _():
            run_scr[...] = jnp.zeros_like(run_scr)
            lane1 = lax.broadcasted_iota(jnp.int32, (1, LANES), 1)
            is_c = lane1 < N_CLASSES
            end_tile = first_tile + tiles
            n_valid = jnp.sum(jnp.where(is_c, tiles, 0.0), axis=1, keepdims=True)
            rows = te_ref.shape[0]
            j = lax.broadcasted_iota(jnp.int32, (rows, LANES), 0).astype(F32)
            j = jnp.minimum(j, n_valid - 1.0)
            lane2 = lax.broadcasted_iota(jnp.int32, (rows, LANES), 1)
            done = jnp.where(jnp.logical_and(lane2 < N_CLASSES, end_tile <= j), 1.0, 0.0)
            tc = jnp.sum(done, axis=1, keepdims=True)
            tg = jnp.floor((tc + 0.5) * (1.0 / N_PAIRS))
            ta, tb = _pair_slots(tc - tg * N_PAIRS)
            te = jnp.where(lane2 == 0, tg * EXPERTS_PER_GROUP + ta, tg * EXPERTS_PER_GROUP + tb)
            te_ref[...] = te.astype(jnp.int32)
            last_tile = jnp.where(jnp.logical_and(is_c, tiles > 0), end_tile - 1.0, -1.0)
            meta = jnp.where(lane1 == N_CLASSES, n_valid, last_tile)
            meta_ref[...] = jnp.broadcast_to(meta, meta_ref.shape).astype(jnp.int32)

        rank = jnp.dot(ltri_ref[...], sel.astype(BF16), preferred_element_type=F32) + run_scr[...]
        dest = jnp.sum(jnp.where(is_cls, first_tile * tile_rows + rank, 0.0),
                       axis=1, keepdims=True)
        pos_ref[...] = jnp.where(lane == 0, dest, 0.0).astype(jnp.int32)
        run_scr[...] += jnp.sum(sel, axis=0, keepdims=True)


def _plan(route, n_tiles_max):
    n = route.shape[0]
    tm = min(PLAN_TM, n)
    r = np.arange(tm)
    ltri = jnp.asarray((r[None, :] < r[:, None]).astype(np.float32), dtype=BF16)
    e = np.arange(LANES)
    utri = jnp.asarray((e[:, None] < e[None, :]).astype(np.float32), dtype=BF16)
    te_rows = -(-n_tiles_max // SUBLANES) * SUBLANES
    kern = functools.partial(_plan_kernel, tile_rows=MOE_TM)
    return pl.pallas_call(
        kern,
        grid=(2, n // tm),
        in_specs=[pl.BlockSpec((tm, LANES), lambda p, i: (i, 0)),
                  pl.BlockSpec((tm, tm), lambda p, i: (0, 0)),
                  pl.BlockSpec((LANES, LANES), lambda p, i: (0, 0))],
        out_specs=[pl.BlockSpec((tm, LANES), lambda p, i: (i * p, 0)),
                   pl.BlockSpec((te_rows, LANES), lambda p, i: (0, 0)),
                   pl.BlockSpec((SUBLANES, LANES), lambda p, i: (0, 0))],
        out_shape=[jax.ShapeDtypeStruct((n, LANES), jnp.int32),
                   jax.ShapeDtypeStruct((te_rows, LANES), jnp.int32),
                   jax.ShapeDtypeStruct((SUBLANES, LANES), jnp.int32)],
        scratch_shapes=[pltpu.VMEM((1, LANES), F32), pltpu.VMEM((1, LANES), F32)],
        compiler_params=_params(("arbitrary", "arbitrary")),
        name="plan",
    )(route, ltri, utri)


DMA_UNROLL = 8


def _token_copy(src_ref, r, dst_ref, p, sem, rows):
    return pltpu.make_async_copy(src_ref.at[pl.ds(pl.multiple_of(r * rows, rows), rows)],
                                 dst_ref.at[pl.ds(pl.multiple_of(p * rows, rows), rows)], sem)


def _dispatch_kernel(pos_ref, meta_ref, t_ref, xs_ref, zero_scr, sem, *, tm, tile_rows):
    i = pl.program_id(0)
    tile_rows = tile_rows * SUBLANES
    n_tiles = xs_ref.shape[0] // tile_rows

    @pl.when(i == 0)
    def _():
        zero_scr[...] = jnp.zeros_like(zero_scr)
        n_valid = meta_ref[N_CLASSES]
        clears = [(meta_ref[c], meta_ref[c] >= 0) for c in range(N_CLASSES)]
        clears += [(n_tiles - 1 - k, n_tiles - 1 - k >= n_valid) for k in range(N_CLASSES)]

        def clear(tile):
            return pltpu.make_async_copy(
                zero_scr, xs_ref.at[pl.ds(tile * tile_rows, tile_rows)], sem)

        for tile, cond in clears:
            @pl.when(cond)
            def _():
                clear(tile).start()
        for tile, cond in clears:
            @pl.when(cond)
            def _():
                clear(tile).wait()

    base = i * tm

    def start(r2, carry):
        for k in range(2):
            r = 2 * r2 + k
            _token_copy(t_ref, r, xs_ref, pos_ref[base + r], sem, SUBLANES).start(priority=k)
        return carry

    def wait(r, carry):
        _token_copy(t_ref, r, xs_ref, pos_ref[base + r], sem, SUBLANES).wait()
        return carry

    lax.fori_loop(0, tm // 2, start, 0, unroll=DMA_UNROLL)
    lax.fori_loop(0, tm, wait, 0, unroll=DMA_UNROLL)


def _dispatch(pos, meta, t, n_rows):
    n = t.shape[0] // SUBLANES
    tm = min(DISPATCH_TM, n)
    kern = functools.partial(_dispatch_kernel, tm=tm, tile_rows=MOE_TM)
    grid_spec = pltpu.PrefetchScalarGridSpec(
        num_scalar_prefetch=2,
        grid=(n // tm,),
        in_specs=[pl.BlockSpec((tm * SUBLANES, LANES), lambda i, *_: (i, 0))],
        out_specs=pl.BlockSpec(memory_space=pl.ANY),
        scratch_shapes=[pltpu.VMEM((MOE_TM * SUBLANES, LANES), F32), pltpu.SemaphoreType.DMA(())],
    )
    return pl.pallas_call(
        kern,
        grid_spec=grid_spec,
        out_shape=jax.ShapeDtypeStruct((n_rows * SUBLANES, LANES), F32),
        compiler_params=_params(("arbitrary",)),
        name="dispatch",
    )(pos, meta, t)


def _expert_kernel(tea_ref, teb_ref, nv_ref, xs_ref, wga, wua, wda, wgb, wub, wdb, ys_ref):
    in_use = pl.program_id(0) < nv_ref[0]
    rows = xs_ref.shape[0] // SUBLANES

    @pl.when(in_use)
    def _():
        x = _load_token_tiles(xs_ref).astype(BF16)
        for slot, (wg, wu, wd) in enumerate(((wga, wua, wda), (wgb, wub, wdb))):
            hg = jnp.dot(x, wg[...].astype(BF16), preferred_element_type=F32)
            hu = jnp.dot(x, wu[...].astype(BF16), preferred_element_type=F32)
            a = (hg * jax.nn.sigmoid(hg) * hu).astype(BF16)
            y = jnp.dot(a, wd[...].astype(BF16), preferred_element_type=F32)
            for s in range(SUBLANES):
                ys_ref[pl.ds(slot * SUBLANES + s, rows, stride=TOKEN_ROWS_OUT), :] = (
                    y[:, s * LANES:(s + 1) * LANES])

    @pl.when(jnp.logical_not(in_use))
    def _():
        ys_ref[...] = jnp.zeros_like(ys_ref)


def _experts(tea, teb, nv, xs, wg, wu, wd):
    _, d, de = wg.shape
    assert d == SUBLANES * LANES
    n_tiles = xs.shape[0] // (MOE_TM * SUBLANES)
    wa = lambda shape: pl.BlockSpec(shape, lambda j, tea, teb, nv: (tea[j], 0, 0))
    wb = lambda shape: pl.BlockSpec(shape, lambda j, tea, teb, nv: (teb[j], 0, 0))
    grid_spec = pltpu.PrefetchScalarGridSpec(
        num_scalar_prefetch=3,
        grid=(n_tiles,),
        in_specs=[pl.BlockSpec((MOE_TM * SUBLANES, LANES), lambda j, *_: (j, 0)),
                  wa((None, d, de)), wa((None, d, de)), wa((None, de, d)),
                  wb((None, d, de)), wb((None, d, de)), wb((None, de, d))],
        out_specs=pl.BlockSpec((MOE_TM * TOKEN_ROWS_OUT, LANES), lambda j, *_: (j, 0)),
    )
    return pl.pallas_call(
        _expert_kernel,
        grid_spec=grid_spec,
        out_shape=jax.ShapeDtypeStruct((n_tiles * MOE_TM * TOKEN_ROWS_OUT, LANES), F32),
        compiler_params=_params(("arbitrary",)),
        name="experts",
    )(tea, teb, nv, xs, wg, wu, wd, wg, wu, wd)


def _combine_kernel(pos_ref, ys_ref, x1_ref, route_ref, fg_ref, o_ref, g, sem, *, tm):
    i = pl.program_id(0)
    slot = i % 2

    def gather(step, buf, wait):
        base = step * tm

        def body(r2, carry):
            for k in range(2):
                r = 2 * r2 + k
                c = _token_copy(ys_ref, pos_ref[base + r], g.at[buf], r, sem.at[buf],
                                TOKEN_ROWS_OUT)
                if wait:
                    c.wait()
                else:
                    c.start(priority=k)
            return carry

        lax.fori_loop(0, tm // 2, body, 0, unroll=DMA_UNROLL)

    @pl.when(i == 0)
    def _():
        gather(0, 0, wait=False)

    @pl.when(i + 1 < pl.num_programs(0))
    def _():
        gather(i + 1, 1 - slot, wait=False)

    gather(i, slot, wait=True)
    gs = g.at[slot]
    ya, yb = (jnp.concatenate([gs[pl.ds(off + s, tm, stride=TOKEN_ROWS_OUT), :]
                               for s in range(SUBLANES)], axis=1) for off in (0, SUBLANES))
    route = route_ref[...]
    x2 = x1_ref[...] + (route[:, ROUTE_WA:ROUTE_WA + 1] * ya + route[:, ROUTE_WB:ROUTE_WB + 1] * yb)
    var = jnp.mean(x2 * x2, axis=-1, keepdims=True)
    o_ref[...] = x2 * lax.rsqrt(var + EPS) * fg_ref[...]


def _combine(pos, ys, x1, route, final_g):
    n, d = x1.shape
    tm = min(COMBINE_TM, n)
    kern = functools.partial(_combine_kernel, tm=tm)
    grid_spec = pltpu.PrefetchScalarGridSpec(
        num_scalar_prefetch=1,
        grid=(n // tm,),
        in_specs=[pl.BlockSpec(memory_space=pl.ANY),
                  pl.BlockSpec((tm, d), lambda i, *_: (i, 0)),
                  pl.BlockSpec((tm, LANES), lambda i, *_: (i, 0)),
                  pl.BlockSpec((1, d), lambda i, *_: (0, 0))],
        out_specs=pl.BlockSpec((tm, d), lambda i, *_: (i, 0)),
        scratch_shapes=[pltpu.VMEM((2, tm * TOKEN_ROWS_OUT, LANES), F32),
                        pltpu.SemaphoreType.DMA((2,))],
    )
    return pl.pallas_call(
        kern,
        grid_spec=grid_spec,
        out_shape=jax.ShapeDtypeStruct((n, d), F32),
        compiler_params=_params(("arbitrary",)),
        name="combine",
    )(pos, ys, x1, route, final_g.reshape(1, d))


def _moe_sparse(t, route, wg, wu, wd, x1, final_g):
    n = x1.shape[0]
    n_tiles_max = (n + N_CLASSES * (MOE_TM - 1)) // MOE_TM
    pos, te, meta = _plan(route, n_tiles_max)
    pos = pos[:, 0]
    xs = _dispatch(pos, meta[0, :N_CLASSES + 1], t, n_tiles_max * MOE_TM)
    ys = _experts(te[:n_tiles_max, 0], te[:n_tiles_max, 1], meta[0, N_CLASSES:N_CLASSES + 1],
                  xs, wg, wu, wd)
    return _combine(pos, ys, x1, route, final_g)


def _suffix_ones(t):
    j = np.arange(t)[:, None]
    s = np.arange(t)[None, :]
    return jnp.asarray((j >= s).astype(np.float32), dtype=BF16)


def _causal_bias(t):
    row = np.arange(t)[:, None]
    col = np.arange(t)[None, :]
    diag = np.where(col < row, 0.0, MASK_BIAS).astype(np.float32)
    return jnp.asarray(np.stack([np.zeros_like(diag), diag]))


def _chunk_prefix_ones(tt, c):
    t = np.arange(tt)[:, None]
    j = np.arange(tt)[None, :]
    return jnp.asarray(((j <= t) & (t // c == j // c)).astype(np.float32), dtype=BF16)


def _block_diag_ones(w, blk):
    a = np.arange(w)
    return jnp.asarray((a[:, None] // blk == a[None, :] // blk).astype(np.float32), dtype=BF16)


def kernel(x, ln1_g, w_in, w_branch_sb, w_branch_hg, hg_norm_g, hg_lb_logits, w_out, ln2_g,
           w_router_group, b_router_group, w_router_expert, b_router_expert,
           w_exp_gate, w_exp_up, w_exp_down, final_g):
    bsz, seq, d = x.shape
    depth = w_in.shape[0]
    n = bsz * seq
    sb_width = SB_HEADS * SB_HEAD_DIM
    hg_width = HG_HEADS * HG_DIM

    tri_attn = _suffix_ones(min(ATTN_T, seq))
    bias_attn = _causal_bias(min(ATTN_T, seq))
    tt = min(HGRN_TT, seq)
    tri_hg = _chunk_prefix_ones(tt, min(HGRN_C, tt))
    bd = _block_diag_ones(min(MXU_DIM, hg_width), HG_DIM)

    x2 = x.reshape(n, d)
    for l in range(depth):
        qkv, rest = _inproj(x2, ln1_g[l], w_in[l].astype(BF16), sb_width)
        y_sb = _attn(qkv, bsz, seq, tri_attn, bias_attn)
        y_hg = _hgrn(rest, hg_lb_logits, hg_norm_g[l], bsz, seq, l, tri_hg, bd)

        pad = LANES - N_EXPERTS - N_GROUPS
        wr = jnp.concatenate([w_router_expert[l], w_router_group[l],
                              jnp.zeros((d, pad), F32)], axis=1)
        wr_hi = wr.astype(BF16)
        wr_lo = (wr - wr_hi.astype(F32)).astype(BF16)
        wr_split = jnp.concatenate([wr_hi, wr_lo], axis=1)
        br = jnp.concatenate([b_router_expert[l], b_router_group[l],
                              jnp.zeros((pad,), F32)]).reshape(1, LANES)

        last = l == depth - 1
        x1, t, route = _merge(x2, y_sb, y_hg, rest, w_branch_sb[l].astype(BF16),
                              w_branch_hg[l].astype(BF16), w_out[l].astype(BF16), ln2_g[l],
                              wr_split, br)
        assert last, "final rmsnorm is fused into the last layer's combine kernel"
        x2 = _moe_sparse(t, route, w_exp_gate[l], w_exp_up[l], w_exp_down[l], x1, final_g)
    return x2.reshape(bsz, seq, d)
```

```python
import functools

import jax
import jax.numpy as jnp
import numpy as np
from jax import lax
from jax.experimental import pallas as pl
from jax.experimental.pallas import tpu as pltpu

F32 = jnp.float32
BF16 = jnp.bfloat16

EPS = 1e-6
SB_HEADS = 8
SB_HEAD_DIM = 64
HG_HEADS = 8
HG_DIM = 64
N_GROUPS = 4
EXPERTS_PER_GROUP = 4
N_EXPERTS = N_GROUPS * EXPERTS_PER_GROUP

LANES = 128
SUBLANES = 8
MXU_DIM = 256
LOG2E = 1.4426950408889634
VMEM_LIMIT = 48 * 1024 * 1024

INPROJ_TM = 2048
INPROJ_TN = 512
ATTN_T = MXU_DIM
HGRN_TT = 256
HGRN_C = 32
MERGE_TM = 512
MOE_TM = 384
PLAN_TM = 1024
DISPATCH_TM = 2048
COMBINE_TM = 1024


def _params(sem):
    return pltpu.CompilerParams(dimension_semantics=sem, vmem_limit_bytes=VMEM_LIMIT)


def _split_dot(x, m, passes):
    acc = None
    r = x
    for p in range(passes):
        h = r.astype(BF16)
        term = jnp.dot(h, m, preferred_element_type=F32)
        acc = term if acc is None else acc + term
        if p + 1 < passes:
            r = r - h.astype(F32)
    return acc


def _inproj_kernel(x_ref, g_ref, w_ref, qkv_ref, rest_ref, h_scr, *, q_scale):
    j = pl.program_id(1)

    @pl.when(j == 0)
    def _():
        x = x_ref[...]
        var = jnp.mean(x * x, axis=-1, keepdims=True)
        h_scr[...] = (x * lax.rsqrt(var + EPS) * g_ref[...]).astype(BF16)

    acc = jnp.dot(h_scr[...], w_ref[...].astype(BF16), preferred_element_type=F32)
    qkv_ref[...] = (acc * jnp.where(j == 0, q_scale, 1.0)).astype(BF16)
    rest_ref[...] = acc


def _inproj(x2, ln_g, w_in, sb_width):
    n, d = x2.shape
    cols = w_in.shape[1]
    tm, tn = min(INPROJ_TM, n), INPROJ_TN
    assert sb_width == tn, "q block must be exactly one column tile"
    nq = 3 * sb_width // tn
    nj = cols // tn
    kern = functools.partial(_inproj_kernel, q_scale=SB_HEAD_DIM ** -0.5)
    return pl.pallas_call(
        kern,
        grid=(n // tm, nj),
        in_specs=[
            pl.BlockSpec((tm, d), lambda i, j: (i, 0)),
            pl.BlockSpec((1, d), lambda i, j: (0, 0)),
            pl.BlockSpec((d, tn), lambda i, j: (0, j)),
        ],
        out_specs=[
            pl.BlockSpec((tm, tn), lambda i, j: (i, jnp.minimum(j, nq))),
            pl.BlockSpec((tm, tn), lambda i, j: (i, jnp.maximum(j - nq, 0))),
        ],
        out_shape=[
            jax.ShapeDtypeStruct((n, (nq + 1) * tn), BF16),
            jax.ShapeDtypeStruct((n, cols - nq * tn), F32),
        ],
        scratch_shapes=[pltpu.VMEM((tm, d), BF16)],
        compiler_params=_params(("parallel", "arbitrary")),
        name="inproj",
    )(x2, ln_g.reshape(1, d), w_in)


ATTN_STAGES = 3
ATTN_STREAMS = 2
MASK_BIAS = -1e30
ATTN_SKIP = 111.0


def _attn_kernel(q_ref, k_ref, v_ref, tri_ref, bias_ref, o_ref,
                 z0, z1, z2, i0, i1, i2, acc_ref, c_ref, *, t, nq):
    zbuf = (z0, z1, z2)
    ibuf = (i0, i1, i2)
    for r in zbuf + ibuf:
        r[...] = jnp.zeros_like(r)
    acc_ref[...] = jnp.zeros_like(acc_ref)
    c_ref[...] = jnp.zeros_like(c_ref)

    lane = lax.broadcasted_iota(jnp.int32, (t, LANES), 1)
    head0 = lane < SB_HEAD_DIM
    nt = (((1,), (1,)), ((), ()))

    def stage_a(qi, kj, slot):
        q = q_ref[pl.ds(pl.multiple_of(qi * t, t), t), :]
        zero = jnp.zeros_like(q)
        q2 = jnp.concatenate([jnp.where(head0, q, zero), jnp.where(head0, zero, q)], axis=0)
        k = k_ref[pl.ds(pl.multiple_of(kj * t, t), t), :]
        z = lax.dot_general(q2, k, nt, preferred_element_type=F32)
        bias = bias_ref[(qi == kj).astype(jnp.int32)]
        zbuf[slot][...] = z + jnp.concatenate([bias, bias], axis=0)

    def stage_b(slot):
        z = zbuf[slot][...]
        p = jnp.maximum(z, 0.0) + jnp.log(1.0 + jnp.exp2(jnp.abs(z) * (-LOG2E)))
        incl = jnp.dot(p.astype(BF16), tri_ref[...], preferred_element_type=F32)
        ibuf[slot][...] = incl
        return incl[:, 0:1]

    def stage_c(qi, kj, slot, stream):
        first = qi == kj
        incl = ibuf[slot][...]
        c = jnp.where(first, 0.0, c_ref[stream])
        a = jnp.exp2((zbuf[slot][...] - incl - c) * LOG2E)
        v = v_ref[pl.ds(pl.multiple_of(kj * t, t), t), :]
        pv = jnp.dot(a.astype(BF16), v, preferred_element_type=F32)
        acc = jnp.where(first, pv, acc_ref[stream] + pv)
        acc_ref[stream] = acc
        c_ref[stream] = c + incl[:, 0:1]
        o_ref[pl.ds(pl.multiple_of(qi * t, t), t), :] = (
            jnp.where(head0, acc[0:t], acc[t:2 * t]).astype(o_ref.dtype))

    def block(carry):
        qs, ks, qn, kn, skip_q, drained = carry
        qs, ks, qn, kn, skip_q = list(qs), list(ks), list(qn), list(kn), list(skip_q)
        all_done = qn[0] >= nq
        for p in range(1, ATTN_STREAMS):
            all_done = jnp.logical_and(all_done, qn[p] >= nq)
        drained = drained + all_done.astype(jnp.int32)
        for r in range(ATTN_STAGES * ATTN_STREAMS):
            sa, sc, sb = r % ATTN_STAGES, (r + 1) % ATTN_STAGES, (r + 2) % ATTN_STAGES
            pa, pb = r % ATTN_STREAMS, (r - 1) % ATTN_STREAMS
            stage_c(qs[sc], ks[sc], sc, pa)
            total_b = stage_b(sb)
            carry_b = jnp.where(qs[sb] == ks[sb], 0.0, c_ref[pb]) + total_b
            skip = skip_q[pa] == qn[pa]
            q_cur = jnp.where(skip, qn[pa] + ATTN_STREAMS, qn[pa])
            k_cur = jnp.where(skip, qn[pa] + ATTN_STREAMS, kn[pa])
            drain = q_cur >= nq
            qa = jnp.where(drain, 0, q_cur)
            ka = jnp.where(drain, 0, k_cur)
            stage_a(qa, ka, sa)
            last = k_cur == 0
            qn[pa] = jnp.where(jnp.logical_and(last, jnp.logical_not(drain)),
                               q_cur + ATTN_STREAMS, q_cur)
            kn[pa] = jnp.where(drain, k_cur, jnp.where(last, q_cur + ATTN_STREAMS, k_cur - 1))
            skip_q[pb] = jnp.where(jnp.min(carry_b) >= ATTN_SKIP, qs[sb], skip_q[pb])
            qs[sa], ks[sa] = qa, ka
        return tuple(qs), tuple(ks), tuple(qn), tuple(kn), tuple(skip_q), drained

    zero = jnp.int32(0)
    first_q = tuple(jnp.int32(p) for p in range(ATTN_STREAMS))
    init = ((zero,) * ATTN_STAGES, (zero,) * ATTN_STAGES, first_q, first_q,
            (jnp.int32(-1),) * ATTN_STREAMS, zero)
    lax.while_loop(lambda carry: carry[5] < 1, block, init)


def _attn(qkv, bsz, seq, tri, bias):
    n = bsz * seq
    t = min(ATTN_T, seq)
    pairs = SB_HEADS * SB_HEAD_DIM // LANES
    kern = functools.partial(_attn_kernel, t=t, nq=seq // t)
    return pl.pallas_call(
        kern,
        grid=(bsz, pairs),
        in_specs=[
            pl.BlockSpec((seq, LANES), lambda b, p: (b, p)),
            pl.BlockSpec((seq, LANES), lambda b, p: (b, pairs + p)),
            pl.BlockSpec((seq, LANES), lambda b, p: (b, 2 * pairs + p)),
            pl.BlockSpec((t, t), lambda b, p: (0, 0)),
            pl.BlockSpec((2, t, t), lambda b, p: (0, 0, 0)),
        ],
        out_specs=pl.BlockSpec((seq, LANES), lambda b, p: (b, p)),
        out_shape=jax.ShapeDtypeStruct((n, pairs * LANES), BF16),
        scratch_shapes=([pltpu.VMEM((2 * t, t), F32)] * (2 * ATTN_STAGES)
                        + [pltpu.VMEM((ATTN_STREAMS, 2 * t, LANES), F32),
                           pltpu.VMEM((ATTN_STREAMS, 2 * t, 1), F32)]),
        compiler_params=_params(("parallel", "parallel")),
        name="attn",
    )(qkv, qkv, qkv, tri, bias)


def _group_dot(x, bd):
    g = bd.shape[0]
    parts = [jnp.dot(x[:, i:i + g], bd, preferred_element_type=F32)
             for i in range(0, x.shape[1], g)]
    return jnp.concatenate(parts, axis=1)


def _hgrn_kernel(q_ref, f_ref, i_ref, g_ref, lbl_ref, ng_ref, tri_ref, bd_ref, o_ref,
                 st_scr, wpad, vpad, wsh, vsh, *, tt, c, layer):
    ti = pl.program_id(1)
    w = q_ref.shape[1]
    nch = tt // c

    @pl.when(ti == 0)
    def _():
        st_scr[...] = jnp.zeros_like(st_scr)

    lg = lbl_ref[...]
    e = jnp.exp(lg - jnp.max(lg, axis=0, keepdims=True))
    lb = jnp.sum(e[0:layer + 1], axis=0, keepdims=True) / jnp.sum(e, axis=0, keepdims=True)

    f = lb + (1.0 - lb) * jax.nn.sigmoid(f_ref[...])
    kk = 1.0 - f
    qv = q_ref[...]
    qs = qv * jax.nn.sigmoid(qv)
    v = i_ref[...]
    bd = bd_ref[...]
    cum = _split_dot_left(tri_ref[...], jnp.log(f) * LOG2E, 3)
    wk = jnp.log(kk) * LOG2E - cum

    pos = lax.broadcasted_iota(jnp.int32, (tt, 1), 0) % c
    wpad[0:SUBLANES, :] = jnp.zeros((SUBLANES, w), F32)
    vpad[0:SUBLANES, :] = jnp.zeros((SUBLANES, w), F32)
    wpad[SUBLANES:SUBLANES + tt, :] = wk
    vpad[SUBLANES:SUBLANES + tt, :] = v
    for b in range(SUBLANES):
        wb = wpad[SUBLANES - b:SUBLANES - b + tt, :]
        vb = vpad[SUBLANES - b:SUBLANES - b + tt, :]
        if b:
            wb = jnp.where(pos >= b, wb, -jnp.inf)
        wsh[b] = wb.reshape(nch, c, w)
        vsh[b] = vb.reshape(nch, c, w)

    qs3 = qs.reshape(nch, c, w)
    cum3 = cum.reshape(nch, c, w)
    acc = None
    for a in reversed(range(c // SUBLANES)):
        rows = c - SUBLANES * a
        qa = qs3[:, SUBLANES * a:, :].reshape(nch * rows, w)
        ca = cum3[:, SUBLANES * a:, :].reshape(nch * rows, w)
        acc_a = None
        for b in range(SUBLANES):
            wb = wsh[b, :, 0:rows, :].reshape(nch * rows, w)
            vb = vsh[b, :, 0:rows, :].reshape(nch * rows, w)
            dd = qa * jnp.exp2(ca + wb)
            term = _group_dot(dd.astype(BF16), bd) * vb
            acc_a = term if acc_a is None else acc_a + term
        acc_a = acc_a.reshape(nch, rows, w)
        if acc is not None:
            acc_a = acc_a + jnp.concatenate([jnp.zeros((nch, SUBLANES, w), F32), acc], axis=1)
        acc = acc_a
    acc = acc.reshape(tt, w)

    grp = st_scr.shape[1]
    bdmask = bd[0:grp, 0:grp] != 0
    outs = []
    for ci in range(nch):
        r0 = ci * c
        cum_c = cum[r0:r0 + c]
        last = cum_c[c - 1:c]
        qd = (qs[r0:r0 + c] * jnp.exp2(cum_c)).astype(BF16)
        kd = (kk[r0:r0 + c] * jnp.exp2(last - cum_c)).astype(BF16)
        vc = v[r0:r0 + c].astype(BF16)
        dec = jnp.exp2(last)
        o_parts = []
        for gi in range(w // grp):
            sl = slice(gi * grp, (gi + 1) * grp)
            st = st_scr[gi]
            o_parts.append(lax.dot_general(qd[:, sl], st.astype(BF16), (((1,), (1,)), ((), ())),
                                           preferred_element_type=F32))
            upd = lax.dot_general(vc[:, sl], kd[:, sl], (((0,), (0,)), ((), ())),
                                  preferred_element_type=F32)
            st_scr[gi] = st * dec[:, sl] + jnp.where(bdmask, upd, 0.0)
        outs.append(jnp.concatenate(o_parts, axis=1))
    o = acc + jnp.concatenate(outs, axis=0)

    o2 = o * o
    o2_hi = o2.astype(BF16)
    o2_lo = (o2 - o2_hi.astype(F32)).astype(BF16)
    ms = (_group_dot(o2_hi, bd) + _group_dot(o2_lo, bd)) * (1.0 / HG_DIM)
    gv = g_ref[...]
    o = o * lax.rsqrt(ms + EPS) * ng_ref[...] * (gv * jax.nn.sigmoid(gv))
    o_ref[...] = o.astype(o_ref.dtype)


def _split_dot_left(m, x, passes):
    acc = None
    r = x
    for p in range(passes):
        h = r.astype(BF16)
        term = jnp.dot(m, h, preferred_element_type=F32)
        acc = term if acc is None else acc + term
        if p + 1 < passes:
            r = r - h.astype(F32)
    return acc


def _hgrn(rest, lb_logits, norm_g, bsz, seq, layer, tri, bd):
    n = bsz * seq
    w = HG_HEADS * HG_DIM
    grp = bd.shape[0]
    tt = min(HGRN_TT, seq)
    c = min(HGRN_C, tt)
    nt = seq // tt
    kern = functools.partial(_hgrn_kernel, tt=tt, c=c, layer=layer)
    col = lambda j: pl.BlockSpec((tt, w), lambda b, i, j=j: (b * nt + i, j))
    const = lambda shape: pl.BlockSpec(shape, lambda b, i: (0, 0))
    return pl.pallas_call(
        kern,
        grid=(bsz, nt),
        in_specs=[col(0), col(1), col(2), col(3),
                  const(lb_logits.shape), const((1, w)), const((tt, tt)), const((grp, grp))],
        out_specs=pl.BlockSpec((tt, w), lambda b, i: (b * nt + i, 0)),
        out_shape=jax.ShapeDtypeStruct((n, w), BF16),
        scratch_shapes=([pltpu.VMEM((w // LANES, LANES, LANES), F32)]
                        + [pltpu.VMEM((SUBLANES + tt, w), F32)] * 2
                        + [pltpu.VMEM((SUBLANES, tt // c, c, w), F32)] * 2),
        compiler_params=_params(("parallel", "arbitrary")),
        name="hgrn",
    )(rest, rest, rest, rest, lb_logits, norm_g.reshape(1, w), tri, bd)


PAIR_A = (0, 0, 0, 1, 1, 3)
PAIR_B = (1, 2, 3, 3, 2, 2)
N_PAIRS = len(PAIR_A)
N_CLASSES = N_GROUPS * N_PAIRS
assert sorted(tuple(sorted(p)) for p in zip(PAIR_A, PAIR_B)) == [
    (a, b) for a in range(EXPERTS_PER_GROUP) for b in range(a + 1, EXPERTS_PER_GROUP)]
ROUTE_CLS, ROUTE_WA, ROUTE_WB = 0, 1, 2


def _pair_slots(pidx):
    a = b = jnp.zeros_like(pidx)
    for p in range(N_PAIRS):
        a = jnp.where(pidx == p, float(PAIR_A[p]), a)
        b = jnp.where(pidx == p, float(PAIR_B[p]), b)
    return a, b


def _pair_index(lo, hi):
    pidx = jnp.zeros_like(lo)
    for p in range(N_PAIRS):
        is_p = jnp.logical_and(lo == min(PAIR_A[p], PAIR_B[p]), hi == max(PAIR_A[p], PAIR_B[p]))
        pidx = jnp.where(is_p, float(p), pidx)
    return pidx


def _route(lg):
    lane = lax.broadcasted_iota(jnp.int32, lg.shape, 1)
    neg = jnp.float32(-jnp.inf)
    big = jnp.int32(LANES)
    gmask = jnp.logical_and(lane >= N_EXPERTS, lane < N_EXPERTS + N_GROUPS)
    gl = jnp.where(gmask, lg, neg)
    gmax = jnp.max(gl, axis=1, keepdims=True)
    gidx = jnp.min(jnp.where(gl == gmax, lane, big), axis=1, keepdims=True) - N_EXPERTS
    w_grp = 1.0 / jnp.sum(jnp.where(gmask, jnp.exp(gl - gmax), 0.0), axis=1, keepdims=True)
    in_grp = jnp.logical_and(lane < N_EXPERTS, lane // EXPERTS_PER_GROUP == gidx)
    l1 = jnp.where(in_grp, lg, neg)
    v1 = jnp.max(l1, axis=1, keepdims=True)
    i1 = jnp.min(jnp.where(l1 == v1, lane, big), axis=1, keepdims=True)
    l2 = jnp.where(jnp.logical_and(in_grp, lane != i1), lg, neg)
    v2 = jnp.max(l2, axis=1, keepdims=True)
    i2 = jnp.min(jnp.where(l2 == v2, lane, big), axis=1, keepdims=True)
    e2 = jnp.exp(v2 - v1)
    p1 = 1.0 / (1.0 + e2)
    p2 = e2 * p1
    loc1 = (i1 - gidx * EXPERTS_PER_GROUP).astype(F32)
    loc2 = (i2 - gidx * EXPERTS_PER_GROUP).astype(F32)
    pidx = _pair_index(jnp.minimum(loc1, loc2), jnp.maximum(loc1, loc2))
    cls = gidx.astype(F32) * N_PAIRS + pidx
    a_loc, _ = _pair_slots(pidx)
    first_is_a = loc1 == a_loc
    wa = jnp.where(first_is_a, p1, p2) * w_grp
    wb = jnp.where(first_is_a, p2, p1) * w_grp
    return jnp.where(lane == ROUTE_CLS, cls,
                     jnp.where(lane == ROUTE_WA, wa, jnp.where(lane == ROUTE_WB, wb, 0.0)))


def _store_token_tiles(ref, x):
    rows, d = x.shape
    assert d == SUBLANES * LANES
    for s in range(SUBLANES):
        ref[pl.ds(s, rows, stride=SUBLANES), :] = x[:, s * LANES:(s + 1) * LANES]


def _load_token_tiles(ref):
    rows = ref.shape[0] // SUBLANES
    return jnp.concatenate([ref[pl.ds(s, rows, stride=SUBLANES), :] for s in range(SUBLANES)],
                           axis=1)


def _merge_kernel(x_ref, ysb_ref, yhg_ref, gsb_ref, ghg_ref, wbs_ref, wbh_ref, wo_ref,
                  ln_ref, wr_ref, br_ref, x1_ref, t_ref, route_ref):
    a = jnp.dot(ysb_ref[...], wbs_ref[...], preferred_element_type=F32)
    b = jnp.dot(yhg_ref[...], wbh_ref[...], preferred_element_type=F32)
    merged = jax.nn.sigmoid(gsb_ref[...]) * a + jax.nn.sigmoid(ghg_ref[...]) * b
    x1 = x_ref[...] + jnp.dot(merged.astype(BF16), wo_ref[...], preferred_element_type=F32)
    x1_ref[...] = x1
    var = jnp.mean(x1 * x1, axis=-1, keepdims=True)
    t = x1 * lax.rsqrt(var + EPS) * ln_ref[...]
    _store_token_tiles(t_ref, t)
    t_hi = t.astype(BF16)
    t_lo = (t - t_hi.astype(F32)).astype(BF16)
    p_hi = jnp.dot(t_hi, wr_ref[...], preferred_element_type=F32)
    p_lo = jnp.dot(t_lo, wr_ref[...], preferred_element_type=F32)
    lg = (p_hi[:, :LANES] + p_hi[:, LANES:]) + (p_lo[:, :LANES] + p_lo[:, LANES:]) + br_ref[...]
    route_ref[...] = _route(lg)


def _merge(x2, y_sb, y_hg, rest, wbs, wbh, wo, ln_g, wr, br):
    n, d = x2.shape
    tm = min(MERGE_TM, n)
    w_sb, w_hg = y_sb.shape[1], y_hg.shape[1]
    gate_blk = (rest.shape[1] - 2 * d) // d
    row = lambda wdt, j=0: pl.BlockSpec((tm, wdt), lambda i, j=j: (i, j))
    const = lambda shape: pl.BlockSpec(shape, lambda i: (0, 0))
    return pl.pallas_call(
        _merge_kernel,
        grid=(n // tm,),
        in_specs=[row(d), row(w_sb), row(w_hg), row(d, gate_blk), row(d, gate_blk + 1),
                  const(wbs.shape), const(wbh.shape), const(wo.shape), const((1, d)),
                  const(wr.shape), const((1, LANES))],
        out_specs=[row(d), pl.BlockSpec((tm * SUBLANES, LANES), lambda i: (i, 0)), row(LANES)],
        out_shape=[jax.ShapeDtypeStruct((n, d), F32),
                   jax.ShapeDtypeStruct((n * SUBLANES, LANES), F32),
                   jax.ShapeDtypeStruct((n, LANES), F32)],
        compiler_params=_params(("parallel",)),
        name="merge",
    )(x2, y_sb, y_hg, rest, rest, wbs, wbh, wo, ln_g.reshape(1, d), wr, br)


TOKEN_ROWS_OUT = 2 * SUBLANES


def _plan_kernel(route_ref, ltri_ref, utri_ref, pos_ref, te_ref, meta_ref,
                 tot_scr, run_scr, *, tile_rows):
    ph = pl.program_id(0)
    i = pl.program_id(1)
    route = route_ref[...]
    tm = route.shape[0]
    lane = lax.broadcasted_iota(jnp.int32, (tm, LANES), 1)
    is_cls = lane.astype(F32) == route[:, ROUTE_CLS:ROUTE_CLS + 1]
    sel = jnp.where(is_cls, 1.0, 0.0)

    @pl.when(jnp.logical_and(ph == 0, i == 0))
    def _():
        tot_scr[...] = jnp.zeros_like(tot_scr)

    @pl.when(ph == 0)
    def _():
        tot_scr[...] += jnp.sum(sel, axis=0, keepdims=True)

    @pl.when(ph == 1)
    def _():
        tot = tot_scr[...]
        top = tot + (tile_rows - 1.0)
        tiles = jnp.floor(top * (1.0 / tile_rows))
        tiles = jnp.where(tiles * tile_rows > top, tiles - 1.0, tiles)
        tiles = jnp.where((tiles + 1.0) * tile_rows <= top, tiles + 1.0, tiles)
        first_tile = _split_dot(jnp.broadcast_to(tiles, (SUBLANES, LANES)), utri_ref[...], 3)[0:1]

        @pl.when(i == 0)
        def _():
            run_scr[...] = jnp.zeros_like(run_scr)
            lane1 = lax.broadcasted_iota(jnp.int32, (1, LANES), 1)
            is_c = lane1 < N_CLASSES
            end_tile = first_tile + tiles
            n_valid = jnp.sum(jnp.where(is_c, tiles, 0.0), axis=1, keepdims=True)
            rows = te_ref.shape[0]
            j = lax.broadcasted_iota(jnp.int32, (rows, LANES), 0).astype(F32)
            j = jnp.minimum(j, n_valid - 1.0)
            lane2 = lax.broadcasted_iota(jnp.int32, (rows, LANES), 1)
            done = jnp.where(jnp.logical_and(lane2 < N_CLASSES, end_tile <= j), 1.0, 0.0)
            tc = jnp.sum(done, axis=1, keepdims=True)
            tg = jnp.floor((tc + 0.5) * (1.0 / N_PAIRS))
            ta, tb = _pair_slots(tc - tg * N_PAIRS)
            te = jnp.where(lane2 == 0, tg * EXPERTS_PER_GROUP + ta, tg * EXPERTS_PER_GROUP + tb)
            te_ref[...] = te.astype(jnp.int32)
            last_tile = jnp.where(jnp.logical_and(is_c, tiles > 0), end_tile - 1.0, -1.0)
            meta = jnp.where(lane1 == N_CLASSES, n_valid, last_tile)
            meta_ref[...] = jnp.broadcast_to(meta, meta_ref.shape).astype(jnp.int32)

        rank = jnp.dot(ltri_ref[...], sel.astype(BF16), preferred_element_type=F32) + run_scr[...]
        dest = jnp.sum(jnp.where(is_cls, first_tile * tile_rows + rank, 0.0),
                       axis=1, keepdims=True)
        pos_ref[...] = jnp.where(lane == 0, dest, 0.0).astype(jnp.int32)
        run_scr[...] += jnp.sum(sel, axis=0, keepdims=True)


def _plan(route, n_tiles_max):
    n = route.shape[0]
    tm = min(PLAN_TM, n)
    r = np.arange(tm)
    ltri = jnp.asarray((r[None, :] < r[:, None]).astype(np.float32), dtype=BF16)
    e = np.arange(LANES)
    utri = jnp.asarray((e[:, None] < e[None, :]).astype(np.float32), dtype=BF16)
    te_rows = -(-n_tiles_max // SUBLANES) * SUBLANES
    kern = functools.partial(_plan_kernel, tile_rows=MOE_TM)
    return pl.pallas_call(
        kern,
        grid=(2, n // tm),
        in_specs=[pl.BlockSpec((tm, LANES), lambda p, i: (i, 0)),
                  pl.BlockSpec((tm, tm), lambda p, i: (0, 0)),
                  pl.BlockSpec((LANES, LANES), lambda p, i: (0, 0))],
        out_specs=[pl.BlockSpec((tm, LANES), lambda p, i: (i * p, 0)),
                   pl.BlockSpec((te_rows, LANES), lambda p, i: (0, 0)),
                   pl.BlockSpec((SUBLANES, LANES), lambda p, i: (0, 0))],
        out_shape=[jax.ShapeDtypeStruct((n, LANES), jnp.int32),
                   jax.ShapeDtypeStruct((te_rows, LANES), jnp.int32),
                   jax.ShapeDtypeStruct((SUBLANES, LANES), jnp.int32)],
        scratch_shapes=[pltpu.VMEM((1, LANES), F32), pltpu.VMEM((1, LANES), F32)],
        compiler_params=_params(("arbitrary", "arbitrary")),
        name="plan",
    )(route, ltri, utri)


DMA_UNROLL = 8


def _token_copy(src_ref, r, dst_ref, p, sem, rows):
    return pltpu.make_async_copy(src_ref.at[pl.ds(pl.multiple_of(r * rows, rows), rows)],
                                 dst_ref.at[pl.ds(pl.multiple_of(p * rows, rows), rows)], sem)


def _dispatch_kernel(pos_ref, meta_ref, t_ref, xs_ref, zero_scr, sem, *, tm, tile_rows):
    i = pl.program_id(0)
    tile_rows = tile_rows * SUBLANES
    n_tiles = xs_ref.shape[0] // tile_rows

    @pl.when(i == 0)
    def _():
        zero_scr[...] = jnp.zeros_like(zero_scr)
        n_valid = meta_ref[N_CLASSES]
        clears = [(meta_ref[c], meta_ref[c] >= 0) for c in range(N_CLASSES)]
        clears += [(n_tiles - 1 - k, n_tiles - 1 - k >= n_valid) for k in range(N_CLASSES)]

        def clear(tile):
            return pltpu.make_async_copy(
                zero_scr, xs_ref.at[pl.ds(tile * tile_rows, tile_rows)], sem)

        for tile, cond in clears:
            @pl.when(cond)
            def _():
                clear(tile).start()
        for tile, cond in clears:
            @pl.when(cond)
            def _():
                clear(tile).wait()

    base = i * tm

    def start(r2, carry):
        for k in range(2):
            r = 2 * r2 + k
            _token_copy(t_ref, r, xs_ref, pos_ref[base + r], sem, SUBLANES).start(priority=k)
        return carry

    def wait(r, carry):
        _token_copy(t_ref, r, xs_ref, pos_ref[base + r], sem, SUBLANES).wait()
        return carry

    lax.fori_loop(0, tm // 2, start, 0, unroll=DMA_UNROLL)
    lax.fori_loop(0, tm, wait, 0, unroll=DMA_UNROLL)


def _dispatch(pos, meta, t, n_rows):
    n = t.shape[0] // SUBLANES
    tm = min(DISPATCH_TM, n)
    kern = functools.partial(_dispatch_kernel, tm=tm, tile_rows=MOE_TM)
    grid_spec = pltpu.PrefetchScalarGridSpec(
        num_scalar_prefetch=2,
        grid=(n // tm,),
        in_specs=[pl.BlockSpec((tm * SUBLANES, LANES), lambda i, *_: (i, 0))],
        out_specs=pl.BlockSpec(memory_space=pl.ANY),
        scratch_shapes=[pltpu.VMEM((MOE_TM * SUBLANES, LANES), F32), pltpu.SemaphoreType.DMA(())],
    )
    return pl.pallas_call(
        kern,
        grid_spec=grid_spec,
        out_shape=jax.ShapeDtypeStruct((n_rows * SUBLANES, LANES), F32),
        compiler_params=_params(("arbitrary",)),
        name="dispatch",
    )(pos, meta, t)


def _expert_kernel(tea_ref, teb_ref, nv_ref, xs_ref, wga, wua, wda, wgb, wub, wdb, ys_ref):
    in_use = pl.program_id(0) < nv_ref[0]
    rows = xs_ref.shape[0] // SUBLANES

    @pl.when(in_use)
    def _():
        x = _load_token_tiles(xs_ref).astype(BF16)
        for slot, (wg, wu, wd) in enumerate(((wga, wua, wda), (wgb, wub, wdb))):
            hg = jnp.dot(x, wg[...].astype(BF16), preferred_element_type=F32)
            hu = jnp.dot(x, wu[...].astype(BF16), preferred_element_type=F32)
            a = (hg * jax.nn.sigmoid(hg) * hu).astype(BF16)
            y = jnp.dot(a, wd[...].astype(BF16), preferred_element_type=F32)
            for s in range(SUBLANES):
                ys_ref[pl.ds(slot * SUBLANES + s, rows, stride=TOKEN_ROWS_OUT), :] = (
                    y[:, s * LANES:(s + 1) * LANES])

    @pl.when(jnp.logical_not(in_use))
    def _():
        ys_ref[...] = jnp.zeros_like(ys_ref)


def _experts(tea, teb, nv, xs, wg, wu, wd):
    _, d, de = wg.shape
    assert d == SUBLANES * LANES
    n_tiles = xs.shape[0] // (MOE_TM * SUBLANES)
    wa = lambda shape: pl.BlockSpec(shape, lambda j, tea, teb, nv: (tea[j], 0, 0))
    wb = lambda shape: pl.BlockSpec(shape, lambda j, tea, teb, nv: (teb[j], 0, 0))
    grid_spec = pltpu.PrefetchScalarGridSpec(
        num_scalar_prefetch=3,
        grid=(n_tiles,),
        in_specs=[pl.BlockSpec((MOE_TM * SUBLANES, LANES),
                               lambda j, tea, teb, nv: (jnp.minimum(j, nv[0] - 1), 0)),
                  wa((None, d, de)), wa((None, d, de)), wa((None, de, d)),
                  wb((None, d, de)), wb((None, d, de)), wb((None, de, d))],
        out_specs=pl.BlockSpec((MOE_TM * TOKEN_ROWS_OUT, LANES), lambda j, *_: (j, 0)),
    )
    return pl.pallas_call(
        _expert_kernel,
        grid_spec=grid_spec,
        out_shape=jax.ShapeDtypeStruct((n_tiles * MOE_TM * TOKEN_ROWS_OUT, LANES), F32),
        compiler_params=_params(("arbitrary",)),
        name="experts",
    )(tea, teb, nv, xs, wg, wu, wd, wg, wu, wd)


def _combine_kernel(pos_ref, ys_ref, x1_ref, route_ref, fg_ref, o_ref, g, sem, *, tm):
    i = pl.program_id(0)
    slot = i % 2

    def gather(step, buf, wait):
        base = step * tm

        def body(r2, carry):
            for k in range(2):
                r = 2 * r2 + k
                c = _token_copy(ys_ref, pos_ref[base + r], g.at[buf], r, sem.at[buf],
                                TOKEN_ROWS_OUT)
                if wait:
                    c.wait()
                else:
                    c.start(priority=k)
            return carry

        lax.fori_loop(0, tm // 2, body, 0, unroll=DMA_UNROLL)

    @pl.when(i == 0)
    def _():
        gather(0, 0, wait=False)

    @pl.when(i + 1 < pl.num_programs(0))
    def _():
        gather(i + 1, 1 - slot, wait=False)

    gather(i, slot, wait=True)
    gs = g.at[slot]
    ya, yb = (jnp.concatenate([gs[pl.ds(off + s, tm, stride=TOKEN_ROWS_OUT), :]
                               for s in range(SUBLANES)], axis=1) for off in (0, SUBLANES))
    route = route_ref[...]
    x2 = x1_ref[...] + (route[:, ROUTE_WA:ROUTE_WA + 1] * ya + route[:, ROUTE_WB:ROUTE_WB + 1] * yb)
    var = jnp.mean(x2 * x2, axis=-1, keepdims=True)
    o_ref[...] = x2 * lax.rsqrt(var + EPS) * fg_ref[...]


def _combine(pos, ys, x1, route, final_g):
    n, d = x1.shape
    tm = min(COMBINE_TM, n)
    kern = functools.partial(_combine_kernel, tm=tm)
    grid_spec = pltpu.PrefetchScalarGridSpec(
        num_scalar_prefetch=1,
        grid=(n // tm,),
        in_specs=[pl.BlockSpec(memory_space=pl.ANY),
                  pl.BlockSpec((tm, d), lambda i, *_: (i, 0)),
                  pl.BlockSpec((tm, LANES), lambda i, *_: (i, 0)),
                  pl.BlockSpec((1, d), lambda i, *_: (0, 0))],
        out_specs=pl.BlockSpec((tm, d), lambda i, *_: (i, 0)),
        scratch_shapes=[pltpu.VMEM((2, tm * TOKEN_ROWS_OUT, LANES), F32),
                        pltpu.SemaphoreType.DMA((2,))],
    )
    return pl.pallas_call(
        kern,
        grid_spec=grid_spec,
        out_shape=jax.ShapeDtypeStruct((n, d), F32),
        compiler_params=_params(("arbitrary",)),
        name="combine",
    )(pos, ys, x1, route, final_g.reshape(1, d))


def _moe_sparse(t, route, wg, wu, wd, x1, final_g):
    n = x1.shape[0]
    n_tiles_max = (n + N_CLASSES * (MOE_TM - 1)) // MOE_TM
    pos, te, meta = _plan(route, n_tiles_max)
    pos = pos[:, 0]
    xs = _dispatch(pos, meta[0, :N_CLASSES + 1], t, n_tiles_max * MOE_TM)
    ys = _experts(te[:n_tiles_max, 0], te[:n_tiles_max, 1], meta[0, N_CLASSES:N_CLASSES + 1],
                  xs, wg, wu, wd)
    return _combine(pos, ys, x1, route, final_g)


def _suffix_ones(t):
    j = np.arange(t)[:, None]
    s = np.arange(t)[None, :]
    return jnp.asarray((j >= s).astype(np.float32), dtype=BF16)


def _causal_bias(t):
    row = np.arange(t)[:, None]
    col = np.arange(t)[None, :]
    diag = np.where(col < row, 0.0, MASK_BIAS).astype(np.float32)
    return jnp.asarray(np.stack([np.zeros_like(diag), diag]))


def _chunk_prefix_ones(tt, c):
    t = np.arange(tt)[:, None]
    j = np.arange(tt)[None, :]
    return jnp.asarray(((j <= t) & (t // c == j // c)).astype(np.float32), dtype=BF16)


def _block_diag_ones(w, blk):
    a = np.arange(w)
    return jnp.asarray((a[:, None] // blk == a[None, :] // blk).astype(np.float32), dtype=BF16)


def kernel(x, ln1_g, w_in, w_branch_sb, w_branch_hg, hg_norm_g, hg_lb_logits, w_out, ln2_g,
           w_router_group, b_router_group, w_router_expert, b_router_expert,
           w_exp_gate, w_exp_up, w_exp_down, final_g):
    bsz, seq, d = x.shape
    depth = w_in.shape[0]
    n = bsz * seq
    sb_width = SB_HEADS * SB_HEAD_DIM
    hg_width = HG_HEADS * HG_DIM

    tri_attn = _suffix_ones(min(ATTN_T, seq))
    bias_attn = _causal_bias(min(ATTN_T, seq))
    tt = min(HGRN_TT, seq)
    tri_hg = _chunk_prefix_ones(tt, min(HGRN_C, tt))
    bd = _block_diag_ones(min(MXU_DIM, hg_width), HG_DIM)

    x2 = x.reshape(n, d)
    for l in range(depth):
        qkv, rest = _inproj(x2, ln1_g[l], w_in[l], sb_width)
        y_sb = _attn(qkv, bsz, seq, tri_attn, bias_attn)
        y_hg = _hgrn(rest, hg_lb_logits, hg_norm_g[l], bsz, seq, l, tri_hg, bd)

        pad = LANES - N_EXPERTS - N_GROUPS
        wr = jnp.concatenate([w_router_expert[l], w_router_group[l],
                              jnp.zeros((d, pad), F32)], axis=1)
        wr_hi = wr.astype(BF16)
        wr_lo = (wr - wr_hi.astype(F32)).astype(BF16)
        wr_split = jnp.concatenate([wr_hi, wr_lo], axis=1)
        br = jnp.concatenate([b_router_expert[l], b_router_group[l],
                              jnp.zeros((pad,), F32)]).reshape(1, LANES)

        last = l == depth - 1
        x1, t, route = _merge(x2, y_sb, y_hg, rest, w_branch_sb[l].astype(BF16),
                              w_branch_hg[l].astype(BF16), w_out[l].astype(BF16), ln2_g[l],
                              wr_split, br)
        assert last, "final rmsnorm is fused into the last layer's combine kernel"
        x2 = _moe_sparse(t, route, w_exp_gate[l], w_exp_up[l], w_exp_down[l], x1, final_g)
    return x2.reshape(bsz, seq, d)
```

```python
import functools

import jax
import jax.numpy as jnp
import numpy as np
from jax import lax
from jax.experimental import pallas as pl
from jax.experimental.pallas import tpu as pltpu

F32 = jnp.float32
BF16 = jnp.bfloat16

EPS = 1e-6
SB_HEADS = 8
SB_HEAD_DIM = 64
HG_HEADS = 8
HG_DIM = 64
N_GROUPS = 4
EXPERTS_PER_GROUP = 4
N_EXPERTS = N_GROUPS * EXPERTS_PER_GROUP

LANES = 128
SUBLANES = 8
MXU_DIM = 256
LOG2E = 1.4426950408889634
VMEM_LIMIT = 48 * 1024 * 1024

INPROJ_TM = 2048
INPROJ_TN = 512
ATTN_T = MXU_DIM
HGRN_TT = 256
HGRN_C = 32
MERGE_TM = 512
MOE_TM = 384
PLAN_TM = 1024
DISPATCH_TM = 1024
COMBINE_TM = 512


def _params(sem):
    return pltpu.CompilerParams(dimension_semantics=sem, vmem_limit_bytes=VMEM_LIMIT)


def _split_dot(x, m, passes):
    acc = None
    r = x
    for p in range(passes):
        h = r.astype(BF16)
        term = jnp.dot(h, m, preferred_element_type=F32)
        acc = term if acc is None else acc + term
        if p + 1 < passes:
            r = r - h.astype(F32)
    return acc


def _inproj_kernel(x_ref, g_ref, w_ref, qkv_ref, rest_ref, h_scr, *, q_scale):
    j = pl.program_id(1)

    @pl.when(j == 0)
    def _():
        x = x_ref[...]
        var = jnp.mean(x * x, axis=-1, keepdims=True)
        h_scr[...] = (x * lax.rsqrt(var + EPS) * g_ref[...]).astype(BF16)

    acc = jnp.dot(h_scr[...], w_ref[...].astype(BF16), preferred_element_type=F32)
    qkv_ref[...] = (acc * jnp.where(j == 0, q_scale, 1.0)).astype(BF16)
    rest_ref[...] = acc


def _inproj(x2, ln_g, w_in, sb_width):
    n, d = x2.shape
    cols = w_in.shape[1]
    tm, tn = min(INPROJ_TM, n), INPROJ_TN
    assert sb_width == tn, "q block must be exactly one column tile"
    nq = 3 * sb_width // tn
    nj = cols // tn
    kern = functools.partial(_inproj_kernel, q_scale=SB_HEAD_DIM ** -0.5)
    return pl.pallas_call(
        kern,
        grid=(n // tm, nj),
        in_specs=[
            pl.BlockSpec((tm, d), lambda i, j: (i, 0)),
            pl.BlockSpec((1, d), lambda i, j: (0, 0)),
            pl.BlockSpec((d, tn), lambda i, j: (0, j)),
        ],
        out_specs=[
            pl.BlockSpec((tm, tn), lambda i, j: (i, jnp.minimum(j, nq))),
            pl.BlockSpec((tm, tn), lambda i, j: (i, jnp.maximum(j - nq, 0))),
        ],
        out_shape=[
            jax.ShapeDtypeStruct((n, (nq + 1) * tn), BF16),
            jax.ShapeDtypeStruct((n, cols - nq * tn), F32),
        ],
        scratch_shapes=[pltpu.VMEM((tm, d), BF16)],
        compiler_params=_params(("parallel", "arbitrary")),
        name="inproj",
    )(x2, ln_g.reshape(1, d), w_in)


ATTN_STAGES = 3
ATTN_STREAMS = 2
MASK_BIAS = -1e30
ATTN_SKIP = 111.0


def _attn_kernel(q_ref, k_ref, v_ref, tri_ref, bias_ref, o_ref,
                 z0, z1, z2, i0, i1, i2, acc_ref, c_ref, *, t, nq):
    zbuf = (z0, z1, z2)
    ibuf = (i0, i1, i2)
    for r in zbuf + ibuf:
        r[...] = jnp.zeros_like(r)
    acc_ref[...] = jnp.zeros_like(acc_ref)
    c_ref[...] = jnp.zeros_like(c_ref)

    lane = lax.broadcasted_iota(jnp.int32, (t, LANES), 1)
    head0 = lane < SB_HEAD_DIM
    nt = (((1,), (1,)), ((), ()))

    def stage_a(qi, kj, slot):
        q = q_ref[pl.ds(pl.multiple_of(qi * t, t), t), :]
        zero = jnp.zeros_like(q)
        q2 = jnp.concatenate([jnp.where(head0, q, zero), jnp.where(head0, zero, q)], axis=0)
        k = k_ref[pl.ds(pl.multiple_of(kj * t, t), t), :]
        z = lax.dot_general(q2, k, nt, preferred_element_type=F32)
        bias = bias_ref[(qi == kj).astype(jnp.int32)]
        zbuf[slot][...] = z + jnp.concatenate([bias, bias], axis=0)

    def stage_b(slot):
        z = zbuf[slot][...]
        p = jnp.maximum(z, 0.0) + jnp.log(1.0 + jnp.exp2(jnp.abs(z) * (-LOG2E)))
        incl = jnp.dot(p.astype(BF16), tri_ref[...], preferred_element_type=F32)
        ibuf[slot][...] = incl
        return incl[:, 0:1]

    def stage_c(qi, kj, slot, stream):
        first = qi == kj
        incl = ibuf[slot][...]
        c = jnp.where(first, 0.0, c_ref[stream])
        a = jnp.exp2((zbuf[slot][...] - incl - c) * LOG2E)
        v = v_ref[pl.ds(pl.multiple_of(kj * t, t), t), :]
        pv = jnp.dot(a.astype(BF16), v, preferred_element_type=F32)
        acc = jnp.where(first, pv, acc_ref[stream] + pv)
        acc_ref[stream] = acc
        c_ref[stream] = c + incl[:, 0:1]
        o_ref[pl.ds(pl.multiple_of(qi * t, t), t), :] = (
            jnp.where(head0, acc[0:t], acc[t:2 * t]).astype(o_ref.dtype))

    def block(carry):
        qs, ks, qn, kn, skip_q, drained = carry
        qs, ks, qn, kn, skip_q = list(qs), list(ks), list(qn), list(kn), list(skip_q)
        all_done = qn[0] >= nq
        for p in range(1, ATTN_STREAMS):
            all_done = jnp.logical_and(all_done, qn[p] >= nq)
        drained = drained + all_done.astype(jnp.int32)
        for r in range(ATTN_STAGES * ATTN_STREAMS):
            sa, sc, sb = r % ATTN_STAGES, (r + 1) % ATTN_STAGES, (r + 2) % ATTN_STAGES
            pa, pb = r % ATTN_STREAMS, (r - 1) % ATTN_STREAMS
            stage_c(qs[sc], ks[sc], sc, pa)
            total_b = stage_b(sb)
            carry_b = jnp.where(qs[sb] == ks[sb], 0.0, c_ref[pb]) + total_b
            skip = skip_q[pa] == qn[pa]
            q_cur = jnp.where(skip, qn[pa] + ATTN_STREAMS, qn[pa])
            k_cur = jnp.where(skip, qn[pa] + ATTN_STREAMS, kn[pa])
            drain = q_cur >= nq
            qa = jnp.where(drain, 0, q_cur)
            ka = jnp.where(drain, 0, k_cur)
            stage_a(qa, ka, sa)
            last = k_cur == 0
            qn[pa] = jnp.where(jnp.logical_and(last, jnp.logical_not(drain)),
                               q_cur + ATTN_STREAMS, q_cur)
            kn[pa] = jnp.where(drain, k_cur, jnp.where(last, q_cur + ATTN_STREAMS, k_cur - 1))
            skip_q[pb] = jnp.where(jnp.min(carry_b) >= ATTN_SKIP, qs[sb], skip_q[pb])
            qs[sa], ks[sa] = qa, ka
        return tuple(qs), tuple(ks), tuple(qn), tuple(kn), tuple(skip_q), drained

    zero = jnp.int32(0)
    first_q = tuple(jnp.int32(p) for p in range(ATTN_STREAMS))
    init = ((zero,) * ATTN_STAGES, (zero,) * ATTN_STAGES, first_q, first_q,
            (jnp.int32(-1),) * ATTN_STREAMS, zero)
    lax.while_loop(lambda carry: carry[5] < 1, block, init)


def _attn(qkv, bsz, seq, tri, bias):
    n = bsz * seq
    t = min(ATTN_T, seq)
    pairs = SB_HEADS * SB_HEAD_DIM // LANES
    kern = functools.partial(_attn_kernel, t=t, nq=seq // t)
    return pl.pallas_call(
        kern,
        grid=(bsz, pairs),
        in_specs=[
            pl.BlockSpec((seq, LANES), lambda b, p: (b, p)),
            pl.BlockSpec((seq, LANES), lambda b, p: (b, pairs + p)),
            pl.BlockSpec((seq, LANES), lambda b, p: (b, 2 * pairs + p)),
            pl.BlockSpec((t, t), lambda b, p: (0, 0)),
            pl.BlockSpec((2, t, t), lambda b, p: (0, 0, 0)),
        ],
        out_specs=pl.BlockSpec((seq, LANES), lambda b, p: (b, p)),
        out_shape=jax.ShapeDtypeStruct((n, pairs * LANES), BF16),
        scratch_shapes=([pltpu.VMEM((2 * t, t), F32)] * (2 * ATTN_STAGES)
                        + [pltpu.VMEM((ATTN_STREAMS, 2 * t, LANES), F32),
                           pltpu.VMEM((ATTN_STREAMS, 2 * t, 1), F32)]),
        compiler_params=_params(("parallel", "parallel")),
        name="attn",
    )(qkv, qkv, qkv, tri, bias)


def _group_dot(x, bd):
    g = bd.shape[0]
    parts = [jnp.dot(x[:, i:i + g], bd, preferred_element_type=F32)
             for i in range(0, x.shape[1], g)]
    return jnp.concatenate(parts, axis=1)


def _hgrn_kernel(q_ref, f_ref, i_ref, g_ref, lbl_ref, ng_ref, tri_ref, bd_ref, o_ref,
                 st_scr, wpad, vpad, wsh, vsh, *, tt, c, layer):
    ti = pl.program_id(1)
    w = q_ref.shape[1]
    nch = tt // c

    @pl.when(ti == 0)
    def _():
        st_scr[...] = jnp.zeros_like(st_scr)

    lg = lbl_ref[...]
    e = jnp.exp(lg - jnp.max(lg, axis=0, keepdims=True))
    lb = jnp.sum(e[0:layer + 1], axis=0, keepdims=True) / jnp.sum(e, axis=0, keepdims=True)

    f = lb + (1.0 - lb) * jax.nn.sigmoid(f_ref[...])
    kk = 1.0 - f
    qv = q_ref[...]
    qs = qv * jax.nn.sigmoid(qv)
    v = i_ref[...]
    bd = bd_ref[...]
    cum = _split_dot_left(tri_ref[...], jnp.log(f) * LOG2E, 3)
    wk = jnp.log(kk) * LOG2E - cum

    pos = lax.broadcasted_iota(jnp.int32, (tt, 1), 0) % c
    wpad[0:SUBLANES, :] = jnp.zeros((SUBLANES, w), F32)
    vpad[0:SUBLANES, :] = jnp.zeros((SUBLANES, w), F32)
    wpad[SUBLANES:SUBLANES + tt, :] = wk
    vpad[SUBLANES:SUBLANES + tt, :] = v
    for b in range(SUBLANES):
        wb = wpad[SUBLANES - b:SUBLANES - b + tt, :]
        vb = vpad[SUBLANES - b:SUBLANES - b + tt, :]
        if b:
            wb = jnp.where(pos >= b, wb, -jnp.inf)
        wsh[b] = wb.reshape(nch, c, w)
        vsh[b] = vb.reshape(nch, c, w)

    qs3 = qs.reshape(nch, c, w)
    cum3 = cum.reshape(nch, c, w)
    acc = None
    for a in reversed(range(c // SUBLANES)):
        rows = c - SUBLANES * a
        qa = qs3[:, SUBLANES * a:, :].reshape(nch * rows, w)
        ca = cum3[:, SUBLANES * a:, :].reshape(nch * rows, w)
        acc_a = None
        for b in range(SUBLANES):
            wb = wsh[b, :, 0:rows, :].reshape(nch * rows, w)
            vb = vsh[b, :, 0:rows, :].reshape(nch * rows, w)
            dd = qa * jnp.exp2(ca + wb)
            term = _group_dot(dd.astype(BF16), bd) * vb
            acc_a = term if acc_a is None else acc_a + term
        acc_a = acc_a.reshape(nch, rows, w)
        if acc is not None:
            acc_a = acc_a + jnp.concatenate([jnp.zeros((nch, SUBLANES, w), F32), acc], axis=1)
        acc = acc_a
    acc = acc.reshape(tt, w)

    grp = st_scr.shape[1]
    bdmask = bd[0:grp, 0:grp] != 0
    outs = []
    for ci in range(nch):
        r0 = ci * c
        cum_c = cum[r0:r0 + c]
        last = cum_c[c - 1:c]
        qd = (qs[r0:r0 + c] * jnp.exp2(cum_c)).astype(BF16)
        kd = (kk[r0:r0 + c] * jnp.exp2(last - cum_c)).astype(BF16)
        vc = v[r0:r0 + c].astype(BF16)
        dec = jnp.exp2(last)
        o_parts = []
        for gi in range(w // grp):
            sl = slice(gi * grp, (gi + 1) * grp)
            st = st_scr[gi]
            o_parts.append(lax.dot_general(qd[:, sl], st.astype(BF16), (((1,), (1,)), ((), ())),
                                           preferred_element_type=F32))
            upd = lax.dot_general(vc[:, sl], kd[:, sl], (((0,), (0,)), ((), ())),
                                  preferred_element_type=F32)
            st_scr[gi] = st * dec[:, sl] + jnp.where(bdmask, upd, 0.0)
        outs.append(jnp.concatenate(o_parts, axis=1))
    o = acc + jnp.concatenate(outs, axis=0)

    o2 = o * o
    o2_hi = o2.astype(BF16)
    o2_lo = (o2 - o2_hi.astype(F32)).astype(BF16)
    ms = (_group_dot(o2_hi, bd) + _group_dot(o2_lo, bd)) * (1.0 / HG_DIM)
    gv = g_ref[...]
    o = o * lax.rsqrt(ms + EPS) * ng_ref[...] * (gv * jax.nn.sigmoid(gv))
    o_ref[...] = o.astype(o_ref.dtype)


def _split_dot_left(m, x, passes):
    acc = None
    r = x
    for p in range(passes):
        h = r.astype(BF16)
        term = jnp.dot(m, h, preferred_element_type=F32)
        acc = term if acc is None else acc + term
        if p + 1 < passes:
            r = r - h.astype(F32)
    return acc


def _hgrn(rest, lb_logits, norm_g, bsz, seq, layer, tri, bd):
    n = bsz * seq
    w = HG_HEADS * HG_DIM
    grp = bd.shape[0]
    tt = min(HGRN_TT, seq)
    c = min(HGRN_C, tt)
    nt = seq // tt
    kern = functools.partial(_hgrn_kernel, tt=tt, c=c, layer=layer)
    col = lambda j: pl.BlockSpec((tt, w), lambda b, i, j=j: (b * nt + i, j))
    const = lambda shape: pl.BlockSpec(shape, lambda b, i: (0, 0))
    return pl.pallas_call(
        kern,
        grid=(bsz, nt),
        in_specs=[col(0), col(1), col(2), col(3),
                  const(lb_logits.shape), const((1, w)), const((tt, tt)), const((grp, grp))],
        out_specs=pl.BlockSpec((tt, w), lambda b, i: (b * nt + i, 0)),
        out_shape=jax.ShapeDtypeStruct((n, w), BF16),
        scratch_shapes=([pltpu.VMEM((w // LANES, LANES, LANES), F32)]
                        + [pltpu.VMEM((SUBLANES + tt, w), F32)] * 2
                        + [pltpu.VMEM((SUBLANES, tt // c, c, w), F32)] * 2),
        compiler_params=_params(("parallel", "arbitrary")),
        name="hgrn",
    )(rest, rest, rest, rest, lb_logits, norm_g.reshape(1, w), tri, bd)


PAIR_A = (0, 0, 0, 1, 1, 3)
PAIR_B = (1, 2, 3, 3, 2, 2)
N_PAIRS = len(PAIR_A)
N_CLASSES = N_GROUPS * N_PAIRS
assert sorted(tuple(sorted(p)) for p in zip(PAIR_A, PAIR_B)) == [
    (a, b) for a in range(EXPERTS_PER_GROUP) for b in range(a + 1, EXPERTS_PER_GROUP)]
ROUTE_CLS, ROUTE_WA, ROUTE_WB = 0, 1, 2


def _pair_slots(pidx):
    a = b = jnp.zeros_like(pidx)
    for p in range(N_PAIRS):
        a = jnp.where(pidx == p, float(PAIR_A[p]), a)
        b = jnp.where(pidx == p, float(PAIR_B[p]), b)
    return a, b


def _pair_index(lo, hi):
    pidx = jnp.zeros_like(lo)
    for p in range(N_PAIRS):
        is_p = jnp.logical_and(lo == min(PAIR_A[p], PAIR_B[p]), hi == max(PAIR_A[p], PAIR_B[p]))
        pidx = jnp.where(is_p, float(p), pidx)
    return pidx


def _route(lg):
    lane = lax.broadcasted_iota(jnp.int32, lg.shape, 1)
    neg = jnp.float32(-jnp.inf)
    big = jnp.int32(LANES)
    gmask = jnp.logical_and(lane >= N_EXPERTS, lane < N_EXPERTS + N_GROUPS)
    gl = jnp.where(gmask, lg, neg)
    gmax = jnp.max(gl, axis=1, keepdims=True)
    gidx = jnp.min(jnp.where(gl == gmax, lane, big), axis=1, keepdims=True) - N_EXPERTS
    w_grp = 1.0 / jnp.sum(jnp.where(gmask, jnp.exp(gl - gmax), 0.0), axis=1, keepdims=True)
    in_grp = jnp.logical_and(lane < N_EXPERTS, lane // EXPERTS_PER_GROUP == gidx)
    l1 = jnp.where(in_grp, lg, neg)
    v1 = jnp.max(l1, axis=1, keepdims=True)
    i1 = jnp.min(jnp.where(l1 == v1, lane, big), axis=1, keepdims=True)
    l2 = jnp.where(jnp.logical_and(in_grp, lane != i1), lg, neg)
    v2 = jnp.max(l2, axis=1, keepdims=True)
    i2 = jnp.min(jnp.where(l2 == v2, lane, big), axis=1, keepdims=True)
    e2 = jnp.exp(v2 - v1)
    p1 = 1.0 / (1.0 + e2)
    p2 = e2 * p1
    loc1 = (i1 - gidx * EXPERTS_PER_GROUP).astype(F32)
    loc2 = (i2 - gidx * EXPERTS_PER_GROUP).astype(F32)
    pidx = _pair_index(jnp.minimum(loc1, loc2), jnp.maximum(loc1, loc2))
    cls = gidx.astype(F32) * N_PAIRS + pidx
    a_loc, _ = _pair_slots(pidx)
    first_is_a = loc1 == a_loc
    wa = jnp.where(first_is_a, p1, p2) * w_grp
    wb = jnp.where(first_is_a, p2, p1) * w_grp
    return jnp.where(lane == ROUTE_CLS, cls,
                     jnp.where(lane == ROUTE_WA, wa, jnp.where(lane == ROUTE_WB, wb, 0.0)))


def _store_token_tiles(ref, x):
    rows, d = x.shape
    assert d == SUBLANES * LANES
    for s in range(SUBLANES):
        ref[pl.ds(s, rows, stride=SUBLANES), :] = x[:, s * LANES:(s + 1) * LANES]


def _load_token_tiles(ref):
    rows = ref.shape[0] // SUBLANES
    return jnp.concatenate([ref[pl.ds(s, rows, stride=SUBLANES), :] for s in range(SUBLANES)],
                           axis=1)


def _merge_kernel(x_ref, ysb_ref, yhg_ref, gsb_ref, ghg_ref, wbs_ref, wbh_ref, wo_ref,
                  ln_ref, wr_ref, br_ref, x1_ref, t_ref, route_ref):
    a = jnp.dot(ysb_ref[...], wbs_ref[...], preferred_element_type=F32)
    b = jnp.dot(yhg_ref[...], wbh_ref[...], preferred_element_type=F32)
    merged = jax.nn.sigmoid(gsb_ref[...]) * a + jax.nn.sigmoid(ghg_ref[...]) * b
    x1 = x_ref[...] + jnp.dot(merged.astype(BF16), wo_ref[...], preferred_element_type=F32)
    x1_ref[...] = x1
    var = jnp.mean(x1 * x1, axis=-1, keepdims=True)
    t = x1 * lax.rsqrt(var + EPS) * ln_ref[...]
    _store_token_tiles(t_ref, t)
    t_hi = t.astype(BF16)
    t_lo = (t - t_hi.astype(F32)).astype(BF16)
    p_hi = jnp.dot(t_hi, wr_ref[...], preferred_element_type=F32)
    p_lo = jnp.dot(t_lo, wr_ref[...], preferred_element_type=F32)
    lg = (p_hi[:, :LANES] + p_hi[:, LANES:]) + (p_lo[:, :LANES] + p_lo[:, LANES:]) + br_ref[...]
    route_ref[...] = _route(lg)


def _merge(x2, y_sb, y_hg, rest, wbs, wbh, wo, ln_g, wr, br):
    n, d = x2.shape
    tm = min(MERGE_TM, n)
    w_sb, w_hg = y_sb.shape[1], y_hg.shape[1]
    gate_blk = (rest.shape[1] - 2 * d) // d
    row = lambda wdt, j=0: pl.BlockSpec((tm, wdt), lambda i, j=j: (i, j))
    const = lambda shape: pl.BlockSpec(shape, lambda i: (0, 0))
    return pl.pallas_call(
        _merge_kernel,
        grid=(n // tm,),
        in_specs=[row(d), row(w_sb), row(w_hg), row(d, gate_blk), row(d, gate_blk + 1),
                  const(wbs.shape), const(wbh.shape), const(wo.shape), const((1, d)),
                  const(wr.shape), const((1, LANES))],
        out_specs=[row(d), pl.BlockSpec((tm * SUBLANES, LANES), lambda i: (i, 0)), row(LANES)],
        out_shape=[jax.ShapeDtypeStruct((n, d), F32),
                   jax.ShapeDtypeStruct((n * SUBLANES, LANES), F32),
                   jax.ShapeDtypeStruct((n, LANES), F32)],
        compiler_params=_params(("parallel",)),
        name="merge",
    )(x2, y_sb, y_hg, rest, rest, wbs, wbh, wo, ln_g.reshape(1, d), wr, br)


TOKEN_ROWS_OUT = 2 * SUBLANES


def _plan_kernel(route_ref, ltri_ref, utri_ref, pos_ref, te_ref, meta_ref,
                 tot_scr, run_scr, *, tile_rows):
    ph = pl.program_id(0)
    i = pl.program_id(1)
    route = route_ref[...]
    tm = route.shape[0]
    lane = lax.broadcasted_iota(jnp.int32, (tm, LANES), 1)
    is_cls = lane.astype(F32) == route[:, ROUTE_CLS:ROUTE_CLS + 1]
    sel = jnp.where(is_cls, 1.0, 0.0)

    @pl.when(jnp.logical_and(ph == 0, i == 0))
    def _():
        tot_scr[...] = jnp.zeros_like(tot_scr)

    @pl.when(ph == 0)
    def _():
        tot_scr[...] += jnp.sum(sel, axis=0, keepdims=True)

    @pl.when(ph == 1)
    def _():
        tot = tot_scr[...]
        top = tot + (tile_rows - 1.0)
        tiles = jnp.floor(top * (1.0 / tile_rows))
        tiles = jnp.where(tiles * tile_rows > top, tiles - 1.0, tiles)
        tiles = jnp.where((tiles + 1.0) * tile_rows <= top, tiles + 1.0, tiles)
        first_tile = _split_dot(jnp.broadcast_to(tiles, (SUBLANES, LANES)), utri_ref[...], 3)[0:1]

        @pl.when(i == 0)
        def _():
            run_scr[...] = jnp.zeros_like(run_scr)
            lane1 = lax.broadcasted_iota(jnp.int32, (1, LANES), 1)
            is_c = lane1 < N_CLASSES
            end_tile = first_tile + tiles
            n_valid = jnp.sum(jnp.where(is_c, tiles, 0.0), axis=1, keepdims=True)
            rows = te_ref.shape[0]
            j = lax.broadcasted_iota(jnp.int32, (rows, LANES), 0).astype(F32)
            j = jnp.minimum(j, n_valid - 1.0)
            lane2 = lax.broadcasted_iota(jnp.int32, (rows, LANES), 1)
            done = jnp.where(jnp.logical_and(lane2 < N_CLASSES, end_tile <= j), 1.0, 0.0)
            tc = jnp.sum(done, axis=1, keepdims=True)
            tg = jnp.floor((tc + 0.5) * (1.0 / N_PAIRS))
            ta, tb = _pair_slots(tc - tg * N_PAIRS)
            te = jnp.where(lane2 == 0, tg * EXPERTS_PER_GROUP + ta, tg * EXPERTS_PER_GROUP + tb)
            te_ref[...] = te.astype(jnp.int32)
            last_tile = jnp.where(jnp.logical_and(is_c, tiles > 0), end_tile - 1.0, -1.0)
            meta = jnp.where(lane1 == N_CLASSES, n_valid, last_tile)
            meta_ref[...] = jnp.broadcast_to(meta, meta_ref.shape).astype(jnp.int32)

        rank = jnp.dot(ltri_ref[...], sel.astype(BF16), preferred_element_type=F32) + run_scr[...]
        dest = jnp.sum(jnp.where(is_cls, first_tile * tile_rows + rank, 0.0),
                       axis=1, keepdims=True)
        pos_ref[...] = jnp.where(lane == 0, dest, 0.0).astype(jnp.int32)
        run_scr[...] += jnp.sum(sel, axis=0, keepdims=True)


def _plan(route, n_tiles_max):
    n = route.shape[0]
    tm = min(PLAN_TM, n)
    r = np.arange(tm)
    ltri = jnp.asarray((r[None, :] < r[:, None]).astype(np.float32), dtype=BF16)
    e = np.arange(LANES)
    utri = jnp.asarray((e[:, None] < e[None, :]).astype(np.float32), dtype=BF16)
    te_rows = -(-n_tiles_max // SUBLANES) * SUBLANES
    kern = functools.partial(_plan_kernel, tile_rows=MOE_TM)
    return pl.pallas_call(
        kern,
        grid=(2, n // tm),
        in_specs=[pl.BlockSpec((tm, LANES), lambda p, i: (i, 0)),
                  pl.BlockSpec((tm, tm), lambda p, i: (0, 0)),
                  pl.BlockSpec((LANES, LANES), lambda p, i: (0, 0))],
        out_specs=[pl.BlockSpec((tm, LANES), lambda p, i: (i * p, 0)),
                   pl.BlockSpec((te_rows, LANES), lambda p, i: (0, 0)),
                   pl.BlockSpec((SUBLANES, LANES), lambda p, i: (0, 0))],
        out_shape=[jax.ShapeDtypeStruct((n, LANES), jnp.int32),
                   jax.ShapeDtypeStruct((te_rows, LANES), jnp.int32),
                   jax.ShapeDtypeStruct((SUBLANES, LANES), jnp.int32)],
        scratch_shapes=[pltpu.VMEM((1, LANES), F32), pltpu.VMEM((1, LANES), F32)],
        compiler_params=_params(("arbitrary", "arbitrary")),
        name="plan",
    )(route, ltri, utri)


DMA_UNROLL = 8


def _token_copy(src_ref, r, dst_ref, p, sem, rows):
    return pltpu.make_async_copy(src_ref.at[pl.ds(pl.multiple_of(r * rows, rows), rows)],
                                 dst_ref.at[pl.ds(pl.multiple_of(p * rows, rows), rows)], sem)


def _dispatch_kernel(pos_ref, meta_ref, t_ref, xs_ref, zero_scr, sem, *, tm, tile_rows):
    i = pl.program_id(0)
    tile_rows = tile_rows * SUBLANES
    n_tiles = xs_ref.shape[0] // tile_rows

    @pl.when(i == 0)
    def _():
        zero_scr[...] = jnp.zeros_like(zero_scr)
        n_valid = meta_ref[N_CLASSES]
        clears = [(meta_ref[c], meta_ref[c] >= 0) for c in range(N_CLASSES)]
        clears += [(n_tiles - 1 - k, n_tiles - 1 - k >= n_valid) for k in range(N_CLASSES)]

        def clear(tile):
            return pltpu.make_async_copy(
                zero_scr, xs_ref.at[pl.ds(tile * tile_rows, tile_rows)], sem)

        for tile, cond in clears:
            @pl.when(cond)
            def _():
                clear(tile).start()
        for tile, cond in clears:
            @pl.when(cond)
            def _():
                clear(tile).wait()

    base = i * tm

    def start(r2, carry):
        for k in range(2):
            r = 2 * r2 + k
            _token_copy(t_ref, r, xs_ref, pos_ref[base + r], sem, SUBLANES).start(priority=k)
        return carry

    def wait(r, carry):
        _token_copy(t_ref, r, xs_ref, pos_ref[base + r], sem, SUBLANES).wait()
        return carry

    lax.fori_loop(0, tm // 2, start, 0, unroll=DMA_UNROLL)
    lax.fori_loop(0, tm, wait, 0, unroll=DMA_UNROLL)


def _dispatch(pos, meta, t, n_rows):
    n = t.shape[0] // SUBLANES
    tm = min(DISPATCH_TM, n)
    kern = functools.partial(_dispatch_kernel, tm=tm, tile_rows=MOE_TM)
    grid_spec = pltpu.PrefetchScalarGridSpec(
        num_scalar_prefetch=2,
        grid=(n // tm,),
        in_specs=[pl.BlockSpec((tm * SUBLANES, LANES), lambda i, *_: (i, 0))],
        out_specs=pl.BlockSpec(memory_space=pl.ANY),
        scratch_shapes=[pltpu.VMEM((MOE_TM * SUBLANES, LANES), F32), pltpu.SemaphoreType.DMA(())],
    )
    return pl.pallas_call(
        kern,
        grid_spec=grid_spec,
        out_shape=jax.ShapeDtypeStruct((n_rows * SUBLANES, LANES), F32),
        compiler_params=_params(("arbitrary",)),
        name="dispatch",
    )(pos, meta, t)


def _expert_kernel(tea_ref, teb_ref, nv_ref, xs_ref, wga, wua, wda, wgb, wub, wdb, ys_ref):
    in_use = pl.program_id(0) < nv_ref[0]
    rows = xs_ref.shape[0] // SUBLANES

    @pl.when(in_use)
    def _():
        x = _load_token_tiles(xs_ref).astype(BF16)
        for slot, (wg, wu, wd) in enumerate(((wga, wua, wda), (wgb, wub, wdb))):
            hg = jnp.dot(x, wg[...].astype(BF16), preferred_element_type=F32)
            hu = jnp.dot(x, wu[...].astype(BF16), preferred_element_type=F32)
            a = (hg * jax.nn.sigmoid(hg) * hu).astype(BF16)
            y = jnp.dot(a, wd[...].astype(BF16), preferred_element_type=F32)
            for s in range(SUBLANES):
                ys_ref[pl.ds(slot * SUBLANES + s, rows, stride=TOKEN_ROWS_OUT), :] = (
                    y[:, s * LANES:(s + 1) * LANES])

    @pl.when(jnp.logical_not(in_use))
    def _():
        ys_ref[...] = jnp.zeros_like(ys_ref)


def _experts(tea, teb, nv, xs, wg, wu, wd):
    _, d, de = wg.shape
    assert d == SUBLANES * LANES
    n_tiles = xs.shape[0] // (MOE_TM * SUBLANES)
    wa = lambda shape: pl.BlockSpec(shape, lambda j, tea, teb, nv: (tea[j], 0, 0))
    wb = lambda shape: pl.BlockSpec(shape, lambda j, tea, teb, nv: (teb[j], 0, 0))
    grid_spec = pltpu.PrefetchScalarGridSpec(
        num_scalar_prefetch=3,
        grid=(n_tiles,),
        in_specs=[pl.BlockSpec((MOE_TM * SUBLANES, LANES),
                               lambda j, tea, teb, nv: (jnp.minimum(j, nv[0] - 1), 0)),
                  wa((None, d, de)), wa((None, d, de)), wa((None, de, d)),
                  wb((None, d, de)), wb((None, d, de)), wb((None, de, d))],
        out_specs=pl.BlockSpec((MOE_TM * TOKEN_ROWS_OUT, LANES), lambda j, *_: (j, 0)),
    )
    return pl.pallas_call(
        _expert_kernel,
        grid_spec=grid_spec,
        out_shape=jax.ShapeDtypeStruct((n_tiles * MOE_TM * TOKEN_ROWS_OUT, LANES), F32),
        compiler_params=_params(("arbitrary",)),
        name="experts",
    )(tea, teb, nv, xs, wg, wu, wd, wg, wu, wd)


def _combine_kernel(pos_ref, ys_ref, x1_ref, route_ref, fg_ref, o_ref, g, sem, *, tm):
    i = pl.program_id(0)
    slot = i % 2

    def gather(step, buf, wait):
        base = step * tm

        def body(r2, carry):
            for k in range(2):
                r = 2 * r2 + k
                c = _token_copy(ys_ref, pos_ref[base + r], g.at[buf], r, sem.at[buf],
                                TOKEN_ROWS_OUT)
                if wait:
                    c.wait()
                else:
                    c.start(priority=k)
            return carry

        lax.fori_loop(0, tm // 2, body, 0, unroll=DMA_UNROLL)

    @pl.when(i == 0)
    def _():
        gather(0, 0, wait=False)

    @pl.when(i + 1 < pl.num_programs(0))
    def _():
        gather(i + 1, 1 - slot, wait=False)

    gather(i, slot, wait=True)
    gs = g.at[slot]
    ya, yb = (jnp.concatenate([gs[pl.ds(off + s, tm, stride=TOKEN_ROWS_OUT), :]
                               for s in range(SUBLANES)], axis=1) for off in (0, SUBLANES))
    route = route_ref[...]
    x2 = x1_ref[...] + (route[:, ROUTE_WA:ROUTE_WA + 1] * ya + route[:, ROUTE_WB:ROUTE_WB + 1] * yb)
    var = jnp.mean(x2 * x2, axis=-1, keepdims=True)
    o_ref[...] = x2 * lax.rsqrt(var + EPS) * fg_ref[...]


def _combine(pos, ys, x1, route, final_g):
    n, d = x1.shape
    tm = min(COMBINE_TM, n)
    kern = functools.partial(_combine_kernel, tm=tm)
    grid_spec = pltpu.PrefetchScalarGridSpec(
        num_scalar_prefetch=1,
        grid=(n // tm,),
        in_specs=[pl.BlockSpec(memory_space=pl.ANY),
                  pl.BlockSpec((tm, d), lambda i, *_: (i, 0)),
                  pl.BlockSpec((tm, LANES), lambda i, *_: (i, 0)),
                  pl.BlockSpec((1, d), lambda i, *_: (0, 0))],
        out_specs=pl.BlockSpec((tm, d), lambda i, *_: (i, 0)),
        scratch_shapes=[pltpu.VMEM((2, tm * TOKEN_ROWS_OUT, LANES), F32),
                        pltpu.SemaphoreType.DMA((2,))],
    )
    return pl.pallas_call(
        kern,
        grid_spec=grid_spec,
        out_shape=jax.ShapeDtypeStruct((n, d), F32),
        compiler_params=_params(("arbitrary",)),
        name="combine",
    )(pos, ys, x1, route, final_g.reshape(1, d))


def _moe_sparse(t, route, wg, wu, wd, x1, final_g):
    n = x1.shape[0]
    n_tiles_max = (n + N_CLASSES * (MOE_TM - 1)) // MOE_TM
    pos, te, meta = _plan(route, n_tiles_max)
    pos = pos[:, 0]
    xs = _dispatch(pos, meta[0, :N_CLASSES + 1], t, n_tiles_max * MOE_TM)
    ys = _experts(te[:n_tiles_max, 0], te[:n_tiles_max, 1], meta[0, N_CLASSES:N_CLASSES + 1],
                  xs, wg, wu, wd)
    return _combine(pos, ys, x1, route, final_g)


def _suffix_ones(t):
    j = np.arange(t)[:, None]
    s = np.arange(t)[None, :]
    return jnp.asarray((j >= s).astype(np.float32), dtype=BF16)


def _causal_bias(t):
    row = np.arange(t)[:, None]
    col = np.arange(t)[None, :]
    diag = np.where(col < row, 0.0, MASK_BIAS).astype(np.float32)
    return jnp.asarray(np.stack([np.zeros_like(diag), diag]))


def _chunk_prefix_ones(tt, c):
    t = np.arange(tt)[:, None]
    j = np.arange(tt)[None, :]
    return jnp.asarray(((j <= t) & (t // c == j // c)).astype(np.float32), dtype=BF16)


def _block_diag_ones(w, blk):
    a = np.arange(w)
    return jnp.asarray((a[:, None] // blk == a[None, :] // blk).astype(np.float32), dtype=BF16)


def kernel(x, ln1_g, w_in, w_branch_sb, w_branch_hg, hg_norm_g, hg_lb_logits, w_out, ln2_g,
           w_router_group, b_router_group, w_router_expert, b_router_expert,
           w_exp_gate, w_exp_up, w_exp_down, final_g):
    bsz, seq, d = x.shape
    depth = w_in.shape[0]
    n = bsz * seq
    sb_width = SB_HEADS * SB_HEAD_DIM
    hg_width = HG_HEADS * HG_DIM

    tri_attn = _suffix_ones(min(ATTN_T, seq))
    bias_attn = _causal_bias(min(ATTN_T, seq))
    tt = min(HGRN_TT, seq)
    tri_hg = _chunk_prefix_ones(tt, min(HGRN_C, tt))
    bd = _block_diag_ones(min(MXU_DIM, hg_width), HG_DIM)

    x2 = x.reshape(n, d)
    for l in range(depth):
        qkv, rest = _inproj(x2, ln1_g[l], w_in[l], sb_width)
        y_sb = _attn(qkv, bsz, seq, tri_attn, bias_attn)
        y_hg = _hgrn(rest, hg_lb_logits, hg_norm_g[l], bsz, seq, l, tri_hg, bd)

        pad = LANES - N_EXPERTS - N_GROUPS
        wr = jnp.concatenate([w_router_expert[l], w_router_group[l],
                              jnp.zeros((d, pad), F32)], axis=1)
        wr_hi = wr.astype(BF16)
        wr_lo = (wr - wr_hi.astype(F32)).astype(BF16)
        wr_split = jnp.concatenate([wr_hi, wr_lo], axis=1)
        br = jnp.concatenate([b_router_expert[l], b_router_group[l],
                              jnp.zeros((pad,), F32)]).reshape(1, LANES)

        last = l == depth - 1
        x1, t, route = _merge(x2, y_sb, y_hg, rest, w_branch_sb[l].astype(BF16),
                              w_branch_hg[l].astype(BF16), w_out[l].astype(BF16), ln2_g[l],
                              wr_split, br)
        assert last, "final rmsnorm is fused into the last layer's combine kernel"
        x2 = _moe_sparse(t, route, w_exp_gate[l], w_exp_up[l], w_exp_down[l], x1, final_g)
    return x2.reshape(bsz, seq, d)
```

```python
import functools

import jax
import jax.numpy as jnp
import numpy as np
from jax import lax
from jax.experimental import pallas as pl
from jax.experimental.pallas import tpu as pltpu

F32 = jnp.float32
BF16 = jnp.bfloat16

EPS = 1e-6
SB_HEADS = 8
SB_HEAD_DIM = 64
HG_HEADS = 8
HG_DIM = 64
N_GROUPS = 4
EXPERTS_PER_GROUP = 4
N_EXPERTS = N_GROUPS * EXPERTS_PER_GROUP

LANES = 128
SUBLANES = 8
MXU_DIM = 256
LOG2E = 1.4426950408889634
VMEM_LIMIT = 48 * 1024 * 1024

INPROJ_TM = 2048
INPROJ_TN = 512
ATTN_T = MXU_DIM
HGRN_TT = 256
HGRN_C = 32
MERGE_TM = 512
MOE_TM = 384
PLAN_TM = 1024
DISPATCH_TM = 1024
COMBINE_TM = 512


def _params(sem):
    return pltpu.CompilerParams(dimension_semantics=sem, vmem_limit_bytes=VMEM_LIMIT)


def _split_dot(x, m, passes):
    acc = None
    r = x
    for p in range(passes):
        h = r.astype(BF16)
        term = jnp.dot(h, m, preferred_element_type=F32)
        acc = term if acc is None else acc + term
        if p + 1 < passes:
            r = r - h.astype(F32)
    return acc


def _inproj_kernel(x_ref, g_ref, w_ref, qkv_ref, rest_ref, h_scr, *, q_scale):
    j = pl.program_id(1)

    @pl.when(j == 0)
    def _():
        x = x_ref[...]
        var = jnp.mean(x * x, axis=-1, keepdims=True)
        h_scr[...] = (x * lax.rsqrt(var + EPS) * g_ref[...]).astype(BF16)

    acc = jnp.dot(h_scr[...], w_ref[...].astype(BF16), preferred_element_type=F32)
    qkv_ref[...] = (acc * jnp.where(j == 0, q_scale, 1.0)).astype(BF16)
    rest_ref[...] = acc


def _inproj(x2, ln_g, w_in, sb_width):
    n, d = x2.shape
    cols = w_in.shape[1]
    tm, tn = min(INPROJ_TM, n), INPROJ_TN
    assert sb_width == tn, "q block must be exactly one column tile"
    nq = 3 * sb_width // tn
    nj = cols // tn
    kern = functools.partial(_inproj_kernel, q_scale=SB_HEAD_DIM ** -0.5)
    return pl.pallas_call(
        kern,
        grid=(n // tm, nj),
        in_specs=[
            pl.BlockSpec((tm, d), lambda i, j: (i, 0)),
            pl.BlockSpec((1, d), lambda i, j: (0, 0)),
            pl.BlockSpec((d, tn), lambda i, j: (0, j)),
        ],
        out_specs=[
            pl.BlockSpec((tm, tn), lambda i, j: (i, jnp.minimum(j, nq))),
            pl.BlockSpec((tm, tn), lambda i, j: (i, jnp.maximum(j - nq, 0))),
        ],
        out_shape=[
            jax.ShapeDtypeStruct((n, (nq + 1) * tn), BF16),
            jax.ShapeDtypeStruct((n, cols - nq * tn), F32),
        ],
        scratch_shapes=[pltpu.VMEM((tm, d), BF16)],
        compiler_params=_params(("parallel", "arbitrary")),
        name="inproj",
    )(x2, ln_g.reshape(1, d), w_in)


ATTN_STAGES = 3
ATTN_STREAMS = 2
MASK_BIAS = -1e30
ATTN_SKIP = 111.0


def _attn_kernel(q_ref, k_ref, v_ref, tri_ref, bias_ref, o_ref,
                 z0, z1, z2, i0, i1, i2, acc_ref, c_ref, t_ref, *, t, nq):
    zbuf = (z0, z1, z2)
    ibuf = (i0, i1, i2)
    for r in zbuf + ibuf + (acc_ref, c_ref, t_ref):
        r[...] = jnp.zeros_like(r)

    lane = lax.broadcasted_iota(jnp.int32, (t, LANES), 1)
    head0 = lane < SB_HEAD_DIM
    nt = (((1,), (1,)), ((), ()))

    def stage_a(qi, kj, slot):
        q = q_ref[pl.ds(pl.multiple_of(qi * t, t), t), :]
        zero = jnp.zeros_like(q)
        q2 = jnp.concatenate([jnp.where(head0, q, zero), jnp.where(head0, zero, q)], axis=0)
        k = k_ref[pl.ds(pl.multiple_of(kj * t, t), t), :]
        z = lax.dot_general(q2, k, nt, preferred_element_type=F32)
        bias = bias_ref[(qi == kj).astype(jnp.int32)]
        zbuf[slot][...] = z + jnp.concatenate([bias, bias], axis=0)

    def stage_b(slot):
        z = zbuf[slot][...]
        p = jnp.maximum(z, 0.0) + jnp.log(1.0 + jnp.exp2(jnp.abs(z) * (-LOG2E)))
        zbuf[slot][...] = z - p
        later = jnp.dot(p.astype(BF16), tri_ref[...], preferred_element_type=F32)
        ibuf[slot][...] = later
        total = later[:, 0:1] + p[:, 0:1]
        t_ref[slot] = total
        return total

    def stage_c(qi, kj, slot, stream):
        first = qi == kj
        c = jnp.where(first, 0.0, c_ref[stream])
        a = jnp.exp2((zbuf[slot][...] - ibuf[slot][...] - c) * LOG2E)
        v = v_ref[pl.ds(pl.multiple_of(kj * t, t), t), :]
        pv = jnp.dot(a.astype(BF16), v, preferred_element_type=F32)
        acc = jnp.where(first, pv, acc_ref[stream] + pv)
        acc_ref[stream] = acc
        c_ref[stream] = c + t_ref[slot]
        o_ref[pl.ds(pl.multiple_of(qi * t, t), t), :] = (
            jnp.where(head0, acc[0:t], acc[t:2 * t]).astype(o_ref.dtype))

    def block(carry):
        qs, ks, qn, kn, skip_q, drained = carry
        qs, ks, qn, kn, skip_q = list(qs), list(ks), list(qn), list(kn), list(skip_q)
        all_done = qn[0] >= nq
        for p in range(1, ATTN_STREAMS):
            all_done = jnp.logical_and(all_done, qn[p] >= nq)
        drained = drained + all_done.astype(jnp.int32)
        for r in range(ATTN_STAGES * ATTN_STREAMS):
            sa, sc, sb = r % ATTN_STAGES, (r + 1) % ATTN_STAGES, (r + 2) % ATTN_STAGES
            pa, pb = r % ATTN_STREAMS, (r - 1) % ATTN_STREAMS
            stage_c(qs[sc], ks[sc], sc, pa)
            total_b = stage_b(sb)
            carry_b = jnp.where(qs[sb] == ks[sb], 0.0, c_ref[pb]) + total_b
            skip = skip_q[pa] == qn[pa]
            q_cur = jnp.where(skip, qn[pa] + ATTN_STREAMS, qn[pa])
            k_cur = jnp.where(skip, qn[pa] + ATTN_STREAMS, kn[pa])
            drain = q_cur >= nq
            qa = jnp.where(drain, 0, q_cur)
            ka = jnp.where(drain, 0, k_cur)
            stage_a(qa, ka, sa)
            last = k_cur == 0
            qn[pa] = jnp.where(jnp.logical_and(last, jnp.logical_not(drain)),
                               q_cur + ATTN_STREAMS, q_cur)
            kn[pa] = jnp.where(drain, k_cur, jnp.where(last, q_cur + ATTN_STREAMS, k_cur - 1))
            skip_q[pb] = jnp.where(jnp.min(carry_b) >= ATTN_SKIP, qs[sb], skip_q[pb])
            qs[sa], ks[sa] = qa, ka
        return tuple(qs), tuple(ks), tuple(qn), tuple(kn), tuple(skip_q), drained

    zero = jnp.int32(0)
    first_q = tuple(jnp.int32(p) for p in range(ATTN_STREAMS))
    init = ((zero,) * ATTN_STAGES, (zero,) * ATTN_STAGES, first_q, first_q,
            (jnp.int32(-1),) * ATTN_STREAMS, zero)
    lax.while_loop(lambda carry: carry[5] < 1, block, init)


def _attn(qkv, bsz, seq, tri, bias):
    n = bsz * seq
    t = min(ATTN_T, seq)
    pairs = SB_HEADS * SB_HEAD_DIM // LANES
    kern = functools.partial(_attn_kernel, t=t, nq=seq // t)
    return pl.pallas_call(
        kern,
        grid=(bsz, pairs),
        in_specs=[
            pl.BlockSpec((seq, LANES), lambda b, p: (b, p)),
            pl.BlockSpec((seq, LANES), lambda b, p: (b, pairs + p)),
            pl.BlockSpec((seq, LANES), lambda b, p: (b, 2 * pairs + p)),
            pl.BlockSpec((t, t), lambda b, p: (0, 0)),
            pl.BlockSpec((2, t, t), lambda b, p: (0, 0, 0)),
        ],
        out_specs=pl.BlockSpec((seq, LANES), lambda b, p: (b, p)),
        out_shape=jax.ShapeDtypeStruct((n, pairs * LANES), BF16),
        scratch_shapes=([pltpu.VMEM((2 * t, t), F32)] * (2 * ATTN_STAGES)
                        + [pltpu.VMEM((ATTN_STREAMS, 2 * t, LANES), F32),
                           pltpu.VMEM((ATTN_STREAMS, 2 * t, 1), F32),
                           pltpu.VMEM((ATTN_STAGES, 2 * t, 1), F32)]),
        compiler_params=_params(("parallel", "parallel")),
        name="attn",
    )(qkv, qkv, qkv, tri, bias)


def _group_dot(x, bd):
    g = bd.shape[0]
    parts = [jnp.dot(x[:, i:i + g], bd, preferred_element_type=F32)
             for i in range(0, x.shape[1], g)]
    return jnp.concatenate(parts, axis=1)


def _hgrn_kernel(q_ref, f_ref, i_ref, g_ref, lbl_ref, ng_ref, tri_ref, bd_ref, o_ref,
                 st_scr, wpad, vpad, wsh, vsh, *, tt, c, layer):
    ti = pl.program_id(1)
    w = q_ref.shape[1]
    nch = tt // c

    @pl.when(ti == 0)
    def _():
        st_scr[...] = jnp.zeros_like(st_scr)

    lg = lbl_ref[...]
    e = jnp.exp(lg - jnp.max(lg, axis=0, keepdims=True))
    lb = jnp.sum(e[0:layer + 1], axis=0, keepdims=True) / jnp.sum(e, axis=0, keepdims=True)

    f = lb + (1.0 - lb) * jax.nn.sigmoid(f_ref[...])
    kk = 1.0 - f
    qv = q_ref[...]
    qs = qv * jax.nn.sigmoid(qv)
    v = i_ref[...]
    bd = bd_ref[...]
    cum = _split_dot_left(tri_ref[...], jnp.log(f) * LOG2E, 3)
    wk = jnp.log(kk) * LOG2E - cum

    pos = lax.broadcasted_iota(jnp.int32, (tt, 1), 0) % c
    wpad[0:SUBLANES, :] = jnp.zeros((SUBLANES, w), F32)
    vpad[0:SUBLANES, :] = jnp.zeros((SUBLANES, w), F32)
    wpad[SUBLANES:SUBLANES + tt, :] = wk
    vpad[SUBLANES:SUBLANES + tt, :] = v
    for b in range(SUBLANES):
        wb = wpad[SUBLANES - b:SUBLANES - b + tt, :]
        vb = vpad[SUBLANES - b:SUBLANES - b + tt, :]
        if b:
            wb = jnp.where(pos >= b, wb, -jnp.inf)
        wsh[b] = wb.reshape(nch, c, w)
        vsh[b] = vb.reshape(nch, c, w)

    qs3 = qs.reshape(nch, c, w)
    cum3 = cum.reshape(nch, c, w)
    acc = None
    for a in reversed(range(c // SUBLANES)):
        rows = c - SUBLANES * a
        qa = qs3[:, SUBLANES * a:, :].reshape(nch * rows, w)
        ca = cum3[:, SUBLANES * a:, :].reshape(nch * rows, w)
        acc_a = None
        for b in range(SUBLANES):
            wb = wsh[b, :, 0:rows, :].reshape(nch * rows, w)
            vb = vsh[b, :, 0:rows, :].reshape(nch * rows, w)
            dd = qa * jnp.exp2(ca + wb)
            term = _group_dot(dd.astype(BF16), bd) * vb
            acc_a = term if acc_a is None else acc_a + term
        acc_a = acc_a.reshape(nch, rows, w)
        if acc is not None:
            acc_a = acc_a + jnp.concatenate([jnp.zeros((nch, SUBLANES, w), F32), acc], axis=1)
        acc = acc_a
    acc = acc.reshape(tt, w)

    grp = st_scr.shape[1]
    bdmask = bd[0:grp, 0:grp] != 0
    outs = []
    for ci in range(nch):
        r0 = ci * c
        cum_c = cum[r0:r0 + c]
        last = cum_c[c - 1:c]
        qd = (qs[r0:r0 + c] * jnp.exp2(cum_c)).astype(BF16)
        kd = (kk[r0:r0 + c] * jnp.exp2(last - cum_c)).astype(BF16)
        vc = v[r0:r0 + c].astype(BF16)
        dec = jnp.exp2(last)
        o_parts = []
        for gi in range(w // grp):
            sl = slice(gi * grp, (gi + 1) * grp)
            st = st_scr[gi]
            o_parts.append(lax.dot_general(qd[:, sl], st.astype(BF16), (((1,), (1,)), ((), ())),
                                           preferred_element_type=F32))
            upd = lax.dot_general(vc[:, sl], kd[:, sl], (((0,), (0,)), ((), ())),
                                  preferred_element_type=F32)
            st_scr[gi] = st * dec[:, sl] + jnp.where(bdmask, upd, 0.0)
        outs.append(jnp.concatenate(o_parts, axis=1))
    o = acc + jnp.concatenate(outs, axis=0)

    o2 = o * o
    o2_hi = o2.astype(BF16)
    o2_lo = (o2 - o2_hi.astype(F32)).astype(BF16)
    ms = (_group_dot(o2_hi, bd) + _group_dot(o2_lo, bd)) * (1.0 / HG_DIM)
    gv = g_ref[...]
    o = o * lax.rsqrt(ms + EPS) * ng_ref[...] * (gv * jax.nn.sigmoid(gv))
    o_ref[...] = o.astype(o_ref.dtype)


def _split_dot_left(m, x, passes):
    acc = None
    r = x
    for p in range(passes):
        h = r.astype(BF16)
        term = jnp.dot(m, h, preferred_element_type=F32)
        acc = term if acc is None else acc + term
        if p + 1 < passes:
            r = r - h.astype(F32)
    return acc


def _hgrn(rest, lb_logits, norm_g, bsz, seq, layer, tri, bd):
    n = bsz * seq
    w = HG_HEADS * HG_DIM
    grp = bd.shape[0]
    tt = min(HGRN_TT, seq)
    c = min(HGRN_C, tt)
    nt = seq // tt
    kern = functools.partial(_hgrn_kernel, tt=tt, c=c, layer=layer)
    col = lambda j: pl.BlockSpec((tt, w), lambda b, i, j=j: (b * nt + i, j))
    const = lambda shape: pl.BlockSpec(shape, lambda b, i: (0, 0))
    return pl.pallas_call(
        kern,
        grid=(bsz, nt),
        in_specs=[col(0), col(1), col(2), col(3),
                  const(lb_logits.shape), const((1, w)), const((tt, tt)), const((grp, grp))],
        out_specs=pl.BlockSpec((tt, w), lambda b, i: (b * nt + i, 0)),
        out_shape=jax.ShapeDtypeStruct((n, w), BF16),
        scratch_shapes=([pltpu.VMEM((w // LANES, LANES, LANES), F32)]
                        + [pltpu.VMEM((SUBLANES + tt, w), F32)] * 2
                        + [pltpu.VMEM((SUBLANES, tt // c, c, w), F32)] * 2),
        compiler_params=_params(("parallel", "arbitrary")),
        name="hgrn",
    )(rest, rest, rest, rest, lb_logits, norm_g.reshape(1, w), tri, bd)


PAIR_A = (0, 0, 0, 1, 1, 3)
PAIR_B = (1, 2, 3, 3, 2, 2)
N_PAIRS = len(PAIR_A)
N_CLASSES = N_GROUPS * N_PAIRS
assert sorted(tuple(sorted(p)) for p in zip(PAIR_A, PAIR_B)) == [
    (a, b) for a in range(EXPERTS_PER_GROUP) for b in range(a + 1, EXPERTS_PER_GROUP)]
ROUTE_CLS, ROUTE_WA, ROUTE_WB = 0, 1, 2


def _pair_slots(pidx):
    a = b = jnp.zeros_like(pidx)
    for p in range(N_PAIRS):
        a = jnp.where(pidx == p, float(PAIR_A[p]), a)
        b = jnp.where(pidx == p, float(PAIR_B[p]), b)
    return a, b


def _pair_index(lo, hi):
    pidx = jnp.zeros_like(lo)
    for p in range(N_PAIRS):
        is_p = jnp.logical_and(lo == min(PAIR_A[p], PAIR_B[p]), hi == max(PAIR_A[p], PAIR_B[p]))
        pidx = jnp.where(is_p, float(p), pidx)
    return pidx


def _route(lg):
    lane = lax.broadcasted_iota(jnp.int32, lg.shape, 1)
    neg = jnp.float32(-jnp.inf)
    big = jnp.int32(LANES)
    gmask = jnp.logical_and(lane >= N_EXPERTS, lane < N_EXPERTS + N_GROUPS)
    gl = jnp.where(gmask, lg, neg)
    gmax = jnp.max(gl, axis=1, keepdims=True)
    gidx = jnp.min(jnp.where(gl == gmax, lane, big), axis=1, keepdims=True) - N_EXPERTS
    w_grp = 1.0 / jnp.sum(jnp.where(gmask, jnp.exp(gl - gmax), 0.0), axis=1, keepdims=True)
    in_grp = jnp.logical_and(lane < N_EXPERTS, lane // EXPERTS_PER_GROUP == gidx)
    l1 = jnp.where(in_grp, lg, neg)
    v1 = jnp.max(l1, axis=1, keepdims=True)
    i1 = jnp.min(jnp.where(l1 == v1, lane, big), axis=1, keepdims=True)
    l2 = jnp.where(jnp.logical_and(in_grp, lane != i1), lg, neg)
    v2 = jnp.max(l2, axis=1, keepdims=True)
    i2 = jnp.min(jnp.where(l2 == v2, lane, big), axis=1, keepdims=True)
    e2 = jnp.exp(v2 - v1)
    p1 = 1.0 / (1.0 + e2)
    p2 = e2 * p1
    loc1 = (i1 - gidx * EXPERTS_PER_GROUP).astype(F32)
    loc2 = (i2 - gidx * EXPERTS_PER_GROUP).astype(F32)
    pidx = _pair_index(jnp.minimum(loc1, loc2), jnp.maximum(loc1, loc2))
    cls = gidx.astype(F32) * N_PAIRS + pidx
    a_loc, _ = _pair_slots(pidx)
    first_is_a = loc1 == a_loc
    wa = jnp.where(first_is_a, p1, p2) * w_grp
    wb = jnp.where(first_is_a, p2, p1) * w_grp
    return jnp.where(lane == ROUTE_CLS, cls,
                     jnp.where(lane == ROUTE_WA, wa, jnp.where(lane == ROUTE_WB, wb, 0.0)))


def _store_token_tiles(ref, x):
    rows, d = x.shape
    assert d == SUBLANES * LANES
    for s in range(SUBLANES):
        ref[pl.ds(s, rows, stride=SUBLANES), :] = x[:, s * LANES:(s + 1) * LANES]


def _load_token_tiles(ref):
    rows = ref.shape[0] // SUBLANES
    return jnp.concatenate([ref[pl.ds(s, rows, stride=SUBLANES), :] for s in range(SUBLANES)],
                           axis=1)


def _merge_kernel(x_ref, ysb_ref, yhg_ref, gsb_ref, ghg_ref, wbs_ref, wbh_ref, wo_ref,
                  ln_ref, wr_ref, br_ref, x1_ref, t_ref, route_ref):
    a = jnp.dot(ysb_ref[...], wbs_ref[...], preferred_element_type=F32)
    b = jnp.dot(yhg_ref[...], wbh_ref[...], preferred_element_type=F32)
    merged = jax.nn.sigmoid(gsb_ref[...]) * a + jax.nn.sigmoid(ghg_ref[...]) * b
    x1 = x_ref[...] + jnp.dot(merged.astype(BF16), wo_ref[...], preferred_element_type=F32)
    x1_ref[...] = x1
    var = jnp.mean(x1 * x1, axis=-1, keepdims=True)
    t = x1 * lax.rsqrt(var + EPS) * ln_ref[...]
    _store_token_tiles(t_ref, t)
    t_hi = t.astype(BF16)
    t_lo = (t - t_hi.astype(F32)).astype(BF16)
    p_hi = jnp.dot(t_hi, wr_ref[...], preferred_element_type=F32)
    p_lo = jnp.dot(t_lo, wr_ref[...], preferred_element_type=F32)
    lg = (p_hi[:, :LANES] + p_hi[:, LANES:]) + (p_lo[:, :LANES] + p_lo[:, LANES:]) + br_ref[...]
    route_ref[...] = _route(lg)


def _merge(x2, y_sb, y_hg, rest, wbs, wbh, wo, ln_g, wr, br):
    n, d = x2.shape
    tm = min(MERGE_TM, n)
    w_sb, w_hg = y_sb.shape[1], y_hg.shape[1]
    gate_blk = (rest.shape[1] - 2 * d) // d
    row = lambda wdt, j=0: pl.BlockSpec((tm, wdt), lambda i, j=j: (i, j))
    const = lambda shape: pl.BlockSpec(shape, lambda i: (0, 0))
    return pl.pallas_call(
        _merge_kernel,
        grid=(n // tm,),
        in_specs=[row(d), row(w_sb), row(w_hg), row(d, gate_blk), row(d, gate_blk + 1),
                  const(wbs.shape), const(wbh.shape), const(wo.shape), const((1, d)),
                  const(wr.shape), const((1, LANES))],
        out_specs=[row(d), pl.BlockSpec((tm * SUBLANES, LANES), lambda i: (i, 0)), row(LANES)],
        out_shape=[jax.ShapeDtypeStruct((n, d), F32),
                   jax.ShapeDtypeStruct((n * SUBLANES, LANES), F32),
                   jax.ShapeDtypeStruct((n, LANES), F32)],
        compiler_params=_params(("parallel",)),
        name="merge",
    )(x2, y_sb, y_hg, rest, rest, wbs, wbh, wo, ln_g.reshape(1, d), wr, br)


TOKEN_ROWS_OUT = 2 * SUBLANES


def _plan_kernel(route_ref, ltri_ref, utri_ref, pos_ref, te_ref, meta_ref,
                 tot_scr, run_scr, *, tile_rows):
    ph = pl.program_id(0)
    i = pl.program_id(1)
    route = route_ref[...]
    tm = route.shape[0]
    lane = lax.broadcasted_iota(jnp.int32, (tm, LANES), 1)
    is_cls = lane.astype(F32) == route[:, ROUTE_CLS:ROUTE_CLS + 1]
    sel = jnp.where(is_cls, 1.0, 0.0)

    @pl.when(jnp.logical_and(ph == 0, i == 0))
    def _():
        tot_scr[...] = jnp.zeros_like(tot_scr)

    @pl.when(ph == 0)
    def _():
        tot_scr[...] += jnp.sum(sel, axis=0, keepdims=True)

    @pl.when(ph == 1)
    def _():
        tot = tot_scr[...]
        top = tot + (tile_rows - 1.0)
        tiles = jnp.floor(top * (1.0 / tile_rows))
        tiles = jnp.where(tiles * tile_rows > top, tiles - 1.0, tiles)
        tiles = jnp.where((tiles + 1.0) * tile_rows <= top, tiles + 1.0, tiles)
        first_tile = _split_dot(jnp.broadcast_to(tiles, (SUBLANES, LANES)), utri_ref[...], 3)[0:1]

        @pl.when(i == 0)
        def _():
            run_scr[...] = jnp.zeros_like(run_scr)
            lane1 = lax.broadcasted_iota(jnp.int32, (1, LANES), 1)
            is_c = lane1 < N_CLASSES
            end_tile = first_tile + tiles
            n_valid = jnp.sum(jnp.where(is_c, tiles, 0.0), axis=1, keepdims=True)
            rows = te_ref.shape[0]
            j = lax.broadcasted_iota(jnp.int32, (rows, LANES), 0).astype(F32)
            j = jnp.minimum(j, n_valid - 1.0)
            lane2 = lax.broadcasted_iota(jnp.int32, (rows, LANES), 1)
            done = jnp.where(jnp.logical_and(lane2 < N_CLASSES, end_tile <= j), 1.0, 0.0)
            tc = jnp.sum(done, axis=1, keepdims=True)
            tg = jnp.floor((tc + 0.5) * (1.0 / N_PAIRS))
            ta, tb = _pair_slots(tc - tg * N_PAIRS)
            te = jnp.where(lane2 == 0, tg * EXPERTS_PER_GROUP + ta, tg * EXPERTS_PER_GROUP + tb)
            te_ref[...] = te.astype(jnp.int32)
            last_tile = jnp.where(jnp.logical_and(is_c, tiles > 0), end_tile - 1.0, -1.0)
            meta = jnp.where(lane1 == N_CLASSES, n_valid, last_tile)
            meta_ref[...] = jnp.broadcast_to(meta, meta_ref.shape).astype(jnp.int32)

        rank = jnp.dot(ltri_ref[...], sel.astype(BF16), preferred_element_type=F32) + run_scr[...]
        dest = jnp.sum(jnp.where(is_cls, first_tile * tile_rows + rank, 0.0),
                       axis=1, keepdims=True)
        pos_ref[...] = jnp.where(lane == 0, dest, 0.0).astype(jnp.int32)
        run_scr[...] += jnp.sum(sel, axis=0, keepdims=True)


def _plan(route, n_tiles_max):
    n = route.shape[0]
    tm = min(PLAN_TM, n)
    r = np.arange(tm)
    ltri = jnp.asarray((r[None, :] < r[:, None]).astype(np.float32), dtype=BF16)
    e = np.arange(LANES)
    utri = jnp.asarray((e[:, None] < e[None, :]).astype(np.float32), dtype=BF16)
    te_rows = -(-n_tiles_max // SUBLANES) * SUBLANES
    kern = functools.partial(_plan_kernel, tile_rows=MOE_TM)
    return pl.pallas_call(
        kern,
        grid=(2, n // tm),
        in_specs=[pl.BlockSpec((tm, LANES), lambda p, i: (i, 0)),
                  pl.BlockSpec((tm, tm), lambda p, i: (0, 0)),
                  pl.BlockSpec((LANES, LANES), lambda p, i: (0, 0))],
        out_specs=[pl.BlockSpec((tm, LANES), lambda p, i: (i * p, 0)),
                   pl.BlockSpec((te_rows, LANES), lambda p, i: (0, 0)),
                   pl.BlockSpec((SUBLANES, LANES), lambda p, i: (0, 0))],
        out_shape=[jax.ShapeDtypeStruct((n, LANES), jnp.int32),
                   jax.ShapeDtypeStruct((te_rows, LANES), jnp.int32),
                   jax.ShapeDtypeStruct((SUBLANES, LANES), jnp.int32)],
        scratch_shapes=[pltpu.VMEM((1, LANES), F32), pltpu.VMEM((1, LANES), F32)],
        compiler_params=_params(("arbitrary", "arbitrary")),
        name="plan",
    )(route, ltri, utri)


DMA_UNROLL = 8


def _token_copy(src_ref, r, dst_ref, p, sem, rows):
    return pltpu.make_async_copy(src_ref.at[pl.ds(pl.multiple_of(r * rows, rows), rows)],
                                 dst_ref.at[pl.ds(pl.multiple_of(p * rows, rows), rows)], sem)


def _dispatch_kernel(pos_ref, meta_ref, t_ref, xs_ref, zero_scr, sem, *, tm, tile_rows):
    i = pl.program_id(0)
    tile_rows = tile_rows * SUBLANES
    n_tiles = xs_ref.shape[0] // tile_rows

    @pl.when(i == 0)
    def _():
        zero_scr[...] = jnp.zeros_like(zero_scr)
        n_valid = meta_ref[N_CLASSES]
        clears = [(meta_ref[c], meta_ref[c] >= 0) for c in range(N_CLASSES)]
        clears += [(n_tiles - 1 - k, n_tiles - 1 - k >= n_valid) for k in range(N_CLASSES)]

        def clear(tile):
            return pltpu.make_async_copy(
                zero_scr, xs_ref.at[pl.ds(tile * tile_rows, tile_rows)], sem)

        for tile, cond in clears:
            @pl.when(cond)
            def _():
                clear(tile).start()
        for tile, cond in clears:
            @pl.when(cond)
            def _():
                clear(tile).wait()

    base = i * tm

    def start(r2, carry):
        for k in range(2):
            r = 2 * r2 + k
            _token_copy(t_ref, r, xs_ref, pos_ref[base + r], sem, SUBLANES).start(priority=k)
        return carry

    def wait(r, carry):
        _token_copy(t_ref, r, xs_ref, pos_ref[base + r], sem, SUBLANES).wait()
        return carry

    lax.fori_loop(0, tm // 2, start, 0, unroll=DMA_UNROLL)
    lax.fori_loop(0, tm, wait, 0, unroll=DMA_UNROLL)


def _dispatch(pos, meta, t, n_rows):
    n = t.shape[0] // SUBLANES
    tm = min(DISPATCH_TM, n)
    kern = functools.partial(_dispatch_kernel, tm=tm, tile_rows=MOE_TM)
    grid_spec = pltpu.PrefetchScalarGridSpec(
        num_scalar_prefetch=2,
        grid=(n // tm,),
        in_specs=[pl.BlockSpec((tm * SUBLANES, LANES), lambda i, *_: (i, 0))],
        out_specs=pl.BlockSpec(memory_space=pl.ANY),
        scratch_shapes=[pltpu.VMEM((MOE_TM * SUBLANES, LANES), F32), pltpu.SemaphoreType.DMA(())],
    )
    return pl.pallas_call(
        kern,
        grid_spec=grid_spec,
        out_shape=jax.ShapeDtypeStruct((n_rows * SUBLANES, LANES), F32),
        compiler_params=_params(("arbitrary",)),
        name="dispatch",
    )(pos, meta, t)


def _expert_kernel(tea_ref, teb_ref, nv_ref, xs_ref, wga, wua, wda, wgb, wub, wdb, ys_ref):
    in_use = pl.program_id(0) < nv_ref[0]
    rows = xs_ref.shape[0] // SUBLANES

    @pl.when(in_use)
    def _():
        x = _load_token_tiles(xs_ref).astype(BF16)
        for slot, (wg, wu, wd) in enumerate(((wga, wua, wda), (wgb, wub, wdb))):
            hg = jnp.dot(x, wg[...].astype(BF16), preferred_element_type=F32)
            hu = jnp.dot(x, wu[...].astype(BF16), preferred_element_type=F32)
            a = (hg * jax.nn.sigmoid(hg) * hu).astype(BF16)
            y = jnp.dot(a, wd[...].astype(BF16), preferred_element_type=F32)
            for s in range(SUBLANES):
                ys_ref[pl.ds(slot * SUBLANES + s, rows, stride=TOKEN_ROWS_OUT), :] = (
                    y[:, s * LANES:(s + 1) * LANES])

    @pl.when(jnp.logical_not(in_use))
    def _():
        ys_ref[...] = jnp.zeros_like(ys_ref)


def _experts(tea, teb, nv, xs, wg, wu, wd):
    _, d, de = wg.shape
    assert d == SUBLANES * LANES
    n_tiles = xs.shape[0] // (MOE_TM * SUBLANES)
    wa = lambda shape: pl.BlockSpec(shape, lambda j, tea, teb, nv: (tea[j], 0, 0))
    wb = lambda shape: pl.BlockSpec(shape, lambda j, tea, teb, nv: (teb[j], 0, 0))
    grid_spec = pltpu.PrefetchScalarGridSpec(
        num_scalar_prefetch=3,
        grid=(n_tiles,),
        in_specs=[pl.BlockSpec((MOE_TM * SUBLANES, LANES),
                               lambda j, tea, teb, nv: (jnp.minimum(j, nv[0] - 1), 0)),
                  wa((None, d, de)), wa((None, d, de)), wa((None, de, d)),
                  wb((None, d, de)), wb((None, d, de)), wb((None, de, d))],
        out_specs=pl.BlockSpec((MOE_TM * TOKEN_ROWS_OUT, LANES), lambda j, *_: (j, 0)),
    )
    return pl.pallas_call(
        _expert_kernel,
        grid_spec=grid_spec,
        out_shape=jax.ShapeDtypeStruct((n_tiles * MOE_TM * TOKEN_ROWS_OUT, LANES), F32),
        compiler_params=_params(("arbitrary",)),
        name="experts",
    )(tea, teb, nv, xs, wg, wu, wd, wg, wu, wd)


def _combine_kernel(pos_ref, ys_ref, x1_ref, route_ref, fg_ref, o_ref, g, sem, *, tm):
    i = pl.program_id(0)
    slot = i % 2

    def gather(step, buf, wait):
        base = step * tm

        def body(r2, carry):
            for k in range(2):
                r = 2 * r2 + k
                c = _token_copy(ys_ref, pos_ref[base + r], g.at[buf], r, sem.at[buf],
                                TOKEN_ROWS_OUT)
                if wait:
                    c.wait()
                else:
                    c.start(priority=k)
            return carry

        lax.fori_loop(0, tm // 2, body, 0, unroll=DMA_UNROLL)

    @pl.when(i == 0)
    def _():
        gather(0, 0, wait=False)

    @pl.when(i + 1 < pl.num_programs(0))
    def _():
        gather(i + 1, 1 - slot, wait=False)

    gather(i, slot, wait=True)
    gs = g.at[slot]
    ya, yb = (jnp.concatenate([gs[pl.ds(off + s, tm, stride=TOKEN_ROWS_OUT), :]
                               for s in range(SUBLANES)], axis=1) for off in (0, SUBLANES))
    route = route_ref[...]
    x2 = x1_ref[...] + (route[:, ROUTE_WA:ROUTE_WA + 1] * ya + route[:, ROUTE_WB:ROUTE_WB + 1] * yb)
    var = jnp.mean(x2 * x2, axis=-1, keepdims=True)
    o_ref[...] = x2 * lax.rsqrt(var + EPS) * fg_ref[...]


def _combine(pos, ys, x1, route, final_g):
    n, d = x1.shape
    tm = min(COMBINE_TM, n)
    kern = functools.partial(_combine_kernel, tm=tm)
    grid_spec = pltpu.PrefetchScalarGridSpec(
        num_scalar_prefetch=1,
        grid=(n // tm,),
        in_specs=[pl.BlockSpec(memory_space=pl.ANY),
                  pl.BlockSpec((tm, d), lambda i, *_: (i, 0)),
                  pl.BlockSpec((tm, LANES), lambda i, *_: (i, 0)),
                  pl.BlockSpec((1, d), lambda i, *_: (0, 0))],
        out_specs=pl.BlockSpec((tm, d), lambda i, *_: (i, 0)),
        scratch_shapes=[pltpu.VMEM((2, tm * TOKEN_ROWS_OUT, LANES), F32),
                        pltpu.SemaphoreType.DMA((2,))],
    )
    return pl.pallas_call(
        kern,
        grid_spec=grid_spec,
        out_shape=jax.ShapeDtypeStruct((n, d), F32),
        compiler_params=_params(("arbitrary",)),
        name="combine",
    )(pos, ys, x1, route, final_g.reshape(1, d))


def _moe_sparse(t, route, wg, wu, wd, x1, final_g):
    n = x1.shape[0]
    n_tiles_max = (n + N_CLASSES * (MOE_TM - 1)) // MOE_TM
    pos, te, meta = _plan(route, n_tiles_max)
    pos = pos[:, 0]
    xs = _dispatch(pos, meta[0, :N_CLASSES + 1], t, n_tiles_max * MOE_TM)
    ys = _experts(te[:n_tiles_max, 0], te[:n_tiles_max, 1], meta[0, N_CLASSES:N_CLASSES + 1],
                  xs, wg, wu, wd)
    return _combine(pos, ys, x1, route, final_g)


def _suffix_ones(t):
    j = np.arange(t)[:, None]
    s = np.arange(t)[None, :]
    return jnp.asarray((j > s).astype(np.float32), dtype=BF16)


def _causal_bias(t):
    row = np.arange(t)[:, None]
    col = np.arange(t)[None, :]
    diag = np.where(col < row, 0.0, MASK_BIAS).astype(np.float32)
    return jnp.asarray(np.stack([np.zeros_like(diag), diag]))


def _chunk_prefix_ones(tt, c):
    t = np.arange(tt)[:, None]
    j = np.arange(tt)[None, :]
    return jnp.asarray(((j <= t) & (t // c == j // c)).astype(np.float32), dtype=BF16)


def _block_diag_ones(w, blk):
    a = np.arange(w)
    return jnp.asarray((a[:, None] // blk == a[None, :] // blk).astype(np.float32), dtype=BF16)


def kernel(x, ln1_g, w_in, w_branch_sb, w_branch_hg, hg_norm_g, hg_lb_logits, w_out, ln2_g,
           w_router_group, b_router_group, w_router_expert, b_router_expert,
           w_exp_gate, w_exp_up, w_exp_down, final_g):
    bsz, seq, d = x.shape
    depth = w_in.shape[0]
    n = bsz * seq
    sb_width = SB_HEADS * SB_HEAD_DIM
    hg_width = HG_HEADS * HG_DIM

    tri_attn = _suffix_ones(min(ATTN_T, seq))
    bias_attn = _causal_bias(min(ATTN_T, seq))
    tt = min(HGRN_TT, seq)
    tri_hg = _chunk_prefix_ones(tt, min(HGRN_C, tt))
    bd = _block_diag_ones(min(MXU_DIM, hg_width), HG_DIM)

    x2 = x.reshape(n, d)
    for l in range(depth):
        qkv, rest = _inproj(x2, ln1_g[l], w_in[l], sb_width)
        y_sb = _attn(qkv, bsz, seq, tri_attn, bias_attn)
        y_hg = _hgrn(rest, hg_lb_logits, hg_norm_g[l], bsz, seq, l, tri_hg, bd)

        pad = LANES - N_EXPERTS - N_GROUPS
        wr = jnp.concatenate([w_router_expert[l], w_router_group[l],
                              jnp.zeros((d, pad), F32)], axis=1)
        wr_hi = wr.astype(BF16)
        wr_lo = (wr - wr_hi.astype(F32)).astype(BF16)
        wr_split = jnp.concatenate([wr_hi, wr_lo], axis=1)
        br = jnp.concatenate([b_router_expert[l], b_router_group[l],
                              jnp.zeros((pad,), F32)]).reshape(1, LANES)

        last = l == depth - 1
        x1, t, route = _merge(x2, y_sb, y_hg, rest, w_branch_sb[l].astype(BF16),
                              w_branch_hg[l].astype(BF16), w_out[l].astype(BF16), ln2_g[l],
                              wr_split, br)
        assert last, "final rmsnorm is fused into the last layer's combine kernel"
        x2 = _moe_sparse(t, route, w_exp_gate[l], w_exp_up[l], w_exp_down[l], x1, final_g)
    return x2.reshape(bsz, seq, d)
```

```python
import functools

import jax
import jax.numpy as jnp
import numpy as np
from jax import lax
from jax.experimental import pallas as pl
from jax.experimental.pallas import tpu as pltpu

F32 = jnp.float32
BF16 = jnp.bfloat16

EPS = 1e-6
SB_HEADS = 8
SB_HEAD_DIM = 64
HG_HEADS = 8
HG_DIM = 64
N_GROUPS = 4
EXPERTS_PER_GROUP = 4
N_EXPERTS = N_GROUPS * EXPERTS_PER_GROUP

LANES = 128
SUBLANES = 8
MXU_DIM = 256
LOG2E = 1.4426950408889634
VMEM_LIMIT = 48 * 1024 * 1024

INPROJ_TM = 2048
INPROJ_TN = 512
ATTN_T = MXU_DIM
HGRN_TT = 256
HGRN_C = 32
MERGE_TM = 512
MOE_TM = 384
PLAN_TM = 1024
DISPATCH_TM = 1024
COMBINE_TM = 512


def _params(sem):
    return pltpu.CompilerParams(dimension_semantics=sem, vmem_limit_bytes=VMEM_LIMIT)


def _split_dot(x, m, passes):
    acc = None
    r = x
    for p in range(passes):
        h = r.astype(BF16)
        term = jnp.dot(h, m, preferred_element_type=F32)
        acc = term if acc is None else acc + term
        if p + 1 < passes:
            r = r - h.astype(F32)
    return acc


def _inproj_kernel(x_ref, g_ref, w_ref, qkv_ref, rest_ref, h_scr, *, q_scale):
    j = pl.program_id(1)

    @pl.when(j == 0)
    def _():
        x = x_ref[...]
        var = jnp.mean(x * x, axis=-1, keepdims=True)
        h_scr[...] = (x * lax.rsqrt(var + EPS) * g_ref[...]).astype(BF16)

    acc = jnp.dot(h_scr[...], w_ref[...].astype(BF16), preferred_element_type=F32)
    qkv_ref[...] = (acc * jnp.where(j == 0, q_scale, 1.0)).astype(BF16)
    rest_ref[...] = acc


def _inproj(x2, ln_g, w_in, sb_width):
    n, d = x2.shape
    cols = w_in.shape[1]
    tm, tn = min(INPROJ_TM, n), INPROJ_TN
    assert sb_width == tn, "q block must be exactly one column tile"
    nq = 3 * sb_width // tn
    nj = cols // tn
    kern = functools.partial(_inproj_kernel, q_scale=SB_HEAD_DIM ** -0.5)
    return pl.pallas_call(
        kern,
        grid=(n // tm, nj),
        in_specs=[
            pl.BlockSpec((tm, d), lambda i, j: (i, 0)),
            pl.BlockSpec((1, d), lambda i, j: (0, 0)),
            pl.BlockSpec((d, tn), lambda i, j: (0, j)),
        ],
        out_specs=[
            pl.BlockSpec((tm, tn), lambda i, j: (i, jnp.minimum(j, nq))),
            pl.BlockSpec((tm, tn), lambda i, j: (i, jnp.maximum(j - nq, 0))),
        ],
        out_shape=[
            jax.ShapeDtypeStruct((n, (nq + 1) * tn), BF16),
            jax.ShapeDtypeStruct((n, cols - nq * tn), F32),
        ],
        scratch_shapes=[pltpu.VMEM((tm, d), BF16)],
        compiler_params=_params(("parallel", "arbitrary")),
        name="inproj",
    )(x2, ln_g.reshape(1, d), w_in)


ATTN_STAGES = 3
ATTN_STREAMS = 2
MASK_BIAS = -1e30
ATTN_SKIP = 111.0


def _attn_kernel(q_ref, k_ref, v_ref, tri_ref, bias_ref, o_ref,
                 z0, z1, z2, i0, i1, i2, acc_ref, c_ref, *, t, nq):
    zbuf = (z0, z1, z2)
    ibuf = (i0, i1, i2)
    for r in zbuf + ibuf:
        r[...] = jnp.zeros_like(r)
    acc_ref[...] = jnp.zeros_like(acc_ref)
    c_ref[...] = jnp.zeros_like(c_ref)

    lane = lax.broadcasted_iota(jnp.int32, (t, LANES), 1)
    head0 = lane < SB_HEAD_DIM
    nt = (((1,), (1,)), ((), ()))

    def stage_a(qi, kj, slot):
        q = q_ref[pl.ds(pl.multiple_of(qi * t, t), t), :]
        zero = jnp.zeros_like(q)
        q2 = jnp.concatenate([jnp.where(head0, q, zero), jnp.where(head0, zero, q)], axis=0)
        k = k_ref[pl.ds(pl.multiple_of(kj * t, t), t), :]
        z = lax.dot_general(q2, k, nt, preferred_element_type=F32)
        bias = bias_ref[(qi == kj).astype(jnp.int32)]
        zbuf[slot][...] = z + jnp.concatenate([bias, bias], axis=0)

    def stage_b(slot):
        z = zbuf[slot][...]
        p = jnp.maximum(z, 0.0) + jnp.log(1.0 + jnp.exp2(jnp.abs(z) * (-LOG2E)))
        incl = jnp.dot(p.astype(BF16), tri_ref[...], preferred_element_type=F32)
        ibuf[slot][...] = incl
        return incl[:, 0:1]

    def stage_c(qi, kj, slot, stream):
        first = qi == kj
        incl = ibuf[slot][...]
        c = jnp.where(first, 0.0, c_ref[stream])
        a = jnp.exp2((zbuf[slot][...] - incl - c) * LOG2E)
        v = v_ref[pl.ds(pl.multiple_of(kj * t, t), t), :]
        pv = jnp.dot(a.astype(BF16), v, preferred_element_type=F32)
        acc = jnp.where(first, pv, acc_ref[stream] + pv)
        acc_ref[stream] = acc
        c_ref[stream] = c + incl[:, 0:1]
        o_ref[pl.ds(pl.multiple_of(qi * t, t), t), :] = (
            jnp.where(head0, acc[0:t], acc[t:2 * t]).astype(o_ref.dtype))

    def block(carry):
        qs, ks, qn, kn, skip_q, drained = carry
        qs, ks, qn, kn, skip_q = list(qs), list(ks), list(qn), list(kn), list(skip_q)
        all_done = qn[0] >= nq
        for p in range(1, ATTN_STREAMS):
            all_done = jnp.logical_and(all_done, qn[p] >= nq)
        drained = drained + all_done.astype(jnp.int32)
        for r in range(ATTN_STAGES * ATTN_STREAMS):
            sa, sc, sb = r % ATTN_STAGES, (r + 1) % ATTN_STAGES, (r + 2) % ATTN_STAGES
            pa, pb = r % ATTN_STREAMS, (r - 1) % ATTN_STREAMS
            stage_c(qs[sc], ks[sc], sc, pa)
            total_b = stage_b(sb)
            carry_b = jnp.where(qs[sb] == ks[sb], 0.0, c_ref[pb]) + total_b
            skip = skip_q[pa] == qn[pa]
            q_cur = jnp.where(skip, qn[pa] + ATTN_STREAMS, qn[pa])
            k_cur = jnp.where(skip, qn[pa] + ATTN_STREAMS, kn[pa])
            drain = q_cur >= nq
            qa = jnp.where(drain, 0, q_cur)
            ka = jnp.where(drain, 0, k_cur)
            stage_a(qa, ka, sa)
            last = k_cur == 0
            qn[pa] = jnp.where(jnp.logical_and(last, jnp.logical_not(drain)),
                               q_cur + ATTN_STREAMS, q_cur)
            kn[pa] = jnp.where(drain, k_cur, jnp.where(last, q_cur + ATTN_STREAMS, k_cur - 1))
            skip_q[pb] = jnp.where(jnp.min(carry_b) >= ATTN_SKIP, qs[sb], skip_q[pb])
            qs[sa], ks[sa] = qa, ka
        return tuple(qs), tuple(ks), tuple(qn), tuple(kn), tuple(skip_q), drained

    zero = jnp.int32(0)
    first_q = tuple(jnp.int32(p) for p in range(ATTN_STREAMS))
    init = ((zero,) * ATTN_STAGES, (zero,) * ATTN_STAGES, first_q, first_q,
            (jnp.int32(-1),) * ATTN_STREAMS, zero)
    lax.while_loop(lambda carry: carry[5] < 1, block, init)


def _attn(qkv, bsz, seq, tri, bias):
    n = bsz * seq
    t = min(ATTN_T, seq)
    pairs = SB_HEADS * SB_HEAD_DIM // LANES
    kern = functools.partial(_attn_kernel, t=t, nq=seq // t)
    return pl.pallas_call(
        kern,
        grid=(bsz, pairs),
        in_specs=[
            pl.BlockSpec((seq, LANES), lambda b, p: (b, p)),
            pl.BlockSpec((seq, LANES), lambda b, p: (b, pairs + p)),
            pl.BlockSpec((seq, LANES), lambda b, p: (b, 2 * pairs + p)),
            pl.BlockSpec((t, t), lambda b, p: (0, 0)),
            pl.BlockSpec((2, t, t), lambda b, p: (0, 0, 0)),
        ],
        out_specs=pl.BlockSpec((seq, LANES), lambda b, p: (b, p)),
        out_shape=jax.ShapeDtypeStruct((n, pairs * LANES), BF16),
        scratch_shapes=([pltpu.VMEM((2 * t, t), F32)] * (2 * ATTN_STAGES)
                        + [pltpu.VMEM((ATTN_STREAMS, 2 * t, LANES), F32),
                           pltpu.VMEM((ATTN_STREAMS, 2 * t, 1), F32)]),
        compiler_params=_params(("parallel", "parallel")),
        name="attn",
    )(qkv, qkv, qkv, tri, bias)


def _group_dot(x, bd):
    g = bd.shape[0]
    parts = [jnp.dot(x[:, i:i + g], bd, preferred_element_type=F32)
             for i in range(0, x.shape[1], g)]
    return jnp.concatenate(parts, axis=1)


def _hgrn_kernel(q_ref, f_ref, i_ref, g_ref, lbl_ref, ng_ref, tri_ref, bd_ref, o_ref,
                 st_scr, wpad, vpad, wsh, vsh, *, tt, c, layer):
    ti = pl.program_id(1)
    w = q_ref.shape[1]
    nch = tt // c

    @pl.when(ti == 0)
    def _():
        st_scr[...] = jnp.zeros_like(st_scr)

    lg = lbl_ref[...]
    e = jnp.exp(lg - jnp.max(lg, axis=0, keepdims=True))
    lb = jnp.sum(e[0:layer + 1], axis=0, keepdims=True) / jnp.sum(e, axis=0, keepdims=True)

    f = lb + (1.0 - lb) * jax.nn.sigmoid(f_ref[...])
    kk = 1.0 - f
    qv = q_ref[...]
    qs = qv * jax.nn.sigmoid(qv)
    v = i_ref[...]
    bd = bd_ref[...]
    cum = _split_dot_left(tri_ref[...], jnp.log(f) * LOG2E, 3)
    wk = jnp.log(kk) * LOG2E - cum

    pos = lax.broadcasted_iota(jnp.int32, (tt, 1), 0) % c
    wpad[0:SUBLANES, :] = jnp.zeros((SUBLANES, w), F32)
    vpad[0:SUBLANES, :] = jnp.zeros((SUBLANES, w), F32)
    wpad[SUBLANES:SUBLANES + tt, :] = wk
    vpad[SUBLANES:SUBLANES + tt, :] = v
    for b in range(SUBLANES):
        wb = wpad[SUBLANES - b:SUBLANES - b + tt, :]
        vb = vpad[SUBLANES - b:SUBLANES - b + tt, :]
        if b:
            wb = jnp.where(pos >= b, wb, -jnp.inf)
        wsh[b] = wb.reshape(nch, c, w)
        vsh[b] = vb.reshape(nch, c, w)

    qs3 = qs.reshape(nch, c, w)
    cum3 = cum.reshape(nch, c, w)
    acc = None
    for a in reversed(range(c // SUBLANES)):
        rows = c - SUBLANES * a
        qa = qs3[:, SUBLANES * a:, :].reshape(nch * rows, w)
        ca = cum3[:, SUBLANES * a:, :].reshape(nch * rows, w)
        acc_a = None
        for b in range(SUBLANES):
            wb = wsh[b, :, 0:rows, :].reshape(nch * rows, w)
            vb = vsh[b, :, 0:rows, :].reshape(nch * rows, w)
            dd = qa * jnp.exp2(ca + wb)
            term = _group_dot(dd.astype(BF16), bd) * vb
            acc_a = term if acc_a is None else acc_a + term
        acc_a = acc_a.reshape(nch, rows, w)
        if acc is not None:
            acc_a = acc_a + jnp.concatenate([jnp.zeros((nch, SUBLANES, w), F32), acc], axis=1)
        acc = acc_a
    acc = acc.reshape(tt, w)

    grp = st_scr.shape[1]
    bdmask = bd[0:grp, 0:grp] != 0
    outs = []
    for ci in range(nch):
        r0 = ci * c
        cum_c = cum[r0:r0 + c]
        last = cum_c[c - 1:c]
        qd = (qs[r0:r0 + c] * jnp.exp2(cum_c)).astype(BF16)
        kd = (kk[r0:r0 + c] * jnp.exp2(last - cum_c)).astype(BF16)
        vc = v[r0:r0 + c].astype(BF16)
        dec = jnp.exp2(last)
        o_parts = []
        for gi in range(w // grp):
            sl = slice(gi * grp, (gi + 1) * grp)
            st = st_scr[gi]
            o_parts.append(lax.dot_general(qd[:, sl], st.astype(BF16), (((1,), (1,)), ((), ())),
                                           preferred_element_type=F32))
            upd = lax.dot_general(vc[:, sl], kd[:, sl], (((0,), (0,)), ((), ())),
                                  preferred_element_type=F32)
            st_scr[gi] = st * dec[:, sl] + jnp.where(bdmask, upd, 0.0)
        outs.append(jnp.concatenate(o_parts, axis=1))
    o = acc + jnp.concatenate(outs, axis=0)

    o2 = o * o
    o2_hi = o2.astype(BF16)
    o2_lo = (o2 - o2_hi.astype(F32)).astype(BF16)
    ms = (_group_dot(o2_hi, bd) + _group_dot(o2_lo, bd)) * (1.0 / HG_DIM)
    gv = g_ref[...]
    o = o * lax.rsqrt(ms + EPS) * ng_ref[...] * (gv * jax.nn.sigmoid(gv))
    o_ref[...] = o.astype(o_ref.dtype)


def _split_dot_left(m, x, passes):
    acc = None
    r = x
    for p in range(passes):
        h = r.astype(BF16)
        term = jnp.dot(m, h, preferred_element_type=F32)
        acc = term if acc is None else acc + term
        if p + 1 < passes:
            r = r - h.astype(F32)
    return acc


def _hgrn(rest, lb_logits, norm_g, bsz, seq, layer, tri, bd):
    n = bsz * seq
    w = HG_HEADS * HG_DIM
    grp = bd.shape[0]
    tt = min(HGRN_TT, seq)
    c = min(HGRN_C, tt)
    nt = seq // tt
    kern = functools.partial(_hgrn_kernel, tt=tt, c=c, layer=layer)
    col = lambda j: pl.BlockSpec((tt, w), lambda b, i, j=j: (b * nt + i, j))
    const = lambda shape: pl.BlockSpec(shape, lambda b, i: (0, 0))
    return pl.pallas_call(
        kern,
        grid=(bsz, nt),
        in_specs=[col(0), col(1), col(2), col(3),
                  const(lb_logits.shape), const((1, w)), const((tt, tt)), const((grp, grp))],
        out_specs=pl.BlockSpec((tt, w), lambda b, i: (b * nt + i, 0)),
        out_shape=jax.ShapeDtypeStruct((n, w), BF16),
        scratch_shapes=([pltpu.VMEM((w // LANES, LANES, LANES), F32)]
                        + [pltpu.VMEM((SUBLANES + tt, w), F32)] * 2
                        + [pltpu.VMEM((SUBLANES, tt // c, c, w), F32)] * 2),
        compiler_params=_params(("parallel", "arbitrary")),
        name="hgrn",
    )(rest, rest, rest, rest, lb_logits, norm_g.reshape(1, w), tri, bd)


PAIR_A = (0, 0, 0, 1, 1, 3)
PAIR_B = (1, 2, 3, 3, 2, 2)
N_PAIRS = len(PAIR_A)
N_CLASSES = N_GROUPS * N_PAIRS
assert sorted(tuple(sorted(p)) for p in zip(PAIR_A, PAIR_B)) == [
    (a, b) for a in range(EXPERTS_PER_GROUP) for b in range(a + 1, EXPERTS_PER_GROUP)]
ROUTE_CLS, ROUTE_WA, ROUTE_WB = 0, 1, 2


def _pair_slots(pidx):
    a = b = jnp.zeros_like(pidx)
    for p in range(N_PAIRS):
        a = jnp.where(pidx == p, float(PAIR_A[p]), a)
        b = jnp.where(pidx == p, float(PAIR_B[p]), b)
    return a, b


def _pair_index(lo, hi):
    pidx = jnp.zeros_like(lo)
    for p in range(N_PAIRS):
        is_p = jnp.logical_and(lo == min(PAIR_A[p], PAIR_B[p]), hi == max(PAIR_A[p], PAIR_B[p]))
        pidx = jnp.where(is_p, float(p), pidx)
    return pidx


def _route(lg):
    lane = lax.broadcasted_iota(jnp.int32, lg.shape, 1)
    neg = jnp.float32(-jnp.inf)
    big = jnp.int32(LANES)
    gmask = jnp.logical_and(lane >= N_EXPERTS, lane < N_EXPERTS + N_GROUPS)
    gl = jnp.where(gmask, lg, neg)
    gmax = jnp.max(gl, axis=1, keepdims=True)
    gidx = jnp.min(jnp.where(gl == gmax, lane, big), axis=1, keepdims=True) - N_EXPERTS
    w_grp = 1.0 / jnp.sum(jnp.where(gmask, jnp.exp(gl - gmax), 0.0), axis=1, keepdims=True)
    in_grp = jnp.logical_and(lane < N_EXPERTS, lane // EXPERTS_PER_GROUP == gidx)
    l1 = jnp.where(in_grp, lg, neg)
    v1 = jnp.max(l1, axis=1, keepdims=True)
    i1 = jnp.min(jnp.where(l1 == v1, lane, big), axis=1, keepdims=True)
    l2 = jnp.where(jnp.logical_and(in_grp, lane != i1), lg, neg)
    v2 = jnp.max(l2, axis=1, keepdims=True)
    i2 = jnp.min(jnp.where(l2 == v2, lane, big), axis=1, keepdims=True)
    e2 = jnp.exp(v2 - v1)
    p1 = 1.0 / (1.0 + e2)
    p2 = e2 * p1
    loc1 = (i1 - gidx * EXPERTS_PER_GROUP).astype(F32)
    loc2 = (i2 - gidx * EXPERTS_PER_GROUP).astype(F32)
    pidx = _pair_index(jnp.minimum(loc1, loc2), jnp.maximum(loc1, loc2))
    cls = gidx.astype(F32) * N_PAIRS + pidx
    a_loc, _ = _pair_slots(pidx)
    first_is_a = loc1 == a_loc
    wa = jnp.where(first_is_a, p1, p2) * w_grp
    wb = jnp.where(first_is_a, p2, p1) * w_grp
    return jnp.where(lane == ROUTE_CLS, cls,
                     jnp.where(lane == ROUTE_WA, wa, jnp.where(lane == ROUTE_WB, wb, 0.0)))


def _store_token_tiles(ref, x):
    rows, d = x.shape
    assert d == SUBLANES * LANES
    for s in range(SUBLANES):
        ref[pl.ds(s, rows, stride=SUBLANES), :] = x[:, s * LANES:(s + 1) * LANES]


def _load_token_tiles(ref):
    rows = ref.shape[0] // SUBLANES
    return jnp.concatenate([ref[pl.ds(s, rows, stride=SUBLANES), :] for s in range(SUBLANES)],
                           axis=1)


def _merge_kernel(x_ref, ysb_ref, yhg_ref, gsb_ref, ghg_ref, wbs_ref, wbh_ref, wo_ref,
                  ln_ref, wr_ref, br_ref, x1_ref, t_ref, route_ref):
    a = jnp.dot(ysb_ref[...], wbs_ref[...], preferred_element_type=F32)
    b = jnp.dot(yhg_ref[...], wbh_ref[...], preferred_element_type=F32)
    merged = jax.nn.sigmoid(gsb_ref[...]) * a + jax.nn.sigmoid(ghg_ref[...]) * b
    x1 = x_ref[...] + jnp.dot(merged.astype(BF16), wo_ref[...], preferred_element_type=F32)
    x1_ref[...] = x1
    var = jnp.mean(x1 * x1, axis=-1, keepdims=True)
    t = x1 * lax.rsqrt(var + EPS) * ln_ref[...]
    _store_token_tiles(t_ref, t)
    t_hi = t.astype(BF16)
    t_lo = (t - t_hi.astype(F32)).astype(BF16)
    p_hi = jnp.dot(t_hi, wr_ref[...], preferred_element_type=F32)
    p_lo = jnp.dot(t_lo, wr_ref[...], preferred_element_type=F32)
    lg = (p_hi[:, :LANES] + p_hi[:, LANES:]) + (p_lo[:, :LANES] + p_lo[:, LANES:]) + br_ref[...]
    route_ref[...] = _route(lg)


def _merge(x2, y_sb, y_hg, rest, wbs, wbh, wo, ln_g, wr, br):
    n, d = x2.shape
    tm = min(MERGE_TM, n)
    w_sb, w_hg = y_sb.shape[1], y_hg.shape[1]
    gate_blk = (rest.shape[1] - 2 * d) // d
    row = lambda wdt, j=0: pl.BlockSpec((tm, wdt), lambda i, j=j: (i, j))
    const = lambda shape: pl.BlockSpec(shape, lambda i: (0, 0))
    return pl.pallas_call(
        _merge_kernel,
        grid=(n // tm,),
        in_specs=[row(d), row(w_sb), row(w_hg), row(d, gate_blk), row(d, gate_blk + 1),
                  const(wbs.shape), const(wbh.shape), const(wo.shape), const((1, d)),
                  const(wr.shape), const((1, LANES))],
        out_specs=[row(d), pl.BlockSpec((tm * SUBLANES, LANES), lambda i: (i, 0)), row(LANES)],
        out_shape=[jax.ShapeDtypeStruct((n, d), F32),
                   jax.ShapeDtypeStruct((n * SUBLANES, LANES), F32),
                   jax.ShapeDtypeStruct((n, LANES), F32)],
        compiler_params=_params(("parallel",)),
        name="merge",
    )(x2, y_sb, y_hg, rest, rest, wbs, wbh, wo, ln_g.reshape(1, d), wr, br)


TOKEN_ROWS_OUT = 2 * SUBLANES


def _plan_kernel(route_ref, ltri_ref, utri_ref, pos_ref, te_ref, meta_ref,
                 tot_scr, run_scr, *, tile_rows):
    ph = pl.program_id(0)
    i = pl.program_id(1)
    route = route_ref[...]
    tm = route.shape[0]
    lane = lax.broadcasted_iota(jnp.int32, (tm, LANES), 1)
    is_cls = lane.astype(F32) == route[:, ROUTE_CLS:ROUTE_CLS + 1]
    sel = jnp.where(is_cls, 1.0, 0.0)

    @pl.when(jnp.logical_and(ph == 0, i == 0))
    def _():
        tot_scr[...] = jnp.zeros_like(tot_scr)

    @pl.when(ph == 0)
    def _():
        tot_scr[...] += jnp.sum(sel, axis=0, keepdims=True)

    @pl.when(ph == 1)
    def _():
        tot = tot_scr[...]
        top = tot + (tile_rows - 1.0)
        tiles = jnp.floor(top * (1.0 / tile_rows))
        tiles = jnp.where(tiles * tile_rows > top, tiles - 1.0, tiles)
        tiles = jnp.where((tiles + 1.0) * tile_rows <= top, tiles + 1.0, tiles)
        first_tile = _split_dot(jnp.broadcast_to(tiles, (SUBLANES, LANES)), utri_ref[...], 3)[0:1]

        @pl.when(i == 0)
        def _():
            run_scr[...] = jnp.zeros_like(run_scr)
            lane1 = lax.broadcasted_iota(jnp.int32, (1, LANES), 1)
            is_c = lane1 < N_CLASSES
            end_tile = first_tile + tiles
            n_valid = jnp.sum(jnp.where(is_c, tiles, 0.0), axis=1, keepdims=True)
            rows = te_ref.shape[0]
            j = lax.broadcasted_iota(jnp.int32, (rows, LANES), 0).astype(F32)
            j = jnp.minimum(j, n_valid - 1.0)
            lane2 = lax.broadcasted_iota(jnp.int32, (rows, LANES), 1)
            done = jnp.where(jnp.logical_and(lane2 < N_CLASSES, end_tile <= j), 1.0, 0.0)
            tc = jnp.sum(done, axis=1, keepdims=True)
            tg = jnp.floor((tc + 0.5) * (1.0 / N_PAIRS))
            ta, tb = _pair_slots(tc - tg * N_PAIRS)
            te = jnp.where(lane2 == 0, tg * EXPERTS_PER_GROUP + ta, tg * EXPERTS_PER_GROUP + tb)
            te_ref[...] = te.astype(jnp.int32)
            last_tile = jnp.where(jnp.logical_and(is_c, tiles > 0), end_tile - 1.0, -1.0)
            meta = jnp.where(lane1 == N_CLASSES, n_valid, last_tile)
            meta_ref[...] = jnp.broadcast_to(meta, meta_ref.shape).astype(jnp.int32)

        rank = jnp.dot(ltri_ref[...], sel.astype(BF16), preferred_element_type=F32) + run_scr[...]
        dest = jnp.sum(jnp.where(is_cls, first_tile * tile_rows + rank, 0.0),
                       axis=1, keepdims=True)
        pos_ref[...] = jnp.where(lane == 0, dest, 0.0).astype(jnp.int32)
        run_scr[...] += jnp.sum(sel, axis=0, keepdims=True)


def _plan(route, n_tiles_max):
    n = route.shape[0]
    tm = min(PLAN_TM, n)
    r = np.arange(tm)
    ltri = jnp.asarray((r[None, :] < r[:, None]).astype(np.float32), dtype=BF16)
    e = np.arange(LANES)
    utri = jnp.asarray((e[:, None] < e[None, :]).astype(np.float32), dtype=BF16)
    te_rows = -(-n_tiles_max // SUBLANES) * SUBLANES
    kern = functools.partial(_plan_kernel, tile_rows=MOE_TM)
    return pl.pallas_call(
        kern,
        grid=(2, n // tm),
        in_specs=[pl.BlockSpec((tm, LANES), lambda p, i: (i, 0)),
                  pl.BlockSpec((tm, tm), lambda p, i: (0, 0)),
                  pl.BlockSpec((LANES, LANES), lambda p, i: (0, 0))],
        out_specs=[pl.BlockSpec((tm, LANES), lambda p, i: (i * p, 0)),
                   pl.BlockSpec((te_rows, LANES), lambda p, i: (0, 0)),
                   pl.BlockSpec((SUBLANES, LANES), lambda p, i: (0, 0))],
        out_shape=[jax.ShapeDtypeStruct((n, LANES), jnp.int32),
                   jax.ShapeDtypeStruct((te_rows, LANES), jnp.int32),
                   jax.ShapeDtypeStruct((SUBLANES, LANES), jnp.int32)],
        scratch_shapes=[pltpu.VMEM((1, LANES), F32), pltpu.VMEM((1, LANES), F32)],
        compiler_params=_params(("arbitrary", "arbitrary")),
        name="plan",
    )(route, ltri, utri)


DMA_UNROLL = 8


def _token_copy(src_ref, r, dst_ref, p, sem, rows):
    return pltpu.make_async_copy(src_ref.at[pl.ds(pl.multiple_of(r * rows, rows), rows)],
                                 dst_ref.at[pl.ds(pl.multiple_of(p * rows, rows), rows)], sem)


def _dispatch_kernel(pos_ref, meta_ref, t_ref, xs_ref, zero_scr, sem, *, tm, tile_rows):
    i = pl.program_id(0)
    tile_rows = tile_rows * SUBLANES
    n_tiles = xs_ref.shape[0] // tile_rows

    @pl.when(i == 0)
    def _():
        zero_scr[...] = jnp.zeros_like(zero_scr)
        n_valid = meta_ref[N_CLASSES]
        clears = [(meta_ref[c], meta_ref[c] >= 0) for c in range(N_CLASSES)]
        clears += [(n_tiles - 1 - k, n_tiles - 1 - k >= n_valid) for k in range(N_CLASSES)]

        def clear(tile):
            return pltpu.make_async_copy(
                zero_scr, xs_ref.at[pl.ds(tile * tile_rows, tile_rows)], sem)

        for tile, cond in clears:
            @pl.when(cond)
            def _():
                clear(tile).start()
        for tile, cond in clears:
            @pl.when(cond)
            def _():
                clear(tile).wait()

    base = i * tm

    def start(r2, carry):
        for k in range(2):
            r = 2 * r2 + k
            _token_copy(t_ref, r, xs_ref, pos_ref[base + r], sem, SUBLANES).start(priority=k)
        return carry

    def wait(r, carry):
        _token_copy(t_ref, r, xs_ref, pos_ref[base + r], sem, SUBLANES).wait()
        return carry

    lax.fori_loop(0, tm // 2, start, 0, unroll=DMA_UNROLL)
    lax.fori_loop(0, tm, wait, 0, unroll=DMA_UNROLL)


def _dispatch(pos, meta, t, n_rows):
    n = t.shape[0] // SUBLANES
    tm = min(DISPATCH_TM, n)
    kern = functools.partial(_dispatch_kernel, tm=tm, tile_rows=MOE_TM)
    grid_spec = pltpu.PrefetchScalarGridSpec(
        num_scalar_prefetch=2,
        grid=(n // tm,),
        in_specs=[pl.BlockSpec((tm * SUBLANES, LANES), lambda i, *_: (i, 0))],
        out_specs=pl.BlockSpec(memory_space=pl.ANY),
        scratch_shapes=[pltpu.VMEM((MOE_TM * SUBLANES, LANES), F32), pltpu.SemaphoreType.DMA(())],
    )
    return pl.pallas_call(
        kern,
        grid_spec=grid_spec,
        out_shape=jax.ShapeDtypeStruct((n_rows * SUBLANES, LANES), F32),
        compiler_params=_params(("arbitrary",)),
        name="dispatch",
    )(pos, meta, t)


def _expert_kernel(tea_ref, teb_ref, nv_ref, xs_ref, wga, wua, wda, wgb, wub, wdb, ys_ref):
    in_use = pl.program_id(0) < nv_ref[0]
    rows = xs_ref.shape[0] // SUBLANES

    @pl.when(in_use)
    def _():
        x = _load_token_tiles(xs_ref).astype(BF16)
        for slot, (wg, wu, wd) in enumerate(((wga, wua, wda), (wgb, wub, wdb))):
            hg = jnp.dot(x, wg[...].astype(BF16), preferred_element_type=F32)
            hu = jnp.dot(x, wu[...].astype(BF16), preferred_element_type=F32)
            a = (hg * jax.nn.sigmoid(hg) * hu).astype(BF16)
            y = jnp.dot(a, wd[...].astype(BF16), preferred_element_type=F32)
            for s in range(SUBLANES):
                ys_ref[pl.ds(slot * SUBLANES + s, rows, stride=TOKEN_ROWS_OUT), :] = (
                    y[:, s * LANES:(s + 1) * LANES])

    @pl.when(jnp.logical_not(in_use))
    def _():
        ys_ref[...] = jnp.zeros_like(ys_ref)


def _experts(tea, teb, nv, xs, wg, wu, wd):
    _, d, de = wg.shape
    assert d == SUBLANES * LANES
    n_tiles = xs.shape[0] // (MOE_TM * SUBLANES)
    wa = lambda shape: pl.BlockSpec(shape, lambda j, tea, teb, nv: (tea[j], 0, 0))
    wb = lambda shape: pl.BlockSpec(shape, lambda j, tea, teb, nv: (teb[j], 0, 0))
    grid_spec = pltpu.PrefetchScalarGridSpec(
        num_scalar_prefetch=3,
        grid=(n_tiles,),
        in_specs=[pl.BlockSpec((MOE_TM * SUBLANES, LANES),
                               lambda j, tea, teb, nv: (jnp.minimum(j, nv[0] - 1), 0)),
                  wa((None, d, de)), wa((None, d, de)), wa((None, de, d)),
                  wb((None, d, de)), wb((None, d, de)), wb((None, de, d))],
        out_specs=pl.BlockSpec((MOE_TM * TOKEN_ROWS_OUT, LANES), lambda j, *_: (j, 0)),
    )
    return pl.pallas_call(
        _expert_kernel,
        grid_spec=grid_spec,
        out_shape=jax.ShapeDtypeStruct((n_tiles * MOE_TM * TOKEN_ROWS_OUT, LANES), F32),
        compiler_params=_params(("arbitrary",)),
        name="experts",
    )(tea, teb, nv, xs, wg, wu, wd, wg, wu, wd)


def _combine_kernel(pos_ref, ys_ref, x1_ref, route_ref, fg_ref, o_ref, g, sem, *, tm):
    i = pl.program_id(0)
    slot = i % 2

    def gather(step, buf, wait):
        base = step * tm

        def body(r2, carry):
            for k in range(2):
                r = 2 * r2 + k
                c = _token_copy(ys_ref, pos_ref[base + r], g.at[buf], r, sem.at[buf],
                                TOKEN_ROWS_OUT)
                if wait:
                    c.wait()
                else:
                    c.start(priority=k)
            return carry

        lax.fori_loop(0, tm // 2, body, 0, unroll=DMA_UNROLL)

    @pl.when(i == 0)
    def _():
        gather(0, 0, wait=False)

    @pl.when(i + 1 < pl.num_programs(0))
    def _():
        gather(i + 1, 1 - slot, wait=False)

    gather(i, slot, wait=True)
    gs = g.at[slot]
    ya, yb = (jnp.concatenate([gs[pl.ds(off + s, tm, stride=TOKEN_ROWS_OUT), :]
                               for s in range(SUBLANES)], axis=1) for off in (0, SUBLANES))
    route = route_ref[...]
    x2 = x1_ref[...] + (route[:, ROUTE_WA:ROUTE_WA + 1] * ya + route[:, ROUTE_WB:ROUTE_WB + 1] * yb)
    var = jnp.mean(x2 * x2, axis=-1, keepdims=True)
    o_ref[...] = x2 * lax.rsqrt(var + EPS) * fg_ref[...]


def _combine(pos, ys, x1, route, final_g):
    n, d = x1.shape
    tm = min(COMBINE_TM, n)
    kern = functools.partial(_combine_kernel, tm=tm)
    grid_spec = pltpu.PrefetchScalarGridSpec(
        num_scalar_prefetch=1,
        grid=(n // tm,),
        in_specs=[pl.BlockSpec(memory_space=pl.ANY),
                  pl.BlockSpec((tm, d), lambda i, *_: (i, 0)),
                  pl.BlockSpec((tm, LANES), lambda i, *_: (i, 0)),
                  pl.BlockSpec((1, d), lambda i, *_: (0, 0))],
        out_specs=pl.BlockSpec((tm, d), lambda i, *_: (i, 0)),
        scratch_shapes=[pltpu.VMEM((2, tm * TOKEN_ROWS_OUT, LANES), F32),
                        pltpu.SemaphoreType.DMA((2,))],
    )
    return pl.pallas_call(
        kern,
        grid_spec=grid_spec,
        out_shape=jax.ShapeDtypeStruct((n, d), F32),
        compiler_params=_params(("arbitrary",)),
        name="combine",
    )(pos, ys, x1, route, final_g.reshape(1, d))


def _moe_sparse(t, route, wg, wu, wd, x1, final_g):
    n = x1.shape[0]
    n_tiles_max = (n + N_CLASSES * (MOE_TM - 1)) // MOE_TM
    pos, te, meta = _plan(route, n_tiles_max)
    pos = pos[:, 0]
    xs = _dispatch(pos, meta[0, :N_CLASSES + 1], t, n_tiles_max * MOE_TM)
    ys = _experts(te[:n_tiles_max, 0], te[:n_tiles_max, 1], meta[0, N_CLASSES:N_CLASSES + 1],
                  xs, wg, wu, wd)
    return _combine(pos, ys, x1, route, final_g)


def _suffix_ones(t):
    j = np.arange(t)[:, None]
    s = np.arange(t)[None, :]
    return jnp.asarray((j >= s).astype(np.float32), dtype=BF16)


def _causal_bias(t):
    row = np.arange(t)[:, None]
    col = np.arange(t)[None, :]
    diag = np.where(col < row, 0.0, MASK_BIAS).astype(np.float32)
    return jnp.asarray(np.stack([np.zeros_like(diag), diag]))


def _chunk_prefix_ones(tt, c):
    t = np.arange(tt)[:, None]
    j = np.arange(tt)[None, :]
    return jnp.asarray(((j <= t) & (t // c == j // c)).astype(np.float32), dtype=BF16)


def _block_diag_ones(w, blk):
    a = np.arange(w)
    return jnp.asarray((a[:, None] // blk == a[None, :] // blk).astype(np.float32), dtype=BF16)


def kernel(x, ln1_g, w_in, w_branch_sb, w_branch_hg, hg_norm_g, hg_lb_logits, w_out, ln2_g,
           w_router_group, b_router_group, w_router_expert, b_router_expert,
           w_exp_gate, w_exp_up, w_exp_down, final_g):
    bsz, seq, d = x.shape
    depth = w_in.shape[0]
    n = bsz * seq
    sb_width = SB_HEADS * SB_HEAD_DIM
    hg_width = HG_HEADS * HG_DIM

    tri_attn = _suffix_ones(min(ATTN_T, seq))
    bias_attn = _causal_bias(min(ATTN_T, seq))
    tt = min(HGRN_TT, seq)
    tri_hg = _chunk_prefix_ones(tt, min(HGRN_C, tt))
    bd = _block_diag_ones(min(MXU_DIM, hg_width), HG_DIM)

    x2 = x.reshape(n, d)
    for l in range(depth):
        qkv, rest = _inproj(x2, ln1_g[l], w_in[l], sb_width)
        y_sb = _attn(qkv, bsz, seq, tri_attn, bias_attn)
        y_hg = _hgrn(rest, hg_lb_logits, hg_norm_g[l], bsz, seq, l, tri_hg, bd)

        pad = LANES - N_EXPERTS - N_GROUPS
        wr = jnp.concatenate([w_router_expert[l], w_router_group[l],
                              jnp.zeros((d, pad), F32)], axis=1)
        wr_hi = wr.astype(BF16)
        wr_lo = (wr - wr_hi.astype(F32)).astype(BF16)
        wr_split = jnp.concatenate([wr_hi, wr_lo], axis=1)
        br = jnp.concatenate([b_router_expert[l], b_router_group[l],
                              jnp.zeros((pad,), F32)]).reshape(1, LANES)

        last = l == depth - 1
        x1, t, route = _merge(x2, y_sb, y_hg, rest, w_branch_sb[l].astype(BF16),
                              w_branch_hg[l].astype(BF16), w_out[l].astype(BF16), ln2_g[l],
                              wr_split, br)
        assert last, "final rmsnorm is fused into the last layer's combine kernel"
        x2 = _moe_sparse(t, route, w_exp_gate[l], w_exp_up[l], w_exp_down[l], x1, final_g)
    return x2.reshape(bsz, seq, d)
```

```python
import functools

import jax
import jax.numpy as jnp
import numpy as np
from jax import lax
from jax.experimental import pallas as pl
from jax.experimental.pallas import tpu as pltpu

F32 = jnp.float32
BF16 = jnp.bfloat16

EPS = 1e-6
SB_HEADS = 8
SB_HEAD_DIM = 64
HG_HEADS = 8
HG_DIM = 64
N_GROUPS = 4
EXPERTS_PER_GROUP = 4
N_EXPERTS = N_GROUPS * EXPERTS_PER_GROUP

LANES = 128
SUBLANES = 8
MXU_DIM = 256
LOG2E = 1.4426950408889634
VMEM_LIMIT = 48 * 1024 * 1024

INPROJ_TM = 2048
INPROJ_TN = 512
ATTN_T = MXU_DIM
HGRN_TT = 256
HGRN_C = 32
MERGE_TM = 512
MOE_TM = 384
PLAN_TM = 1024
DISPATCH_TM = 1024
COMBINE_TM = 512


def _params(sem):
    return pltpu.CompilerParams(dimension_semantics=sem, vmem_limit_bytes=VMEM_LIMIT)


def _split_dot(x, m, passes):
    acc = None
    r = x
    for p in range(passes):
        h = r.astype(BF16)
        term = jnp.dot(h, m, preferred_element_type=F32)
        acc = term if acc is None else acc + term
        if p + 1 < passes:
            r = r - h.astype(F32)
    return acc


def _inproj_kernel(x_ref, g_ref, w_ref, qkv_ref, rest_ref, h_scr, *, q_scale):
    j = pl.program_id(1)

    @pl.when(j == 0)
    def _():
        x = x_ref[...]
        var = jnp.mean(x * x, axis=-1, keepdims=True)
        h_scr[...] = (x * lax.rsqrt(var + EPS) * g_ref[...]).astype(BF16)

    acc = jnp.dot(h_scr[...], w_ref[...].astype(BF16), preferred_element_type=F32)
    qkv_ref[...] = (acc * jnp.where(j == 0, q_scale, 1.0)).astype(BF16)
    rest_ref[...] = acc


def _inproj(x2, ln_g, w_in, sb_width):
    n, d = x2.shape
    cols = w_in.shape[1]
    tm, tn = min(INPROJ_TM, n), INPROJ_TN
    assert sb_width == tn, "q block must be exactly one column tile"
    nq = 3 * sb_width // tn
    nj = cols // tn
    kern = functools.partial(_inproj_kernel, q_scale=SB_HEAD_DIM ** -0.5)
    return pl.pallas_call(
        kern,
        grid=(n // tm, nj),
        in_specs=[
            pl.BlockSpec((tm, d), lambda i, j: (i, 0)),
            pl.BlockSpec((1, d), lambda i, j: (0, 0)),
            pl.BlockSpec((d, tn), lambda i, j: (0, j)),
        ],
        out_specs=[
            pl.BlockSpec((tm, tn), lambda i, j: (i, jnp.minimum(j, nq))),
            pl.BlockSpec((tm, tn), lambda i, j: (i, jnp.maximum(j - nq, 0))),
        ],
        out_shape=[
            jax.ShapeDtypeStruct((n, (nq + 1) * tn), BF16),
            jax.ShapeDtypeStruct((n, cols - nq * tn), F32),
        ],
        scratch_shapes=[pltpu.VMEM((tm, d), BF16)],
        compiler_params=_params(("parallel", "arbitrary")),
        name="inproj",
    )(x2, ln_g.reshape(1, d), w_in)


ATTN_STAGES = 3
ATTN_STREAMS = 2
MASK_BIAS = -1e30
ATTN_SKIP = 111.0


def _attn_kernel(q_ref, k_ref, v_ref, tri_ref, bias_ref, o_ref,
                 z0, z1, z2, i0, i1, i2, acc_ref, c_ref, *, t, nq):
    zbuf = (z0, z1, z2)
    ibuf = (i0, i1, i2)
    for r in zbuf + ibuf + (acc_ref, c_ref):
        r[...] = jnp.zeros_like(r)

    lane = lax.broadcasted_iota(jnp.int32, (t, LANES), 1)
    head0 = lane < SB_HEAD_DIM
    nt = (((1,), (1,)), ((), ()))

    def stage_a(qi, kj, slot):
        q = q_ref[pl.ds(pl.multiple_of(qi * t, t), t), :]
        zero = jnp.zeros_like(q)
        q2 = jnp.concatenate([jnp.where(head0, q, zero), jnp.where(head0, zero, q)], axis=0)
        k = k_ref[pl.ds(pl.multiple_of(kj * t, t), t), :]
        z = lax.dot_general(q2, k, nt, preferred_element_type=F32)
        bias = bias_ref[(qi == kj).astype(jnp.int32)]
        zbuf[slot][...] = z + jnp.concatenate([bias, bias], axis=0)

    def stage_b(slot):
        z = zbuf[slot][...]
        p = jnp.maximum(z, 0.0) + jnp.log(1.0 + jnp.exp2(jnp.abs(z) * (-LOG2E)))
        incl = jnp.dot(p.astype(BF16), tri_ref[...], preferred_element_type=F32) + p
        ibuf[slot][...] = incl
        return incl[:, 0:1]

    def stage_c(qi, kj, slot, stream):
        first = qi == kj
        incl = ibuf[slot][...]
        c = jnp.where(first, 0.0, c_ref[stream])
        a = jnp.exp2((zbuf[slot][...] - incl - c) * LOG2E)
        v = v_ref[pl.ds(pl.multiple_of(kj * t, t), t), :]
        pv = jnp.dot(a.astype(BF16), v, preferred_element_type=F32)
        acc = jnp.where(first, pv, acc_ref[stream] + pv)
        acc_ref[stream] = acc
        c_ref[stream] = c + incl[:, 0:1]
        o_ref[pl.ds(pl.multiple_of(qi * t, t), t), :] = (
            jnp.where(head0, acc[0:t], acc[t:2 * t]).astype(o_ref.dtype))

    def block(carry):
        qs, ks, qn, kn, skip_q, drained = carry
        qs, ks, qn, kn, skip_q = list(qs), list(ks), list(qn), list(kn), list(skip_q)
        all_done = qn[0] >= nq
        for p in range(1, ATTN_STREAMS):
            all_done = jnp.logical_and(all_done, qn[p] >= nq)
        drained = drained + all_done.astype(jnp.int32)
        for r in range(ATTN_STAGES * ATTN_STREAMS):
            sa, sc, sb = r % ATTN_STAGES, (r + 1) % ATTN_STAGES, (r + 2) % ATTN_STAGES
            pa, pb = r % ATTN_STREAMS, (r - 1) % ATTN_STREAMS
            stage_c(qs[sc], ks[sc], sc, pa)
            total_b = stage_b(sb)
            carry_b = jnp.where(qs[sb] == ks[sb], 0.0, c_ref[pb]) + total_b
            skip = skip_q[pa] == qn[pa]
            q_cur = jnp.where(skip, qn[pa] + ATTN_STREAMS, qn[pa])
            k_cur = jnp.where(skip, qn[pa] + ATTN_STREAMS, kn[pa])
            drain = q_cur >= nq
            qa = jnp.where(drain, 0, q_cur)
            ka = jnp.where(drain, 0, k_cur)
            stage_a(qa, ka, sa)
            last = k_cur == 0
            qn[pa] = jnp.where(jnp.logical_and(last, jnp.logical_not(drain)),
                               q_cur + ATTN_STREAMS, q_cur)
            kn[pa] = jnp.where(drain, k_cur, jnp.where(last, q_cur + ATTN_STREAMS, k_cur - 1))
            skip_q[pb] = jnp.where(jnp.min(carry_b) >= ATTN_SKIP, qs[sb], skip_q[pb])
            qs[sa], ks[sa] = qa, ka
        return tuple(qs), tuple(ks), tuple(qn), tuple(kn), tuple(skip_q), drained

    zero = jnp.int32(0)
    first_q = tuple(jnp.int32(p) for p in range(ATTN_STREAMS))
    init = ((zero,) * ATTN_STAGES, (zero,) * ATTN_STAGES, first_q, first_q,
            (jnp.int32(-1),) * ATTN_STREAMS, zero)
    lax.while_loop(lambda carry: carry[5] < 1, block, init)


def _attn(qkv, bsz, seq, tri, bias):
    n = bsz * seq
    t = min(ATTN_T, seq)
    pairs = SB_HEADS * SB_HEAD_DIM // LANES
    kern = functools.partial(_attn_kernel, t=t, nq=seq // t)
    return pl.pallas_call(
        kern,
        grid=(bsz, pairs),
        in_specs=[
            pl.BlockSpec((seq, LANES), lambda b, p: (b, p)),
            pl.BlockSpec((seq, LANES), lambda b, p: (b, pairs + p)),
            pl.BlockSpec((seq, LANES), lambda b, p: (b, 2 * pairs + p)),
            pl.BlockSpec((t, t), lambda b, p: (0, 0)),
            pl.BlockSpec((2, t, t), lambda b, p: (0, 0, 0)),
        ],
        out_specs=pl.BlockSpec((seq, LANES), lambda b, p: (b, p)),
        out_shape=jax.ShapeDtypeStruct((n, pairs * LANES), BF16),
        scratch_shapes=([pltpu.VMEM((2 * t, t), F32)] * (2 * ATTN_STAGES)
                        + [pltpu.VMEM((ATTN_STREAMS, 2 * t, LANES), F32),
                           pltpu.VMEM((ATTN_STREAMS, 2 * t, 1), F32)]),
        compiler_params=_params(("parallel", "parallel")),
        name="attn",
    )(qkv, qkv, qkv, tri, bias)


def _group_dot(x, bd):
    g = bd.shape[0]
    parts = [jnp.dot(x[:, i:i + g], bd, preferred_element_type=F32)
             for i in range(0, x.shape[1], g)]
    return jnp.concatenate(parts, axis=1)


def _hgrn_kernel(q_ref, f_ref, i_ref, g_ref, lbl_ref, ng_ref, tri_ref, bd_ref, o_ref,
                 st_scr, wpad, vpad, wsh, vsh, *, tt, c, layer):
    ti = pl.program_id(1)
    w = q_ref.shape[1]
    nch = tt // c

    @pl.when(ti == 0)
    def _():
        st_scr[...] = jnp.zeros_like(st_scr)

    lg = lbl_ref[...]
    e = jnp.exp(lg - jnp.max(lg, axis=0, keepdims=True))
    lb = jnp.sum(e[0:layer + 1], axis=0, keepdims=True) / jnp.sum(e, axis=0, keepdims=True)

    f = lb + (1.0 - lb) * jax.nn.sigmoid(f_ref[...])
    kk = 1.0 - f
    qv = q_ref[...]
    qs = qv * jax.nn.sigmoid(qv)
    v = i_ref[...]
    bd = bd_ref[...]
    cum = _split_dot_left(tri_ref[...], jnp.log(f) * LOG2E, 3)
    wk = jnp.log(kk) * LOG2E - cum

    pos = lax.broadcasted_iota(jnp.int32, (tt, 1), 0) % c
    wpad[0:SUBLANES, :] = jnp.zeros((SUBLANES, w), F32)
    vpad[0:SUBLANES, :] = jnp.zeros((SUBLANES, w), F32)
    wpad[SUBLANES:SUBLANES + tt, :] = wk
    vpad[SUBLANES:SUBLANES + tt, :] = v
    for b in range(SUBLANES):
        wb = wpad[SUBLANES - b:SUBLANES - b + tt, :]
        vb = vpad[SUBLANES - b:SUBLANES - b + tt, :]
        if b:
            wb = jnp.where(pos >= b, wb, -jnp.inf)
        wsh[b] = wb.reshape(nch, c, w)
        vsh[b] = vb.reshape(nch, c, w)

    qs3 = qs.reshape(nch, c, w)
    cum3 = cum.reshape(nch, c, w)
    acc = None
    for a in reversed(range(c // SUBLANES)):
        rows = c - SUBLANES * a
        qa = qs3[:, SUBLANES * a:, :].reshape(nch * rows, w)
        ca = cum3[:, SUBLANES * a:, :].reshape(nch * rows, w)
        acc_a = None
        for b in range(SUBLANES):
            wb = wsh[b, :, 0:rows, :].reshape(nch * rows, w)
            vb = vsh[b, :, 0:rows, :].reshape(nch * rows, w)
            dd = qa * jnp.exp2(ca + wb)
            term = _group_dot(dd.astype(BF16), bd) * vb
            acc_a = term if acc_a is None else acc_a + term
        acc_a = acc_a.reshape(nch, rows, w)
        if acc is not None:
            acc_a = acc_a + jnp.concatenate([jnp.zeros((nch, SUBLANES, w), F32), acc], axis=1)
        acc = acc_a
    acc = acc.reshape(tt, w)

    grp = st_scr.shape[1]
    bdmask = bd[0:grp, 0:grp] != 0
    outs = []
    for ci in range(nch):
        r0 = ci * c
        cum_c = cum[r0:r0 + c]
        last = cum_c[c - 1:c]
        qd = (qs[r0:r0 + c] * jnp.exp2(cum_c)).astype(BF16)
        kd = (kk[r0:r0 + c] * jnp.exp2(last - cum_c)).astype(BF16)
        vc = v[r0:r0 + c].astype(BF16)
        dec = jnp.exp2(last)
        o_parts = []
        for gi in range(w // grp):
            sl = slice(gi * grp, (gi + 1) * grp)
            st = st_scr[gi]
            o_parts.append(lax.dot_general(qd[:, sl], st.astype(BF16), (((1,), (1,)), ((), ())),
                                           preferred_element_type=F32))
            upd = lax.dot_general(vc[:, sl], kd[:, sl], (((0,), (0,)), ((), ())),
                                  preferred_element_type=F32)
            st_scr[gi] = st * dec[:, sl] + jnp.where(bdmask, upd, 0.0)
        outs.append(jnp.concatenate(o_parts, axis=1))
    o = acc + jnp.concatenate(outs, axis=0)

    o2 = o * o
    o2_hi = o2.astype(BF16)
    o2_lo = (o2 - o2_hi.astype(F32)).astype(BF16)
    ms = (_group_dot(o2_hi, bd) + _group_dot(o2_lo, bd)) * (1.0 / HG_DIM)
    gv = g_ref[...]
    o = o * lax.rsqrt(ms + EPS) * ng_ref[...] * (gv * jax.nn.sigmoid(gv))
    o_ref[...] = o.astype(o_ref.dtype)


def _split_dot_left(m, x, passes):
    acc = None
    r = x
    for p in range(passes):
        h = r.astype(BF16)
        term = jnp.dot(m, h, preferred_element_type=F32)
        acc = term if acc is None else acc + term
        if p + 1 < passes:
            r = r - h.astype(F32)
    return acc


def _hgrn(rest, lb_logits, norm_g, bsz, seq, layer, tri, bd):
    n = bsz * seq
    w = HG_HEADS * HG_DIM
    grp = bd.shape[0]
    tt = min(HGRN_TT, seq)
    c = min(HGRN_C, tt)
    nt = seq // tt
    kern = functools.partial(_hgrn_kernel, tt=tt, c=c, layer=layer)
    col = lambda j: pl.BlockSpec((tt, w), lambda b, i, j=j: (b * nt + i, j))
    const = lambda shape: pl.BlockSpec(shape, lambda b, i: (0, 0))
    return pl.pallas_call(
        kern,
        grid=(bsz, nt),
        in_specs=[col(0), col(1), col(2), col(3),
                  const(lb_logits.shape), const((1, w)), const((tt, tt)), const((grp, grp))],
        out_specs=pl.BlockSpec((tt, w), lambda b, i: (b * nt + i, 0)),
        out_shape=jax.ShapeDtypeStruct((n, w), BF16),
        scratch_shapes=([pltpu.VMEM((w // LANES, LANES, LANES), F32)]
                        + [pltpu.VMEM((SUBLANES + tt, w), F32)] * 2
                        + [pltpu.VMEM((SUBLANES, tt // c, c, w), F32)] * 2),
        compiler_params=_params(("parallel", "arbitrary")),
        name="hgrn",
    )(rest, rest, rest, rest, lb_logits, norm_g.reshape(1, w), tri, bd)


PAIR_A = (0, 0, 0, 1, 1, 3)
PAIR_B = (1, 2, 3, 3, 2, 2)
N_PAIRS = len(PAIR_A)
N_CLASSES = N_GROUPS * N_PAIRS
assert sorted(tuple(sorted(p)) for p in zip(PAIR_A, PAIR_B)) == [
    (a, b) for a in range(EXPERTS_PER_GROUP) for b in range(a + 1, EXPERTS_PER_GROUP)]
ROUTE_CLS, ROUTE_WA, ROUTE_WB = 0, 1, 2


def _pair_slots(pidx):
    a = b = jnp.zeros_like(pidx)
    for p in range(N_PAIRS):
        a = jnp.where(pidx == p, float(PAIR_A[p]), a)
        b = jnp.where(pidx == p, float(PAIR_B[p]), b)
    return a, b


def _pair_index(lo, hi):
    pidx = jnp.zeros_like(lo)
    for p in range(N_PAIRS):
        is_p = jnp.logical_and(lo == min(PAIR_A[p], PAIR_B[p]), hi == max(PAIR_A[p], PAIR_B[p]))
        pidx = jnp.where(is_p, float(p), pidx)
    return pidx


def _route(lg):
    lane = lax.broadcasted_iota(jnp.int32, lg.shape, 1)
    neg = jnp.float32(-jnp.inf)
    big = jnp.int32(LANES)
    gmask = jnp.logical_and(lane >= N_EXPERTS, lane < N_EXPERTS + N_GROUPS)
    gl = jnp.where(gmask, lg, neg)
    gmax = jnp.max(gl, axis=1, keepdims=True)
    gidx = jnp.min(jnp.where(gl == gmax, lane, big), axis=1, keepdims=True) - N_EXPERTS
    w_grp = 1.0 / jnp.sum(jnp.where(gmask, jnp.exp(gl - gmax), 0.0), axis=1, keepdims=True)
    in_grp = jnp.logical_and(lane < N_EXPERTS, lane // EXPERTS_PER_GROUP == gidx)
    l1 = jnp.where(in_grp, lg, neg)
    v1 = jnp.max(l1, axis=1, keepdims=True)
    i1 = jnp.min(jnp.where(l1 == v1, lane, big), axis=1, keepdims=True)
    l2 = jnp.where(jnp.logical_and(in_grp, lane != i1), lg, neg)
    v2 = jnp.max(l2, axis=1, keepdims=True)
    i2 = jnp.min(jnp.where(l2 == v2, lane, big), axis=1, keepdims=True)
    e2 = jnp.exp(v2 - v1)
    p1 = 1.0 / (1.0 + e2)
    p2 = e2 * p1
    loc1 = (i1 - gidx * EXPERTS_PER_GROUP).astype(F32)
    loc2 = (i2 - gidx * EXPERTS_PER_GROUP).astype(F32)
    pidx = _pair_index(jnp.minimum(loc1, loc2), jnp.maximum(loc1, loc2))
    cls = gidx.astype(F32) * N_PAIRS + pidx
    a_loc, _ = _pair_slots(pidx)
    first_is_a = loc1 == a_loc
    wa = jnp.where(first_is_a, p1, p2) * w_grp
    wb = jnp.where(first_is_a, p2, p1) * w_grp
    return jnp.where(lane == ROUTE_CLS, cls,
                     jnp.where(lane == ROUTE_WA, wa, jnp.where(lane == ROUTE_WB, wb, 0.0)))


def _store_token_tiles(ref, x):
    rows, d = x.shape
    assert d == SUBLANES * LANES
    for s in range(SUBLANES):
        ref[pl.ds(s, rows, stride=SUBLANES), :] = x[:, s * LANES:(s + 1) * LANES]


def _load_token_tiles(ref):
    rows = ref.shape[0] // SUBLANES
    return jnp.concatenate([ref[pl.ds(s, rows, stride=SUBLANES), :] for s in range(SUBLANES)],
                           axis=1)


def _merge_kernel(x_ref, ysb_ref, yhg_ref, gsb_ref, ghg_ref, wbs_ref, wbh_ref, wo_ref,
                  ln_ref, wr_ref, br_ref, x1_ref, t_ref, route_ref):
    a = jnp.dot(ysb_ref[...], wbs_ref[...], preferred_element_type=F32)
    b = jnp.dot(yhg_ref[...], wbh_ref[...], preferred_element_type=F32)
    merged = jax.nn.sigmoid(gsb_ref[...]) * a + jax.nn.sigmoid(ghg_ref[...]) * b
    x1 = x_ref[...] + jnp.dot(merged.astype(BF16), wo_ref[...], preferred_element_type=F32)
    x1_ref[...] = x1
    var = jnp.mean(x1 * x1, axis=-1, keepdims=True)
    t = x1 * lax.rsqrt(var + EPS) * ln_ref[...]
    _store_token_tiles(t_ref, t)
    t_hi = t.astype(BF16)
    t_lo = (t - t_hi.astype(F32)).astype(BF16)
    p_hi = jnp.dot(t_hi, wr_ref[...], preferred_element_type=F32)
    p_lo = jnp.dot(t_lo, wr_ref[...], preferred_element_type=F32)
    lg = (p_hi[:, :LANES] + p_hi[:, LANES:]) + (p_lo[:, :LANES] + p_lo[:, LANES:]) + br_ref[...]
    route_ref[...] = _route(lg)


def _merge(x2, y_sb, y_hg, rest, wbs, wbh, wo, ln_g, wr, br):
    n, d = x2.shape
    tm = min(MERGE_TM, n)
    w_sb, w_hg = y_sb.shape[1], y_hg.shape[1]
    gate_blk = (rest.shape[1] - 2 * d) // d
    row = lambda wdt, j=0: pl.BlockSpec((tm, wdt), lambda i, j=j: (i, j))
    const = lambda shape: pl.BlockSpec(shape, lambda i: (0, 0))
    return pl.pallas_call(
        _merge_kernel,
        grid=(n // tm,),
        in_specs=[row(d), row(w_sb), row(w_hg), row(d, gate_blk), row(d, gate_blk + 1),
                  const(wbs.shape), const(wbh.shape), const(wo.shape), const((1, d)),
                  const(wr.shape), const((1, LANES))],
        out_specs=[row(d), pl.BlockSpec((tm * SUBLANES, LANES), lambda i: (i, 0)), row(LANES)],
        out_shape=[jax.ShapeDtypeStruct((n, d), F32),
                   jax.ShapeDtypeStruct((n * SUBLANES, LANES), F32),
                   jax.ShapeDtypeStruct((n, LANES), F32)],
        compiler_params=_params(("parallel",)),
        name="merge",
    )(x2, y_sb, y_hg, rest, rest, wbs, wbh, wo, ln_g.reshape(1, d), wr, br)


TOKEN_ROWS_OUT = 2 * SUBLANES


def _plan_kernel(route_ref, ltri_ref, utri_ref, pos_ref, te_ref, meta_ref,
                 tot_scr, run_scr, *, tile_rows):
    ph = pl.program_id(0)
    i = pl.program_id(1)
    route = route_ref[...]
    tm = route.shape[0]
    lane = lax.broadcasted_iota(jnp.int32, (tm, LANES), 1)
    is_cls = lane.astype(F32) == route[:, ROUTE_CLS:ROUTE_CLS + 1]
    sel = jnp.where(is_cls, 1.0, 0.0)

    @pl.when(jnp.logical_and(ph == 0, i == 0))
    def _():
        tot_scr[...] = jnp.zeros_like(tot_scr)

    @pl.when(ph == 0)
    def _():
        tot_scr[...] += jnp.sum(sel, axis=0, keepdims=True)

    @pl.when(ph == 1)
    def _():
        tot = tot_scr[...]
        top = tot + (tile_rows - 1.0)
        tiles = jnp.floor(top * (1.0 / tile_rows))
        tiles = jnp.where(tiles * tile_rows > top, tiles - 1.0, tiles)
        tiles = jnp.where((tiles + 1.0) * tile_rows <= top, tiles + 1.0, tiles)
        first_tile = _split_dot(jnp.broadcast_to(tiles, (SUBLANES, LANES)), utri_ref[...], 3)[0:1]

        @pl.when(i == 0)
        def _():
            run_scr[...] = jnp.zeros_like(run_scr)
            lane1 = lax.broadcasted_iota(jnp.int32, (1, LANES), 1)
            is_c = lane1 < N_CLASSES
            end_tile = first_tile + tiles
            n_valid = jnp.sum(jnp.where(is_c, tiles, 0.0), axis=1, keepdims=True)
            rows = te_ref.shape[0]
            j = lax.broadcasted_iota(jnp.int32, (rows, LANES), 0).astype(F32)
            j = jnp.minimum(j, n_valid - 1.0)
            lane2 = lax.broadcasted_iota(jnp.int32, (rows, LANES), 1)
            done = jnp.where(jnp.logical_and(lane2 < N_CLASSES, end_tile <= j), 1.0, 0.0)
            tc = jnp.sum(done, axis=1, keepdims=True)
            tg = jnp.floor((tc + 0.5) * (1.0 / N_PAIRS))
            ta, tb = _pair_slots(tc - tg * N_PAIRS)
            te = jnp.where(lane2 == 0, tg * EXPERTS_PER_GROUP + ta, tg * EXPERTS_PER_GROUP + tb)
            te_ref[...] = te.astype(jnp.int32)
            last_tile = jnp.where(jnp.logical_and(is_c, tiles > 0), end_tile - 1.0, -1.0)
            meta = jnp.where(lane1 == N_CLASSES, n_valid, last_tile)
            meta_ref[...] = jnp.broadcast_to(meta, meta_ref.shape).astype(jnp.int32)

        rank = jnp.dot(ltri_ref[...], sel.astype(BF16), preferred_element_type=F32) + run_scr[...]
        dest = jnp.sum(jnp.where(is_cls, first_tile * tile_rows + rank, 0.0),
                       axis=1, keepdims=True)
        pos_ref[...] = jnp.where(lane == 0, dest, 0.0).astype(jnp.int32)
        run_scr[...] += jnp.sum(sel, axis=0, keepdims=True)


def _plan(route, n_tiles_max):
    n = route.shape[0]
    tm = min(PLAN_TM, n)
    r = np.arange(tm)
    ltri = jnp.asarray((r[None, :] < r[:, None]).astype(np.float32), dtype=BF16)
    e = np.arange(LANES)
    utri = jnp.asarray((e[:, None] < e[None, :]).astype(np.float32), dtype=BF16)
    te_rows = -(-n_tiles_max // SUBLANES) * SUBLANES
    kern = functools.partial(_plan_kernel, tile_rows=MOE_TM)
    return pl.pallas_call(
        kern,
        grid=(2, n // tm),
        in_specs=[pl.BlockSpec((tm, LANES), lambda p, i: (i, 0)),
                  pl.BlockSpec((tm, tm), lambda p, i: (0, 0)),
                  pl.BlockSpec((LANES, LANES), lambda p, i: (0, 0))],
        out_specs=[pl.BlockSpec((tm, LANES), lambda p, i: (i * p, 0)),
                   pl.BlockSpec((te_rows, LANES), lambda p, i: (0, 0)),
                   pl.BlockSpec((SUBLANES, LANES), lambda p, i: (0, 0))],
        out_shape=[jax.ShapeDtypeStruct((n, LANES), jnp.int32),
                   jax.ShapeDtypeStruct((te_rows, LANES), jnp.int32),
                   jax.ShapeDtypeStruct((SUBLANES, LANES), jnp.int32)],
        scratch_shapes=[pltpu.VMEM((1, LANES), F32), pltpu.VMEM((1, LANES), F32)],
        compiler_params=_params(("arbitrary", "arbitrary")),
        name="plan",
    )(route, ltri, utri)


DMA_UNROLL = 8


def _token_copy(src_ref, r, dst_ref, p, sem, rows):
    return pltpu.make_async_copy(src_ref.at[pl.ds(pl.multiple_of(r * rows, rows), rows)],
                                 dst_ref.at[pl.ds(pl.multiple_of(p * rows, rows), rows)], sem)


def _dispatch_kernel(pos_ref, meta_ref, t_ref, xs_ref, zero_scr, sem, tail_sem, *, tm, tile_rows):
    i = pl.program_id(0)
    tile_rows = tile_rows * SUBLANES
    n_tiles = xs_ref.shape[0] // tile_rows
    n_valid = meta_ref[N_CLASSES]

    def clear(tile, s):
        return pltpu.make_async_copy(zero_scr, xs_ref.at[pl.ds(tile * tile_rows, tile_rows)], s)

    def for_each(clears, fn):
        for tile, cond in clears:
            @pl.when(cond)
            def _():
                fn(tile)

    last_tiles = [(meta_ref[c], meta_ref[c] >= 0) for c in range(N_CLASSES)]
    past_tiles = [(n_tiles - 1 - k, n_tiles - 1 - k >= n_valid) for k in range(N_CLASSES)]

    @pl.when(i == 0)
    def _():
        zero_scr[...] = jnp.zeros_like(zero_scr)
        for_each(last_tiles, lambda tile: clear(tile, sem).start())
        for_each(past_tiles, lambda tile: clear(tile, tail_sem).start())
        for_each(last_tiles, lambda tile: clear(tile, sem).wait())

    base = i * tm

    def start(r2, carry):
        for k in range(2):
            r = 2 * r2 + k
            _token_copy(t_ref, r, xs_ref, pos_ref[base + r], sem, SUBLANES).start(priority=k)
        return carry

    def wait(r, carry):
        _token_copy(t_ref, r, xs_ref, pos_ref[base + r], sem, SUBLANES).wait()
        return carry

    lax.fori_loop(0, tm // 2, start, 0, unroll=DMA_UNROLL)
    lax.fori_loop(0, tm, wait, 0, unroll=DMA_UNROLL)

    @pl.when(i == pl.num_programs(0) - 1)
    def _():
        for_each(past_tiles, lambda tile: clear(tile, tail_sem).wait())


def _dispatch(pos, meta, t, n_rows):
    n = t.shape[0] // SUBLANES
    tm = min(DISPATCH_TM, n)
    kern = functools.partial(_dispatch_kernel, tm=tm, tile_rows=MOE_TM)
    grid_spec = pltpu.PrefetchScalarGridSpec(
        num_scalar_prefetch=2,
        grid=(n // tm,),
        in_specs=[pl.BlockSpec((tm * SUBLANES, LANES), lambda i, *_: (i, 0))],
        out_specs=pl.BlockSpec(memory_space=pl.ANY),
        scratch_shapes=[pltpu.VMEM((MOE_TM * SUBLANES, LANES), F32),
                        pltpu.SemaphoreType.DMA(()), pltpu.SemaphoreType.DMA(())],
    )
    return pl.pallas_call(
        kern,
        grid_spec=grid_spec,
        out_shape=jax.ShapeDtypeStruct((n_rows * SUBLANES, LANES), F32),
        compiler_params=_params(("arbitrary",)),
        name="dispatch",
    )(pos, meta, t)


def _expert_kernel(tea_ref, teb_ref, nv_ref, xs_ref, wga, wua, wda, wgb, wub, wdb, ys_ref):
    in_use = pl.program_id(0) < nv_ref[0]
    rows = xs_ref.shape[0] // SUBLANES

    @pl.when(in_use)
    def _():
        x = _load_token_tiles(xs_ref).astype(BF16)
        for slot, (wg, wu, wd) in enumerate(((wga, wua, wda), (wgb, wub, wdb))):
            hg = jnp.dot(x, wg[...].astype(BF16), preferred_element_type=F32)
            hu = jnp.dot(x, wu[...].astype(BF16), preferred_element_type=F32)
            a = (hg * jax.nn.sigmoid(hg) * hu).astype(BF16)
            y = jnp.dot(a, wd[...].astype(BF16), preferred_element_type=F32)
            for s in range(SUBLANES):
                ys_ref[pl.ds(slot * SUBLANES + s, rows, stride=TOKEN_ROWS_OUT), :] = (
                    y[:, s * LANES:(s + 1) * LANES])

    @pl.when(jnp.logical_not(in_use))
    def _():
        ys_ref[...] = jnp.zeros_like(ys_ref)


def _experts(tea, teb, nv, xs, wg, wu, wd):
    _, d, de = wg.shape
    assert d == SUBLANES * LANES
    n_tiles = xs.shape[0] // (MOE_TM * SUBLANES)
    wa = lambda shape: pl.BlockSpec(shape, lambda j, tea, teb, nv: (tea[j], 0, 0))
    wb = lambda shape: pl.BlockSpec(shape, lambda j, tea, teb, nv: (teb[j], 0, 0))
    grid_spec = pltpu.PrefetchScalarGridSpec(
        num_scalar_prefetch=3,
        grid=(n_tiles,),
        in_specs=[pl.BlockSpec((MOE_TM * SUBLANES, LANES),
                               lambda j, tea, teb, nv: (jnp.minimum(j, nv[0] - 1), 0)),
                  wa((None, d, de)), wa((None, d, de)), wa((None, de, d)),
                  wb((None, d, de)), wb((None, d, de)), wb((None, de, d))],
        out_specs=pl.BlockSpec((MOE_TM * TOKEN_ROWS_OUT, LANES), lambda j, *_: (j, 0)),
    )
    return pl.pallas_call(
        _expert_kernel,
        grid_spec=grid_spec,
        out_shape=jax.ShapeDtypeStruct((n_tiles * MOE_TM * TOKEN_ROWS_OUT, LANES), F32),
        compiler_params=_params(("arbitrary",)),
        name="experts",
    )(tea, teb, nv, xs, wg, wu, wd, wg, wu, wd)


def _combine_kernel(pos_ref, ys_ref, x1_ref, route_ref, fg_ref, o_ref, g, sem, *, tm):
    i = pl.program_id(0)
    slot = i % 2

    def gather(step, buf, wait):
        base = step * tm

        def body(r2, carry):
            for k in range(2):
                r = 2 * r2 + k
                c = _token_copy(ys_ref, pos_ref[base + r], g.at[buf], r, sem.at[buf],
                                TOKEN_ROWS_OUT)
                if wait:
                    c.wait()
                else:
                    c.start(priority=k)
            return carry

        lax.fori_loop(0, tm // 2, body, 0, unroll=DMA_UNROLL)

    @pl.when(i == 0)
    def _():
        gather(0, 0, wait=False)

    @pl.when(i + 1 < pl.num_programs(0))
    def _():
        gather(i + 1, 1 - slot, wait=False)

    gather(i, slot, wait=True)
    gs = g.at[slot]
    ya, yb = (jnp.concatenate([gs[pl.ds(off + s, tm, stride=TOKEN_ROWS_OUT), :]
                               for s in range(SUBLANES)], axis=1) for off in (0, SUBLANES))
    route = route_ref[...]
    x2 = x1_ref[...] + (route[:, ROUTE_WA:ROUTE_WA + 1] * ya + route[:, ROUTE_WB:ROUTE_WB + 1] * yb)
    var = jnp.mean(x2 * x2, axis=-1, keepdims=True)
    o_ref[...] = x2 * lax.rsqrt(var + EPS) * fg_ref[...]


def _combine(pos, ys, x1, route, final_g):
    n, d = x1.shape
    tm = min(COMBINE_TM, n)
    kern = functools.partial(_combine_kernel, tm=tm)
    grid_spec = pltpu.PrefetchScalarGridSpec(
        num_scalar_prefetch=1,
        grid=(n // tm,),
        in_specs=[pl.BlockSpec(memory_space=pl.ANY),
                  pl.BlockSpec((tm, d), lambda i, *_: (i, 0)),
                  pl.BlockSpec((tm, LANES), lambda i, *_: (i, 0)),
                  pl.BlockSpec((1, d), lambda i, *_: (0, 0))],
        out_specs=pl.BlockSpec((tm, d), lambda i, *_: (i, 0)),
        scratch_shapes=[pltpu.VMEM((2, tm * TOKEN_ROWS_OUT, LANES), F32),
                        pltpu.SemaphoreType.DMA((2,))],
    )
    return pl.pallas_call(
        kern,
        grid_spec=grid_spec,
        out_shape=jax.ShapeDtypeStruct((n, d), F32),
        compiler_params=_params(("arbitrary",)),
        name="combine",
    )(pos, ys, x1, route, final_g.reshape(1, d))


def _moe_sparse(t, route, wg, wu, wd, x1, final_g):
    n = x1.shape[0]
    n_tiles_max = (n + N_CLASSES * (MOE_TM - 1)) // MOE_TM
    pos, te, meta = _plan(route, n_tiles_max)
    pos = pos[:, 0]
    xs = _dispatch(pos, meta[0, :N_CLASSES + 1], t, n_tiles_max * MOE_TM)
    ys = _experts(te[:n_tiles_max, 0], te[:n_tiles_max, 1], meta[0, N_CLASSES:N_CLASSES + 1],
                  xs, wg, wu, wd)
    return _combine(pos, ys, x1, route, final_g)


def _suffix_ones(t):
    j = np.arange(t)[:, None]
    s = np.arange(t)[None, :]
    return jnp.asarray((j > s).astype(np.float32), dtype=BF16)


def _causal_bias(t):
    row = np.arange(t)[:, None]
    col = np.arange(t)[None, :]
    diag = np.where(col < row, 0.0, MASK_BIAS).astype(np.float32)
    return jnp.asarray(np.stack([np.zeros_like(diag), diag]))


def _chunk_prefix_ones(tt, c):
    t = np.arange(tt)[:, None]
    j = np.arange(tt)[None, :]
    return jnp.asarray(((j <= t) & (t // c == j // c)).astype(np.float32), dtype=BF16)


def _block_diag_ones(w, blk):
    a = np.arange(w)
    return jnp.asarray((a[:, None] // blk == a[None, :] // blk).astype(np.float32), dtype=BF16)


def kernel(x, ln1_g, w_in, w_branch_sb, w_branch_hg, hg_norm_g, hg_lb_logits, w_out, ln2_g,
           w_router_group, b_router_group, w_router_expert, b_router_expert,
           w_exp_gate, w_exp_up, w_exp_down, final_g):
    bsz, seq, d = x.shape
    depth = w_in.shape[0]
    n = bsz * seq
    sb_width = SB_HEADS * SB_HEAD_DIM
    hg_width = HG_HEADS * HG_DIM

    tri_attn = _suffix_ones(min(ATTN_T, seq))
    bias_attn = _causal_bias(min(ATTN_T, seq))
    tt = min(HGRN_TT, seq)
    tri_hg = _chunk_prefix_ones(tt, min(HGRN_C, tt))
    bd = _block_diag_ones(min(MXU_DIM, hg_width), HG_DIM)

    x2 = x.reshape(n, d)
    for l in range(depth):
        qkv, rest = _inproj(x2, ln1_g[l], w_in[l], sb_width)
        y_sb = _attn(qkv, bsz, seq, tri_attn, bias_attn)
        y_hg = _hgrn(rest, hg_lb_logits, hg_norm_g[l], bsz, seq, l, tri_hg, bd)

        pad = LANES - N_EXPERTS - N_GROUPS
        wr = jnp.concatenate([w_router_expert[l], w_router_group[l],
                              jnp.zeros((d, pad), F32)], axis=1)
        wr_hi = wr.astype(BF16)
        wr_lo = (wr - wr_hi.astype(F32)).astype(BF16)
        wr_split = jnp.concatenate([wr_hi, wr_lo], axis=1)
        br = jnp.concatenate([b_router_expert[l], b_router_group[l],
                              jnp.zeros((pad,), F32)]).reshape(1, LANES)

        last = l == depth - 1
        x1, t, route = _merge(x2, y_sb, y_hg, rest, w_branch_sb[l].astype(BF16),
                              w_branch_hg[l].astype(BF16), w_out[l].astype(BF16), ln2_g[l],
                              wr_split, br)
        assert last, "final rmsnorm is fused into the last layer's combine kernel"
        x2 = _moe_sparse(t, route, w_exp_gate[l], w_exp_up[l], w_exp_down[l], x1, final_g)
    return x2.reshape(bsz, seq, d)
```

```python
import functools

import jax
import jax.numpy as jnp
import numpy as np
from jax import lax
from jax.experimental import pallas as pl
from jax.experimental.pallas import tpu as pltpu

F32 = jnp.float32
BF16 = jnp.bfloat16

EPS = 1e-6
SB_HEADS = 8
SB_HEAD_DIM = 64
HG_HEADS = 8
HG_DIM = 64
N_GROUPS = 4
EXPERTS_PER_GROUP = 4
N_EXPERTS = N_GROUPS * EXPERTS_PER_GROUP

LANES = 128
SUBLANES = 8
MXU_DIM = 256
LOG2E = 1.4426950408889634
VMEM_LIMIT = 48 * 1024 * 1024

INPROJ_TM = 2048
INPROJ_TN = 512
ATTN_T = MXU_DIM
HGRN_TT = 256
HGRN_C = 32
MERGE_TM = 512
MOE_TM = 384
PLAN_TM = 1024
DISPATCH_TM = 1024
COMBINE_TM = 512


def _params(sem):
    return pltpu.CompilerParams(dimension_semantics=sem, vmem_limit_bytes=VMEM_LIMIT)


def _split_dot(x, m, passes):
    acc = None
    r = x
    for p in range(passes):
        h = r.astype(BF16)
        term = jnp.dot(h, m, preferred_element_type=F32)
        acc = term if acc is None else acc + term
        if p + 1 < passes:
            r = r - h.astype(F32)
    return acc


def _inproj_kernel(x_ref, g_ref, w_ref, qkv_ref, rest_ref, h_scr, *, q_scale):
    j = pl.program_id(1)

    @pl.when(j == 0)
    def _():
        x = x_ref[...]
        var = jnp.mean(x * x, axis=-1, keepdims=True)
        h_scr[...] = (x * lax.rsqrt(var + EPS) * g_ref[...]).astype(BF16)

    acc = jnp.dot(h_scr[...], w_ref[...].astype(BF16), preferred_element_type=F32)
    qkv_ref[...] = (acc * jnp.where(j == 0, q_scale, 1.0)).astype(BF16)
    rest_ref[...] = acc


def _inproj(x2, ln_g, w_in, sb_width):
    n, d = x2.shape
    cols = w_in.shape[1]
    tm, tn = min(INPROJ_TM, n), INPROJ_TN
    assert sb_width == tn, "q block must be exactly one column tile"
    nq = 3 * sb_width // tn
    nj = cols // tn
    kern = functools.partial(_inproj_kernel, q_scale=SB_HEAD_DIM ** -0.5)
    return pl.pallas_call(
        kern,
        grid=(n // tm, nj),
        in_specs=[
            pl.BlockSpec((tm, d), lambda i, j: (i, 0)),
            pl.BlockSpec((1, d), lambda i, j: (0, 0)),
            pl.BlockSpec((d, tn), lambda i, j: (0, j)),
        ],
        out_specs=[
            pl.BlockSpec((tm, tn), lambda i, j: (i, jnp.minimum(j, nq))),
            pl.BlockSpec((tm, tn), lambda i, j: (i, jnp.maximum(j - nq, 0))),
        ],
        out_shape=[
            jax.ShapeDtypeStruct((n, (nq + 1) * tn), BF16),
            jax.ShapeDtypeStruct((n, cols - nq * tn), F32),
        ],
        scratch_shapes=[pltpu.VMEM((tm, d), BF16)],
        compiler_params=_params(("parallel", "arbitrary")),
        name="inproj",
    )(x2, ln_g.reshape(1, d), w_in)


ATTN_STAGES = 3
ATTN_STREAMS = 2
MASK_BIAS = -1e30
ATTN_SKIP = 111.0


def _attn_kernel(q_ref, k_ref, v_ref, tri_ref, bias_ref, o_ref,
                 z0, z1, z2, i0, i1, i2, acc_ref, c_ref, *, t, nq):
    zbuf = (z0, z1, z2)
    ibuf = (i0, i1, i2)
    for r in zbuf + ibuf + (acc_ref, c_ref):
        r[...] = jnp.zeros_like(r)

    lane = lax.broadcasted_iota(jnp.int32, (t, LANES), 1)
    head0 = lane < SB_HEAD_DIM
    nt = (((1,), (1,)), ((), ()))

    def stage_a(qi, kj, slot):
        q = q_ref[pl.ds(pl.multiple_of(qi * t, t), t), :]
        zero = jnp.zeros_like(q)
        q2 = jnp.concatenate([jnp.where(head0, q, zero), jnp.where(head0, zero, q)], axis=0)
        k = k_ref[pl.ds(pl.multiple_of(kj * t, t), t), :]
        z = lax.dot_general(q2, k, nt, preferred_element_type=F32)
        bias = bias_ref[(qi == kj).astype(jnp.int32)]
        zbuf[slot][...] = z + jnp.concatenate([bias, bias], axis=0)

    def stage_b(slot):
        z = zbuf[slot][...]
        p = jnp.maximum(z, 0.0) + jnp.log(1.0 + jnp.exp2(jnp.abs(z) * (-LOG2E)))
        incl = jnp.dot(p.astype(BF16), tri_ref[...], preferred_element_type=F32) + p
        ibuf[slot][...] = incl
        return incl[:, 0:1]

    def stage_c(qi, kj, slot, stream):
        first = qi == kj
        incl = ibuf[slot][...]
        c = jnp.where(first, 0.0, c_ref[stream])
        a = jnp.exp2((zbuf[slot][...] - incl - c) * LOG2E)
        v = v_ref[pl.ds(pl.multiple_of(kj * t, t), t), :]
        pv = jnp.dot(a.astype(BF16), v, preferred_element_type=F32)
        acc = jnp.where(first, pv, acc_ref[stream] + pv)
        acc_ref[stream] = acc
        c_ref[stream] = c + incl[:, 0:1]
        o_ref[pl.ds(pl.multiple_of(qi * t, t), t), :] = (
            jnp.where(head0, acc[0:t], acc[t:2 * t]).astype(o_ref.dtype))

    def block(carry):
        qs, ks, qn, kn, skip_q, drained = carry
        qs, ks, qn, kn, skip_q = list(qs), list(ks), list(qn), list(kn), list(skip_q)
        all_done = qn[0] >= nq
        for p in range(1, ATTN_STREAMS):
            all_done = jnp.logical_and(all_done, qn[p] >= nq)
        drained = drained + all_done.astype(jnp.int32)
        for r in range(ATTN_STAGES * ATTN_STREAMS):
            sa, sc, sb = r % ATTN_STAGES, (r + 1) % ATTN_STAGES, (r + 2) % ATTN_STAGES
            pa, pb = r % ATTN_STREAMS, (r - 1) % ATTN_STREAMS
            stage_c(qs[sc], ks[sc], sc, pa)
            total_b = stage_b(sb)
            carry_b = jnp.where(qs[sb] == ks[sb], 0.0, c_ref[pb]) + total_b
            skip = skip_q[pa] == qn[pa]
            q_cur = jnp.where(skip, qn[pa] + ATTN_STREAMS, qn[pa])
            k_cur = jnp.where(skip, qn[pa] + ATTN_STREAMS, kn[pa])
            drain = q_cur >= nq
            qa = jnp.where(drain, 0, q_cur)
            ka = jnp.where(drain, 0, k_cur)
            stage_a(qa, ka, sa)
            last = k_cur == 0
            qn[pa] = jnp.where(jnp.logical_and(last, jnp.logical_not(drain)),
                               q_cur + ATTN_STREAMS, q_cur)
            kn[pa] = jnp.where(drain, k_cur, jnp.where(last, q_cur + ATTN_STREAMS, k_cur - 1))
            skip_q[pb] = jnp.where(jnp.min(carry_b) >= ATTN_SKIP, qs[sb], skip_q[pb])
            qs[sa], ks[sa] = qa, ka
        return tuple(qs), tuple(ks), tuple(qn), tuple(kn), tuple(skip_q), drained

    zero = jnp.int32(0)
    first_q = tuple(jnp.int32(p) for p in range(ATTN_STREAMS))
    init = ((zero,) * ATTN_STAGES, (zero,) * ATTN_STAGES, first_q, first_q,
            (jnp.int32(-1),) * ATTN_STREAMS, zero)
    lax.while_loop(lambda carry: carry[5] < 1, block, init)


def _attn(qkv, bsz, seq, tri, bias):
    n = bsz * seq
    t = min(ATTN_T, seq)
    pairs = SB_HEADS * SB_HEAD_DIM // LANES
    kern = functools.partial(_attn_kernel, t=t, nq=seq // t)
    return pl.pallas_call(
        kern,
        grid=(bsz, pairs),
        in_specs=[
            pl.BlockSpec((seq, LANES), lambda b, p: (b, p)),
            pl.BlockSpec((seq, LANES), lambda b, p: (b, pairs + p)),
            pl.BlockSpec((seq, LANES), lambda b, p: (b, 2 * pairs + p)),
            pl.BlockSpec((t, t), lambda b, p: (0, 0)),
            pl.BlockSpec((2, t, t), lambda b, p: (0, 0, 0)),
        ],
        out_specs=pl.BlockSpec((seq, LANES), lambda b, p: (b, p)),
        out_shape=jax.ShapeDtypeStruct((n, pairs * LANES), BF16),
        scratch_shapes=([pltpu.VMEM((2 * t, t), F32)] * (2 * ATTN_STAGES)
                        + [pltpu.VMEM((ATTN_STREAMS, 2 * t, LANES), F32),
                           pltpu.VMEM((ATTN_STREAMS, 2 * t, 1), F32)]),
        compiler_params=_params(("parallel", "parallel")),
        name="attn",
    )(qkv, qkv, qkv, tri, bias)


def _group_dot(x, bd):
    g = bd.shape[0]
    parts = [jnp.dot(x[:, i:i + g], bd, preferred_element_type=F32)
             for i in range(0, x.shape[1], g)]
    return jnp.concatenate(parts, axis=1)


def _hgrn_kernel(q_ref, f_ref, i_ref, g_ref, lbl_ref, ng_ref, tri_ref, bd_ref, o_ref,
                 st_scr, wpad, vpad, wsh, vsh, *, tt, c, layer):
    ti = pl.program_id(1)
    w = q_ref.shape[1]
    nch = tt // c

    @pl.when(ti == 0)
    def _():
        st_scr[...] = jnp.zeros_like(st_scr)

    lg = lbl_ref[...]
    e = jnp.exp(lg - jnp.max(lg, axis=0, keepdims=True))
    lb = jnp.sum(e[0:layer + 1], axis=0, keepdims=True) / jnp.sum(e, axis=0, keepdims=True)

    f = lb + (1.0 - lb) * jax.nn.sigmoid(f_ref[...])
    kk = 1.0 - f
    qv = q_ref[...]
    qs = qv * jax.nn.sigmoid(qv)
    v = i_ref[...]
    bd = bd_ref[...]
    cum = _split_dot_left(tri_ref[...], jnp.log(f) * LOG2E, 3)
    wk = jnp.log(kk) * LOG2E - cum

    pos = lax.broadcasted_iota(jnp.int32, (tt, 1), 0) % c
    wpad[0:SUBLANES, :] = jnp.zeros((SUBLANES, w), F32)
    vpad[0:SUBLANES, :] = jnp.zeros((SUBLANES, w), F32)
    wpad[SUBLANES:SUBLANES + tt, :] = wk
    vpad[SUBLANES:SUBLANES + tt, :] = v
    for b in range(SUBLANES):
        wb = wpad[SUBLANES - b:SUBLANES - b + tt, :]
        vb = vpad[SUBLANES - b:SUBLANES - b + tt, :]
        if b:
            wb = jnp.where(pos >= b, wb, -jnp.inf)
        wsh[b] = wb.reshape(nch, c, w)
        vsh[b] = vb.reshape(nch, c, w)

    qs3 = qs.reshape(nch, c, w)
    cum3 = cum.reshape(nch, c, w)
    acc = None
    for a in reversed(range(c // SUBLANES)):
        rows = c - SUBLANES * a
        qa = qs3[:, SUBLANES * a:, :].reshape(nch * rows, w)
        ca = cum3[:, SUBLANES * a:, :].reshape(nch * rows, w)
        acc_a = None
        for b in range(SUBLANES):
            wb = wsh[b, :, 0:rows, :].reshape(nch * rows, w)
            vb = vsh[b, :, 0:rows, :].reshape(nch * rows, w)
            dd = qa * jnp.exp2(ca + wb)
            term = _group_dot(dd.astype(BF16), bd) * vb
            acc_a = term if acc_a is None else acc_a + term
        acc_a = acc_a.reshape(nch, rows, w)
        if acc is not None:
            acc_a = acc_a + jnp.concatenate([jnp.zeros((nch, SUBLANES, w), F32), acc], axis=1)
        acc = acc_a
    acc = acc.reshape(tt, w)

    grp = st_scr.shape[1]
    bdmask = bd[0:grp, 0:grp] != 0
    outs = []
    for ci in range(nch):
        r0 = ci * c
        cum_c = cum[r0:r0 + c]
        last = cum_c[c - 1:c]
        qd = (qs[r0:r0 + c] * jnp.exp2(cum_c)).astype(BF16)
        kd = (kk[r0:r0 + c] * jnp.exp2(last - cum_c)).astype(BF16)
        vc = v[r0:r0 + c].astype(BF16)
        dec = jnp.exp2(last)
        o_parts = []
        for gi in range(w // grp):
            sl = slice(gi * grp, (gi + 1) * grp)
            st = st_scr[gi]
            o_parts.append(lax.dot_general(qd[:, sl], st.astype(BF16), (((1,), (1,)), ((), ())),
                                           preferred_element_type=F32))
            upd = lax.dot_general(vc[:, sl], kd[:, sl], (((0,), (0,)), ((), ())),
                                  preferred_element_type=F32)
            st_scr[gi] = st * dec[:, sl] + jnp.where(bdmask, upd, 0.0)
        outs.append(jnp.concatenate(o_parts, axis=1))
    o = acc + jnp.concatenate(outs, axis=0)

    o2 = o * o
    o2_hi = o2.astype(BF16)
    o2_lo = (o2 - o2_hi.astype(F32)).astype(BF16)
    ms = (_group_dot(o2_hi, bd) + _group_dot(o2_lo, bd)) * (1.0 / HG_DIM)
    gv = g_ref[...]
    o = o * lax.rsqrt(ms + EPS) * ng_ref[...] * (gv * jax.nn.sigmoid(gv))
    o_ref[...] = o.astype(o_ref.dtype)


def _split_dot_left(m, x, passes):
    acc = None
    r = x
    for p in range(passes):
        h = r.astype(BF16)
        term = jnp.dot(m, h, preferred_element_type=F32)
        acc = term if acc is None else acc + term
        if p + 1 < passes:
            r = r - h.astype(F32)
    return acc


def _hgrn(rest, lb_logits, norm_g, bsz, seq, layer, tri, bd):
    n = bsz * seq
    w = HG_HEADS * HG_DIM
    grp = bd.shape[0]
    tt = min(HGRN_TT, seq)
    c = min(HGRN_C, tt)
    nt = seq // tt
    kern = functools.partial(_hgrn_kernel, tt=tt, c=c, layer=layer)
    col = lambda j: pl.BlockSpec((tt, w), lambda b, i, j=j: (b * nt + i, j))
    const = lambda shape: pl.BlockSpec(shape, lambda b, i: (0, 0))
    return pl.pallas_call(
        kern,
        grid=(bsz, nt),
        in_specs=[col(0), col(1), col(2), col(3),
                  const(lb_logits.shape), const((1, w)), const((tt, tt)), const((grp, grp))],
        out_specs=pl.BlockSpec((tt, w), lambda b, i: (b * nt + i, 0)),
        out_shape=jax.ShapeDtypeStruct((n, w), BF16),
        scratch_shapes=([pltpu.VMEM((w // LANES, LANES, LANES), F32)]
                        + [pltpu.VMEM((SUBLANES + tt, w), F32)] * 2
                        + [pltpu.VMEM((SUBLANES, tt // c, c, w), F32)] * 2),
        compiler_params=_params(("parallel", "arbitrary")),
        name="hgrn",
    )(rest, rest, rest, rest, lb_logits, norm_g.reshape(1, w), tri, bd)


PAIR_A = (0, 0, 0, 1, 1, 3)
PAIR_B = (1, 2, 3, 3, 2, 2)
N_PAIRS = len(PAIR_A)
N_CLASSES = N_GROUPS * N_PAIRS
assert sorted(tuple(sorted(p)) for p in zip(PAIR_A, PAIR_B)) == [
    (a, b) for a in range(EXPERTS_PER_GROUP) for b in range(a + 1, EXPERTS_PER_GROUP)]
ROUTE_CLS, ROUTE_WA, ROUTE_WB = 0, 1, 2


def _pair_slots(pidx):
    a = b = jnp.zeros_like(pidx)
    for p in range(N_PAIRS):
        a = jnp.where(pidx == p, float(PAIR_A[p]), a)
        b = jnp.where(pidx == p, float(PAIR_B[p]), b)
    return a, b


def _pair_index(lo, hi):
    pidx = jnp.zeros_like(lo)
    for p in range(N_PAIRS):
        is_p = jnp.logical_and(lo == min(PAIR_A[p], PAIR_B[p]), hi == max(PAIR_A[p], PAIR_B[p]))
        pidx = jnp.where(is_p, float(p), pidx)
    return pidx


def _route(lg):
    lane = lax.broadcasted_iota(jnp.int32, lg.shape, 1)
    neg = jnp.float32(-jnp.inf)
    big = jnp.int32(LANES)
    gmask = jnp.logical_and(lane >= N_EXPERTS, lane < N_EXPERTS + N_GROUPS)
    gl = jnp.where(gmask, lg, neg)
    gmax = jnp.max(gl, axis=1, keepdims=True)
    gidx = jnp.min(jnp.where(gl == gmax, lane, big), axis=1, keepdims=True) - N_EXPERTS
    w_grp = 1.0 / jnp.sum(jnp.where(gmask, jnp.exp(gl - gmax), 0.0), axis=1, keepdims=True)
    in_grp = jnp.logical_and(lane < N_EXPERTS, lane // EXPERTS_PER_GROUP == gidx)
    l1 = jnp.where(in_grp, lg, neg)
    v1 = jnp.max(l1, axis=1, keepdims=True)
    i1 = jnp.min(jnp.where(l1 == v1, lane, big), axis=1, keepdims=True)
    l2 = jnp.where(jnp.logical_and(in_grp, lane != i1), lg, neg)
    v2 = jnp.max(l2, axis=1, keepdims=True)
    i2 = jnp.min(jnp.where(l2 == v2, lane, big), axis=1, keepdims=True)
    e2 = jnp.exp(v2 - v1)
    p1 = 1.0 / (1.0 + e2)
    p2 = e2 * p1
    loc1 = (i1 - gidx * EXPERTS_PER_GROUP).astype(F32)
    loc2 = (i2 - gidx * EXPERTS_PER_GROUP).astype(F32)
    pidx = _pair_index(jnp.minimum(loc1, loc2), jnp.maximum(loc1, loc2))
    cls = gidx.astype(F32) * N_PAIRS + pidx
    a_loc, _ = _pair_slots(pidx)
    first_is_a = loc1 == a_loc
    wa = jnp.where(first_is_a, p1, p2) * w_grp
    wb = jnp.where(first_is_a, p2, p1) * w_grp
    return jnp.where(lane == ROUTE_CLS, cls,
                     jnp.where(lane == ROUTE_WA, wa, jnp.where(lane == ROUTE_WB, wb, 0.0)))


def _store_token_tiles(ref, x):
    rows, d = x.shape
    assert d == SUBLANES * LANES
    for s in range(SUBLANES):
        ref[pl.ds(s, rows, stride=SUBLANES), :] = x[:, s * LANES:(s + 1) * LANES]


def _load_token_tiles(ref):
    rows = ref.shape[0] // SUBLANES
    return jnp.concatenate([ref[pl.ds(s, rows, stride=SUBLANES), :] for s in range(SUBLANES)],
                           axis=1)


def _merge_kernel(x_ref, ysb_ref, yhg_ref, gsb_ref, ghg_ref, wbs_ref, wbh_ref, wo_ref,
                  ln_ref, wr_ref, br_ref, x1_ref, t_ref, route_ref, wbs_scr, wbh_scr, wo_scr):
    @pl.when(pl.program_id(0) == 0)
    def _():
        wbs_scr[...] = wbs_ref[...].astype(BF16)
        wbh_scr[...] = wbh_ref[...].astype(BF16)
        wo_scr[...] = wo_ref[...].astype(BF16)

    a = jnp.dot(ysb_ref[...], wbs_scr[...], preferred_element_type=F32)
    b = jnp.dot(yhg_ref[...], wbh_scr[...], preferred_element_type=F32)
    merged = jax.nn.sigmoid(gsb_ref[...]) * a + jax.nn.sigmoid(ghg_ref[...]) * b
    x1 = x_ref[...] + jnp.dot(merged.astype(BF16), wo_scr[...], preferred_element_type=F32)
    x1_ref[...] = x1
    var = jnp.mean(x1 * x1, axis=-1, keepdims=True)
    t = x1 * lax.rsqrt(var + EPS) * ln_ref[...]
    _store_token_tiles(t_ref, t)
    t_hi = t.astype(BF16)
    t_lo = (t - t_hi.astype(F32)).astype(BF16)
    p_hi = jnp.dot(t_hi, wr_ref[...], preferred_element_type=F32)
    p_lo = jnp.dot(t_lo, wr_ref[...], preferred_element_type=F32)
    lg = (p_hi[:, :LANES] + p_hi[:, LANES:]) + (p_lo[:, :LANES] + p_lo[:, LANES:]) + br_ref[...]
    route_ref[...] = _route(lg)


def _merge(x2, y_sb, y_hg, rest, wbs, wbh, wo, ln_g, wr, br):
    n, d = x2.shape
    tm = min(MERGE_TM, n)
    w_sb, w_hg = y_sb.shape[1], y_hg.shape[1]
    gate_blk = (rest.shape[1] - 2 * d) // d
    row = lambda wdt, j=0: pl.BlockSpec((tm, wdt), lambda i, j=j: (i, j))
    const = lambda shape: pl.BlockSpec(shape, lambda i: (0, 0))
    return pl.pallas_call(
        _merge_kernel,
        grid=(n // tm,),
        in_specs=[row(d), row(w_sb), row(w_hg), row(d, gate_blk), row(d, gate_blk + 1),
                  const(wbs.shape), const(wbh.shape), const(wo.shape), const((1, d)),
                  const(wr.shape), const((1, LANES))],
        out_specs=[row(d), pl.BlockSpec((tm * SUBLANES, LANES), lambda i: (i, 0)), row(LANES)],
        out_shape=[jax.ShapeDtypeStruct((n, d), F32),
                   jax.ShapeDtypeStruct((n * SUBLANES, LANES), F32),
                   jax.ShapeDtypeStruct((n, LANES), F32)],
        scratch_shapes=[pltpu.VMEM(w.shape, BF16) for w in (wbs, wbh, wo)],
        compiler_params=_params(("arbitrary",)),
        name="merge",
    )(x2, y_sb, y_hg, rest, rest, wbs, wbh, wo, ln_g.reshape(1, d), wr, br)


TOKEN_ROWS_OUT = 2 * SUBLANES


def _plan_kernel(route_ref, ltri_ref, utri_ref, pos_ref, te_ref, meta_ref,
                 tot_scr, run_scr, *, tile_rows):
    ph = pl.program_id(0)
    i = pl.program_id(1)
    route = route_ref[...]
    tm = route.shape[0]
    lane = lax.broadcasted_iota(jnp.int32, (tm, LANES), 1)
    is_cls = lane.astype(F32) == route[:, ROUTE_CLS:ROUTE_CLS + 1]
    sel = jnp.where(is_cls, 1.0, 0.0)

    @pl.when(jnp.logical_and(ph == 0, i == 0))
    def _():
        tot_scr[...] = jnp.zeros_like(tot_scr)

    @pl.when(ph == 0)
    def _():
        tot_scr[...] += jnp.sum(sel, axis=0, keepdims=True)

    @pl.when(ph == 1)
    def _():
        tot = tot_scr[...]
        top = tot + (tile_rows - 1.0)
        tiles = jnp.floor(top * (1.0 / tile_rows))
        tiles = jnp.where(tiles * tile_rows > top, tiles - 1.0, tiles)
        tiles = jnp.where((tiles + 1.0) * tile_rows <= top, tiles + 1.0, tiles)
        first_tile = _split_dot(jnp.broadcast_to(tiles, (SUBLANES, LANES)), utri_ref[...], 3)[0:1]

        @pl.when(i == 0)
        def _():
            run_scr[...] = jnp.zeros_like(run_scr)
            lane1 = lax.broadcasted_iota(jnp.int32, (1, LANES), 1)
            is_c = lane1 < N_CLASSES
            end_tile = first_tile + tiles
            n_valid = jnp.sum(jnp.where(is_c, tiles, 0.0), axis=1, keepdims=True)
            rows = te_ref.shape[0]
            j = lax.broadcasted_iota(jnp.int32, (rows, LANES), 0).astype(F32)
            j = jnp.minimum(j, n_valid - 1.0)
            lane2 = lax.broadcasted_iota(jnp.int32, (rows, LANES), 1)
            done = jnp.where(jnp.logical_and(lane2 < N_CLASSES, end_tile <= j), 1.0, 0.0)
            tc = jnp.sum(done, axis=1, keepdims=True)
            tg = jnp.floor((tc + 0.5) * (1.0 / N_PAIRS))
            ta, tb = _pair_slots(tc - tg * N_PAIRS)
            te = jnp.where(lane2 == 0, tg * EXPERTS_PER_GROUP + ta, tg * EXPERTS_PER_GROUP + tb)
            te_ref[...] = te.astype(jnp.int32)
            last_tile = jnp.where(jnp.logical_and(is_c, tiles > 0), end_tile - 1.0, -1.0)
            meta = jnp.where(lane1 == N_CLASSES, n_valid, last_tile)
            meta_ref[...] = jnp.broadcast_to(meta, meta_ref.shape).astype(jnp.int32)

        rank = jnp.dot(ltri_ref[...], sel.astype(BF16), preferred_element_type=F32) + run_scr[...]
        dest = jnp.sum(jnp.where(is_cls, first_tile * tile_rows + rank, 0.0),
                       axis=1, keepdims=True)
        pos_ref[...] = jnp.where(lane == 0, dest, 0.0).astype(jnp.int32)
        run_scr[...] += jnp.sum(sel, axis=0, keepdims=True)


def _plan(route, n_tiles_max):
    n = route.shape[0]
    tm = min(PLAN_TM, n)
    r = np.arange(tm)
    ltri = jnp.asarray((r[None, :] < r[:, None]).astype(np.float32), dtype=BF16)
    e = np.arange(LANES)
    utri = jnp.asarray((e[:, None] < e[None, :]).astype(np.float32), dtype=BF16)
    te_rows = -(-n_tiles_max // SUBLANES) * SUBLANES
    kern = functools.partial(_plan_kernel, tile_rows=MOE_TM)
    return pl.pallas_call(
        kern,
        grid=(2, n // tm),
        in_specs=[pl.BlockSpec((tm, LANES), lambda p, i: (i, 0)),
                  pl.BlockSpec((tm, tm), lambda p, i: (0, 0)),
                  pl.BlockSpec((LANES, LANES), lambda p, i: (0, 0))],
        out_specs=[pl.BlockSpec((tm, LANES), lambda p, i: (i * p, 0)),
                   pl.BlockSpec((te_rows, LANES), lambda p, i: (0, 0)),
                   pl.BlockSpec((SUBLANES, LANES), lambda p, i: (0, 0))],
        out_shape=[jax.ShapeDtypeStruct((n, LANES), jnp.int32),
                   jax.ShapeDtypeStruct((te_rows, LANES), jnp.int32),
                   jax.ShapeDtypeStruct((SUBLANES, LANES), jnp.int32)],
        scratch_shapes=[pltpu.VMEM((1, LANES), F32), pltpu.VMEM((1, LANES), F32)],
        compiler_params=_params(("arbitrary", "arbitrary")),
        name="plan",
    )(route, ltri, utri)


DMA_UNROLL = 8


def _token_copy(src_ref, r, dst_ref, p, sem, rows):
    return pltpu.make_async_copy(src_ref.at[pl.ds(pl.multiple_of(r * rows, rows), rows)],
                                 dst_ref.at[pl.ds(pl.multiple_of(p * rows, rows), rows)], sem)


def _dispatch_kernel(pos_ref, meta_ref, t_ref, xs_ref, zero_scr, sem, tail_sem, *, tm, tile_rows):
    i = pl.program_id(0)
    tile_rows = tile_rows * SUBLANES
    n_tiles = xs_ref.shape[0] // tile_rows
    n_valid = meta_ref[N_CLASSES]

    def clear(tile, s):
        return pltpu.make_async_copy(zero_scr, xs_ref.at[pl.ds(tile * tile_rows, tile_rows)], s)

    def for_each(clears, fn):
        for tile, cond in clears:
            @pl.when(cond)
            def _():
                fn(tile)

    last_tiles = [(meta_ref[c], meta_ref[c] >= 0) for c in range(N_CLASSES)]
    past_tiles = [(n_tiles - 1 - k, n_tiles - 1 - k >= n_valid) for k in range(N_CLASSES)]

    @pl.when(i == 0)
    def _():
        zero_scr[...] = jnp.zeros_like(zero_scr)
        for_each(last_tiles, lambda tile: clear(tile, sem).start())
        for_each(past_tiles, lambda tile: clear(tile, tail_sem).start())
        for_each(last_tiles, lambda tile: clear(tile, sem).wait())

    base = i * tm

    def start(r2, carry):
        for k in range(2):
            r = 2 * r2 + k
            _token_copy(t_ref, r, xs_ref, pos_ref[base + r], sem, SUBLANES).start(priority=k)
        return carry

    def wait(r, carry):
        _token_copy(t_ref, r, xs_ref, pos_ref[base + r], sem, SUBLANES).wait()
        return carry

    lax.fori_loop(0, tm // 2, start, 0, unroll=DMA_UNROLL)
    lax.fori_loop(0, tm, wait, 0, unroll=DMA_UNROLL)

    @pl.when(i == pl.num_programs(0) - 1)
    def _():
        for_each(past_tiles, lambda tile: clear(tile, tail_sem).wait())


def _dispatch(pos, meta, t, n_rows):
    n = t.shape[0] // SUBLANES
    tm = min(DISPATCH_TM, n)
    kern = functools.partial(_dispatch_kernel, tm=tm, tile_rows=MOE_TM)
    grid_spec = pltpu.PrefetchScalarGridSpec(
        num_scalar_prefetch=2,
        grid=(n // tm,),
        in_specs=[pl.BlockSpec((tm * SUBLANES, LANES), lambda i, *_: (i, 0))],
        out_specs=pl.BlockSpec(memory_space=pl.ANY),
        scratch_shapes=[pltpu.VMEM((MOE_TM * SUBLANES, LANES), F32),
                        pltpu.SemaphoreType.DMA(()), pltpu.SemaphoreType.DMA(())],
    )
    return pl.pallas_call(
        kern,
        grid_spec=grid_spec,
        out_shape=jax.ShapeDtypeStruct((n_rows * SUBLANES, LANES), F32),
        compiler_params=_params(("arbitrary",)),
        name="dispatch",
    )(pos, meta, t)


def _expert_kernel(tea_ref, teb_ref, nv_ref, xs_ref, wga, wua, wda, wgb, wub, wdb, ys_ref):
    in_use = pl.program_id(0) < nv_ref[0]
    rows = xs_ref.shape[0] // SUBLANES

    @pl.when(in_use)
    def _():
        x = _load_token_tiles(xs_ref).astype(BF16)
        for slot, (wg, wu, wd) in enumerate(((wga, wua, wda), (wgb, wub, wdb))):
            hg = jnp.dot(x, wg[...].astype(BF16), preferred_element_type=F32)
            hu = jnp.dot(x, wu[...].astype(BF16), preferred_element_type=F32)
            a = (hg * jax.nn.sigmoid(hg) * hu).astype(BF16)
            y = jnp.dot(a, wd[...].astype(BF16), preferred_element_type=F32)
            for s in range(SUBLANES):
                ys_ref[pl.ds(slot * SUBLANES + s, rows, stride=TOKEN_ROWS_OUT), :] = (
                    y[:, s * LANES:(s + 1) * LANES])

    @pl.when(jnp.logical_not(in_use))
    def _():
        ys_ref[...] = jnp.zeros_like(ys_ref)


def _experts(tea, teb, nv, xs, wg, wu, wd):
    _, d, de = wg.shape
    assert d == SUBLANES * LANES
    n_tiles = xs.shape[0] // (MOE_TM * SUBLANES)
    wa = lambda shape: pl.BlockSpec(shape, lambda j, tea, teb, nv: (tea[j], 0, 0))
    wb = lambda shape: pl.BlockSpec(shape, lambda j, tea, teb, nv: (teb[j], 0, 0))
    grid_spec = pltpu.PrefetchScalarGridSpec(
        num_scalar_prefetch=3,
        grid=(n_tiles,),
        in_specs=[pl.BlockSpec((MOE_TM * SUBLANES, LANES),
                               lambda j, tea, teb, nv: (jnp.minimum(j, nv[0] - 1), 0)),
                  wa((None, d, de)), wa((None, d, de)), wa((None, de, d)),
                  wb((None, d, de)), wb((None, d, de)), wb((None, de, d))],
        out_specs=pl.BlockSpec((MOE_TM * TOKEN_ROWS_OUT, LANES), lambda j, *_: (j, 0)),
    )
    return pl.pallas_call(
        _expert_kernel,
        grid_spec=grid_spec,
        out_shape=jax.ShapeDtypeStruct((n_tiles * MOE_TM * TOKEN_ROWS_OUT, LANES), F32),
        compiler_params=_params(("arbitrary",)),
        name="experts",
    )(tea, teb, nv, xs, wg, wu, wd, wg, wu, wd)


def _combine_kernel(pos_ref, ys_ref, x1_ref, route_ref, fg_ref, o_ref, g, sem, *, tm):
    i = pl.program_id(0)
    slot = i % 2

    def gather(step, buf, wait):
        base = step * tm

        def body(r2, carry):
            for k in range(2):
                r = 2 * r2 + k
                c = _token_copy(ys_ref, pos_ref[base + r], g.at[buf], r, sem.at[buf],
                                TOKEN_ROWS_OUT)
                if wait:
                    c.wait()
                else:
                    c.start(priority=k)
            return carry

        lax.fori_loop(0, tm // 2, body, 0, unroll=DMA_UNROLL)

    @pl.when(i == 0)
    def _():
        gather(0, 0, wait=False)

    @pl.when(i + 1 < pl.num_programs(0))
    def _():
        gather(i + 1, 1 - slot, wait=False)

    gather(i, slot, wait=True)
    gs = g.at[slot]
    ya, yb = (jnp.concatenate([gs[pl.ds(off + s, tm, stride=TOKEN_ROWS_OUT), :]
                               for s in range(SUBLANES)], axis=1) for off in (0, SUBLANES))
    route = route_ref[...]
    x2 = x1_ref[...] + (route[:, ROUTE_WA:ROUTE_WA + 1] * ya + route[:, ROUTE_WB:ROUTE_WB + 1] * yb)
    var = jnp.mean(x2 * x2, axis=-1, keepdims=True)
    o_ref[...] = x2 * lax.rsqrt(var + EPS) * fg_ref[...]


def _combine(pos, ys, x1, route, final_g):
    n, d = x1.shape
    tm = min(COMBINE_TM, n)
    kern = functools.partial(_combine_kernel, tm=tm)
    grid_spec = pltpu.PrefetchScalarGridSpec(
        num_scalar_prefetch=1,
        grid=(n // tm,),
        in_specs=[pl.BlockSpec(memory_space=pl.ANY),
                  pl.BlockSpec((tm, d), lambda i, *_: (i, 0)),
                  pl.BlockSpec((tm, LANES), lambda i, *_: (i, 0)),
                  pl.BlockSpec((1, d), lambda i, *_: (0, 0))],
        out_specs=pl.BlockSpec((tm, d), lambda i, *_: (i, 0)),
        scratch_shapes=[pltpu.VMEM((2, tm * TOKEN_ROWS_OUT, LANES), F32),
                        pltpu.SemaphoreType.DMA((2,))],
    )
    return pl.pallas_call(
        kern,
        grid_spec=grid_spec,
        out_shape=jax.ShapeDtypeStruct((n, d), F32),
        compiler_params=_params(("arbitrary",)),
        name="combine",
    )(pos, ys, x1, route, final_g.reshape(1, d))


def _moe_sparse(t, route, wg, wu, wd, x1, final_g):
    n = x1.shape[0]
    n_tiles_max = (n + N_CLASSES * (MOE_TM - 1)) // MOE_TM
    pos, te, meta = _plan(route, n_tiles_max)
    pos = pos[:, 0]
    xs = _dispatch(pos, meta[0, :N_CLASSES + 1], t, n_tiles_max * MOE_TM)
    ys = _experts(te[:n_tiles_max, 0], te[:n_tiles_max, 1], meta[0, N_CLASSES:N_CLASSES + 1],
                  xs, wg, wu, wd)
    return _combine(pos, ys, x1, route, final_g)


def _suffix_ones(t):
    j = np.arange(t)[:, None]
    s = np.arange(t)[None, :]
    return jnp.asarray((j > s).astype(np.float32), dtype=BF16)


def _causal_bias(t):
    row = np.arange(t)[:, None]
    col = np.arange(t)[None, :]
    diag = np.where(col < row, 0.0, MASK_BIAS).astype(np.float32)
    return jnp.asarray(np.stack([np.zeros_like(diag), diag]))


def _chunk_prefix_ones(tt, c):
    t = np.arange(tt)[:, None]
    j = np.arange(tt)[None, :]
    return jnp.asarray(((j <= t) & (t // c == j // c)).astype(np.float32), dtype=BF16)


def _block_diag_ones(w, blk):
    a = np.arange(w)
    return jnp.asarray((a[:, None] // blk == a[None, :] // blk).astype(np.float32), dtype=BF16)


def kernel(x, ln1_g, w_in, w_branch_sb, w_branch_hg, hg_norm_g, hg_lb_logits, w_out, ln2_g,
           w_router_group, b_router_group, w_router_expert, b_router_expert,
           w_exp_gate, w_exp_up, w_exp_down, final_g):
    bsz, seq, d = x.shape
    depth = w_in.shape[0]
    n = bsz * seq
    sb_width = SB_HEADS * SB_HEAD_DIM
    hg_width = HG_HEADS * HG_DIM

    tri_attn = _suffix_ones(min(ATTN_T, seq))
    bias_attn = _causal_bias(min(ATTN_T, seq))
    tt = min(HGRN_TT, seq)
    tri_hg = _chunk_prefix_ones(tt, min(HGRN_C, tt))
    bd = _block_diag_ones(min(MXU_DIM, hg_width), HG_DIM)

    x2 = x.reshape(n, d)
    for l in range(depth):
        qkv, rest = _inproj(x2, ln1_g[l], w_in[l], sb_width)
        y_sb = _attn(qkv, bsz, seq, tri_attn, bias_attn)
        y_hg = _hgrn(rest, hg_lb_logits, hg_norm_g[l], bsz, seq, l, tri_hg, bd)

        pad = LANES - N_EXPERTS - N_GROUPS
        wr = jnp.concatenate([w_router_expert[l], w_router_group[l],
                              jnp.zeros((d, pad), F32)], axis=1)
        wr_hi = wr.astype(BF16)
        wr_lo = (wr - wr_hi.astype(F32)).astype(BF16)
        wr_split = jnp.concatenate([wr_hi, wr_lo], axis=1)
        br = jnp.concatenate([b_router_expert[l], b_router_group[l],
                              jnp.zeros((pad,), F32)]).reshape(1, LANES)

        last = l == depth - 1
        x1, t, route = _merge(x2, y_sb, y_hg, rest, w_branch_sb[l], w_branch_hg[l], w_out[l],
                              ln2_g[l], wr_split, br)
        assert last, "final rmsnorm is fused into the last layer's combine kernel"
        x2 = _moe_sparse(t, route, w_exp_gate[l], w_exp_up[l], w_exp_down[l], x1, final_g)
    return x2.reshape(bsz, seq, d)
```

```python
import functools

import jax
import jax.numpy as jnp
import numpy as np
from jax import lax
from jax.experimental import pallas as pl
from jax.experimental.pallas import tpu as pltpu

F32 = jnp.float32
BF16 = jnp.bfloat16

EPS = 1e-6
SB_HEADS = 8
SB_HEAD_DIM = 64
HG_HEADS = 8
HG_DIM = 64
N_GROUPS = 4
EXPERTS_PER_GROUP = 4
N_EXPERTS = N_GROUPS * EXPERTS_PER_GROUP

LANES = 128
SUBLANES = 8
MXU_DIM = 256
LOG2E = 1.4426950408889634
VMEM_LIMIT = 48 * 1024 * 1024

INPROJ_TM = 2048
INPROJ_TN = 512
ATTN_T = MXU_DIM
HGRN_TT = 256
HGRN_C = 32
MERGE_TM = 512
MOE_TM = 384
PLAN_TM = 1024
DISPATCH_TM = 1024
COMBINE_TM = 512


def _params(sem):
    return pltpu.CompilerParams(dimension_semantics=sem, vmem_limit_bytes=VMEM_LIMIT)


def _split_dot(x, m, passes):
    acc = None
    r = x
    for p in range(passes):
        h = r.astype(BF16)
        term = jnp.dot(h, m, preferred_element_type=F32)
        acc = term if acc is None else acc + term
        if p + 1 < passes:
            r = r - h.astype(F32)
    return acc


def _inproj_kernel(x_ref, g_ref, w_ref, qkv_ref, rest_ref, h_scr, *, q_scale):
    j = pl.program_id(1)

    @pl.when(j == 0)
    def _():
        x = x_ref[...]
        var = jnp.mean(x * x, axis=-1, keepdims=True)
        h_scr[...] = (x * lax.rsqrt(var + EPS) * g_ref[...]).astype(BF16)

    acc = jnp.dot(h_scr[...], w_ref[...].astype(BF16), preferred_element_type=F32)
    qkv_ref[...] = (acc * jnp.where(j == 0, q_scale, 1.0)).astype(BF16)
    rest_ref[...] = acc


def _inproj(x2, ln_g, w_in, sb_width):
    n, d = x2.shape
    cols = w_in.shape[1]
    tm, tn = min(INPROJ_TM, n), INPROJ_TN
    assert sb_width == tn, "q block must be exactly one column tile"
    nq = 3 * sb_width // tn
    nj = cols // tn
    kern = functools.partial(_inproj_kernel, q_scale=SB_HEAD_DIM ** -0.5)
    return pl.pallas_call(
        kern,
        grid=(n // tm, nj),
        in_specs=[
            pl.BlockSpec((tm, d), lambda i, j: (i, 0)),
            pl.BlockSpec((1, d), lambda i, j: (0, 0)),
            pl.BlockSpec((d, tn), lambda i, j: (0, j)),
        ],
        out_specs=[
            pl.BlockSpec((tm, tn), lambda i, j: (i, jnp.minimum(j, nq))),
            pl.BlockSpec((tm, tn), lambda i, j: (i, jnp.maximum(j - nq, 0))),
        ],
        out_shape=[
            jax.ShapeDtypeStruct((n, (nq + 1) * tn), BF16),
            jax.ShapeDtypeStruct((n, cols - nq * tn), F32),
        ],
        scratch_shapes=[pltpu.VMEM((tm, d), BF16)],
        compiler_params=_params(("parallel", "arbitrary")),
        name="inproj",
    )(x2, ln_g.reshape(1, d), w_in)


ATTN_STAGES = 3
ATTN_STREAMS = 2
MASK_BIAS = -1e30
ATTN_SKIP = 111.0


def _attn_kernel(q_ref, k_ref, v_ref, tri_ref, bias_ref, o_ref,
                 z0, z1, z2, i0, i1, i2, acc_ref, c_ref, *, t, nq):
    zbuf = (z0, z1, z2)
    ibuf = (i0, i1, i2)
    for r in zbuf + ibuf + (acc_ref, c_ref):
        r[...] = jnp.zeros_like(r)

    lane = lax.broadcasted_iota(jnp.int32, (t, LANES), 1)
    head0 = lane < SB_HEAD_DIM
    nt = (((1,), (1,)), ((), ()))

    def stage_a(qi, kj, slot):
        q = q_ref[pl.ds(pl.multiple_of(qi * t, t), t), :]
        zero = jnp.zeros_like(q)
        q2 = jnp.concatenate([jnp.where(head0, q, zero), jnp.where(head0, zero, q)], axis=0)
        k = k_ref[pl.ds(pl.multiple_of(kj * t, t), t), :]
        z = lax.dot_general(q2, k, nt, preferred_element_type=F32)
        bias = bias_ref[(qi == kj).astype(jnp.int32)]
        zbuf[slot][...] = z + jnp.concatenate([bias, bias], axis=0)

    def stage_b(slot):
        z = zbuf[slot][...]
        p = jnp.maximum(z, 0.0) + jnp.log(1.0 + jnp.exp2(jnp.abs(z) * (-LOG2E)))
        incl = jnp.dot(p.astype(BF16), tri_ref[...], preferred_element_type=F32) + p
        ibuf[slot][...] = incl
        return incl[:, 0:1]

    def stage_c(qi, kj, slot, stream):
        first = qi == kj
        incl = ibuf[slot][...]
        c = jnp.where(first, 0.0, c_ref[stream])
        a = jnp.exp2((zbuf[slot][...] - incl - c) * LOG2E)
        v = v_ref[pl.ds(pl.multiple_of(kj * t, t), t), :]
        pv = jnp.dot(a.astype(BF16), v, preferred_element_type=F32)
        acc = jnp.where(first, pv, acc_ref[stream] + pv)
        acc_ref[stream] = acc
        c_ref[stream] = c + incl[:, 0:1]
        o_ref[pl.ds(pl.multiple_of(qi * t, t), t), :] = (
            jnp.where(head0, acc[0:t], acc[t:2 * t]).astype(o_ref.dtype))

    def block(carry):
        qs, ks, qn, kn, skip_q, drained = carry
        qs, ks, qn, kn, skip_q = list(qs), list(ks), list(qn), list(kn), list(skip_q)
        all_done = qn[0] >= nq
        for p in range(1, ATTN_STREAMS):
            all_done = jnp.logical_and(all_done, qn[p] >= nq)
        drained = drained + all_done.astype(jnp.int32)
        for r in range(ATTN_STAGES * ATTN_STREAMS):
            sa, sc, sb = r % ATTN_STAGES, (r + 1) % ATTN_STAGES, (r + 2) % ATTN_STAGES
            pa, pb = r % ATTN_STREAMS, (r - 1) % ATTN_STREAMS
            stage_c(qs[sc], ks[sc], sc, pa)
            total_b = stage_b(sb)
            carry_b = jnp.where(qs[sb] == ks[sb], 0.0, c_ref[pb]) + total_b
            skip = skip_q[pa] == qn[pa]
            q_cur = jnp.where(skip, qn[pa] + ATTN_STREAMS, qn[pa])
            k_cur = jnp.where(skip, qn[pa] + ATTN_STREAMS, kn[pa])
            drain = q_cur >= nq
            qa = jnp.where(drain, 0, q_cur)
            ka = jnp.where(drain, 0, k_cur)
            stage_a(qa, ka, sa)
            last = k_cur == 0
            qn[pa] = jnp.where(jnp.logical_and(last, jnp.logical_not(drain)),
                               q_cur + ATTN_STREAMS, q_cur)
            kn[pa] = jnp.where(drain, k_cur, jnp.where(last, q_cur + ATTN_STREAMS, k_cur - 1))
            skip_q[pb] = jnp.where(jnp.min(carry_b) >= ATTN_SKIP, qs[sb], skip_q[pb])
            qs[sa], ks[sa] = qa, ka
        return tuple(qs), tuple(ks), tuple(qn), tuple(kn), tuple(skip_q), drained

    zero = jnp.int32(0)
    first_q = tuple(jnp.int32(p) for p in range(ATTN_STREAMS))
    init = ((zero,) * ATTN_STAGES, (zero,) * ATTN_STAGES, first_q, first_q,
            (jnp.int32(-1),) * ATTN_STREAMS, zero)
    lax.while_loop(lambda carry: carry[5] < 1, block, init)


def _attn(qkv, bsz, seq, tri, bias):
    n = bsz * seq
    t = min(ATTN_T, seq)
    pairs = SB_HEADS * SB_HEAD_DIM // LANES
    kern = functools.partial(_attn_kernel, t=t, nq=seq // t)
    return pl.pallas_call(
        kern,
        grid=(bsz, pairs),
        in_specs=[
            pl.BlockSpec((seq, LANES), lambda b, p: (b, p)),
            pl.BlockSpec((seq, LANES), lambda b, p: (b, pairs + p)),
            pl.BlockSpec((seq, LANES), lambda b, p: (b, 2 * pairs + p)),
            pl.BlockSpec((t, t), lambda b, p: (0, 0)),
            pl.BlockSpec((2, t, t), lambda b, p: (0, 0, 0)),
        ],
        out_specs=pl.BlockSpec((seq, LANES), lambda b, p: (b, p)),
        out_shape=jax.ShapeDtypeStruct((n, pairs * LANES), BF16),
        scratch_shapes=([pltpu.VMEM((2 * t, t), F32)] * (2 * ATTN_STAGES)
                        + [pltpu.VMEM((ATTN_STREAMS, 2 * t, LANES), F32),
                           pltpu.VMEM((ATTN_STREAMS, 2 * t, 1), F32)]),
        compiler_params=_params(("parallel", "parallel")),
        name="attn",
    )(qkv, qkv, qkv, tri, bias)


def _group_dot(x, bd):
    g = bd.shape[0]
    parts = [jnp.dot(x[:, i:i + g], bd, preferred_element_type=F32)
             for i in range(0, x.shape[1], g)]
    return jnp.concatenate(parts, axis=1)


def _hgrn_kernel(q_ref, f_ref, i_ref, g_ref, lbl_ref, ng_ref, tri_ref, bd_ref, o_ref,
                 st_scr, wpad, vpad, wsh, vsh, *, tt, c, layer):
    ti = pl.program_id(1)
    w = q_ref.shape[1]
    nch = tt // c

    @pl.when(ti == 0)
    def _():
        st_scr[...] = jnp.zeros_like(st_scr)

    lg = lbl_ref[...]
    e = jnp.exp(lg - jnp.max(lg, axis=0, keepdims=True))
    lb = jnp.sum(e[0:layer + 1], axis=0, keepdims=True) / jnp.sum(e, axis=0, keepdims=True)

    f = lb + (1.0 - lb) * jax.nn.sigmoid(f_ref[...])
    kk = 1.0 - f
    qv = q_ref[...]
    qs = qv * jax.nn.sigmoid(qv)
    v = i_ref[...]
    bd = bd_ref[...]
    cum = _split_dot_left(tri_ref[...], jnp.log(f) * LOG2E, 3)
    wk = jnp.log(kk) * LOG2E - cum

    pos = lax.broadcasted_iota(jnp.int32, (tt, 1), 0) % c
    wpad[0:SUBLANES, :] = jnp.zeros((SUBLANES, w), F32)
    vpad[0:SUBLANES, :] = jnp.zeros((SUBLANES, w), F32)
    wpad[SUBLANES:SUBLANES + tt, :] = wk
    vpad[SUBLANES:SUBLANES + tt, :] = v
    for b in range(SUBLANES):
        wb = wpad[SUBLANES - b:SUBLANES - b + tt, :]
        vb = vpad[SUBLANES - b:SUBLANES - b + tt, :]
        if b:
            wb = jnp.where(pos >= b, wb, -jnp.inf)
        wsh[b] = wb.reshape(nch, c, w)
        vsh[b] = vb.reshape(nch, c, w)

    qs3 = qs.reshape(nch, c, w)
    cum3 = cum.reshape(nch, c, w)
    acc = None
    for a in reversed(range(c // SUBLANES)):
        rows = c - SUBLANES * a
        qa = qs3[:, SUBLANES * a:, :].reshape(nch * rows, w)
        ca = cum3[:, SUBLANES * a:, :].reshape(nch * rows, w)
        acc_a = None
        for b in range(SUBLANES):
            wb = wsh[b, :, 0:rows, :].reshape(nch * rows, w)
            vb = vsh[b, :, 0:rows, :].reshape(nch * rows, w)
            dd = qa * jnp.exp2(ca + wb)
            term = _group_dot(dd.astype(BF16), bd) * vb
            acc_a = term if acc_a is None else acc_a + term
        acc_a = acc_a.reshape(nch, rows, w)
        if acc is not None:
            acc_a = acc_a + jnp.concatenate([jnp.zeros((nch, SUBLANES, w), F32), acc], axis=1)
        acc = acc_a
    acc = acc.reshape(tt, w)

    grp = st_scr.shape[1]
    bdmask = bd[0:grp, 0:grp] != 0
    outs = []
    for ci in range(nch):
        r0 = ci * c
        cum_c = cum[r0:r0 + c]
        last = cum_c[c - 1:c]
        qd = (qs[r0:r0 + c] * jnp.exp2(cum_c)).astype(BF16)
        kd = (kk[r0:r0 + c] * jnp.exp2(last - cum_c)).astype(BF16)
        vc = v[r0:r0 + c].astype(BF16)
        dec = jnp.exp2(last)
        o_parts = []
        for gi in range(w // grp):
            sl = slice(gi * grp, (gi + 1) * grp)
            st = st_scr[gi]
            o_parts.append(lax.dot_general(qd[:, sl], st.astype(BF16), (((1,), (1,)), ((), ())),
                                           preferred_element_type=F32))
            upd = lax.dot_general(vc[:, sl], kd[:, sl], (((0,), (0,)), ((), ())),
                                  preferred_element_type=F32)
            st_scr[gi] = st * dec[:, sl] + jnp.where(bdmask, upd, 0.0)
        outs.append(jnp.concatenate(o_parts, axis=1))
    o = acc + jnp.concatenate(outs, axis=0)

    o2 = o * o
    o2_hi = o2.astype(BF16)
    o2_lo = (o2 - o2_hi.astype(F32)).astype(BF16)
    ms = (_group_dot(o2_hi, bd) + _group_dot(o2_lo, bd)) * (1.0 / HG_DIM)
    gv = g_ref[...]
    o = o * lax.rsqrt(ms + EPS) * ng_ref[...] * (gv * jax.nn.sigmoid(gv))
    o_ref[...] = o.astype(o_ref.dtype)


def _split_dot_left(m, x, passes):
    acc = None
    r = x
    for p in range(passes):
        h = r.astype(BF16)
        term = jnp.dot(m, h, preferred_element_type=F32)
        acc = term if acc is None else acc + term
        if p + 1 < passes:
            r = r - h.astype(F32)
    return acc


def _hgrn(rest, lb_logits, norm_g, bsz, seq, layer, tri, bd):
    n = bsz * seq
    w = HG_HEADS * HG_DIM
    grp = bd.shape[0]
    tt = min(HGRN_TT, seq)
    c = min(HGRN_C, tt)
    nt = seq // tt
    kern = functools.partial(_hgrn_kernel, tt=tt, c=c, layer=layer)
    col = lambda j: pl.BlockSpec((tt, w), lambda b, i, j=j: (b * nt + i, j))
    const = lambda shape: pl.BlockSpec(shape, lambda b, i: (0, 0))
    return pl.pallas_call(
        kern,
        grid=(bsz, nt),
        in_specs=[col(0), col(1), col(2), col(3),
                  const(lb_logits.shape), const((1, w)), const((tt, tt)), const((grp, grp))],
        out_specs=pl.BlockSpec((tt, w), lambda b, i: (b * nt + i, 0)),
        out_shape=jax.ShapeDtypeStruct((n, w), BF16),
        scratch_shapes=([pltpu.VMEM((w // LANES, LANES, LANES), F32)]
                        + [pltpu.VMEM((SUBLANES + tt, w), F32)] * 2
                        + [pltpu.VMEM((SUBLANES, tt // c, c, w), F32)] * 2),
        compiler_params=_params(("parallel", "arbitrary")),
        name="hgrn",
    )(rest, rest, rest, rest, lb_logits, norm_g.reshape(1, w), tri, bd)


PAIR_A = (0, 0, 0, 1, 1, 3)
PAIR_B = (1, 2, 3, 3, 2, 2)
N_PAIRS = len(PAIR_A)
N_CLASSES = N_GROUPS * N_PAIRS
assert sorted(tuple(sorted(p)) for p in zip(PAIR_A, PAIR_B)) == [
    (a, b) for a in range(EXPERTS_PER_GROUP) for b in range(a + 1, EXPERTS_PER_GROUP)]
ROUTE_CLS, ROUTE_WA, ROUTE_WB = 0, 1, 2


def _pair_slots(pidx):
    a = b = jnp.zeros_like(pidx)
    for p in range(N_PAIRS):
        a = jnp.where(pidx == p, float(PAIR_A[p]), a)
        b = jnp.where(pidx == p, float(PAIR_B[p]), b)
    return a, b


def _pair_index(lo, hi):
    pidx = jnp.zeros_like(lo)
    for p in range(N_PAIRS):
        is_p = jnp.logical_and(lo == min(PAIR_A[p], PAIR_B[p]), hi == max(PAIR_A[p], PAIR_B[p]))
        pidx = jnp.where(is_p, float(p), pidx)
    return pidx


def _route(lg):
    lane = lax.broadcasted_iota(jnp.int32, lg.shape, 1)
    neg = jnp.float32(-jnp.inf)
    big = jnp.int32(LANES)
    gmask = jnp.logical_and(lane >= N_EXPERTS, lane < N_EXPERTS + N_GROUPS)
    gl = jnp.where(gmask, lg, neg)
    gmax = jnp.max(gl, axis=1, keepdims=True)
    gidx = jnp.min(jnp.where(gl == gmax, lane, big), axis=1, keepdims=True) - N_EXPERTS
    w_grp = 1.0 / jnp.sum(jnp.where(gmask, jnp.exp(gl - gmax), 0.0), axis=1, keepdims=True)
    in_grp = jnp.logical_and(lane < N_EXPERTS, lane // EXPERTS_PER_GROUP == gidx)
    l1 = jnp.where(in_grp, lg, neg)
    v1 = jnp.max(l1, axis=1, keepdims=True)
    i1 = jnp.min(jnp.where(l1 == v1, lane, big), axis=1, keepdims=True)
    l2 = jnp.where(jnp.logical_and(in_grp, lane != i1), lg, neg)
    v2 = jnp.max(l2, axis=1, keepdims=True)
    i2 = jnp.min(jnp.where(l2 == v2, lane, big), axis=1, keepdims=True)
    e2 = jnp.exp(v2 - v1)
    p1 = 1.0 / (1.0 + e2)
    p2 = e2 * p1
    loc1 = (i1 - gidx * EXPERTS_PER_GROUP).astype(F32)
    loc2 = (i2 - gidx * EXPERTS_PER_GROUP).astype(F32)
    pidx = _pair_index(jnp.minimum(loc1, loc2), jnp.maximum(loc1, loc2))
    cls = gidx.astype(F32) * N_PAIRS + pidx
    a_loc, _ = _pair_slots(pidx)
    first_is_a = loc1 == a_loc
    wa = jnp.where(first_is_a, p1, p2) * w_grp
    wb = jnp.where(first_is_a, p2, p1) * w_grp
    return jnp.where(lane == ROUTE_CLS, cls,
                     jnp.where(lane == ROUTE_WA, wa, jnp.where(lane == ROUTE_WB, wb, 0.0)))


def _store_token_tiles(ref, x):
    rows, d = x.shape
    assert d == SUBLANES * LANES
    for s in range(SUBLANES):
        ref[pl.ds(s, rows, stride=SUBLANES), :] = x[:, s * LANES:(s + 1) * LANES]


def _load_token_tiles(ref):
    rows = ref.shape[0] // SUBLANES
    return jnp.concatenate([ref[pl.ds(s, rows, stride=SUBLANES), :] for s in range(SUBLANES)],
                           axis=1)


def _merge_kernel(x_ref, ysb_ref, yhg_ref, gsb_ref, ghg_ref, wbs_ref, wbh_ref, wo_ref,
                  ln_ref, wr_ref, br_ref, x1_ref, t_ref, route_ref, wbs_scr, wbh_scr, wo_scr):
    @pl.when(pl.program_id(0) == 0)
    def _():
        wbs_scr[...] = wbs_ref[...].astype(BF16)
        wbh_scr[...] = wbh_ref[...].astype(BF16)
        wo_scr[...] = wo_ref[...].astype(BF16)

    a = jnp.dot(ysb_ref[...], wbs_scr[...], preferred_element_type=F32)
    b = jnp.dot(yhg_ref[...], wbh_scr[...], preferred_element_type=F32)
    merged = jax.nn.sigmoid(gsb_ref[...]) * a + jax.nn.sigmoid(ghg_ref[...]) * b
    x1 = x_ref[...] + jnp.dot(merged.astype(BF16), wo_scr[...], preferred_element_type=F32)
    x1_ref[...] = x1
    var = jnp.mean(x1 * x1, axis=-1, keepdims=True)
    t = x1 * lax.rsqrt(var + EPS) * ln_ref[...]
    _store_token_tiles(t_ref, t)
    t_hi = t.astype(BF16)
    t_lo = (t - t_hi.astype(F32)).astype(BF16)
    p_hi = jnp.dot(t_hi, wr_ref[...], preferred_element_type=F32)
    p_lo = jnp.dot(t_lo, wr_ref[...], preferred_element_type=F32)
    lg = (p_hi[:, :LANES] + p_hi[:, LANES:]) + (p_lo[:, :LANES] + p_lo[:, LANES:]) + br_ref[...]
    route_ref[...] = _route(lg)


def _merge(x2, y_sb, y_hg, rest, wbs, wbh, wo, ln_g, wr, br):
    n, d = x2.shape
    tm = min(MERGE_TM, n)
    w_sb, w_hg = y_sb.shape[1], y_hg.shape[1]
    gate_blk = (rest.shape[1] - 2 * d) // d
    row = lambda wdt, j=0: pl.BlockSpec((tm, wdt), lambda i, j=j: (i, j))
    const = lambda shape: pl.BlockSpec(shape, lambda i: (0, 0))
    return pl.pallas_call(
        _merge_kernel,
        grid=(n // tm,),
        in_specs=[row(d), row(w_sb), row(w_hg), row(d, gate_blk), row(d, gate_blk + 1),
                  const(wbs.shape), const(wbh.shape), const(wo.shape), const((1, d)),
                  const(wr.shape), const((1, LANES))],
        out_specs=[row(d), pl.BlockSpec((tm * SUBLANES, LANES), lambda i: (i, 0)), row(LANES)],
        out_shape=[jax.ShapeDtypeStruct((n, d), F32),
                   jax.ShapeDtypeStruct((n * SUBLANES, LANES), F32),
                   jax.ShapeDtypeStruct((n, LANES), F32)],
        scratch_shapes=[pltpu.VMEM(w.shape, BF16) for w in (wbs, wbh, wo)],
        compiler_params=_params(("arbitrary",)),
        name="merge",
    )(x2, y_sb, y_hg, rest, rest, wbs, wbh, wo, ln_g.reshape(1, d), wr, br)


TOKEN_ROWS_OUT = 2 * SUBLANES


def _plan_kernel(route_ref, ltri_ref, utri_ref, pos_ref, te_ref, meta_ref,
                 tot_scr, run_scr, *, tile_rows):
    ph = pl.program_id(0)
    i = pl.program_id(1)
    route = route_ref[...]
    tm = route.shape[0]
    lane = lax.broadcasted_iota(jnp.int32, (tm, LANES), 1)
    is_cls = lane.astype(F32) == route[:, ROUTE_CLS:ROUTE_CLS + 1]
    sel = jnp.where(is_cls, 1.0, 0.0)

    @pl.when(jnp.logical_and(ph == 0, i == 0))
    def _():
        tot_scr[...] = jnp.zeros_like(tot_scr)

    @pl.when(ph == 0)
    def _():
        tot_scr[...] += jnp.sum(sel, axis=0, keepdims=True)

    @pl.when(ph == 1)
    def _():
        tot = tot_scr[...]
        top = tot + (tile_rows - 1.0)
        tiles = jnp.floor(top * (1.0 / tile_rows))
        tiles = jnp.where(tiles * tile_rows > top, tiles - 1.0, tiles)
        tiles = jnp.where((tiles + 1.0) * tile_rows <= top, tiles + 1.0, tiles)
        first_tile = _split_dot(jnp.broadcast_to(tiles, (SUBLANES, LANES)), utri_ref[...], 3)[0:1]

        @pl.when(i == 0)
        def _():
            run_scr[...] = jnp.zeros_like(run_scr)
            lane1 = lax.broadcasted_iota(jnp.int32, (1, LANES), 1)
            is_c = lane1 < N_CLASSES
            end_tile = first_tile + tiles
            n_valid = jnp.sum(jnp.where(is_c, tiles, 0.0), axis=1, keepdims=True)
            rows = te_ref.shape[0]
            j = lax.broadcasted_iota(jnp.int32, (rows, LANES), 0).astype(F32)
            j = jnp.minimum(j, n_valid - 1.0)
            lane2 = lax.broadcasted_iota(jnp.int32, (rows, LANES), 1)
            done = jnp.where(jnp.logical_and(lane2 < N_CLASSES, end_tile <= j), 1.0, 0.0)
            tc = jnp.sum(done, axis=1, keepdims=True)
            tg = jnp.floor((tc + 0.5) * (1.0 / N_PAIRS))
            ta, tb = _pair_slots(tc - tg * N_PAIRS)
            te = jnp.where(lane2 == 0, tg * EXPERTS_PER_GROUP + ta, tg * EXPERTS_PER_GROUP + tb)
            te_ref[...] = te.astype(jnp.int32)
            last_tile = jnp.where(jnp.logical_and(is_c, tiles > 0), end_tile - 1.0, -1.0)
            meta = jnp.where(lane1 == N_CLASSES, n_valid, last_tile)
            meta_ref[...] = jnp.broadcast_to(meta, meta_ref.shape).astype(jnp.int32)

        rank = jnp.dot(ltri_ref[...], sel.astype(BF16), preferred_element_type=F32) + run_scr[...]
        dest = jnp.sum(jnp.where(is_cls, first_tile * tile_rows + rank, 0.0),
                       axis=1, keepdims=True)
        row = lax.broadcasted_iota(jnp.int32, (tm, LANES), 0)
        spread = jnp.where(lane == row % LANES, dest, 0.0)
        pos_ref[...] = jnp.sum(spread.reshape(tm // LANES, LANES, LANES), axis=1).astype(jnp.int32)
        run_scr[...] += jnp.sum(sel, axis=0, keepdims=True)


def _plan(route, n_tiles_max):
    n = route.shape[0]
    tm = min(PLAN_TM, n)
    r = np.arange(tm)
    ltri = jnp.asarray((r[None, :] < r[:, None]).astype(np.float32), dtype=BF16)
    e = np.arange(LANES)
    utri = jnp.asarray((e[:, None] < e[None, :]).astype(np.float32), dtype=BF16)
    te_rows = -(-n_tiles_max // SUBLANES) * SUBLANES
    kern = functools.partial(_plan_kernel, tile_rows=MOE_TM)
    return pl.pallas_call(
        kern,
        grid=(2, n // tm),
        in_specs=[pl.BlockSpec((tm, LANES), lambda p, i: (i, 0)),
                  pl.BlockSpec((tm, tm), lambda p, i: (0, 0)),
                  pl.BlockSpec((LANES, LANES), lambda p, i: (0, 0))],
        out_specs=[pl.BlockSpec((tm // LANES, LANES), lambda p, i: (i * p, 0)),
                   pl.BlockSpec((te_rows, LANES), lambda p, i: (0, 0)),
                   pl.BlockSpec((SUBLANES, LANES), lambda p, i: (0, 0))],
        out_shape=[jax.ShapeDtypeStruct((n // LANES, LANES), jnp.int32),
                   jax.ShapeDtypeStruct((te_rows, LANES), jnp.int32),
                   jax.ShapeDtypeStruct((SUBLANES, LANES), jnp.int32)],
        scratch_shapes=[pltpu.VMEM((1, LANES), F32), pltpu.VMEM((1, LANES), F32)],
        compiler_params=_params(("arbitrary", "arbitrary")),
        name="plan",
    )(route, ltri, utri)


DMA_UNROLL = 8


def _token_copy(src_ref, r, dst_ref, p, sem, rows):
    return pltpu.make_async_copy(src_ref.at[pl.ds(pl.multiple_of(r * rows, rows), rows)],
                                 dst_ref.at[pl.ds(pl.multiple_of(p * rows, rows), rows)], sem)


def _dispatch_kernel(pos_ref, meta_ref, t_ref, xs_ref, zero_scr, sem, tail_sem, *, tm, tile_rows):
    i = pl.program_id(0)
    tile_rows = tile_rows * SUBLANES
    n_tiles = xs_ref.shape[0] // tile_rows
    n_valid = meta_ref[N_CLASSES]

    def clear(tile, s):
        return pltpu.make_async_copy(zero_scr, xs_ref.at[pl.ds(tile * tile_rows, tile_rows)], s)

    def for_each(clears, fn):
        for tile, cond in clears:
            @pl.when(cond)
            def _():
                fn(tile)

    last_tiles = [(meta_ref[c], meta_ref[c] >= 0) for c in range(N_CLASSES)]
    past_tiles = [(n_tiles - 1 - k, n_tiles - 1 - k >= n_valid) for k in range(N_CLASSES)]

    @pl.when(i == 0)
    def _():
        zero_scr[...] = jnp.zeros_like(zero_scr)
        for_each(last_tiles, lambda tile: clear(tile, sem).start())
        for_each(past_tiles, lambda tile: clear(tile, tail_sem).start())
        for_each(last_tiles, lambda tile: clear(tile, sem).wait())

    base = i * tm

    def start(r2, carry):
        for k in range(2):
            r = 2 * r2 + k
            _token_copy(t_ref, r, xs_ref, pos_ref[base + r], sem, SUBLANES).start(priority=k)
        return carry

    def wait(r, carry):
        _token_copy(t_ref, r, xs_ref, pos_ref[base + r], sem, SUBLANES).wait()
        return carry

    lax.fori_loop(0, tm // 2, start, 0, unroll=DMA_UNROLL)
    lax.fori_loop(0, tm, wait, 0, unroll=DMA_UNROLL)

    @pl.when(i == pl.num_programs(0) - 1)
    def _():
        for_each(past_tiles, lambda tile: clear(tile, tail_sem).wait())


def _dispatch(pos, meta, t, n_rows):
    n = t.shape[0] // SUBLANES
    tm = min(DISPATCH_TM, n)
    kern = functools.partial(_dispatch_kernel, tm=tm, tile_rows=MOE_TM)
    grid_spec = pltpu.PrefetchScalarGridSpec(
        num_scalar_prefetch=2,
        grid=(n // tm,),
        in_specs=[pl.BlockSpec((tm * SUBLANES, LANES), lambda i, *_: (i, 0))],
        out_specs=pl.BlockSpec(memory_space=pl.ANY),
        scratch_shapes=[pltpu.VMEM((MOE_TM * SUBLANES, LANES), F32),
                        pltpu.SemaphoreType.DMA(()), pltpu.SemaphoreType.DMA(())],
    )
    return pl.pallas_call(
        kern,
        grid_spec=grid_spec,
        out_shape=jax.ShapeDtypeStruct((n_rows * SUBLANES, LANES), F32),
        compiler_params=_params(("arbitrary",)),
        name="dispatch",
    )(pos, meta, t)


def _expert_kernel(tea_ref, teb_ref, nv_ref, xs_ref, wga, wua, wda, wgb, wub, wdb, ys_ref):
    in_use = pl.program_id(0) < nv_ref[0]
    rows = xs_ref.shape[0] // SUBLANES

    @pl.when(in_use)
    def _():
        x = _load_token_tiles(xs_ref).astype(BF16)
        for slot, (wg, wu, wd) in enumerate(((wga, wua, wda), (wgb, wub, wdb))):
            hg = jnp.dot(x, wg[...].astype(BF16), preferred_element_type=F32)
            hu = jnp.dot(x, wu[...].astype(BF16), preferred_element_type=F32)
            a = (hg * jax.nn.sigmoid(hg) * hu).astype(BF16)
            y = jnp.dot(a, wd[...].astype(BF16), preferred_element_type=F32)
            for s in range(SUBLANES):
                ys_ref[pl.ds(slot * SUBLANES + s, rows, stride=TOKEN_ROWS_OUT), :] = (
                    y[:, s * LANES:(s + 1) * LANES])

    @pl.when(jnp.logical_not(in_use))
    def _():
        ys_ref[...] = jnp.zeros_like(ys_ref)


def _experts(tea, teb, nv, xs, wg, wu, wd):
    _, d, de = wg.shape
    assert d == SUBLANES * LANES
    n_tiles = xs.shape[0] // (MOE_TM * SUBLANES)
    wa = lambda shape: pl.BlockSpec(shape, lambda j, tea, teb, nv: (tea[j], 0, 0))
    wb = lambda shape: pl.BlockSpec(shape, lambda j, tea, teb, nv: (teb[j], 0, 0))
    grid_spec = pltpu.PrefetchScalarGridSpec(
        num_scalar_prefetch=3,
        grid=(n_tiles,),
        in_specs=[pl.BlockSpec((MOE_TM * SUBLANES, LANES),
                               lambda j, tea, teb, nv: (jnp.minimum(j, nv[0] - 1), 0)),
                  wa((None, d, de)), wa((None, d, de)), wa((None, de, d)),
                  wb((None, d, de)), wb((None, d, de)), wb((None, de, d))],
        out_specs=pl.BlockSpec((MOE_TM * TOKEN_ROWS_OUT, LANES), lambda j, *_: (j, 0)),
    )
    return pl.pallas_call(
        _expert_kernel,
        grid_spec=grid_spec,
        out_shape=jax.ShapeDtypeStruct((n_tiles * MOE_TM * TOKEN_ROWS_OUT, LANES), F32),
        compiler_params=_params(("arbitrary",)),
        name="experts",
    )(tea, teb, nv, xs, wg, wu, wd, wg, wu, wd)


def _combine_kernel(pos_ref, ys_ref, x1_ref, route_ref, fg_ref, o_ref, g, sem, *, tm):
    i = pl.program_id(0)
    slot = i % 2

    def gather(step, buf, wait):
        base = step * tm

        def body(r2, carry):
            for k in range(2):
                r = 2 * r2 + k
                c = _token_copy(ys_ref, pos_ref[base + r], g.at[buf], r, sem.at[buf],
                                TOKEN_ROWS_OUT)
                if wait:
                    c.wait()
                else:
                    c.start(priority=k)
            return carry

        lax.fori_loop(0, tm // 2, body, 0, unroll=DMA_UNROLL)

    @pl.when(i == 0)
    def _():
        gather(0, 0, wait=False)

    @pl.when(i + 1 < pl.num_programs(0))
    def _():
        gather(i + 1, 1 - slot, wait=False)

    gather(i, slot, wait=True)
    gs = g.at[slot]
    ya, yb = (jnp.concatenate([gs[pl.ds(off + s, tm, stride=TOKEN_ROWS_OUT), :]
                               for s in range(SUBLANES)], axis=1) for off in (0, SUBLANES))
    route = route_ref[...]
    x2 = x1_ref[...] + (route[:, ROUTE_WA:ROUTE_WA + 1] * ya + route[:, ROUTE_WB:ROUTE_WB + 1] * yb)
    var = jnp.mean(x2 * x2, axis=-1, keepdims=True)
    o_ref[...] = x2 * lax.rsqrt(var + EPS) * fg_ref[...]


def _combine(pos, ys, x1, route, final_g):
    n, d = x1.shape
    tm = min(COMBINE_TM, n)
    kern = functools.partial(_combine_kernel, tm=tm)
    grid_spec = pltpu.PrefetchScalarGridSpec(
        num_scalar_prefetch=1,
        grid=(n // tm,),
        in_specs=[pl.BlockSpec(memory_space=pl.ANY),
                  pl.BlockSpec((tm, d), lambda i, *_: (i, 0)),
                  pl.BlockSpec((tm, LANES), lambda i, *_: (i, 0)),
                  pl.BlockSpec((1, d), lambda i, *_: (0, 0))],
        out_specs=pl.BlockSpec((tm, d), lambda i, *_: (i, 0)),
        scratch_shapes=[pltpu.VMEM((2, tm * TOKEN_ROWS_OUT, LANES), F32),
                        pltpu.SemaphoreType.DMA((2,))],
    )
    return pl.pallas_call(
        kern,
        grid_spec=grid_spec,
        out_shape=jax.ShapeDtypeStruct((n, d), F32),
        compiler_params=_params(("arbitrary",)),
        name="combine",
    )(pos, ys, x1, route, final_g.reshape(1, d))


def _moe_sparse(t, route, wg, wu, wd, x1, final_g):
    n = x1.shape[0]
    n_tiles_max = (n + N_CLASSES * (MOE_TM - 1)) // MOE_TM
    pos, te, meta = _plan(route, n_tiles_max)
    pos = pos.reshape(n)
    xs = _dispatch(pos, meta[0, :N_CLASSES + 1], t, n_tiles_max * MOE_TM)
    ys = _experts(te[:n_tiles_max, 0], te[:n_tiles_max, 1], meta[0, N_CLASSES:N_CLASSES + 1],
                  xs, wg, wu, wd)
    return _combine(pos, ys, x1, route, final_g)


def _suffix_ones(t):
    j = np.arange(t)[:, None]
    s = np.arange(t)[None, :]
    return jnp.asarray((j > s).astype(np.float32), dtype=BF16)


def _causal_bias(t):
    row = np.arange(t)[:, None]
    col = np.arange(t)[None, :]
    diag = np.where(col < row, 0.0, MASK_BIAS).astype(np.float32)
    return jnp.asarray(np.stack([np.zeros_like(diag), diag]))


def _chunk_prefix_ones(tt, c):
    t = np.arange(tt)[:, None]
    j = np.arange(tt)[None, :]
    return jnp.asarray(((j <= t) & (t // c == j // c)).astype(np.float32), dtype=BF16)


def _block_diag_ones(w, blk):
    a = np.arange(w)
    return jnp.asarray((a[:, None] // blk == a[None, :] // blk).astype(np.float32), dtype=BF16)


def kernel(x, ln1_g, w_in, w_branch_sb, w_branch_hg, hg_norm_g, hg_lb_logits, w_out, ln2_g,
           w_router_group, b_router_group, w_router_expert, b_router_expert,
           w_exp_gate, w_exp_up, w_exp_down, final_g):
    bsz, seq, d = x.shape
    depth = w_in.shape[0]
    n = bsz * seq
    sb_width = SB_HEADS * SB_HEAD_DIM
    hg_width = HG_HEADS * HG_DIM

    tri_attn = _suffix_ones(min(ATTN_T, seq))
    bias_attn = _causal_bias(min(ATTN_T, seq))
    tt = min(HGRN_TT, seq)
    tri_hg = _chunk_prefix_ones(tt, min(HGRN_C, tt))
    bd = _block_diag_ones(min(MXU_DIM, hg_width), HG_DIM)

    x2 = x.reshape(n, d)
    for l in range(depth):
        qkv, rest = _inproj(x2, ln1_g[l], w_in[l], sb_width)
        y_sb = _attn(qkv, bsz, seq, tri_attn, bias_attn)
        y_hg = _hgrn(rest, hg_lb_logits, hg_norm_g[l], bsz, seq, l, tri_hg, bd)

        pad = LANES - N_EXPERTS - N_GROUPS
        wr = jnp.concatenate([w_router_expert[l], w_router_group[l],
                              jnp.zeros((d, pad), F32)], axis=1)
        wr_hi = wr.astype(BF16)
        wr_lo = (wr - wr_hi.astype(F32)).astype(BF16)
        wr_split = jnp.concatenate([wr_hi, wr_lo], axis=1)
        br = jnp.concatenate([b_router_expert[l], b_router_group[l],
                              jnp.zeros((pad,), F32)]).reshape(1, LANES)

        last = l == depth - 1
        x1, t, route = _merge(x2, y_sb, y_hg, rest, w_branch_sb[l], w_branch_hg[l], w_out[l],
                              ln2_g[l], wr_split, br)
        assert last, "final rmsnorm is fused into the last layer's combine kernel"
        x2 = _moe_sparse(t, route, w_exp_gate[l], w_exp_up[l], w_exp_down[l], x1, final_g)
    return x2.reshape(bsz, seq, d)
```

```python
import functools

import jax
import jax.numpy as jnp
import numpy as np
from jax import lax
from jax.experimental import pallas as pl
from jax.experimental.pallas import tpu as pltpu

F32 = jnp.float32
BF16 = jnp.bfloat16

EPS = 1e-6
SB_HEADS = 8
SB_HEAD_DIM = 64
HG_HEADS = 8
HG_DIM = 64
N_GROUPS = 4
EXPERTS_PER_GROUP = 4
N_EXPERTS = N_GROUPS * EXPERTS_PER_GROUP

LANES = 128
SUBLANES = 8
MXU_DIM = 256
LOG2E = 1.4426950408889634
VMEM_LIMIT = 48 * 1024 * 1024

INPROJ_TM = 2048
INPROJ_TN = 512
ATTN_T = MXU_DIM
HGRN_TT = 256
HGRN_C = 32
MERGE_TM = 512
MOE_TM = 384
PLAN_TM = 1024
DISPATCH_TM = 1024
COMBINE_TM = 512


def _params(sem):
    return pltpu.CompilerParams(dimension_semantics=sem, vmem_limit_bytes=VMEM_LIMIT)


def _split_dot(x, m, passes):
    acc = None
    r = x
    for p in range(passes):
        h = r.astype(BF16)
        term = jnp.dot(h, m, preferred_element_type=F32)
        acc = term if acc is None else acc + term
        if p + 1 < passes:
            r = r - h.astype(F32)
    return acc


def _inproj_kernel(x_ref, g_ref, w_ref, qkv_ref, rest_ref, h_scr, *, q_scale):
    j = pl.program_id(1)

    @pl.when(j == 0)
    def _():
        x = x_ref[...]
        var = jnp.mean(x * x, axis=-1, keepdims=True)
        h_scr[...] = (x * lax.rsqrt(var + EPS) * g_ref[...]).astype(BF16)

    acc = jnp.dot(h_scr[...], w_ref[...].astype(BF16), preferred_element_type=F32)
    qkv_ref[...] = (acc * jnp.where(j == 0, q_scale, 1.0)).astype(BF16)
    rest_ref[...] = acc


def _inproj(x2, ln_g, w_in, sb_width):
    n, d = x2.shape
    cols = w_in.shape[1]
    tm, tn = min(INPROJ_TM, n), INPROJ_TN
    assert sb_width == tn, "q block must be exactly one column tile"
    nq = 3 * sb_width // tn
    nj = cols // tn
    kern = functools.partial(_inproj_kernel, q_scale=SB_HEAD_DIM ** -0.5)
    return pl.pallas_call(
        kern,
        grid=(n // tm, nj),
        in_specs=[
            pl.BlockSpec((tm, d), lambda i, j: (i, 0)),
            pl.BlockSpec((1, d), lambda i, j: (0, 0)),
            pl.BlockSpec((d, tn), lambda i, j: (0, j)),
        ],
        out_specs=[
            pl.BlockSpec((tm, tn), lambda i, j: (i, jnp.minimum(j, nq))),
            pl.BlockSpec((tm, tn), lambda i, j: (i, jnp.maximum(j - nq, 0))),
        ],
        out_shape=[
            jax.ShapeDtypeStruct((n, (nq + 1) * tn), BF16),
            jax.ShapeDtypeStruct((n, cols - nq * tn), F32),
        ],
        scratch_shapes=[pltpu.VMEM((tm, d), BF16)],
        compiler_params=_params(("parallel", "arbitrary")),
        name="inproj",
    )(x2, ln_g.reshape(1, d), w_in)


ATTN_STAGES = 3
ATTN_STREAMS = 2
MASK_BIAS = -1e30
ATTN_SKIP = 111.0


def _attn_kernel(q_ref, k_ref, v_ref, tri_ref, bias_ref, o_ref,
                 z0, z1, z2, i0, i1, i2, acc_ref, c_ref, *, t, nq):
    zbuf = (z0, z1, z2)
    ibuf = (i0, i1, i2)
    for r in zbuf + ibuf + (acc_ref, c_ref):
        r[...] = jnp.zeros_like(r)

    lane = lax.broadcasted_iota(jnp.int32, (t, LANES), 1)
    head0 = lane < SB_HEAD_DIM
    nt = (((1,), (1,)), ((), ()))

    def stage_a(qi, kj, slot):
        q = q_ref[pl.ds(pl.multiple_of(qi * t, t), t), :]
        zero = jnp.zeros_like(q)
        q2 = jnp.concatenate([jnp.where(head0, q, zero), jnp.where(head0, zero, q)], axis=0)
        k = k_ref[pl.ds(pl.multiple_of(kj * t, t), t), :]
        z = lax.dot_general(q2, k, nt, preferred_element_type=F32)
        bias = bias_ref[(qi == kj).astype(jnp.int32)]
        zbuf[slot][...] = z + jnp.concatenate([bias, bias], axis=0)

    def stage_b(slot):
        z = zbuf[slot][...]
        p = jnp.maximum(z, 0.0) + jnp.log(1.0 + jnp.exp2(jnp.abs(z) * (-LOG2E)))
        incl = jnp.dot(p.astype(BF16), tri_ref[...], preferred_element_type=F32) + p
        ibuf[slot][...] = incl
        return incl[:, 0:1]

    def stage_c(qi, kj, slot, stream):
        first = qi == kj
        incl = ibuf[slot][...]
        c = jnp.where(first, 0.0, c_ref[stream])
        a = jnp.exp2((zbuf[slot][...] - incl - c) * LOG2E)
        v = v_ref[pl.ds(pl.multiple_of(kj * t, t), t), :]
        pv = jnp.dot(a.astype(BF16), v, preferred_element_type=F32)
        acc = jnp.where(first, pv, acc_ref[stream] + pv)
        acc_ref[stream] = acc
        c_ref[stream] = c + incl[:, 0:1]
        o_ref[pl.ds(pl.multiple_of(qi * t, t), t), :] = (
            jnp.where(head0, acc[0:t], acc[t:2 * t]).astype(o_ref.dtype))

    def block(carry):
        qs, ks, qn, kn, skip_q, drained = carry
        qs, ks, qn, kn, skip_q = list(qs), list(ks), list(qn), list(kn), list(skip_q)
        all_done = qn[0] >= nq
        for p in range(1, ATTN_STREAMS):
            all_done = jnp.logical_and(all_done, qn[p] >= nq)
        drained = drained + all_done.astype(jnp.int32)
        for r in range(ATTN_STAGES * ATTN_STREAMS):
            sa, sc, sb = r % ATTN_STAGES, (r + 1) % ATTN_STAGES, (r + 2) % ATTN_STAGES
            pa, pb = r % ATTN_STREAMS, (r - 1) % ATTN_STREAMS
            stage_c(qs[sc], ks[sc], sc, pa)
            total_b = stage_b(sb)
            carry_b = jnp.where(qs[sb] == ks[sb], 0.0, c_ref[pb]) + total_b
            skip = skip_q[pa] == qn[pa]
            q_cur = jnp.where(skip, qn[pa] + ATTN_STREAMS, qn[pa])
            k_cur = jnp.where(skip, qn[pa] + ATTN_STREAMS, kn[pa])
            drain = q_cur >= nq
            qa = jnp.where(drain, 0, q_cur)
            ka = jnp.where(drain, 0, k_cur)
            stage_a(qa, ka, sa)
            last = k_cur == 0
            qn[pa] = jnp.where(jnp.logical_and(last, jnp.logical_not(drain)),
                               q_cur + ATTN_STREAMS, q_cur)
            kn[pa] = jnp.where(drain, k_cur, jnp.where(last, q_cur + ATTN_STREAMS, k_cur - 1))
            skip_q[pb] = jnp.where(jnp.min(carry_b) >= ATTN_SKIP, qs[sb], skip_q[pb])
            qs[sa], ks[sa] = qa, ka
        return tuple(qs), tuple(ks), tuple(qn), tuple(kn), tuple(skip_q), drained

    zero = jnp.int32(0)
    first_q = tuple(jnp.int32(p) for p in range(ATTN_STREAMS))
    init = ((zero,) * ATTN_STAGES, (zero,) * ATTN_STAGES, first_q, first_q,
            (jnp.int32(-1),) * ATTN_STREAMS, zero)
    lax.while_loop(lambda carry: carry[5] < 1, block, init)


def _attn(qkv, bsz, seq, tri, bias):
    n = bsz * seq
    t = min(ATTN_T, seq)
    pairs = SB_HEADS * SB_HEAD_DIM // LANES
    kern = functools.partial(_attn_kernel, t=t, nq=seq // t)
    return pl.pallas_call(
        kern,
        grid=(bsz, pairs),
        in_specs=[
            pl.BlockSpec((seq, LANES), lambda b, p: (b, p)),
            pl.BlockSpec((seq, LANES), lambda b, p: (b, pairs + p)),
            pl.BlockSpec((seq, LANES), lambda b, p: (b, 2 * pairs + p)),
            pl.BlockSpec((t, t), lambda b, p: (0, 0)),
            pl.BlockSpec((2, t, t), lambda b, p: (0, 0, 0)),
        ],
        out_specs=pl.BlockSpec((seq, LANES), lambda b, p: (b, p)),
        out_shape=jax.ShapeDtypeStruct((n, pairs * LANES), BF16),
        scratch_shapes=([pltpu.VMEM((2 * t, t), F32)] * (2 * ATTN_STAGES)
                        + [pltpu.VMEM((ATTN_STREAMS, 2 * t, LANES), F32),
                           pltpu.VMEM((ATTN_STREAMS, 2 * t, 1), F32)]),
        compiler_params=_params(("parallel", "parallel")),
        name="attn",
    )(qkv, qkv, qkv, tri, bias)


def _group_dot(x, bd):
    g = bd.shape[0]
    parts = [jnp.dot(x[:, i:i + g], bd, preferred_element_type=F32)
             for i in range(0, x.shape[1], g)]
    return jnp.concatenate(parts, axis=1)


def _hgrn_kernel(q_ref, f_ref, i_ref, g_ref, lbl_ref, ng_ref, tri_ref, bd_ref, o_ref,
                 st_scr, wpad, vpad, wsh, vsh, *, tt, c, layer):
    ti = pl.program_id(1)
    w = q_ref.shape[1]
    nch = tt // c

    @pl.when(ti == 0)
    def _():
        st_scr[...] = jnp.zeros_like(st_scr)

    lg = lbl_ref[...]
    e = jnp.exp(lg - jnp.max(lg, axis=0, keepdims=True))
    lb = jnp.sum(e[0:layer + 1], axis=0, keepdims=True) / jnp.sum(e, axis=0, keepdims=True)

    f = lb + (1.0 - lb) * jax.nn.sigmoid(f_ref[...])
    kk = 1.0 - f
    qv = q_ref[...]
    qs = qv * jax.nn.sigmoid(qv)
    v = i_ref[...]
    bd = bd_ref[...]
    cum = _split_dot_left(tri_ref[...], jnp.log(f) * LOG2E, 3)
    wk = jnp.log(kk) * LOG2E - cum

    pos = lax.broadcasted_iota(jnp.int32, (tt, 1), 0) % c
    wpad[0:SUBLANES, :] = jnp.zeros((SUBLANES, w), F32)
    vpad[0:SUBLANES, :] = jnp.zeros((SUBLANES, w), F32)
    wpad[SUBLANES:SUBLANES + tt, :] = wk
    vpad[SUBLANES:SUBLANES + tt, :] = v
    for b in range(SUBLANES):
        wb = wpad[SUBLANES - b:SUBLANES - b + tt, :]
        vb = vpad[SUBLANES - b:SUBLANES - b + tt, :]
        if b:
            wb = jnp.where(pos >= b, wb, -jnp.inf)
        wsh[b] = wb.reshape(nch, c, w)
        vsh[b] = vb.reshape(nch, c, w)

    qs3 = qs.reshape(nch, c, w)
    cum3 = cum.reshape(nch, c, w)
    acc = None
    for a in reversed(range(c // SUBLANES)):
        rows = c - SUBLANES * a
        qa = qs3[:, SUBLANES * a:, :].reshape(nch * rows, w)
        ca = cum3[:, SUBLANES * a:, :].reshape(nch * rows, w)
        acc_a = None
        for b in range(SUBLANES):
            wb = wsh[b, :, 0:rows, :].reshape(nch * rows, w)
            vb = vsh[b, :, 0:rows, :].reshape(nch * rows, w)
            dd = qa * jnp.exp2(ca + wb)
            term = _group_dot(dd.astype(BF16), bd) * vb
            acc_a = term if acc_a is None else acc_a + term
        acc_a = acc_a.reshape(nch, rows, w)
        if acc is not None:
            acc_a = acc_a + jnp.concatenate([jnp.zeros((nch, SUBLANES, w), F32), acc], axis=1)
        acc = acc_a
    acc = acc.reshape(tt, w)

    grp = st_scr.shape[1]
    bdmask = bd[0:grp, 0:grp] != 0
    outs = []
    for ci in range(nch):
        r0 = ci * c
        cum_c = cum[r0:r0 + c]
        last = cum_c[c - 1:c]
        qd = (qs[r0:r0 + c] * jnp.exp2(cum_c)).astype(BF16)
        kd = (kk[r0:r0 + c] * jnp.exp2(last - cum_c)).astype(BF16)
        vc = v[r0:r0 + c].astype(BF16)
        dec = jnp.exp2(last)
        o_parts = []
        for gi in range(w // grp):
            sl = slice(gi * grp, (gi + 1) * grp)
            st = st_scr[gi]
            o_parts.append(lax.dot_general(qd[:, sl], st.astype(BF16), (((1,), (1,)), ((), ())),
                                           preferred_element_type=F32))
            upd = lax.dot_general(vc[:, sl], kd[:, sl], (((0,), (0,)), ((), ())),
                                  preferred_element_type=F32)
            st_scr[gi] = st * dec[:, sl] + jnp.where(bdmask, upd, 0.0)
        outs.append(jnp.concatenate(o_parts, axis=1))
    o = acc + jnp.concatenate(outs, axis=0)

    o2 = o * o
    o2_hi = o2.astype(BF16)
    o2_lo = (o2 - o2_hi.astype(F32)).astype(BF16)
    ms = (_group_dot(o2_hi, bd) + _group_dot(o2_lo, bd)) * (1.0 / HG_DIM)
    gv = g_ref[...]
    o = o * lax.rsqrt(ms + EPS) * ng_ref[...] * (gv * jax.nn.sigmoid(gv))
    o_ref[...] = o.astype(o_ref.dtype)


def _split_dot_left(m, x, passes):
    acc = None
    r = x
    for p in range(passes):
        h = r.astype(BF16)
        term = jnp.dot(m, h, preferred_element_type=F32)
        acc = term if acc is None else acc + term
        if p + 1 < passes:
            r = r - h.astype(F32)
    return acc


def _hgrn(rest, lb_logits, norm_g, bsz, seq, layer, tri, bd):
    n = bsz * seq
    w = HG_HEADS * HG_DIM
    grp = bd.shape[0]
    tt = min(HGRN_TT, seq)
    c = min(HGRN_C, tt)
    nt = seq // tt
    kern = functools.partial(_hgrn_kernel, tt=tt, c=c, layer=layer)
    col = lambda j: pl.BlockSpec((tt, w), lambda b, i, j=j: (b * nt + i, j))
    const = lambda shape: pl.BlockSpec(shape, lambda b, i: (0, 0))
    return pl.pallas_call(
        kern,
        grid=(bsz, nt),
        in_specs=[col(0), col(1), col(2), col(3),
                  const(lb_logits.shape), const((1, w)), const((tt, tt)), const((grp, grp))],
        out_specs=pl.BlockSpec((tt, w), lambda b, i: (b * nt + i, 0)),
        out_shape=jax.ShapeDtypeStruct((n, w), BF16),
        scratch_shapes=([pltpu.VMEM((w // LANES, LANES, LANES), F32)]
                        + [pltpu.VMEM((SUBLANES + tt, w), F32)] * 2
                        + [pltpu.VMEM((SUBLANES, tt // c, c, w), F32)] * 2),
        compiler_params=_params(("parallel", "arbitrary")),
        name="hgrn",
    )(rest, rest, rest, rest, lb_logits, norm_g.reshape(1, w), tri, bd)


PAIR_A = (0, 0, 0, 1, 1, 3)
PAIR_B = (1, 2, 3, 3, 2, 2)
N_PAIRS = len(PAIR_A)
N_CLASSES = N_GROUPS * N_PAIRS
assert sorted(tuple(sorted(p)) for p in zip(PAIR_A, PAIR_B)) == [
    (a, b) for a in range(EXPERTS_PER_GROUP) for b in range(a + 1, EXPERTS_PER_GROUP)]
ROUTE_CLS, ROUTE_WA, ROUTE_WB = 0, 1, 2


def _pair_slots(pidx):
    a = b = jnp.zeros_like(pidx)
    for p in range(N_PAIRS):
        a = jnp.where(pidx == p, float(PAIR_A[p]), a)
        b = jnp.where(pidx == p, float(PAIR_B[p]), b)
    return a, b


def _pair_index(lo, hi):
    pidx = jnp.zeros_like(lo)
    for p in range(N_PAIRS):
        is_p = jnp.logical_and(lo == min(PAIR_A[p], PAIR_B[p]), hi == max(PAIR_A[p], PAIR_B[p]))
        pidx = jnp.where(is_p, float(p), pidx)
    return pidx


def _route(lg):
    lane = lax.broadcasted_iota(jnp.int32, lg.shape, 1)
    neg = jnp.float32(-jnp.inf)
    big = jnp.int32(LANES)
    gmask = jnp.logical_and(lane >= N_EXPERTS, lane < N_EXPERTS + N_GROUPS)
    gl = jnp.where(gmask, lg, neg)
    gmax = jnp.max(gl, axis=1, keepdims=True)
    gidx = jnp.min(jnp.where(gl == gmax, lane, big), axis=1, keepdims=True) - N_EXPERTS
    w_grp = 1.0 / jnp.sum(jnp.where(gmask, jnp.exp(gl - gmax), 0.0), axis=1, keepdims=True)
    in_grp = jnp.logical_and(lane < N_EXPERTS, lane // EXPERTS_PER_GROUP == gidx)
    l1 = jnp.where(in_grp, lg, neg)
    v1 = jnp.max(l1, axis=1, keepdims=True)
    i1 = jnp.min(jnp.where(l1 == v1, lane, big), axis=1, keepdims=True)
    l2 = jnp.where(jnp.logical_and(in_grp, lane != i1), lg, neg)
    v2 = jnp.max(l2, axis=1, keepdims=True)
    i2 = jnp.min(jnp.where(l2 == v2, lane, big), axis=1, keepdims=True)
    e2 = jnp.exp(v2 - v1)
    p1 = 1.0 / (1.0 + e2)
    p2 = e2 * p1
    loc1 = (i1 - gidx * EXPERTS_PER_GROUP).astype(F32)
    loc2 = (i2 - gidx * EXPERTS_PER_GROUP).astype(F32)
    pidx = _pair_index(jnp.minimum(loc1, loc2), jnp.maximum(loc1, loc2))
    cls = gidx.astype(F32) * N_PAIRS + pidx
    a_loc, _ = _pair_slots(pidx)
    first_is_a = loc1 == a_loc
    wa = jnp.where(first_is_a, p1, p2) * w_grp
    wb = jnp.where(first_is_a, p2, p1) * w_grp
    return jnp.where(lane == ROUTE_CLS, cls,
                     jnp.where(lane == ROUTE_WA, wa, jnp.where(lane == ROUTE_WB, wb, 0.0)))


def _store_token_tiles(ref, x):
    rows, d = x.shape
    assert d == SUBLANES * LANES
    for s in range(SUBLANES):
        ref[pl.ds(s, rows, stride=SUBLANES), :] = x[:, s * LANES:(s + 1) * LANES]


def _load_token_tiles(ref):
    rows = ref.shape[0] // SUBLANES
    return jnp.concatenate([ref[pl.ds(s, rows, stride=SUBLANES), :] for s in range(SUBLANES)],
                           axis=1)


def _merge_kernel(x_ref, ysb_ref, yhg_ref, gsb_ref, ghg_ref, wbs_ref, wbh_ref, wo_ref,
                  ln_ref, wr_ref, br_ref, x1_ref, t_ref, route_ref, wbs_scr, wbh_scr, wo_scr):
    @pl.when(pl.program_id(0) == 0)
    def _():
        wbs_scr[...] = wbs_ref[...].astype(BF16)
        wbh_scr[...] = wbh_ref[...].astype(BF16)
        wo_scr[...] = wo_ref[...].astype(BF16)

    a = jnp.dot(ysb_ref[...], wbs_scr[...], preferred_element_type=F32)
    b = jnp.dot(yhg_ref[...], wbh_scr[...], preferred_element_type=F32)
    merged = jax.nn.sigmoid(gsb_ref[...]) * a + jax.nn.sigmoid(ghg_ref[...]) * b
    x1 = x_ref[...] + jnp.dot(merged.astype(BF16), wo_scr[...], preferred_element_type=F32)
    x1_ref[...] = x1
    var = jnp.mean(x1 * x1, axis=-1, keepdims=True)
    t = x1 * lax.rsqrt(var + EPS) * ln_ref[...]
    _store_token_tiles(t_ref, t)
    t_hi = t.astype(BF16)
    t_lo = (t - t_hi.astype(F32)).astype(BF16)
    p_hi = jnp.dot(t_hi, wr_ref[...], preferred_element_type=F32)
    p_lo = jnp.dot(t_lo, wr_ref[...], preferred_element_type=F32)
    lg = (p_hi[:, :LANES] + p_hi[:, LANES:]) + (p_lo[:, :LANES] + p_lo[:, LANES:]) + br_ref[...]
    route_ref[...] = _route(lg)


def _merge(x2, y_sb, y_hg, rest, wbs, wbh, wo, ln_g, wr, br):
    n, d = x2.shape
    tm = min(MERGE_TM, n)
    w_sb, w_hg = y_sb.shape[1], y_hg.shape[1]
    gate_blk = (rest.shape[1] - 2 * d) // d
    row = lambda wdt, j=0: pl.BlockSpec((tm, wdt), lambda i, j=j: (i, j))
    const = lambda shape: pl.BlockSpec(shape, lambda i: (0, 0))
    return pl.pallas_call(
        _merge_kernel,
        grid=(n // tm,),
        in_specs=[row(d), row(w_sb), row(w_hg), row(d, gate_blk), row(d, gate_blk + 1),
                  const(wbs.shape), const(wbh.shape), const(wo.shape), const((1, d)),
                  const(wr.shape), const((1, LANES))],
        out_specs=[row(d), pl.BlockSpec((tm * SUBLANES, LANES), lambda i: (i, 0)), row(LANES)],
        out_shape=[jax.ShapeDtypeStruct((n, d), F32),
                   jax.ShapeDtypeStruct((n * SUBLANES, LANES), F32),
                   jax.ShapeDtypeStruct((n, LANES), F32)],
        scratch_shapes=[pltpu.VMEM(w.shape, BF16) for w in (wbs, wbh, wo)],
        compiler_params=_params(("arbitrary",)),
        name="merge",
    )(x2, y_sb, y_hg, rest, rest, wbs, wbh, wo, ln_g.reshape(1, d), wr, br)


TOKEN_ROWS_OUT = 2 * SUBLANES


def _plan_kernel(route_ref, ltri_ref, utri_ref, pos_ref, te_ref, meta_ref,
                 tot_scr, run_scr, *, tile_rows):
    ph = pl.program_id(0)
    i = pl.program_id(1)
    route = route_ref[...]
    tm = route.shape[0]
    lane = lax.broadcasted_iota(jnp.int32, (tm, LANES), 1)
    is_cls = lane.astype(F32) == route[:, ROUTE_CLS:ROUTE_CLS + 1]
    sel = jnp.where(is_cls, 1.0, 0.0)

    @pl.when(jnp.logical_and(ph == 0, i == 0))
    def _():
        tot_scr[...] = jnp.zeros_like(tot_scr)

    @pl.when(ph == 0)
    def _():
        tot_scr[...] += jnp.sum(sel, axis=0, keepdims=True)

    @pl.when(ph == 1)
    def _():
        tot = tot_scr[...]
        top = tot + (tile_rows - 1.0)
        tiles = jnp.floor(top * (1.0 / tile_rows))
        tiles = jnp.where(tiles * tile_rows > top, tiles - 1.0, tiles)
        tiles = jnp.where((tiles + 1.0) * tile_rows <= top, tiles + 1.0, tiles)
        first_tile = _split_dot(jnp.broadcast_to(tiles, (SUBLANES, LANES)), utri_ref[...], 3)[0:1]

        @pl.when(i == 0)
        def _():
            run_scr[...] = jnp.zeros_like(run_scr)
            lane1 = lax.broadcasted_iota(jnp.int32, (1, LANES), 1)
            is_c = lane1 < N_CLASSES
            end_tile = first_tile + tiles
            n_valid = jnp.sum(jnp.where(is_c, tiles, 0.0), axis=1, keepdims=True)
            rows = te_ref.shape[0]
            j = lax.broadcasted_iota(jnp.int32, (rows, LANES), 0).astype(F32)
            j = jnp.minimum(j, n_valid - 1.0)
            lane2 = lax.broadcasted_iota(jnp.int32, (rows, LANES), 1)
            done = jnp.where(jnp.logical_and(lane2 < N_CLASSES, end_tile <= j), 1.0, 0.0)
            tc = jnp.sum(done, axis=1, keepdims=True)
            tg = jnp.floor((tc + 0.5) * (1.0 / N_PAIRS))
            ta, tb = _pair_slots(tc - tg * N_PAIRS)
            te = jnp.where(lane2 == 0, tg * EXPERTS_PER_GROUP + ta, tg * EXPERTS_PER_GROUP + tb)
            te_ref[...] = te.astype(jnp.int32)
            last_tile = jnp.where(jnp.logical_and(is_c, tiles > 0), end_tile - 1.0, -1.0)
            meta = jnp.where(lane1 == N_CLASSES, n_valid, last_tile)
            meta_ref[...] = jnp.broadcast_to(meta, meta_ref.shape).astype(jnp.int32)

        rank = jnp.dot(ltri_ref[...], sel.astype(BF16), preferred_element_type=F32) + run_scr[...]
        dest = jnp.sum(jnp.where(is_cls, first_tile * tile_rows + rank, 0.0),
                       axis=1, keepdims=True)
        row = lax.broadcasted_iota(jnp.int32, (tm, LANES), 0)
        spread = jnp.where(lane == row % LANES, dest, 0.0)
        pos_ref[...] = jnp.sum(spread.reshape(tm // LANES, LANES, LANES), axis=1).astype(jnp.int32)
        run_scr[...] += jnp.sum(sel, axis=0, keepdims=True)


def _plan(route, n_tiles_max):
    n = route.shape[0]
    tm = min(PLAN_TM, n)
    r = np.arange(tm)
    ltri = jnp.asarray((r[None, :] < r[:, None]).astype(np.float32), dtype=BF16)
    e = np.arange(LANES)
    utri = jnp.asarray((e[:, None] < e[None, :]).astype(np.float32), dtype=BF16)
    te_rows = -(-n_tiles_max // SUBLANES) * SUBLANES
    kern = functools.partial(_plan_kernel, tile_rows=MOE_TM)
    return pl.pallas_call(
        kern,
        grid=(2, n // tm),
        in_specs=[pl.BlockSpec((tm, LANES), lambda p, i: (i, 0)),
                  pl.BlockSpec((tm, tm), lambda p, i: (0, 0)),
                  pl.BlockSpec((LANES, LANES), lambda p, i: (0, 0))],
        out_specs=[pl.BlockSpec((tm // LANES, LANES), lambda p, i: (i * p, 0)),
                   pl.BlockSpec((te_rows, LANES), lambda p, i: (0, 0)),
                   pl.BlockSpec((SUBLANES, LANES), lambda p, i: (0, 0))],
        out_shape=[jax.ShapeDtypeStruct((n // LANES, LANES), jnp.int32),
                   jax.ShapeDtypeStruct((te_rows, LANES), jnp.int32),
                   jax.ShapeDtypeStruct((SUBLANES, LANES), jnp.int32)],
        scratch_shapes=[pltpu.VMEM((1, LANES), F32), pltpu.VMEM((1, LANES), F32)],
        compiler_params=_params(("arbitrary", "arbitrary")),
        name="plan",
    )(route, ltri, utri)


DMA_UNROLL = 8


def _token_copy(src_ref, r, dst_ref, p, sem, rows):
    return pltpu.make_async_copy(src_ref.at[pl.ds(pl.multiple_of(r * rows, rows), rows)],
                                 dst_ref.at[pl.ds(pl.multiple_of(p * rows, rows), rows)], sem)


def _dispatch_kernel(pos_ref, meta_ref, t_ref, xs_ref, zero_scr, sem, tail_sem, *, tm, tile_rows):
    i = pl.program_id(0)
    tile_rows = tile_rows * SUBLANES
    n_tiles = xs_ref.shape[0] // tile_rows
    n_valid = meta_ref[N_CLASSES]

    def clear(tile, s):
        return pltpu.make_async_copy(zero_scr, xs_ref.at[pl.ds(tile * tile_rows, tile_rows)], s)

    def for_each(clears, fn):
        for tile, cond in clears:
            @pl.when(cond)
            def _():
                fn(tile)

    last_tiles = [(meta_ref[c], meta_ref[c] >= 0) for c in range(N_CLASSES)]
    past_tiles = [(n_tiles - 1 - k, n_tiles - 1 - k >= n_valid) for k in range(N_CLASSES)]

    @pl.when(i == 0)
    def _():
        zero_scr[...] = jnp.zeros_like(zero_scr)
        for_each(last_tiles, lambda tile: clear(tile, sem).start())
        for_each(past_tiles, lambda tile: clear(tile, tail_sem).start())
        for_each(last_tiles, lambda tile: clear(tile, sem).wait())

    base = i * tm

    def start(r2, carry):
        for k in range(2):
            r = 2 * r2 + k
            _token_copy(t_ref, r, xs_ref, pos_ref[base + r], sem, SUBLANES).start(priority=k)
        return carry

    lax.fori_loop(0, tm // 2, start, 0, unroll=DMA_UNROLL)
    pltpu.make_async_copy(t_ref, xs_ref.at[pl.ds(0, tm * SUBLANES)], sem).wait()

    @pl.when(i == pl.num_programs(0) - 1)
    def _():
        for_each(past_tiles, lambda tile: clear(tile, tail_sem).wait())


def _dispatch(pos, meta, t, n_rows):
    n = t.shape[0] // SUBLANES
    tm = min(DISPATCH_TM, n)
    kern = functools.partial(_dispatch_kernel, tm=tm, tile_rows=MOE_TM)
    grid_spec = pltpu.PrefetchScalarGridSpec(
        num_scalar_prefetch=2,
        grid=(n // tm,),
        in_specs=[pl.BlockSpec((tm * SUBLANES, LANES), lambda i, *_: (i, 0))],
        out_specs=pl.BlockSpec(memory_space=pl.ANY),
        scratch_shapes=[pltpu.VMEM((MOE_TM * SUBLANES, LANES), F32),
                        pltpu.SemaphoreType.DMA(()), pltpu.SemaphoreType.DMA(())],
    )
    return pl.pallas_call(
        kern,
        grid_spec=grid_spec,
        out_shape=jax.ShapeDtypeStruct((n_rows * SUBLANES, LANES), F32),
        compiler_params=_params(("arbitrary",)),
        name="dispatch",
    )(pos, meta, t)


def _expert_kernel(tea_ref, teb_ref, nv_ref, xs_ref, wga, wua, wda, wgb, wub, wdb, ys_ref):
    in_use = pl.program_id(0) < nv_ref[0]
    rows = xs_ref.shape[0] // SUBLANES

    @pl.when(in_use)
    def _():
        x = _load_token_tiles(xs_ref).astype(BF16)
        for slot, (wg, wu, wd) in enumerate(((wga, wua, wda), (wgb, wub, wdb))):
            hg = jnp.dot(x, wg[...].astype(BF16), preferred_element_type=F32)
            hu = jnp.dot(x, wu[...].astype(BF16), preferred_element_type=F32)
            a = (hg * jax.nn.sigmoid(hg) * hu).astype(BF16)
            y = jnp.dot(a, wd[...].astype(BF16), preferred_element_type=F32)
            for s in range(SUBLANES):
                ys_ref[pl.ds(slot * SUBLANES + s, rows, stride=TOKEN_ROWS_OUT), :] = (
                    y[:, s * LANES:(s + 1) * LANES])

    @pl.when(jnp.logical_not(in_use))
    def _():
        ys_ref[...] = jnp.zeros_like(ys_ref)


def _experts(tea, teb, nv, xs, wg, wu, wd):
    _, d, de = wg.shape
    assert d == SUBLANES * LANES
    n_tiles = xs.shape[0] // (MOE_TM * SUBLANES)
    wa = lambda shape: pl.BlockSpec(shape, lambda j, tea, teb, nv: (tea[j], 0, 0))
    wb = lambda shape: pl.BlockSpec(shape, lambda j, tea, teb, nv: (teb[j], 0, 0))
    grid_spec = pltpu.PrefetchScalarGridSpec(
        num_scalar_prefetch=3,
        grid=(n_tiles,),
        in_specs=[pl.BlockSpec((MOE_TM * SUBLANES, LANES),
                               lambda j, tea, teb, nv: (jnp.minimum(j, nv[0] - 1), 0)),
                  wa((None, d, de)), wa((None, d, de)), wa((None, de, d)),
                  wb((None, d, de)), wb((None, d, de)), wb((None, de, d))],
        out_specs=pl.BlockSpec((MOE_TM * TOKEN_ROWS_OUT, LANES), lambda j, *_: (j, 0)),
    )
    return pl.pallas_call(
        _expert_kernel,
        grid_spec=grid_spec,
        out_shape=jax.ShapeDtypeStruct((n_tiles * MOE_TM * TOKEN_ROWS_OUT, LANES), F32),
        compiler_params=_params(("arbitrary",)),
        name="experts",
    )(tea, teb, nv, xs, wg, wu, wd, wg, wu, wd)


def _combine_kernel(pos_ref, ys_ref, x1_ref, route_ref, fg_ref, o_ref, g, sem, *, tm):
    i = pl.program_id(0)
    slot = i % 2

    def gather(step, buf):
        base = step * tm

        def body(r2, carry):
            for k in range(2):
                r = 2 * r2 + k
                _token_copy(ys_ref, pos_ref[base + r], g.at[buf], r, sem.at[buf],
                            TOKEN_ROWS_OUT).start(priority=k)
            return carry

        lax.fori_loop(0, tm // 2, body, 0, unroll=DMA_UNROLL)

    @pl.when(i == 0)
    def _():
        gather(0, 0)

    @pl.when(i + 1 < pl.num_programs(0))
    def _():
        gather(i + 1, 1 - slot)

    pltpu.make_async_copy(ys_ref.at[pl.ds(0, tm * TOKEN_ROWS_OUT)], g.at[slot],
                          sem.at[slot]).wait()
    gs = g.at[slot]
    ya, yb = (jnp.concatenate([gs[pl.ds(off + s, tm, stride=TOKEN_ROWS_OUT), :]
                               for s in range(SUBLANES)], axis=1) for off in (0, SUBLANES))
    route = route_ref[...]
    x2 = x1_ref[...] + (route[:, ROUTE_WA:ROUTE_WA + 1] * ya + route[:, ROUTE_WB:ROUTE_WB + 1] * yb)
    var = jnp.mean(x2 * x2, axis=-1, keepdims=True)
    o_ref[...] = x2 * lax.rsqrt(var + EPS) * fg_ref[...]


def _combine(pos, ys, x1, route, final_g):
    n, d = x1.shape
    tm = min(COMBINE_TM, n)
    kern = functools.partial(_combine_kernel, tm=tm)
    grid_spec = pltpu.PrefetchScalarGridSpec(
        num_scalar_prefetch=1,
        grid=(n // tm,),
        in_specs=[pl.BlockSpec(memory_space=pl.ANY),
                  pl.BlockSpec((tm, d), lambda i, *_: (i, 0)),
                  pl.BlockSpec((tm, LANES), lambda i, *_: (i, 0)),
                  pl.BlockSpec((1, d), lambda i, *_: (0, 0))],
        out_specs=pl.BlockSpec((tm, d), lambda i, *_: (i, 0)),
        scratch_shapes=[pltpu.VMEM((2, tm * TOKEN_ROWS_OUT, LANES), F32),
                        pltpu.SemaphoreType.DMA((2,))],
    )
    return pl.pallas_call(
        kern,
        grid_spec=grid_spec,
        out_shape=jax.ShapeDtypeStruct((n, d), F32),
        compiler_params=_params(("arbitrary",)),
        name="combine",
    )(pos, ys, x1, route, final_g.reshape(1, d))


def _moe_sparse(t, route, wg, wu, wd, x1, final_g):
    n = x1.shape[0]
    n_tiles_max = (n + N_CLASSES * (MOE_TM - 1)) // MOE_TM
    pos, te, meta = _plan(route, n_tiles_max)
    pos = pos.reshape(n)
    xs = _dispatch(pos, meta[0, :N_CLASSES + 1], t, n_tiles_max * MOE_TM)
    ys = _experts(te[:n_tiles_max, 0], te[:n_tiles_max, 1], meta[0, N_CLASSES:N_CLASSES + 1],
                  xs, wg, wu, wd)
    return _combine(pos, ys, x1, route, final_g)


def _suffix_ones(t):
    j = np.arange(t)[:, None]
    s = np.arange(t)[None, :]
    return jnp.asarray((j > s).astype(np.float32), dtype=BF16)


def _causal_bias(t):
    row = np.arange(t)[:, None]
    col = np.arange(t)[None, :]
    diag = np.where(col < row, 0.0, MASK_BIAS).astype(np.float32)
    return jnp.asarray(np.stack([np.zeros_like(diag), diag]))


def _chunk_prefix_ones(tt, c):
    t = np.arange(tt)[:, None]
    j = np.arange(tt)[None, :]
    return jnp.asarray(((j <= t) & (t // c == j // c)).astype(np.float32), dtype=BF16)


def _block_diag_ones(w, blk):
    a = np.arange(w)
    return jnp.asarray((a[:, None] // blk == a[None, :] // blk).astype(np.float32), dtype=BF16)


def kernel(x, ln1_g, w_in, w_branch_sb, w_branch_hg, hg_norm_g, hg_lb_logits, w_out, ln2_g,
           w_router_group, b_router_group, w_router_expert, b_router_expert,
           w_exp_gate, w_exp_up, w_exp_down, final_g):
    bsz, seq, d = x.shape
    depth = w_in.shape[0]
    n = bsz * seq
    sb_width = SB_HEADS * SB_HEAD_DIM
    hg_width = HG_HEADS * HG_DIM

    tri_attn = _suffix_ones(min(ATTN_T, seq))
    bias_attn = _causal_bias(min(ATTN_T, seq))
    tt = min(HGRN_TT, seq)
    tri_hg = _chunk_prefix_ones(tt, min(HGRN_C, tt))
    bd = _block_diag_ones(min(MXU_DIM, hg_width), HG_DIM)

    x2 = x.reshape(n, d)
    for l in range(depth):
        qkv, rest = _inproj(x2, ln1_g[l], w_in[l], sb_width)
        y_sb = _attn(qkv, bsz, seq, tri_attn, bias_attn)
        y_hg = _hgrn(rest, hg_lb_logits, hg_norm_g[l], bsz, seq, l, tri_hg, bd)

        pad = LANES - N_EXPERTS - N_GROUPS
        wr = jnp.concatenate([w_router_expert[l], w_router_group[l],
                              jnp.zeros((d, pad), F32)], axis=1)
        wr_hi = wr.astype(BF16)
        wr_lo = (wr - wr_hi.astype(F32)).astype(BF16)
        wr_split = jnp.concatenate([wr_hi, wr_lo], axis=1)
        br = jnp.concatenate([b_router_expert[l], b_router_group[l],
                              jnp.zeros((pad,), F32)]).reshape(1, LANES)

        last = l == depth - 1
        x1, t, route = _merge(x2, y_sb, y_hg, rest, w_branch_sb[l], w_branch_hg[l], w_out[l],
                              ln2_g[l], wr_split, br)
        assert last, "final rmsnorm is fused into the last layer's combine kernel"
        x2 = _moe_sparse(t, route, w_exp_gate[l], w_exp_up[l], w_exp_down[l], x1, final_g)
    return x2.reshape(bsz, seq, d)
```

```python
import functools

import jax
import jax.numpy as jnp
import numpy as np
from jax import lax
from jax.experimental import pallas as pl
from jax.experimental.pallas import tpu as pltpu

F32 = jnp.float32
BF16 = jnp.bfloat16

EPS = 1e-6
SB_HEADS = 8
SB_HEAD_DIM = 64
HG_HEADS = 8
HG_DIM = 64
N_GROUPS = 4
EXPERTS_PER_GROUP = 4
N_EXPERTS = N_GROUPS * EXPERTS_PER_GROUP

LANES = 128
SUBLANES = 8
MXU_DIM = 256
LOG2E = 1.4426950408889634
VMEM_LIMIT = 56 * 1024 * 1024

INPROJ_TM = 2048
INPROJ_TN = 512
ATTN_T = MXU_DIM
HGRN_TT = 256
HGRN_C = 32
MERGE_TM = 512
MOE_TM = 384
PLAN_TM = 1024
DISPATCH_TM = 1024
COMBINE_TM = 512


def _params(sem):
    return pltpu.CompilerParams(dimension_semantics=sem, vmem_limit_bytes=VMEM_LIMIT)


def _split_dot(x, m, passes):
    acc = None
    r = x
    for p in range(passes):
        h = r.astype(BF16)
        term = jnp.dot(h, m, preferred_element_type=F32)
        acc = term if acc is None else acc + term
        if p + 1 < passes:
            r = r - h.astype(F32)
    return acc


def _inproj_kernel(x_ref, g_ref, w_ref, qkv_ref, rest_ref, h_scr, w_scr, *, q_scale):
    i = pl.program_id(0)
    j = pl.program_id(1)

    @pl.when(j == 0)
    def _():
        x = x_ref[...]
        var = jnp.mean(x * x, axis=-1, keepdims=True)
        h_scr[...] = (x * lax.rsqrt(var + EPS) * g_ref[...]).astype(BF16)

    @pl.when(i == 0)
    def _():
        w_scr[j] = w_ref[...].astype(BF16)

    acc = jnp.dot(h_scr[...], w_scr[j], preferred_element_type=F32)
    qkv_ref[...] = (acc * jnp.where(j == 0, q_scale, 1.0)).astype(BF16)
    rest_ref[...] = acc


def _inproj(x2, ln_g, w_in, sb_width):
    n, d = x2.shape
    cols = w_in.shape[1]
    tm, tn = min(INPROJ_TM, n), INPROJ_TN
    assert sb_width == tn, "q block must be exactly one column tile"
    nq = 3 * sb_width // tn
    nj = cols // tn
    kern = functools.partial(_inproj_kernel, q_scale=SB_HEAD_DIM ** -0.5)
    return pl.pallas_call(
        kern,
        grid=(n // tm, nj),
        in_specs=[
            pl.BlockSpec((tm, d), lambda i, j: (i, 0)),
            pl.BlockSpec((1, d), lambda i, j: (0, 0)),
            pl.BlockSpec((d, tn), lambda i, j: (0, jnp.where(i == 0, j, nj - 1))),
        ],
        out_specs=[
            pl.BlockSpec((tm, tn), lambda i, j: (i, jnp.minimum(j, nq))),
            pl.BlockSpec((tm, tn), lambda i, j: (i, jnp.maximum(j - nq, 0))),
        ],
        out_shape=[
            jax.ShapeDtypeStruct((n, (nq + 1) * tn), BF16),
            jax.ShapeDtypeStruct((n, cols - nq * tn), F32),
        ],
        scratch_shapes=[pltpu.VMEM((tm, d), BF16), pltpu.VMEM((nj, d, tn), BF16)],
        compiler_params=_params(("arbitrary", "arbitrary")),
        name="inproj",
    )(x2, ln_g.reshape(1, d), w_in)


ATTN_STAGES = 3
ATTN_STREAMS = 2
MASK_BIAS = -1e30
ATTN_SKIP = 111.0


def _attn_kernel(q_ref, k_ref, v_ref, tri_ref, bias_ref, o_ref,
                 z0, z1, z2, i0, i1, i2, acc_ref, c_ref, *, t, nq):
    zbuf = (z0, z1, z2)
    ibuf = (i0, i1, i2)
    for r in zbuf + ibuf + (acc_ref, c_ref):
        r[...] = jnp.zeros_like(r)

    lane = lax.broadcasted_iota(jnp.int32, (t, LANES), 1)
    head0 = lane < SB_HEAD_DIM
    nt = (((1,), (1,)), ((), ()))

    def stage_a(qi, kj, slot):
        q = q_ref[pl.ds(pl.multiple_of(qi * t, t), t), :]
        zero = jnp.zeros_like(q)
        q2 = jnp.concatenate([jnp.where(head0, q, zero), jnp.where(head0, zero, q)], axis=0)
        k = k_ref[pl.ds(pl.multiple_of(kj * t, t), t), :]
        z = lax.dot_general(q2, k, nt, preferred_element_type=F32)
        bias = bias_ref[(qi == kj).astype(jnp.int32)]
        zbuf[slot][...] = z + jnp.concatenate([bias, bias], axis=0)

    def stage_b(slot):
        z = zbuf[slot][...]
        p = jnp.maximum(z, 0.0) + jnp.log(1.0 + jnp.exp2(jnp.abs(z) * (-LOG2E)))
        incl = jnp.dot(p.astype(BF16), tri_ref[...], preferred_element_type=F32) + p
        ibuf[slot][...] = incl
        return incl[:, 0:1]

    def stage_c(qi, kj, slot, stream):
        first = qi == kj
        incl = ibuf[slot][...]
        c = jnp.where(first, 0.0, c_ref[stream])
        a = jnp.exp2((zbuf[slot][...] - incl - c) * LOG2E)
        v = v_ref[pl.ds(pl.multiple_of(kj * t, t), t), :]
        pv = jnp.dot(a.astype(BF16), v, preferred_element_type=F32)
        acc = jnp.where(first, pv, acc_ref[stream] + pv)
        acc_ref[stream] = acc
        c_ref[stream] = c + incl[:, 0:1]
        o_ref[pl.ds(pl.multiple_of(qi * t, t), t), :] = (
            jnp.where(head0, acc[0:t], acc[t:2 * t]).astype(o_ref.dtype))

    def block(carry):
        qs, ks, qn, kn, skip_q, drained = carry
        qs, ks, qn, kn, skip_q = list(qs), list(ks), list(qn), list(kn), list(skip_q)
        all_done = qn[0] >= nq
        for p in range(1, ATTN_STREAMS):
            all_done = jnp.logical_and(all_done, qn[p] >= nq)
        drained = drained + all_done.astype(jnp.int32)
        for r in range(ATTN_STAGES * ATTN_STREAMS):
            sa, sc, sb = r % ATTN_STAGES, (r + 1) % ATTN_STAGES, (r + 2) % ATTN_STAGES
            pa, pb = r % ATTN_STREAMS, (r - 1) % ATTN_STREAMS
            stage_c(qs[sc], ks[sc], sc, pa)
            total_b = stage_b(sb)
            carry_b = jnp.where(qs[sb] == ks[sb], 0.0, c_ref[pb]) + total_b
            skip = skip_q[pa] == qn[pa]
            q_cur = jnp.where(skip, qn[pa] + ATTN_STREAMS, qn[pa])
            k_cur = jnp.where(skip, qn[pa] + ATTN_STREAMS, kn[pa])
            drain = q_cur >= nq
            qa = jnp.where(drain, 0, q_cur)
            ka = jnp.where(drain, 0, k_cur)
            stage_a(qa, ka, sa)
            last = k_cur == 0
            qn[pa] = jnp.where(jnp.logical_and(last, jnp.logical_not(drain)),
                               q_cur + ATTN_STREAMS, q_cur)
            kn[pa] = jnp.where(drain, k_cur, jnp.where(last, q_cur + ATTN_STREAMS, k_cur - 1))
            skip_q[pb] = jnp.where(jnp.min(carry_b) >= ATTN_SKIP, qs[sb], skip_q[pb])
            qs[sa], ks[sa] = qa, ka
        return tuple(qs), tuple(ks), tuple(qn), tuple(kn), tuple(skip_q), drained

    zero = jnp.int32(0)
    first_q = tuple(jnp.int32(p) for p in range(ATTN_STREAMS))
    init = ((zero,) * ATTN_STAGES, (zero,) * ATTN_STAGES, first_q, first_q,
            (jnp.int32(-1),) * ATTN_STREAMS, zero)
    lax.while_loop(lambda carry: carry[5] < 1, block, init)


def _attn(qkv, bsz, seq, tri, bias):
    n = bsz * seq
    t = min(ATTN_T, seq)
    pairs = SB_HEADS * SB_HEAD_DIM // LANES
    kern = functools.partial(_attn_kernel, t=t, nq=seq // t)
    return pl.pallas_call(
        kern,
        grid=(bsz, pairs),
        in_specs=[
            pl.BlockSpec((seq, LANES), lambda b, p: (b, p)),
            pl.BlockSpec((seq, LANES), lambda b, p: (b, pairs + p)),
            pl.BlockSpec((seq, LANES), lambda b, p: (b, 2 * pairs + p)),
            pl.BlockSpec((t, t), lambda b, p: (0, 0)),
            pl.BlockSpec((2, t, t), lambda b, p: (0, 0, 0)),
        ],
        out_specs=pl.BlockSpec((seq, LANES), lambda b, p: (b, p)),
        out_shape=jax.ShapeDtypeStruct((n, pairs * LANES), BF16),
        scratch_shapes=([pltpu.VMEM((2 * t, t), F32)] * (2 * ATTN_STAGES)
                        + [pltpu.VMEM((ATTN_STREAMS, 2 * t, LANES), F32),
                           pltpu.VMEM((ATTN_STREAMS, 2 * t, 1), F32)]),
        compiler_params=_params(("parallel", "parallel")),
        name="attn",
    )(qkv, qkv, qkv, tri, bias)


def _group_dot(x, bd):
    g = bd.shape[0]
    parts = [jnp.dot(x[:, i:i + g], bd, preferred_element_type=F32)
             for i in range(0, x.shape[1], g)]
    return jnp.concatenate(parts, axis=1)


def _hgrn_kernel(q_ref, f_ref, i_ref, g_ref, lbl_ref, ng_ref, tri_ref, bd_ref, o_ref,
                 st_scr, wpad, vpad, wsh, vsh, *, tt, c, layer):
    ti = pl.program_id(1)
    w = q_ref.shape[1]
    nch = tt // c

    @pl.when(ti == 0)
    def _():
        st_scr[...] = jnp.zeros_like(st_scr)

    lg = lbl_ref[...]
    e = jnp.exp(lg - jnp.max(lg, axis=0, keepdims=True))
    lb = jnp.sum(e[0:layer + 1], axis=0, keepdims=True) / jnp.sum(e, axis=0, keepdims=True)

    f = lb + (1.0 - lb) * jax.nn.sigmoid(f_ref[...])
    kk = 1.0 - f
    qv = q_ref[...]
    qs = qv * jax.nn.sigmoid(qv)
    v = i_ref[...]
    bd = bd_ref[...]
    cum = _split_dot_left(tri_ref[...], jnp.log(f) * LOG2E, 3)
    wk = jnp.log(kk) * LOG2E - cum

    pos = lax.broadcasted_iota(jnp.int32, (tt, 1), 0) % c
    wpad[0:SUBLANES, :] = jnp.zeros((SUBLANES, w), F32)
    vpad[0:SUBLANES, :] = jnp.zeros((SUBLANES, w), F32)
    wpad[SUBLANES:SUBLANES + tt, :] = wk
    vpad[SUBLANES:SUBLANES + tt, :] = v
    for b in range(SUBLANES):
        wb = wpad[SUBLANES - b:SUBLANES - b + tt, :]
        vb = vpad[SUBLANES - b:SUBLANES - b + tt, :]
        if b:
            wb = jnp.where(pos >= b, wb, -jnp.inf)
        wsh[b] = wb.reshape(nch, c, w)
        vsh[b] = vb.reshape(nch, c, w)

    qs3 = qs.reshape(nch, c, w)
    cum3 = cum.reshape(nch, c, w)
    acc = None
    for a in reversed(range(c // SUBLANES)):
        rows = c - SUBLANES * a
        qa = qs3[:, SUBLANES * a:, :].reshape(nch * rows, w)
        ca = cum3[:, SUBLANES * a:, :].reshape(nch * rows, w)
        acc_a = None
        for b in range(SUBLANES):
            wb = wsh[b, :, 0:rows, :].reshape(nch * rows, w)
            vb = vsh[b, :, 0:rows, :].reshape(nch * rows, w)
            dd = qa * jnp.exp2(ca + wb)
            term = _group_dot(dd.astype(BF16), bd) * vb
            acc_a = term if acc_a is None else acc_a + term
        acc_a = acc_a.reshape(nch, rows, w)
        if acc is not None:
            acc_a = acc_a + jnp.concatenate([jnp.zeros((nch, SUBLANES, w), F32), acc], axis=1)
        acc = acc_a
    acc = acc.reshape(tt, w)

    grp = st_scr.shape[1]
    bdmask = bd[0:grp, 0:grp] != 0
    outs = []
    for ci in range(nch):
        r0 = ci * c
        cum_c = cum[r0:r0 + c]
        last = cum_c[c - 1:c]
        qd = (qs[r0:r0 + c] * jnp.exp2(cum_c)).astype(BF16)
        kd = (kk[r0:r0 + c] * jnp.exp2(last - cum_c)).astype(BF16)
        vc = v[r0:r0 + c].astype(BF16)
        dec = jnp.exp2(last)
        o_parts = []
        for gi in range(w // grp):
            sl = slice(gi * grp, (gi + 1) * grp)
            st = st_scr[gi]
            o_parts.append(lax.dot_general(qd[:, sl], st.astype(BF16), (((1,), (1,)), ((), ())),
                                           preferred_element_type=F32))
            upd = lax.dot_general(vc[:, sl], kd[:, sl], (((0,), (0,)), ((), ())),
                                  preferred_element_type=F32)
            st_scr[gi] = st * dec[:, sl] + jnp.where(bdmask, upd, 0.0)
        outs.append(jnp.concatenate(o_parts, axis=1))
    o = acc + jnp.concatenate(outs, axis=0)

    o2 = o * o
    o2_hi = o2.astype(BF16)
    o2_lo = (o2 - o2_hi.astype(F32)).astype(BF16)
    ms = (_group_dot(o2_hi, bd) + _group_dot(o2_lo, bd)) * (1.0 / HG_DIM)
    gv = g_ref[...]
    o = o * lax.rsqrt(ms + EPS) * ng_ref[...] * (gv * jax.nn.sigmoid(gv))
    o_ref[...] = o.astype(o_ref.dtype)


def _split_dot_left(m, x, passes):
    acc = None
    r = x
    for p in range(passes):
        h = r.astype(BF16)
        term = jnp.dot(m, h, preferred_element_type=F32)
        acc = term if acc is None else acc + term
        if p + 1 < passes:
            r = r - h.astype(F32)
    return acc


def _hgrn(rest, lb_logits, norm_g, bsz, seq, layer, tri, bd):
    n = bsz * seq
    w = HG_HEADS * HG_DIM
    grp = bd.shape[0]
    tt = min(HGRN_TT, seq)
    c = min(HGRN_C, tt)
    nt = seq // tt
    kern = functools.partial(_hgrn_kernel, tt=tt, c=c, layer=layer)
    col = lambda j: pl.BlockSpec((tt, w), lambda b, i, j=j: (b * nt + i, j))
    const = lambda shape: pl.BlockSpec(shape, lambda b, i: (0, 0))
    return pl.pallas_call(
        kern,
        grid=(bsz, nt),
        in_specs=[col(0), col(1), col(2), col(3),
                  const(lb_logits.shape), const((1, w)), const((tt, tt)), const((grp, grp))],
        out_specs=pl.BlockSpec((tt, w), lambda b, i: (b * nt + i, 0)),
        out_shape=jax.ShapeDtypeStruct((n, w), BF16),
        scratch_shapes=([pltpu.VMEM((w // LANES, LANES, LANES), F32)]
                        + [pltpu.VMEM((SUBLANES + tt, w), F32)] * 2
                        + [pltpu.VMEM((SUBLANES, tt // c, c, w), F32)] * 2),
        compiler_params=_params(("parallel", "arbitrary")),
        name="hgrn",
    )(rest, rest, rest, rest, lb_logits, norm_g.reshape(1, w), tri, bd)


PAIR_A = (0, 0, 0, 1, 1, 3)
PAIR_B = (1, 2, 3, 3, 2, 2)
N_PAIRS = len(PAIR_A)
N_CLASSES = N_GROUPS * N_PAIRS
assert sorted(tuple(sorted(p)) for p in zip(PAIR_A, PAIR_B)) == [
    (a, b) for a in range(EXPERTS_PER_GROUP) for b in range(a + 1, EXPERTS_PER_GROUP)]
ROUTE_CLS, ROUTE_WA, ROUTE_WB = 0, 1, 2


def _pair_slots(pidx):
    a = b = jnp.zeros_like(pidx)
    for p in range(N_PAIRS):
        a = jnp.where(pidx == p, float(PAIR_A[p]), a)
        b = jnp.where(pidx == p, float(PAIR_B[p]), b)
    return a, b


def _pair_index(lo, hi):
    pidx = jnp.zeros_like(lo)
    for p in range(N_PAIRS):
        is_p = jnp.logical_and(lo == min(PAIR_A[p], PAIR_B[p]), hi == max(PAIR_A[p], PAIR_B[p]))
        pidx = jnp.where(is_p, float(p), pidx)
    return pidx


def _route(lg):
    lane = lax.broadcasted_iota(jnp.int32, lg.shape, 1)
    neg = jnp.float32(-jnp.inf)
    big = jnp.int32(LANES)
    gmask = jnp.logical_and(lane >= N_EXPERTS, lane < N_EXPERTS + N_GROUPS)
    gl = jnp.where(gmask, lg, neg)
    gmax = jnp.max(gl, axis=1, keepdims=True)
    gidx = jnp.min(jnp.where(gl == gmax, lane, big), axis=1, keepdims=True) - N_EXPERTS
    w_grp = 1.0 / jnp.sum(jnp.where(gmask, jnp.exp(gl - gmax), 0.0), axis=1, keepdims=True)
    in_grp = jnp.logical_and(lane < N_EXPERTS, lane // EXPERTS_PER_GROUP == gidx)
    l1 = jnp.where(in_grp, lg, neg)
    v1 = jnp.max(l1, axis=1, keepdims=True)
    i1 = jnp.min(jnp.where(l1 == v1, lane, big), axis=1, keepdims=True)
    l2 = jnp.where(jnp.logical_and(in_grp, lane != i1), lg, neg)
    v2 = jnp.max(l2, axis=1, keepdims=True)
    i2 = jnp.min(jnp.where(l2 == v2, lane, big), axis=1, keepdims=True)
    e2 = jnp.exp(v2 - v1)
    p1 = 1.0 / (1.0 + e2)
    p2 = e2 * p1
    loc1 = (i1 - gidx * EXPERTS_PER_GROUP).astype(F32)
    loc2 = (i2 - gidx * EXPERTS_PER_GROUP).astype(F32)
    pidx = _pair_index(jnp.minimum(loc1, loc2), jnp.maximum(loc1, loc2))
    cls = gidx.astype(F32) * N_PAIRS + pidx
    a_loc, _ = _pair_slots(pidx)
    first_is_a = loc1 == a_loc
    wa = jnp.where(first_is_a, p1, p2) * w_grp
    wb = jnp.where(first_is_a, p2, p1) * w_grp
    return jnp.where(lane == ROUTE_CLS, cls,
                     jnp.where(lane == ROUTE_WA, wa, jnp.where(lane == ROUTE_WB, wb, 0.0)))


def _store_token_tiles(ref, x):
    rows, d = x.shape
    assert d == SUBLANES * LANES
    for s in range(SUBLANES):
        ref[pl.ds(s, rows, stride=SUBLANES), :] = x[:, s * LANES:(s + 1) * LANES]


def _load_token_tiles(ref):
    rows = ref.shape[0] // SUBLANES
    return jnp.concatenate([ref[pl.ds(s, rows, stride=SUBLANES), :] for s in range(SUBLANES)],
                           axis=1)


def _merge_kernel(x_ref, ysb_ref, yhg_ref, gsb_ref, ghg_ref, wbs_ref, wbh_ref, wo_ref,
                  ln_ref, wr_ref, br_ref, x1_ref, t_ref, route_ref, wbs_scr, wbh_scr, wo_scr):
    @pl.when(pl.program_id(0) == 0)
    def _():
        wbs_scr[...] = wbs_ref[...].astype(BF16)
        wbh_scr[...] = wbh_ref[...].astype(BF16)
        wo_scr[...] = wo_ref[...].astype(BF16)

    a = jnp.dot(ysb_ref[...], wbs_scr[...], preferred_element_type=F32)
    b = jnp.dot(yhg_ref[...], wbh_scr[...], preferred_element_type=F32)
    merged = jax.nn.sigmoid(gsb_ref[...]) * a + jax.nn.sigmoid(ghg_ref[...]) * b
    x1 = x_ref[...] + jnp.dot(merged.astype(BF16), wo_scr[...], preferred_element_type=F32)
    x1_ref[...] = x1
    var = jnp.mean(x1 * x1, axis=-1, keepdims=True)
    t = x1 * lax.rsqrt(var + EPS) * ln_ref[...]
    _store_token_tiles(t_ref, t)
    t_hi = t.astype(BF16)
    t_lo = (t - t_hi.astype(F32)).astype(BF16)
    p_hi = jnp.dot(t_hi, wr_ref[...], preferred_element_type=F32)
    p_lo = jnp.dot(t_lo, wr_ref[...], preferred_element_type=F32)
    lg = (p_hi[:, :LANES] + p_hi[:, LANES:]) + (p_lo[:, :LANES] + p_lo[:, LANES:]) + br_ref[...]
    route_ref[...] = _route(lg)


def _merge(x2, y_sb, y_hg, rest, wbs, wbh, wo, ln_g, wr, br):
    n, d = x2.shape
    tm = min(MERGE_TM, n)
    w_sb, w_hg = y_sb.shape[1], y_hg.shape[1]
    gate_blk = (rest.shape[1] - 2 * d) // d
    row = lambda wdt, j=0: pl.BlockSpec((tm, wdt), lambda i, j=j: (i, j))
    const = lambda shape: pl.BlockSpec(shape, lambda i: (0, 0))
    return pl.pallas_call(
        _merge_kernel,
        grid=(n // tm,),
        in_specs=[row(d), row(w_sb), row(w_hg), row(d, gate_blk), row(d, gate_blk + 1),
                  const(wbs.shape), const(wbh.shape), const(wo.shape), const((1, d)),
                  const(wr.shape), const((1, LANES))],
        out_specs=[row(d), pl.BlockSpec((tm * SUBLANES, LANES), lambda i: (i, 0)), row(LANES)],
        out_shape=[jax.ShapeDtypeStruct((n, d), F32),
                   jax.ShapeDtypeStruct((n * SUBLANES, LANES), F32),
                   jax.ShapeDtypeStruct((n, LANES), F32)],
        scratch_shapes=[pltpu.VMEM(w.shape, BF16) for w in (wbs, wbh, wo)],
        compiler_params=_params(("arbitrary",)),
        name="merge",
    )(x2, y_sb, y_hg, rest, rest, wbs, wbh, wo, ln_g.reshape(1, d), wr, br)


TOKEN_ROWS_OUT = 2 * SUBLANES


def _plan_kernel(route_ref, ltri_ref, utri_ref, pos_ref, te_ref, meta_ref,
                 tot_scr, run_scr, *, tile_rows):
    ph = pl.program_id(0)
    i = pl.program_id(1)
    route = route_ref[...]
    tm = route.shape[0]
    lane = lax.broadcasted_iota(jnp.int32, (tm, LANES), 1)
    is_cls = lane.astype(F32) == route[:, ROUTE_CLS:ROUTE_CLS + 1]
    sel = jnp.where(is_cls, 1.0, 0.0)

    @pl.when(jnp.logical_and(ph == 0, i == 0))
    def _():
        tot_scr[...] = jnp.zeros_like(tot_scr)

    @pl.when(ph == 0)
    def _():
        tot_scr[...] += jnp.sum(sel, axis=0, keepdims=True)

    @pl.when(ph == 1)
    def _():
        tot = tot_scr[...]
        top = tot + (tile_rows - 1.0)
        tiles = jnp.floor(top * (1.0 / tile_rows))
        tiles = jnp.where(tiles * tile_rows > top, tiles - 1.0, tiles)
        tiles = jnp.where((tiles + 1.0) * tile_rows <= top, tiles + 1.0, tiles)
        first_tile = _split_dot(jnp.broadcast_to(tiles, (SUBLANES, LANES)), utri_ref[...], 3)[0:1]

        @pl.when(i == 0)
        def _():
            run_scr[...] = jnp.zeros_like(run_scr)
            lane1 = lax.broadcasted_iota(jnp.int32, (1, LANES), 1)
            is_c = lane1 < N_CLASSES
            end_tile = first_tile + tiles
            n_valid = jnp.sum(jnp.where(is_c, tiles, 0.0), axis=1, keepdims=True)
            rows = te_ref.shape[0]
            j = lax.broadcasted_iota(jnp.int32, (rows, LANES), 0).astype(F32)
            j = jnp.minimum(j, n_valid - 1.0)
            lane2 = lax.broadcasted_iota(jnp.int32, (rows, LANES), 1)
            done = jnp.where(jnp.logical_and(lane2 < N_CLASSES, end_tile <= j), 1.0, 0.0)
            tc = jnp.sum(done, axis=1, keepdims=True)
            tg = jnp.floor((tc + 0.5) * (1.0 / N_PAIRS))
            ta, tb = _pair_slots(tc - tg * N_PAIRS)
            te = jnp.where(lane2 == 0, tg * EXPERTS_PER_GROUP + ta, tg * EXPERTS_PER_GROUP + tb)
            te_ref[...] = te.astype(jnp.int32)
            last_tile = jnp.where(jnp.logical_and(is_c, tiles > 0), end_tile - 1.0, -1.0)
            meta = jnp.where(lane1 == N_CLASSES, n_valid, last_tile)
            meta_ref[...] = jnp.broadcast_to(meta, meta_ref.shape).astype(jnp.int32)

        rank = jnp.dot(ltri_ref[...], sel.astype(BF16), preferred_element_type=F32) + run_scr[...]
        dest = jnp.sum(jnp.where(is_cls, first_tile * tile_rows + rank, 0.0),
                       axis=1, keepdims=True)
        row = lax.broadcasted_iota(jnp.int32, (tm, LANES), 0)
        spread = jnp.where(lane == row % LANES, dest, 0.0)
        pos_ref[...] = jnp.sum(spread.reshape(tm // LANES, LANES, LANES), axis=1).astype(jnp.int32)
        run_scr[...] += jnp.sum(sel, axis=0, keepdims=True)


def _plan(route, n_tiles_max):
    n = route.shape[0]
    tm = min(PLAN_TM, n)
    r = np.arange(tm)
    ltri = jnp.asarray((r[None, :] < r[:, None]).astype(np.float32), dtype=BF16)
    e = np.arange(LANES)
    utri = jnp.asarray((e[:, None] < e[None, :]).astype(np.float32), dtype=BF16)
    te_rows = -(-n_tiles_max // SUBLANES) * SUBLANES
    kern = functools.partial(_plan_kernel, tile_rows=MOE_TM)
    return pl.pallas_call(
        kern,
        grid=(2, n // tm),
        in_specs=[pl.BlockSpec((tm, LANES), lambda p, i: (i, 0)),
                  pl.BlockSpec((tm, tm), lambda p, i: (0, 0)),
                  pl.BlockSpec((LANES, LANES), lambda p, i: (0, 0))],
        out_specs=[pl.BlockSpec((tm // LANES, LANES), lambda p, i: (i * p, 0)),
                   pl.BlockSpec((te_rows, LANES), lambda p, i: (0, 0)),
                   pl.BlockSpec((SUBLANES, LANES), lambda p, i: (0, 0))],
        out_shape=[jax.ShapeDtypeStruct((n // LANES, LANES), jnp.int32),
                   jax.ShapeDtypeStruct((te_rows, LANES), jnp.int32),
                   jax.ShapeDtypeStruct((SUBLANES, LANES), jnp.int32)],
        scratch_shapes=[pltpu.VMEM((1, LANES), F32), pltpu.VMEM((1, LANES), F32)],
        compiler_params=_params(("arbitrary", "arbitrary")),
        name="plan",
    )(route, ltri, utri)


DMA_UNROLL = 8


def _token_copy(src_ref, r, dst_ref, p, sem, rows):
    return pltpu.make_async_copy(src_ref.at[pl.ds(pl.multiple_of(r * rows, rows), rows)],
                                 dst_ref.at[pl.ds(pl.multiple_of(p * rows, rows), rows)], sem)


def _dispatch_kernel(pos_ref, meta_ref, t_ref, xs_ref, zero_scr, sem, tail_sem, *, tm, tile_rows):
    i = pl.program_id(0)
    tile_rows = tile_rows * SUBLANES
    n_tiles = xs_ref.shape[0] // tile_rows
    n_valid = meta_ref[N_CLASSES]

    def clear(tile, s):
        return pltpu.make_async_copy(zero_scr, xs_ref.at[pl.ds(tile * tile_rows, tile_rows)], s)

    def for_each(clears, fn):
        for tile, cond in clears:
            @pl.when(cond)
            def _():
                fn(tile)

    last_tiles = [(meta_ref[c], meta_ref[c] >= 0) for c in range(N_CLASSES)]
    past_tiles = [(n_tiles - 1 - k, n_tiles - 1 - k >= n_valid) for k in range(N_CLASSES)]

    @pl.when(i == 0)
    def _():
        zero_scr[...] = jnp.zeros_like(zero_scr)
        for_each(last_tiles, lambda tile: clear(tile, sem).start())
        for_each(past_tiles, lambda tile: clear(tile, tail_sem).start())
        for_each(last_tiles, lambda tile: clear(tile, sem).wait())

    base = i * tm

    def start(r2, carry):
        for k in range(2):
            r = 2 * r2 + k
            _token_copy(t_ref, r, xs_ref, pos_ref[base + r], sem, SUBLANES).start(priority=k)
        return carry

    lax.fori_loop(0, tm // 2, start, 0, unroll=DMA_UNROLL)
    pltpu.make_async_copy(t_ref, xs_ref.at[pl.ds(0, tm * SUBLANES)], sem).wait()

    @pl.when(i == pl.num_programs(0) - 1)
    def _():
        for_each(past_tiles, lambda tile: clear(tile, tail_sem).wait())


def _dispatch(pos, meta, t, n_rows):
    n = t.shape[0] // SUBLANES
    tm = min(DISPATCH_TM, n)
    kern = functools.partial(_dispatch_kernel, tm=tm, tile_rows=MOE_TM)
    grid_spec = pltpu.PrefetchScalarGridSpec(
        num_scalar_prefetch=2,
        grid=(n // tm,),
        in_specs=[pl.BlockSpec((tm * SUBLANES, LANES), lambda i, *_: (i, 0))],
        out_specs=pl.BlockSpec(memory_space=pl.ANY),
        scratch_shapes=[pltpu.VMEM((MOE_TM * SUBLANES, LANES), F32),
                        pltpu.SemaphoreType.DMA(()), pltpu.SemaphoreType.DMA(())],
    )
    return pl.pallas_call(
        kern,
        grid_spec=grid_spec,
        out_shape=jax.ShapeDtypeStruct((n_rows * SUBLANES, LANES), F32),
        compiler_params=_params(("arbitrary",)),
        name="dispatch",
    )(pos, meta, t)


def _expert_kernel(tea_ref, teb_ref, nv_ref, xs_ref, wga, wua, wda, wgb, wub, wdb, ys_ref):
    in_use = pl.program_id(0) < nv_ref[0]
    rows = xs_ref.shape[0] // SUBLANES

    @pl.when(in_use)
    def _():
        x = _load_token_tiles(xs_ref).astype(BF16)
        for slot, (wg, wu, wd) in enumerate(((wga, wua, wda), (wgb, wub, wdb))):
            hg = jnp.dot(x, wg[...].astype(BF16), preferred_element_type=F32)
            hu = jnp.dot(x, wu[...].astype(BF16), preferred_element_type=F32)
            a = (hg * jax.nn.sigmoid(hg) * hu).astype(BF16)
            y = jnp.dot(a, wd[...].astype(BF16), preferred_element_type=F32)
            for s in range(SUBLANES):
                ys_ref[pl.ds(slot * SUBLANES + s, rows, stride=TOKEN_ROWS_OUT), :] = (
                    y[:, s * LANES:(s + 1) * LANES])

    @pl.when(jnp.logical_not(in_use))
    def _():
        ys_ref[...] = jnp.zeros_like(ys_ref)


def _experts(tea, teb, nv, xs, wg, wu, wd):
    _, d, de = wg.shape
    assert d == SUBLANES * LANES
    n_tiles = xs.shape[0] // (MOE_TM * SUBLANES)
    wa = lambda shape: pl.BlockSpec(shape, lambda j, tea, teb, nv: (tea[j], 0, 0))
    wb = lambda shape: pl.BlockSpec(shape, lambda j, tea, teb, nv: (teb[j], 0, 0))
    grid_spec = pltpu.PrefetchScalarGridSpec(
        num_scalar_prefetch=3,
        grid=(n_tiles,),
        in_specs=[pl.BlockSpec((MOE_TM * SUBLANES, LANES),
                               lambda j, tea, teb, nv: (jnp.minimum(j, nv[0] - 1), 0)),
                  wa((None, d, de)), wa((None, d, de)), wa((None, de, d)),
                  wb((None, d, de)), wb((None, d, de)), wb((None, de, d))],
        out_specs=pl.BlockSpec((MOE_TM * TOKEN_ROWS_OUT, LANES), lambda j, *_: (j, 0)),
    )
    return pl.pallas_call(
        _expert_kernel,
        grid_spec=grid_spec,
        out_shape=jax.ShapeDtypeStruct((n_tiles * MOE_TM * TOKEN_ROWS_OUT, LANES), F32),
        compiler_params=_params(("arbitrary",)),
        name="experts",
    )(tea, teb, nv, xs, wg, wu, wd, wg, wu, wd)


def _combine_kernel(pos_ref, ys_ref, x1_ref, route_ref, fg_ref, o_ref, g, sem, *, tm):
    i = pl.program_id(0)
    slot = i % 2

    def gather(step, buf):
        base = step * tm

        def body(r2, carry):
            for k in range(2):
                r = 2 * r2 + k
                _token_copy(ys_ref, pos_ref[base + r], g.at[buf], r, sem.at[buf],
                            TOKEN_ROWS_OUT).start(priority=k)
            return carry

        lax.fori_loop(0, tm // 2, body, 0, unroll=DMA_UNROLL)

    @pl.when(i == 0)
    def _():
        gather(0, 0)

    @pl.when(i + 1 < pl.num_programs(0))
    def _():
        gather(i + 1, 1 - slot)

    pltpu.make_async_copy(ys_ref.at[pl.ds(0, tm * TOKEN_ROWS_OUT)], g.at[slot],
                          sem.at[slot]).wait()
    gs = g.at[slot]
    ya, yb = (jnp.concatenate([gs[pl.ds(off + s, tm, stride=TOKEN_ROWS_OUT), :]
                               for s in range(SUBLANES)], axis=1) for off in (0, SUBLANES))
    route = route_ref[...]
    x2 = x1_ref[...] + (route[:, ROUTE_WA:ROUTE_WA + 1] * ya + route[:, ROUTE_WB:ROUTE_WB + 1] * yb)
    var = jnp.mean(x2 * x2, axis=-1, keepdims=True)
    o_ref[...] = x2 * lax.rsqrt(var + EPS) * fg_ref[...]


def _combine(pos, ys, x1, route, final_g):
    n, d = x1.shape
    tm = min(COMBINE_TM, n)
    kern = functools.partial(_combine_kernel, tm=tm)
    grid_spec = pltpu.PrefetchScalarGridSpec(
        num_scalar_prefetch=1,
        grid=(n // tm,),
        in_specs=[pl.BlockSpec(memory_space=pl.ANY),
                  pl.BlockSpec((tm, d), lambda i, *_: (i, 0)),
                  pl.BlockSpec((tm, LANES), lambda i, *_: (i, 0)),
                  pl.BlockSpec((1, d), lambda i, *_: (0, 0))],
        out_specs=pl.BlockSpec((tm, d), lambda i, *_: (i, 0)),
        scratch_shapes=[pltpu.VMEM((2, tm * TOKEN_ROWS_OUT, LANES), F32),
                        pltpu.SemaphoreType.DMA((2,))],
    )
    return pl.pallas_call(
        kern,
        grid_spec=grid_spec,
        out_shape=jax.ShapeDtypeStruct((n, d), F32),
        compiler_params=_params(("arbitrary",)),
        name="combine",
    )(pos, ys, x1, route, final_g.reshape(1, d))


def _moe_sparse(t, route, wg, wu, wd, x1, final_g):
    n = x1.shape[0]
    n_tiles_max = (n + N_CLASSES * (MOE_TM - 1)) // MOE_TM
    pos, te, meta = _plan(route, n_tiles_max)
    pos = pos.reshape(n)
    xs = _dispatch(pos, meta[0, :N_CLASSES + 1], t, n_tiles_max * MOE_TM)
    ys = _experts(te[:n_tiles_max, 0], te[:n_tiles_max, 1], meta[0, N_CLASSES:N_CLASSES + 1],
                  xs, wg, wu, wd)
    return _combine(pos, ys, x1, route, final_g)


def _suffix_ones(t):
    j = np.arange(t)[:, None]
    s = np.arange(t)[None, :]
    return jnp.asarray((j > s).astype(np.float32), dtype=BF16)


def _causal_bias(t):
    row = np.arange(t)[:, None]
    col = np.arange(t)[None, :]
    diag = np.where(col < row, 0.0, MASK_BIAS).astype(np.float32)
    return jnp.asarray(np.stack([np.zeros_like(diag), diag]))


def _chunk_prefix_ones(tt, c):
    t = np.arange(tt)[:, None]
    j = np.arange(tt)[None, :]
    return jnp.asarray(((j <= t) & (t // c == j // c)).astype(np.float32), dtype=BF16)


def _block_diag_ones(w, blk):
    a = np.arange(w)
    return jnp.asarray((a[:, None] // blk == a[None, :] // blk).astype(np.float32), dtype=BF16)


def kernel(x, ln1_g, w_in, w_branch_sb, w_branch_hg, hg_norm_g, hg_lb_logits, w_out, ln2_g,
           w_router_group, b_router_group, w_router_expert, b_router_expert,
           w_exp_gate, w_exp_up, w_exp_down, final_g):
    bsz, seq, d = x.shape
    depth = w_in.shape[0]
    n = bsz * seq
    sb_width = SB_HEADS * SB_HEAD_DIM
    hg_width = HG_HEADS * HG_DIM

    tri_attn = _suffix_ones(min(ATTN_T, seq))
    bias_attn = _causal_bias(min(ATTN_T, seq))
    tt = min(HGRN_TT, seq)
    tri_hg = _chunk_prefix_ones(tt, min(HGRN_C, tt))
    bd = _block_diag_ones(min(MXU_DIM, hg_width), HG_DIM)

    x2 = x.reshape(n, d)
    for l in range(depth):
        qkv, rest = _inproj(x2, ln1_g[l], w_in[l], sb_width)
        y_sb = _attn(qkv, bsz, seq, tri_attn, bias_attn)
        y_hg = _hgrn(rest, hg_lb_logits, hg_norm_g[l], bsz, seq, l, tri_hg, bd)

        pad = LANES - N_EXPERTS - N_GROUPS
        wr = jnp.concatenate([w_router_expert[l], w_router_group[l],
                              jnp.zeros((d, pad), F32)], axis=1)
        wr_hi = wr.astype(BF16)
        wr_lo = (wr - wr_hi.astype(F32)).astype(BF16)
        wr_split = jnp.concatenate([wr_hi, wr_lo], axis=1)
        br = jnp.concatenate([b_router_expert[l], b_router_group[l],
                              jnp.zeros((pad,), F32)]).reshape(1, LANES)

        last = l == depth - 1
        x1, t, route = _merge(x2, y_sb, y_hg, rest, w_branch_sb[l], w_branch_hg[l], w_out[l],
                              ln2_g[l], wr_split, br)
        assert last, "final rmsnorm is fused into the last layer's combine kernel"
        x2 = _moe_sparse(t, route, w_exp_gate[l], w_exp_up[l], w_exp_down[l], x1, final_g)
    return x2.reshape(bsz, seq, d)
```

```python
import functools

import jax
import jax.numpy as jnp
import numpy as np
from jax import lax
from jax.experimental import pallas as pl
from jax.experimental.pallas import tpu as pltpu

F32 = jnp.float32
BF16 = jnp.bfloat16

EPS = 1e-6
SB_HEADS = 8
SB_HEAD_DIM = 64
HG_HEADS = 8
HG_DIM = 64
N_GROUPS = 4
EXPERTS_PER_GROUP = 4
N_EXPERTS = N_GROUPS * EXPERTS_PER_GROUP

LANES = 128
SUBLANES = 8
MXU_DIM = 256
LOG2E = 1.4426950408889634
VMEM_LIMIT = 56 * 1024 * 1024

INPROJ_TM = 2048
INPROJ_TN = 512
ATTN_T = MXU_DIM
HGRN_TT = 256
HGRN_C = 32
MERGE_TM = 512
MOE_TM = 384
PLAN_TM = 1024
DISPATCH_TM = 1024
COMBINE_TM = 512


def _params(sem):
    return pltpu.CompilerParams(dimension_semantics=sem, vmem_limit_bytes=VMEM_LIMIT)


def _split_dot(x, m, passes):
    acc = None
    r = x
    for p in range(passes):
        h = r.astype(BF16)
        term = jnp.dot(h, m, preferred_element_type=F32)
        acc = term if acc is None else acc + term
        if p + 1 < passes:
            r = r - h.astype(F32)
    return acc


def _inproj_kernel(x_ref, g_ref, w_ref, qkv_ref, rest_ref, h_scr, w_scr, *, q_scale, nq):
    i = pl.program_id(0)
    j = pl.program_id(1)

    @pl.when(j == 0)
    def _():
        x = x_ref[...]
        var = jnp.mean(x * x, axis=-1, keepdims=True)
        h_scr[...] = (x * lax.rsqrt(var + EPS) * g_ref[...]).astype(BF16)

    @pl.when(i == 0)
    def _():
        w_scr[j] = w_ref[...].astype(BF16)

    acc = jnp.dot(h_scr[...], w_scr[j], preferred_element_type=F32)
    qkv = (acc * jnp.where(j == 0, q_scale, 1.0)).astype(BF16)
    qkv_ref[...] = jnp.where(j < nq, qkv, qkv_ref[...])
    rest_ref[...] = acc


def _inproj(x2, ln_g, w_in, sb_width):
    n, d = x2.shape
    cols = w_in.shape[1]
    tm, tn = min(INPROJ_TM, n), INPROJ_TN
    assert sb_width == tn, "q block must be exactly one column tile"
    nq = 3 * sb_width // tn
    nj = cols // tn
    kern = functools.partial(_inproj_kernel, q_scale=SB_HEAD_DIM ** -0.5, nq=nq)
    return pl.pallas_call(
        kern,
        grid=(n // tm, nj),
        in_specs=[
            pl.BlockSpec((tm, d), lambda i, j: (i, 0)),
            pl.BlockSpec((1, d), lambda i, j: (0, 0)),
            pl.BlockSpec((d, tn), lambda i, j: (0, jnp.where(i == 0, j, nj - 1))),
        ],
        out_specs=[
            pl.BlockSpec((tm, tn), lambda i, j: (i, jnp.minimum(j, nq - 1))),
            pl.BlockSpec((tm, tn), lambda i, j: (i, jnp.maximum(j - nq, 0))),
        ],
        out_shape=[
            jax.ShapeDtypeStruct((n, nq * tn), BF16),
            jax.ShapeDtypeStruct((n, cols - nq * tn), F32),
        ],
        scratch_shapes=[pltpu.VMEM((tm, d), BF16), pltpu.VMEM((nj, d, tn), BF16)],
        compiler_params=_params(("arbitrary", "arbitrary")),
        name="inproj",
    )(x2, ln_g.reshape(1, d), w_in)


ATTN_STAGES = 3
ATTN_STREAMS = 2
MASK_BIAS = -1e30
ATTN_SKIP = 111.0


def _attn_kernel(q_ref, k_ref, v_ref, tri_ref, bias_ref, o_ref,
                 z0, z1, z2, i0, i1, i2, acc_ref, c_ref, *, t, nq):
    zbuf = (z0, z1, z2)
    ibuf = (i0, i1, i2)
    for r in zbuf + ibuf + (acc_ref, c_ref):
        r[...] = jnp.zeros_like(r)

    lane = lax.broadcasted_iota(jnp.int32, (t, LANES), 1)
    head0 = lane < SB_HEAD_DIM
    nt = (((1,), (1,)), ((), ()))

    def stage_a(qi, kj, slot):
        q = q_ref[pl.ds(pl.multiple_of(qi * t, t), t), :]
        zero = jnp.zeros_like(q)
        q2 = jnp.concatenate([jnp.where(head0, q, zero), jnp.where(head0, zero, q)], axis=0)
        k = k_ref[pl.ds(pl.multiple_of(kj * t, t), t), :]
        z = lax.dot_general(q2, k, nt, preferred_element_type=F32)
        bias = bias_ref[(qi == kj).astype(jnp.int32)]
        zbuf[slot][...] = z + jnp.concatenate([bias, bias], axis=0)

    def stage_b(slot):
        z = zbuf[slot][...]
        p = jnp.maximum(z, 0.0) + jnp.log(1.0 + jnp.exp2(jnp.abs(z) * (-LOG2E)))
        incl = jnp.dot(p.astype(BF16), tri_ref[...], preferred_element_type=F32) + p
        ibuf[slot][...] = incl
        return incl[:, 0:1]

    def stage_c(qi, kj, slot, stream):
        first = qi == kj
        incl = ibuf[slot][...]
        c = jnp.where(first, 0.0, c_ref[stream])
        a = jnp.exp2((zbuf[slot][...] - incl - c) * LOG2E)
        v = v_ref[pl.ds(pl.multiple_of(kj * t, t), t), :]
        pv = jnp.dot(a.astype(BF16), v, preferred_element_type=F32)
        acc = jnp.where(first, pv, acc_ref[stream] + pv)
        acc_ref[stream] = acc
        c_ref[stream] = c + incl[:, 0:1]
        o_ref[pl.ds(pl.multiple_of(qi * t, t), t), :] = (
            jnp.where(head0, acc[0:t], acc[t:2 * t]).astype(o_ref.dtype))

    def block(carry):
        qs, ks, qn, kn, skip_q, drained = carry
        qs, ks, qn, kn, skip_q = list(qs), list(ks), list(qn), list(kn), list(skip_q)
        all_done = qn[0] >= nq
        for p in range(1, ATTN_STREAMS):
            all_done = jnp.logical_and(all_done, qn[p] >= nq)
        drained = drained + all_done.astype(jnp.int32)
        for r in range(ATTN_STAGES * ATTN_STREAMS):
            sa, sc, sb = r % ATTN_STAGES, (r + 1) % ATTN_STAGES, (r + 2) % ATTN_STAGES
            pa, pb = r % ATTN_STREAMS, (r - 1) % ATTN_STREAMS
            stage_c(qs[sc], ks[sc], sc, pa)
            total_b = stage_b(sb)
            carry_b = jnp.where(qs[sb] == ks[sb], 0.0, c_ref[pb]) + total_b
            skip = skip_q[pa] == qn[pa]
            q_cur = jnp.where(skip, qn[pa] + ATTN_STREAMS, qn[pa])
            k_cur = jnp.where(skip, qn[pa] + ATTN_STREAMS, kn[pa])
            drain = q_cur >= nq
            qa = jnp.where(drain, 0, q_cur)
            ka = jnp.where(drain, 0, k_cur)
            stage_a(qa, ka, sa)
            last = k_cur == 0
            qn[pa] = jnp.where(jnp.logical_and(last, jnp.logical_not(drain)),
                               q_cur + ATTN_STREAMS, q_cur)
            kn[pa] = jnp.where(drain, k_cur, jnp.where(last, q_cur + ATTN_STREAMS, k_cur - 1))
            skip_q[pb] = jnp.where(jnp.min(carry_b) >= ATTN_SKIP, qs[sb], skip_q[pb])
            qs[sa], ks[sa] = qa, ka
        return tuple(qs), tuple(ks), tuple(qn), tuple(kn), tuple(skip_q), drained

    zero = jnp.int32(0)
    first_q = tuple(jnp.int32(p) for p in range(ATTN_STREAMS))
    init = ((zero,) * ATTN_STAGES, (zero,) * ATTN_STAGES, first_q, first_q,
            (jnp.int32(-1),) * ATTN_STREAMS, zero)
    lax.while_loop(lambda carry: carry[5] < 1, block, init)


def _attn(qkv, bsz, seq, tri, bias):
    n = bsz * seq
    t = min(ATTN_T, seq)
    pairs = SB_HEADS * SB_HEAD_DIM // LANES
    kern = functools.partial(_attn_kernel, t=t, nq=seq // t)
    return pl.pallas_call(
        kern,
        grid=(bsz, pairs),
        in_specs=[
            pl.BlockSpec((seq, LANES), lambda b, p: (b, p)),
            pl.BlockSpec((seq, LANES), lambda b, p: (b, pairs + p)),
            pl.BlockSpec((seq, LANES), lambda b, p: (b, 2 * pairs + p)),
            pl.BlockSpec((t, t), lambda b, p: (0, 0)),
            pl.BlockSpec((2, t, t), lambda b, p: (0, 0, 0)),
        ],
        out_specs=pl.BlockSpec((seq, LANES), lambda b, p: (b, p)),
        out_shape=jax.ShapeDtypeStruct((n, pairs * LANES), BF16),
        scratch_shapes=([pltpu.VMEM((2 * t, t), F32)] * (2 * ATTN_STAGES)
                        + [pltpu.VMEM((ATTN_STREAMS, 2 * t, LANES), F32),
                           pltpu.VMEM((ATTN_STREAMS, 2 * t, 1), F32)]),
        compiler_params=_params(("parallel", "parallel")),
        name="attn",
    )(qkv, qkv, qkv, tri, bias)


def _group_dot(x, bd):
    g = bd.shape[0]
    parts = [jnp.dot(x[:, i:i + g], bd, preferred_element_type=F32)
             for i in range(0, x.shape[1], g)]
    return jnp.concatenate(parts, axis=1)


def _hgrn_kernel(q_ref, f_ref, i_ref, g_ref, lbl_ref, ng_ref, tri_ref, bd_ref, o_ref,
                 st_scr, wpad, vpad, wsh, vsh, *, tt, c, layer):
    ti = pl.program_id(1)
    w = q_ref.shape[1]
    nch = tt // c

    @pl.when(ti == 0)
    def _():
        st_scr[...] = jnp.zeros_like(st_scr)

    lg = lbl_ref[...]
    e = jnp.exp(lg - jnp.max(lg, axis=0, keepdims=True))
    lb = jnp.sum(e[0:layer + 1], axis=0, keepdims=True) / jnp.sum(e, axis=0, keepdims=True)

    f = lb + (1.0 - lb) * jax.nn.sigmoid(f_ref[...])
    kk = 1.0 - f
    qv = q_ref[...]
    qs = qv * jax.nn.sigmoid(qv)
    v = i_ref[...]
    bd = bd_ref[...]
    cum = _split_dot_left(tri_ref[...], jnp.log(f) * LOG2E, 3)
    wk = jnp.log(kk) * LOG2E - cum

    pos = lax.broadcasted_iota(jnp.int32, (tt, 1), 0) % c
    wpad[0:SUBLANES, :] = jnp.zeros((SUBLANES, w), F32)
    vpad[0:SUBLANES, :] = jnp.zeros((SUBLANES, w), F32)
    wpad[SUBLANES:SUBLANES + tt, :] = wk
    vpad[SUBLANES:SUBLANES + tt, :] = v
    for b in range(SUBLANES):
        wb = wpad[SUBLANES - b:SUBLANES - b + tt, :]
        vb = vpad[SUBLANES - b:SUBLANES - b + tt, :]
        if b:
            wb = jnp.where(pos >= b, wb, -jnp.inf)
        wsh[b] = wb.reshape(nch, c, w)
        vsh[b] = vb.reshape(nch, c, w)

    qs3 = qs.reshape(nch, c, w)
    cum3 = cum.reshape(nch, c, w)
    acc = None
    for a in reversed(range(c // SUBLANES)):
        rows = c - SUBLANES * a
        qa = qs3[:, SUBLANES * a:, :].reshape(nch * rows, w)
        ca = cum3[:, SUBLANES * a:, :].reshape(nch * rows, w)
        acc_a = None
        for b in range(SUBLANES):
            wb = wsh[b, :, 0:rows, :].reshape(nch * rows, w)
            vb = vsh[b, :, 0:rows, :].reshape(nch * rows, w)
            dd = qa * jnp.exp2(ca + wb)
            term = _group_dot(dd.astype(BF16), bd) * vb
            acc_a = term if acc_a is None else acc_a + term
        acc_a = acc_a.reshape(nch, rows, w)
        if acc is not None:
            acc_a = acc_a + jnp.concatenate([jnp.zeros((nch, SUBLANES, w), F32), acc], axis=1)
        acc = acc_a
    acc = acc.reshape(tt, w)

    grp = st_scr.shape[1]
    bdmask = bd[0:grp, 0:grp] != 0
    outs = []
    for ci in range(nch):
        r0 = ci * c
        cum_c = cum[r0:r0 + c]
        last = cum_c[c - 1:c]
        qd = (qs[r0:r0 + c] * jnp.exp2(cum_c)).astype(BF16)
        kd = (kk[r0:r0 + c] * jnp.exp2(last - cum_c)).astype(BF16)
        vc = v[r0:r0 + c].astype(BF16)
        dec = jnp.exp2(last)
        o_parts = []
        for gi in range(w // grp):
            sl = slice(gi * grp, (gi + 1) * grp)
            st = st_scr[gi]
            o_parts.append(lax.dot_general(qd[:, sl], st.astype(BF16), (((1,), (1,)), ((), ())),
                                           preferred_element_type=F32))
            upd = lax.dot_general(vc[:, sl], kd[:, sl], (((0,), (0,)), ((), ())),
                                  preferred_element_type=F32)
            st_scr[gi] = st * dec[:, sl] + jnp.where(bdmask, upd, 0.0)
        outs.append(jnp.concatenate(o_parts, axis=1))
    o = acc + jnp.concatenate(outs, axis=0)

    o2 = o * o
    o2_hi = o2.astype(BF16)
    o2_lo = (o2 - o2_hi.astype(F32)).astype(BF16)
    ms = (_group_dot(o2_hi, bd) + _group_dot(o2_lo, bd)) * (1.0 / HG_DIM)
    gv = g_ref[...]
    o = o * lax.rsqrt(ms + EPS) * ng_ref[...] * (gv * jax.nn.sigmoid(gv))
    o_ref[...] = o.astype(o_ref.dtype)


def _split_dot_left(m, x, passes):
    acc = None
    r = x
    for p in range(passes):
        h = r.astype(BF16)
        term = jnp.dot(m, h, preferred_element_type=F32)
        acc = term if acc is None else acc + term
        if p + 1 < passes:
            r = r - h.astype(F32)
    return acc


def _hgrn(rest, lb_logits, norm_g, bsz, seq, layer, tri, bd):
    n = bsz * seq
    w = HG_HEADS * HG_DIM
    grp = bd.shape[0]
    tt = min(HGRN_TT, seq)
    c = min(HGRN_C, tt)
    nt = seq // tt
    kern = functools.partial(_hgrn_kernel, tt=tt, c=c, layer=layer)
    col = lambda j: pl.BlockSpec((tt, w), lambda b, i, j=j: (b * nt + i, j))
    const = lambda shape: pl.BlockSpec(shape, lambda b, i: (0, 0))
    return pl.pallas_call(
        kern,
        grid=(bsz, nt),
        in_specs=[col(0), col(1), col(2), col(3),
                  const(lb_logits.shape), const((1, w)), const((tt, tt)), const((grp, grp))],
        out_specs=pl.BlockSpec((tt, w), lambda b, i: (b * nt + i, 0)),
        out_shape=jax.ShapeDtypeStruct((n, w), BF16),
        scratch_shapes=([pltpu.VMEM((w // LANES, LANES, LANES), F32)]
                        + [pltpu.VMEM((SUBLANES + tt, w), F32)] * 2
                        + [pltpu.VMEM((SUBLANES, tt // c, c, w), F32)] * 2),
        compiler_params=_params(("parallel", "arbitrary")),
        name="hgrn",
    )(rest, rest, rest, rest, lb_logits, norm_g.reshape(1, w), tri, bd)


PAIR_A = (0, 0, 0, 1, 1, 3)
PAIR_B = (1, 2, 3, 3, 2, 2)
N_PAIRS = len(PAIR_A)
N_CLASSES = N_GROUPS * N_PAIRS
assert sorted(tuple(sorted(p)) for p in zip(PAIR_A, PAIR_B)) == [
    (a, b) for a in range(EXPERTS_PER_GROUP) for b in range(a + 1, EXPERTS_PER_GROUP)]
ROUTE_CLS, ROUTE_WA, ROUTE_WB = 0, 1, 2


def _pair_slots(pidx):
    a = b = jnp.zeros_like(pidx)
    for p in range(N_PAIRS):
        a = jnp.where(pidx == p, float(PAIR_A[p]), a)
        b = jnp.where(pidx == p, float(PAIR_B[p]), b)
    return a, b


def _pair_index(lo, hi):
    pidx = jnp.zeros_like(lo)
    for p in range(N_PAIRS):
        is_p = jnp.logical_and(lo == min(PAIR_A[p], PAIR_B[p]), hi == max(PAIR_A[p], PAIR_B[p]))
        pidx = jnp.where(is_p, float(p), pidx)
    return pidx


def _route(lg):
    lane = lax.broadcasted_iota(jnp.int32, lg.shape, 1)
    neg = jnp.float32(-jnp.inf)
    big = jnp.int32(LANES)
    gmask = jnp.logical_and(lane >= N_EXPERTS, lane < N_EXPERTS + N_GROUPS)
    gl = jnp.where(gmask, lg, neg)
    gmax = jnp.max(gl, axis=1, keepdims=True)
    gidx = jnp.min(jnp.where(gl == gmax, lane, big), axis=1, keepdims=True) - N_EXPERTS
    w_grp = 1.0 / jnp.sum(jnp.where(gmask, jnp.exp(gl - gmax), 0.0), axis=1, keepdims=True)
    in_grp = jnp.logical_and(lane < N_EXPERTS, lane // EXPERTS_PER_GROUP == gidx)
    l1 = jnp.where(in_grp, lg, neg)
    v1 = jnp.max(l1, axis=1, keepdims=True)
    i1 = jnp.min(jnp.where(l1 == v1, lane, big), axis=1, keepdims=True)
    l2 = jnp.where(jnp.logical_and(in_grp, lane != i1), lg, neg)
    v2 = jnp.max(l2, axis=1, keepdims=True)
    i2 = jnp.min(jnp.where(l2 == v2, lane, big), axis=1, keepdims=True)
    e2 = jnp.exp(v2 - v1)
    p1 = 1.0 / (1.0 + e2)
    p2 = e2 * p1
    loc1 = (i1 - gidx * EXPERTS_PER_GROUP).astype(F32)
    loc2 = (i2 - gidx * EXPERTS_PER_GROUP).astype(F32)
    pidx = _pair_index(jnp.minimum(loc1, loc2), jnp.maximum(loc1, loc2))
    cls = gidx.astype(F32) * N_PAIRS + pidx
    a_loc, _ = _pair_slots(pidx)
    first_is_a = loc1 == a_loc
    wa = jnp.where(first_is_a, p1, p2) * w_grp
    wb = jnp.where(first_is_a, p2, p1) * w_grp
    return jnp.where(lane == ROUTE_CLS, cls,
                     jnp.where(lane == ROUTE_WA, wa, jnp.where(lane == ROUTE_WB, wb, 0.0)))


def _store_token_tiles(ref, x):
    rows, d = x.shape
    assert d == SUBLANES * LANES
    for s in range(SUBLANES):
        ref[pl.ds(s, rows, stride=SUBLANES), :] = x[:, s * LANES:(s + 1) * LANES]


def _load_token_tiles(ref):
    rows = ref.shape[0] // SUBLANES
    return jnp.concatenate([ref[pl.ds(s, rows, stride=SUBLANES), :] for s in range(SUBLANES)],
                           axis=1)


def _merge_kernel(x_ref, ysb_ref, yhg_ref, gsb_ref, ghg_ref, wbs_ref, wbh_ref, wo_ref,
                  ln_ref, wr_ref, br_ref, x1_ref, t_ref, route_ref, wbs_scr, wbh_scr, wo_scr):
    @pl.when(pl.program_id(0) == 0)
    def _():
        wbs_scr[...] = wbs_ref[...].astype(BF16)
        wbh_scr[...] = wbh_ref[...].astype(BF16)
        wo_scr[...] = wo_ref[...].astype(BF16)

    a = jnp.dot(ysb_ref[...], wbs_scr[...], preferred_element_type=F32)
    b = jnp.dot(yhg_ref[...], wbh_scr[...], preferred_element_type=F32)
    merged = jax.nn.sigmoid(gsb_ref[...]) * a + jax.nn.sigmoid(ghg_ref[...]) * b
    x1 = x_ref[...] + jnp.dot(merged.astype(BF16), wo_scr[...], preferred_element_type=F32)
    x1_ref[...] = x1
    var = jnp.mean(x1 * x1, axis=-1, keepdims=True)
    t = x1 * lax.rsqrt(var + EPS) * ln_ref[...]
    _store_token_tiles(t_ref, t)
    t_hi = t.astype(BF16)
    t_lo = (t - t_hi.astype(F32)).astype(BF16)
    p_hi = jnp.dot(t_hi, wr_ref[...], preferred_element_type=F32)
    p_lo = jnp.dot(t_lo, wr_ref[...], preferred_element_type=F32)
    lg = (p_hi[:, :LANES] + p_hi[:, LANES:]) + (p_lo[:, :LANES] + p_lo[:, LANES:]) + br_ref[...]
    route_ref[...] = _route(lg)


def _merge(x2, y_sb, y_hg, rest, wbs, wbh, wo, ln_g, wr, br):
    n, d = x2.shape
    tm = min(MERGE_TM, n)
    w_sb, w_hg = y_sb.shape[1], y_hg.shape[1]
    gate_blk = (rest.shape[1] - 2 * d) // d
    row = lambda wdt, j=0: pl.BlockSpec((tm, wdt), lambda i, j=j: (i, j))
    const = lambda shape: pl.BlockSpec(shape, lambda i: (0, 0))
    return pl.pallas_call(
        _merge_kernel,
        grid=(n // tm,),
        in_specs=[row(d), row(w_sb), row(w_hg), row(d, gate_blk), row(d, gate_blk + 1),
                  const(wbs.shape), const(wbh.shape), const(wo.shape), const((1, d)),
                  const(wr.shape), const((1, LANES))],
        out_specs=[row(d), pl.BlockSpec((tm * SUBLANES, LANES), lambda i: (i, 0)), row(LANES)],
        out_shape=[jax.ShapeDtypeStruct((n, d), F32),
                   jax.ShapeDtypeStruct((n * SUBLANES, LANES), F32),
                   jax.ShapeDtypeStruct((n, LANES), F32)],
        scratch_shapes=[pltpu.VMEM(w.shape, BF16) for w in (wbs, wbh, wo)],
        compiler_params=_params(("arbitrary",)),
        name="merge",
    )(x2, y_sb, y_hg, rest, rest, wbs, wbh, wo, ln_g.reshape(1, d), wr, br)


TOKEN_ROWS_OUT = 2 * SUBLANES


def _plan_kernel(route_ref, ltri_ref, utri_ref, pos_ref, te_ref, meta_ref,
                 tot_scr, run_scr, *, tile_rows):
    ph = pl.program_id(0)
    i = pl.program_id(1)
    route = route_ref[...]
    tm = route.shape[0]
    lane = lax.broadcasted_iota(jnp.int32, (tm, LANES), 1)
    is_cls = lane.astype(F32) == route[:, ROUTE_CLS:ROUTE_CLS + 1]
    sel = jnp.where(is_cls, 1.0, 0.0)

    @pl.when(jnp.logical_and(ph == 0, i == 0))
    def _():
        tot_scr[...] = jnp.zeros_like(tot_scr)

    @pl.when(ph == 0)
    def _():
        tot_scr[...] += jnp.sum(sel, axis=0, keepdims=True)

    @pl.when(ph == 1)
    def _():
        tot = tot_scr[...]
        top = tot + (tile_rows - 1.0)
        tiles = jnp.floor(top * (1.0 / tile_rows))
        tiles = jnp.where(tiles * tile_rows > top, tiles - 1.0, tiles)
        tiles = jnp.where((tiles + 1.0) * tile_rows <= top, tiles + 1.0, tiles)
        first_tile = _split_dot(jnp.broadcast_to(tiles, (SUBLANES, LANES)), utri_ref[...], 3)[0:1]

        @pl.when(i == 0)
        def _():
            run_scr[...] = jnp.zeros_like(run_scr)
            lane1 = lax.broadcasted_iota(jnp.int32, (1, LANES), 1)
            is_c = lane1 < N_CLASSES
            end_tile = first_tile + tiles
            n_valid = jnp.sum(jnp.where(is_c, tiles, 0.0), axis=1, keepdims=True)
            rows = te_ref.shape[0]
            j = lax.broadcasted_iota(jnp.int32, (rows, LANES), 0).astype(F32)
            j = jnp.minimum(j, n_valid - 1.0)
            lane2 = lax.broadcasted_iota(jnp.int32, (rows, LANES), 1)
            done = jnp.where(jnp.logical_and(lane2 < N_CLASSES, end_tile <= j), 1.0, 0.0)
            tc = jnp.sum(done, axis=1, keepdims=True)
            tg = jnp.floor((tc + 0.5) * (1.0 / N_PAIRS))
            ta, tb = _pair_slots(tc - tg * N_PAIRS)
            te = jnp.where(lane2 == 0, tg * EXPERTS_PER_GROUP + ta, tg * EXPERTS_PER_GROUP + tb)
            te_ref[...] = te.astype(jnp.int32)
            last_tile = jnp.where(jnp.logical_and(is_c, tiles > 0), end_tile - 1.0, -1.0)
            meta = jnp.where(lane1 == N_CLASSES, n_valid, last_tile)
            meta_ref[...] = jnp.broadcast_to(meta, meta_ref.shape).astype(jnp.int32)

        rank = jnp.dot(ltri_ref[...], sel.astype(BF16), preferred_element_type=F32) + run_scr[...]
        dest = jnp.sum(jnp.where(is_cls, first_tile * tile_rows + rank, 0.0),
                       axis=1, keepdims=True)
        row = lax.broadcasted_iota(jnp.int32, (tm, LANES), 0)
        spread = jnp.where(lane == row % LANES, dest, 0.0)
        pos_ref[...] = jnp.sum(spread.reshape(tm // LANES, LANES, LANES), axis=1).astype(jnp.int32)
        run_scr[...] += jnp.sum(sel, axis=0, keepdims=True)


def _plan(route, n_tiles_max):
    n = route.shape[0]
    tm = min(PLAN_TM, n)
    r = np.arange(tm)
    ltri = jnp.asarray((r[None, :] < r[:, None]).astype(np.float32), dtype=BF16)
    e = np.arange(LANES)
    utri = jnp.asarray((e[:, None] < e[None, :]).astype(np.float32), dtype=BF16)
    te_rows = -(-n_tiles_max // SUBLANES) * SUBLANES
    kern = functools.partial(_plan_kernel, tile_rows=MOE_TM)
    return pl.pallas_call(
        kern,
        grid=(2, n // tm),
        in_specs=[pl.BlockSpec((tm, LANES), lambda p, i: (i, 0)),
                  pl.BlockSpec((tm, tm), lambda p, i: (0, 0)),
                  pl.BlockSpec((LANES, LANES), lambda p, i: (0, 0))],
        out_specs=[pl.BlockSpec((tm // LANES, LANES), lambda p, i: (i * p, 0)),
                   pl.BlockSpec((te_rows, LANES), lambda p, i: (0, 0)),
                   pl.BlockSpec((SUBLANES, LANES), lambda p, i: (0, 0))],
        out_shape=[jax.ShapeDtypeStruct((n // LANES, LANES), jnp.int32),
                   jax.ShapeDtypeStruct((te_rows, LANES), jnp.int32),
                   jax.ShapeDtypeStruct((SUBLANES, LANES), jnp.int32)],
        scratch_shapes=[pltpu.VMEM((1, LANES), F32), pltpu.VMEM((1, LANES), F32)],
        compiler_params=_params(("arbitrary", "arbitrary")),
        name="plan",
    )(route, ltri, utri)


DMA_UNROLL = 8


def _token_copy(src_ref, r, dst_ref, p, sem, rows):
    return pltpu.make_async_copy(src_ref.at[pl.ds(pl.multiple_of(r * rows, rows), rows)],
                                 dst_ref.at[pl.ds(pl.multiple_of(p * rows, rows), rows)], sem)


def _dispatch_kernel(pos_ref, meta_ref, t_ref, xs_ref, zero_scr, sem, tail_sem, *, tm, tile_rows):
    i = pl.program_id(0)
    tile_rows = tile_rows * SUBLANES
    n_tiles = xs_ref.shape[0] // tile_rows
    n_valid = meta_ref[N_CLASSES]

    def clear(tile, s):
        return pltpu.make_async_copy(zero_scr, xs_ref.at[pl.ds(tile * tile_rows, tile_rows)], s)

    def for_each(clears, fn):
        for tile, cond in clears:
            @pl.when(cond)
            def _():
                fn(tile)

    last_tiles = [(meta_ref[c], meta_ref[c] >= 0) for c in range(N_CLASSES)]
    past_tiles = [(n_tiles - 1 - k, n_tiles - 1 - k >= n_valid) for k in range(N_CLASSES)]

    @pl.when(i == 0)
    def _():
        zero_scr[...] = jnp.zeros_like(zero_scr)
        for_each(last_tiles, lambda tile: clear(tile, sem).start())
        for_each(past_tiles, lambda tile: clear(tile, tail_sem).start())
        for_each(last_tiles, lambda tile: clear(tile, sem).wait())

    base = i * tm

    def start(r2, carry):
        for k in range(2):
            r = 2 * r2 + k
            _token_copy(t_ref, r, xs_ref, pos_ref[base + r], sem, SUBLANES).start(priority=k)
        return carry

    lax.fori_loop(0, tm // 2, start, 0, unroll=DMA_UNROLL)
    pltpu.make_async_copy(t_ref, xs_ref.at[pl.ds(0, tm * SUBLANES)], sem).wait()

    @pl.when(i == pl.num_programs(0) - 1)
    def _():
        for_each(past_tiles, lambda tile: clear(tile, tail_sem).wait())


def _dispatch(pos, meta, t, n_rows):
    n = t.shape[0] // SUBLANES
    tm = min(DISPATCH_TM, n)
    kern = functools.partial(_dispatch_kernel, tm=tm, tile_rows=MOE_TM)
    grid_spec = pltpu.PrefetchScalarGridSpec(
        num_scalar_prefetch=2,
        grid=(n // tm,),
        in_specs=[pl.BlockSpec((tm * SUBLANES, LANES), lambda i, *_: (i, 0))],
        out_specs=pl.BlockSpec(memory_space=pl.ANY),
        scratch_shapes=[pltpu.VMEM((MOE_TM * SUBLANES, LANES), F32),
                        pltpu.SemaphoreType.DMA(()), pltpu.SemaphoreType.DMA(())],
    )
    return pl.pallas_call(
        kern,
        grid_spec=grid_spec,
        out_shape=jax.ShapeDtypeStruct((n_rows * SUBLANES, LANES), F32),
        compiler_params=_params(("arbitrary",)),
        name="dispatch",
    )(pos, meta, t)


def _expert_kernel(tea_ref, teb_ref, nv_ref, xs_ref, wga, wua, wda, wgb, wub, wdb, ys_ref):
    in_use = pl.program_id(0) < nv_ref[0]
    rows = xs_ref.shape[0] // SUBLANES

    @pl.when(in_use)
    def _():
        x = _load_token_tiles(xs_ref).astype(BF16)
        for slot, (wg, wu, wd) in enumerate(((wga, wua, wda), (wgb, wub, wdb))):
            hg = jnp.dot(x, wg[...].astype(BF16), preferred_element_type=F32)
            hu = jnp.dot(x, wu[...].astype(BF16), preferred_element_type=F32)
            a = (hg * jax.nn.sigmoid(hg) * hu).astype(BF16)
            y = jnp.dot(a, wd[...].astype(BF16), preferred_element_type=F32)
            for s in range(SUBLANES):
                ys_ref[pl.ds(slot * SUBLANES + s, rows, stride=TOKEN_ROWS_OUT), :] = (
                    y[:, s * LANES:(s + 1) * LANES])

    @pl.when(jnp.logical_not(in_use))
    def _():
        ys_ref[...] = jnp.zeros_like(ys_ref)


def _experts(tea, teb, nv, xs, wg, wu, wd):
    _, d, de = wg.shape
    assert d == SUBLANES * LANES
    n_tiles = xs.shape[0] // (MOE_TM * SUBLANES)
    wa = lambda shape: pl.BlockSpec(shape, lambda j, tea, teb, nv: (tea[j], 0, 0))
    wb = lambda shape: pl.BlockSpec(shape, lambda j, tea, teb, nv: (teb[j], 0, 0))
    grid_spec = pltpu.PrefetchScalarGridSpec(
        num_scalar_prefetch=3,
        grid=(n_tiles,),
        in_specs=[pl.BlockSpec((MOE_TM * SUBLANES, LANES),
                               lambda j, tea, teb, nv: (jnp.minimum(j, nv[0] - 1), 0)),
                  wa((None, d, de)), wa((None, d, de)), wa((None, de, d)),
                  wb((None, d, de)), wb((None, d, de)), wb((None, de, d))],
        out_specs=pl.BlockSpec((MOE_TM * TOKEN_ROWS_OUT, LANES), lambda j, *_: (j, 0)),
    )
    return pl.pallas_call(
        _expert_kernel,
        grid_spec=grid_spec,
        out_shape=jax.ShapeDtypeStruct((n_tiles * MOE_TM * TOKEN_ROWS_OUT, LANES), F32),
        compiler_params=_params(("arbitrary",)),
        name="experts",
    )(tea, teb, nv, xs, wg, wu, wd, wg, wu, wd)


def _combine_kernel(pos_ref, ys_ref, x1_ref, route_ref, fg_ref, o_ref, g, sem, *, tm):
    i = pl.program_id(0)
    slot = i % 2

    def gather(step, buf):
        base = step * tm

        def body(r2, carry):
            for k in range(2):
                r = 2 * r2 + k
                _token_copy(ys_ref, pos_ref[base + r], g.at[buf], r, sem.at[buf],
                            TOKEN_ROWS_OUT).start(priority=k)
            return carry

        lax.fori_loop(0, tm // 2, body, 0, unroll=DMA_UNROLL)

    @pl.when(i == 0)
    def _():
        gather(0, 0)

    @pl.when(i + 1 < pl.num_programs(0))
    def _():
        gather(i + 1, 1 - slot)

    pltpu.make_async_copy(ys_ref.at[pl.ds(0, tm * TOKEN_ROWS_OUT)], g.at[slot],
                          sem.at[slot]).wait()
    gs = g.at[slot]
    ya, yb = (jnp.concatenate([gs[pl.ds(off + s, tm, stride=TOKEN_ROWS_OUT), :]
                               for s in range(SUBLANES)], axis=1) for off in (0, SUBLANES))
    route = route_ref[...]
    x2 = x1_ref[...] + (route[:, ROUTE_WA:ROUTE_WA + 1] * ya + route[:, ROUTE_WB:ROUTE_WB + 1] * yb)
    var = jnp.mean(x2 * x2, axis=-1, keepdims=True)
    o_ref[...] = x2 * lax.rsqrt(var + EPS) * fg_ref[...]


def _combine(pos, ys, x1, route, final_g):
    n, d = x1.shape
    tm = min(COMBINE_TM, n)
    kern = functools.partial(_combine_kernel, tm=tm)
    grid_spec = pltpu.PrefetchScalarGridSpec(
        num_scalar_prefetch=1,
        grid=(n // tm,),
        in_specs=[pl.BlockSpec(memory_space=pl.ANY),
                  pl.BlockSpec((tm, d), lambda i, *_: (i, 0)),
                  pl.BlockSpec((tm, LANES), lambda i, *_: (i, 0)),
                  pl.BlockSpec((1, d), lambda i, *_: (0, 0))],
        out_specs=pl.BlockSpec((tm, d), lambda i, *_: (i, 0)),
        scratch_shapes=[pltpu.VMEM((2, tm * TOKEN_ROWS_OUT, LANES), F32),
                        pltpu.SemaphoreType.DMA((2,))],
    )
    return pl.pallas_call(
        kern,
        grid_spec=grid_spec,
        out_shape=jax.ShapeDtypeStruct((n, d), F32),
        compiler_params=_params(("arbitrary",)),
        name="combine",
    )(pos, ys, x1, route, final_g.reshape(1, d))


def _moe_sparse(t, route, wg, wu, wd, x1, final_g):
    n = x1.shape[0]
    n_tiles_max = (n + N_CLASSES * (MOE_TM - 1)) // MOE_TM
    pos, te, meta = _plan(route, n_tiles_max)
    pos = pos.reshape(n)
    xs = _dispatch(pos, meta[0, :N_CLASSES + 1], t, n_tiles_max * MOE_TM)
    ys = _experts(te[:n_tiles_max, 0], te[:n_tiles_max, 1], meta[0, N_CLASSES:N_CLASSES + 1],
                  xs, wg, wu, wd)
    return _combine(pos, ys, x1, route, final_g)


def _suffix_ones(t):
    j = np.arange(t)[:, None]
    s = np.arange(t)[None, :]
    return jnp.asarray((j > s).astype(np.float32), dtype=BF16)


def _causal_bias(t):
    row = np.arange(t)[:, None]
    col = np.arange(t)[None, :]
    diag = np.where(col < row, 0.0, MASK_BIAS).astype(np.float32)
    return jnp.asarray(np.stack([np.zeros_like(diag), diag]))


def _chunk_prefix_ones(tt, c):
    t = np.arange(tt)[:, None]
    j = np.arange(tt)[None, :]
    return jnp.asarray(((j <= t) & (t // c == j // c)).astype(np.float32), dtype=BF16)


def _block_diag_ones(w, blk):
    a = np.arange(w)
    return jnp.asarray((a[:, None] // blk == a[None, :] // blk).astype(np.float32), dtype=BF16)


def kernel(x, ln1_g, w_in, w_branch_sb, w_branch_hg, hg_norm_g, hg_lb_logits, w_out, ln2_g,
           w_router_group, b_router_group, w_router_expert, b_router_expert,
           w_exp_gate, w_exp_up, w_exp_down, final_g):
    bsz, seq, d = x.shape
    depth = w_in.shape[0]
    n = bsz * seq
    sb_width = SB_HEADS * SB_HEAD_DIM
    hg_width = HG_HEADS * HG_DIM

    tri_attn = _suffix_ones(min(ATTN_T, seq))
    bias_attn = _causal_bias(min(ATTN_T, seq))
    tt = min(HGRN_TT, seq)
    tri_hg = _chunk_prefix_ones(tt, min(HGRN_C, tt))
    bd = _block_diag_ones(min(MXU_DIM, hg_width), HG_DIM)

    x2 = x.reshape(n, d)
    for l in range(depth):
        qkv, rest = _inproj(x2, ln1_g[l], w_in[l], sb_width)
        y_sb = _attn(qkv, bsz, seq, tri_attn, bias_attn)
        y_hg = _hgrn(rest, hg_lb_logits, hg_norm_g[l], bsz, seq, l, tri_hg, bd)

        pad = LANES - N_EXPERTS - N_GROUPS
        wr = jnp.concatenate([w_router_expert[l], w_router_group[l],
                              jnp.zeros((d, pad), F32)], axis=1)
        wr_hi = wr.astype(BF16)
        wr_lo = (wr - wr_hi.astype(F32)).astype(BF16)
        wr_split = jnp.concatenate([wr_hi, wr_lo], axis=1)
        br = jnp.concatenate([b_router_expert[l], b_router_group[l],
                              jnp.zeros((pad,), F32)]).reshape(1, LANES)

        last = l == depth - 1
        x1, t, route = _merge(x2, y_sb, y_hg, rest, w_branch_sb[l], w_branch_hg[l], w_out[l],
                              ln2_g[l], wr_split, br)
        assert last, "final rmsnorm is fused into the last layer's combine kernel"
        x2 = _moe_sparse(t, route, w_exp_gate[l], w_exp_up[l], w_exp_down[l], x1, final_g)
    return x2.reshape(bsz, seq, d)
```

```python
import functools

import jax
import jax.numpy as jnp
import numpy as np
from jax import lax
from jax.experimental import pallas as pl
from jax.experimental.pallas import tpu as pltpu

F32 = jnp.float32
BF16 = jnp.bfloat16

EPS = 1e-6
SB_HEADS = 8
SB_HEAD_DIM = 64
HG_HEADS = 8
HG_DIM = 64
N_GROUPS = 4
EXPERTS_PER_GROUP = 4
N_EXPERTS = N_GROUPS * EXPERTS_PER_GROUP

LANES = 128
SUBLANES = 8
MXU_DIM = 256
LOG2E = 1.4426950408889634
VMEM_LIMIT = 56 * 1024 * 1024

INPROJ_TM = 2048
INPROJ_TN = 512
ATTN_T = MXU_DIM
HGRN_TT = 256
HGRN_C = 32
MERGE_TM = 512
MOE_TM = 384
PLAN_TM = 1024
DISPATCH_TM = 1024
COMBINE_TM = 512


def _params(sem):
    return pltpu.CompilerParams(dimension_semantics=sem, vmem_limit_bytes=VMEM_LIMIT)


def _split_dot(x, m, passes):
    acc = None
    r = x
    for p in range(passes):
        h = r.astype(BF16)
        term = jnp.dot(h, m, preferred_element_type=F32)
        acc = term if acc is None else acc + term
        if p + 1 < passes:
            r = r - h.astype(F32)
    return acc


def _inproj_kernel(x_ref, g_ref, w_ref, qkv_ref, rest_ref, h_scr, w_scr, *, q_scale):
    i = pl.program_id(0)
    j = pl.program_id(1)

    @pl.when(j == 0)
    def _():
        x = x_ref[...]
        var = jnp.mean(x * x, axis=-1, keepdims=True)
        h_scr[...] = (x * lax.rsqrt(var + EPS) * g_ref[...]).astype(BF16)

    @pl.when(i == 0)
    def _():
        w_scr[j] = w_ref[...].astype(BF16)

    acc = jnp.dot(h_scr[...], w_scr[j], preferred_element_type=F32)
    qkv_ref[...] = (acc * jnp.where(j == 0, q_scale, 1.0)).astype(BF16)
    rest_ref[...] = acc


def _inproj(x2, ln_g, w_in, sb_width):
    n, d = x2.shape
    cols = w_in.shape[1]
    tm, tn = min(INPROJ_TM, n), INPROJ_TN
    assert sb_width == tn, "q block must be exactly one column tile"
    nq = 3 * sb_width // tn
    nj = cols // tn
    kern = functools.partial(_inproj_kernel, q_scale=SB_HEAD_DIM ** -0.5)
    return pl.pallas_call(
        kern,
        grid=(n // tm, nj),
        in_specs=[
            pl.BlockSpec((tm, d), lambda i, j: (i, 0)),
            pl.BlockSpec((1, d), lambda i, j: (0, 0)),
            pl.BlockSpec((d, tn), lambda i, j: (0, jnp.where(i == 0, j, nj - 1))),
        ],
        out_specs=[
            pl.BlockSpec((tm, tn), lambda i, j: (i, jnp.minimum(j, nq))),
            pl.BlockSpec((tm, tn), lambda i, j: (i, jnp.maximum(j - nq, 0))),
        ],
        out_shape=[
            jax.ShapeDtypeStruct((n, (nq + 1) * tn), BF16),
            jax.ShapeDtypeStruct((n, cols - nq * tn), F32),
        ],
        scratch_shapes=[pltpu.VMEM((tm, d), BF16), pltpu.VMEM((nj, d, tn), BF16)],
        compiler_params=_params(("arbitrary", "arbitrary")),
        name="inproj",
    )(x2, ln_g.reshape(1, d), w_in)


ATTN_STAGES = 3
ATTN_STREAMS = 2
MASK_BIAS = -1e30
ATTN_SKIP = 111.0


def _attn_kernel(q_ref, k_ref, v_ref, tri_ref, bias_ref, o_ref,
                 z0, z1, z2, i0, i1, i2, acc_ref, c_ref, *, t, nq):
    zbuf = (z0, z1, z2)
    ibuf = (i0, i1, i2)
    for r in zbuf + ibuf + (acc_ref, c_ref):
        r[...] = jnp.zeros_like(r)

    lane = lax.broadcasted_iota(jnp.int32, (t, LANES), 1)
    head0 = lane < SB_HEAD_DIM
    nt = (((1,), (1,)), ((), ()))

    def stage_a(qi, kj, slot):
        q = q_ref[pl.ds(pl.multiple_of(qi * t, t), t), :]
        zero = jnp.zeros_like(q)
        q2 = jnp.concatenate([jnp.where(head0, q, zero), jnp.where(head0, zero, q)], axis=0)
        k = k_ref[pl.ds(pl.multiple_of(kj * t, t), t), :]
        z = lax.dot_general(q2, k, nt, preferred_element_type=F32)
        bias = bias_ref[(qi == kj).astype(jnp.int32)]
        zbuf[slot][...] = z + jnp.concatenate([bias, bias], axis=0)

    def stage_b(slot):
        z = zbuf[slot][...]
        p = jnp.maximum(z, 0.0) + jnp.log(1.0 + jnp.exp2(jnp.abs(z) * (-LOG2E)))
        incl = jnp.dot(p.astype(BF16), tri_ref[...], preferred_element_type=F32) + p
        ibuf[slot][...] = incl
        return incl[:, 0:1]

    def stage_c(qi, kj, slot, stream):
        first = qi == kj
        incl = ibuf[slot][...]
        c = jnp.where(first, 0.0, c_ref[stream])
        a = jnp.exp2((zbuf[slot][...] - incl - c) * LOG2E)
        v = v_ref[pl.ds(pl.multiple_of(kj * t, t), t), :]
        pv = jnp.dot(a.astype(BF16), v, preferred_element_type=F32)
        acc = jnp.where(first, pv, acc_ref[stream] + pv)
        acc_ref[stream] = acc
        c_ref[stream] = c + incl[:, 0:1]
        o_ref[pl.ds(pl.multiple_of(qi * t, t), t), :] = (
            jnp.where(head0, acc[0:t], acc[t:2 * t]).astype(o_ref.dtype))

    def block(carry):
        qs, ks, qn, kn, skip_q, drained = carry
        qs, ks, qn, kn, skip_q = list(qs), list(ks), list(qn), list(kn), list(skip_q)
        all_done = qn[0] >= nq
        for p in range(1, ATTN_STREAMS):
            all_done = jnp.logical_and(all_done, qn[p] >= nq)
        drained = drained + all_done.astype(jnp.int32)
        for r in range(ATTN_STAGES * ATTN_STREAMS):
            sa, sc, sb = r % ATTN_STAGES, (r + 1) % ATTN_STAGES, (r + 2) % ATTN_STAGES
            pa, pb = r % ATTN_STREAMS, (r - 1) % ATTN_STREAMS
            stage_c(qs[sc], ks[sc], sc, pa)
            total_b = stage_b(sb)
            carry_b = jnp.where(qs[sb] == ks[sb], 0.0, c_ref[pb]) + total_b
            skip = skip_q[pa] == qn[pa]
            q_cur = jnp.where(skip, qn[pa] + ATTN_STREAMS, qn[pa])
            k_cur = jnp.where(skip, qn[pa] + ATTN_STREAMS, kn[pa])
            drain = q_cur >= nq
            qa = jnp.where(drain, 0, q_cur)
            ka = jnp.where(drain, 0, k_cur)
            stage_a(qa, ka, sa)
            last = k_cur == 0
            qn[pa] = jnp.where(jnp.logical_and(last, jnp.logical_not(drain)),
                               q_cur + ATTN_STREAMS, q_cur)
            kn[pa] = jnp.where(drain, k_cur, jnp.where(last, q_cur + ATTN_STREAMS, k_cur - 1))
            skip_q[pb] = jnp.where(jnp.min(carry_b) >= ATTN_SKIP, qs[sb], skip_q[pb])
            qs[sa], ks[sa] = qa, ka
        return tuple(qs), tuple(ks), tuple(qn), tuple(kn), tuple(skip_q), drained

    zero = jnp.int32(0)
    first_q = tuple(jnp.int32(p) for p in range(ATTN_STREAMS))
    init = ((zero,) * ATTN_STAGES, (zero,) * ATTN_STAGES, first_q, first_q,
            (jnp.int32(-1),) * ATTN_STREAMS, zero)
    lax.while_loop(lambda carry: carry[5] < 1, block, init)


def _attn(qkv, bsz, seq, tri, bias):
    n = bsz * seq
    t = min(ATTN_T, seq)
    pairs = SB_HEADS * SB_HEAD_DIM // LANES
    kern = functools.partial(_attn_kernel, t=t, nq=seq // t)
    return pl.pallas_call(
        kern,
        grid=(bsz, pairs),
        in_specs=[
            pl.BlockSpec((seq, LANES), lambda b, p: (b, p)),
            pl.BlockSpec((seq, LANES), lambda b, p: (b, pairs + p)),
            pl.BlockSpec((seq, LANES), lambda b, p: (b, 2 * pairs + p)),
            pl.BlockSpec((t, t), lambda b, p: (0, 0)),
            pl.BlockSpec((2, t, t), lambda b, p: (0, 0, 0)),
        ],
        out_specs=pl.BlockSpec((seq, LANES), lambda b, p: (b, p)),
        out_shape=jax.ShapeDtypeStruct((n, pairs * LANES), BF16),
        scratch_shapes=([pltpu.VMEM((2 * t, t), F32)] * (2 * ATTN_STAGES)
                        + [pltpu.VMEM((ATTN_STREAMS, 2 * t, LANES), F32),
                           pltpu.VMEM((ATTN_STREAMS, 2 * t, 1), F32)]),
        compiler_params=_params(("parallel", "parallel")),
        name="attn",
    )(qkv, qkv, qkv, tri, bias)


def _group_dot(x, bd):
    g = bd.shape[0]
    parts = [jnp.dot(x[:, i:i + g], bd, preferred_element_type=F32)
             for i in range(0, x.shape[1], g)]
    return jnp.concatenate(parts, axis=1)


def _hgrn_kernel(q_ref, f_ref, i_ref, g_ref, lbl_ref, ng_ref, tri_ref, bd_ref, o_ref,
                 st_scr, wpad, vpad, wsh, vsh, *, tt, c, layer):
    ti = pl.program_id(1)
    w = q_ref.shape[1]
    nch = tt // c

    @pl.when(ti == 0)
    def _():
        st_scr[...] = jnp.zeros_like(st_scr)

    lg = lbl_ref[...]
    e = jnp.exp(lg - jnp.max(lg, axis=0, keepdims=True))
    lb = jnp.sum(e[0:layer + 1], axis=0, keepdims=True) / jnp.sum(e, axis=0, keepdims=True)

    f = lb + (1.0 - lb) * jax.nn.sigmoid(f_ref[...])
    kk = 1.0 - f
    qv = q_ref[...]
    qs = qv * jax.nn.sigmoid(qv)
    v = i_ref[...]
    bd = bd_ref[...]
    cum = _split_dot_left(tri_ref[...], jnp.log(f) * LOG2E, 3)
    wk = jnp.log(kk) * LOG2E - cum

    pos = lax.broadcasted_iota(jnp.int32, (tt, 1), 0) % c
    wpad[0:SUBLANES, :] = jnp.zeros((SUBLANES, w), F32)
    vpad[0:SUBLANES, :] = jnp.zeros((SUBLANES, w), F32)
    wpad[SUBLANES:SUBLANES + tt, :] = wk
    vpad[SUBLANES:SUBLANES + tt, :] = v
    for b in range(SUBLANES):
        wb = wpad[SUBLANES - b:SUBLANES - b + tt, :]
        vb = vpad[SUBLANES - b:SUBLANES - b + tt, :]
        if b:
            wb = jnp.where(pos >= b, wb, -jnp.inf)
        wsh[b] = wb.reshape(nch, c, w)
        vsh[b] = vb.reshape(nch, c, w)

    qs3 = qs.reshape(nch, c, w)
    cum3 = cum.reshape(nch, c, w)
    acc = None
    for a in reversed(range(c // SUBLANES)):
        rows = c - SUBLANES * a
        qa = qs3[:, SUBLANES * a:, :].reshape(nch * rows, w)
        ca = cum3[:, SUBLANES * a:, :].reshape(nch * rows, w)
        acc_a = None
        for b in range(SUBLANES):
            wb = wsh[b, :, 0:rows, :].reshape(nch * rows, w)
            vb = vsh[b, :, 0:rows, :].reshape(nch * rows, w)
            dd = qa * jnp.exp2(ca + wb)
            term = _group_dot(dd.astype(BF16), bd) * vb
            acc_a = term if acc_a is None else acc_a + term
        acc_a = acc_a.reshape(nch, rows, w)
        if acc is not None:
            acc_a = acc_a + jnp.concatenate([jnp.zeros((nch, SUBLANES, w), F32), acc], axis=1)
        acc = acc_a
    acc = acc.reshape(tt, w)

    grp = st_scr.shape[1]
    bdmask = bd[0:grp, 0:grp] != 0
    outs = []
    for ci in range(nch):
        r0 = ci * c
        cum_c = cum[r0:r0 + c]
        last = cum_c[c - 1:c]
        qd = (qs[r0:r0 + c] * jnp.exp2(cum_c)).astype(BF16)
        kd = (kk[r0:r0 + c] * jnp.exp2(last - cum_c)).astype(BF16)
        vc = v[r0:r0 + c].astype(BF16)
        dec = jnp.exp2(last)
        o_parts = []
        for gi in range(w // grp):
            sl = slice(gi * grp, (gi + 1) * grp)
            st = st_scr[gi]
            o_parts.append(lax.dot_general(qd[:, sl], st.astype(BF16), (((1,), (1,)), ((), ())),
                                           preferred_element_type=F32))
            upd = lax.dot_general(vc[:, sl], kd[:, sl], (((0,), (0,)), ((), ())),
                                  preferred_element_type=F32)
            st_scr[gi] = st * dec[:, sl] + jnp.where(bdmask, upd, 0.0)
        outs.append(jnp.concatenate(o_parts, axis=1))
    o = acc + jnp.concatenate(outs, axis=0)

    o2 = o * o
    o2_hi = o2.astype(BF16)
    o2_lo = (o2 - o2_hi.astype(F32)).astype(BF16)
    ms = (_group_dot(o2_hi, bd) + _group_dot(o2_lo, bd)) * (1.0 / HG_DIM)
    gv = g_ref[...]
    o = o * lax.rsqrt(ms + EPS) * ng_ref[...] * (gv * jax.nn.sigmoid(gv))
    o_ref[...] = o.astype(o_ref.dtype)


def _split_dot_left(m, x, passes):
    acc = None
    r = x
    for p in range(passes):
        h = r.astype(BF16)
        term = jnp.dot(m, h, preferred_element_type=F32)
        acc = term if acc is None else acc + term
        if p + 1 < passes:
            r = r - h.astype(F32)
    return acc


def _hgrn(rest, lb_logits, norm_g, bsz, seq, layer, tri, bd):
    n = bsz * seq
    w = HG_HEADS * HG_DIM
    grp = bd.shape[0]
    tt = min(HGRN_TT, seq)
    c = min(HGRN_C, tt)
    nt = seq // tt
    kern = functools.partial(_hgrn_kernel, tt=tt, c=c, layer=layer)
    col = lambda j: pl.BlockSpec((tt, w), lambda b, i, j=j: (b * nt + i, j))
    const = lambda shape: pl.BlockSpec(shape, lambda b, i: (0, 0))
    return pl.pallas_call(
        kern,
        grid=(bsz, nt),
        in_specs=[col(0), col(1), col(2), col(3),
                  const(lb_logits.shape), const((1, w)), const((tt, tt)), const((grp, grp))],
        out_specs=pl.BlockSpec((tt, w), lambda b, i: (b * nt + i, 0)),
        out_shape=jax.ShapeDtypeStruct((n, w), BF16),
        scratch_shapes=([pltpu.VMEM((w // LANES, LANES, LANES), F32)]
                        + [pltpu.VMEM((SUBLANES + tt, w), F32)] * 2
                        + [pltpu.VMEM((SUBLANES, tt // c, c, w), F32)] * 2),
        compiler_params=_params(("parallel", "arbitrary")),
        name="hgrn",
    )(rest, rest, rest, rest, lb_logits, norm_g.reshape(1, w), tri, bd)


PAIR_A = (0, 0, 0, 1, 1, 3)
PAIR_B = (1, 2, 3, 3, 2, 2)
N_PAIRS = len(PAIR_A)
N_CLASSES = N_GROUPS * N_PAIRS
assert sorted(tuple(sorted(p)) for p in zip(PAIR_A, PAIR_B)) == [
    (a, b) for a in range(EXPERTS_PER_GROUP) for b in range(a + 1, EXPERTS_PER_GROUP)]
ROUTE_CLS, ROUTE_WA, ROUTE_WB = 0, 1, 2


def _pair_slots(pidx):
    a = b = jnp.zeros_like(pidx)
    for p in range(N_PAIRS):
        a = jnp.where(pidx == p, float(PAIR_A[p]), a)
        b = jnp.where(pidx == p, float(PAIR_B[p]), b)
    return a, b


def _pair_index(lo, hi):
    pidx = jnp.zeros_like(lo)
    for p in range(N_PAIRS):
        is_p = jnp.logical_and(lo == min(PAIR_A[p], PAIR_B[p]), hi == max(PAIR_A[p], PAIR_B[p]))
        pidx = jnp.where(is_p, float(p), pidx)
    return pidx


def _route(lg):
    lane = lax.broadcasted_iota(jnp.int32, lg.shape, 1)
    neg = jnp.float32(-jnp.inf)
    big = jnp.int32(LANES)
    gmask = jnp.logical_and(lane >= N_EXPERTS, lane < N_EXPERTS + N_GROUPS)
    gl = jnp.where(gmask, lg, neg)
    gmax = jnp.max(gl, axis=1, keepdims=True)
    gidx = jnp.min(jnp.where(gl == gmax, lane, big), axis=1, keepdims=True) - N_EXPERTS
    w_grp = 1.0 / jnp.sum(jnp.where(gmask, jnp.exp(gl - gmax), 0.0), axis=1, keepdims=True)
    in_grp = jnp.logical_and(lane < N_EXPERTS, lane // EXPERTS_PER_GROUP == gidx)
    l1 = jnp.where(in_grp, lg, neg)
    v1 = jnp.max(l1, axis=1, keepdims=True)
    i1 = jnp.min(jnp.where(l1 == v1, lane, big), axis=1, keepdims=True)
    l2 = jnp.where(jnp.logical_and(in_grp, lane != i1), lg, neg)
    v2 = jnp.max(l2, axis=1, keepdims=True)
    i2 = jnp.min(jnp.where(l2 == v2, lane, big), axis=1, keepdims=True)
    e2 = jnp.exp(v2 - v1)
    p1 = 1.0 / (1.0 + e2)
    p2 = e2 * p1
    loc1 = (i1 - gidx * EXPERTS_PER_GROUP).astype(F32)
    loc2 = (i2 - gidx * EXPERTS_PER_GROUP).astype(F32)
    pidx = _pair_index(jnp.minimum(loc1, loc2), jnp.maximum(loc1, loc2))
    cls = gidx.astype(F32) * N_PAIRS + pidx
    a_loc, _ = _pair_slots(pidx)
    first_is_a = loc1 == a_loc
    wa = jnp.where(first_is_a, p1, p2) * w_grp
    wb = jnp.where(first_is_a, p2, p1) * w_grp
    return jnp.where(lane == ROUTE_CLS, cls,
                     jnp.where(lane == ROUTE_WA, wa, jnp.where(lane == ROUTE_WB, wb, 0.0)))


def _store_token_tiles(ref, x):
    rows, d = x.shape
    assert d == SUBLANES * LANES
    for s in range(SUBLANES):
        ref[pl.ds(s, rows, stride=SUBLANES), :] = x[:, s * LANES:(s + 1) * LANES]


def _load_token_tiles(ref):
    rows = ref.shape[0] // SUBLANES
    return jnp.concatenate([ref[pl.ds(s, rows, stride=SUBLANES), :] for s in range(SUBLANES)],
                           axis=1)


def _merge_kernel(x_ref, ysb_ref, yhg_ref, gsb_ref, ghg_ref, wbs_ref, wbh_ref, wo_ref,
                  ln_ref, wr_ref, br_ref, x1_ref, t_ref, route_ref, tot_ref,
                  wbs_scr, wbh_scr, wo_scr):
    @pl.when(pl.program_id(0) == 0)
    def _():
        wbs_scr[...] = wbs_ref[...].astype(BF16)
        wbh_scr[...] = wbh_ref[...].astype(BF16)
        wo_scr[...] = wo_ref[...].astype(BF16)
        tot_ref[...] = jnp.zeros_like(tot_ref)

    a = jnp.dot(ysb_ref[...], wbs_scr[...], preferred_element_type=F32)
    b = jnp.dot(yhg_ref[...], wbh_scr[...], preferred_element_type=F32)
    merged = jax.nn.sigmoid(gsb_ref[...]) * a + jax.nn.sigmoid(ghg_ref[...]) * b
    x1 = x_ref[...] + jnp.dot(merged.astype(BF16), wo_scr[...], preferred_element_type=F32)
    x1_ref[...] = x1
    var = jnp.mean(x1 * x1, axis=-1, keepdims=True)
    t = x1 * lax.rsqrt(var + EPS) * ln_ref[...]
    _store_token_tiles(t_ref, t)
    t_hi = t.astype(BF16)
    t_lo = (t - t_hi.astype(F32)).astype(BF16)
    p_hi = jnp.dot(t_hi, wr_ref[...], preferred_element_type=F32)
    p_lo = jnp.dot(t_lo, wr_ref[...], preferred_element_type=F32)
    lg = (p_hi[:, :LANES] + p_hi[:, LANES:]) + (p_lo[:, :LANES] + p_lo[:, LANES:]) + br_ref[...]
    route = _route(lg)
    route_ref[...] = route
    lane = lax.broadcasted_iota(jnp.int32, route.shape, 1)
    sel = jnp.where(lane.astype(F32) == route[:, ROUTE_CLS:ROUTE_CLS + 1], 1.0, 0.0)
    tot_ref[...] += jnp.broadcast_to(jnp.sum(sel, axis=0, keepdims=True), tot_ref.shape)


def _merge(x2, y_sb, y_hg, rest, wbs, wbh, wo, ln_g, wr, br):
    n, d = x2.shape
    tm = min(MERGE_TM, n)
    w_sb, w_hg = y_sb.shape[1], y_hg.shape[1]
    gate_blk = (rest.shape[1] - 2 * d) // d
    row = lambda wdt, j=0: pl.BlockSpec((tm, wdt), lambda i, j=j: (i, j))
    const = lambda shape: pl.BlockSpec(shape, lambda i: (0, 0))
    return pl.pallas_call(
        _merge_kernel,
        grid=(n // tm,),
        in_specs=[row(d), row(w_sb), row(w_hg), row(d, gate_blk), row(d, gate_blk + 1),
                  const(wbs.shape), const(wbh.shape), const(wo.shape), const((1, d)),
                  const(wr.shape), const((1, LANES))],
        out_specs=[row(d), pl.BlockSpec((tm * SUBLANES, LANES), lambda i: (i, 0)), row(LANES),
                   const((SUBLANES, LANES))],
        out_shape=[jax.ShapeDtypeStruct((n, d), F32),
                   jax.ShapeDtypeStruct((n * SUBLANES, LANES), F32),
                   jax.ShapeDtypeStruct((n, LANES), F32),
                   jax.ShapeDtypeStruct((SUBLANES, LANES), F32)],
        scratch_shapes=[pltpu.VMEM(w.shape, BF16) for w in (wbs, wbh, wo)],
        compiler_params=_params(("arbitrary",)),
        name="merge",
    )(x2, y_sb, y_hg, rest, rest, wbs, wbh, wo, ln_g.reshape(1, d), wr, br)


TOKEN_ROWS_OUT = 2 * SUBLANES


def _plan_kernel(route_ref, tot_ref, ltri_ref, utri_ref, pos_ref, te_ref, meta_ref,
                 run_scr, *, tile_rows):
    i = pl.program_id(0)
    route = route_ref[...]
    tm = route.shape[0]
    lane = lax.broadcasted_iota(jnp.int32, (tm, LANES), 1)
    is_cls = lane.astype(F32) == route[:, ROUTE_CLS:ROUTE_CLS + 1]
    sel = jnp.where(is_cls, 1.0, 0.0)

    tot = tot_ref[0:1, :]
    top = tot + (tile_rows - 1.0)
    tiles = jnp.floor(top * (1.0 / tile_rows))
    tiles = jnp.where(tiles * tile_rows > top, tiles - 1.0, tiles)
    tiles = jnp.where((tiles + 1.0) * tile_rows <= top, tiles + 1.0, tiles)
    first_tile = _split_dot(jnp.broadcast_to(tiles, (SUBLANES, LANES)), utri_ref[...], 3)[0:1]

    @pl.when(i == 0)
    def _():
        run_scr[...] = jnp.zeros_like(run_scr)
        lane1 = lax.broadcasted_iota(jnp.int32, (1, LANES), 1)
        is_c = lane1 < N_CLASSES
        end_tile = first_tile + tiles
        n_valid = jnp.sum(jnp.where(is_c, tiles, 0.0), axis=1, keepdims=True)
        rows = te_ref.shape[0]
        j = lax.broadcasted_iota(jnp.int32, (rows, LANES), 0).astype(F32)
        j = jnp.minimum(j, n_valid - 1.0)
        lane2 = lax.broadcasted_iota(jnp.int32, (rows, LANES), 1)
        done = jnp.where(jnp.logical_and(lane2 < N_CLASSES, end_tile <= j), 1.0, 0.0)
        tc = jnp.sum(done, axis=1, keepdims=True)
        tg = jnp.floor((tc + 0.5) * (1.0 / N_PAIRS))
        ta, tb = _pair_slots(tc - tg * N_PAIRS)
        te = jnp.where(lane2 == 0, tg * EXPERTS_PER_GROUP + ta, tg * EXPERTS_PER_GROUP + tb)
        te_ref[...] = te.astype(jnp.int32)
        last_tile = jnp.where(jnp.logical_and(is_c, tiles > 0), end_tile - 1.0, -1.0)
        meta = jnp.where(lane1 == N_CLASSES, n_valid, last_tile)
        meta_ref[...] = jnp.broadcast_to(meta, meta_ref.shape).astype(jnp.int32)

    rank = jnp.dot(ltri_ref[...], sel.astype(BF16), preferred_element_type=F32) + run_scr[...]
    dest = jnp.sum(jnp.where(is_cls, first_tile * tile_rows + rank, 0.0), axis=1, keepdims=True)
    row = lax.broadcasted_iota(jnp.int32, (tm, LANES), 0)
    spread = jnp.where(lane == row % LANES, dest, 0.0)
    pos_ref[...] = jnp.sum(spread.reshape(tm // LANES, LANES, LANES), axis=1).astype(jnp.int32)
    run_scr[...] += jnp.sum(sel, axis=0, keepdims=True)


def _plan(route, tot, n_tiles_max):
    n = route.shape[0]
    tm = min(PLAN_TM, n)
    r = np.arange(tm)
    ltri = jnp.asarray((r[None, :] < r[:, None]).astype(np.float32), dtype=BF16)
    e = np.arange(LANES)
    utri = jnp.asarray((e[:, None] < e[None, :]).astype(np.float32), dtype=BF16)
    te_rows = -(-n_tiles_max // SUBLANES) * SUBLANES
    kern = functools.partial(_plan_kernel, tile_rows=MOE_TM)
    return pl.pallas_call(
        kern,
        grid=(n // tm,),
        in_specs=[pl.BlockSpec((tm, LANES), lambda i: (i, 0)),
                  pl.BlockSpec((SUBLANES, LANES), lambda i: (0, 0)),
                  pl.BlockSpec((tm, tm), lambda i: (0, 0)),
                  pl.BlockSpec((LANES, LANES), lambda i: (0, 0))],
        out_specs=[pl.BlockSpec((tm // LANES, LANES), lambda i: (i, 0)),
                   pl.BlockSpec((te_rows, LANES), lambda i: (0, 0)),
                   pl.BlockSpec((SUBLANES, LANES), lambda i: (0, 0))],
        out_shape=[jax.ShapeDtypeStruct((n // LANES, LANES), jnp.int32),
                   jax.ShapeDtypeStruct((te_rows, LANES), jnp.int32),
                   jax.ShapeDtypeStruct((SUBLANES, LANES), jnp.int32)],
        scratch_shapes=[pltpu.VMEM((1, LANES), F32)],
        compiler_params=_params(("arbitrary",)),
        name="plan",
    )(route, tot, ltri, utri)


DMA_UNROLL = 8


def _token_copy(src_ref, r, dst_ref, p, sem, rows):
    return pltpu.make_async_copy(src_ref.at[pl.ds(pl.multiple_of(r * rows, rows), rows)],
                                 dst_ref.at[pl.ds(pl.multiple_of(p * rows, rows), rows)], sem)


def _dispatch_kernel(pos_ref, meta_ref, t_ref, xs_ref, zero_scr, sem, tail_sem, *, tm, tile_rows):
    i = pl.program_id(0)
    tile_rows = tile_rows * SUBLANES
    n_tiles = xs_ref.shape[0] // tile_rows
    n_valid = meta_ref[N_CLASSES]

    def clear(tile, s):
        return pltpu.make_async_copy(zero_scr, xs_ref.at[pl.ds(tile * tile_rows, tile_rows)], s)

    def for_each(clears, fn):
        for tile, cond in clears:
            @pl.when(cond)
            def _():
                fn(tile)

    last_tiles = [(meta_ref[c], meta_ref[c] >= 0) for c in range(N_CLASSES)]
    past_tiles = [(n_tiles - 1 - k, n_tiles - 1 - k >= n_valid) for k in range(N_CLASSES)]

    @pl.when(i == 0)
    def _():
        zero_scr[...] = jnp.zeros_like(zero_scr)
        for_each(last_tiles, lambda tile: clear(tile, sem).start())
        for_each(past_tiles, lambda tile: clear(tile, tail_sem).start())
        for_each(last_tiles, lambda tile: clear(tile, sem).wait())

    base = i * tm

    def start(r2, carry):
        for k in range(2):
            r = 2 * r2 + k
            _token_copy(t_ref, r, xs_ref, pos_ref[base + r], sem, SUBLANES).start(priority=k)
        return carry

    lax.fori_loop(0, tm // 2, start, 0, unroll=DMA_UNROLL)
    pltpu.make_async_copy(t_ref, xs_ref.at[pl.ds(0, tm * SUBLANES)], sem).wait()

    @pl.when(i == pl.num_programs(0) - 1)
    def _():
        for_each(past_tiles, lambda tile: clear(tile, tail_sem).wait())


def _dispatch(pos, meta, t, n_rows):
    n = t.shape[0] // SUBLANES
    tm = min(DISPATCH_TM, n)
    kern = functools.partial(_dispatch_kernel, tm=tm, tile_rows=MOE_TM)
    grid_spec = pltpu.PrefetchScalarGridSpec(
        num_scalar_prefetch=2,
        grid=(n // tm,),
        in_specs=[pl.BlockSpec((tm * SUBLANES, LANES), lambda i, *_: (i, 0))],
        out_specs=pl.BlockSpec(memory_space=pl.ANY),
        scratch_shapes=[pltpu.VMEM((MOE_TM * SUBLANES, LANES), F32),
                        pltpu.SemaphoreType.DMA(()), pltpu.SemaphoreType.DMA(())],
    )
    return pl.pallas_call(
        kern,
        grid_spec=grid_spec,
        out_shape=jax.ShapeDtypeStruct((n_rows * SUBLANES, LANES), F32),
        compiler_params=_params(("arbitrary",)),
        name="dispatch",
    )(pos, meta, t)


def _expert_kernel(tea_ref, teb_ref, nv_ref, xs_ref, wga, wua, wda, wgb, wub, wdb, ys_ref):
    in_use = pl.program_id(0) < nv_ref[0]
    rows = xs_ref.shape[0] // SUBLANES

    @pl.when(in_use)
    def _():
        x = _load_token_tiles(xs_ref).astype(BF16)
        for slot, (wg, wu, wd) in enumerate(((wga, wua, wda), (wgb, wub, wdb))):
            hg = jnp.dot(x, wg[...].astype(BF16), preferred_element_type=F32)
            hu = jnp.dot(x, wu[...].astype(BF16), preferred_element_type=F32)
            a = (hg * jax.nn.sigmoid(hg) * hu).astype(BF16)
            y = jnp.dot(a, wd[...].astype(BF16), preferred_element_type=F32)
            for s in range(SUBLANES):
                ys_ref[pl.ds(slot * SUBLANES + s, rows, stride=TOKEN_ROWS_OUT), :] = (
                    y[:, s * LANES:(s + 1) * LANES])

    @pl.when(jnp.logical_not(in_use))
    def _():
        ys_ref[...] = jnp.zeros_like(ys_ref)


def _experts(tea, teb, nv, xs, wg, wu, wd):
    _, d, de = wg.shape
    assert d == SUBLANES * LANES
    n_tiles = xs.shape[0] // (MOE_TM * SUBLANES)
    wa = lambda shape: pl.BlockSpec(shape, lambda j, tea, teb, nv: (tea[j], 0, 0))
    wb = lambda shape: pl.BlockSpec(shape, lambda j, tea, teb, nv: (teb[j], 0, 0))
    grid_spec = pltpu.PrefetchScalarGridSpec(
        num_scalar_prefetch=3,
        grid=(n_tiles,),
        in_specs=[pl.BlockSpec((MOE_TM * SUBLANES, LANES),
                               lambda j, tea, teb, nv: (jnp.minimum(j, nv[0] - 1), 0)),
                  wa((None, d, de)), wa((None, d, de)), wa((None, de, d)),
                  wb((None, d, de)), wb((None, d, de)), wb((None, de, d))],
        out_specs=pl.BlockSpec((MOE_TM * TOKEN_ROWS_OUT, LANES), lambda j, *_: (j, 0)),
    )
    return pl.pallas_call(
        _expert_kernel,
        grid_spec=grid_spec,
        out_shape=jax.ShapeDtypeStruct((n_tiles * MOE_TM * TOKEN_ROWS_OUT, LANES), F32),
        compiler_params=_params(("arbitrary",)),
        name="experts",
    )(tea, teb, nv, xs, wg, wu, wd, wg, wu, wd)


def _combine_kernel(pos_ref, ys_ref, x1_ref, route_ref, fg_ref, o_ref, g, sem, *, tm):
    i = pl.program_id(0)
    slot = i % 2

    def gather(step, buf):
        base = step * tm

        def body(r2, carry):
            for k in range(2):
                r = 2 * r2 + k
                _token_copy(ys_ref, pos_ref[base + r], g.at[buf], r, sem.at[buf],
                            TOKEN_ROWS_OUT).start(priority=k)
            return carry

        lax.fori_loop(0, tm // 2, body, 0, unroll=DMA_UNROLL)

    @pl.when(i == 0)
    def _():
        gather(0, 0)

    @pl.when(i + 1 < pl.num_programs(0))
    def _():
        gather(i + 1, 1 - slot)

    pltpu.make_async_copy(ys_ref.at[pl.ds(0, tm * TOKEN_ROWS_OUT)], g.at[slot],
                          sem.at[slot]).wait()
    gs = g.at[slot]
    ya, yb = (jnp.concatenate([gs[pl.ds(off + s, tm, stride=TOKEN_ROWS_OUT), :]
                               for s in range(SUBLANES)], axis=1) for off in (0, SUBLANES))
    route = route_ref[...]
    x2 = x1_ref[...] + (route[:, ROUTE_WA:ROUTE_WA + 1] * ya + route[:, ROUTE_WB:ROUTE_WB + 1] * yb)
    var = jnp.mean(x2 * x2, axis=-1, keepdims=True)
    o_ref[...] = x2 * lax.rsqrt(var + EPS) * fg_ref[...]


def _combine(pos, ys, x1, route, final_g):
    n, d = x1.shape
    tm = min(COMBINE_TM, n)
    kern = functools.partial(_combine_kernel, tm=tm)
    grid_spec = pltpu.PrefetchScalarGridSpec(
        num_scalar_prefetch=1,
        grid=(n // tm,),
        in_specs=[pl.BlockSpec(memory_space=pl.ANY),
                  pl.BlockSpec((tm, d), lambda i, *_: (i, 0)),
                  pl.BlockSpec((tm, LANES), lambda i, *_: (i, 0)),
                  pl.BlockSpec((1, d), lambda i, *_: (0, 0))],
        out_specs=pl.BlockSpec((tm, d), lambda i, *_: (i, 0)),
        scratch_shapes=[pltpu.VMEM((2, tm * TOKEN_ROWS_OUT, LANES), F32),
                        pltpu.SemaphoreType.DMA((2,))],
    )
    return pl.pallas_call(
        kern,
        grid_spec=grid_spec,
        out_shape=jax.ShapeDtypeStruct((n, d), F32),
        compiler_params=_params(("arbitrary",)),
        name="combine",
    )(pos, ys, x1, route, final_g.reshape(1, d))


def _moe_sparse(t, route, tot, wg, wu, wd, x1, final_g):
    n = x1.shape[0]
    n_tiles_max = (n + N_CLASSES * (MOE_TM - 1)) // MOE_TM
    pos, te, meta = _plan(route, tot, n_tiles_max)
    pos = pos.reshape(n)
    xs = _dispatch(pos, meta[0, :N_CLASSES + 1], t, n_tiles_max * MOE_TM)
    ys = _experts(te[:n_tiles_max, 0], te[:n_tiles_max, 1], meta[0, N_CLASSES:N_CLASSES + 1],
                  xs, wg, wu, wd)
    return _combine(pos, ys, x1, route, final_g)


def _suffix_ones(t):
    j = np.arange(t)[:, None]
    s = np.arange(t)[None, :]
    return jnp.asarray((j > s).astype(np.float32), dtype=BF16)


def _causal_bias(t):
    row = np.arange(t)[:, None]
    col = np.arange(t)[None, :]
    diag = np.where(col < row, 0.0, MASK_BIAS).astype(np.float32)
    return jnp.asarray(np.stack([np.zeros_like(diag), diag]))


def _chunk_prefix_ones(tt, c):
    t = np.arange(tt)[:, None]
    j = np.arange(tt)[None, :]
    return jnp.asarray(((j <= t) & (t // c == j // c)).astype(np.float32), dtype=BF16)


def _block_diag_ones(w, blk):
    a = np.arange(w)
    return jnp.asarray((a[:, None] // blk == a[None, :] // blk).astype(np.float32), dtype=BF16)


def kernel(x, ln1_g, w_in, w_branch_sb, w_branch_hg, hg_norm_g, hg_lb_logits, w_out, ln2_g,
           w_router_group, b_router_group, w_router_expert, b_router_expert,
           w_exp_gate, w_exp_up, w_exp_down, final_g):
    bsz, seq, d = x.shape
    depth = w_in.shape[0]
    n = bsz * seq
    sb_width = SB_HEADS * SB_HEAD_DIM
    hg_width = HG_HEADS * HG_DIM

    tri_attn = _suffix_ones(min(ATTN_T, seq))
    bias_attn = _causal_bias(min(ATTN_T, seq))
    tt = min(HGRN_TT, seq)
    tri_hg = _chunk_prefix_ones(tt, min(HGRN_C, tt))
    bd = _block_diag_ones(min(MXU_DIM, hg_width), HG_DIM)

    x2 = x.reshape(n, d)
    for l in range(depth):
        qkv, rest = _inproj(x2, ln1_g[l], w_in[l], sb_width)
        y_sb = _attn(qkv, bsz, seq, tri_attn, bias_attn)
        y_hg = _hgrn(rest, hg_lb_logits, hg_norm_g[l], bsz, seq, l, tri_hg, bd)

        pad = LANES - N_EXPERTS - N_GROUPS
        wr = jnp.concatenate([w_router_expert[l], w_router_group[l],
                              jnp.zeros((d, pad), F32)], axis=1)
        wr_hi = wr.astype(BF16)
        wr_lo = (wr - wr_hi.astype(F32)).astype(BF16)
        wr_split = jnp.concatenate([wr_hi, wr_lo], axis=1)
        br = jnp.concatenate([b_router_expert[l], b_router_group[l],
                              jnp.zeros((pad,), F32)]).reshape(1, LANES)

        last = l == depth - 1
        x1, t, route, tot = _merge(x2, y_sb, y_hg, rest, w_branch_sb[l], w_branch_hg[l],
                                   w_out[l], ln2_g[l], wr_split, br)
        assert last, "final rmsnorm is fused into the last layer's combine kernel"
        x2 = _moe_sparse(t, route, tot, w_exp_gate[l], w_exp_up[l], w_exp_down[l], x1, final_g)
    return x2.reshape(bsz, seq, d)
```

```python
import functools

import jax
import jax.numpy as jnp
import numpy as np
from jax import lax
from jax.experimental import pallas as pl
from jax.experimental.pallas import tpu as pltpu

F32 = jnp.float32
BF16 = jnp.bfloat16

EPS = 1e-6
SB_HEADS = 8
SB_HEAD_DIM = 64
HG_HEADS = 8
HG_DIM = 64
N_GROUPS = 4
EXPERTS_PER_GROUP = 4
N_EXPERTS = N_GROUPS * EXPERTS_PER_GROUP

LANES = 128
SUBLANES = 8
MXU_DIM = 256
LOG2E = 1.4426950408889634
VMEM_LIMIT = 56 * 1024 * 1024

INPROJ_TM = 2048
INPROJ_TN = 512
ATTN_T = MXU_DIM
HGRN_TT = 256
HGRN_C = 32
MERGE_TM = 512
MOE_TM = 384
PLAN_TM = 1024
DISPATCH_TM = 1024
COMBINE_TM = 512


def _params(sem):
    return pltpu.CompilerParams(dimension_semantics=sem, vmem_limit_bytes=VMEM_LIMIT)


def _split_dot(x, m, passes):
    acc = None
    r = x
    for p in range(passes):
        h = r.astype(BF16)
        term = jnp.dot(h, m, preferred_element_type=F32)
        acc = term if acc is None else acc + term
        if p + 1 < passes:
            r = r - h.astype(F32)
    return acc


def _inproj_kernel(x_ref, g_ref, w_ref, qkv_ref, rest_ref, h_scr, w_scr, *, q_scale):
    i = pl.program_id(0)
    j = pl.program_id(1)

    @pl.when(j == 0)
    def _():
        x = x_ref[...]
        var = jnp.mean(x * x, axis=-1, keepdims=True)
        h_scr[...] = (x * lax.rsqrt(var + EPS) * g_ref[...]).astype(BF16)

    @pl.when(i == 0)
    def _():
        w_scr[j] = w_ref[...].astype(BF16)

    acc = jnp.dot(h_scr[...], w_scr[j], preferred_element_type=F32)
    qkv_ref[...] = (acc * jnp.where(j == 0, q_scale, 1.0)).astype(BF16)
    rest_ref[...] = acc


def _inproj(x2, ln_g, w_in, sb_width):
    n, d = x2.shape
    cols = w_in.shape[1]
    tm, tn = min(INPROJ_TM, n), INPROJ_TN
    assert sb_width == tn, "q block must be exactly one column tile"
    nq = 3 * sb_width // tn
    nj = cols // tn
    kern = functools.partial(_inproj_kernel, q_scale=SB_HEAD_DIM ** -0.5)
    return pl.pallas_call(
        kern,
        grid=(n // tm, nj),
        in_specs=[
            pl.BlockSpec((tm, d), lambda i, j: (i, 0)),
            pl.BlockSpec((1, d), lambda i, j: (0, 0)),
            pl.BlockSpec((d, tn), lambda i, j: (0, jnp.where(i == 0, j, nj - 1))),
        ],
        out_specs=[
            pl.BlockSpec((tm, tn), lambda i, j: (i, jnp.minimum(j, nq))),
            pl.BlockSpec((tm, tn), lambda i, j: (i, jnp.maximum(j - nq, 0))),
        ],
        out_shape=[
            jax.ShapeDtypeStruct((n, (nq + 1) * tn), BF16),
            jax.ShapeDtypeStruct((n, cols - nq * tn), F32),
        ],
        scratch_shapes=[pltpu.VMEM((tm, d), BF16), pltpu.VMEM((nj, d, tn), BF16)],
        compiler_params=_params(("arbitrary", "arbitrary")),
        name="inproj",
    )(x2, ln_g.reshape(1, d), w_in)


ATTN_STAGES = 3
ATTN_STREAMS = 2
MASK_BIAS = -1e30
ATTN_SKIP = 111.0


def _attn_kernel(q_ref, k_ref, v_ref, tri_ref, bias_ref, o_ref,
                 z0, z1, z2, i0, i1, i2, acc_ref, c_ref, *, t, nq):
    zbuf = (z0, z1, z2)
    ibuf = (i0, i1, i2)
    for r in zbuf + ibuf + (acc_ref, c_ref):
        r[...] = jnp.zeros_like(r)

    lane = lax.broadcasted_iota(jnp.int32, (t, LANES), 1)
    head0 = lane < SB_HEAD_DIM
    nt = (((1,), (1,)), ((), ()))

    def stage_a(qi, kj, slot):
        q = q_ref[pl.ds(pl.multiple_of(qi * t, t), t), :]
        zero = jnp.zeros_like(q)
        q2 = jnp.concatenate([jnp.where(head0, q, zero), jnp.where(head0, zero, q)], axis=0)
        k = k_ref[pl.ds(pl.multiple_of(kj * t, t), t), :]
        z = lax.dot_general(q2, k, nt, preferred_element_type=F32)
        bias = bias_ref[(qi == kj).astype(jnp.int32)]
        zbuf[slot][...] = z + jnp.concatenate([bias, bias], axis=0)

    def stage_b(slot):
        z = zbuf[slot][...]
        p = jnp.maximum(z, 0.0) + jnp.log(1.0 + jnp.exp2(jnp.abs(z) * (-LOG2E)))
        incl = jnp.dot(p.astype(BF16), tri_ref[...], preferred_element_type=F32) + p
        ibuf[slot][...] = incl
        return incl[:, 0:1]

    def stage_c(qi, kj, slot, stream):
        first = qi == kj
        incl = ibuf[slot][...]
        c = jnp.where(first, 0.0, c_ref[stream])
        a = jnp.exp2((zbuf[slot][...] - incl - c) * LOG2E)
        v = v_ref[pl.ds(pl.multiple_of(kj * t, t), t), :]
        pv = jnp.dot(a.astype(BF16), v, preferred_element_type=F32)
        acc = jnp.where(first, pv, acc_ref[stream] + pv)
        acc_ref[stream] = acc
        c_ref[stream] = c + incl[:, 0:1]
        o_ref[pl.ds(pl.multiple_of(qi * t, t), t), :] = (
            jnp.where(head0, acc[0:t], acc[t:2 * t]).astype(o_ref.dtype))

    def block(carry):
        qs, ks, qn, kn, skip_q, drained = carry
        qs, ks, qn, kn, skip_q = list(qs), list(ks), list(qn), list(kn), list(skip_q)
        all_done = qn[0] >= nq
        for p in range(1, ATTN_STREAMS):
            all_done = jnp.logical_and(all_done, qn[p] >= nq)
        drained = drained + all_done.astype(jnp.int32)
        for r in range(ATTN_STAGES * ATTN_STREAMS):
            sa, sc, sb = r % ATTN_STAGES, (r + 1) % ATTN_STAGES, (r + 2) % ATTN_STAGES
            pa, pb = r % ATTN_STREAMS, (r - 1) % ATTN_STREAMS
            stage_c(qs[sc], ks[sc], sc, pa)
            total_b = stage_b(sb)
            carry_b = jnp.where(qs[sb] == ks[sb], 0.0, c_ref[pb]) + total_b
            skip = skip_q[pa] == qn[pa]
            q_cur = jnp.where(skip, qn[pa] + ATTN_STREAMS, qn[pa])
            k_cur = jnp.where(skip, qn[pa] + ATTN_STREAMS, kn[pa])
            drain = q_cur >= nq
            qa = jnp.where(drain, 0, q_cur)
            ka = jnp.where(drain, 0, k_cur)
            stage_a(qa, ka, sa)
            last = k_cur == 0
            qn[pa] = jnp.where(jnp.logical_and(last, jnp.logical_not(drain)),
                               q_cur + ATTN_STREAMS, q_cur)
            kn[pa] = jnp.where(drain, k_cur, jnp.where(last, q_cur + ATTN_STREAMS, k_cur - 1))
            skip_q[pb] = jnp.where(jnp.min(carry_b) >= ATTN_SKIP, qs[sb], skip_q[pb])
            qs[sa], ks[sa] = qa, ka
        return tuple(qs), tuple(ks), tuple(qn), tuple(kn), tuple(skip_q), drained

    zero = jnp.int32(0)
    first_q = tuple(jnp.int32(p) for p in range(ATTN_STREAMS))
    init = ((zero,) * ATTN_STAGES, (zero,) * ATTN_STAGES, first_q, first_q,
            (jnp.int32(-1),) * ATTN_STREAMS, zero)
    lax.while_loop(lambda carry: carry[5] < 1, block, init)


def _attn(qkv, bsz, seq, tri, bias):
    n = bsz * seq
    t = min(ATTN_T, seq)
    pairs = SB_HEADS * SB_HEAD_DIM // LANES
    kern = functools.partial(_attn_kernel, t=t, nq=seq // t)
    return pl.pallas_call(
        kern,
        grid=(bsz, pairs),
        in_specs=[
            pl.BlockSpec((seq, LANES), lambda b, p: (b, p)),
            pl.BlockSpec((seq, LANES), lambda b, p: (b, pairs + p)),
            pl.BlockSpec((seq, LANES), lambda b, p: (b, 2 * pairs + p)),
            pl.BlockSpec((t, t), lambda b, p: (0, 0)),
            pl.BlockSpec((2, t, t), lambda b, p: (0, 0, 0)),
        ],
        out_specs=pl.BlockSpec((seq, LANES), lambda b, p: (b, p)),
        out_shape=jax.ShapeDtypeStruct((n, pairs * LANES), BF16),
        scratch_shapes=([pltpu.VMEM((2 * t, t), F32)] * (2 * ATTN_STAGES)
                        + [pltpu.VMEM((ATTN_STREAMS, 2 * t, LANES), F32),
                           pltpu.VMEM((ATTN_STREAMS, 2 * t, 1), F32)]),
        compiler_params=_params(("parallel", "parallel")),
        name="attn",
    )(qkv, qkv, qkv, tri, bias)


def _group_dot(x, bd):
    g = bd.shape[0]
    parts = [jnp.dot(x[:, i:i + g], bd, preferred_element_type=F32)
             for i in range(0, x.shape[1], g)]
    return jnp.concatenate(parts, axis=1)


def _hgrn_kernel(q_ref, f_ref, i_ref, g_ref, lbl_ref, ng_ref, tri_ref, bd_ref, o_ref,
                 st_scr, wpad, vpad, wsh, vsh, *, tt, c, layer):
    ti = pl.program_id(1)
    w = q_ref.shape[1]
    nch = tt // c

    @pl.when(ti == 0)
    def _():
        st_scr[...] = jnp.zeros_like(st_scr)

    lg = lbl_ref[...]
    e = jnp.exp(lg - jnp.max(lg, axis=0, keepdims=True))
    lb = jnp.sum(e[0:layer + 1], axis=0, keepdims=True) / jnp.sum(e, axis=0, keepdims=True)

    f = lb + (1.0 - lb) * jax.nn.sigmoid(f_ref[...])
    kk = 1.0 - f
    qv = q_ref[...]
    qs = qv * jax.nn.sigmoid(qv)
    v = i_ref[...]
    bd = bd_ref[...]
    cum = _split_dot_left(tri_ref[...], jnp.log(f) * LOG2E, 3)
    wk = jnp.log(kk) * LOG2E - cum

    pos = lax.broadcasted_iota(jnp.int32, (tt, 1), 0) % c
    wpad[0:SUBLANES, :] = jnp.zeros((SUBLANES, w), F32)
    vpad[0:SUBLANES, :] = jnp.zeros((SUBLANES, w), F32)
    wpad[SUBLANES:SUBLANES + tt, :] = wk
    vpad[SUBLANES:SUBLANES + tt, :] = v
    for b in range(SUBLANES):
        wb = wpad[SUBLANES - b:SUBLANES - b + tt, :]
        vb = vpad[SUBLANES - b:SUBLANES - b + tt, :]
        if b:
            wb = jnp.where(pos >= b, wb, -jnp.inf)
        wsh[b] = wb.reshape(nch, c, w)
        vsh[b] = vb.reshape(nch, c, w)

    qs3 = qs.reshape(nch, c, w)
    cum3 = cum.reshape(nch, c, w)
    acc = None
    for a in reversed(range(c // SUBLANES)):
        rows = c - SUBLANES * a
        qa = qs3[:, SUBLANES * a:, :].reshape(nch * rows, w)
        ca = cum3[:, SUBLANES * a:, :].reshape(nch * rows, w)
        acc_a = None
        for b in range(SUBLANES):
            wb = wsh[b, :, 0:rows, :].reshape(nch * rows, w)
            vb = vsh[b, :, 0:rows, :].reshape(nch * rows, w)
            dd = qa * jnp.exp2(ca + wb)
            term = _group_dot(dd.astype(BF16), bd) * vb
            acc_a = term if acc_a is None else acc_a + term
        acc_a = acc_a.reshape(nch, rows, w)
        if acc is not None:
            acc_a = acc_a + jnp.concatenate([jnp.zeros((nch, SUBLANES, w), F32), acc], axis=1)
        acc = acc_a
    acc = acc.reshape(tt, w)

    grp = st_scr.shape[1]
    bdmask = bd[0:grp, 0:grp] != 0
    outs = []
    for ci in range(nch):
        r0 = ci * c
        cum_c = cum[r0:r0 + c]
        last = cum_c[c - 1:c]
        qd = (qs[r0:r0 + c] * jnp.exp2(cum_c)).astype(BF16)
        kd = (kk[r0:r0 + c] * jnp.exp2(last - cum_c)).astype(BF16)
        vc = v[r0:r0 + c].astype(BF16)
        dec = jnp.exp2(last)
        o_parts = []
        for gi in range(w // grp):
            sl = slice(gi * grp, (gi + 1) * grp)
            st = st_scr[gi]
            o_parts.append(lax.dot_general(qd[:, sl], st.astype(BF16), (((1,), (1,)), ((), ())),
                                           preferred_element_type=F32))
            upd = lax.dot_general(vc[:, sl], kd[:, sl], (((0,), (0,)), ((), ())),
                                  preferred_element_type=F32)
            st_scr[gi] = st * dec[:, sl] + jnp.where(bdmask, upd, 0.0)
        outs.append(jnp.concatenate(o_parts, axis=1))
    o = acc + jnp.concatenate(outs, axis=0)

    o2 = o * o
    o2_hi = o2.astype(BF16)
    o2_lo = (o2 - o2_hi.astype(F32)).astype(BF16)
    ms = (_group_dot(o2_hi, bd) + _group_dot(o2_lo, bd)) * (1.0 / HG_DIM)
    gv = g_ref[...]
    o = o * lax.rsqrt(ms + EPS) * ng_ref[...] * (gv * jax.nn.sigmoid(gv))
    o_ref[...] = o.astype(o_ref.dtype)


def _split_dot_left(m, x, passes):
    acc = None
    r = x
    for p in range(passes):
        h = r.astype(BF16)
        term = jnp.dot(m, h, preferred_element_type=F32)
        acc = term if acc is None else acc + term
        if p + 1 < passes:
            r = r - h.astype(F32)
    return acc


def _hgrn(rest, lb_logits, norm_g, bsz, seq, layer, tri, bd):
    n = bsz * seq
    w = HG_HEADS * HG_DIM
    grp = bd.shape[0]
    tt = min(HGRN_TT, seq)
    c = min(HGRN_C, tt)
    nt = seq // tt
    kern = functools.partial(_hgrn_kernel, tt=tt, c=c, layer=layer)
    col = lambda j: pl.BlockSpec((tt, w), lambda b, i, j=j: (b * nt + i, j))
    const = lambda shape: pl.BlockSpec(shape, lambda b, i: (0, 0))
    return pl.pallas_call(
        kern,
        grid=(bsz, nt),
        in_specs=[col(0), col(1), col(2), col(3),
                  const(lb_logits.shape), const((1, w)), const((tt, tt)), const((grp, grp))],
        out_specs=pl.BlockSpec((tt, w), lambda b, i: (b * nt + i, 0)),
        out_shape=jax.ShapeDtypeStruct((n, w), BF16),
        scratch_shapes=([pltpu.VMEM((w // LANES, LANES, LANES), F32)]
                        + [pltpu.VMEM((SUBLANES + tt, w), F32)] * 2
                        + [pltpu.VMEM((SUBLANES, tt // c, c, w), F32)] * 2),
        compiler_params=_params(("parallel", "arbitrary")),
        name="hgrn",
    )(rest, rest, rest, rest, lb_logits, norm_g.reshape(1, w), tri, bd)


PAIR_A = (0, 0, 0, 1, 1, 3)
PAIR_B = (1, 2, 3, 3, 2, 2)
N_PAIRS = len(PAIR_A)
N_CLASSES = N_GROUPS * N_PAIRS
assert sorted(tuple(sorted(p)) for p in zip(PAIR_A, PAIR_B)) == [
    (a, b) for a in range(EXPERTS_PER_GROUP) for b in range(a + 1, EXPERTS_PER_GROUP)]
ROUTE_CLS, ROUTE_WA, ROUTE_WB = 0, 1, 2


def _pair_slots(pidx):
    a = b = jnp.zeros_like(pidx)
    for p in range(N_PAIRS):
        a = jnp.where(pidx == p, float(PAIR_A[p]), a)
        b = jnp.where(pidx == p, float(PAIR_B[p]), b)
    return a, b


def _pair_index(lo, hi):
    pidx = jnp.zeros_like(lo)
    for p in range(N_PAIRS):
        is_p = jnp.logical_and(lo == min(PAIR_A[p], PAIR_B[p]), hi == max(PAIR_A[p], PAIR_B[p]))
        pidx = jnp.where(is_p, float(p), pidx)
    return pidx


def _route(lg):
    lane = lax.broadcasted_iota(jnp.int32, lg.shape, 1)
    neg = jnp.float32(-jnp.inf)
    big = jnp.int32(LANES)
    gmask = jnp.logical_and(lane >= N_EXPERTS, lane < N_EXPERTS + N_GROUPS)
    gl = jnp.where(gmask, lg, neg)
    gmax = jnp.max(gl, axis=1, keepdims=True)
    gidx = jnp.min(jnp.where(gl == gmax, lane, big), axis=1, keepdims=True) - N_EXPERTS
    w_grp = 1.0 / jnp.sum(jnp.where(gmask, jnp.exp(gl - gmax), 0.0), axis=1, keepdims=True)
    in_grp = jnp.logical_and(lane < N_EXPERTS, lane // EXPERTS_PER_GROUP == gidx)
    l1 = jnp.where(in_grp, lg, neg)
    v1 = jnp.max(l1, axis=1, keepdims=True)
    i1 = jnp.min(jnp.where(l1 == v1, lane, big), axis=1, keepdims=True)
    l2 = jnp.where(jnp.logical_and(in_grp, lane != i1), lg, neg)
    v2 = jnp.max(l2, axis=1, keepdims=True)
    i2 = jnp.min(jnp.where(l2 == v2, lane, big), axis=1, keepdims=True)
    e2 = jnp.exp(v2 - v1)
    p1 = 1.0 / (1.0 + e2)
    p2 = e2 * p1
    loc1 = (i1 - gidx * EXPERTS_PER_GROUP).astype(F32)
    loc2 = (i2 - gidx * EXPERTS_PER_GROUP).astype(F32)
    pidx = _pair_index(jnp.minimum(loc1, loc2), jnp.maximum(loc1, loc2))
    cls = gidx.astype(F32) * N_PAIRS + pidx
    a_loc, _ = _pair_slots(pidx)
    first_is_a = loc1 == a_loc
    wa = jnp.where(first_is_a, p1, p2) * w_grp
    wb = jnp.where(first_is_a, p2, p1) * w_grp
    record = jnp.where(lane == ROUTE_CLS, cls,
                       jnp.where(lane == ROUTE_WA, wa, jnp.where(lane == ROUTE_WB, wb, 0.0)))
    return record, cls


def _store_token_tiles(ref, x):
    rows, d = x.shape
    assert d == SUBLANES * LANES
    for s in range(SUBLANES):
        ref[pl.ds(s, rows, stride=SUBLANES), :] = x[:, s * LANES:(s + 1) * LANES]


def _load_token_tiles(ref):
    rows = ref.shape[0] // SUBLANES
    return jnp.concatenate([ref[pl.ds(s, rows, stride=SUBLANES), :] for s in range(SUBLANES)],
                           axis=1)


def _merge_kernel(x_ref, ysb_ref, yhg_ref, gsb_ref, ghg_ref, wbs_ref, wbh_ref, wo_ref,
                  ln_ref, wr_ref, br_ref, x1_ref, t_ref, route_ref, tot_ref,
                  wbs_scr, wbh_scr, wo_scr):
    @pl.when(pl.program_id(0) == 0)
    def _():
        wbs_scr[...] = wbs_ref[...].astype(BF16)
        wbh_scr[...] = wbh_ref[...].astype(BF16)
        wo_scr[...] = wo_ref[...].astype(BF16)
        tot_ref[...] = jnp.zeros_like(tot_ref)

    a = jnp.dot(ysb_ref[...], wbs_scr[...], preferred_element_type=F32)
    b = jnp.dot(yhg_ref[...], wbh_scr[...], preferred_element_type=F32)
    merged = jax.nn.sigmoid(gsb_ref[...]) * a + jax.nn.sigmoid(ghg_ref[...]) * b
    x1 = x_ref[...] + jnp.dot(merged.astype(BF16), wo_scr[...], preferred_element_type=F32)
    x1_ref[...] = x1
    var = jnp.mean(x1 * x1, axis=-1, keepdims=True)
    t = x1 * lax.rsqrt(var + EPS) * ln_ref[...]
    _store_token_tiles(t_ref, t)
    t_hi = t.astype(BF16)
    t_lo = (t - t_hi.astype(F32)).astype(BF16)
    p_hi = jnp.dot(t_hi, wr_ref[...], preferred_element_type=F32)
    p_lo = jnp.dot(t_lo, wr_ref[...], preferred_element_type=F32)
    lg = (p_hi[:, :LANES] + p_hi[:, LANES:]) + (p_lo[:, :LANES] + p_lo[:, LANES:]) + br_ref[...]
    route, cls = _route(lg)
    route_ref[...] = route
    lane = lax.broadcasted_iota(jnp.int32, route.shape, 1)
    sel = jnp.where(lane.astype(F32) == cls, 1.0, 0.0)
    tot_ref[...] += jnp.broadcast_to(jnp.sum(sel, axis=0, keepdims=True), tot_ref.shape)


def _merge(x2, y_sb, y_hg, rest, wbs, wbh, wo, ln_g, wr, br):
    n, d = x2.shape
    tm = min(MERGE_TM, n)
    w_sb, w_hg = y_sb.shape[1], y_hg.shape[1]
    gate_blk = (rest.shape[1] - 2 * d) // d
    row = lambda wdt, j=0: pl.BlockSpec((tm, wdt), lambda i, j=j: (i, j))
    const = lambda shape: pl.BlockSpec(shape, lambda i: (0, 0))
    return pl.pallas_call(
        _merge_kernel,
        grid=(n // tm,),
        in_specs=[row(d), row(w_sb), row(w_hg), row(d, gate_blk), row(d, gate_blk + 1),
                  const(wbs.shape), const(wbh.shape), const(wo.shape), const((1, d)),
                  const(wr.shape), const((1, LANES))],
        out_specs=[row(d), pl.BlockSpec((tm * SUBLANES, LANES), lambda i: (i, 0)), row(LANES),
                   const((SUBLANES, LANES))],
        out_shape=[jax.ShapeDtypeStruct((n, d), F32),
                   jax.ShapeDtypeStruct((n * SUBLANES, LANES), F32),
                   jax.ShapeDtypeStruct((n, LANES), F32),
                   jax.ShapeDtypeStruct((SUBLANES, LANES), F32)],
        scratch_shapes=[pltpu.VMEM(w.shape, BF16) for w in (wbs, wbh, wo)],
        compiler_params=_params(("arbitrary",)),
        name="merge",
    )(x2, y_sb, y_hg, rest, rest, wbs, wbh, wo, ln_g.reshape(1, d), wr, br)


TOKEN_ROWS_OUT = 2 * SUBLANES


def _plan_kernel(route_ref, tot_ref, ltri_ref, utri_ref, pos_ref, te_ref, meta_ref,
                 run_scr, *, tile_rows):
    i = pl.program_id(0)
    route = route_ref[...]
    tm = route.shape[0]
    lane = lax.broadcasted_iota(jnp.int32, (tm, LANES), 1)
    is_cls = lane.astype(F32) == route[:, ROUTE_CLS:ROUTE_CLS + 1]
    sel = jnp.where(is_cls, 1.0, 0.0)

    tot = tot_ref[0:1, :]
    top = tot + (tile_rows - 1.0)
    tiles = jnp.floor(top * (1.0 / tile_rows))
    tiles = jnp.where(tiles * tile_rows > top, tiles - 1.0, tiles)
    tiles = jnp.where((tiles + 1.0) * tile_rows <= top, tiles + 1.0, tiles)
    first_tile = _split_dot(jnp.broadcast_to(tiles, (SUBLANES, LANES)), utri_ref[...], 3)[0:1]

    @pl.when(i == 0)
    def _():
        run_scr[...] = jnp.zeros_like(run_scr)
        lane1 = lax.broadcasted_iota(jnp.int32, (1, LANES), 1)
        is_c = lane1 < N_CLASSES
        end_tile = first_tile + tiles
        n_valid = jnp.sum(jnp.where(is_c, tiles, 0.0), axis=1, keepdims=True)
        rows = te_ref.shape[0]
        j = lax.broadcasted_iota(jnp.int32, (rows, LANES), 0).astype(F32)
        j = jnp.minimum(j, n_valid - 1.0)
        lane2 = lax.broadcasted_iota(jnp.int32, (rows, LANES), 1)
        done = jnp.where(jnp.logical_and(lane2 < N_CLASSES, end_tile <= j), 1.0, 0.0)
        tc = jnp.sum(done, axis=1, keepdims=True)
        tg = jnp.floor((tc + 0.5) * (1.0 / N_PAIRS))
        ta, tb = _pair_slots(tc - tg * N_PAIRS)
        te = jnp.where(lane2 == 0, tg * EXPERTS_PER_GROUP + ta, tg * EXPERTS_PER_GROUP + tb)
        te_ref[...] = te.astype(jnp.int32)
        last_tile = jnp.where(jnp.logical_and(is_c, tiles > 0), end_tile - 1.0, -1.0)
        meta = jnp.where(lane1 == N_CLASSES, n_valid, last_tile)
        meta_ref[...] = jnp.broadcast_to(meta, meta_ref.shape).astype(jnp.int32)

    rank = jnp.dot(ltri_ref[...], sel.astype(BF16), preferred_element_type=F32) + run_scr[...]
    dest = jnp.sum(jnp.where(is_cls, first_tile * tile_rows + rank, 0.0), axis=1, keepdims=True)
    row = lax.broadcasted_iota(jnp.int32, (tm, LANES), 0)
    spread = jnp.where(lane == row % LANES, dest, 0.0)
    pos_ref[...] = jnp.sum(spread.reshape(tm // LANES, LANES, LANES), axis=1).astype(jnp.int32)
    run_scr[...] += jnp.sum(sel, axis=0, keepdims=True)


def _plan(route, tot, n_tiles_max):
    n = route.shape[0]
    tm = min(PLAN_TM, n)
    r = np.arange(tm)
    ltri = jnp.asarray((r[None, :] < r[:, None]).astype(np.float32), dtype=BF16)
    e = np.arange(LANES)
    utri = jnp.asarray((e[:, None] < e[None, :]).astype(np.float32), dtype=BF16)
    te_rows = -(-n_tiles_max // SUBLANES) * SUBLANES
    kern = functools.partial(_plan_kernel, tile_rows=MOE_TM)
    return pl.pallas_call(
        kern,
        grid=(n // tm,),
        in_specs=[pl.BlockSpec((tm, LANES), lambda i: (i, 0)),
                  pl.BlockSpec((SUBLANES, LANES), lambda i: (0, 0)),
                  pl.BlockSpec((tm, tm), lambda i: (0, 0)),
                  pl.BlockSpec((LANES, LANES), lambda i: (0, 0))],
        out_specs=[pl.BlockSpec((tm // LANES, LANES), lambda i: (i, 0)),
                   pl.BlockSpec((te_rows, LANES), lambda i: (0, 0)),
                   pl.BlockSpec((SUBLANES, LANES), lambda i: (0, 0))],
        out_shape=[jax.ShapeDtypeStruct((n // LANES, LANES), jnp.int32),
                   jax.ShapeDtypeStruct((te_rows, LANES), jnp.int32),
                   jax.ShapeDtypeStruct((SUBLANES, LANES), jnp.int32)],
        scratch_shapes=[pltpu.VMEM((1, LANES), F32)],
        compiler_params=_params(("arbitrary",)),
        name="plan",
    )(route, tot, ltri, utri)


DMA_UNROLL = 8


def _token_copy(src_ref, r, dst_ref, p, sem, rows):
    return pltpu.make_async_copy(src_ref.at[pl.ds(pl.multiple_of(r * rows, rows), rows)],
                                 dst_ref.at[pl.ds(pl.multiple_of(p * rows, rows), rows)], sem)


def _dispatch_kernel(pos_ref, meta_ref, t_ref, xs_ref, zero_scr, sem, tail_sem, *, tm, tile_rows):
    i = pl.program_id(0)
    tile_rows = tile_rows * SUBLANES
    n_tiles = xs_ref.shape[0] // tile_rows
    n_valid = meta_ref[N_CLASSES]

    def clear(tile, s):
        return pltpu.make_async_copy(zero_scr, xs_ref.at[pl.ds(tile * tile_rows, tile_rows)], s)

    def for_each(clears, fn):
        for tile, cond in clears:
            @pl.when(cond)
            def _():
                fn(tile)

    last_tiles = [(meta_ref[c], meta_ref[c] >= 0) for c in range(N_CLASSES)]
    past_tiles = [(n_tiles - 1 - k, n_tiles - 1 - k >= n_valid) for k in range(N_CLASSES)]

    @pl.when(i == 0)
    def _():
        zero_scr[...] = jnp.zeros_like(zero_scr)
        for_each(last_tiles, lambda tile: clear(tile, sem).start())
        for_each(past_tiles, lambda tile: clear(tile, tail_sem).start())
        for_each(last_tiles, lambda tile: clear(tile, sem).wait())

    base = i * tm

    def start(r2, carry):
        for k in range(2):
            r = 2 * r2 + k
            _token_copy(t_ref, r, xs_ref, pos_ref[base + r], sem, SUBLANES).start(priority=k)
        return carry

    lax.fori_loop(0, tm // 2, start, 0, unroll=DMA_UNROLL)
    pltpu.make_async_copy(t_ref, xs_ref.at[pl.ds(0, tm * SUBLANES)], sem).wait()

    @pl.when(i == pl.num_programs(0) - 1)
    def _():
        for_each(past_tiles, lambda tile: clear(tile, tail_sem).wait())


def _dispatch(pos, meta, t, n_rows):
    n = t.shape[0] // SUBLANES
    tm = min(DISPATCH_TM, n)
    kern = functools.partial(_dispatch_kernel, tm=tm, tile_rows=MOE_TM)
    grid_spec = pltpu.PrefetchScalarGridSpec(
        num_scalar_prefetch=2,
        grid=(n // tm,),
        in_specs=[pl.BlockSpec((tm * SUBLANES, LANES), lambda i, *_: (i, 0))],
        out_specs=pl.BlockSpec(memory_space=pl.ANY),
        scratch_shapes=[pltpu.VMEM((MOE_TM * SUBLANES, LANES), F32),
                        pltpu.SemaphoreType.DMA(()), pltpu.SemaphoreType.DMA(())],
    )
    return pl.pallas_call(
        kern,
        grid_spec=grid_spec,
        out_shape=jax.ShapeDtypeStruct((n_rows * SUBLANES, LANES), F32),
        compiler_params=_params(("arbitrary",)),
        name="dispatch",
    )(pos, meta, t)


def _expert_kernel(tea_ref, teb_ref, nv_ref, xs_ref, wga, wua, wda, wgb, wub, wdb, ys_ref):
    in_use = pl.program_id(0) < nv_ref[0]
    rows = xs_ref.shape[0] // SUBLANES

    @pl.when(in_use)
    def _():
        x = _load_token_tiles(xs_ref).astype(BF16)
        for slot, (wg, wu, wd) in enumerate(((wga, wua, wda), (wgb, wub, wdb))):
            hg = jnp.dot(x, wg[...].astype(BF16), preferred_element_type=F32)
            hu = jnp.dot(x, wu[...].astype(BF16), preferred_element_type=F32)
            a = (hg * jax.nn.sigmoid(hg) * hu).astype(BF16)
            y = jnp.dot(a, wd[...].astype(BF16), preferred_element_type=F32)
            for s in range(SUBLANES):
                ys_ref[pl.ds(slot * SUBLANES + s, rows, stride=TOKEN_ROWS_OUT), :] = (
                    y[:, s * LANES:(s + 1) * LANES])

    @pl.when(jnp.logical_not(in_use))
    def _():
        ys_ref[...] = jnp.zeros_like(ys_ref)


def _experts(tea, teb, nv, xs, wg, wu, wd):
    _, d, de = wg.shape
    assert d == SUBLANES * LANES
    n_tiles = xs.shape[0] // (MOE_TM * SUBLANES)
    wa = lambda shape: pl.BlockSpec(shape, lambda j, tea, teb, nv: (tea[j], 0, 0))
    wb = lambda shape: pl.BlockSpec(shape, lambda j, tea, teb, nv: (teb[j], 0, 0))
    grid_spec = pltpu.PrefetchScalarGridSpec(
        num_scalar_prefetch=3,
        grid=(n_tiles,),
        in_specs=[pl.BlockSpec((MOE_TM * SUBLANES, LANES),
                               lambda j, tea, teb, nv: (jnp.minimum(j, nv[0] - 1), 0)),
                  wa((None, d, de)), wa((None, d, de)), wa((None, de, d)),
                  wb((None, d, de)), wb((None, d, de)), wb((None, de, d))],
        out_specs=pl.BlockSpec((MOE_TM * TOKEN_ROWS_OUT, LANES), lambda j, *_: (j, 0)),
    )
    return pl.pallas_call(
        _expert_kernel,
        grid_spec=grid_spec,
        out_shape=jax.ShapeDtypeStruct((n_tiles * MOE_TM * TOKEN_ROWS_OUT, LANES), F32),
        compiler_params=_params(("arbitrary",)),
        name="experts",
    )(tea, teb, nv, xs, wg, wu, wd, wg, wu, wd)


def _combine_kernel(pos_ref, ys_ref, x1_ref, route_ref, fg_ref, o_ref, g, sem, *, tm):
    i = pl.program_id(0)
    slot = i % 2

    def gather(step, buf):
        base = step * tm

        def body(r2, carry):
            for k in range(2):
                r = 2 * r2 + k
                _token_copy(ys_ref, pos_ref[base + r], g.at[buf], r, sem.at[buf],
                            TOKEN_ROWS_OUT).start(priority=k)
            return carry

        lax.fori_loop(0, tm // 2, body, 0, unroll=DMA_UNROLL)

    @pl.when(i == 0)
    def _():
        gather(0, 0)

    @pl.when(i + 1 < pl.num_programs(0))
    def _():
        gather(i + 1, 1 - slot)

    pltpu.make_async_copy(ys_ref.at[pl.ds(0, tm * TOKEN_ROWS_OUT)], g.at[slot],
                          sem.at[slot]).wait()
    gs = g.at[slot]
    ya, yb = (jnp.concatenate([gs[pl.ds(off + s, tm, stride=TOKEN_ROWS_OUT), :]
                               for s in range(SUBLANES)], axis=1) for off in (0, SUBLANES))
    route = route_ref[...]
    x2 = x1_ref[...] + (route[:, ROUTE_WA:ROUTE_WA + 1] * ya + route[:, ROUTE_WB:ROUTE_WB + 1] * yb)
    var = jnp.mean(x2 * x2, axis=-1, keepdims=True)
    o_ref[...] = x2 * lax.rsqrt(var + EPS) * fg_ref[...]


def _combine(pos, ys, x1, route, final_g):
    n, d = x1.shape
    tm = min(COMBINE_TM, n)
    kern = functools.partial(_combine_kernel, tm=tm)
    grid_spec = pltpu.PrefetchScalarGridSpec(
        num_scalar_prefetch=1,
        grid=(n // tm,),
        in_specs=[pl.BlockSpec(memory_space=pl.ANY),
                  pl.BlockSpec((tm, d), lambda i, *_: (i, 0)),
                  pl.BlockSpec((tm, LANES), lambda i, *_: (i, 0)),
                  pl.BlockSpec((1, d), lambda i, *_: (0, 0))],
        out_specs=pl.BlockSpec((tm, d), lambda i, *_: (i, 0)),
        scratch_shapes=[pltpu.VMEM((2, tm * TOKEN_ROWS_OUT, LANES), F32),
                        pltpu.SemaphoreType.DMA((2,))],
    )
    return pl.pallas_call(
        kern,
        grid_spec=grid_spec,
        out_shape=jax.ShapeDtypeStruct((n, d), F32),
        compiler_params=_params(("arbitrary",)),
        name="combine",
    )(pos, ys, x1, route, final_g.reshape(1, d))


def _moe_sparse(t, route, tot, wg, wu, wd, x1, final_g):
    n = x1.shape[0]
    n_tiles_max = (n + N_CLASSES * (MOE_TM - 1)) // MOE_TM
    pos, te, meta = _plan(route, tot, n_tiles_max)
    pos = pos.reshape(n)
    xs = _dispatch(pos, meta[0, :N_CLASSES + 1], t, n_tiles_max * MOE_TM)
    ys = _experts(te[:n_tiles_max, 0], te[:n_tiles_max, 1], meta[0, N_CLASSES:N_CLASSES + 1],
                  xs, wg, wu, wd)
    return _combine(pos, ys, x1, route, final_g)


def _suffix_ones(t):
    j = np.arange(t)[:, None]
    s = np.arange(t)[None, :]
    return jnp.asarray((j > s).astype(np.float32), dtype=BF16)


def _causal_bias(t):
    row = np.arange(t)[:, None]
    col = np.arange(t)[None, :]
    diag = np.where(col < row, 0.0, MASK_BIAS).astype(np.float32)
    return jnp.asarray(np.stack([np.zeros_like(diag), diag]))


def _chunk_prefix_ones(tt, c):
    t = np.arange(tt)[:, None]
    j = np.arange(tt)[None, :]
    return jnp.asarray(((j <= t) & (t // c == j // c)).astype(np.float32), dtype=BF16)


def _block_diag_ones(w, blk):
    a = np.arange(w)
    return jnp.asarray((a[:, None] // blk == a[None, :] // blk).astype(np.float32), dtype=BF16)


def kernel(x, ln1_g, w_in, w_branch_sb, w_branch_hg, hg_norm_g, hg_lb_logits, w_out, ln2_g,
           w_router_group, b_router_group, w_router_expert, b_router_expert,
           w_exp_gate, w_exp_up, w_exp_down, final_g):
    bsz, seq, d = x.shape
    depth = w_in.shape[0]
    n = bsz * seq
    sb_width = SB_HEADS * SB_HEAD_DIM
    hg_width = HG_HEADS * HG_DIM

    tri_attn = _suffix_ones(min(ATTN_T, seq))
    bias_attn = _causal_bias(min(ATTN_T, seq))
    tt = min(HGRN_TT, seq)
    tri_hg = _chunk_prefix_ones(tt, min(HGRN_C, tt))
    bd = _block_diag_ones(min(MXU_DIM, hg_width), HG_DIM)

    x2 = x.reshape(n, d)
    for l in range(depth):
        qkv, rest = _inproj(x2, ln1_g[l], w_in[l], sb_width)
        y_sb = _attn(qkv, bsz, seq, tri_attn, bias_attn)
        y_hg = _hgrn(rest, hg_lb_logits, hg_norm_g[l], bsz, seq, l, tri_hg, bd)

        pad = LANES - N_EXPERTS - N_GROUPS
        wr = jnp.concatenate([w_router_expert[l], w_router_group[l],
                              jnp.zeros((d, pad), F32)], axis=1)
        wr_hi = wr.astype(BF16)
        wr_lo = (wr - wr_hi.astype(F32)).astype(BF16)
        wr_split = jnp.concatenate([wr_hi, wr_lo], axis=1)
        br = jnp.concatenate([b_router_expert[l], b_router_group[l],
                              jnp.zeros((pad,), F32)]).reshape(1, LANES)

        last = l == depth - 1
        x1, t, route, tot = _merge(x2, y_sb, y_hg, rest, w_branch_sb[l], w_branch_hg[l],
                                   w_out[l], ln2_g[l], wr_split, br)
        assert last, "final rmsnorm is fused into the last layer's combine kernel"
        x2 = _moe_sparse(t, route, tot, w_exp_gate[l], w_exp_up[l], w_exp_down[l], x1, final_g)
    return x2.reshape(bsz, seq, d)
```

```python
import functools

import jax
import jax.numpy as jnp
import numpy as np
from jax import lax
from jax.experimental import pallas as pl
from jax.experimental.pallas import tpu as pltpu

F32 = jnp.float32
BF16 = jnp.bfloat16

EPS = 1e-6
SB_HEADS = 8
SB_HEAD_DIM = 64
HG_HEADS = 8
HG_DIM = 64
N_GROUPS = 4
EXPERTS_PER_GROUP = 4
N_EXPERTS = N_GROUPS * EXPERTS_PER_GROUP

LANES = 128
SUBLANES = 8
MXU_DIM = 256
LOG2E = 1.4426950408889634
VMEM_LIMIT = 56 * 1024 * 1024

INPROJ_TM = 2048
INPROJ_TN = 512
ATTN_T = MXU_DIM
HGRN_TT = 256
HGRN_C = 32
MERGE_TM = 512
MOE_TM = 384
PLAN_TM = 1024
DISPATCH_TM = 1024
COMBINE_TM = 512


def _params(sem):
    return pltpu.CompilerParams(dimension_semantics=sem, vmem_limit_bytes=VMEM_LIMIT)


def _split_dot(x, m, passes):
    acc = None
    r = x
    for p in range(passes):
        h = r.astype(BF16)
        term = jnp.dot(h, m, preferred_element_type=F32)
        acc = term if acc is None else acc + term
        if p + 1 < passes:
            r = r - h.astype(F32)
    return acc


def _inproj_kernel(x_ref, g_ref, w_ref, qkv_ref, rest_ref, h_scr, w_scr, *, q_scale):
    i = pl.program_id(0)
    j = pl.program_id(1)

    @pl.when(j == 0)
    def _():
        x = x_ref[...]
        var = jnp.mean(x * x, axis=-1, keepdims=True)
        h_scr[...] = (x * lax.rsqrt(var + EPS) * g_ref[...]).astype(BF16)

    @pl.when(i == 0)
    def _():
        w_scr[j] = w_ref[...].astype(BF16)

    acc = jnp.dot(h_scr[...], w_scr[j], preferred_element_type=F32)
    qkv_ref[...] = (acc * jnp.where(j == 0, q_scale, 1.0)).astype(BF16)
    rest_ref[...] = acc


def _inproj(x2, ln_g, w_in, sb_width):
    n, d = x2.shape
    cols = w_in.shape[1]
    tm, tn = min(INPROJ_TM, n), INPROJ_TN
    assert sb_width == tn, "q block must be exactly one column tile"
    nq = 3 * sb_width // tn
    nj = cols // tn
    kern = functools.partial(_inproj_kernel, q_scale=SB_HEAD_DIM ** -0.5)
    return pl.pallas_call(
        kern,
        grid=(n // tm, nj),
        in_specs=[
            pl.BlockSpec((tm, d), lambda i, j: (i, 0)),
            pl.BlockSpec((1, d), lambda i, j: (0, 0)),
            pl.BlockSpec((d, tn), lambda i, j: (0, jnp.where(i == 0, j, nj - 1))),
        ],
        out_specs=[
            pl.BlockSpec((tm, tn), lambda i, j: (i, jnp.minimum(j, nq))),
            pl.BlockSpec((tm, tn), lambda i, j: (i, jnp.maximum(j - nq, 0))),
        ],
        out_shape=[
            jax.ShapeDtypeStruct((n, (nq + 1) * tn), BF16),
            jax.ShapeDtypeStruct((n, cols - nq * tn), F32),
        ],
        scratch_shapes=[pltpu.VMEM((tm, d), BF16), pltpu.VMEM((nj, d, tn), BF16)],
        compiler_params=_params(("arbitrary", "arbitrary")),
        name="inproj",
    )(x2, ln_g.reshape(1, d), w_in)


ATTN_STAGES = 3
ATTN_STREAMS = 2
MASK_BIAS = -1e30
ATTN_SKIP = 111.0


def _attn_kernel(q_ref, k_ref, v_ref, tri_ref, bias_ref, o_ref,
                 z0, z1, z2, i0, i1, i2, acc_ref, c_ref, *, t, nq):
    zbuf = (z0, z1, z2)
    ibuf = (i0, i1, i2)
    for r in zbuf + ibuf + (acc_ref, c_ref):
        r[...] = jnp.zeros_like(r)

    lane = lax.broadcasted_iota(jnp.int32, (t, LANES), 1)
    head0 = lane < SB_HEAD_DIM
    nt = (((1,), (1,)), ((), ()))

    def stage_a(qi, kj, slot):
        q = q_ref[pl.ds(pl.multiple_of(qi * t, t), t), :]
        zero = jnp.zeros_like(q)
        q2 = jnp.concatenate([jnp.where(head0, q, zero), jnp.where(head0, zero, q)], axis=0)
        k = k_ref[pl.ds(pl.multiple_of(kj * t, t), t), :]
        z = lax.dot_general(q2, k, nt, preferred_element_type=F32)
        bias = bias_ref[(qi == kj).astype(jnp.int32)]
        zbuf[slot][...] = z + jnp.concatenate([bias, bias], axis=0)

    def stage_b(slot):
        z = zbuf[slot][...]
        p = jnp.maximum(z, 0.0) + jnp.log(1.0 + jnp.exp2(jnp.abs(z) * (-LOG2E)))
        incl = jnp.dot(p.astype(BF16), tri_ref[...], preferred_element_type=F32) + p
        ibuf[slot][...] = incl
        return incl[:, 0:1]

    def stage_c(qi, kj, slot, stream):
        first = qi == kj
        incl = ibuf[slot][...]
        c = jnp.where(first, 0.0, c_ref[stream])
        a = jnp.exp2((zbuf[slot][...] - incl - c) * LOG2E)
        v = v_ref[pl.ds(pl.multiple_of(kj * t, t), t), :]
        pv = jnp.dot(a.astype(BF16), v, preferred_element_type=F32)
        acc = jnp.where(first, pv, acc_ref[stream] + pv)
        acc_ref[stream] = acc
        c_ref[stream] = c + incl[:, 0:1]
        o_ref[pl.ds(pl.multiple_of(qi * t, t), t), :] = (
            jnp.where(head0, acc[0:t], acc[t:2 * t]).astype(o_ref.dtype))

    def block(carry):
        qs, ks, qn, kn, skip_q, drained = carry
        qs, ks, qn, kn, skip_q = list(qs), list(ks), list(qn), list(kn), list(skip_q)
        all_done = qn[0] >= nq
        for p in range(1, ATTN_STREAMS):
            all_done = jnp.logical_and(all_done, qn[p] >= nq)
        drained = drained + all_done.astype(jnp.int32)
        for r in range(ATTN_STAGES * ATTN_STREAMS):
            sa, sc, sb = r % ATTN_STAGES, (r + 1) % ATTN_STAGES, (r + 2) % ATTN_STAGES
            pa, pb = r % ATTN_STREAMS, (r - 1) % ATTN_STREAMS
            stage_c(qs[sc], ks[sc], sc, pa)
            total_b = stage_b(sb)
            carry_b = jnp.where(qs[sb] == ks[sb], 0.0, c_ref[pb]) + total_b
            skip = skip_q[pa] == qn[pa]
            q_cur = jnp.where(skip, qn[pa] + ATTN_STREAMS, qn[pa])
            k_cur = jnp.where(skip, qn[pa] + ATTN_STREAMS, kn[pa])
            drain = q_cur >= nq
            qa = jnp.where(drain, 0, q_cur)
            ka = jnp.where(drain, 0, k_cur)
            stage_a(qa, ka, sa)
            last = k_cur == 0
            qn[pa] = jnp.where(jnp.logical_and(last, jnp.logical_not(drain)),
                               q_cur + ATTN_STREAMS, q_cur)
            kn[pa] = jnp.where(drain, k_cur, jnp.where(last, q_cur + ATTN_STREAMS, k_cur - 1))
            skip_q[pb] = jnp.where(jnp.min(carry_b) >= ATTN_SKIP, qs[sb], skip_q[pb])
            qs[sa], ks[sa] = qa, ka
        return tuple(qs), tuple(ks), tuple(qn), tuple(kn), tuple(skip_q), drained

    zero = jnp.int32(0)
    first_q = tuple(jnp.int32(p) for p in range(ATTN_STREAMS))
    init = ((zero,) * ATTN_STAGES, (zero,) * ATTN_STAGES, first_q, first_q,
            (jnp.int32(-1),) * ATTN_STREAMS, zero)
    lax.while_loop(lambda carry: carry[5] < 1, block, init)


def _attn(qkv, bsz, seq, tri, bias):
    n = bsz * seq
    t = min(ATTN_T, seq)
    pairs = SB_HEADS * SB_HEAD_DIM // LANES
    kern = functools.partial(_attn_kernel, t=t, nq=seq // t)
    return pl.pallas_call(
        kern,
        grid=(bsz, pairs),
        in_specs=[
            pl.BlockSpec((seq, LANES), lambda b, p: (b, p)),
            pl.BlockSpec((seq, LANES), lambda b, p: (b, pairs + p)),
            pl.BlockSpec((seq, LANES), lambda b, p: (b, 2 * pairs + p)),
            pl.BlockSpec((t, t), lambda b, p: (0, 0)),
            pl.BlockSpec((2, t, t), lambda b, p: (0, 0, 0)),
        ],
        out_specs=pl.BlockSpec((seq, LANES), lambda b, p: (b, p)),
        out_shape=jax.ShapeDtypeStruct((n, pairs * LANES), BF16),
        scratch_shapes=([pltpu.VMEM((2 * t, t), F32)] * (2 * ATTN_STAGES)
                        + [pltpu.VMEM((ATTN_STREAMS, 2 * t, LANES), F32),
                           pltpu.VMEM((ATTN_STREAMS, 2 * t, 1), F32)]),
        compiler_params=_params(("parallel", "parallel")),
        name="attn",
    )(qkv, qkv, qkv, tri, bias)


def _group_dot(x, bd):
    g = bd.shape[0]
    parts = [jnp.dot(x[:, i:i + g], bd, preferred_element_type=F32)
             for i in range(0, x.shape[1], g)]
    return jnp.concatenate(parts, axis=1)


def _hgrn_kernel(q_ref, f_ref, i_ref, g_ref, lbl_ref, ng_ref, tri_ref, bd_ref, o_ref,
                 st_scr, wpad, vpad, wsh, vsh, *, tt, c, layer):
    ti = pl.program_id(1)
    w = q_ref.shape[1]
    nch = tt // c

    @pl.when(ti == 0)
    def _():
        st_scr[...] = jnp.zeros_like(st_scr)

    lg = lbl_ref[...]
    e = jnp.exp(lg - jnp.max(lg, axis=0, keepdims=True))
    lb = jnp.sum(e[0:layer + 1], axis=0, keepdims=True) / jnp.sum(e, axis=0, keepdims=True)

    f = lb + (1.0 - lb) * jax.nn.sigmoid(f_ref[...])
    kk = 1.0 - f
    qv = q_ref[...]
    qs = qv * jax.nn.sigmoid(qv)
    v = i_ref[...]
    bd = bd_ref[...]
    cum = _split_dot_left(tri_ref[...], jnp.log(f) * LOG2E, 3)
    wk = jnp.log(kk) * LOG2E - cum

    pos = lax.broadcasted_iota(jnp.int32, (tt, 1), 0) % c
    wpad[0:SUBLANES, :] = jnp.zeros((SUBLANES, w), F32)
    vpad[0:SUBLANES, :] = jnp.zeros((SUBLANES, w), F32)
    wpad[SUBLANES:SUBLANES + tt, :] = wk
    vpad[SUBLANES:SUBLANES + tt, :] = v
    for b in range(SUBLANES):
        wb = wpad[SUBLANES - b:SUBLANES - b + tt, :]
        vb = vpad[SUBLANES - b:SUBLANES - b + tt, :]
        if b:
            wb = jnp.where(pos >= b, wb, -jnp.inf)
        wsh[b] = wb.reshape(nch, c, w)
        vsh[b] = vb.reshape(nch, c, w)

    qs3 = qs.reshape(nch, c, w)
    cum3 = cum.reshape(nch, c, w)
    acc = None
    for a in reversed(range(c // SUBLANES)):
        rows = c - SUBLANES * a
        qa = qs3[:, SUBLANES * a:, :].reshape(nch * rows, w)
        ca = cum3[:, SUBLANES * a:, :].reshape(nch * rows, w)
        acc_a = None
        for b in range(SUBLANES):
            wb = wsh[b, :, 0:rows, :].reshape(nch * rows, w)
            vb = vsh[b, :, 0:rows, :].reshape(nch * rows, w)
            dd = qa * jnp.exp2(ca + wb)
            term = _group_dot(dd.astype(BF16), bd) * vb
            acc_a = term if acc_a is None else acc_a + term
        acc_a = acc_a.reshape(nch, rows, w)
        if acc is not None:
            acc_a = acc_a + jnp.concatenate([jnp.zeros((nch, SUBLANES, w), F32), acc], axis=1)
        acc = acc_a
    acc = acc.reshape(tt, w)

    grp = st_scr.shape[1]
    bdmask = bd[0:grp, 0:grp] != 0
    outs = []
    for ci in range(nch):
        r0 = ci * c
        cum_c = cum[r0:r0 + c]
        last = cum_c[c - 1:c]
        qd = (qs[r0:r0 + c] * jnp.exp2(cum_c)).astype(BF16)
        kd = (kk[r0:r0 + c] * jnp.exp2(last - cum_c)).astype(BF16)
        vc = v[r0:r0 + c].astype(BF16)
        dec = jnp.exp2(last)
        o_parts = []
        for gi in range(w // grp):
            sl = slice(gi * grp, (gi + 1) * grp)
            st = st_scr[gi]
            o_parts.append(lax.dot_general(qd[:, sl], st.astype(BF16), (((1,), (1,)), ((), ())),
                                           preferred_element_type=F32))
            upd = lax.dot_general(vc[:, sl], kd[:, sl], (((0,), (0,)), ((), ())),
                                  preferred_element_type=F32)
            st_scr[gi] = st * dec[:, sl] + jnp.where(bdmask, upd, 0.0)
        outs.append(jnp.concatenate(o_parts, axis=1))
    o = acc + jnp.concatenate(outs, axis=0)

    o2 = o * o
    o2_hi = o2.astype(BF16)
    o2_lo = (o2 - o2_hi.astype(F32)).astype(BF16)
    ms = (_group_dot(o2_hi, bd) + _group_dot(o2_lo, bd)) * (1.0 / HG_DIM)
    gv = g_ref[...]
    o = o * lax.rsqrt(ms + EPS) * ng_ref[...] * (gv * jax.nn.sigmoid(gv))
    o_ref[...] = o.astype(o_ref.dtype)


def _split_dot_left(m, x, passes):
    acc = None
    r = x
    for p in range(passes):
        h = r.astype(BF16)
        term = jnp.dot(m, h, preferred_element_type=F32)
        acc = term if acc is None else acc + term
        if p + 1 < passes:
            r = r - h.astype(F32)
    return acc


def _hgrn(rest, lb_logits, norm_g, bsz, seq, layer, tri, bd):
    n = bsz * seq
    w = HG_HEADS * HG_DIM
    grp = bd.shape[0]
    tt = min(HGRN_TT, seq)
    c = min(HGRN_C, tt)
    nt = seq // tt
    kern = functools.partial(_hgrn_kernel, tt=tt, c=c, layer=layer)
    col = lambda j: pl.BlockSpec((tt, w), lambda b, i, j=j: (b * nt + i, j))
    const = lambda shape: pl.BlockSpec(shape, lambda b, i: (0, 0))
    return pl.pallas_call(
        kern,
        grid=(bsz, nt),
        in_specs=[col(0), col(1), col(2), col(3),
                  const(lb_logits.shape), const((1, w)), const((tt, tt)), const((grp, grp))],
        out_specs=pl.BlockSpec((tt, w), lambda b, i: (b * nt + i, 0)),
        out_shape=jax.ShapeDtypeStruct((n, w), BF16),
        scratch_shapes=([pltpu.VMEM((w // LANES, LANES, LANES), F32)]
                        + [pltpu.VMEM((SUBLANES + tt, w), F32)] * 2
                        + [pltpu.VMEM((SUBLANES, tt // c, c, w), F32)] * 2),
        compiler_params=_params(("parallel", "arbitrary")),
        name="hgrn",
    )(rest, rest, rest, rest, lb_logits, norm_g.reshape(1, w), tri, bd)


PAIR_A = (0, 0, 0, 1, 1, 3)
PAIR_B = (1, 2, 3, 3, 2, 2)
N_PAIRS = len(PAIR_A)
N_CLASSES = N_GROUPS * N_PAIRS
assert sorted(tuple(sorted(p)) for p in zip(PAIR_A, PAIR_B)) == [
    (a, b) for a in range(EXPERTS_PER_GROUP) for b in range(a + 1, EXPERTS_PER_GROUP)]
ROUTE_CLS, ROUTE_WA, ROUTE_WB = 0, 1, 2


def _pair_slots(pidx):
    a = b = jnp.zeros_like(pidx)
    for p in range(N_PAIRS):
        a = jnp.where(pidx == p, float(PAIR_A[p]), a)
        b = jnp.where(pidx == p, float(PAIR_B[p]), b)
    return a, b


def _pair_index(lo, hi):
    pidx = jnp.zeros_like(lo)
    for p in range(N_PAIRS):
        is_p = jnp.logical_and(lo == min(PAIR_A[p], PAIR_B[p]), hi == max(PAIR_A[p], PAIR_B[p]))
        pidx = jnp.where(is_p, float(p), pidx)
    return pidx


def _route(lg):
    lane = lax.broadcasted_iota(jnp.int32, lg.shape, 1)
    neg = jnp.float32(-jnp.inf)
    big = jnp.int32(LANES)
    gmask = jnp.logical_and(lane >= N_EXPERTS, lane < N_EXPERTS + N_GROUPS)
    gl = jnp.where(gmask, lg, neg)
    gmax = jnp.max(gl, axis=1, keepdims=True)
    gidx = jnp.min(jnp.where(gl == gmax, lane, big), axis=1, keepdims=True) - N_EXPERTS
    w_grp = 1.0 / jnp.sum(jnp.where(gmask, jnp.exp(gl - gmax), 0.0), axis=1, keepdims=True)
    in_grp = jnp.logical_and(lane < N_EXPERTS, lane // EXPERTS_PER_GROUP == gidx)
    l1 = jnp.where(in_grp, lg, neg)
    v1 = jnp.max(l1, axis=1, keepdims=True)
    i1 = jnp.min(jnp.where(l1 == v1, lane, big), axis=1, keepdims=True)
    l2 = jnp.where(jnp.logical_and(in_grp, lane != i1), lg, neg)
    v2 = jnp.max(l2, axis=1, keepdims=True)
    i2 = jnp.min(jnp.where(l2 == v2, lane, big), axis=1, keepdims=True)
    e2 = jnp.exp(v2 - v1)
    p1 = 1.0 / (1.0 + e2)
    p2 = e2 * p1
    loc1 = (i1 - gidx * EXPERTS_PER_GROUP).astype(F32)
    loc2 = (i2 - gidx * EXPERTS_PER_GROUP).astype(F32)
    pidx = _pair_index(jnp.minimum(loc1, loc2), jnp.maximum(loc1, loc2))
    cls = gidx.astype(F32) * N_PAIRS + pidx
    a_loc, _ = _pair_slots(pidx)
    first_is_a = loc1 == a_loc
    wa = jnp.where(first_is_a, p1, p2) * w_grp
    wb = jnp.where(first_is_a, p2, p1) * w_grp
    record = jnp.where(lane == ROUTE_CLS, cls,
                       jnp.where(lane == ROUTE_WA, wa, jnp.where(lane == ROUTE_WB, wb, 0.0)))
    return record, cls


def _store_token_tiles(ref, x):
    rows, d = x.shape
    assert d == SUBLANES * LANES
    for s in range(SUBLANES):
        ref[pl.ds(s, rows, stride=SUBLANES), :] = x[:, s * LANES:(s + 1) * LANES]


def _load_token_tiles(ref):
    rows = ref.shape[0] // SUBLANES
    return jnp.concatenate([ref[pl.ds(s, rows, stride=SUBLANES), :] for s in range(SUBLANES)],
                           axis=1)


def _merge_kernel(x_ref, ysb_ref, yhg_ref, gsb_ref, ghg_ref, wbs_ref, wbh_ref, wo_ref,
                  ln_ref, wr_ref, br_ref, x1_ref, t_ref, route_ref, sel_ref, tot_ref,
                  wbs_scr, wbh_scr, wo_scr):
    @pl.when(pl.program_id(0) == 0)
    def _():
        wbs_scr[...] = wbs_ref[...].astype(BF16)
        wbh_scr[...] = wbh_ref[...].astype(BF16)
        wo_scr[...] = wo_ref[...].astype(BF16)
        tot_ref[...] = jnp.zeros_like(tot_ref)

    a = jnp.dot(ysb_ref[...], wbs_scr[...], preferred_element_type=F32)
    b = jnp.dot(yhg_ref[...], wbh_scr[...], preferred_element_type=F32)
    merged = jax.nn.sigmoid(gsb_ref[...]) * a + jax.nn.sigmoid(ghg_ref[...]) * b
    x1 = x_ref[...] + jnp.dot(merged.astype(BF16), wo_scr[...], preferred_element_type=F32)
    x1_ref[...] = x1
    var = jnp.mean(x1 * x1, axis=-1, keepdims=True)
    t = x1 * lax.rsqrt(var + EPS) * ln_ref[...]
    _store_token_tiles(t_ref, t)
    t_hi = t.astype(BF16)
    t_lo = (t - t_hi.astype(F32)).astype(BF16)
    p_hi = jnp.dot(t_hi, wr_ref[...], preferred_element_type=F32)
    p_lo = jnp.dot(t_lo, wr_ref[...], preferred_element_type=F32)
    lg = (p_hi[:, :LANES] + p_hi[:, LANES:]) + (p_lo[:, :LANES] + p_lo[:, LANES:]) + br_ref[...]
    route, cls = _route(lg)
    route_ref[...] = route
    lane = lax.broadcasted_iota(jnp.int32, route.shape, 1)
    sel = jnp.where(lane.astype(F32) == cls, 1.0, 0.0)
    sel_ref[...] = sel.astype(BF16)
    tot_ref[...] += jnp.broadcast_to(jnp.sum(sel, axis=0, keepdims=True), tot_ref.shape)


def _merge(x2, y_sb, y_hg, rest, wbs, wbh, wo, ln_g, wr, br):
    n, d = x2.shape
    tm = min(MERGE_TM, n)
    w_sb, w_hg = y_sb.shape[1], y_hg.shape[1]
    gate_blk = (rest.shape[1] - 2 * d) // d
    row = lambda wdt, j=0: pl.BlockSpec((tm, wdt), lambda i, j=j: (i, j))
    const = lambda shape: pl.BlockSpec(shape, lambda i: (0, 0))
    return pl.pallas_call(
        _merge_kernel,
        grid=(n // tm,),
        in_specs=[row(d), row(w_sb), row(w_hg), row(d, gate_blk), row(d, gate_blk + 1),
                  const(wbs.shape), const(wbh.shape), const(wo.shape), const((1, d)),
                  const(wr.shape), const((1, LANES))],
        out_specs=[row(d), pl.BlockSpec((tm * SUBLANES, LANES), lambda i: (i, 0)), row(LANES),
                   row(LANES), const((SUBLANES, LANES))],
        out_shape=[jax.ShapeDtypeStruct((n, d), F32),
                   jax.ShapeDtypeStruct((n * SUBLANES, LANES), F32),
                   jax.ShapeDtypeStruct((n, LANES), F32),
                   jax.ShapeDtypeStruct((n, LANES), BF16),
                   jax.ShapeDtypeStruct((SUBLANES, LANES), F32)],
        scratch_shapes=[pltpu.VMEM(w.shape, BF16) for w in (wbs, wbh, wo)],
        compiler_params=_params(("arbitrary",)),
        name="merge",
    )(x2, y_sb, y_hg, rest, rest, wbs, wbh, wo, ln_g.reshape(1, d), wr, br)


TOKEN_ROWS_OUT = 2 * SUBLANES


def _plan_kernel(sel_ref, tot_ref, ltri_ref, utri_ref, pos_ref, te_ref, meta_ref,
                 run_scr, *, tile_rows):
    i = pl.program_id(0)
    sel_b = sel_ref[...]
    tm = sel_b.shape[0]
    lane = lax.broadcasted_iota(jnp.int32, (tm, LANES), 1)
    sel = sel_b.astype(F32)
    is_cls = sel > 0.5

    tot = tot_ref[0:1, :]
    top = tot + (tile_rows - 1.0)
    tiles = jnp.floor(top * (1.0 / tile_rows))
    tiles = jnp.where(tiles * tile_rows > top, tiles - 1.0, tiles)
    tiles = jnp.where((tiles + 1.0) * tile_rows <= top, tiles + 1.0, tiles)
    first_tile = _split_dot(jnp.broadcast_to(tiles, (SUBLANES, LANES)), utri_ref[...], 3)[0:1]

    @pl.when(i == 0)
    def _():
        run_scr[...] = jnp.zeros_like(run_scr)
        lane1 = lax.broadcasted_iota(jnp.int32, (1, LANES), 1)
        is_c = lane1 < N_CLASSES
        end_tile = first_tile + tiles
        n_valid = jnp.sum(jnp.where(is_c, tiles, 0.0), axis=1, keepdims=True)
        rows = te_ref.shape[0]
        j = lax.broadcasted_iota(jnp.int32, (rows, LANES), 0).astype(F32)
        j = jnp.minimum(j, n_valid - 1.0)
        lane2 = lax.broadcasted_iota(jnp.int32, (rows, LANES), 1)
        done = jnp.where(jnp.logical_and(lane2 < N_CLASSES, end_tile <= j), 1.0, 0.0)
        tc = jnp.sum(done, axis=1, keepdims=True)
        tg = jnp.floor((tc + 0.5) * (1.0 / N_PAIRS))
        ta, tb = _pair_slots(tc - tg * N_PAIRS)
        te = jnp.where(lane2 == 0, tg * EXPERTS_PER_GROUP + ta, tg * EXPERTS_PER_GROUP + tb)
        te_ref[...] = te.astype(jnp.int32)
        last_tile = jnp.where(jnp.logical_and(is_c, tiles > 0), end_tile - 1.0, -1.0)
        meta = jnp.where(lane1 == N_CLASSES, n_valid, last_tile)
        meta_ref[...] = jnp.broadcast_to(meta, meta_ref.shape).astype(jnp.int32)

    rank = jnp.dot(ltri_ref[...], sel_b, preferred_element_type=F32) + run_scr[...]
    dest = jnp.sum(jnp.where(is_cls, first_tile * tile_rows + rank, 0.0), axis=1, keepdims=True)
    row = lax.broadcasted_iota(jnp.int32, (tm, LANES), 0)
    spread = jnp.where(lane == row % LANES, dest, 0.0)
    pos_ref[...] = jnp.sum(spread.reshape(tm // LANES, LANES, LANES), axis=1).astype(jnp.int32)
    run_scr[...] += jnp.sum(sel, axis=0, keepdims=True)


def _plan(sel, tot, n_tiles_max):
    n = sel.shape[0]
    tm = min(PLAN_TM, n)
    r = np.arange(tm)
    ltri = jnp.asarray((r[None, :] < r[:, None]).astype(np.float32), dtype=BF16)
    e = np.arange(LANES)
    utri = jnp.asarray((e[:, None] < e[None, :]).astype(np.float32), dtype=BF16)
    te_rows = -(-n_tiles_max // SUBLANES) * SUBLANES
    kern = functools.partial(_plan_kernel, tile_rows=MOE_TM)
    return pl.pallas_call(
        kern,
        grid=(n // tm,),
        in_specs=[pl.BlockSpec((tm, LANES), lambda i: (i, 0)),
                  pl.BlockSpec((SUBLANES, LANES), lambda i: (0, 0)),
                  pl.BlockSpec((tm, tm), lambda i: (0, 0)),
                  pl.BlockSpec((LANES, LANES), lambda i: (0, 0))],
        out_specs=[pl.BlockSpec((tm // LANES, LANES), lambda i: (i, 0)),
                   pl.BlockSpec((te_rows, LANES), lambda i: (0, 0)),
                   pl.BlockSpec((SUBLANES, LANES), lambda i: (0, 0))],
        out_shape=[jax.ShapeDtypeStruct((n // LANES, LANES), jnp.int32),
                   jax.ShapeDtypeStruct((te_rows, LANES), jnp.int32),
                   jax.ShapeDtypeStruct((SUBLANES, LANES), jnp.int32)],
        scratch_shapes=[pltpu.VMEM((1, LANES), F32)],
        compiler_params=_params(("arbitrary",)),
        name="plan",
    )(sel, tot, ltri, utri)


DMA_UNROLL = 8


def _token_copy(src_ref, r, dst_ref, p, sem, rows):
    return pltpu.make_async_copy(src_ref.at[pl.ds(pl.multiple_of(r * rows, rows), rows)],
                                 dst_ref.at[pl.ds(pl.multiple_of(p * rows, rows), rows)], sem)


def _dispatch_kernel(pos_ref, meta_ref, t_ref, xs_ref, zero_scr, sem, tail_sem, *, tm, tile_rows):
    i = pl.program_id(0)
    tile_rows = tile_rows * SUBLANES
    n_tiles = xs_ref.shape[0] // tile_rows
    n_valid = meta_ref[N_CLASSES]

    def clear(tile, s):
        return pltpu.make_async_copy(zero_scr, xs_ref.at[pl.ds(tile * tile_rows, tile_rows)], s)

    def for_each(clears, fn):
        for tile, cond in clears:
            @pl.when(cond)
            def _():
                fn(tile)

    last_tiles = [(meta_ref[c], meta_ref[c] >= 0) for c in range(N_CLASSES)]
    past_tiles = [(n_tiles - 1 - k, n_tiles - 1 - k >= n_valid) for k in range(N_CLASSES)]

    @pl.when(i == 0)
    def _():
        zero_scr[...] = jnp.zeros_like(zero_scr)
        for_each(last_tiles, lambda tile: clear(tile, sem).start())
        for_each(past_tiles, lambda tile: clear(tile, tail_sem).start())
        for_each(last_tiles, lambda tile: clear(tile, sem).wait())

    base = i * tm

    def start(r2, carry):
        for k in range(2):
            r = 2 * r2 + k
            _token_copy(t_ref, r, xs_ref, pos_ref[base + r], sem, SUBLANES).start(priority=k)
        return carry

    lax.fori_loop(0, tm // 2, start, 0, unroll=DMA_UNROLL)
    pltpu.make_async_copy(t_ref, xs_ref.at[pl.ds(0, tm * SUBLANES)], sem).wait()

    @pl.when(i == pl.num_programs(0) - 1)
    def _():
        for_each(past_tiles, lambda tile: clear(tile, tail_sem).wait())


def _dispatch(pos, meta, t, n_rows):
    n = t.shape[0] // SUBLANES
    tm = min(DISPATCH_TM, n)
    kern = functools.partial(_dispatch_kernel, tm=tm, tile_rows=MOE_TM)
    grid_spec = pltpu.PrefetchScalarGridSpec(
        num_scalar_prefetch=2,
        grid=(n // tm,),
        in_specs=[pl.BlockSpec((tm * SUBLANES, LANES), lambda i, *_: (i, 0))],
        out_specs=pl.BlockSpec(memory_space=pl.ANY),
        scratch_shapes=[pltpu.VMEM((MOE_TM * SUBLANES, LANES), F32),
                        pltpu.SemaphoreType.DMA(()), pltpu.SemaphoreType.DMA(())],
    )
    return pl.pallas_call(
        kern,
        grid_spec=grid_spec,
        out_shape=jax.ShapeDtypeStruct((n_rows * SUBLANES, LANES), F32),
        compiler_params=_params(("arbitrary",)),
        name="dispatch",
    )(pos, meta, t)


def _expert_kernel(tea_ref, teb_ref, nv_ref, xs_ref, wga, wua, wda, wgb, wub, wdb, ys_ref):
    in_use = pl.program_id(0) < nv_ref[0]
    rows = xs_ref.shape[0] // SUBLANES

    @pl.when(in_use)
    def _():
        x = _load_token_tiles(xs_ref).astype(BF16)
        for slot, (wg, wu, wd) in enumerate(((wga, wua, wda), (wgb, wub, wdb))):
            hg = jnp.dot(x, wg[...].astype(BF16), preferred_element_type=F32)
            hu = jnp.dot(x, wu[...].astype(BF16), preferred_element_type=F32)
            a = (hg * jax.nn.sigmoid(hg) * hu).astype(BF16)
            y = jnp.dot(a, wd[...].astype(BF16), preferred_element_type=F32)
            for s in range(SUBLANES):
                ys_ref[pl.ds(slot * SUBLANES + s, rows, stride=TOKEN_ROWS_OUT), :] = (
                    y[:, s * LANES:(s + 1) * LANES])

    @pl.when(jnp.logical_not(in_use))
    def _():
        ys_ref[...] = jnp.zeros_like(ys_ref)


def _experts(tea, teb, nv, xs, wg, wu, wd):
    _, d, de = wg.shape
    assert d == SUBLANES * LANES
    n_tiles = xs.shape[0] // (MOE_TM * SUBLANES)
    wa = lambda shape: pl.BlockSpec(shape, lambda j, tea, teb, nv: (tea[j], 0, 0))
    wb = lambda shape: pl.BlockSpec(shape, lambda j, tea, teb, nv: (teb[j], 0, 0))
    grid_spec = pltpu.PrefetchScalarGridSpec(
        num_scalar_prefetch=3,
        grid=(n_tiles,),
        in_specs=[pl.BlockSpec((MOE_TM * SUBLANES, LANES),
                               lambda j, tea, teb, nv: (jnp.minimum(j, nv[0] - 1), 0)),
                  wa((None, d, de)), wa((None, d, de)), wa((None, de, d)),
                  wb((None, d, de)), wb((None, d, de)), wb((None, de, d))],
        out_specs=pl.BlockSpec((MOE_TM * TOKEN_ROWS_OUT, LANES), lambda j, *_: (j, 0)),
    )
    return pl.pallas_call(
        _expert_kernel,
        grid_spec=grid_spec,
        out_shape=jax.ShapeDtypeStruct((n_tiles * MOE_TM * TOKEN_ROWS_OUT, LANES), F32),
        compiler_params=_params(("arbitrary",)),
        name="experts",
    )(tea, teb, nv, xs, wg, wu, wd, wg, wu, wd)


def _combine_kernel(pos_ref, ys_ref, x1_ref, route_ref, fg_ref, o_ref, g, sem, *, tm):
    i = pl.program_id(0)
    slot = i % 2

    def gather(step, buf):
        base = step * tm

        def body(r2, carry):
            for k in range(2):
                r = 2 * r2 + k
                _token_copy(ys_ref, pos_ref[base + r], g.at[buf], r, sem.at[buf],
                            TOKEN_ROWS_OUT).start(priority=k)
            return carry

        lax.fori_loop(0, tm // 2, body, 0, unroll=DMA_UNROLL)

    @pl.when(i == 0)
    def _():
        gather(0, 0)

    @pl.when(i + 1 < pl.num_programs(0))
    def _():
        gather(i + 1, 1 - slot)

    pltpu.make_async_copy(ys_ref.at[pl.ds(0, tm * TOKEN_ROWS_OUT)], g.at[slot],
                          sem.at[slot]).wait()
    gs = g.at[slot]
    ya, yb = (jnp.concatenate([gs[pl.ds(off + s, tm, stride=TOKEN_ROWS_OUT), :]
                               for s in range(SUBLANES)], axis=1) for off in (0, SUBLANES))
    route = route_ref[...]
    x2 = x1_ref[...] + (route[:, ROUTE_WA:ROUTE_WA + 1] * ya + route[:, ROUTE_WB:ROUTE_WB + 1] * yb)
    var = jnp.mean(x2 * x2, axis=-1, keepdims=True)
    o_ref[...] = x2 * lax.rsqrt(var + EPS) * fg_ref[...]


def _combine(pos, ys, x1, route, final_g):
    n, d = x1.shape
    tm = min(COMBINE_TM, n)
    kern = functools.partial(_combine_kernel, tm=tm)
    grid_spec = pltpu.PrefetchScalarGridSpec(
        num_scalar_prefetch=1,
        grid=(n // tm,),
        in_specs=[pl.BlockSpec(memory_space=pl.ANY),
                  pl.BlockSpec((tm, d), lambda i, *_: (i, 0)),
                  pl.BlockSpec((tm, LANES), lambda i, *_: (i, 0)),
                  pl.BlockSpec((1, d), lambda i, *_: (0, 0))],
        out_specs=pl.BlockSpec((tm, d), lambda i, *_: (i, 0)),
        scratch_shapes=[pltpu.VMEM((2, tm * TOKEN_ROWS_OUT, LANES), F32),
                        pltpu.SemaphoreType.DMA((2,))],
    )
    return pl.pallas_call(
        kern,
        grid_spec=grid_spec,
        out_shape=jax.ShapeDtypeStruct((n, d), F32),
        compiler_params=_params(("arbitrary",)),
        name="combine",
    )(pos, ys, x1, route, final_g.reshape(1, d))


def _moe_sparse(t, route, sel, tot, wg, wu, wd, x1, final_g):
    n = x1.shape[0]
    n_tiles_max = (n + N_CLASSES * (MOE_TM - 1)) // MOE_TM
    pos, te, meta = _plan(sel, tot, n_tiles_max)
    pos = pos.reshape(n)
    xs = _dispatch(pos, meta[0, :N_CLASSES + 1], t, n_tiles_max * MOE_TM)
    ys = _experts(te[:n_tiles_max, 0], te[:n_tiles_max, 1], meta[0, N_CLASSES:N_CLASSES + 1],
                  xs, wg, wu, wd)
    return _combine(pos, ys, x1, route, final_g)


def _suffix_ones(t):
    j = np.arange(t)[:, None]
    s = np.arange(t)[None, :]
    return jnp.asarray((j > s).astype(np.float32), dtype=BF16)


def _causal_bias(t):
    row = np.arange(t)[:, None]
    col = np.arange(t)[None, :]
    diag = np.where(col < row, 0.0, MASK_BIAS).astype(np.float32)
    return jnp.asarray(np.stack([np.zeros_like(diag), diag]))


def _chunk_prefix_ones(tt, c):
    t = np.arange(tt)[:, None]
    j = np.arange(tt)[None, :]
    return jnp.asarray(((j <= t) & (t // c == j // c)).astype(np.float32), dtype=BF16)


def _block_diag_ones(w, blk):
    a = np.arange(w)
    return jnp.asarray((a[:, None] // blk == a[None, :] // blk).astype(np.float32), dtype=BF16)


def kernel(x, ln1_g, w_in, w_branch_sb, w_branch_hg, hg_norm_g, hg_lb_logits, w_out, ln2_g,
           w_router_group, b_router_group, w_router_expert, b_router_expert,
           w_exp_gate, w_exp_up, w_exp_down, final_g):
    bsz, seq, d = x.shape
    depth = w_in.shape[0]
    n = bsz * seq
    sb_width = SB_HEADS * SB_HEAD_DIM
    hg_width = HG_HEADS * HG_DIM

    tri_attn = _suffix_ones(min(ATTN_T, seq))
    bias_attn = _causal_bias(min(ATTN_T, seq))
    tt = min(HGRN_TT, seq)
    tri_hg = _chunk_prefix_ones(tt, min(HGRN_C, tt))
    bd = _block_diag_ones(min(MXU_DIM, hg_width), HG_DIM)

    x2 = x.reshape(n, d)
    for l in range(depth):
        qkv, rest = _inproj(x2, ln1_g[l], w_in[l], sb_width)
        y_sb = _attn(qkv, bsz, seq, tri_attn, bias_attn)
        y_hg = _hgrn(rest, hg_lb_logits, hg_norm_g[l], bsz, seq, l, tri_hg, bd)

        pad = LANES - N_EXPERTS - N_GROUPS
        wr = jnp.concatenate([w_router_expert[l], w_router_group[l],
                              jnp.zeros((d, pad), F32)], axis=1)
        wr_hi = wr.astype(BF16)
        wr_lo = (wr - wr_hi.astype(F32)).astype(BF16)
        wr_split = jnp.concatenate([wr_hi, wr_lo], axis=1)
        br = jnp.concatenate([b_router_expert[l], b_router_group[l],
                              jnp.zeros((pad,), F32)]).reshape(1, LANES)

        last = l == depth - 1
        x1, t, route, sel, tot = _merge(x2, y_sb, y_hg, rest, w_branch_sb[l], w_branch_hg[l],
                                        w_out[l], ln2_g[l], wr_split, br)
        assert last, "final rmsnorm is fused into the last layer's combine kernel"
        x2 = _moe_sparse(t, route, sel, tot, w_exp_gate[l], w_exp_up[l], w_exp_down[l], x1,
                         final_g)
    return x2.reshape(bsz, seq, d)
```

```python
import functools

import jax
import jax.numpy as jnp
import numpy as np
from jax import lax
from jax.experimental import pallas as pl
from jax.experimental.pallas import tpu as pltpu

F32 = jnp.float32
BF16 = jnp.bfloat16

EPS = 1e-6
SB_HEADS = 8
SB_HEAD_DIM = 64
HG_HEADS = 8
HG_DIM = 64
N_GROUPS = 4
EXPERTS_PER_GROUP = 4
N_EXPERTS = N_GROUPS * EXPERTS_PER_GROUP

LANES = 128
SUBLANES = 8
MXU_DIM = 256
LOG2E = 1.4426950408889634
VMEM_LIMIT = 56 * 1024 * 1024

INPROJ_TM = 2048
INPROJ_TN = 512
ATTN_T = MXU_DIM
HGRN_TT = 256
HGRN_C = 32
MERGE_TM = 512
MOE_TM = 384
PLAN_TM = 2048
DISPATCH_TM = 1024
COMBINE_TM = 512


def _params(sem):
    return pltpu.CompilerParams(dimension_semantics=sem, vmem_limit_bytes=VMEM_LIMIT)


def _split_dot(x, m, passes):
    acc = None
    r = x
    for p in range(passes):
        h = r.astype(BF16)
        term = jnp.dot(h, m, preferred_element_type=F32)
        acc = term if acc is None else acc + term
        if p + 1 < passes:
            r = r - h.astype(F32)
    return acc


def _inproj_kernel(x_ref, g_ref, w_ref, qkv_ref, rest_ref, h_scr, w_scr, *, q_scale):
    i = pl.program_id(0)
    j = pl.program_id(1)

    @pl.when(j == 0)
    def _():
        x = x_ref[...]
        var = jnp.mean(x * x, axis=-1, keepdims=True)
        h_scr[...] = (x * lax.rsqrt(var + EPS) * g_ref[...]).astype(BF16)

    @pl.when(i == 0)
    def _():
        w_scr[j] = w_ref[...].astype(BF16)

    acc = jnp.dot(h_scr[...], w_scr[j], preferred_element_type=F32)
    qkv_ref[...] = (acc * jnp.where(j == 0, q_scale, 1.0)).astype(BF16)
    rest_ref[...] = acc


def _inproj(x2, ln_g, w_in, sb_width):
    n, d = x2.shape
    cols = w_in.shape[1]
    tm, tn = min(INPROJ_TM, n), INPROJ_TN
    assert sb_width == tn, "q block must be exactly one column tile"
    nq = 3 * sb_width // tn
    nj = cols // tn
    kern = functools.partial(_inproj_kernel, q_scale=SB_HEAD_DIM ** -0.5)
    return pl.pallas_call(
        kern,
        grid=(n // tm, nj),
        in_specs=[
            pl.BlockSpec((tm, d), lambda i, j: (i, 0)),
            pl.BlockSpec((1, d), lambda i, j: (0, 0)),
            pl.BlockSpec((d, tn), lambda i, j: (0, jnp.where(i == 0, j, nj - 1))),
        ],
        out_specs=[
            pl.BlockSpec((tm, tn), lambda i, j: (i, jnp.minimum(j, nq))),
            pl.BlockSpec((tm, tn), lambda i, j: (i, jnp.maximum(j - nq, 0))),
        ],
        out_shape=[
            jax.ShapeDtypeStruct((n, (nq + 1) * tn), BF16),
            jax.ShapeDtypeStruct((n, cols - nq * tn), F32),
        ],
        scratch_shapes=[pltpu.VMEM((tm, d), BF16), pltpu.VMEM((nj, d, tn), BF16)],
        compiler_params=_params(("arbitrary", "arbitrary")),
        name="inproj",
    )(x2, ln_g.reshape(1, d), w_in)


ATTN_STAGES = 3
ATTN_STREAMS = 2
MASK_BIAS = -1e30
ATTN_SKIP = 111.0


def _attn_kernel(q_ref, k_ref, v_ref, tri_ref, bias_ref, o_ref,
                 z0, z1, z2, i0, i1, i2, acc_ref, c_ref, *, t, nq):
    zbuf = (z0, z1, z2)
    ibuf = (i0, i1, i2)
    for r in zbuf + ibuf + (acc_ref, c_ref):
        r[...] = jnp.zeros_like(r)

    lane = lax.broadcasted_iota(jnp.int32, (t, LANES), 1)
    head0 = lane < SB_HEAD_DIM
    nt = (((1,), (1,)), ((), ()))

    def stage_a(qi, kj, slot):
        q = q_ref[pl.ds(pl.multiple_of(qi * t, t), t), :]
        zero = jnp.zeros_like(q)
        q2 = jnp.concatenate([jnp.where(head0, q, zero), jnp.where(head0, zero, q)], axis=0)
        k = k_ref[pl.ds(pl.multiple_of(kj * t, t), t), :]
        z = lax.dot_general(q2, k, nt, preferred_element_type=F32)
        bias = bias_ref[(qi == kj).astype(jnp.int32)]
        zbuf[slot][...] = z + jnp.concatenate([bias, bias], axis=0)

    def stage_b(slot):
        z = zbuf[slot][...]
        p = jnp.maximum(z, 0.0) + jnp.log(1.0 + jnp.exp2(jnp.abs(z) * (-LOG2E)))
        incl = jnp.dot(p.astype(BF16), tri_ref[...], preferred_element_type=F32) + p
        ibuf[slot][...] = incl
        return incl[:, 0:1]

    def stage_c(qi, kj, slot, stream):
        first = qi == kj
        incl = ibuf[slot][...]
        c = jnp.where(first, 0.0, c_ref[stream])
        a = jnp.exp2((zbuf[slot][...] - incl - c) * LOG2E)
        v = v_ref[pl.ds(pl.multiple_of(kj * t, t), t), :]
        pv = jnp.dot(a.astype(BF16), v, preferred_element_type=F32)
        acc = jnp.where(first, pv, acc_ref[stream] + pv)
        acc_ref[stream] = acc
        c_ref[stream] = c + incl[:, 0:1]
        o_ref[pl.ds(pl.multiple_of(qi * t, t), t), :] = (
            jnp.where(head0, acc[0:t], acc[t:2 * t]).astype(o_ref.dtype))

    def block(carry):
        qs, ks, qn, kn, skip_q, drained = carry
        qs, ks, qn, kn, skip_q = list(qs), list(ks), list(qn), list(kn), list(skip_q)
        all_done = qn[0] >= nq
        for p in range(1, ATTN_STREAMS):
            all_done = jnp.logical_and(all_done, qn[p] >= nq)
        drained = drained + all_done.astype(jnp.int32)
        for r in range(ATTN_STAGES * ATTN_STREAMS):
            sa, sc, sb = r % ATTN_STAGES, (r + 1) % ATTN_STAGES, (r + 2) % ATTN_STAGES
            pa, pb = r % ATTN_STREAMS, (r - 1) % ATTN_STREAMS
            stage_c(qs[sc], ks[sc], sc, pa)
            total_b = stage_b(sb)
            carry_b = jnp.where(qs[sb] == ks[sb], 0.0, c_ref[pb]) + total_b
            skip = skip_q[pa] == qn[pa]
            q_cur = jnp.where(skip, qn[pa] + ATTN_STREAMS, qn[pa])
            k_cur = jnp.where(skip, qn[pa] + ATTN_STREAMS, kn[pa])
            drain = q_cur >= nq
            qa = jnp.where(drain, 0, q_cur)
            ka = jnp.where(drain, 0, k_cur)
            stage_a(qa, ka, sa)
            last = k_cur == 0
            qn[pa] = jnp.where(jnp.logical_and(last, jnp.logical_not(drain)),
                               q_cur + ATTN_STREAMS, q_cur)
            kn[pa] = jnp.where(drain, k_cur, jnp.where(last, q_cur + ATTN_STREAMS, k_cur - 1))
            skip_q[pb] = jnp.where(jnp.min(carry_b) >= ATTN_SKIP, qs[sb], skip_q[pb])
            qs[sa], ks[sa] = qa, ka
        return tuple(qs), tuple(ks), tuple(qn), tuple(kn), tuple(skip_q), drained

    zero = jnp.int32(0)
    first_q = tuple(jnp.int32(p) for p in range(ATTN_STREAMS))
    init = ((zero,) * ATTN_STAGES, (zero,) * ATTN_STAGES, first_q, first_q,
            (jnp.int32(-1),) * ATTN_STREAMS, zero)
    lax.while_loop(lambda carry: carry[5] < 1, block, init)


def _attn(qkv, bsz, seq, tri, bias):
    n = bsz * seq
    t = min(ATTN_T, seq)
    pairs = SB_HEADS * SB_HEAD_DIM // LANES
    kern = functools.partial(_attn_kernel, t=t, nq=seq // t)
    return pl.pallas_call(
        kern,
        grid=(bsz, pairs),
        in_specs=[
            pl.BlockSpec((seq, LANES), lambda b, p: (b, p)),
            pl.BlockSpec((seq, LANES), lambda b, p: (b, pairs + p)),
            pl.BlockSpec((seq, LANES), lambda b, p: (b, 2 * pairs + p)),
            pl.BlockSpec((t, t), lambda b, p: (0, 0)),
            pl.BlockSpec((2, t, t), lambda b, p: (0, 0, 0)),
        ],
        out_specs=pl.BlockSpec((seq, LANES), lambda b, p: (b, p)),
        out_shape=jax.ShapeDtypeStruct((n, pairs * LANES), BF16),
        scratch_shapes=([pltpu.VMEM((2 * t, t), F32)] * (2 * ATTN_STAGES)
                        + [pltpu.VMEM((ATTN_STREAMS, 2 * t, LANES), F32),
                           pltpu.VMEM((ATTN_STREAMS, 2 * t, 1), F32)]),
        compiler_params=_params(("parallel", "parallel")),
        name="attn",
    )(qkv, qkv, qkv, tri, bias)


def _group_dot(x, bd):
    g = bd.shape[0]
    parts = [jnp.dot(x[:, i:i + g], bd, preferred_element_type=F32)
             for i in range(0, x.shape[1], g)]
    return jnp.concatenate(parts, axis=1)


def _hgrn_kernel(q_ref, f_ref, i_ref, g_ref, lbl_ref, ng_ref, tri_ref, bd_ref, o_ref,
                 st_scr, wpad, vpad, wsh, vsh, *, tt, c, layer):
    ti = pl.program_id(1)
    w = q_ref.shape[1]
    nch = tt // c

    @pl.when(ti == 0)
    def _():
        st_scr[...] = jnp.zeros_like(st_scr)

    lg = lbl_ref[...]
    e = jnp.exp(lg - jnp.max(lg, axis=0, keepdims=True))
    lb = jnp.sum(e[0:layer + 1], axis=0, keepdims=True) / jnp.sum(e, axis=0, keepdims=True)

    f = lb + (1.0 - lb) * jax.nn.sigmoid(f_ref[...])
    kk = 1.0 - f
    qv = q_ref[...]
    qs = qv * jax.nn.sigmoid(qv)
    v = i_ref[...]
    bd = bd_ref[...]
    cum = _split_dot_left(tri_ref[...], jnp.log(f) * LOG2E, 3)
    wk = jnp.log(kk) * LOG2E - cum

    pos = lax.broadcasted_iota(jnp.int32, (tt, 1), 0) % c
    wpad[0:SUBLANES, :] = jnp.zeros((SUBLANES, w), F32)
    vpad[0:SUBLANES, :] = jnp.zeros((SUBLANES, w), F32)
    wpad[SUBLANES:SUBLANES + tt, :] = wk
    vpad[SUBLANES:SUBLANES + tt, :] = v
    for b in range(SUBLANES):
        wb = wpad[SUBLANES - b:SUBLANES - b + tt, :]
        vb = vpad[SUBLANES - b:SUBLANES - b + tt, :]
        if b:
            wb = jnp.where(pos >= b, wb, -jnp.inf)
        wsh[b] = wb.reshape(nch, c, w)
        vsh[b] = vb.reshape(nch, c, w)

    qs3 = qs.reshape(nch, c, w)
    cum3 = cum.reshape(nch, c, w)
    acc = None
    for a in reversed(range(c // SUBLANES)):
        rows = c - SUBLANES * a
        qa = qs3[:, SUBLANES * a:, :].reshape(nch * rows, w)
        ca = cum3[:, SUBLANES * a:, :].reshape(nch * rows, w)
        acc_a = None
        for b in range(SUBLANES):
            wb = wsh[b, :, 0:rows, :].reshape(nch * rows, w)
            vb = vsh[b, :, 0:rows, :].reshape(nch * rows, w)
            dd = qa * jnp.exp2(ca + wb)
            term = _group_dot(dd.astype(BF16), bd) * vb
            acc_a = term if acc_a is None else acc_a + term
        acc_a = acc_a.reshape(nch, rows, w)
        if acc is not None:
            acc_a = acc_a + jnp.concatenate([jnp.zeros((nch, SUBLANES, w), F32), acc], axis=1)
        acc = acc_a
    acc = acc.reshape(tt, w)

    grp = st_scr.shape[1]
    bdmask = bd[0:grp, 0:grp] != 0
    outs = []
    for ci in range(nch):
        r0 = ci * c
        cum_c = cum[r0:r0 + c]
        last = cum_c[c - 1:c]
        qd = (qs[r0:r0 + c] * jnp.exp2(cum_c)).astype(BF16)
        kd = (kk[r0:r0 + c] * jnp.exp2(last - cum_c)).astype(BF16)
        vc = v[r0:r0 + c].astype(BF16)
        dec = jnp.exp2(last)
        o_parts = []
        for gi in range(w // grp):
            sl = slice(gi * grp, (gi + 1) * grp)
            st = st_scr[gi]
            o_parts.append(lax.dot_general(qd[:, sl], st.astype(BF16), (((1,), (1,)), ((), ())),
                                           preferred_element_type=F32))
            upd = lax.dot_general(vc[:, sl], kd[:, sl], (((0,), (0,)), ((), ())),
                                  preferred_element_type=F32)
            st_scr[gi] = st * dec[:, sl] + jnp.where(bdmask, upd, 0.0)
        outs.append(jnp.concatenate(o_parts, axis=1))
    o = acc + jnp.concatenate(outs, axis=0)

    o2 = o * o
    o2_hi = o2.astype(BF16)
    o2_lo = (o2 - o2_hi.astype(F32)).astype(BF16)
    ms = (_group_dot(o2_hi, bd) + _group_dot(o2_lo, bd)) * (1.0 / HG_DIM)
    gv = g_ref[...]
    o = o * lax.rsqrt(ms + EPS) * ng_ref[...] * (gv * jax.nn.sigmoid(gv))
    o_ref[...] = o.astype(o_ref.dtype)


def _split_dot_left(m, x, passes):
    acc = None
    r = x
    for p in range(passes):
        h = r.astype(BF16)
        term = jnp.dot(m, h, preferred_element_type=F32)
        acc = term if acc is None else acc + term
        if p + 1 < passes:
            r = r - h.astype(F32)
    return acc


def _hgrn(rest, lb_logits, norm_g, bsz, seq, layer, tri, bd):
    n = bsz * seq
    w = HG_HEADS * HG_DIM
    grp = bd.shape[0]
    tt = min(HGRN_TT, seq)
    c = min(HGRN_C, tt)
    nt = seq // tt
    kern = functools.partial(_hgrn_kernel, tt=tt, c=c, layer=layer)
    col = lambda j: pl.BlockSpec((tt, w), lambda b, i, j=j: (b * nt + i, j))
    const = lambda shape: pl.BlockSpec(shape, lambda b, i: (0, 0))
    return pl.pallas_call(
        kern,
        grid=(bsz, nt),
        in_specs=[col(0), col(1), col(2), col(3),
                  const(lb_logits.shape), const((1, w)), const((tt, tt)), const((grp, grp))],
        out_specs=pl.BlockSpec((tt, w), lambda b, i: (b * nt + i, 0)),
        out_shape=jax.ShapeDtypeStruct((n, w), BF16),
        scratch_shapes=([pltpu.VMEM((w // LANES, LANES, LANES), F32)]
                        + [pltpu.VMEM((SUBLANES + tt, w), F32)] * 2
                        + [pltpu.VMEM((SUBLANES, tt // c, c, w), F32)] * 2),
        compiler_params=_params(("parallel", "arbitrary")),
        name="hgrn",
    )(rest, rest, rest, rest, lb_logits, norm_g.reshape(1, w), tri, bd)


PAIR_A = (0, 0, 0, 1, 1, 3)
PAIR_B = (1, 2, 3, 3, 2, 2)
N_PAIRS = len(PAIR_A)
N_CLASSES = N_GROUPS * N_PAIRS
assert sorted(tuple(sorted(p)) for p in zip(PAIR_A, PAIR_B)) == [
    (a, b) for a in range(EXPERTS_PER_GROUP) for b in range(a + 1, EXPERTS_PER_GROUP)]
ROUTE_CLS, ROUTE_WA, ROUTE_WB = 0, 1, 2


def _pair_slots(pidx):
    a = b = jnp.zeros_like(pidx)
    for p in range(N_PAIRS):
        a = jnp.where(pidx == p, float(PAIR_A[p]), a)
        b = jnp.where(pidx == p, float(PAIR_B[p]), b)
    return a, b


def _pair_index(lo, hi):
    pidx = jnp.zeros_like(lo)
    for p in range(N_PAIRS):
        is_p = jnp.logical_and(lo == min(PAIR_A[p], PAIR_B[p]), hi == max(PAIR_A[p], PAIR_B[p]))
        pidx = jnp.where(is_p, float(p), pidx)
    return pidx


def _route(lg):
    lane = lax.broadcasted_iota(jnp.int32, lg.shape, 1)
    neg = jnp.float32(-jnp.inf)
    big = jnp.int32(LANES)
    gmask = jnp.logical_and(lane >= N_EXPERTS, lane < N_EXPERTS + N_GROUPS)
    gl = jnp.where(gmask, lg, neg)
    gmax = jnp.max(gl, axis=1, keepdims=True)
    gidx = jnp.min(jnp.where(gl == gmax, lane, big), axis=1, keepdims=True) - N_EXPERTS
    w_grp = 1.0 / jnp.sum(jnp.where(gmask, jnp.exp(gl - gmax), 0.0), axis=1, keepdims=True)
    in_grp = jnp.logical_and(lane < N_EXPERTS, lane // EXPERTS_PER_GROUP == gidx)
    l1 = jnp.where(in_grp, lg, neg)
    v1 = jnp.max(l1, axis=1, keepdims=True)
    i1 = jnp.min(jnp.where(l1 == v1, lane, big), axis=1, keepdims=True)
    l2 = jnp.where(jnp.logical_and(in_grp, lane != i1), lg, neg)
    v2 = jnp.max(l2, axis=1, keepdims=True)
    i2 = jnp.min(jnp.where(l2 == v2, lane, big), axis=1, keepdims=True)
    e2 = jnp.exp(v2 - v1)
    p1 = 1.0 / (1.0 + e2)
    p2 = e2 * p1
    loc1 = (i1 - gidx * EXPERTS_PER_GROUP).astype(F32)
    loc2 = (i2 - gidx * EXPERTS_PER_GROUP).astype(F32)
    pidx = _pair_index(jnp.minimum(loc1, loc2), jnp.maximum(loc1, loc2))
    cls = gidx.astype(F32) * N_PAIRS + pidx
    a_loc, _ = _pair_slots(pidx)
    first_is_a = loc1 == a_loc
    wa = jnp.where(first_is_a, p1, p2) * w_grp
    wb = jnp.where(first_is_a, p2, p1) * w_grp
    record = jnp.where(lane == ROUTE_CLS, cls,
                       jnp.where(lane == ROUTE_WA, wa, jnp.where(lane == ROUTE_WB, wb, 0.0)))
    return record, cls


def _store_token_tiles(ref, x):
    rows, d = x.shape
    assert d == SUBLANES * LANES
    for s in range(SUBLANES):
        ref[pl.ds(s, rows, stride=SUBLANES), :] = x[:, s * LANES:(s + 1) * LANES]


def _load_token_tiles(ref):
    rows = ref.shape[0] // SUBLANES
    return jnp.concatenate([ref[pl.ds(s, rows, stride=SUBLANES), :] for s in range(SUBLANES)],
                           axis=1)


def _merge_kernel(x_ref, ysb_ref, yhg_ref, gsb_ref, ghg_ref, wbs_ref, wbh_ref, wo_ref,
                  ln_ref, wr_ref, br_ref, x1_ref, t_ref, route_ref, tot_ref,
                  wbs_scr, wbh_scr, wo_scr):
    @pl.when(pl.program_id(0) == 0)
    def _():
        wbs_scr[...] = wbs_ref[...].astype(BF16)
        wbh_scr[...] = wbh_ref[...].astype(BF16)
        wo_scr[...] = wo_ref[...].astype(BF16)
        tot_ref[...] = jnp.zeros_like(tot_ref)

    a = jnp.dot(ysb_ref[...], wbs_scr[...], preferred_element_type=F32)
    b = jnp.dot(yhg_ref[...], wbh_scr[...], preferred_element_type=F32)
    merged = jax.nn.sigmoid(gsb_ref[...]) * a + jax.nn.sigmoid(ghg_ref[...]) * b
    x1 = x_ref[...] + jnp.dot(merged.astype(BF16), wo_scr[...], preferred_element_type=F32)
    x1_ref[...] = x1
    var = jnp.mean(x1 * x1, axis=-1, keepdims=True)
    t = x1 * lax.rsqrt(var + EPS) * ln_ref[...]
    _store_token_tiles(t_ref, t)
    t_hi = t.astype(BF16)
    t_lo = (t - t_hi.astype(F32)).astype(BF16)
    p_hi = jnp.dot(t_hi, wr_ref[...], preferred_element_type=F32)
    p_lo = jnp.dot(t_lo, wr_ref[...], preferred_element_type=F32)
    lg = (p_hi[:, :LANES] + p_hi[:, LANES:]) + (p_lo[:, :LANES] + p_lo[:, LANES:]) + br_ref[...]
    route, cls = _route(lg)
    route_ref[...] = route
    lane = lax.broadcasted_iota(jnp.int32, route.shape, 1)
    sel = jnp.where(lane.astype(F32) == cls, 1.0, 0.0)
    tot_ref[...] += jnp.broadcast_to(jnp.sum(sel, axis=0, keepdims=True), tot_ref.shape)


def _merge(x2, y_sb, y_hg, rest, wbs, wbh, wo, ln_g, wr, br):
    n, d = x2.shape
    tm = min(MERGE_TM, n)
    w_sb, w_hg = y_sb.shape[1], y_hg.shape[1]
    gate_blk = (rest.shape[1] - 2 * d) // d
    row = lambda wdt, j=0: pl.BlockSpec((tm, wdt), lambda i, j=j: (i, j))
    const = lambda shape: pl.BlockSpec(shape, lambda i: (0, 0))
    return pl.pallas_call(
        _merge_kernel,
        grid=(n // tm,),
        in_specs=[row(d), row(w_sb), row(w_hg), row(d, gate_blk), row(d, gate_blk + 1),
                  const(wbs.shape), const(wbh.shape), const(wo.shape), const((1, d)),
                  const(wr.shape), const((1, LANES))],
        out_specs=[row(d), pl.BlockSpec((tm * SUBLANES, LANES), lambda i: (i, 0)), row(LANES),
                   const((SUBLANES, LANES))],
        out_shape=[jax.ShapeDtypeStruct((n, d), F32),
                   jax.ShapeDtypeStruct((n * SUBLANES, LANES), F32),
                   jax.ShapeDtypeStruct((n, LANES), F32),
                   jax.ShapeDtypeStruct((SUBLANES, LANES), F32)],
        scratch_shapes=[pltpu.VMEM(w.shape, BF16) for w in (wbs, wbh, wo)],
        compiler_params=_params(("arbitrary",)),
        name="merge",
    )(x2, y_sb, y_hg, rest, rest, wbs, wbh, wo, ln_g.reshape(1, d), wr, br)


TOKEN_ROWS_OUT = 2 * SUBLANES


def _plan_kernel(route_ref, tot_ref, ltri_ref, utri_ref, pos_ref, te_ref, meta_ref,
                 run_scr, *, tile_rows):
    i = pl.program_id(0)
    route = route_ref[...]
    tm = route.shape[0]
    lane = lax.broadcasted_iota(jnp.int32, (tm, LANES), 1)
    is_cls = lane.astype(F32) == route[:, ROUTE_CLS:ROUTE_CLS + 1]
    sel = jnp.where(is_cls, 1.0, 0.0)

    tot = tot_ref[0:1, :]
    top = tot + (tile_rows - 1.0)
    tiles = jnp.floor(top * (1.0 / tile_rows))
    tiles = jnp.where(tiles * tile_rows > top, tiles - 1.0, tiles)
    tiles = jnp.where((tiles + 1.0) * tile_rows <= top, tiles + 1.0, tiles)
    first_tile = _split_dot(jnp.broadcast_to(tiles, (SUBLANES, LANES)), utri_ref[...], 3)[0:1]

    @pl.when(i == 0)
    def _():
        run_scr[...] = jnp.zeros_like(run_scr)
        lane1 = lax.broadcasted_iota(jnp.int32, (1, LANES), 1)
        is_c = lane1 < N_CLASSES
        end_tile = first_tile + tiles
        n_valid = jnp.sum(jnp.where(is_c, tiles, 0.0), axis=1, keepdims=True)
        rows = te_ref.shape[0]
        j = lax.broadcasted_iota(jnp.int32, (rows, LANES), 0).astype(F32)
        j = jnp.minimum(j, n_valid - 1.0)
        lane2 = lax.broadcasted_iota(jnp.int32, (rows, LANES), 1)
        done = jnp.where(jnp.logical_and(lane2 < N_CLASSES, end_tile <= j), 1.0, 0.0)
        tc = jnp.sum(done, axis=1, keepdims=True)
        tg = jnp.floor((tc + 0.5) * (1.0 / N_PAIRS))
        ta, tb = _pair_slots(tc - tg * N_PAIRS)
        te = jnp.where(lane2 == 0, tg * EXPERTS_PER_GROUP + ta, tg * EXPERTS_PER_GROUP + tb)
        te_ref[...] = te.astype(jnp.int32)
        last_tile = jnp.where(jnp.logical_and(is_c, tiles > 0), end_tile - 1.0, -1.0)
        meta = jnp.where(lane1 == N_CLASSES, n_valid, last_tile)
        meta_ref[...] = jnp.broadcast_to(meta, meta_ref.shape).astype(jnp.int32)

    rank = jnp.dot(ltri_ref[...], sel.astype(BF16), preferred_element_type=F32) + run_scr[...]
    dest = jnp.sum(jnp.where(is_cls, first_tile * tile_rows + rank, 0.0), axis=1, keepdims=True)
    row = lax.broadcasted_iota(jnp.int32, (tm, LANES), 0)
    spread = jnp.where(lane == row % LANES, dest, 0.0)
    pos_ref[...] = jnp.sum(spread.reshape(tm // LANES, LANES, LANES), axis=1).astype(jnp.int32)
    run_scr[...] += jnp.sum(sel, axis=0, keepdims=True)


def _plan(route, tot, n_tiles_max):
    n = route.shape[0]
    tm = min(PLAN_TM, n)
    r = np.arange(tm)
    ltri = jnp.asarray((r[None, :] < r[:, None]).astype(np.float32), dtype=BF16)
    e = np.arange(LANES)
    utri = jnp.asarray((e[:, None] < e[None, :]).astype(np.float32), dtype=BF16)
    te_rows = -(-n_tiles_max // SUBLANES) * SUBLANES
    kern = functools.partial(_plan_kernel, tile_rows=MOE_TM)
    return pl.pallas_call(
        kern,
        grid=(n // tm,),
        in_specs=[pl.BlockSpec((tm, LANES), lambda i: (i, 0)),
                  pl.BlockSpec((SUBLANES, LANES), lambda i: (0, 0)),
                  pl.BlockSpec((tm, tm), lambda i: (0, 0)),
                  pl.BlockSpec((LANES, LANES), lambda i: (0, 0))],
        out_specs=[pl.BlockSpec((tm // LANES, LANES), lambda i: (i, 0)),
                   pl.BlockSpec((te_rows, LANES), lambda i: (0, 0)),
                   pl.BlockSpec((SUBLANES, LANES), lambda i: (0, 0))],
        out_shape=[jax.ShapeDtypeStruct((n // LANES, LANES), jnp.int32),
                   jax.ShapeDtypeStruct((te_rows, LANES), jnp.int32),
                   jax.ShapeDtypeStruct((SUBLANES, LANES), jnp.int32)],
        scratch_shapes=[pltpu.VMEM((1, LANES), F32)],
        compiler_params=_params(("arbitrary",)),
        name="plan",
    )(route, tot, ltri, utri)


DMA_UNROLL = 8


def _token_copy(src_ref, r, dst_ref, p, sem, rows):
    return pltpu.make_async_copy(src_ref.at[pl.ds(pl.multiple_of(r * rows, rows), rows)],
                                 dst_ref.at[pl.ds(pl.multiple_of(p * rows, rows), rows)], sem)


def _dispatch_kernel(pos_ref, meta_ref, t_ref, xs_ref, zero_scr, sem, tail_sem, *, tm, tile_rows):
    i = pl.program_id(0)
    tile_rows = tile_rows * SUBLANES
    n_tiles = xs_ref.shape[0] // tile_rows
    n_valid = meta_ref[N_CLASSES]

    def clear(tile, s):
        return pltpu.make_async_copy(zero_scr, xs_ref.at[pl.ds(tile * tile_rows, tile_rows)], s)

    def for_each(clears, fn):
        for tile, cond in clears:
            @pl.when(cond)
            def _():
                fn(tile)

    last_tiles = [(meta_ref[c], meta_ref[c] >= 0) for c in range(N_CLASSES)]
    past_tiles = [(n_tiles - 1 - k, n_tiles - 1 - k >= n_valid) for k in range(N_CLASSES)]

    @pl.when(i == 0)
    def _():
        zero_scr[...] = jnp.zeros_like(zero_scr)
        for_each(last_tiles, lambda tile: clear(tile, sem).start())
        for_each(past_tiles, lambda tile: clear(tile, tail_sem).start())
        for_each(last_tiles, lambda tile: clear(tile, sem).wait())

    base = i * tm

    def start(r2, carry):
        for k in range(2):
            r = 2 * r2 + k
            _token_copy(t_ref, r, xs_ref, pos_ref[base + r], sem, SUBLANES).start(priority=k)
        return carry

    lax.fori_loop(0, tm // 2, start, 0, unroll=DMA_UNROLL)
    pltpu.make_async_copy(t_ref, xs_ref.at[pl.ds(0, tm * SUBLANES)], sem).wait()

    @pl.when(i == pl.num_programs(0) - 1)
    def _():
        for_each(past_tiles, lambda tile: clear(tile, tail_sem).wait())


def _dispatch(pos, meta, t, n_rows):
    n = t.shape[0] // SUBLANES
    tm = min(DISPATCH_TM, n)
    kern = functools.partial(_dispatch_kernel, tm=tm, tile_rows=MOE_TM)
    grid_spec = pltpu.PrefetchScalarGridSpec(
        num_scalar_prefetch=2,
        grid=(n // tm,),
        in_specs=[pl.BlockSpec((tm * SUBLANES, LANES), lambda i, *_: (i, 0))],
        out_specs=pl.BlockSpec(memory_space=pl.ANY),
        scratch_shapes=[pltpu.VMEM((MOE_TM * SUBLANES, LANES), F32),
                        pltpu.SemaphoreType.DMA(()), pltpu.SemaphoreType.DMA(())],
    )
    return pl.pallas_call(
        kern,
        grid_spec=grid_spec,
        out_shape=jax.ShapeDtypeStruct((n_rows * SUBLANES, LANES), F32),
        compiler_params=_params(("arbitrary",)),
        name="dispatch",
    )(pos, meta, t)


def _expert_kernel(tea_ref, teb_ref, nv_ref, xs_ref, wga, wua, wda, wgb, wub, wdb, ys_ref):
    in_use = pl.program_id(0) < nv_ref[0]
    rows = xs_ref.shape[0] // SUBLANES

    @pl.when(in_use)
    def _():
        x = _load_token_tiles(xs_ref).astype(BF16)
        for slot, (wg, wu, wd) in enumerate(((wga, wua, wda), (wgb, wub, wdb))):
            hg = jnp.dot(x, wg[...].astype(BF16), preferred_element_type=F32)
            hu = jnp.dot(x, wu[...].astype(BF16), preferred_element_type=F32)
            a = (hg * jax.nn.sigmoid(hg) * hu).astype(BF16)
            y = jnp.dot(a, wd[...].astype(BF16), preferred_element_type=F32)
            for s in range(SUBLANES):
                ys_ref[pl.ds(slot * SUBLANES + s, rows, stride=TOKEN_ROWS_OUT), :] = (
                    y[:, s * LANES:(s + 1) * LANES])

    @pl.when(jnp.logical_not(in_use))
    def _():
        ys_ref[...] = jnp.zeros_like(ys_ref)


def _experts(tea, teb, nv, xs, wg, wu, wd):
    _, d, de = wg.shape
    assert d == SUBLANES * LANES
    n_tiles = xs.shape[0] // (MOE_TM * SUBLANES)
    wa = lambda shape: pl.BlockSpec(shape, lambda j, tea, teb, nv: (tea[j], 0, 0))
    wb = lambda shape: pl.BlockSpec(shape, lambda j, tea, teb, nv: (teb[j], 0, 0))
    grid_spec = pltpu.PrefetchScalarGridSpec(
        num_scalar_prefetch=3,
        grid=(n_tiles,),
        in_specs=[pl.BlockSpec((MOE_TM * SUBLANES, LANES),
                               lambda j, tea, teb, nv: (jnp.minimum(j, nv[0] - 1), 0)),
                  wa((None, d, de)), wa((None, d, de)), wa((None, de, d)),
                  wb((None, d, de)), wb((None, d, de)), wb((None, de, d))],
        out_specs=pl.BlockSpec((MOE_TM * TOKEN_ROWS_OUT, LANES), lambda j, *_: (j, 0)),
    )
    return pl.pallas_call(
        _expert_kernel,
        grid_spec=grid_spec,
        out_shape=jax.ShapeDtypeStruct((n_tiles * MOE_TM * TOKEN_ROWS_OUT, LANES), F32),
        compiler_params=_params(("arbitrary",)),
        name="experts",
    )(tea, teb, nv, xs, wg, wu, wd, wg, wu, wd)


def _combine_kernel(pos_ref, ys_ref, x1_ref, route_ref, fg_ref, o_ref, g, sem, *, tm):
    i = pl.program_id(0)
    slot = i % 2

    def gather(step, buf):
        base = step * tm

        def body(r2, carry):
            for k in range(2):
                r = 2 * r2 + k
                _token_copy(ys_ref, pos_ref[base + r], g.at[buf], r, sem.at[buf],
                            TOKEN_ROWS_OUT).start(priority=k)
            return carry

        lax.fori_loop(0, tm // 2, body, 0, unroll=DMA_UNROLL)

    @pl.when(i == 0)
    def _():
        gather(0, 0)

    @pl.when(i + 1 < pl.num_programs(0))
    def _():
        gather(i + 1, 1 - slot)

    pltpu.make_async_copy(ys_ref.at[pl.ds(0, tm * TOKEN_ROWS_OUT)], g.at[slot],
                          sem.at[slot]).wait()
    gs = g.at[slot]
    ya, yb = (jnp.concatenate([gs[pl.ds(off + s, tm, stride=TOKEN_ROWS_OUT), :]
                               for s in range(SUBLANES)], axis=1) for off in (0, SUBLANES))
    route = route_ref[...]
    x2 = x1_ref[...] + (route[:, ROUTE_WA:ROUTE_WA + 1] * ya + route[:, ROUTE_WB:ROUTE_WB + 1] * yb)
    var = jnp.mean(x2 * x2, axis=-1, keepdims=True)
    o_ref[...] = x2 * lax.rsqrt(var + EPS) * fg_ref[...]


def _combine(pos, ys, x1, route, final_g):
    n, d = x1.shape
    tm = min(COMBINE_TM, n)
    kern = functools.partial(_combine_kernel, tm=tm)
    grid_spec = pltpu.PrefetchScalarGridSpec(
        num_scalar_prefetch=1,
        grid=(n // tm,),
        in_specs=[pl.BlockSpec(memory_space=pl.ANY),
                  pl.BlockSpec((tm, d), lambda i, *_: (i, 0)),
                  pl.BlockSpec((tm, LANES), lambda i, *_: (i, 0)),
                  pl.BlockSpec((1, d), lambda i, *_: (0, 0))],
        out_specs=pl.BlockSpec((tm, d), lambda i, *_: (i, 0)),
        scratch_shapes=[pltpu.VMEM((2, tm * TOKEN_ROWS_OUT, LANES), F32),
                        pltpu.SemaphoreType.DMA((2,))],
    )
    return pl.pallas_call(
        kern,
        grid_spec=grid_spec,
        out_shape=jax.ShapeDtypeStruct((n, d), F32),
        compiler_params=_params(("arbitrary",)),
        name="combine",
    )(pos, ys, x1, route, final_g.reshape(1, d))


def _moe_sparse(t, route, tot, wg, wu, wd, x1, final_g):
    n = x1.shape[0]
    n_tiles_max = (n + N_CLASSES * (MOE_TM - 1)) // MOE_TM
    pos, te, meta = _plan(route, tot, n_tiles_max)
    pos = pos.reshape(n)
    xs = _dispatch(pos, meta[0, :N_CLASSES + 1], t, n_tiles_max * MOE_TM)
    ys = _experts(te[:n_tiles_max, 0], te[:n_tiles_max, 1], meta[0, N_CLASSES:N_CLASSES + 1],
                  xs, wg, wu, wd)
    return _combine(pos, ys, x1, route, final_g)


def _suffix_ones(t):
    j = np.arange(t)[:, None]
    s = np.arange(t)[None, :]
    return jnp.asarray((j > s).astype(np.float32), dtype=BF16)


def _causal_bias(t):
    row = np.arange(t)[:, None]
    col = np.arange(t)[None, :]
    diag = np.where(col < row, 0.0, MASK_BIAS).astype(np.float32)
    return jnp.asarray(np.stack([np.zeros_like(diag), diag]))


def _chunk_prefix_ones(tt, c):
    t = np.arange(tt)[:, None]
    j = np.arange(tt)[None, :]
    return jnp.asarray(((j <= t) & (t // c == j // c)).astype(np.float32), dtype=BF16)


def _block_diag_ones(w, blk):
    a = np.arange(w)
    return jnp.asarray((a[:, None] // blk == a[None, :] // blk).astype(np.float32), dtype=BF16)


def kernel(x, ln1_g, w_in, w_branch_sb, w_branch_hg, hg_norm_g, hg_lb_logits, w_out, ln2_g,
           w_router_group, b_router_group, w_router_expert, b_router_expert,
           w_exp_gate, w_exp_up, w_exp_down, final_g):
    bsz, seq, d = x.shape
    depth = w_in.shape[0]
    n = bsz * seq
    sb_width = SB_HEADS * SB_HEAD_DIM
    hg_width = HG_HEADS * HG_DIM

    tri_attn = _suffix_ones(min(ATTN_T, seq))
    bias_attn = _causal_bias(min(ATTN_T, seq))
    tt = min(HGRN_TT, seq)
    tri_hg = _chunk_prefix_ones(tt, min(HGRN_C, tt))
    bd = _block_diag_ones(min(MXU_DIM, hg_width), HG_DIM)

    x2 = x.reshape(n, d)
    for l in range(depth):
        qkv, rest = _inproj(x2, ln1_g[l], w_in[l], sb_width)
        y_sb = _attn(qkv, bsz, seq, tri_attn, bias_attn)
        y_hg = _hgrn(rest, hg_lb_logits, hg_norm_g[l], bsz, seq, l, tri_hg, bd)

        pad = LANES - N_EXPERTS - N_GROUPS
        wr = jnp.concatenate([w_router_expert[l], w_router_group[l],
                              jnp.zeros((d, pad), F32)], axis=1)
        wr_hi = wr.astype(BF16)
        wr_lo = (wr - wr_hi.astype(F32)).astype(BF16)
        wr_split = jnp.concatenate([wr_hi, wr_lo], axis=1)
        br = jnp.concatenate([b_router_expert[l], b_router_group[l],
                              jnp.zeros((pad,), F32)]).reshape(1, LANES)

        last = l == depth - 1
        x1, t, route, tot = _merge(x2, y_sb, y_hg, rest, w_branch_sb[l], w_branch_hg[l],
                                   w_out[l], ln2_g[l], wr_split, br)
        assert last, "final rmsnorm is fused into the last layer's combine kernel"
        x2 = _moe_sparse(t, route, tot, w_exp_gate[l], w_exp_up[l], w_exp_down[l], x1, final_g)
    return x2.reshape(bsz, seq, d)
```

```python
import functools

import jax
import jax.numpy as jnp
import numpy as np
from jax import lax
from jax.experimental import pallas as pl
from jax.experimental.pallas import tpu as pltpu

F32 = jnp.float32
BF16 = jnp.bfloat16

EPS = 1e-6
SB_HEADS = 8
SB_HEAD_DIM = 64
HG_HEADS = 8
HG_DIM = 64
N_GROUPS = 4
EXPERTS_PER_GROUP = 4
N_EXPERTS = N_GROUPS * EXPERTS_PER_GROUP

LANES = 128
SUBLANES = 8
MXU_DIM = 256
LOG2E = 1.4426950408889634
VMEM_LIMIT = 56 * 1024 * 1024

INPROJ_TM = 2048
INPROJ_TN = 512
ATTN_T = MXU_DIM
HGRN_TT = 256
HGRN_C = 32
MERGE_TM = 512
MOE_TM = 384
PLAN_TM = 1024
DISPATCH_TM = 1024
COMBINE_TM = 512


def _params(sem):
    return pltpu.CompilerParams(dimension_semantics=sem, vmem_limit_bytes=VMEM_LIMIT)


def _split_dot(x, m, passes):
    acc = None
    r = x
    for p in range(passes):
        h = r.astype(BF16)
        term = jnp.dot(h, m, preferred_element_type=F32)
        acc = term if acc is None else acc + term
        if p + 1 < passes:
            r = r - h.astype(F32)
    return acc


def _inproj_kernel(x_ref, g_ref, w_ref, qkv_ref, rest_ref, h_scr, w_scr, *, q_scale):
    i = pl.program_id(0)
    j = pl.program_id(1)

    @pl.when(j == 0)
    def _():
        x = x_ref[...]
        var = jnp.mean(x * x, axis=-1, keepdims=True)
        h_scr[...] = (x * lax.rsqrt(var + EPS) * g_ref[...]).astype(BF16)

    @pl.when(i == 0)
    def _():
        w_scr[j] = w_ref[...].astype(BF16)

    acc = jnp.dot(h_scr[...], w_scr[j], preferred_element_type=F32)
    qkv_ref[...] = (acc * jnp.where(j == 0, q_scale, 1.0)).astype(BF16)
    rest_ref[...] = acc


def _inproj(x2, ln_g, w_in, sb_width):
    n, d = x2.shape
    cols = w_in.shape[1]
    tm, tn = min(INPROJ_TM, n), INPROJ_TN
    assert sb_width == tn, "q block must be exactly one column tile"
    nq = 3 * sb_width // tn
    nj = cols // tn
    kern = functools.partial(_inproj_kernel, q_scale=SB_HEAD_DIM ** -0.5)
    return pl.pallas_call(
        kern,
        grid=(n // tm, nj),
        in_specs=[
            pl.BlockSpec((tm, d), lambda i, j: (i, 0)),
            pl.BlockSpec((1, d), lambda i, j: (0, 0)),
            pl.BlockSpec((d, tn), lambda i, j: (0, jnp.where(i == 0, j, nj - 1))),
        ],
        out_specs=[
            pl.BlockSpec((tm, tn), lambda i, j: (i, jnp.minimum(j, nq))),
            pl.BlockSpec((tm, tn), lambda i, j: (i, jnp.maximum(j - nq, 0))),
        ],
        out_shape=[
            jax.ShapeDtypeStruct((n, (nq + 1) * tn), BF16),
            jax.ShapeDtypeStruct((n, cols - nq * tn), F32),
        ],
        scratch_shapes=[pltpu.VMEM((tm, d), BF16), pltpu.VMEM((nj, d, tn), BF16)],
        compiler_params=_params(("arbitrary", "arbitrary")),
        name="inproj",
    )(x2, ln_g.reshape(1, d), w_in)


ATTN_STAGES = 3
ATTN_STREAMS = 2
MASK_BIAS = -1e30
ATTN_SKIP = 111.0


def _attn_kernel(q_ref, k_ref, v_ref, tri_ref, bias_ref, o_ref,
                 z0, z1, z2, i0, i1, i2, acc_ref, c_ref, *, t, nq):
    zbuf = (z0, z1, z2)
    ibuf = (i0, i1, i2)
    for r in zbuf + ibuf + (acc_ref, c_ref):
        r[...] = jnp.zeros_like(r)

    lane = lax.broadcasted_iota(jnp.int32, (t, LANES), 1)
    head0 = lane < SB_HEAD_DIM
    nt = (((1,), (1,)), ((), ()))

    def stage_a(qi, kj, slot):
        q = q_ref[pl.ds(pl.multiple_of(qi * t, t), t), :]
        zero = jnp.zeros_like(q)
        q2 = jnp.concatenate([jnp.where(head0, q, zero), jnp.where(head0, zero, q)], axis=0)
        k = k_ref[pl.ds(pl.multiple_of(kj * t, t), t), :]
        z = lax.dot_general(q2, k, nt, preferred_element_type=F32)
        bias = bias_ref[(qi == kj).astype(jnp.int32)]
        zbuf[slot][...] = z + jnp.concatenate([bias, bias], axis=0)

    def stage_b(slot):
        z = zbuf[slot][...]
        p = jnp.maximum(z, 0.0) + jnp.log(1.0 + jnp.exp2(jnp.abs(z) * (-LOG2E)))
        incl = jnp.dot(p.astype(BF16), tri_ref[...], preferred_element_type=F32) + p
        ibuf[slot][...] = incl
        return incl[:, 0:1]

    def stage_c(qi, kj, slot, stream):
        first = qi == kj
        incl = ibuf[slot][...]
        c = jnp.where(first, 0.0, c_ref[stream])
        a = jnp.exp2((zbuf[slot][...] - incl - c) * LOG2E)
        v = v_ref[pl.ds(pl.multiple_of(kj * t, t), t), :]
        pv = jnp.dot(a.astype(BF16), v, preferred_element_type=F32)
        acc = jnp.where(first, pv, acc_ref[stream] + pv)
        acc_ref[stream] = acc
        c_ref[stream] = c + incl[:, 0:1]
        o_ref[pl.ds(pl.multiple_of(qi * t, t), t), :] = (
            jnp.where(head0, acc[0:t], acc[t:2 * t]).astype(o_ref.dtype))

    def block(carry):
        qs, ks, qn, kn, skip_q, drained = carry
        qs, ks, qn, kn, skip_q = list(qs), list(ks), list(qn), list(kn), list(skip_q)
        all_done = qn[0] >= nq
        for p in range(1, ATTN_STREAMS):
            all_done = jnp.logical_and(all_done, qn[p] >= nq)
        drained = drained + all_done.astype(jnp.int32)
        for r in range(ATTN_STAGES * ATTN_STREAMS):
            sa, sc, sb = r % ATTN_STAGES, (r + 1) % ATTN_STAGES, (r + 2) % ATTN_STAGES
            pa, pb = r % ATTN_STREAMS, (r - 1) % ATTN_STREAMS
            stage_c(qs[sc], ks[sc], sc, pa)
            total_b = stage_b(sb)
            carry_b = jnp.where(qs[sb] == ks[sb], 0.0, c_ref[pb]) + total_b
            skip = skip_q[pa] == qn[pa]
            q_cur = jnp.where(skip, qn[pa] + ATTN_STREAMS, qn[pa])
            k_cur = jnp.where(skip, qn[pa] + ATTN_STREAMS, kn[pa])
            drain = q_cur >= nq
            qa = jnp.where(drain, 0, q_cur)
            ka = jnp.where(drain, 0, k_cur)
            stage_a(qa, ka, sa)
            last = k_cur == 0
            qn[pa] = jnp.where(jnp.logical_and(last, jnp.logical_not(drain)),
                               q_cur + ATTN_STREAMS, q_cur)
            kn[pa] = jnp.where(drain, k_cur, jnp.where(last, q_cur + ATTN_STREAMS, k_cur - 1))
            skip_q[pb] = jnp.where(jnp.min(carry_b) >= ATTN_SKIP, qs[sb], skip_q[pb])
            qs[sa], ks[sa] = qa, ka
        return tuple(qs), tuple(ks), tuple(qn), tuple(kn), tuple(skip_q), drained

    zero = jnp.int32(0)
    first_q = tuple(jnp.int32(p) for p in range(ATTN_STREAMS))
    init = ((zero,) * ATTN_STAGES, (zero,) * ATTN_STAGES, first_q, first_q,
            (jnp.int32(-1),) * ATTN_STREAMS, zero)
    lax.while_loop(lambda carry: carry[5] < 1, block, init)


def _attn(qkv, bsz, seq, tri, bias):
    n = bsz * seq
    t = min(ATTN_T, seq)
    pairs = SB_HEADS * SB_HEAD_DIM // LANES
    kern = functools.partial(_attn_kernel, t=t, nq=seq // t)
    return pl.pallas_call(
        kern,
        grid=(bsz, pairs),
        in_specs=[
            pl.BlockSpec((seq, LANES), lambda b, p: (b, p)),
            pl.BlockSpec((seq, LANES), lambda b, p: (b, pairs + p)),
            pl.BlockSpec((seq, LANES), lambda b, p: (b, 2 * pairs + p)),
            pl.BlockSpec((t, t), lambda b, p: (0, 0)),
            pl.BlockSpec((2, t, t), lambda b, p: (0, 0, 0)),
        ],
        out_specs=pl.BlockSpec((seq, LANES), lambda b, p: (b, p)),
        out_shape=jax.ShapeDtypeStruct((n, pairs * LANES), BF16),
        scratch_shapes=([pltpu.VMEM((2 * t, t), F32)] * (2 * ATTN_STAGES)
                        + [pltpu.VMEM((ATTN_STREAMS, 2 * t, LANES), F32),
                           pltpu.VMEM((ATTN_STREAMS, 2 * t, 1), F32)]),
        compiler_params=_params(("parallel", "parallel")),
        name="attn",
    )(qkv, qkv, qkv, tri, bias)


def _group_dot(x, bd):
    g = bd.shape[0]
    parts = [jnp.dot(x[:, i:i + g], bd, preferred_element_type=F32)
             for i in range(0, x.shape[1], g)]
    return jnp.concatenate(parts, axis=1)


def _hgrn_kernel(q_ref, f_ref, i_ref, g_ref, lbl_ref, ng_ref, tri_ref, bd_ref, o_ref,
                 st_scr, wpad, vpad, wsh, vsh, *, tt, c, layer):
    ti = pl.program_id(1)
    w = q_ref.shape[1]
    nch = tt // c

    @pl.when(ti == 0)
    def _():
        st_scr[...] = jnp.zeros_like(st_scr)

    lg = lbl_ref[...]
    e = jnp.exp(lg - jnp.max(lg, axis=0, keepdims=True))
    lb = jnp.sum(e[0:layer + 1], axis=0, keepdims=True) / jnp.sum(e, axis=0, keepdims=True)

    f = lb + (1.0 - lb) * jax.nn.sigmoid(f_ref[...])
    kk = 1.0 - f
    qv = q_ref[...]
    qs = qv * jax.nn.sigmoid(qv)
    v = i_ref[...]
    bd = bd_ref[...]
    cum = _split_dot_left(tri_ref[...], jnp.log(f) * LOG2E, 3)
    wk = jnp.log(kk) * LOG2E - cum

    pos = lax.broadcasted_iota(jnp.int32, (tt, 1), 0) % c
    wpad[0:SUBLANES, :] = jnp.zeros((SUBLANES, w), F32)
    vpad[0:SUBLANES, :] = jnp.zeros((SUBLANES, w), F32)
    wpad[SUBLANES:SUBLANES + tt, :] = wk
    vpad[SUBLANES:SUBLANES + tt, :] = v
    for b in range(SUBLANES):
        wb = wpad[SUBLANES - b:SUBLANES - b + tt, :]
        vb = vpad[SUBLANES - b:SUBLANES - b + tt, :]
        if b:
            wb = jnp.where(pos >= b, wb, -jnp.inf)
        wsh[b] = wb.reshape(nch, c, w)
        vsh[b] = vb.reshape(nch, c, w)

    qs3 = qs.reshape(nch, c, w)
    cum3 = cum.reshape(nch, c, w)
    acc = None
    for a in reversed(range(c // SUBLANES)):
        rows = c - SUBLANES * a
        qa = qs3[:, SUBLANES * a:, :].reshape(nch * rows, w)
        ca = cum3[:, SUBLANES * a:, :].reshape(nch * rows, w)
        acc_a = None
        for b in range(SUBLANES):
            wb = wsh[b, :, 0:rows, :].reshape(nch * rows, w)
            vb = vsh[b, :, 0:rows, :].reshape(nch * rows, w)
            dd = qa * jnp.exp2(ca + wb)
            term = _group_dot(dd.astype(BF16), bd) * vb
            acc_a = term if acc_a is None else acc_a + term
        acc_a = acc_a.reshape(nch, rows, w)
        if acc is not None:
            acc_a = acc_a + jnp.concatenate([jnp.zeros((nch, SUBLANES, w), F32), acc], axis=1)
        acc = acc_a
    acc = acc.reshape(tt, w)

    grp = st_scr.shape[1]
    bdmask = bd[0:grp, 0:grp] != 0
    outs = []
    for ci in range(nch):
        r0 = ci * c
        cum_c = cum[r0:r0 + c]
        last = cum_c[c - 1:c]
        qd = (qs[r0:r0 + c] * jnp.exp2(cum_c)).astype(BF16)
        kd = (kk[r0:r0 + c] * jnp.exp2(last - cum_c)).astype(BF16)
        vc = v[r0:r0 + c].astype(BF16)
        dec = jnp.exp2(last)
        o_parts = []
        for gi in range(w // grp):
            sl = slice(gi * grp, (gi + 1) * grp)
            st = st_scr[gi]
            o_parts.append(lax.dot_general(qd[:, sl], st.astype(BF16), (((1,), (1,)), ((), ())),
                                           preferred_element_type=F32))
            upd = lax.dot_general(vc[:, sl], kd[:, sl], (((0,), (0,)), ((), ())),
                                  preferred_element_type=F32)
            st_scr[gi] = st * dec[:, sl] + jnp.where(bdmask, upd, 0.0)
        outs.append(jnp.concatenate(o_parts, axis=1))
    o = acc + jnp.concatenate(outs, axis=0)

    o2 = o * o
    o2_hi = o2.astype(BF16)
    o2_lo = (o2 - o2_hi.astype(F32)).astype(BF16)
    ms = (_group_dot(o2_hi, bd) + _group_dot(o2_lo, bd)) * (1.0 / HG_DIM)
    gv = g_ref[...]
    o = o * lax.rsqrt(ms + EPS) * ng_ref[...] * (gv * jax.nn.sigmoid(gv))
    o_ref[...] = o.astype(o_ref.dtype)


def _split_dot_left(m, x, passes):
    acc = None
    r = x
    for p in range(passes):
        h = r.astype(BF16)
        term = jnp.dot(m, h, preferred_element_type=F32)
        acc = term if acc is None else acc + term
        if p + 1 < passes:
            r = r - h.astype(F32)
    return acc


def _hgrn(rest, lb_logits, norm_g, bsz, seq, layer, tri, bd):
    n = bsz * seq
    w = HG_HEADS * HG_DIM
    grp = bd.shape[0]
    tt = min(HGRN_TT, seq)
    c = min(HGRN_C, tt)
    nt = seq // tt
    kern = functools.partial(_hgrn_kernel, tt=tt, c=c, layer=layer)
    col = lambda j: pl.BlockSpec((tt, w), lambda b, i, j=j: (b * nt + i, j))
    const = lambda shape: pl.BlockSpec(shape, lambda b, i: (0, 0))
    return pl.pallas_call(
        kern,
        grid=(bsz, nt),
        in_specs=[col(0), col(1), col(2), col(3),
                  const(lb_logits.shape), const((1, w)), const((tt, tt)), const((grp, grp))],
        out_specs=pl.BlockSpec((tt, w), lambda b, i: (b * nt + i, 0)),
        out_shape=jax.ShapeDtypeStruct((n, w), BF16),
        scratch_shapes=([pltpu.VMEM((w // LANES, LANES, LANES), F32)]
                        + [pltpu.VMEM((SUBLANES + tt, w), F32)] * 2
                        + [pltpu.VMEM((SUBLANES, tt // c, c, w), F32)] * 2),
        compiler_params=_params(("parallel", "arbitrary")),
        name="hgrn",
    )(rest, rest, rest, rest, lb_logits, norm_g.reshape(1, w), tri, bd)


PAIR_A = (0, 0, 0, 1, 1, 3)
PAIR_B = (1, 2, 3, 3, 2, 2)
N_PAIRS = len(PAIR_A)
N_CLASSES = N_GROUPS * N_PAIRS
assert sorted(tuple(sorted(p)) for p in zip(PAIR_A, PAIR_B)) == [
    (a, b) for a in range(EXPERTS_PER_GROUP) for b in range(a + 1, EXPERTS_PER_GROUP)]
ROUTE_CLS, ROUTE_WA, ROUTE_WB = 0, 1, 2


def _pair_slots(pidx):
    a = b = jnp.zeros_like(pidx)
    for p in range(N_PAIRS):
        a = jnp.where(pidx == p, float(PAIR_A[p]), a)
        b = jnp.where(pidx == p, float(PAIR_B[p]), b)
    return a, b


def _pair_index(lo, hi):
    pidx = jnp.zeros_like(lo)
    for p in range(N_PAIRS):
        is_p = jnp.logical_and(lo == min(PAIR_A[p], PAIR_B[p]), hi == max(PAIR_A[p], PAIR_B[p]))
        pidx = jnp.where(is_p, float(p), pidx)
    return pidx


def _route(lg):
    lane = lax.broadcasted_iota(jnp.int32, lg.shape, 1)
    neg = jnp.float32(-jnp.inf)
    big = jnp.int32(LANES)
    gmask = jnp.logical_and(lane >= N_EXPERTS, lane < N_EXPERTS + N_GROUPS)
    gl = jnp.where(gmask, lg, neg)
    gmax = jnp.max(gl, axis=1, keepdims=True)
    gidx = jnp.min(jnp.where(gl == gmax, lane, big), axis=1, keepdims=True) - N_EXPERTS
    w_grp = 1.0 / jnp.sum(jnp.where(gmask, jnp.exp(gl - gmax), 0.0), axis=1, keepdims=True)
    in_grp = jnp.logical_and(lane < N_EXPERTS, lane // EXPERTS_PER_GROUP == gidx)
    l1 = jnp.where(in_grp, lg, neg)
    v1 = jnp.max(l1, axis=1, keepdims=True)
    i1 = jnp.min(jnp.where(l1 == v1, lane, big), axis=1, keepdims=True)
    l2 = jnp.where(jnp.logical_and(in_grp, lane != i1), lg, neg)
    v2 = jnp.max(l2, axis=1, keepdims=True)
    i2 = jnp.min(jnp.where(l2 == v2, lane, big), axis=1, keepdims=True)
    e2 = jnp.exp(v2 - v1)
    p1 = 1.0 / (1.0 + e2)
    p2 = e2 * p1
    loc1 = (i1 - gidx * EXPERTS_PER_GROUP).astype(F32)
    loc2 = (i2 - gidx * EXPERTS_PER_GROUP).astype(F32)
    pidx = _pair_index(jnp.minimum(loc1, loc2), jnp.maximum(loc1, loc2))
    cls = gidx.astype(F32) * N_PAIRS + pidx
    a_loc, _ = _pair_slots(pidx)
    first_is_a = loc1 == a_loc
    wa = jnp.where(first_is_a, p1, p2) * w_grp
    wb = jnp.where(first_is_a, p2, p1) * w_grp
    record = jnp.where(lane == ROUTE_CLS, cls,
                       jnp.where(lane == ROUTE_WA, wa, jnp.where(lane == ROUTE_WB, wb, 0.0)))
    return record, cls


def _store_token_tiles(ref, x):
    rows, d = x.shape
    assert d == SUBLANES * LANES
    for s in range(SUBLANES):
        ref[pl.ds(s, rows, stride=SUBLANES), :] = x[:, s * LANES:(s + 1) * LANES]


def _load_token_tiles(ref):
    rows = ref.shape[0] // SUBLANES
    return jnp.concatenate([ref[pl.ds(s, rows, stride=SUBLANES), :] for s in range(SUBLANES)],
                           axis=1)


def _merge_kernel(x_ref, ysb_ref, yhg_ref, gsb_ref, ghg_ref, wbs_ref, wbh_ref, wo_ref,
                  ln_ref, wr_ref, br_ref, x1_ref, t_ref, route_ref, tot_ref,
                  wbs_scr, wbh_scr, wo_scr):
    @pl.when(pl.program_id(0) == 0)
    def _():
        wbs_scr[...] = wbs_ref[...].astype(BF16)
        wbh_scr[...] = wbh_ref[...].astype(BF16)
        wo_scr[...] = wo_ref[...].astype(BF16)
        tot_ref[...] = jnp.zeros_like(tot_ref)

    a = jnp.dot(ysb_ref[...], wbs_scr[...], preferred_element_type=F32)
    b = jnp.dot(yhg_ref[...], wbh_scr[...], preferred_element_type=F32)
    merged = jax.nn.sigmoid(gsb_ref[...]) * a + jax.nn.sigmoid(ghg_ref[...]) * b
    x1 = x_ref[...] + jnp.dot(merged.astype(BF16), wo_scr[...], preferred_element_type=F32)
    x1_ref[...] = x1
    var = jnp.mean(x1 * x1, axis=-1, keepdims=True)
    t = x1 * lax.rsqrt(var + EPS) * ln_ref[...]
    _store_token_tiles(t_ref, t)
    t_hi = t.astype(BF16)
    t_lo = (t - t_hi.astype(F32)).astype(BF16)
    p_hi = jnp.dot(t_hi, wr_ref[...], preferred_element_type=F32)
    p_lo = jnp.dot(t_lo, wr_ref[...], preferred_element_type=F32)
    lg = (p_hi[:, :LANES] + p_hi[:, LANES:]) + (p_lo[:, :LANES] + p_lo[:, LANES:]) + br_ref[...]
    route, cls = _route(lg)
    route_ref[...] = route
    lane = lax.broadcasted_iota(jnp.int32, route.shape, 1)
    sel = jnp.where(lane.astype(F32) == cls, 1.0, 0.0)
    tot_ref[...] += jnp.broadcast_to(jnp.sum(sel, axis=0, keepdims=True), tot_ref.shape)


def _merge(x2, y_sb, y_hg, rest, wbs, wbh, wo, ln_g, wr, br):
    n, d = x2.shape
    tm = min(MERGE_TM, n)
    w_sb, w_hg = y_sb.shape[1], y_hg.shape[1]
    gate_blk = (rest.shape[1] - 2 * d) // d
    row = lambda wdt, j=0: pl.BlockSpec((tm, wdt), lambda i, j=j: (i, j))
    const = lambda shape: pl.BlockSpec(shape, lambda i: (0, 0))
    return pl.pallas_call(
        _merge_kernel,
        grid=(n // tm,),
        in_specs=[row(d), row(w_sb), row(w_hg), row(d, gate_blk), row(d, gate_blk + 1),
                  const(wbs.shape), const(wbh.shape), const(wo.shape), const((1, d)),
                  const(wr.shape), const((1, LANES))],
        out_specs=[row(d), pl.BlockSpec((tm * SUBLANES, LANES), lambda i: (i, 0)), row(LANES),
                   const((SUBLANES, LANES))],
        out_shape=[jax.ShapeDtypeStruct((n, d), F32),
                   jax.ShapeDtypeStruct((n * SUBLANES, LANES), F32),
                   jax.ShapeDtypeStruct((n, LANES), F32),
                   jax.ShapeDtypeStruct((SUBLANES, LANES), F32)],
        scratch_shapes=[pltpu.VMEM(w.shape, BF16) for w in (wbs, wbh, wo)],
        compiler_params=_params(("arbitrary",)),
        name="merge",
    )(x2, y_sb, y_hg, rest, rest, wbs, wbh, wo, ln_g.reshape(1, d), wr, br)


TOKEN_ROWS_OUT = 2 * SUBLANES


def _plan_kernel(route_ref, tot_ref, ltri_ref, utri_ref, pos_ref, te_ref, meta_ref,
                 run_scr, *, tile_rows):
    i = pl.program_id(0)
    route = route_ref[...]
    tm = route.shape[0]
    lane = lax.broadcasted_iota(jnp.int32, (tm, LANES), 1)
    is_cls = lane.astype(F32) == route[:, ROUTE_CLS:ROUTE_CLS + 1]
    sel = jnp.where(is_cls, 1.0, 0.0)

    tot = tot_ref[0:1, :]
    top = tot + (tile_rows - 1.0)
    tiles = jnp.floor(top * (1.0 / tile_rows))
    tiles = jnp.where(tiles * tile_rows > top, tiles - 1.0, tiles)
    tiles = jnp.where((tiles + 1.0) * tile_rows <= top, tiles + 1.0, tiles)
    first_tile = _split_dot(jnp.broadcast_to(tiles, (SUBLANES, LANES)), utri_ref[...], 3)[0:1]

    @pl.when(i == 0)
    def _():
        run_scr[...] = jnp.zeros_like(run_scr)
        lane1 = lax.broadcasted_iota(jnp.int32, (1, LANES), 1)
        is_c = lane1 < N_CLASSES
        end_tile = first_tile + tiles
        n_valid = jnp.sum(jnp.where(is_c, tiles, 0.0), axis=1, keepdims=True)
        rows = te_ref.shape[0]
        j = lax.broadcasted_iota(jnp.int32, (rows, LANES), 0).astype(F32)
        j = jnp.minimum(j, n_valid - 1.0)
        lane2 = lax.broadcasted_iota(jnp.int32, (rows, LANES), 1)
        done = jnp.where(jnp.logical_and(lane2 < N_CLASSES, end_tile <= j), 1.0, 0.0)
        tc = jnp.sum(done, axis=1, keepdims=True)
        tg = jnp.floor((tc + 0.5) * (1.0 / N_PAIRS))
        ta, tb = _pair_slots(tc - tg * N_PAIRS)
        te = jnp.where(lane2 == 0, tg * EXPERTS_PER_GROUP + ta, tg * EXPERTS_PER_GROUP + tb)
        te_ref[...] = te.astype(jnp.int32)
        last_tile = jnp.where(jnp.logical_and(is_c, tiles > 0), end_tile - 1.0, -1.0)
        meta = jnp.where(lane1 == N_CLASSES, n_valid, last_tile)
        meta_ref[...] = jnp.broadcast_to(meta, meta_ref.shape).astype(jnp.int32)

    rank = jnp.dot(ltri_ref[...], sel.astype(BF16), preferred_element_type=F32) + run_scr[...]
    dest = jnp.sum(jnp.where(is_cls, first_tile * tile_rows + rank, 0.0), axis=1, keepdims=True)
    row = lax.broadcasted_iota(jnp.int32, (tm, LANES), 0)
    spread = jnp.where(lane == row % LANES, dest, 0.0)
    pos_ref[...] = jnp.sum(spread.reshape(tm // LANES, LANES, LANES), axis=1).astype(jnp.int32)
    run_scr[...] += jnp.sum(sel, axis=0, keepdims=True)


def _plan(route, tot, n_tiles_max):
    n = route.shape[0]
    tm = min(PLAN_TM, n)
    r = np.arange(tm)
    ltri = jnp.asarray((r[None, :] < r[:, None]).astype(np.float32), dtype=BF16)
    e = np.arange(LANES)
    utri = jnp.asarray((e[:, None] < e[None, :]).astype(np.float32), dtype=BF16)
    te_rows = -(-n_tiles_max // SUBLANES) * SUBLANES
    kern = functools.partial(_plan_kernel, tile_rows=MOE_TM)
    return pl.pallas_call(
        kern,
        grid=(n // tm,),
        in_specs=[pl.BlockSpec((tm, LANES), lambda i: (i, 0)),
                  pl.BlockSpec((SUBLANES, LANES), lambda i: (0, 0)),
                  pl.BlockSpec((tm, tm), lambda i: (0, 0)),
                  pl.BlockSpec((LANES, LANES), lambda i: (0, 0))],
        out_specs=[pl.BlockSpec((tm // LANES, LANES), lambda i: (i, 0)),
                   pl.BlockSpec((te_rows, LANES), lambda i: (0, 0)),
                   pl.BlockSpec((SUBLANES, LANES), lambda i: (0, 0))],
        out_shape=[jax.ShapeDtypeStruct((n // LANES, LANES), jnp.int32),
                   jax.ShapeDtypeStruct((te_rows, LANES), jnp.int32),
                   jax.ShapeDtypeStruct((SUBLANES, LANES), jnp.int32)],
        scratch_shapes=[pltpu.VMEM((1, LANES), F32)],
        compiler_params=_params(("arbitrary",)),
        name="plan",
    )(route, tot, ltri, utri)


DMA_UNROLL = 8


def _token_copy(src_ref, r, dst_ref, p, sem, rows):
    return pltpu.make_async_copy(src_ref.at[pl.ds(pl.multiple_of(r * rows, rows), rows)],
                                 dst_ref.at[pl.ds(pl.multiple_of(p * rows, rows), rows)], sem)


def _dispatch_kernel(pos_ref, meta_ref, t_ref, xs_ref, zero_scr, sem, tail_sem, *, tm, tile_rows):
    i = pl.program_id(0)
    tile_rows = tile_rows * SUBLANES
    n_tiles = xs_ref.shape[0] // tile_rows
    n_valid = meta_ref[N_CLASSES]

    def clear(tile, s):
        return pltpu.make_async_copy(zero_scr, xs_ref.at[pl.ds(tile * tile_rows, tile_rows)], s)

    def for_each(clears, fn):
        for tile, cond in clears:
            @pl.when(cond)
            def _():
                fn(tile)

    last_tiles = [(meta_ref[c], meta_ref[c] >= 0) for c in range(N_CLASSES)]
    past_tiles = [(n_tiles - 1 - k, n_tiles - 1 - k >= n_valid) for k in range(N_CLASSES)]

    @pl.when(i == 0)
    def _():
        zero_scr[...] = jnp.zeros_like(zero_scr)
        for_each(last_tiles, lambda tile: clear(tile, sem).start())
        for_each(past_tiles, lambda tile: clear(tile, tail_sem).start())
        for_each(last_tiles, lambda tile: clear(tile, sem).wait())

    base = i * tm

    def start(r2, carry):
        for k in range(2):
            r = 2 * r2 + k
            _token_copy(t_ref, r, xs_ref, pos_ref[base + r], sem, SUBLANES).start(priority=k)
        return carry

    lax.fori_loop(0, tm // 2, start, 0, unroll=DMA_UNROLL)
    pltpu.make_async_copy(t_ref, xs_ref.at[pl.ds(0, tm * SUBLANES)], sem).wait()

    @pl.when(i == pl.num_programs(0) - 1)
    def _():
        for_each(past_tiles, lambda tile: clear(tile, tail_sem).wait())


def _dispatch(pos, meta, t, n_rows):
    n = t.shape[0] // SUBLANES
    tm = min(DISPATCH_TM, n)
    kern = functools.partial(_dispatch_kernel, tm=tm, tile_rows=MOE_TM)
    grid_spec = pltpu.PrefetchScalarGridSpec(
        num_scalar_prefetch=2,
        grid=(n // tm,),
        in_specs=[pl.BlockSpec((tm * SUBLANES, LANES), lambda i, *_: (i, 0))],
        out_specs=pl.BlockSpec(memory_space=pl.ANY),
        scratch_shapes=[pltpu.VMEM((MOE_TM * SUBLANES, LANES), F32),
                        pltpu.SemaphoreType.DMA(()), pltpu.SemaphoreType.DMA(())],
    )
    return pl.pallas_call(
        kern,
        grid_spec=grid_spec,
        out_shape=jax.ShapeDtypeStruct((n_rows * SUBLANES, LANES), F32),
        compiler_params=_params(("arbitrary",)),
        name="dispatch",
    )(pos, meta, t)


def _expert_kernel(nv_ref, rid_a, rid_b, exp_a, exp_b, nr_ref, xs_ref, wg_hbm, wu_hbm, wd_hbm,
                   ys_ref, wg_s, wu_s, wd_s, sem):
    j = pl.program_id(0)
    in_use = j < nv_ref[0]
    rows = xs_ref.shape[0] // SUBLANES
    roles = ((rid_a, exp_a), (rid_b, exp_b))

    def copies(role, run):
        e = roles[role][1][run]
        ring = run % 2
        return [pltpu.make_async_copy(src.at[e], dst.at[role, ring], sem.at[role, ring])
                for src, dst in ((wg_hbm, wg_s), (wu_hbm, wu_s), (wd_hbm, wd_s))]

    for role in range(2):
        rid_ref = roles[role][0]
        run = rid_ref[j]
        first = jnp.logical_or(j == 0, run != rid_ref[jnp.maximum(j - 1, 0)])

        @pl.when(j == 0)
        def _():
            for c in copies(role, 0):
                c.start()

        @pl.when(first)
        def _():
            for c in copies(role, run):
                c.wait()

            @pl.when(run + 1 < nr_ref[role])
            def _():
                for c in copies(role, run + 1):
                    c.start()

    @pl.when(in_use)
    def _():
        x = _load_token_tiles(xs_ref).astype(BF16)
        for role in range(2):
            ring = roles[role][0][j] % 2
            hg = jnp.dot(x, wg_s[role, ring].astype(BF16), preferred_element_type=F32)
            hu = jnp.dot(x, wu_s[role, ring].astype(BF16), preferred_element_type=F32)
            a = (hg * jax.nn.sigmoid(hg) * hu).astype(BF16)
            y = jnp.dot(a, wd_s[role, ring].astype(BF16), preferred_element_type=F32)
            for s in range(SUBLANES):
                ys_ref[pl.ds(role * SUBLANES + s, rows, stride=TOKEN_ROWS_OUT), :] = (
                    y[:, s * LANES:(s + 1) * LANES])

    @pl.when(jnp.logical_not(in_use))
    def _():
        ys_ref[...] = jnp.zeros_like(ys_ref)


def _expert_runs(te):
    first = jnp.concatenate([jnp.ones((1,), jnp.int32),
                             (te[1:] != te[:-1]).astype(jnp.int32)])
    rid = jnp.cumsum(first) - 1
    return rid, jnp.zeros_like(te).at[rid].set(te), rid[-1:] + 1


def _experts(tea, teb, nv, xs, wg, wu, wd):
    _, d, de = wg.shape
    assert d == SUBLANES * LANES
    n_tiles = xs.shape[0] // (MOE_TM * SUBLANES)
    rid_a, exp_a, nr_a = _expert_runs(tea)
    rid_b, exp_b, nr_b = _expert_runs(teb)
    grid_spec = pltpu.PrefetchScalarGridSpec(
        num_scalar_prefetch=6,
        grid=(n_tiles,),
        in_specs=[pl.BlockSpec((MOE_TM * SUBLANES, LANES),
                               lambda j, nv, *_: (jnp.minimum(j, nv[0] - 1), 0)),
                  pl.BlockSpec(memory_space=pl.ANY), pl.BlockSpec(memory_space=pl.ANY),
                  pl.BlockSpec(memory_space=pl.ANY)],
        out_specs=pl.BlockSpec((MOE_TM * TOKEN_ROWS_OUT, LANES), lambda j, *_: (j, 0)),
        scratch_shapes=[pltpu.VMEM((2, 2, d, de), F32), pltpu.VMEM((2, 2, d, de), F32),
                        pltpu.VMEM((2, 2, de, d), F32), pltpu.SemaphoreType.DMA((2, 2))],
    )
    return pl.pallas_call(
        _expert_kernel,
        grid_spec=grid_spec,
        out_shape=jax.ShapeDtypeStruct((n_tiles * MOE_TM * TOKEN_ROWS_OUT, LANES), F32),
        compiler_params=_params(("arbitrary",)),
        name="experts",
    )(nv, rid_a, rid_b, exp_a, exp_b, jnp.concatenate([nr_a, nr_b]), xs, wg, wu, wd)


def _combine_kernel(pos_ref, ys_ref, x1_ref, route_ref, fg_ref, o_ref, g, sem, *, tm):
    i = pl.program_id(0)
    slot = i % 2

    def gather(step, buf):
        base = step * tm

        def body(r2, carry):
            for k in range(2):
                r = 2 * r2 + k
                _token_copy(ys_ref, pos_ref[base + r], g.at[buf], r, sem.at[buf],
                            TOKEN_ROWS_OUT).start(priority=k)
            return carry

        lax.fori_loop(0, tm // 2, body, 0, unroll=DMA_UNROLL)

    @pl.when(i == 0)
    def _():
        gather(0, 0)

    @pl.when(i + 1 < pl.num_programs(0))
    def _():
        gather(i + 1, 1 - slot)

    pltpu.make_async_copy(ys_ref.at[pl.ds(0, tm * TOKEN_ROWS_OUT)], g.at[slot],
                          sem.at[slot]).wait()
    gs = g.at[slot]
    ya, yb = (jnp.concatenate([gs[pl.ds(off + s, tm, stride=TOKEN_ROWS_OUT), :]
                               for s in range(SUBLANES)], axis=1) for off in (0, SUBLANES))
    route = route_ref[...]
    x2 = x1_ref[...] + (route[:, ROUTE_WA:ROUTE_WA + 1] * ya + route[:, ROUTE_WB:ROUTE_WB + 1] * yb)
    var = jnp.mean(x2 * x2, axis=-1, keepdims=True)
    o_ref[...] = x2 * lax.rsqrt(var + EPS) * fg_ref[...]


def _combine(pos, ys, x1, route, final_g):
    n, d = x1.shape
    tm = min(COMBINE_TM, n)
    kern = functools.partial(_combine_kernel, tm=tm)
    grid_spec = pltpu.PrefetchScalarGridSpec(
        num_scalar_prefetch=1,
        grid=(n // tm,),
        in_specs=[pl.BlockSpec(memory_space=pl.ANY),
                  pl.BlockSpec((tm, d), lambda i, *_: (i, 0)),
                  pl.BlockSpec((tm, LANES), lambda i, *_: (i, 0)),
                  pl.BlockSpec((1, d), lambda i, *_: (0, 0))],
        out_specs=pl.BlockSpec((tm, d), lambda i, *_: (i, 0)),
        scratch_shapes=[pltpu.VMEM((2, tm * TOKEN_ROWS_OUT, LANES), F32),
                        pltpu.SemaphoreType.DMA((2,))],
    )
    return pl.pallas_call(
        kern,
        grid_spec=grid_spec,
        out_shape=jax.ShapeDtypeStruct((n, d), F32),
        compiler_params=_params(("arbitrary",)),
        name="combine",
    )(pos, ys, x1, route, final_g.reshape(1, d))


def _moe_sparse(t, route, tot, wg, wu, wd, x1, final_g):
    n = x1.shape[0]
    n_tiles_max = (n + N_CLASSES * (MOE_TM - 1)) // MOE_TM
    pos, te, meta = _plan(route, tot, n_tiles_max)
    pos = pos.reshape(n)
    xs = _dispatch(pos, meta[0, :N_CLASSES + 1], t, n_tiles_max * MOE_TM)
    ys = _experts(te[:n_tiles_max, 0], te[:n_tiles_max, 1], meta[0, N_CLASSES:N_CLASSES + 1],
                  xs, wg, wu, wd)
    return _combine(pos, ys, x1, route, final_g)


def _suffix_ones(t):
    j = np.arange(t)[:, None]
    s = np.arange(t)[None, :]
    return jnp.asarray((j > s).astype(np.float32), dtype=BF16)


def _causal_bias(t):
    row = np.arange(t)[:, None]
    col = np.arange(t)[None, :]
    diag = np.where(col < row, 0.0, MASK_BIAS).astype(np.float32)
    return jnp.asarray(np.stack([np.zeros_like(diag), diag]))


def _chunk_prefix_ones(tt, c):
    t = np.arange(tt)[:, None]
    j = np.arange(tt)[None, :]
    return jnp.asarray(((j <= t) & (t // c == j // c)).astype(np.float32), dtype=BF16)


def _block_diag_ones(w, blk):
    a = np.arange(w)
    return jnp.asarray((a[:, None] // blk == a[None, :] // blk).astype(np.float32), dtype=BF16)


def kernel(x, ln1_g, w_in, w_branch_sb, w_branch_hg, hg_norm_g, hg_lb_logits, w_out, ln2_g,
           w_router_group, b_router_group, w_router_expert, b_router_expert,
           w_exp_gate, w_exp_up, w_exp_down, final_g):
    bsz, seq, d = x.shape
    depth = w_in.shape[0]
    n = bsz * seq
    sb_width = SB_HEADS * SB_HEAD_DIM
    hg_width = HG_HEADS * HG_DIM

    tri_attn = _suffix_ones(min(ATTN_T, seq))
    bias_attn = _causal_bias(min(ATTN_T, seq))
    tt = min(HGRN_TT, seq)
    tri_hg = _chunk_prefix_ones(tt, min(HGRN_C, tt))
    bd = _block_diag_ones(min(MXU_DIM, hg_width), HG_DIM)

    x2 = x.reshape(n, d)
    for l in range(depth):
        qkv, rest = _inproj(x2, ln1_g[l], w_in[l], sb_width)
        y_sb = _attn(qkv, bsz, seq, tri_attn, bias_attn)
        y_hg = _hgrn(rest, hg_lb_logits, hg_norm_g[l], bsz, seq, l, tri_hg, bd)

        pad = LANES - N_EXPERTS - N_GROUPS
        wr = jnp.concatenate([w_router_expert[l], w_router_group[l],
                              jnp.zeros((d, pad), F32)], axis=1)
        wr_hi = wr.astype(BF16)
        wr_lo = (wr - wr_hi.astype(F32)).astype(BF16)
        wr_split = jnp.concatenate([wr_hi, wr_lo], axis=1)
        br = jnp.concatenate([b_router_expert[l], b_router_group[l],
                              jnp.zeros((pad,), F32)]).reshape(1, LANES)

        last = l == depth - 1
        x1, t, route, tot = _merge(x2, y_sb, y_hg, rest, w_branch_sb[l], w_branch_hg[l],
                                   w_out[l], ln2_g[l], wr_split, br)
        assert last, "final rmsnorm is fused into the last layer's combine kernel"
        x2 = _moe_sparse(t, route, tot, w_exp_gate[l], w_exp_up[l], w_exp_down[l], x1, final_g)
    return x2.reshape(bsz, seq, d)
```

```python
import functools

import jax
import jax.numpy as jnp
import numpy as np
from jax import lax
from jax.experimental import pallas as pl
from jax.experimental.pallas import tpu as pltpu

F32 = jnp.float32
BF16 = jnp.bfloat16

EPS = 1e-6
SB_HEADS = 8
SB_HEAD_DIM = 64
HG_HEADS = 8
HG_DIM = 64
N_GROUPS = 4
EXPERTS_PER_GROUP = 4
N_EXPERTS = N_GROUPS * EXPERTS_PER_GROUP

LANES = 128
SUBLANES = 8
MXU_DIM = 256
LOG2E = 1.4426950408889634
VMEM_LIMIT = 56 * 1024 * 1024

INPROJ_TM = 2048
INPROJ_TN = 512
ATTN_T = MXU_DIM
HGRN_TT = 256
HGRN_C = 32
MERGE_TM = 512
MOE_TM = 384
PLAN_TM = 1024
DISPATCH_TM = 1024
COMBINE_TM = 512


def _params(sem):
    return pltpu.CompilerParams(dimension_semantics=sem, vmem_limit_bytes=VMEM_LIMIT)


def _split_dot(x, m, passes):
    acc = None
    r = x
    for p in range(passes):
        h = r.astype(BF16)
        term = jnp.dot(h, m, preferred_element_type=F32)
        acc = term if acc is None else acc + term
        if p + 1 < passes:
            r = r - h.astype(F32)
    return acc


def _inproj_kernel(x_ref, g_ref, w_ref, qkv_ref, rest_ref, h_scr, w_scr, *, q_scale):
    i = pl.program_id(0)
    j = pl.program_id(1)

    @pl.when(j == 0)
    def _():
        x = x_ref[...]
        var = jnp.mean(x * x, axis=-1, keepdims=True)
        h_scr[...] = (x * lax.rsqrt(var + EPS) * g_ref[...]).astype(BF16)

    @pl.when(i == 0)
    def _():
        w_scr[j] = w_ref[...].astype(BF16)

    acc = jnp.dot(h_scr[...], w_scr[j], preferred_element_type=F32)
    qkv_ref[...] = (acc * jnp.where(j == 0, q_scale, 1.0)).astype(BF16)
    rest_ref[...] = acc


def _inproj(x2, ln_g, w_in, sb_width):
    n, d = x2.shape
    cols = w_in.shape[1]
    tm, tn = min(INPROJ_TM, n), INPROJ_TN
    assert sb_width == tn, "q block must be exactly one column tile"
    nq = 3 * sb_width // tn
    nj = cols // tn
    kern = functools.partial(_inproj_kernel, q_scale=SB_HEAD_DIM ** -0.5)
    return pl.pallas_call(
        kern,
        grid=(n // tm, nj),
        in_specs=[
            pl.BlockSpec((tm, d), lambda i, j: (i, 0)),
            pl.BlockSpec((1, d), lambda i, j: (0, 0)),
            pl.BlockSpec((d, tn), lambda i, j: (0, jnp.where(i == 0, j, nj - 1))),
        ],
        out_specs=[
            pl.BlockSpec((tm, tn), lambda i, j: (i, jnp.minimum(j, nq))),
            pl.BlockSpec((tm, tn), lambda i, j: (i, jnp.maximum(j - nq, 0))),
        ],
        out_shape=[
            jax.ShapeDtypeStruct((n, (nq + 1) * tn), BF16),
            jax.ShapeDtypeStruct((n, cols - nq * tn), F32),
        ],
        scratch_shapes=[pltpu.VMEM((tm, d), BF16), pltpu.VMEM((nj, d, tn), BF16)],
        compiler_params=_params(("arbitrary", "arbitrary")),
        name="inproj",
    )(x2, ln_g.reshape(1, d), w_in)


ATTN_STAGES = 3
ATTN_STREAMS = 2
MASK_BIAS = -1e30
ATTN_SKIP = 111.0


def _attn_kernel(q_ref, k_ref, v_ref, tri_ref, bias_ref, o_ref,
                 z0, z1, z2, i0, i1, i2, acc_ref, c_ref, *, t, nq):
    zbuf = (z0, z1, z2)
    ibuf = (i0, i1, i2)
    for r in zbuf + ibuf + (acc_ref, c_ref):
        r[...] = jnp.zeros_like(r)

    lane = lax.broadcasted_iota(jnp.int32, (t, LANES), 1)
    head0 = lane < SB_HEAD_DIM
    nt = (((1,), (1,)), ((), ()))

    def stage_a(qi, kj, slot):
        q = q_ref[pl.ds(pl.multiple_of(qi * t, t), t), :]
        zero = jnp.zeros_like(q)
        q2 = jnp.concatenate([jnp.where(head0, q, zero), jnp.where(head0, zero, q)], axis=0)
        k = k_ref[pl.ds(pl.multiple_of(kj * t, t), t), :]
        z = lax.dot_general(q2, k, nt, preferred_element_type=F32)
        bias = bias_ref[(qi == kj).astype(jnp.int32)]
        zbuf[slot][...] = z + jnp.concatenate([bias, bias], axis=0)

    def stage_b(slot):
        z = zbuf[slot][...]
        p = jnp.maximum(z, 0.0) + jnp.log(1.0 + jnp.exp2(jnp.abs(z) * (-LOG2E)))
        incl = jnp.dot(p.astype(BF16), tri_ref[...], preferred_element_type=F32) + p
        ibuf[slot][...] = incl
        return incl[:, 0:1]

    def stage_c(qi, kj, slot, stream):
        first = qi == kj
        incl = ibuf[slot][...]
        c = jnp.where(first, 0.0, c_ref[stream])
        a = jnp.exp2((zbuf[slot][...] - incl - c) * LOG2E)
        v = v_ref[pl.ds(pl.multiple_of(kj * t, t), t), :]
        pv = jnp.dot(a.astype(BF16), v, preferred_element_type=F32)
        acc = jnp.where(first, pv, acc_ref[stream] + pv)
        acc_ref[stream] = acc
        c_ref[stream] = c + incl[:, 0:1]
        o_ref[pl.ds(pl.multiple_of(qi * t, t), t), :] = (
            jnp.where(head0, acc[0:t], acc[t:2 * t]).astype(o_ref.dtype))

    def block(carry):
        qs, ks, qn, kn, skip_q, drained = carry
        qs, ks, qn, kn, skip_q = list(qs), list(ks), list(qn), list(kn), list(skip_q)
        all_done = qn[0] >= nq
        for p in range(1, ATTN_STREAMS):
            all_done = jnp.logical_and(all_done, qn[p] >= nq)
        drained = drained + all_done.astype(jnp.int32)
        for r in range(ATTN_STAGES * ATTN_STREAMS):
            sa, sc, sb = r % ATTN_STAGES, (r + 1) % ATTN_STAGES, (r + 2) % ATTN_STAGES
            pa, pb = r % ATTN_STREAMS, (r - 1) % ATTN_STREAMS
            stage_c(qs[sc], ks[sc], sc, pa)
            total_b = stage_b(sb)
            carry_b = jnp.where(qs[sb] == ks[sb], 0.0, c_ref[pb]) + total_b
            skip = skip_q[pa] == qn[pa]
            q_cur = jnp.where(skip, qn[pa] + ATTN_STREAMS, qn[pa])
            k_cur = jnp.where(skip, qn[pa] + ATTN_STREAMS, kn[pa])
            drain = q_cur >= nq
            qa = jnp.where(drain, 0, q_cur)
            ka = jnp.where(drain, 0, k_cur)
            stage_a(qa, ka, sa)
            last = k_cur == 0
            qn[pa] = jnp.where(jnp.logical_and(last, jnp.logical_not(drain)),
                               q_cur + ATTN_STREAMS, q_cur)
            kn[pa] = jnp.where(drain, k_cur, jnp.where(last, q_cur + ATTN_STREAMS, k_cur - 1))
            skip_q[pb] = jnp.where(jnp.min(carry_b) >= ATTN_SKIP, qs[sb], skip_q[pb])
            qs[sa], ks[sa] = qa, ka
        return tuple(qs), tuple(ks), tuple(qn), tuple(kn), tuple(skip_q), drained

    zero = jnp.int32(0)
    first_q = tuple(jnp.int32(p) for p in range(ATTN_STREAMS))
    init = ((zero,) * ATTN_STAGES, (zero,) * ATTN_STAGES, first_q, first_q,
            (jnp.int32(-1),) * ATTN_STREAMS, zero)
    lax.while_loop(lambda carry: carry[5] < 1, block, init)


def _attn(qkv, bsz, seq, tri, bias):
    n = bsz * seq
    t = min(ATTN_T, seq)
    pairs = SB_HEADS * SB_HEAD_DIM // LANES
    kern = functools.partial(_attn_kernel, t=t, nq=seq // t)
    return pl.pallas_call(
        kern,
        grid=(bsz, pairs),
        in_specs=[
            pl.BlockSpec((seq, LANES), lambda b, p: (b, p)),
            pl.BlockSpec((seq, LANES), lambda b, p: (b, pairs + p)),
            pl.BlockSpec((seq, LANES), lambda b, p: (b, 2 * pairs + p)),
            pl.BlockSpec((t, t), lambda b, p: (0, 0)),
            pl.BlockSpec((2, t, t), lambda b, p: (0, 0, 0)),
        ],
        out_specs=pl.BlockSpec((seq, LANES), lambda b, p: (b, p)),
        out_shape=jax.ShapeDtypeStruct((n, pairs * LANES), BF16),
        scratch_shapes=([pltpu.VMEM((2 * t, t), F32)] * (2 * ATTN_STAGES)
                        + [pltpu.VMEM((ATTN_STREAMS, 2 * t, LANES), F32),
                           pltpu.VMEM((ATTN_STREAMS, 2 * t, 1), F32)]),
        compiler_params=_params(("parallel", "parallel")),
        name="attn",
    )(qkv, qkv, qkv, tri, bias)


def _group_dot(x, bd):
    g = bd.shape[0]
    parts = [jnp.dot(x[:, i:i + g], bd, preferred_element_type=F32)
             for i in range(0, x.shape[1], g)]
    return jnp.concatenate(parts, axis=1)


def _hgrn_kernel(q_ref, f_ref, i_ref, g_ref, lbl_ref, ng_ref, tri_ref, bd_ref, o_ref,
                 st_scr, wpad, vpad, wsh, vsh, *, tt, c, layer):
    ti = pl.program_id(1)
    w = q_ref.shape[1]
    nch = tt // c

    @pl.when(ti == 0)
    def _():
        st_scr[...] = jnp.zeros_like(st_scr)

    lg = lbl_ref[...]
    e = jnp.exp(lg - jnp.max(lg, axis=0, keepdims=True))
    lb = jnp.sum(e[0:layer + 1], axis=0, keepdims=True) / jnp.sum(e, axis=0, keepdims=True)

    f = lb + (1.0 - lb) * jax.nn.sigmoid(f_ref[...])
    kk = 1.0 - f
    qv = q_ref[...]
    qs = qv * jax.nn.sigmoid(qv)
    v = i_ref[...]
    bd = bd_ref[...]
    cum = _split_dot_left(tri_ref[...], jnp.log(f) * LOG2E, 3)
    wk = jnp.log(kk) * LOG2E - cum

    pos = lax.broadcasted_iota(jnp.int32, (tt, 1), 0) % c
    wpad[0:SUBLANES, :] = jnp.zeros((SUBLANES, w), F32)
    vpad[0:SUBLANES, :] = jnp.zeros((SUBLANES, w), F32)
    wpad[SUBLANES:SUBLANES + tt, :] = wk
    vpad[SUBLANES:SUBLANES + tt, :] = v
    for b in range(SUBLANES):
        wb = wpad[SUBLANES - b:SUBLANES - b + tt, :]
        vb = vpad[SUBLANES - b:SUBLANES - b + tt, :]
        if b:
            wb = jnp.where(pos >= b, wb, -jnp.inf)
        wsh[b] = wb.reshape(nch, c, w)
        vsh[b] = vb.reshape(nch, c, w)

    qs3 = qs.reshape(nch, c, w)
    cum3 = cum.reshape(nch, c, w)
    acc = None
    for a in reversed(range(c // SUBLANES)):
        rows = c - SUBLANES * a
        qa = qs3[:, SUBLANES * a:, :].reshape(nch * rows, w)
        ca = cum3[:, SUBLANES * a:, :].reshape(nch * rows, w)
        acc_a = None
        for b in range(SUBLANES):
            wb = wsh[b, :, 0:rows, :].reshape(nch * rows, w)
            vb = vsh[b, :, 0:rows, :].reshape(nch * rows, w)
            dd = qa * jnp.exp2(ca + wb)
            term = _group_dot(dd.astype(BF16), bd) * vb
            acc_a = term if acc_a is None else acc_a + term
        acc_a = acc_a.reshape(nch, rows, w)
        if acc is not None:
            acc_a = acc_a + jnp.concatenate([jnp.zeros((nch, SUBLANES, w), F32), acc], axis=1)
        acc = acc_a
    acc = acc.reshape(tt, w)

    grp = st_scr.shape[1]
    bdmask = bd[0:grp, 0:grp] != 0
    outs = []
    for ci in range(nch):
        r0 = ci * c
        cum_c = cum[r0:r0 + c]
        last = cum_c[c - 1:c]
        qd = (qs[r0:r0 + c] * jnp.exp2(cum_c)).astype(BF16)
        kd = (kk[r0:r0 + c] * jnp.exp2(last - cum_c)).astype(BF16)
        vc = v[r0:r0 + c].astype(BF16)
        dec = jnp.exp2(last)
        o_parts = []
        for gi in range(w // grp):
            sl = slice(gi * grp, (gi + 1) * grp)
            st = st_scr[gi]
            o_parts.append(lax.dot_general(qd[:, sl], st.astype(BF16), (((1,), (1,)), ((), ())),
                                           preferred_element_type=F32))
            upd = lax.dot_general(vc[:, sl], kd[:, sl], (((0,), (0,)), ((), ())),
                                  preferred_element_type=F32)
            st_scr[gi] = st * dec[:, sl] + jnp.where(bdmask, upd, 0.0)
        outs.append(jnp.concatenate(o_parts, axis=1))
    o = acc + jnp.concatenate(outs, axis=0)

    o2 = o * o
    o2_hi = o2.astype(BF16)
    o2_lo = (o2 - o2_hi.astype(F32)).astype(BF16)
    ms = (_group_dot(o2_hi, bd) + _group_dot(o2_lo, bd)) * (1.0 / HG_DIM)
    gv = g_ref[...]
    o = o * lax.rsqrt(ms + EPS) * ng_ref[...] * (gv * jax.nn.sigmoid(gv))
    o_ref[...] = o.astype(o_ref.dtype)


def _split_dot_left(m, x, passes):
    acc = None
    r = x
    for p in range(passes):
        h = r.astype(BF16)
        term = jnp.dot(m, h, preferred_element_type=F32)
        acc = term if acc is None else acc + term
        if p + 1 < passes:
            r = r - h.astype(F32)
    return acc


def _hgrn(rest, lb_logits, norm_g, bsz, seq, layer, tri, bd):
    n = bsz * seq
    w = HG_HEADS * HG_DIM
    grp = bd.shape[0]
    tt = min(HGRN_TT, seq)
    c = min(HGRN_C, tt)
    nt = seq // tt
    kern = functools.partial(_hgrn_kernel, tt=tt, c=c, layer=layer)
    col = lambda j: pl.BlockSpec((tt, w), lambda b, i, j=j: (b * nt + i, j))
    const = lambda shape: pl.BlockSpec(shape, lambda b, i: (0, 0))
    return pl.pallas_call(
        kern,
        grid=(bsz, nt),
        in_specs=[col(0), col(1), col(2), col(3),
                  const(lb_logits.shape), const((1, w)), const((tt, tt)), const((grp, grp))],
        out_specs=pl.BlockSpec((tt, w), lambda b, i: (b * nt + i, 0)),
        out_shape=jax.ShapeDtypeStruct((n, w), BF16),
        scratch_shapes=([pltpu.VMEM((w // LANES, LANES, LANES), F32)]
                        + [pltpu.VMEM((SUBLANES + tt, w), F32)] * 2
                        + [pltpu.VMEM((SUBLANES, tt // c, c, w), F32)] * 2),
        compiler_params=_params(("parallel", "arbitrary")),
        name="hgrn",
    )(rest, rest, rest, rest, lb_logits, norm_g.reshape(1, w), tri, bd)


PAIR_A = (0, 0, 0, 1, 1, 3)
PAIR_B = (1, 2, 3, 3, 2, 2)
N_PAIRS = len(PAIR_A)
N_CLASSES = N_GROUPS * N_PAIRS
assert sorted(tuple(sorted(p)) for p in zip(PAIR_A, PAIR_B)) == [
    (a, b) for a in range(EXPERTS_PER_GROUP) for b in range(a + 1, EXPERTS_PER_GROUP)]
ROUTE_CLS, ROUTE_WA, ROUTE_WB = 0, 1, 2


def _pair_slots(pidx):
    a = b = jnp.zeros_like(pidx)
    for p in range(N_PAIRS):
        a = jnp.where(pidx == p, float(PAIR_A[p]), a)
        b = jnp.where(pidx == p, float(PAIR_B[p]), b)
    return a, b


def _pair_index(lo, hi):
    pidx = jnp.zeros_like(lo)
    for p in range(N_PAIRS):
        is_p = jnp.logical_and(lo == min(PAIR_A[p], PAIR_B[p]), hi == max(PAIR_A[p], PAIR_B[p]))
        pidx = jnp.where(is_p, float(p), pidx)
    return pidx


def _route(lg):
    lane = lax.broadcasted_iota(jnp.int32, lg.shape, 1)
    neg = jnp.float32(-jnp.inf)
    big = jnp.int32(LANES)
    gmask = jnp.logical_and(lane >= N_EXPERTS, lane < N_EXPERTS + N_GROUPS)
    gl = jnp.where(gmask, lg, neg)
    gmax = jnp.max(gl, axis=1, keepdims=True)
    gidx = jnp.min(jnp.where(gl == gmax, lane, big), axis=1, keepdims=True) - N_EXPERTS
    w_grp = 1.0 / jnp.sum(jnp.where(gmask, jnp.exp(gl - gmax), 0.0), axis=1, keepdims=True)
    in_grp = jnp.logical_and(lane < N_EXPERTS, lane // EXPERTS_PER_GROUP == gidx)
    l1 = jnp.where(in_grp, lg, neg)
    v1 = jnp.max(l1, axis=1, keepdims=True)
    i1 = jnp.min(jnp.where(l1 == v1, lane, big), axis=1, keepdims=True)
    l2 = jnp.where(jnp.logical_and(in_grp, lane != i1), lg, neg)
    v2 = jnp.max(l2, axis=1, keepdims=True)
    i2 = jnp.min(jnp.where(l2 == v2, lane, big), axis=1, keepdims=True)
    e2 = jnp.exp(v2 - v1)
    p1 = 1.0 / (1.0 + e2)
    p2 = e2 * p1
    loc1 = (i1 - gidx * EXPERTS_PER_GROUP).astype(F32)
    loc2 = (i2 - gidx * EXPERTS_PER_GROUP).astype(F32)
    pidx = _pair_index(jnp.minimum(loc1, loc2), jnp.maximum(loc1, loc2))
    cls = gidx.astype(F32) * N_PAIRS + pidx
    a_loc, _ = _pair_slots(pidx)
    first_is_a = loc1 == a_loc
    wa = jnp.where(first_is_a, p1, p2) * w_grp
    wb = jnp.where(first_is_a, p2, p1) * w_grp
    record = jnp.where(lane == ROUTE_CLS, cls,
                       jnp.where(lane == ROUTE_WA, wa, jnp.where(lane == ROUTE_WB, wb, 0.0)))
    return record, cls


def _store_token_tiles(ref, x):
    rows, d = x.shape
    assert d == SUBLANES * LANES
    for s in range(SUBLANES):
        ref[pl.ds(s, rows, stride=SUBLANES), :] = x[:, s * LANES:(s + 1) * LANES]


def _load_token_tiles(ref):
    rows = ref.shape[0] // SUBLANES
    return jnp.concatenate([ref[pl.ds(s, rows, stride=SUBLANES), :] for s in range(SUBLANES)],
                           axis=1)


def _merge_kernel(x_ref, ysb_ref, yhg_ref, gsb_ref, ghg_ref, wbs_ref, wbh_ref, wo_ref,
                  ln_ref, wr_ref, br_ref, x1_ref, t_ref, route_ref, tot_ref,
                  wbs_scr, wbh_scr, wo_scr):
    @pl.when(pl.program_id(0) == 0)
    def _():
        wbs_scr[...] = wbs_ref[...].astype(BF16)
        wbh_scr[...] = wbh_ref[...].astype(BF16)
        wo_scr[...] = wo_ref[...].astype(BF16)
        tot_ref[...] = jnp.zeros_like(tot_ref)

    a = jnp.dot(ysb_ref[...], wbs_scr[...], preferred_element_type=F32)
    b = jnp.dot(yhg_ref[...], wbh_scr[...], preferred_element_type=F32)
    merged = jax.nn.sigmoid(gsb_ref[...]) * a + jax.nn.sigmoid(ghg_ref[...]) * b
    x1 = x_ref[...] + jnp.dot(merged.astype(BF16), wo_scr[...], preferred_element_type=F32)
    x1_ref[...] = x1
    var = jnp.mean(x1 * x1, axis=-1, keepdims=True)
    t = x1 * lax.rsqrt(var + EPS) * ln_ref[...]
    _store_token_tiles(t_ref, t)
    t_hi = t.astype(BF16)
    t_lo = (t - t_hi.astype(F32)).astype(BF16)
    p_hi = jnp.dot(t_hi, wr_ref[...], preferred_element_type=F32)
    p_lo = jnp.dot(t_lo, wr_ref[...], preferred_element_type=F32)
    lg = (p_hi[:, :LANES] + p_hi[:, LANES:]) + (p_lo[:, :LANES] + p_lo[:, LANES:]) + br_ref[...]
    route, cls = _route(lg)
    route_ref[...] = route
    lane = lax.broadcasted_iota(jnp.int32, route.shape, 1)
    sel = jnp.where(lane.astype(F32) == cls, 1.0, 0.0)
    tot_ref[...] += jnp.broadcast_to(jnp.sum(sel, axis=0, keepdims=True), tot_ref.shape)


def _merge(x2, y_sb, y_hg, rest, wbs, wbh, wo, ln_g, wr, br):
    n, d = x2.shape
    tm = min(MERGE_TM, n)
    w_sb, w_hg = y_sb.shape[1], y_hg.shape[1]
    gate_blk = (rest.shape[1] - 2 * d) // d
    row = lambda wdt, j=0: pl.BlockSpec((tm, wdt), lambda i, j=j: (i, j))
    const = lambda shape: pl.BlockSpec(shape, lambda i: (0, 0))
    return pl.pallas_call(
        _merge_kernel,
        grid=(n // tm,),
        in_specs=[row(d), row(w_sb), row(w_hg), row(d, gate_blk), row(d, gate_blk + 1),
                  const(wbs.shape), const(wbh.shape), const(wo.shape), const((1, d)),
                  const(wr.shape), const((1, LANES))],
        out_specs=[row(d), pl.BlockSpec((tm * SUBLANES, LANES), lambda i: (i, 0)), row(LANES),
                   const((SUBLANES, LANES))],
        out_shape=[jax.ShapeDtypeStruct((n, d), F32),
                   jax.ShapeDtypeStruct((n * SUBLANES, LANES), F32),
                   jax.ShapeDtypeStruct((n, LANES), F32),
                   jax.ShapeDtypeStruct((SUBLANES, LANES), F32)],
        scratch_shapes=[pltpu.VMEM(w.shape, BF16) for w in (wbs, wbh, wo)],
        compiler_params=_params(("arbitrary",)),
        name="merge",
    )(x2, y_sb, y_hg, rest, rest, wbs, wbh, wo, ln_g.reshape(1, d), wr, br)


TOKEN_ROWS_OUT = 2 * SUBLANES


def _plan_kernel(route_ref, tot_ref, ltri_ref, utri_ref, pos_ref, te_ref, meta_ref,
                 run_scr, *, tile_rows):
    i = pl.program_id(0)
    route = route_ref[...]
    tm = route.shape[0]
    lane = lax.broadcasted_iota(jnp.int32, (tm, LANES), 1)
    is_cls = lane.astype(F32) == route[:, ROUTE_CLS:ROUTE_CLS + 1]
    sel = jnp.where(is_cls, 1.0, 0.0)

    tot = tot_ref[0:1, :]
    top = tot + (tile_rows - 1.0)
    tiles = jnp.floor(top * (1.0 / tile_rows))
    tiles = jnp.where(tiles * tile_rows > top, tiles - 1.0, tiles)
    tiles = jnp.where((tiles + 1.0) * tile_rows <= top, tiles + 1.0, tiles)
    first_tile = _split_dot(jnp.broadcast_to(tiles, (SUBLANES, LANES)), utri_ref[...], 3)[0:1]

    @pl.when(i == 0)
    def _():
        run_scr[...] = jnp.zeros_like(run_scr)
        lane1 = lax.broadcasted_iota(jnp.int32, (1, LANES), 1)
        is_c = lane1 < N_CLASSES
        end_tile = first_tile + tiles
        n_valid = jnp.sum(jnp.where(is_c, tiles, 0.0), axis=1, keepdims=True)
        rows = te_ref.shape[0]
        j = lax.broadcasted_iota(jnp.int32, (rows, LANES), 0).astype(F32)
        j = jnp.minimum(j, n_valid - 1.0)
        lane2 = lax.broadcasted_iota(jnp.int32, (rows, LANES), 1)
        done = jnp.where(jnp.logical_and(lane2 < N_CLASSES, end_tile <= j), 1.0, 0.0)
        tc = jnp.sum(done, axis=1, keepdims=True)
        tg = jnp.floor((tc + 0.5) * (1.0 / N_PAIRS))
        ta, tb = _pair_slots(tc - tg * N_PAIRS)
        te = jnp.where(lane2 == 0, tg * EXPERTS_PER_GROUP + ta, tg * EXPERTS_PER_GROUP + tb)
        te_ref[...] = te.astype(jnp.int32)
        last_tile = jnp.where(jnp.logical_and(is_c, tiles > 0), end_tile - 1.0, -1.0)
        meta = jnp.where(lane1 == N_CLASSES, n_valid, last_tile)
        meta_ref[...] = jnp.broadcast_to(meta, meta_ref.shape).astype(jnp.int32)

    rank = jnp.dot(ltri_ref[...], sel.astype(BF16), preferred_element_type=F32) + run_scr[...]
    dest = jnp.sum(jnp.where(is_cls, first_tile * tile_rows + rank, 0.0), axis=1, keepdims=True)
    row = lax.broadcasted_iota(jnp.int32, (tm, LANES), 0)
    spread = jnp.where(lane == row % LANES, dest, 0.0)
    pos_ref[...] = jnp.sum(spread.reshape(tm // LANES, LANES, LANES), axis=1).astype(jnp.int32)
    run_scr[...] += jnp.sum(sel, axis=0, keepdims=True)


def _plan(route, tot, n_tiles_max):
    n = route.shape[0]
    tm = min(PLAN_TM, n)
    r = np.arange(tm)
    ltri = jnp.asarray((r[None, :] < r[:, None]).astype(np.float32), dtype=BF16)
    e = np.arange(LANES)
    utri = jnp.asarray((e[:, None] < e[None, :]).astype(np.float32), dtype=BF16)
    te_rows = -(-n_tiles_max // SUBLANES) * SUBLANES
    kern = functools.partial(_plan_kernel, tile_rows=MOE_TM)
    return pl.pallas_call(
        kern,
        grid=(n // tm,),
        in_specs=[pl.BlockSpec((tm, LANES), lambda i: (i, 0)),
                  pl.BlockSpec((SUBLANES, LANES), lambda i: (0, 0)),
                  pl.BlockSpec((tm, tm), lambda i: (0, 0)),
                  pl.BlockSpec((LANES, LANES), lambda i: (0, 0))],
        out_specs=[pl.BlockSpec((tm // LANES, LANES), lambda i: (i, 0)),
                   pl.BlockSpec((te_rows, LANES), lambda i: (0, 0)),
                   pl.BlockSpec((SUBLANES, LANES), lambda i: (0, 0))],
        out_shape=[jax.ShapeDtypeStruct((n // LANES, LANES), jnp.int32),
                   jax.ShapeDtypeStruct((te_rows, LANES), jnp.int32),
                   jax.ShapeDtypeStruct((SUBLANES, LANES), jnp.int32)],
        scratch_shapes=[pltpu.VMEM((1, LANES), F32)],
        compiler_params=_params(("arbitrary",)),
        name="plan",
    )(route, tot, ltri, utri)


DMA_UNROLL = 8


def _token_copy(src_ref, r, dst_ref, p, sem, rows):
    return pltpu.make_async_copy(src_ref.at[pl.ds(pl.multiple_of(r * rows, rows), rows)],
                                 dst_ref.at[pl.ds(pl.multiple_of(p * rows, rows), rows)], sem)


def _dispatch_kernel(pos_ref, meta_ref, t_ref, xs_ref, zero_scr, sem, tail_sem, *, tm, tile_rows):
    i = pl.program_id(0)
    tile_rows = tile_rows * SUBLANES
    n_tiles = xs_ref.shape[0] // tile_rows
    n_valid = meta_ref[N_CLASSES]

    def clear(tile, s):
        return pltpu.make_async_copy(zero_scr, xs_ref.at[pl.ds(tile * tile_rows, tile_rows)], s)

    def for_each(clears, fn):
        for tile, cond in clears:
            @pl.when(cond)
            def _():
                fn(tile)

    last_tiles = [(meta_ref[c], meta_ref[c] >= 0) for c in range(N_CLASSES)]
    past_tiles = [(n_tiles - 1 - k, n_tiles - 1 - k >= n_valid) for k in range(N_CLASSES)]

    @pl.when(i == 0)
    def _():
        zero_scr[...] = jnp.zeros_like(zero_scr)
        for_each(last_tiles, lambda tile: clear(tile, sem).start())
        for_each(past_tiles, lambda tile: clear(tile, tail_sem).start())
        for_each(last_tiles, lambda tile: clear(tile, sem).wait())

    base = i * tm

    def start(r2, carry):
        for k in range(2):
            r = 2 * r2 + k
            _token_copy(t_ref, r, xs_ref, pos_ref[base + r], sem, SUBLANES).start(priority=k)
        return carry

    lax.fori_loop(0, tm // 2, start, 0, unroll=DMA_UNROLL)
    pltpu.make_async_copy(t_ref, xs_ref.at[pl.ds(0, tm * SUBLANES)], sem).wait()

    @pl.when(i == pl.num_programs(0) - 1)
    def _():
        for_each(past_tiles, lambda tile: clear(tile, tail_sem).wait())


def _dispatch(pos, meta, t, n_rows):
    n = t.shape[0] // SUBLANES
    tm = min(DISPATCH_TM, n)
    kern = functools.partial(_dispatch_kernel, tm=tm, tile_rows=MOE_TM)
    grid_spec = pltpu.PrefetchScalarGridSpec(
        num_scalar_prefetch=2,
        grid=(n // tm,),
        in_specs=[pl.BlockSpec((tm * SUBLANES, LANES), lambda i, *_: (i, 0))],
        out_specs=pl.BlockSpec(memory_space=pl.ANY),
        scratch_shapes=[pltpu.VMEM((MOE_TM * SUBLANES, LANES), F32),
                        pltpu.SemaphoreType.DMA(()), pltpu.SemaphoreType.DMA(())],
    )
    return pl.pallas_call(
        kern,
        grid_spec=grid_spec,
        out_shape=jax.ShapeDtypeStruct((n_rows * SUBLANES, LANES), F32),
        compiler_params=_params(("arbitrary",)),
        name="dispatch",
    )(pos, meta, t)


def _expert_kernel(nv_ref, rid_a, rid_b, exp_a, exp_b, nr_ref, xs_ref, wg_hbm, wu_hbm, wd_hbm,
                   ys_ref, wg_s, wu_s, wd_s, wg_b, wu_b, wd_b, sem):
    j = pl.program_id(0)
    in_use = j < nv_ref[0]
    rows = xs_ref.shape[0] // SUBLANES
    roles = ((rid_a, exp_a), (rid_b, exp_b))

    def copies(role, run):
        e = roles[role][1][run]
        ring = run % 2
        return [pltpu.make_async_copy(src.at[e], dst.at[role, ring], sem.at[role, ring])
                for src, dst in ((wg_hbm, wg_s), (wu_hbm, wu_s), (wd_hbm, wd_s))]

    for role in range(2):
        rid_ref = roles[role][0]
        run = rid_ref[j]
        first = jnp.logical_or(j == 0, run != rid_ref[jnp.maximum(j - 1, 0)])

        @pl.when(j == 0)
        def _():
            for c in copies(role, 0):
                c.start()

        @pl.when(first)
        def _():
            for c in copies(role, run):
                c.wait()
            ring = run % 2
            wg_b[role] = wg_s[role, ring].astype(BF16)
            wu_b[role] = wu_s[role, ring].astype(BF16)
            wd_b[role] = wd_s[role, ring].astype(BF16)

            @pl.when(run + 1 < nr_ref[role])
            def _():
                for c in copies(role, run + 1):
                    c.start()

    @pl.when(in_use)
    def _():
        x = _load_token_tiles(xs_ref).astype(BF16)
        for role in range(2):
            hg = jnp.dot(x, wg_b[role], preferred_element_type=F32)
            hu = jnp.dot(x, wu_b[role], preferred_element_type=F32)
            a = (hg * jax.nn.sigmoid(hg) * hu).astype(BF16)
            y = jnp.dot(a, wd_b[role], preferred_element_type=F32)
            for s in range(SUBLANES):
                ys_ref[pl.ds(role * SUBLANES + s, rows, stride=TOKEN_ROWS_OUT), :] = (
                    y[:, s * LANES:(s + 1) * LANES])

    @pl.when(jnp.logical_not(in_use))
    def _():
        ys_ref[...] = jnp.zeros_like(ys_ref)


def _expert_runs(te):
    first = jnp.concatenate([jnp.ones((1,), jnp.int32),
                             (te[1:] != te[:-1]).astype(jnp.int32)])
    rid = jnp.cumsum(first) - 1
    return rid, jnp.zeros_like(te).at[rid].set(te), rid[-1:] + 1


def _experts(tea, teb, nv, xs, wg, wu, wd):
    _, d, de = wg.shape
    assert d == SUBLANES * LANES
    n_tiles = xs.shape[0] // (MOE_TM * SUBLANES)
    rid_a, exp_a, nr_a = _expert_runs(tea)
    rid_b, exp_b, nr_b = _expert_runs(teb)
    grid_spec = pltpu.PrefetchScalarGridSpec(
        num_scalar_prefetch=6,
        grid=(n_tiles,),
        in_specs=[pl.BlockSpec((MOE_TM * SUBLANES, LANES),
                               lambda j, nv, *_: (jnp.minimum(j, nv[0] - 1), 0)),
                  pl.BlockSpec(memory_space=pl.ANY), pl.BlockSpec(memory_space=pl.ANY),
                  pl.BlockSpec(memory_space=pl.ANY)],
        out_specs=pl.BlockSpec((MOE_TM * TOKEN_ROWS_OUT, LANES), lambda j, *_: (j, 0)),
        scratch_shapes=[pltpu.VMEM((2, 2, d, de), F32), pltpu.VMEM((2, 2, d, de), F32),
                        pltpu.VMEM((2, 2, de, d), F32),
                        pltpu.VMEM((2, d, de), BF16), pltpu.VMEM((2, d, de), BF16),
                        pltpu.VMEM((2, de, d), BF16), pltpu.SemaphoreType.DMA((2, 2))],
    )
    return pl.pallas_call(
        _expert_kernel,
        grid_spec=grid_spec,
        out_shape=jax.ShapeDtypeStruct((n_tiles * MOE_TM * TOKEN_ROWS_OUT, LANES), F32),
        compiler_params=_params(("arbitrary",)),
        name="experts",
    )(nv, rid_a, rid_b, exp_a, exp_b, jnp.concatenate([nr_a, nr_b]), xs, wg, wu, wd)


def _combine_kernel(pos_ref, ys_ref, x1_ref, route_ref, fg_ref, o_ref, g, sem, *, tm):
    i = pl.program_id(0)
    slot = i % 2

    def gather(step, buf):
        base = step * tm

        def body(r2, carry):
            for k in range(2):
                r = 2 * r2 + k
                _token_copy(ys_ref, pos_ref[base + r], g.at[buf], r, sem.at[buf],
                            TOKEN_ROWS_OUT).start(priority=k)
            return carry

        lax.fori_loop(0, tm // 2, body, 0, unroll=DMA_UNROLL)

    @pl.when(i == 0)
    def _():
        gather(0, 0)

    @pl.when(i + 1 < pl.num_programs(0))
    def _():
        gather(i + 1, 1 - slot)

    pltpu.make_async_copy(ys_ref.at[pl.ds(0, tm * TOKEN_ROWS_OUT)], g.at[slot],
                          sem.at[slot]).wait()
    gs = g.at[slot]
    ya, yb = (jnp.concatenate([gs[pl.ds(off + s, tm, stride=TOKEN_ROWS_OUT), :]
                               for s in range(SUBLANES)], axis=1) for off in (0, SUBLANES))
    route = route_ref[...]
    x2 = x1_ref[...] + (route[:, ROUTE_WA:ROUTE_WA + 1] * ya + route[:, ROUTE_WB:ROUTE_WB + 1] * yb)
    var = jnp.mean(x2 * x2, axis=-1, keepdims=True)
    o_ref[...] = x2 * lax.rsqrt(var + EPS) * fg_ref[...]


def _combine(pos, ys, x1, route, final_g):
    n, d = x1.shape
    tm = min(COMBINE_TM, n)
    kern = functools.partial(_combine_kernel, tm=tm)
    grid_spec = pltpu.PrefetchScalarGridSpec(
        num_scalar_prefetch=1,
        grid=(n // tm,),
        in_specs=[pl.BlockSpec(memory_space=pl.ANY),
                  pl.BlockSpec((tm, d), lambda i, *_: (i, 0)),
                  pl.BlockSpec((tm, LANES), lambda i, *_: (i, 0)),
                  pl.BlockSpec((1, d), lambda i, *_: (0, 0))],
        out_specs=pl.BlockSpec((tm, d), lambda i, *_: (i, 0)),
        scratch_shapes=[pltpu.VMEM((2, tm * TOKEN_ROWS_OUT, LANES), F32),
                        pltpu.SemaphoreType.DMA((2,))],
    )
    return pl.pallas_call(
        kern,
        grid_spec=grid_spec,
        out_shape=jax.ShapeDtypeStruct((n, d), F32),
        compiler_params=_params(("arbitrary",)),
        name="combine",
    )(pos, ys, x1, route, final_g.reshape(1, d))


def _moe_sparse(t, route, tot, wg, wu, wd, x1, final_g):
    n = x1.shape[0]
    n_tiles_max = (n + N_CLASSES * (MOE_TM - 1)) // MOE_TM
    pos, te, meta = _plan(route, tot, n_tiles_max)
    pos = pos.reshape(n)
    xs = _dispatch(pos, meta[0, :N_CLASSES + 1], t, n_tiles_max * MOE_TM)
    ys = _experts(te[:n_tiles_max, 0], te[:n_tiles_max, 1], meta[0, N_CLASSES:N_CLASSES + 1],
                  xs, wg, wu, wd)
    return _combine(pos, ys, x1, route, final_g)


def _suffix_ones(t):
    j = np.arange(t)[:, None]
    s = np.arange(t)[None, :]
    return jnp.asarray((j > s).astype(np.float32), dtype=BF16)


def _causal_bias(t):
    row = np.arange(t)[:, None]
    col = np.arange(t)[None, :]
    diag = np.where(col < row, 0.0, MASK_BIAS).astype(np.float32)
    return jnp.asarray(np.stack([np.zeros_like(diag), diag]))


def _chunk_prefix_ones(tt, c):
    t = np.arange(tt)[:, None]
    j = np.arange(tt)[None, :]
    return jnp.asarray(((j <= t) & (t // c == j // c)).astype(np.float32), dtype=BF16)


def _block_diag_ones(w, blk):
    a = np.arange(w)
    return jnp.asarray((a[:, None] // blk == a[None, :] // blk).astype(np.float32), dtype=BF16)


def kernel(x, ln1_g, w_in, w_branch_sb, w_branch_hg, hg_norm_g, hg_lb_logits, w_out, ln2_g,
           w_router_group, b_router_group, w_router_expert, b_router_expert,
           w_exp_gate, w_exp_up, w_exp_down, final_g):
    bsz, seq, d = x.shape
    depth = w_in.shape[0]
    n = bsz * seq
    sb_width = SB_HEADS * SB_HEAD_DIM
    hg_width = HG_HEADS * HG_DIM

    tri_attn = _suffix_ones(min(ATTN_T, seq))
    bias_attn = _causal_bias(min(ATTN_T, seq))
    tt = min(HGRN_TT, seq)
    tri_hg = _chunk_prefix_ones(tt, min(HGRN_C, tt))
    bd = _block_diag_ones(min(MXU_DIM, hg_width), HG_DIM)

    x2 = x.reshape(n, d)
    for l in range(depth):
        qkv, rest = _inproj(x2, ln1_g[l], w_in[l], sb_width)
        y_sb = _attn(qkv, bsz, seq, tri_attn, bias_attn)
        y_hg = _hgrn(rest, hg_lb_logits, hg_norm_g[l], bsz, seq, l, tri_hg, bd)

        pad = LANES - N_EXPERTS - N_GROUPS
        wr = jnp.concatenate([w_router_expert[l], w_router_group[l],
                              jnp.zeros((d, pad), F32)], axis=1)
        wr_hi = wr.astype(BF16)
        wr_lo = (wr - wr_hi.astype(F32)).astype(BF16)
        wr_split = jnp.concatenate([wr_hi, wr_lo], axis=1)
        br = jnp.concatenate([b_router_expert[l], b_router_group[l],
                              jnp.zeros((pad,), F32)]).reshape(1, LANES)

        last = l == depth - 1
        x1, t, route, tot = _merge(x2, y_sb, y_hg, rest, w_branch_sb[l], w_branch_hg[l],
                                   w_out[l], ln2_g[l], wr_split, br)
        assert last, "final rmsnorm is fused into the last layer's combine kernel"
        x2 = _moe_sparse(t, route, tot, w_exp_gate[l], w_exp_up[l], w_exp_down[l], x1, final_g)
    return x2.reshape(bsz, seq, d)
```

```python
import functools

import jax
import jax.numpy as jnp
import numpy as np
from jax import lax
from jax.experimental import pallas as pl
from jax.experimental.pallas import tpu as pltpu

F32 = jnp.float32
BF16 = jnp.bfloat16

EPS = 1e-6
SB_HEADS = 8
SB_HEAD_DIM = 64
HG_HEADS = 8
HG_DIM = 64
N_GROUPS = 4
EXPERTS_PER_GROUP = 4
N_EXPERTS = N_GROUPS * EXPERTS_PER_GROUP

LANES = 128
SUBLANES = 8
MXU_DIM = 256
LOG2E = 1.4426950408889634
VMEM_LIMIT = 56 * 1024 * 1024

INPROJ_TM = 2048
INPROJ_TN = 512
ATTN_T = MXU_DIM
HGRN_TT = 256
HGRN_C = 32
MERGE_TM = 512
MOE_TM = 384
PLAN_TM = 1024
DISPATCH_TM = 1024
COMBINE_TM = 512


def _params(sem):
    return pltpu.CompilerParams(dimension_semantics=sem, vmem_limit_bytes=VMEM_LIMIT)


def _split_dot(x, m, passes):
    acc = None
    r = x
    for p in range(passes):
        h = r.astype(BF16)
        term = jnp.dot(h, m, preferred_element_type=F32)
        acc = term if acc is None else acc + term
        if p + 1 < passes:
            r = r - h.astype(F32)
    return acc


def _inproj_kernel(x_ref, g_ref, w_ref, qkv_ref, rest_ref, h_scr, w_scr, *, q_scale):
    i = pl.program_id(0)
    j = pl.program_id(1)

    @pl.when(j == 0)
    def _():
        x = x_ref[...]
        var = jnp.mean(x * x, axis=-1, keepdims=True)
        h_scr[...] = (x * lax.rsqrt(var + EPS) * g_ref[...]).astype(BF16)

    @pl.when(i == 0)
    def _():
        w_scr[j] = w_ref[...].astype(BF16)

    acc = jnp.dot(h_scr[...], w_scr[j], preferred_element_type=F32)
    qkv_ref[...] = (acc * jnp.where(j == 0, q_scale, 1.0)).astype(BF16)
    rest_ref[...] = acc


def _inproj(x2, ln_g, w_in, sb_width):
    n, d = x2.shape
    cols = w_in.shape[1]
    tm, tn = min(INPROJ_TM, n), INPROJ_TN
    assert sb_width == tn, "q block must be exactly one column tile"
    nq = 3 * sb_width // tn
    nj = cols // tn
    kern = functools.partial(_inproj_kernel, q_scale=SB_HEAD_DIM ** -0.5)
    return pl.pallas_call(
        kern,
        grid=(n // tm, nj),
        in_specs=[
            pl.BlockSpec((tm, d), lambda i, j: (i, 0)),
            pl.BlockSpec((1, d), lambda i, j: (0, 0)),
            pl.BlockSpec((d, tn), lambda i, j: (0, jnp.where(i == 0, j, nj - 1))),
        ],
        out_specs=[
            pl.BlockSpec((tm, tn), lambda i, j: (i, jnp.minimum(j, nq))),
            pl.BlockSpec((tm, tn), lambda i, j: (i, jnp.maximum(j - nq, 0))),
        ],
        out_shape=[
            jax.ShapeDtypeStruct((n, (nq + 1) * tn), BF16),
            jax.ShapeDtypeStruct((n, cols - nq * tn), F32),
        ],
        scratch_shapes=[pltpu.VMEM((tm, d), BF16), pltpu.VMEM((nj, d, tn), BF16)],
        compiler_params=_params(("arbitrary", "arbitrary")),
        name="inproj",
    )(x2, ln_g.reshape(1, d), w_in)


ATTN_STAGES = 3
ATTN_STREAMS = 2
MASK_BIAS = -1e30
ATTN_SKIP = 111.0


def _attn_kernel(q_ref, k_ref, v_ref, tri_ref, bias_ref, o_ref,
                 z0, z1, z2, i0, i1, i2, acc_ref, c_ref, *, t, nq):
    zbuf = (z0, z1, z2)
    ibuf = (i0, i1, i2)
    for r in zbuf + ibuf + (acc_ref, c_ref):
        r[...] = jnp.zeros_like(r)

    lane = lax.broadcasted_iota(jnp.int32, (t, LANES), 1)
    head0 = lane < SB_HEAD_DIM
    nt = (((1,), (1,)), ((), ()))

    def stage_a(qi, kj, slot):
        q = q_ref[pl.ds(pl.multiple_of(qi * t, t), t), :]
        zero = jnp.zeros_like(q)
        q2 = jnp.concatenate([jnp.where(head0, q, zero), jnp.where(head0, zero, q)], axis=0)
        k = k_ref[pl.ds(pl.multiple_of(kj * t, t), t), :]
        z = lax.dot_general(q2, k, nt, preferred_element_type=F32)
        bias = bias_ref[(qi == kj).astype(jnp.int32)]
        zbuf[slot][...] = z + jnp.concatenate([bias, bias], axis=0)

    def stage_b(slot):
        z = zbuf[slot][...]
        p = jnp.maximum(z, 0.0) + jnp.log(1.0 + jnp.exp2(jnp.abs(z) * (-LOG2E)))
        incl = jnp.dot(p.astype(BF16), tri_ref[...], preferred_element_type=F32) + p
        ibuf[slot][...] = incl
        return incl[:, 0:1]

    def stage_c(qi, kj, slot, stream):
        first = qi == kj
        incl = ibuf[slot][...]
        c = jnp.where(first, 0.0, c_ref[stream])
        a = jnp.exp2((zbuf[slot][...] - incl - c) * LOG2E)
        v = v_ref[pl.ds(pl.multiple_of(kj * t, t), t), :]
        pv = jnp.dot(a.astype(BF16), v, preferred_element_type=F32)
        acc = jnp.where(first, pv, acc_ref[stream] + pv)
        acc_ref[stream] = acc
        c_ref[stream] = c + incl[:, 0:1]
        o_ref[pl.ds(pl.multiple_of(qi * t, t), t), :] = (
            jnp.where(head0, acc[0:t], acc[t:2 * t]).astype(o_ref.dtype))

    def block(carry):
        qs, ks, qn, kn, skip_q, drained = carry
        qs, ks, qn, kn, skip_q = list(qs), list(ks), list(qn), list(kn), list(skip_q)
        all_done = qn[0] >= nq
        for p in range(1, ATTN_STREAMS):
            all_done = jnp.logical_and(all_done, qn[p] >= nq)
        drained = drained + all_done.astype(jnp.int32)
        for r in range(ATTN_STAGES * ATTN_STREAMS):
            sa, sc, sb = r % ATTN_STAGES, (r + 1) % ATTN_STAGES, (r + 2) % ATTN_STAGES
            pa, pb = r % ATTN_STREAMS, (r - 1) % ATTN_STREAMS
            stage_c(qs[sc], ks[sc], sc, pa)
            total_b = stage_b(sb)
            carry_b = jnp.where(qs[sb] == ks[sb], 0.0, c_ref[pb]) + total_b
            skip = skip_q[pa] == qn[pa]
            q_cur = jnp.where(skip, qn[pa] + ATTN_STREAMS, qn[pa])
            k_cur = jnp.where(skip, qn[pa] + ATTN_STREAMS, kn[pa])
            drain = q_cur >= nq
            qa = jnp.where(drain, 0, q_cur)
            ka = jnp.where(drain, 0, k_cur)
            stage_a(qa, ka, sa)
            last = k_cur == 0
            qn[pa] = jnp.where(jnp.logical_and(last, jnp.logical_not(drain)),
                               q_cur + ATTN_STREAMS, q_cur)
            kn[pa] = jnp.where(drain, k_cur, jnp.where(last, q_cur + ATTN_STREAMS, k_cur - 1))
            skip_q[pb] = jnp.where(jnp.min(carry_b) >= ATTN_SKIP, qs[sb], skip_q[pb])
            qs[sa], ks[sa] = qa, ka
        return tuple(qs), tuple(ks), tuple(qn), tuple(kn), tuple(skip_q), drained

    zero = jnp.int32(0)
    first_q = tuple(jnp.int32(p) for p in range(ATTN_STREAMS))
    init = ((zero,) * ATTN_STAGES, (zero,) * ATTN_STAGES, first_q, first_q,
            (jnp.int32(-1),) * ATTN_STREAMS, zero)
    lax.while_loop(lambda carry: carry[5] < 1, block, init)


def _attn(qkv, bsz, seq, tri, bias):
    n = bsz * seq
    t = min(ATTN_T, seq)
    pairs = SB_HEADS * SB_HEAD_DIM // LANES
    kern = functools.partial(_attn_kernel, t=t, nq=seq // t)
    return pl.pallas_call(
        kern,
        grid=(bsz, pairs),
        in_specs=[
            pl.BlockSpec((seq, LANES), lambda b, p: (b, p)),
            pl.BlockSpec((seq, LANES), lambda b, p: (b, pairs + p)),
            pl.BlockSpec((seq, LANES), lambda b, p: (b, 2 * pairs + p)),
            pl.BlockSpec((t, t), lambda b, p: (0, 0)),
            pl.BlockSpec((2, t, t), lambda b, p: (0, 0, 0)),
        ],
        out_specs=pl.BlockSpec((seq, LANES), lambda b, p: (b, p)),
        out_shape=jax.ShapeDtypeStruct((n, pairs * LANES), BF16),
        scratch_shapes=([pltpu.VMEM((2 * t, t), F32)] * (2 * ATTN_STAGES)
                        + [pltpu.VMEM((ATTN_STREAMS, 2 * t, LANES), F32),
                           pltpu.VMEM((ATTN_STREAMS, 2 * t, 1), F32)]),
        compiler_params=_params(("parallel", "parallel")),
        name="attn",
    )(qkv, qkv, qkv, tri, bias)


def _group_dot(x, bd):
    g = bd.shape[0]
    parts = [jnp.dot(x[:, i:i + g], bd, preferred_element_type=F32)
             for i in range(0, x.shape[1], g)]
    return jnp.concatenate(parts, axis=1)


def _hgrn_kernel(q_ref, f_ref, i_ref, g_ref, lbl_ref, ng_ref, tri_ref, bd_ref, o_ref,
                 st_scr, wpad, vpad, wsh, vsh, *, tt, c, layer):
    ti = pl.program_id(1)
    w = q_ref.shape[1]
    nch = tt // c

    @pl.when(ti == 0)
    def _():
        st_scr[...] = jnp.zeros_like(st_scr)

    lg = lbl_ref[...]
    e = jnp.exp(lg - jnp.max(lg, axis=0, keepdims=True))
    lb = jnp.sum(e[0:layer + 1], axis=0, keepdims=True) / jnp.sum(e, axis=0, keepdims=True)

    f = lb + (1.0 - lb) * jax.nn.sigmoid(f_ref[...])
    kk = 1.0 - f
    qv = q_ref[...]
    qs = qv * jax.nn.sigmoid(qv)
    v = i_ref[...]
    bd = bd_ref[...]
    cum = _split_dot_left(tri_ref[...], jnp.log(f) * LOG2E, 3)
    wk = jnp.log(kk) * LOG2E - cum

    pos = lax.broadcasted_iota(jnp.int32, (tt, 1), 0) % c
    wpad[0:SUBLANES, :] = jnp.zeros((SUBLANES, w), F32)
    vpad[0:SUBLANES, :] = jnp.zeros((SUBLANES, w), F32)
    wpad[SUBLANES:SUBLANES + tt, :] = wk
    vpad[SUBLANES:SUBLANES + tt, :] = v
    for b in range(SUBLANES):
        wb = wpad[SUBLANES - b:SUBLANES - b + tt, :]
        vb = vpad[SUBLANES - b:SUBLANES - b + tt, :]
        if b:
            wb = jnp.where(pos >= b, wb, -jnp.inf)
        wsh[b] = wb.reshape(nch, c, w)
        vsh[b] = vb.reshape(nch, c, w)

    qs3 = qs.reshape(nch, c, w)
    cum3 = cum.reshape(nch, c, w)
    acc = None
    for a in reversed(range(c // SUBLANES)):
        rows = c - SUBLANES * a
        qa = qs3[:, SUBLANES * a:, :].reshape(nch * rows, w)
        ca = cum3[:, SUBLANES * a:, :].reshape(nch * rows, w)
        acc_a = None
        for b in range(SUBLANES):
            wb = wsh[b, :, 0:rows, :].reshape(nch * rows, w)
            vb = vsh[b, :, 0:rows, :].reshape(nch * rows, w)
            dd = qa * jnp.exp2(ca + wb)
            term = _group_dot(dd.astype(BF16), bd) * vb
            acc_a = term if acc_a is None else acc_a + term
        acc_a = acc_a.reshape(nch, rows, w)
        if acc is not None:
            acc_a = acc_a + jnp.concatenate([jnp.zeros((nch, SUBLANES, w), F32), acc], axis=1)
        acc = acc_a
    acc = acc.reshape(tt, w)

    grp = st_scr.shape[1]
    bdmask = bd[0:grp, 0:grp] != 0
    outs = []
    for ci in range(nch):
        r0 = ci * c
        cum_c = cum[r0:r0 + c]
        last = cum_c[c - 1:c]
        qd = (qs[r0:r0 + c] * jnp.exp2(cum_c)).astype(BF16)
        kd = (kk[r0:r0 + c] * jnp.exp2(last - cum_c)).astype(BF16)
        vc = v[r0:r0 + c].astype(BF16)
        dec = jnp.exp2(last)
        o_parts = []
        for gi in range(w // grp):
            sl = slice(gi * grp, (gi + 1) * grp)
            st = st_scr[gi]
            o_parts.append(lax.dot_general(qd[:, sl], st.astype(BF16), (((1,), (1,)), ((), ())),
                                           preferred_element_type=F32))
            upd = lax.dot_general(vc[:, sl], kd[:, sl], (((0,), (0,)), ((), ())),
                                  preferred_element_type=F32)
            st_scr[gi] = st * dec[:, sl] + jnp.where(bdmask, upd, 0.0)
        outs.append(jnp.concatenate(o_parts, axis=1))
    o = acc + jnp.concatenate(outs, axis=0)

    o2 = o * o
    o2_hi = o2.astype(BF16)
    o2_lo = (o2 - o2_hi.astype(F32)).astype(BF16)
    ms = (_group_dot(o2_hi, bd) + _group_dot(o2_lo, bd)) * (1.0 / HG_DIM)
    gv = g_ref[...]
    o = o * lax.rsqrt(ms + EPS) * ng_ref[...] * (gv * jax.nn.sigmoid(gv))
    o_ref[...] = o.astype(o_ref.dtype)


def _split_dot_left(m, x, passes):
    acc = None
    r = x
    for p in range(passes):
        h = r.astype(BF16)
        term = jnp.dot(m, h, preferred_element_type=F32)
        acc = term if acc is None else acc + term
        if p + 1 < passes:
            r = r - h.astype(F32)
    return acc


def _hgrn(rest, lb_logits, norm_g, bsz, seq, layer, tri, bd):
    n = bsz * seq
    w = HG_HEADS * HG_DIM
    grp = bd.shape[0]
    tt = min(HGRN_TT, seq)
    c = min(HGRN_C, tt)
    nt = seq // tt
    kern = functools.partial(_hgrn_kernel, tt=tt, c=c, layer=layer)
    col = lambda j: pl.BlockSpec((tt, w), lambda b, i, j=j: (b * nt + i, j))
    const = lambda shape: pl.BlockSpec(shape, lambda b, i: (0, 0))
    return pl.pallas_call(
        kern,
        grid=(bsz, nt),
        in_specs=[col(0), col(1), col(2), col(3),
                  const(lb_logits.shape), const((1, w)), const((tt, tt)), const((grp, grp))],
        out_specs=pl.BlockSpec((tt, w), lambda b, i: (b * nt + i, 0)),
        out_shape=jax.ShapeDtypeStruct((n, w), BF16),
        scratch_shapes=([pltpu.VMEM((w // LANES, LANES, LANES), F32)]
                        + [pltpu.VMEM((SUBLANES + tt, w), F32)] * 2
                        + [pltpu.VMEM((SUBLANES, tt // c, c, w), F32)] * 2),
        compiler_params=_params(("parallel", "arbitrary")),
        name="hgrn",
    )(rest, rest, rest, rest, lb_logits, norm_g.reshape(1, w), tri, bd)


PAIR_A = (0, 0, 0, 1, 1, 3)
PAIR_B = (1, 2, 3, 3, 2, 2)
N_PAIRS = len(PAIR_A)
N_CLASSES = N_GROUPS * N_PAIRS
assert sorted(tuple(sorted(p)) for p in zip(PAIR_A, PAIR_B)) == [
    (a, b) for a in range(EXPERTS_PER_GROUP) for b in range(a + 1, EXPERTS_PER_GROUP)]
ROUTE_CLS, ROUTE_WA, ROUTE_WB = 0, 1, 2


def _pair_slots(pidx):
    a = b = jnp.zeros_like(pidx)
    for p in range(N_PAIRS):
        a = jnp.where(pidx == p, float(PAIR_A[p]), a)
        b = jnp.where(pidx == p, float(PAIR_B[p]), b)
    return a, b


def _pair_index(lo, hi):
    pidx = jnp.zeros_like(lo)
    for p in range(N_PAIRS):
        is_p = jnp.logical_and(lo == min(PAIR_A[p], PAIR_B[p]), hi == max(PAIR_A[p], PAIR_B[p]))
        pidx = jnp.where(is_p, float(p), pidx)
    return pidx


def _route(lg):
    lane = lax.broadcasted_iota(jnp.int32, lg.shape, 1)
    neg = jnp.float32(-jnp.inf)
    big = jnp.int32(LANES)
    gmask = jnp.logical_and(lane >= N_EXPERTS, lane < N_EXPERTS + N_GROUPS)
    gl = jnp.where(gmask, lg, neg)
    gmax = jnp.max(gl, axis=1, keepdims=True)
    gidx = jnp.min(jnp.where(gl == gmax, lane, big), axis=1, keepdims=True) - N_EXPERTS
    w_grp = 1.0 / jnp.sum(jnp.where(gmask, jnp.exp(gl - gmax), 0.0), axis=1, keepdims=True)
    in_grp = jnp.logical_and(lane < N_EXPERTS, lane // EXPERTS_PER_GROUP == gidx)
    l1 = jnp.where(in_grp, lg, neg)
    v1 = jnp.max(l1, axis=1, keepdims=True)
    i1 = jnp.min(jnp.where(l1 == v1, lane, big), axis=1, keepdims=True)
    l2 = jnp.where(jnp.logical_and(in_grp, lane != i1), lg, neg)
    v2 = jnp.max(l2, axis=1, keepdims=True)
    i2 = jnp.min(jnp.where(l2 == v2, lane, big), axis=1, keepdims=True)
    e2 = jnp.exp(v2 - v1)
    p1 = 1.0 / (1.0 + e2)
    p2 = e2 * p1
    loc1 = (i1 - gidx * EXPERTS_PER_GROUP).astype(F32)
    loc2 = (i2 - gidx * EXPERTS_PER_GROUP).astype(F32)
    pidx = _pair_index(jnp.minimum(loc1, loc2), jnp.maximum(loc1, loc2))
    cls = gidx.astype(F32) * N_PAIRS + pidx
    a_loc, _ = _pair_slots(pidx)
    first_is_a = loc1 == a_loc
    wa = jnp.where(first_is_a, p1, p2) * w_grp
    wb = jnp.where(first_is_a, p2, p1) * w_grp
    record = jnp.where(lane == ROUTE_CLS, cls,
                       jnp.where(lane == ROUTE_WA, wa, jnp.where(lane == ROUTE_WB, wb, 0.0)))
    return record, cls


def _store_token_tiles(ref, x):
    rows, d = x.shape
    assert d == SUBLANES * LANES
    for s in range(SUBLANES):
        ref[pl.ds(s, rows, stride=SUBLANES), :] = x[:, s * LANES:(s + 1) * LANES]


def _load_token_tiles(ref):
    rows = ref.shape[0] // SUBLANES
    return jnp.concatenate([ref[pl.ds(s, rows, stride=SUBLANES), :] for s in range(SUBLANES)],
                           axis=1)


def _merge_kernel(x_ref, ysb_ref, yhg_ref, gsb_ref, ghg_ref, wbs_ref, wbh_ref, wo_ref,
                  ln_ref, wr_ref, br_ref, x1_ref, t_ref, route_ref, tot_ref,
                  wbs_scr, wbh_scr, wo_scr):
    @pl.when(pl.program_id(0) == 0)
    def _():
        wbs_scr[...] = wbs_ref[...].astype(BF16)
        wbh_scr[...] = wbh_ref[...].astype(BF16)
        wo_scr[...] = wo_ref[...].astype(BF16)
        tot_ref[...] = jnp.zeros_like(tot_ref)

    a = jnp.dot(ysb_ref[...], wbs_scr[...], preferred_element_type=F32)
    b = jnp.dot(yhg_ref[...], wbh_scr[...], preferred_element_type=F32)
    merged = jax.nn.sigmoid(gsb_ref[...]) * a + jax.nn.sigmoid(ghg_ref[...]) * b
    x1 = x_ref[...] + jnp.dot(merged.astype(BF16), wo_scr[...], preferred_element_type=F32)
    x1_ref[...] = x1
    var = jnp.mean(x1 * x1, axis=-1, keepdims=True)
    t = x1 * lax.rsqrt(var + EPS) * ln_ref[...]
    _store_token_tiles(t_ref, t)
    t_hi = t.astype(BF16)
    t_lo = (t - t_hi.astype(F32)).astype(BF16)
    p_hi = jnp.dot(t_hi, wr_ref[...], preferred_element_type=F32)
    p_lo = jnp.dot(t_lo, wr_ref[...], preferred_element_type=F32)
    lg = (p_hi[:, :LANES] + p_hi[:, LANES:]) + (p_lo[:, :LANES] + p_lo[:, LANES:]) + br_ref[...]
    route, cls = _route(lg)
    route_ref[...] = route
    lane = lax.broadcasted_iota(jnp.int32, route.shape, 1)
    sel = jnp.where(lane.astype(F32) == cls, 1.0, 0.0)
    tot_ref[...] += jnp.broadcast_to(jnp.sum(sel, axis=0, keepdims=True), tot_ref.shape)


def _merge(x2, y_sb, y_hg, rest, wbs, wbh, wo, ln_g, wr, br):
    n, d = x2.shape
    tm = min(MERGE_TM, n)
    w_sb, w_hg = y_sb.shape[1], y_hg.shape[1]
    gate_blk = (rest.shape[1] - 2 * d) // d
    row = lambda wdt, j=0: pl.BlockSpec((tm, wdt), lambda i, j=j: (i, j))
    const = lambda shape: pl.BlockSpec(shape, lambda i: (0, 0))
    return pl.pallas_call(
        _merge_kernel,
        grid=(n // tm,),
        in_specs=[row(d), row(w_sb), row(w_hg), row(d, gate_blk), row(d, gate_blk + 1),
                  const(wbs.shape), const(wbh.shape), const(wo.shape), const((1, d)),
                  const(wr.shape), const((1, LANES))],
        out_specs=[row(d), pl.BlockSpec((tm * SUBLANES, LANES), lambda i: (i, 0)), row(LANES),
                   const((SUBLANES, LANES))],
        out_shape=[jax.ShapeDtypeStruct((n, d), F32),
                   jax.ShapeDtypeStruct((n * SUBLANES, LANES), F32),
                   jax.ShapeDtypeStruct((n, LANES), F32),
                   jax.ShapeDtypeStruct((SUBLANES, LANES), F32)],
        scratch_shapes=[pltpu.VMEM(w.shape, BF16) for w in (wbs, wbh, wo)],
        compiler_params=_params(("arbitrary",)),
        name="merge",
    )(x2, y_sb, y_hg, rest, rest, wbs, wbh, wo, ln_g.reshape(1, d), wr, br)


TOKEN_ROWS_OUT = 2 * SUBLANES


def _plan_kernel(route_ref, tot_ref, ltri_ref, utri_ref, pos_ref, te_ref, meta_ref,
                 run_scr, *, tile_rows):
    i = pl.program_id(0)
    route = route_ref[...]
    tm = route.shape[0]
    lane = lax.broadcasted_iota(jnp.int32, (tm, LANES), 1)
    is_cls = lane.astype(F32) == route[:, ROUTE_CLS:ROUTE_CLS + 1]
    sel = jnp.where(is_cls, 1.0, 0.0)

    tot = tot_ref[0:1, :]
    top = tot + (tile_rows - 1.0)
    tiles = jnp.floor(top * (1.0 / tile_rows))
    tiles = jnp.where(tiles * tile_rows > top, tiles - 1.0, tiles)
    tiles = jnp.where((tiles + 1.0) * tile_rows <= top, tiles + 1.0, tiles)
    first_tile = _split_dot(jnp.broadcast_to(tiles, (SUBLANES, LANES)), utri_ref[...], 3)[0:1]

    @pl.when(i == 0)
    def _():
        run_scr[...] = jnp.zeros_like(run_scr)
        lane1 = lax.broadcasted_iota(jnp.int32, (1, LANES), 1)
        is_c = lane1 < N_CLASSES
        end_tile = first_tile + tiles
        n_valid = jnp.sum(jnp.where(is_c, tiles, 0.0), axis=1, keepdims=True)
        rows = te_ref.shape[0]
        j = lax.broadcasted_iota(jnp.int32, (rows, LANES), 0).astype(F32)
        j = jnp.minimum(j, n_valid - 1.0)
        lane2 = lax.broadcasted_iota(jnp.int32, (rows, LANES), 1)
        done = jnp.where(jnp.logical_and(lane2 < N_CLASSES, end_tile <= j), 1.0, 0.0)
        tc = jnp.sum(done, axis=1, keepdims=True)
        tg = jnp.floor((tc + 0.5) * (1.0 / N_PAIRS))
        ta, tb = _pair_slots(tc - tg * N_PAIRS)
        te = jnp.where(lane2 == 0, tg * EXPERTS_PER_GROUP + ta, tg * EXPERTS_PER_GROUP + tb)
        te_ref[...] = te.astype(jnp.int32)
        last_tile = jnp.where(jnp.logical_and(is_c, tiles > 0), end_tile - 1.0, -1.0)
        meta = jnp.where(lane1 == N_CLASSES, n_valid, last_tile)
        meta_ref[...] = jnp.broadcast_to(meta, meta_ref.shape).astype(jnp.int32)

    rank = jnp.dot(ltri_ref[...], sel.astype(BF16), preferred_element_type=F32) + run_scr[...]
    dest = jnp.sum(jnp.where(is_cls, first_tile * tile_rows + rank, 0.0), axis=1, keepdims=True)
    row = lax.broadcasted_iota(jnp.int32, (tm, LANES), 0)
    spread = jnp.where(lane == row % LANES, dest, 0.0)
    pos_ref[...] = jnp.sum(spread.reshape(tm // LANES, LANES, LANES), axis=1).astype(jnp.int32)
    run_scr[...] += jnp.sum(sel, axis=0, keepdims=True)


def _plan(route, tot, n_tiles_max):
    n = route.shape[0]
    tm = min(PLAN_TM, n)
    r = np.arange(tm)
    ltri = jnp.asarray((r[None, :] < r[:, None]).astype(np.float32), dtype=BF16)
    e = np.arange(LANES)
    utri = jnp.asarray((e[:, None] < e[None, :]).astype(np.float32), dtype=BF16)
    te_rows = -(-n_tiles_max // SUBLANES) * SUBLANES
    kern = functools.partial(_plan_kernel, tile_rows=MOE_TM)
    return pl.pallas_call(
        kern,
        grid=(n // tm,),
        in_specs=[pl.BlockSpec((tm, LANES), lambda i: (i, 0)),
                  pl.BlockSpec((SUBLANES, LANES), lambda i: (0, 0)),
                  pl.BlockSpec((tm, tm), lambda i: (0, 0)),
                  pl.BlockSpec((LANES, LANES), lambda i: (0, 0))],
        out_specs=[pl.BlockSpec((tm // LANES, LANES), lambda i: (i, 0)),
                   pl.BlockSpec((te_rows, LANES), lambda i: (0, 0)),
                   pl.BlockSpec((SUBLANES, LANES), lambda i: (0, 0))],
        out_shape=[jax.ShapeDtypeStruct((n // LANES, LANES), jnp.int32),
                   jax.ShapeDtypeStruct((te_rows, LANES), jnp.int32),
                   jax.ShapeDtypeStruct((SUBLANES, LANES), jnp.int32)],
        scratch_shapes=[pltpu.VMEM((1, LANES), F32)],
        compiler_params=_params(("arbitrary",)),
        name="plan",
    )(route, tot, ltri, utri)


DMA_UNROLL = 8


def _token_copy(src_ref, r, dst_ref, p, sem, rows):
    return pltpu.make_async_copy(src_ref.at[pl.ds(pl.multiple_of(r * rows, rows), rows)],
                                 dst_ref.at[pl.ds(pl.multiple_of(p * rows, rows), rows)], sem)


def _dispatch_kernel(pos_ref, meta_ref, t_ref, xs_ref, zero_scr, sem, tail_sem, *, tm, tile_rows):
    i = pl.program_id(0)
    tile_rows = tile_rows * SUBLANES
    n_tiles = xs_ref.shape[0] // tile_rows
    n_valid = meta_ref[N_CLASSES]

    def clear(tile, s):
        return pltpu.make_async_copy(zero_scr, xs_ref.at[pl.ds(tile * tile_rows, tile_rows)], s)

    def for_each(clears, fn):
        for tile, cond in clears:
            @pl.when(cond)
            def _():
                fn(tile)

    last_tiles = [(meta_ref[c], meta_ref[c] >= 0) for c in range(N_CLASSES)]
    past_tiles = [(n_tiles - 1 - k, n_tiles - 1 - k >= n_valid) for k in range(N_CLASSES)]

    @pl.when(i == 0)
    def _():
        zero_scr[...] = jnp.zeros_like(zero_scr)
        for_each(last_tiles, lambda tile: clear(tile, sem).start())
        for_each(past_tiles, lambda tile: clear(tile, tail_sem).start())
        for_each(last_tiles, lambda tile: clear(tile, sem).wait())

    base = i * tm

    def start(r2, carry):
        for k in range(2):
            r = 2 * r2 + k
            _token_copy(t_ref, r, xs_ref, pos_ref[base + r], sem, SUBLANES).start(priority=k)
        return carry

    lax.fori_loop(0, tm // 2, start, 0, unroll=DMA_UNROLL)
    pltpu.make_async_copy(t_ref, xs_ref.at[pl.ds(0, tm * SUBLANES)], sem).wait()

    @pl.when(i == pl.num_programs(0) - 1)
    def _():
        for_each(past_tiles, lambda tile: clear(tile, tail_sem).wait())


def _dispatch(pos, meta, t, n_rows):
    n = t.shape[0] // SUBLANES
    tm = min(DISPATCH_TM, n)
    kern = functools.partial(_dispatch_kernel, tm=tm, tile_rows=MOE_TM)
    grid_spec = pltpu.PrefetchScalarGridSpec(
        num_scalar_prefetch=2,
        grid=(n // tm,),
        in_specs=[pl.BlockSpec((tm * SUBLANES, LANES), lambda i, *_: (i, 0))],
        out_specs=pl.BlockSpec(memory_space=pl.ANY),
        scratch_shapes=[pltpu.VMEM((MOE_TM * SUBLANES, LANES), F32),
                        pltpu.SemaphoreType.DMA(()), pltpu.SemaphoreType.DMA(())],
    )
    return pl.pallas_call(
        kern,
        grid_spec=grid_spec,
        out_shape=jax.ShapeDtypeStruct((n_rows * SUBLANES, LANES), F32),
        compiler_params=_params(("arbitrary",)),
        name="dispatch",
    )(pos, meta, t)


def _expert_kernel(nv_ref, rid_a, rid_b, exp_a, exp_b, nr_ref, xs_ref, wg_hbm, wu_hbm, wd_hbm,
                   ys_ref, wg_s, wu_s, wd_s, y_s, zero_s, sem, out_sem, zero_sem):
    j = pl.program_id(0)
    in_use = j < nv_ref[0]
    rows = xs_ref.shape[0] // SUBLANES
    roles = ((rid_a, exp_a), (rid_b, exp_b))

    def copies(role, run):
        e = roles[role][1][run]
        ring = run % 2
        return [pltpu.make_async_copy(src.at[e], dst.at[role, ring], sem.at[role, ring])
                for src, dst in ((wg_hbm, wg_s), (wu_hbm, wu_s), (wd_hbm, wd_s))]

    for role in range(2):
        rid_ref = roles[role][0]
        run = rid_ref[j]
        first = jnp.logical_or(j == 0, run != rid_ref[jnp.maximum(j - 1, 0)])

        @pl.when(j == 0)
        def _():
            for c in copies(role, 0):
                c.start()

        @pl.when(first)
        def _():
            for c in copies(role, run):
                c.wait()

            @pl.when(run + 1 < nr_ref[role])
            def _():
                for c in copies(role, run + 1):
                    c.start()

    n_tiles = pl.num_programs(0)
    nv = nv_ref[0]
    out_rows = y_s.shape[1]
    last = j == n_tiles - 1

    def out_copy(t):
        return pltpu.make_async_copy(y_s.at[t % 2], ys_ref.at[pl.ds(t * out_rows, out_rows)],
                                     out_sem.at[t % 2])

    def zero_copy(t):
        return pltpu.make_async_copy(zero_s, ys_ref.at[pl.ds(t * out_rows, out_rows)], zero_sem)

    past = [(n_tiles - 1 - k, n_tiles - 1 - k >= nv) for k in range(N_CLASSES)]

    @pl.when(j == 0)
    def _():
        zero_s[...] = jnp.zeros_like(zero_s)
        for t, cond in past:
            @pl.when(cond)
            def _():
                zero_copy(t).start()

    @pl.when(jnp.logical_and(j >= 2, j - 2 < nv))
    def _():
        out_copy(j - 2).wait()

    @pl.when(in_use)
    def _():
        x = _load_token_tiles(xs_ref).astype(BF16)
        y_t = y_s.at[j % 2]
        for role in range(2):
            ring = roles[role][0][j] % 2
            hg = jnp.dot(x, wg_s[role, ring].astype(BF16), preferred_element_type=F32)
            hu = jnp.dot(x, wu_s[role, ring].astype(BF16), preferred_element_type=F32)
            a = (hg * jax.nn.sigmoid(hg) * hu).astype(BF16)
            y = jnp.dot(a, wd_s[role, ring].astype(BF16), preferred_element_type=F32)
            for s in range(SUBLANES):
                y_t[pl.ds(role * SUBLANES + s, rows, stride=TOKEN_ROWS_OUT), :] = (
                    y[:, s * LANES:(s + 1) * LANES])
        out_copy(j).start()

    @pl.when(last)
    def _():
        @pl.when(jnp.logical_and(j >= 1, j - 1 < nv))
        def _():
            out_copy(j - 1).wait()

        @pl.when(in_use)
        def _():
            out_copy(j).wait()

        for t, cond in past:
            @pl.when(cond)
            def _():
                zero_copy(t).wait()


def _expert_runs(te):
    first = jnp.concatenate([jnp.ones((1,), jnp.int32),
                             (te[1:] != te[:-1]).astype(jnp.int32)])
    rid = jnp.cumsum(first) - 1
    return rid, jnp.zeros_like(te).at[rid].set(te), rid[-1:] + 1


def _experts(tea, teb, nv, xs, wg, wu, wd):
    _, d, de = wg.shape
    assert d == SUBLANES * LANES
    n_tiles = xs.shape[0] // (MOE_TM * SUBLANES)
    rid_a, exp_a, nr_a = _expert_runs(tea)
    rid_b, exp_b, nr_b = _expert_runs(teb)
    grid_spec = pltpu.PrefetchScalarGridSpec(
        num_scalar_prefetch=6,
        grid=(n_tiles,),
        in_specs=[pl.BlockSpec((MOE_TM * SUBLANES, LANES),
                               lambda j, nv, *_: (jnp.minimum(j, nv[0] - 1), 0)),
                  pl.BlockSpec(memory_space=pl.ANY), pl.BlockSpec(memory_space=pl.ANY),
                  pl.BlockSpec(memory_space=pl.ANY)],
        out_specs=pl.BlockSpec(memory_space=pl.ANY),
        scratch_shapes=[pltpu.VMEM((2, 2, d, de), F32), pltpu.VMEM((2, 2, d, de), F32),
                        pltpu.VMEM((2, 2, de, d), F32),
                        pltpu.VMEM((2, MOE_TM * TOKEN_ROWS_OUT, LANES), F32),
                        pltpu.VMEM((MOE_TM * TOKEN_ROWS_OUT, LANES), F32),
                        pltpu.SemaphoreType.DMA((2, 2)), pltpu.SemaphoreType.DMA((2,)),
                        pltpu.SemaphoreType.DMA(())],
    )
    return pl.pallas_call(
        _expert_kernel,
        grid_spec=grid_spec,
        out_shape=jax.ShapeDtypeStruct((n_tiles * MOE_TM * TOKEN_ROWS_OUT, LANES), F32),
        compiler_params=_params(("arbitrary",)),
        name="experts",
    )(nv, rid_a, rid_b, exp_a, exp_b, jnp.concatenate([nr_a, nr_b]), xs, wg, wu, wd)


def _combine_kernel(pos_ref, ys_ref, x1_ref, route_ref, fg_ref, o_ref, g, sem, *, tm):
    i = pl.program_id(0)
    slot = i % 2

    def gather(step, buf):
        base = step * tm

        def body(r2, carry):
            for k in range(2):
                r = 2 * r2 + k
                _token_copy(ys_ref, pos_ref[base + r], g.at[buf], r, sem.at[buf],
                            TOKEN_ROWS_OUT).start(priority=k)
            return carry

        lax.fori_loop(0, tm // 2, body, 0, unroll=DMA_UNROLL)

    @pl.when(i == 0)
    def _():
        gather(0, 0)

    @pl.when(i + 1 < pl.num_programs(0))
    def _():
        gather(i + 1, 1 - slot)

    pltpu.make_async_copy(ys_ref.at[pl.ds(0, tm * TOKEN_ROWS_OUT)], g.at[slot],
                          sem.at[slot]).wait()
    gs = g.at[slot]
    ya, yb = (jnp.concatenate([gs[pl.ds(off + s, tm, stride=TOKEN_ROWS_OUT), :]
                               for s in range(SUBLANES)], axis=1) for off in (0, SUBLANES))
    route = route_ref[...]
    x2 = x1_ref[...] + (route[:, ROUTE_WA:ROUTE_WA + 1] * ya + route[:, ROUTE_WB:ROUTE_WB + 1] * yb)
    var = jnp.mean(x2 * x2, axis=-1, keepdims=True)
    o_ref[...] = x2 * lax.rsqrt(var + EPS) * fg_ref[...]


def _combine(pos, ys, x1, route, final_g):
    n, d = x1.shape
    tm = min(COMBINE_TM, n)
    kern = functools.partial(_combine_kernel, tm=tm)
    grid_spec = pltpu.PrefetchScalarGridSpec(
        num_scalar_prefetch=1,
        grid=(n // tm,),
        in_specs=[pl.BlockSpec(memory_space=pl.ANY),
                  pl.BlockSpec((tm, d), lambda i, *_: (i, 0)),
                  pl.BlockSpec((tm, LANES), lambda i, *_: (i, 0)),
                  pl.BlockSpec((1, d), lambda i, *_: (0, 0))],
        out_specs=pl.BlockSpec((tm, d), lambda i, *_: (i, 0)),
        scratch_shapes=[pltpu.VMEM((2, tm * TOKEN_ROWS_OUT, LANES), F32),
                        pltpu.SemaphoreType.DMA((2,))],
    )
    return pl.pallas_call(
        kern,
        grid_spec=grid_spec,
        out_shape=jax.ShapeDtypeStruct((n, d), F32),
        compiler_params=_params(("arbitrary",)),
        name="combine",
    )(pos, ys, x1, route, final_g.reshape(1, d))


def _moe_sparse(t, route, tot, wg, wu, wd, x1, final_g):
    n = x1.shape[0]
    n_tiles_max = (n + N_CLASSES * (MOE_TM - 1)) // MOE_TM
    pos, te, meta = _plan(route, tot, n_tiles_max)
    pos = pos.reshape(n)
    xs = _dispatch(pos, meta[0, :N_CLASSES + 1], t, n_tiles_max * MOE_TM)
    ys = _experts(te[:n_tiles_max, 0], te[:n_tiles_max, 1], meta[0, N_CLASSES:N_CLASSES + 1],
                  xs, wg, wu, wd)
    return _combine(pos, ys, x1, route, final_g)


def _suffix_ones(t):
    j = np.arange(t)[:, None]
    s = np.arange(t)[None, :]
    return jnp.asarray((j > s).astype(np.float32), dtype=BF16)


def _causal_bias(t):
    row = np.arange(t)[:, None]
    col = np.arange(t)[None, :]
    diag = np.where(col < row, 0.0, MASK_BIAS).astype(np.float32)
    return jnp.asarray(np.stack([np.zeros_like(diag), diag]))


def _chunk_prefix_ones(tt, c):
    t = np.arange(tt)[:, None]
    j = np.arange(tt)[None, :]
    return jnp.asarray(((j <= t) & (t // c == j // c)).astype(np.float32), dtype=BF16)


def _block_diag_ones(w, blk):
    a = np.arange(w)
    return jnp.asarray((a[:, None] // blk == a[None, :] // blk).astype(np.float32), dtype=BF16)


def kernel(x, ln1_g, w_in, w_branch_sb, w_branch_hg, hg_norm_g, hg_lb_logits, w_out, ln2_g,
           w_router_group, b_router_group, w_router_expert, b_router_expert,
           w_exp_gate, w_exp_up, w_exp_down, final_g):
    bsz, seq, d = x.shape
    depth = w_in.shape[0]
    n = bsz * seq
    sb_width = SB_HEADS * SB_HEAD_DIM
    hg_width = HG_HEADS * HG_DIM

    tri_attn = _suffix_ones(min(ATTN_T, seq))
    bias_attn = _causal_bias(min(ATTN_T, seq))
    tt = min(HGRN_TT, seq)
    tri_hg = _chunk_prefix_ones(tt, min(HGRN_C, tt))
    bd = _block_diag_ones(min(MXU_DIM, hg_width), HG_DIM)

    x2 = x.reshape(n, d)
    for l in range(depth):
        qkv, rest = _inproj(x2, ln1_g[l], w_in[l], sb_width)
        y_sb = _attn(qkv, bsz, seq, tri_attn, bias_attn)
        y_hg = _hgrn(rest, hg_lb_logits, hg_norm_g[l], bsz, seq, l, tri_hg, bd)

        pad = LANES - N_EXPERTS - N_GROUPS
        wr = jnp.concatenate([w_router_expert[l], w_router_group[l],
                              jnp.zeros((d, pad), F32)], axis=1)
        wr_hi = wr.astype(BF16)
        wr_lo = (wr - wr_hi.astype(F32)).astype(BF16)
        wr_split = jnp.concatenate([wr_hi, wr_lo], axis=1)
        br = jnp.concatenate([b_router_expert[l], b_router_group[l],
                              jnp.zeros((pad,), F32)]).reshape(1, LANES)

        last = l == depth - 1
        x1, t, route, tot = _merge(x2, y_sb, y_hg, rest, w_branch_sb[l], w_branch_hg[l],
                                   w_out[l], ln2_g[l], wr_split, br)
        assert last, "final rmsnorm is fused into the last layer's combine kernel"
        x2 = _moe_sparse(t, route, tot, w_exp_gate[l], w_exp_up[l], w_exp_down[l], x1, final_g)
    return x2.reshape(bsz, seq, d)
```

```python
import functools

import jax
import jax.numpy as jnp
import numpy as np
from jax import lax
from jax.experimental import pallas as pl
from jax.experimental.pallas import tpu as pltpu

F32 = jnp.float32
BF16 = jnp.bfloat16

EPS = 1e-6
SB_HEADS = 8
SB_HEAD_DIM = 64
HG_HEADS = 8
HG_DIM = 64
N_GROUPS = 4
EXPERTS_PER_GROUP = 4
N_EXPERTS = N_GROUPS * EXPERTS_PER_GROUP

LANES = 128
SUBLANES = 8
MXU_DIM = 256
LOG2E = 1.4426950408889634
VMEM_LIMIT = 56 * 1024 * 1024

INPROJ_TM = 2048
INPROJ_TN = 512
ATTN_T = MXU_DIM
HGRN_TT = 256
HGRN_C = 32
MERGE_TM = 512
MOE_TM = 384
PLAN_TM = 1024
DISPATCH_TM = 1024
COMBINE_TM = 512


def _params(sem):
    return pltpu.CompilerParams(dimension_semantics=sem, vmem_limit_bytes=VMEM_LIMIT)


def _split_dot(x, m, passes):
    acc = None
    r = x
    for p in range(passes):
        h = r.astype(BF16)
        term = jnp.dot(h, m, preferred_element_type=F32)
        acc = term if acc is None else acc + term
        if p + 1 < passes:
            r = r - h.astype(F32)
    return acc


def _inproj_kernel(x_hbm, g_ref, w_ref, qkv_ref, rest_ref, h_scr, w_scr, x_s, x_sem, *,
                   q_scale):
    i = pl.program_id(0)
    j = pl.program_id(1)
    tm = h_scr.shape[0]

    def x_copy(t):
        return pltpu.make_async_copy(x_hbm.at[pl.ds(t * tm, tm)], x_s.at[t % 2], x_sem.at[t % 2])

    @pl.when(j == 0)
    def _():
        @pl.when(i == 0)
        def _():
            x_copy(0).start()

        x_copy(i).wait()

        @pl.when(i + 1 < pl.num_programs(0))
        def _():
            x_copy(i + 1).start()

        x = x_s[i % 2]
        var = jnp.mean(x * x, axis=-1, keepdims=True)
        h_scr[...] = (x * lax.rsqrt(var + EPS) * g_ref[...]).astype(BF16)

    @pl.when(i == 0)
    def _():
        w_scr[j] = w_ref[...].astype(BF16)

    acc = jnp.dot(h_scr[...], w_scr[j], preferred_element_type=F32)
    qkv_ref[...] = (acc * jnp.where(j == 0, q_scale, 1.0)).astype(BF16)
    rest_ref[...] = acc


def _inproj(x2, ln_g, w_in, sb_width):
    n, d = x2.shape
    cols = w_in.shape[1]
    tm, tn = min(INPROJ_TM, n), INPROJ_TN
    assert sb_width == tn, "q block must be exactly one column tile"
    nq = 3 * sb_width // tn
    nj = cols // tn
    kern = functools.partial(_inproj_kernel, q_scale=SB_HEAD_DIM ** -0.5)
    return pl.pallas_call(
        kern,
        grid=(n // tm, nj),
        in_specs=[
            pl.BlockSpec(memory_space=pl.ANY),
            pl.BlockSpec((1, d), lambda i, j: (0, 0)),
            pl.BlockSpec((d, tn), lambda i, j: (0, jnp.where(i == 0, j, nj - 1))),
        ],
        out_specs=[
            pl.BlockSpec((tm, tn), lambda i, j: (i, jnp.minimum(j, nq))),
            pl.BlockSpec((tm, tn), lambda i, j: (i, jnp.maximum(j - nq, 0))),
        ],
        out_shape=[
            jax.ShapeDtypeStruct((n, (nq + 1) * tn), BF16),
            jax.ShapeDtypeStruct((n, cols - nq * tn), F32),
        ],
        scratch_shapes=[pltpu.VMEM((tm, d), BF16), pltpu.VMEM((nj, d, tn), BF16),
                        pltpu.VMEM((2, tm, d), F32), pltpu.SemaphoreType.DMA((2,))],
        compiler_params=_params(("arbitrary", "arbitrary")),
        name="inproj",
    )(x2, ln_g.reshape(1, d), w_in)


ATTN_STAGES = 3
ATTN_STREAMS = 2
MASK_BIAS = -1e30
ATTN_SKIP = 111.0


def _attn_kernel(q_ref, k_ref, v_ref, tri_ref, bias_ref, o_ref,
                 z0, z1, z2, i0, i1, i2, acc_ref, c_ref, *, t, nq):
    zbuf = (z0, z1, z2)
    ibuf = (i0, i1, i2)
    for r in zbuf + ibuf + (acc_ref, c_ref):
        r[...] = jnp.zeros_like(r)

    lane = lax.broadcasted_iota(jnp.int32, (t, LANES), 1)
    head0 = lane < SB_HEAD_DIM
    nt = (((1,), (1,)), ((), ()))

    def stage_a(qi, kj, slot):
        q = q_ref[pl.ds(pl.multiple_of(qi * t, t), t), :]
        zero = jnp.zeros_like(q)
        q2 = jnp.concatenate([jnp.where(head0, q, zero), jnp.where(head0, zero, q)], axis=0)
        k = k_ref[pl.ds(pl.multiple_of(kj * t, t), t), :]
        z = lax.dot_general(q2, k, nt, preferred_element_type=F32)
        bias = bias_ref[(qi == kj).astype(jnp.int32)]
        zbuf[slot][...] = z + jnp.concatenate([bias, bias], axis=0)

    def stage_b(slot):
        z = zbuf[slot][...]
        p = jnp.maximum(z, 0.0) + jnp.log(1.0 + jnp.exp2(jnp.abs(z) * (-LOG2E)))
        incl = jnp.dot(p.astype(BF16), tri_ref[...], preferred_element_type=F32) + p
        ibuf[slot][...] = incl
        return incl[:, 0:1]

    def stage_c(qi, kj, slot, stream):
        first = qi == kj
        incl = ibuf[slot][...]
        c = jnp.where(first, 0.0, c_ref[stream])
        a = jnp.exp2((zbuf[slot][...] - incl - c) * LOG2E)
        v = v_ref[pl.ds(pl.multiple_of(kj * t, t), t), :]
        pv = jnp.dot(a.astype(BF16), v, preferred_element_type=F32)
        acc = jnp.where(first, pv, acc_ref[stream] + pv)
        acc_ref[stream] = acc
        c_ref[stream] = c + incl[:, 0:1]
        o_ref[pl.ds(pl.multiple_of(qi * t, t), t), :] = (
            jnp.where(head0, acc[0:t], acc[t:2 * t]).astype(o_ref.dtype))

    def block(carry):
        qs, ks, qn, kn, skip_q, drained = carry
        qs, ks, qn, kn, skip_q = list(qs), list(ks), list(qn), list(kn), list(skip_q)
        all_done = qn[0] >= nq
        for p in range(1, ATTN_STREAMS):
            all_done = jnp.logical_and(all_done, qn[p] >= nq)
        drained = drained + all_done.astype(jnp.int32)
        for r in range(ATTN_STAGES * ATTN_STREAMS):
            sa, sc, sb = r % ATTN_STAGES, (r + 1) % ATTN_STAGES, (r + 2) % ATTN_STAGES
            pa, pb = r % ATTN_STREAMS, (r - 1) % ATTN_STREAMS
            stage_c(qs[sc], ks[sc], sc, pa)
            total_b = stage_b(sb)
            carry_b = jnp.where(qs[sb] == ks[sb], 0.0, c_ref[pb]) + total_b
            skip = skip_q[pa] == qn[pa]
            q_cur = jnp.where(skip, qn[pa] + ATTN_STREAMS, qn[pa])
            k_cur = jnp.where(skip, qn[pa] + ATTN_STREAMS, kn[pa])
            drain = q_cur >= nq
            qa = jnp.where(drain, 0, q_cur)
            ka = jnp.where(drain, 0, k_cur)
            stage_a(qa, ka, sa)
            last = k_cur == 0
            qn[pa] = jnp.where(jnp.logical_and(last, jnp.logical_not(drain)),
                               q_cur + ATTN_STREAMS, q_cur)
            kn[pa] = jnp.where(drain, k_cur, jnp.where(last, q_cur + ATTN_STREAMS, k_cur - 1))
            skip_q[pb] = jnp.where(jnp.min(carry_b) >= ATTN_SKIP, qs[sb], skip_q[pb])
            qs[sa], ks[sa] = qa, ka
        return tuple(qs), tuple(ks), tuple(qn), tuple(kn), tuple(skip_q), drained

    zero = jnp.int32(0)
    first_q = tuple(jnp.int32(p) for p in range(ATTN_STREAMS))
    init = ((zero,) * ATTN_STAGES, (zero,) * ATTN_STAGES, first_q, first_q,
            (jnp.int32(-1),) * ATTN_STREAMS, zero)
    lax.while_loop(lambda carry: carry[5] < 1, block, init)


def _attn(qkv, bsz, seq, tri, bias):
    n = bsz * seq
    t = min(ATTN_T, seq)
    pairs = SB_HEADS * SB_HEAD_DIM // LANES
    kern = functools.partial(_attn_kernel, t=t, nq=seq // t)
    return pl.pallas_call(
        kern,
        grid=(bsz, pairs),
        in_specs=[
            pl.BlockSpec((seq, LANES), lambda b, p: (b, p)),
            pl.BlockSpec((seq, LANES), lambda b, p: (b, pairs + p)),
            pl.BlockSpec((seq, LANES), lambda b, p: (b, 2 * pairs + p)),
            pl.BlockSpec((t, t), lambda b, p: (0, 0)),
            pl.BlockSpec((2, t, t), lambda b, p: (0, 0, 0)),
        ],
        out_specs=pl.BlockSpec((seq, LANES), lambda b, p: (b, p)),
        out_shape=jax.ShapeDtypeStruct((n, pairs * LANES), BF16),
        scratch_shapes=([pltpu.VMEM((2 * t, t), F32)] * (2 * ATTN_STAGES)
                        + [pltpu.VMEM((ATTN_STREAMS, 2 * t, LANES), F32),
                           pltpu.VMEM((ATTN_STREAMS, 2 * t, 1), F32)]),
        compiler_params=_params(("parallel", "parallel")),
        name="attn",
    )(qkv, qkv, qkv, tri, bias)


def _group_dot(x, bd):
    g = bd.shape[0]
    parts = [jnp.dot(x[:, i:i + g], bd, preferred_element_type=F32)
             for i in range(0, x.shape[1], g)]
    return jnp.concatenate(parts, axis=1)


def _hgrn_kernel(q_ref, f_ref, i_ref, g_ref, lbl_ref, ng_ref, tri_ref, bd_ref, o_ref,
                 st_scr, wpad, vpad, wsh, vsh, *, tt, c, layer):
    ti = pl.program_id(1)
    w = q_ref.shape[1]
    nch = tt // c

    @pl.when(ti == 0)
    def _():
        st_scr[...] = jnp.zeros_like(st_scr)

    lg = lbl_ref[...]
    e = jnp.exp(lg - jnp.max(lg, axis=0, keepdims=True))
    lb = jnp.sum(e[0:layer + 1], axis=0, keepdims=True) / jnp.sum(e, axis=0, keepdims=True)

    f = lb + (1.0 - lb) * jax.nn.sigmoid(f_ref[...])
    kk = 1.0 - f
    qv = q_ref[...]
    qs = qv * jax.nn.sigmoid(qv)
    v = i_ref[...]
    bd = bd_ref[...]
    cum = _split_dot_left(tri_ref[...], jnp.log(f) * LOG2E, 3)
    wk = jnp.log(kk) * LOG2E - cum

    pos = lax.broadcasted_iota(jnp.int32, (tt, 1), 0) % c
    wpad[0:SUBLANES, :] = jnp.zeros((SUBLANES, w), F32)
    vpad[0:SUBLANES, :] = jnp.zeros((SUBLANES, w), F32)
    wpad[SUBLANES:SUBLANES + tt, :] = wk
    vpad[SUBLANES:SUBLANES + tt, :] = v
    for b in range(SUBLANES):
        wb = wpad[SUBLANES - b:SUBLANES - b + tt, :]
        vb = vpad[SUBLANES - b:SUBLANES - b + tt, :]
        if b:
            wb = jnp.where(pos >= b, wb, -jnp.inf)
        wsh[b] = wb.reshape(nch, c, w)
        vsh[b] = vb.reshape(nch, c, w)

    qs3 = qs.reshape(nch, c, w)
    cum3 = cum.reshape(nch, c, w)
    acc = None
    for a in reversed(range(c // SUBLANES)):
        rows = c - SUBLANES * a
        qa = qs3[:, SUBLANES * a:, :].reshape(nch * rows, w)
        ca = cum3[:, SUBLANES * a:, :].reshape(nch * rows, w)
        acc_a = None
        for b in range(SUBLANES):
            wb = wsh[b, :, 0:rows, :].reshape(nch * rows, w)
            vb = vsh[b, :, 0:rows, :].reshape(nch * rows, w)
            dd = qa * jnp.exp2(ca + wb)
            term = _group_dot(dd.astype(BF16), bd) * vb
            acc_a = term if acc_a is None else acc_a + term
        acc_a = acc_a.reshape(nch, rows, w)
        if acc is not None:
            acc_a = acc_a + jnp.concatenate([jnp.zeros((nch, SUBLANES, w), F32), acc], axis=1)
        acc = acc_a
    acc = acc.reshape(tt, w)

    grp = st_scr.shape[1]
    bdmask = bd[0:grp, 0:grp] != 0
    outs = []
    for ci in range(nch):
        r0 = ci * c
        cum_c = cum[r0:r0 + c]
        last = cum_c[c - 1:c]
        qd = (qs[r0:r0 + c] * jnp.exp2(cum_c)).astype(BF16)
        kd = (kk[r0:r0 + c] * jnp.exp2(last - cum_c)).astype(BF16)
        vc = v[r0:r0 + c].astype(BF16)
        dec = jnp.exp2(last)
        o_parts = []
        for gi in range(w // grp):
            sl = slice(gi * grp, (gi + 1) * grp)
            st = st_scr[gi]
            o_parts.append(lax.dot_general(qd[:, sl], st.astype(BF16), (((1,), (1,)), ((), ())),
                                           preferred_element_type=F32))
            upd = lax.dot_general(vc[:, sl], kd[:, sl], (((0,), (0,)), ((), ())),
                                  preferred_element_type=F32)
            st_scr[gi] = st * dec[:, sl] + jnp.where(bdmask, upd, 0.0)
        outs.append(jnp.concatenate(o_parts, axis=1))
    o = acc + jnp.concatenate(outs, axis=0)

    o2 = o * o
    o2_hi = o2.astype(BF16)
    o2_lo = (o2 - o2_hi.astype(F32)).astype(BF16)
    ms = (_group_dot(o2_hi, bd) + _group_dot(o2_lo, bd)) * (1.0 / HG_DIM)
    gv = g_ref[...]
    o = o * lax.rsqrt(ms + EPS) * ng_ref[...] * (gv * jax.nn.sigmoid(gv))
    o_ref[...] = o.astype(o_ref.dtype)


def _split_dot_left(m, x, passes):
    acc = None
    r = x
    for p in range(passes):
        h = r.astype(BF16)
        term = jnp.dot(m, h, preferred_element_type=F32)
        acc = term if acc is None else acc + term
        if p + 1 < passes:
            r = r - h.astype(F32)
    return acc


def _hgrn(rest, lb_logits, norm_g, bsz, seq, layer, tri, bd):
    n = bsz * seq
    w = HG_HEADS * HG_DIM
    grp = bd.shape[0]
    tt = min(HGRN_TT, seq)
    c = min(HGRN_C, tt)
    nt = seq // tt
    kern = functools.partial(_hgrn_kernel, tt=tt, c=c, layer=layer)
    col = lambda j: pl.BlockSpec((tt, w), lambda b, i, j=j: (b * nt + i, j))
    const = lambda shape: pl.BlockSpec(shape, lambda b, i: (0, 0))
    return pl.pallas_call(
        kern,
        grid=(bsz, nt),
        in_specs=[col(0), col(1), col(2), col(3),
                  const(lb_logits.shape), const((1, w)), const((tt, tt)), const((grp, grp))],
        out_specs=pl.BlockSpec((tt, w), lambda b, i: (b * nt + i, 0)),
        out_shape=jax.ShapeDtypeStruct((n, w), BF16),
        scratch_shapes=([pltpu.VMEM((w // LANES, LANES, LANES), F32)]
                        + [pltpu.VMEM((SUBLANES + tt, w), F32)] * 2
                        + [pltpu.VMEM((SUBLANES, tt // c, c, w), F32)] * 2),
        compiler_params=_params(("parallel", "arbitrary")),
        name="hgrn",
    )(rest, rest, rest, rest, lb_logits, norm_g.reshape(1, w), tri, bd)


PAIR_A = (0, 0, 0, 1, 1, 3)
PAIR_B = (1, 2, 3, 3, 2, 2)
N_PAIRS = len(PAIR_A)
N_CLASSES = N_GROUPS * N_PAIRS
assert sorted(tuple(sorted(p)) for p in zip(PAIR_A, PAIR_B)) == [
    (a, b) for a in range(EXPERTS_PER_GROUP) for b in range(a + 1, EXPERTS_PER_GROUP)]
ROUTE_CLS, ROUTE_WA, ROUTE_WB = 0, 1, 2


def _pair_slots(pidx):
    a = b = jnp.zeros_like(pidx)
    for p in range(N_PAIRS):
        a = jnp.where(pidx == p, float(PAIR_A[p]), a)
        b = jnp.where(pidx == p, float(PAIR_B[p]), b)
    return a, b


def _pair_index(lo, hi):
    pidx = jnp.zeros_like(lo)
    for p in range(N_PAIRS):
        is_p = jnp.logical_and(lo == min(PAIR_A[p], PAIR_B[p]), hi == max(PAIR_A[p], PAIR_B[p]))
        pidx = jnp.where(is_p, float(p), pidx)
    return pidx


def _route(lg):
    lane = lax.broadcasted_iota(jnp.int32, lg.shape, 1)
    neg = jnp.float32(-jnp.inf)
    big = jnp.int32(LANES)
    gmask = jnp.logical_and(lane >= N_EXPERTS, lane < N_EXPERTS + N_GROUPS)
    gl = jnp.where(gmask, lg, neg)
    gmax = jnp.max(gl, axis=1, keepdims=True)
    gidx = jnp.min(jnp.where(gl == gmax, lane, big), axis=1, keepdims=True) - N_EXPERTS
    w_grp = 1.0 / jnp.sum(jnp.where(gmask, jnp.exp(gl - gmax), 0.0), axis=1, keepdims=True)
    in_grp = jnp.logical_and(lane < N_EXPERTS, lane // EXPERTS_PER_GROUP == gidx)
    l1 = jnp.where(in_grp, lg, neg)
    v1 = jnp.max(l1, axis=1, keepdims=True)
    i1 = jnp.min(jnp.where(l1 == v1, lane, big), axis=1, keepdims=True)
    l2 = jnp.where(jnp.logical_and(in_grp, lane != i1), lg, neg)
    v2 = jnp.max(l2, axis=1, keepdims=True)
    i2 = jnp.min(jnp.where(l2 == v2, lane, big), axis=1, keepdims=True)
    e2 = jnp.exp(v2 - v1)
    p1 = 1.0 / (1.0 + e2)
    p2 = e2 * p1
    loc1 = (i1 - gidx * EXPERTS_PER_GROUP).astype(F32)
    loc2 = (i2 - gidx * EXPERTS_PER_GROUP).astype(F32)
    pidx = _pair_index(jnp.minimum(loc1, loc2), jnp.maximum(loc1, loc2))
    cls = gidx.astype(F32) * N_PAIRS + pidx
    a_loc, _ = _pair_slots(pidx)
    first_is_a = loc1 == a_loc
    wa = jnp.where(first_is_a, p1, p2) * w_grp
    wb = jnp.where(first_is_a, p2, p1) * w_grp
    record = jnp.where(lane == ROUTE_CLS, cls,
                       jnp.where(lane == ROUTE_WA, wa, jnp.where(lane == ROUTE_WB, wb, 0.0)))
    return record, cls


def _store_token_tiles(ref, x):
    rows, d = x.shape
    assert d == SUBLANES * LANES
    for s in range(SUBLANES):
        ref[pl.ds(s, rows, stride=SUBLANES), :] = x[:, s * LANES:(s + 1) * LANES]


def _load_token_tiles(ref):
    rows = ref.shape[0] // SUBLANES
    return jnp.concatenate([ref[pl.ds(s, rows, stride=SUBLANES), :] for s in range(SUBLANES)],
                           axis=1)


def _merge_kernel(x_ref, ysb_ref, yhg_ref, gsb_ref, ghg_ref, wbs_ref, wbh_ref, wo_ref,
                  ln_ref, wr_ref, br_ref, x1_ref, t_ref, route_ref, tot_ref,
                  wbs_scr, wbh_scr, wo_scr):
    @pl.when(pl.program_id(0) == 0)
    def _():
        wbs_scr[...] = wbs_ref[...].astype(BF16)
        wbh_scr[...] = wbh_ref[...].astype(BF16)
        wo_scr[...] = wo_ref[...].astype(BF16)
        tot_ref[...] = jnp.zeros_like(tot_ref)

    a = jnp.dot(ysb_ref[...], wbs_scr[...], preferred_element_type=F32)
    b = jnp.dot(yhg_ref[...], wbh_scr[...], preferred_element_type=F32)
    merged = jax.nn.sigmoid(gsb_ref[...]) * a + jax.nn.sigmoid(ghg_ref[...]) * b
    x1 = x_ref[...] + jnp.dot(merged.astype(BF16), wo_scr[...], preferred_element_type=F32)
    x1_ref[...] = x1
    var = jnp.mean(x1 * x1, axis=-1, keepdims=True)
    t = x1 * lax.rsqrt(var + EPS) * ln_ref[...]
    _store_token_tiles(t_ref, t)
    t_hi = t.astype(BF16)
    t_lo = (t - t_hi.astype(F32)).astype(BF16)
    p_hi = jnp.dot(t_hi, wr_ref[...], preferred_element_type=F32)
    p_lo = jnp.dot(t_lo, wr_ref[...], preferred_element_type=F32)
    lg = (p_hi[:, :LANES] + p_hi[:, LANES:]) + (p_lo[:, :LANES] + p_lo[:, LANES:]) + br_ref[...]
    route, cls = _route(lg)
    route_ref[...] = route
    lane = lax.broadcasted_iota(jnp.int32, route.shape, 1)
    sel = jnp.where(lane.astype(F32) == cls, 1.0, 0.0)
    tot_ref[...] += jnp.broadcast_to(jnp.sum(sel, axis=0, keepdims=True), tot_ref.shape)


def _merge(x2, y_sb, y_hg, rest, wbs, wbh, wo, ln_g, wr, br):
    n, d = x2.shape
    tm = min(MERGE_TM, n)
    w_sb, w_hg = y_sb.shape[1], y_hg.shape[1]
    gate_blk = (rest.shape[1] - 2 * d) // d
    row = lambda wdt, j=0: pl.BlockSpec((tm, wdt), lambda i, j=j: (i, j))
    const = lambda shape: pl.BlockSpec(shape, lambda i: (0, 0))
    return pl.pallas_call(
        _merge_kernel,
        grid=(n // tm,),
        in_specs=[row(d), row(w_sb), row(w_hg), row(d, gate_blk), row(d, gate_blk + 1),
                  const(wbs.shape), const(wbh.shape), const(wo.shape), const((1, d)),
                  const(wr.shape), const((1, LANES))],
        out_specs=[row(d), pl.BlockSpec((tm * SUBLANES, LANES), lambda i: (i, 0)), row(LANES),
                   const((SUBLANES, LANES))],
        out_shape=[jax.ShapeDtypeStruct((n, d), F32),
                   jax.ShapeDtypeStruct((n * SUBLANES, LANES), F32),
                   jax.ShapeDtypeStruct((n, LANES), F32),
                   jax.ShapeDtypeStruct((SUBLANES, LANES), F32)],
        scratch_shapes=[pltpu.VMEM(w.shape, BF16) for w in (wbs, wbh, wo)],
        compiler_params=_params(("arbitrary",)),
        name="merge",
    )(x2, y_sb, y_hg, rest, rest, wbs, wbh, wo, ln_g.reshape(1, d), wr, br)


TOKEN_ROWS_OUT = 2 * SUBLANES


def _plan_kernel(route_ref, tot_ref, ltri_ref, utri_ref, pos_ref, te_ref, meta_ref,
                 run_scr, *, tile_rows):
    i = pl.program_id(0)
    route = route_ref[...]
    tm = route.shape[0]
    lane = lax.broadcasted_iota(jnp.int32, (tm, LANES), 1)
    is_cls = lane.astype(F32) == route[:, ROUTE_CLS:ROUTE_CLS + 1]
    sel = jnp.where(is_cls, 1.0, 0.0)

    tot = tot_ref[0:1, :]
    top = tot + (tile_rows - 1.0)
    tiles = jnp.floor(top * (1.0 / tile_rows))
    tiles = jnp.where(tiles * tile_rows > top, tiles - 1.0, tiles)
    tiles = jnp.where((tiles + 1.0) * tile_rows <= top, tiles + 1.0, tiles)
    first_tile = _split_dot(jnp.broadcast_to(tiles, (SUBLANES, LANES)), utri_ref[...], 3)[0:1]

    @pl.when(i == 0)
    def _():
        run_scr[...] = jnp.zeros_like(run_scr)
        lane1 = lax.broadcasted_iota(jnp.int32, (1, LANES), 1)
        is_c = lane1 < N_CLASSES
        end_tile = first_tile + tiles
        n_valid = jnp.sum(jnp.where(is_c, tiles, 0.0), axis=1, keepdims=True)
        rows = te_ref.shape[0]
        j = lax.broadcasted_iota(jnp.int32, (rows, LANES), 0).astype(F32)
        j = jnp.minimum(j, n_valid - 1.0)
        lane2 = lax.broadcasted_iota(jnp.int32, (rows, LANES), 1)
        done = jnp.where(jnp.logical_and(lane2 < N_CLASSES, end_tile <= j), 1.0, 0.0)
        tc = jnp.sum(done, axis=1, keepdims=True)
        tg = jnp.floor((tc + 0.5) * (1.0 / N_PAIRS))
        ta, tb = _pair_slots(tc - tg * N_PAIRS)
        te = jnp.where(lane2 == 0, tg * EXPERTS_PER_GROUP + ta, tg * EXPERTS_PER_GROUP + tb)
        te_ref[...] = te.astype(jnp.int32)
        last_tile = jnp.where(jnp.logical_and(is_c, tiles > 0), end_tile - 1.0, -1.0)
        meta = jnp.where(lane1 == N_CLASSES, n_valid, last_tile)
        meta_ref[...] = jnp.broadcast_to(meta, meta_ref.shape).astype(jnp.int32)

    rank = jnp.dot(ltri_ref[...], sel.astype(BF16), preferred_element_type=F32) + run_scr[...]
    dest = jnp.sum(jnp.where(is_cls, first_tile * tile_rows + rank, 0.0), axis=1, keepdims=True)
    row = lax.broadcasted_iota(jnp.int32, (tm, LANES), 0)
    spread = jnp.where(lane == row % LANES, dest, 0.0)
    pos_ref[...] = jnp.sum(spread.reshape(tm // LANES, LANES, LANES), axis=1).astype(jnp.int32)
    run_scr[...] += jnp.sum(sel, axis=0, keepdims=True)


def _plan(route, tot, n_tiles_max):
    n = route.shape[0]
    tm = min(PLAN_TM, n)
    r = np.arange(tm)
    ltri = jnp.asarray((r[None, :] < r[:, None]).astype(np.float32), dtype=BF16)
    e = np.arange(LANES)
    utri = jnp.asarray((e[:, None] < e[None, :]).astype(np.float32), dtype=BF16)
    te_rows = -(-n_tiles_max // SUBLANES) * SUBLANES
    kern = functools.partial(_plan_kernel, tile_rows=MOE_TM)
    return pl.pallas_call(
        kern,
        grid=(n // tm,),
        in_specs=[pl.BlockSpec((tm, LANES), lambda i: (i, 0)),
                  pl.BlockSpec((SUBLANES, LANES), lambda i: (0, 0)),
                  pl.BlockSpec((tm, tm), lambda i: (0, 0)),
                  pl.BlockSpec((LANES, LANES), lambda i: (0, 0))],
        out_specs=[pl.BlockSpec((tm // LANES, LANES), lambda i: (i, 0)),
                   pl.BlockSpec((te_rows, LANES), lambda i: (0, 0)),
                   pl.BlockSpec((SUBLANES, LANES), lambda i: (0, 0))],
        out_shape=[jax.ShapeDtypeStruct((n // LANES, LANES), jnp.int32),
                   jax.ShapeDtypeStruct((te_rows, LANES), jnp.int32),
                   jax.ShapeDtypeStruct((SUBLANES, LANES), jnp.int32)],
        scratch_shapes=[pltpu.VMEM((1, LANES), F32)],
        compiler_params=_params(("arbitrary",)),
        name="plan",
    )(route, tot, ltri, utri)


DMA_UNROLL = 8


def _token_copy(src_ref, r, dst_ref, p, sem, rows):
    return pltpu.make_async_copy(src_ref.at[pl.ds(pl.multiple_of(r * rows, rows), rows)],
                                 dst_ref.at[pl.ds(pl.multiple_of(p * rows, rows), rows)], sem)


def _dispatch_kernel(pos_ref, meta_ref, t_ref, xs_ref, zero_scr, sem, tail_sem, *, tm, tile_rows):
    i = pl.program_id(0)
    tile_rows = tile_rows * SUBLANES
    n_tiles = xs_ref.shape[0] // tile_rows
    n_valid = meta_ref[N_CLASSES]

    def clear(tile, s):
        return pltpu.make_async_copy(zero_scr, xs_ref.at[pl.ds(tile * tile_rows, tile_rows)], s)

    def for_each(clears, fn):
        for tile, cond in clears:
            @pl.when(cond)
            def _():
                fn(tile)

    last_tiles = [(meta_ref[c], meta_ref[c] >= 0) for c in range(N_CLASSES)]
    past_tiles = [(n_tiles - 1 - k, n_tiles - 1 - k >= n_valid) for k in range(N_CLASSES)]

    @pl.when(i == 0)
    def _():
        zero_scr[...] = jnp.zeros_like(zero_scr)
        for_each(last_tiles, lambda tile: clear(tile, sem).start())
        for_each(past_tiles, lambda tile: clear(tile, tail_sem).start())
        for_each(last_tiles, lambda tile: clear(tile, sem).wait())

    base = i * tm

    def start(r2, carry):
        for k in range(2):
            r = 2 * r2 + k
            _token_copy(t_ref, r, xs_ref, pos_ref[base + r], sem, SUBLANES).start(priority=k)
        return carry

    lax.fori_loop(0, tm // 2, start, 0, unroll=DMA_UNROLL)
    pltpu.make_async_copy(t_ref, xs_ref.at[pl.ds(0, tm * SUBLANES)], sem).wait()

    @pl.when(i == pl.num_programs(0) - 1)
    def _():
        for_each(past_tiles, lambda tile: clear(tile, tail_sem).wait())


def _dispatch(pos, meta, t, n_rows):
    n = t.shape[0] // SUBLANES
    tm = min(DISPATCH_TM, n)
    kern = functools.partial(_dispatch_kernel, tm=tm, tile_rows=MOE_TM)
    grid_spec = pltpu.PrefetchScalarGridSpec(
        num_scalar_prefetch=2,
        grid=(n // tm,),
        in_specs=[pl.BlockSpec((tm * SUBLANES, LANES), lambda i, *_: (i, 0))],
        out_specs=pl.BlockSpec(memory_space=pl.ANY),
        scratch_shapes=[pltpu.VMEM((MOE_TM * SUBLANES, LANES), F32),
                        pltpu.SemaphoreType.DMA(()), pltpu.SemaphoreType.DMA(())],
    )
    return pl.pallas_call(
        kern,
        grid_spec=grid_spec,
        out_shape=jax.ShapeDtypeStruct((n_rows * SUBLANES, LANES), F32),
        compiler_params=_params(("arbitrary",)),
        name="dispatch",
    )(pos, meta, t)


def _expert_kernel(nv_ref, rid_a, rid_b, exp_a, exp_b, nr_ref, xs_ref, wg_hbm, wu_hbm, wd_hbm,
                   ys_ref, wg_s, wu_s, wd_s, sem):
    j = pl.program_id(0)
    in_use = j < nv_ref[0]
    rows = xs_ref.shape[0] // SUBLANES
    roles = ((rid_a, exp_a), (rid_b, exp_b))

    def copies(role, run):
        e = roles[role][1][run]
        ring = run % 2
        return [pltpu.make_async_copy(src.at[e], dst.at[role, ring], sem.at[role, ring])
                for src, dst in ((wg_hbm, wg_s), (wu_hbm, wu_s), (wd_hbm, wd_s))]

    for role in range(2):
        rid_ref = roles[role][0]
        run = rid_ref[j]
        first = jnp.logical_or(j == 0, run != rid_ref[jnp.maximum(j - 1, 0)])

        @pl.when(j == 0)
        def _():
            for c in copies(role, 0):
                c.start()

        @pl.when(first)
        def _():
            for c in copies(role, run):
                c.wait()

            @pl.when(run + 1 < nr_ref[role])
            def _():
                for c in copies(role, run + 1):
                    c.start()

    @pl.when(in_use)
    def _():
        x = _load_token_tiles(xs_ref).astype(BF16)
        for role in range(2):
            ring = roles[role][0][j] % 2
            hg = jnp.dot(x, wg_s[role, ring].astype(BF16), preferred_element_type=F32)
            hu = jnp.dot(x, wu_s[role, ring].astype(BF16), preferred_element_type=F32)
            a = (hg * jax.nn.sigmoid(hg) * hu).astype(BF16)
            y = jnp.dot(a, wd_s[role, ring].astype(BF16), preferred_element_type=F32)
            for s in range(SUBLANES):
                ys_ref[pl.ds(role * SUBLANES + s, rows, stride=TOKEN_ROWS_OUT), :] = (
                    y[:, s * LANES:(s + 1) * LANES])

    @pl.when(jnp.logical_not(in_use))
    def _():
        ys_ref[...] = jnp.zeros_like(ys_ref)


def _expert_runs(te):
    first = jnp.concatenate([jnp.ones((1,), jnp.int32),
                             (te[1:] != te[:-1]).astype(jnp.int32)])
    rid = jnp.cumsum(first) - 1
    return rid, jnp.zeros_like(te).at[rid].set(te), rid[-1:] + 1


def _experts(tea, teb, nv, xs, wg, wu, wd):
    _, d, de = wg.shape
    assert d == SUBLANES * LANES
    n_tiles = xs.shape[0] // (MOE_TM * SUBLANES)
    rid_a, exp_a, nr_a = _expert_runs(tea)
    rid_b, exp_b, nr_b = _expert_runs(teb)
    grid_spec = pltpu.PrefetchScalarGridSpec(
        num_scalar_prefetch=6,
        grid=(n_tiles,),
        in_specs=[pl.BlockSpec((MOE_TM * SUBLANES, LANES),
                               lambda j, nv, *_: (jnp.minimum(j, nv[0] - 1), 0)),
                  pl.BlockSpec(memory_space=pl.ANY), pl.BlockSpec(memory_space=pl.ANY),
                  pl.BlockSpec(memory_space=pl.ANY)],
        out_specs=pl.BlockSpec((MOE_TM * TOKEN_ROWS_OUT, LANES), lambda j, *_: (j, 0)),
        scratch_shapes=[pltpu.VMEM((2, 2, d, de), F32), pltpu.VMEM((2, 2, d, de), F32),
                        pltpu.VMEM((2, 2, de, d), F32), pltpu.SemaphoreType.DMA((2, 2))],
    )
    return pl.pallas_call(
        _expert_kernel,
        grid_spec=grid_spec,
        out_shape=jax.ShapeDtypeStruct((n_tiles * MOE_TM * TOKEN_ROWS_OUT, LANES), F32),
        compiler_params=_params(("arbitrary",)),
        name="experts",
    )(nv, rid_a, rid_b, exp_a, exp_b, jnp.concatenate([nr_a, nr_b]), xs, wg, wu, wd)


def _combine_kernel(pos_ref, ys_ref, x1_ref, route_ref, fg_ref, o_ref, g, sem, *, tm):
    i = pl.program_id(0)
    slot = i % 2

    def gather(step, buf):
        base = step * tm

        def body(r2, carry):
            for k in range(2):
                r = 2 * r2 + k
                _token_copy(ys_ref, pos_ref[base + r], g.at[buf], r, sem.at[buf],
                            TOKEN_ROWS_OUT).start(priority=k)
            return carry

        lax.fori_loop(0, tm // 2, body, 0, unroll=DMA_UNROLL)

    @pl.when(i == 0)
    def _():
        gather(0, 0)

    @pl.when(i + 1 < pl.num_programs(0))
    def _():
        gather(i + 1, 1 - slot)

    pltpu.make_async_copy(ys_ref.at[pl.ds(0, tm * TOKEN_ROWS_OUT)], g.at[slot],
                          sem.at[slot]).wait()
    gs = g.at[slot]
    ya, yb = (jnp.concatenate([gs[pl.ds(off + s, tm, stride=TOKEN_ROWS_OUT), :]
                               for s in range(SUBLANES)], axis=1) for off in (0, SUBLANES))
    route = route_ref[...]
    x2 = x1_ref[...] + (route[:, ROUTE_WA:ROUTE_WA + 1] * ya + route[:, ROUTE_WB:ROUTE_WB + 1] * yb)
    var = jnp.mean(x2 * x2, axis=-1, keepdims=True)
    o_ref[...] = x2 * lax.rsqrt(var + EPS) * fg_ref[...]


def _combine(pos, ys, x1, route, final_g):
    n, d = x1.shape
    tm = min(COMBINE_TM, n)
    kern = functools.partial(_combine_kernel, tm=tm)
    grid_spec = pltpu.PrefetchScalarGridSpec(
        num_scalar_prefetch=1,
        grid=(n // tm,),
        in_specs=[pl.BlockSpec(memory_space=pl.ANY),
                  pl.BlockSpec((tm, d), lambda i, *_: (i, 0)),
                  pl.BlockSpec((tm, LANES), lambda i, *_: (i, 0)),
                  pl.BlockSpec((1, d), lambda i, *_: (0, 0))],
        out_specs=pl.BlockSpec((tm, d), lambda i, *_: (i, 0)),
        scratch_shapes=[pltpu.VMEM((2, tm * TOKEN_ROWS_OUT, LANES), F32),
                        pltpu.SemaphoreType.DMA((2,))],
    )
    return pl.pallas_call(
        kern,
        grid_spec=grid_spec,
        out_shape=jax.ShapeDtypeStruct((n, d), F32),
        compiler_params=_params(("arbitrary",)),
        name="combine",
    )(pos, ys, x1, route, final_g.reshape(1, d))


def _moe_sparse(t, route, tot, wg, wu, wd, x1, final_g):
    n = x1.shape[0]
    n_tiles_max = (n + N_CLASSES * (MOE_TM - 1)) // MOE_TM
    pos, te, meta = _plan(route, tot, n_tiles_max)
    pos = pos.reshape(n)
    xs = _dispatch(pos, meta[0, :N_CLASSES + 1], t, n_tiles_max * MOE_TM)
    ys = _experts(te[:n_tiles_max, 0], te[:n_tiles_max, 1], meta[0, N_CLASSES:N_CLASSES + 1],
                  xs, wg, wu, wd)
    return _combine(pos, ys, x1, route, final_g)


def _suffix_ones(t):
    j = np.arange(t)[:, None]
    s = np.arange(t)[None, :]
    return jnp.asarray((j > s).astype(np.float32), dtype=BF16)


def _causal_bias(t):
    row = np.arange(t)[:, None]
    col = np.arange(t)[None, :]
    diag = np.where(col < row, 0.0, MASK_BIAS).astype(np.float32)
    return jnp.asarray(np.stack([np.zeros_like(diag), diag]))


def _chunk_prefix_ones(tt, c):
    t = np.arange(tt)[:, None]
    j = np.arange(tt)[None, :]
    return jnp.asarray(((j <= t) & (t // c == j // c)).astype(np.float32), dtype=BF16)


def _block_diag_ones(w, blk):
    a = np.arange(w)
    return jnp.asarray((a[:, None] // blk == a[None, :] // blk).astype(np.float32), dtype=BF16)


def kernel(x, ln1_g, w_in, w_branch_sb, w_branch_hg, hg_norm_g, hg_lb_logits, w_out, ln2_g,
           w_router_group, b_router_group, w_router_expert, b_router_expert,
           w_exp_gate, w_exp_up, w_exp_down, final_g):
    bsz, seq, d = x.shape
    depth = w_in.shape[0]
    n = bsz * seq
    sb_width = SB_HEADS * SB_HEAD_DIM
    hg_width = HG_HEADS * HG_DIM

    tri_attn = _suffix_ones(min(ATTN_T, seq))
    bias_attn = _causal_bias(min(ATTN_T, seq))
    tt = min(HGRN_TT, seq)
    tri_hg = _chunk_prefix_ones(tt, min(HGRN_C, tt))
    bd = _block_diag_ones(min(MXU_DIM, hg_width), HG_DIM)

    x2 = x.reshape(n, d)
    for l in range(depth):
        qkv, rest = _inproj(x2, ln1_g[l], w_in[l], sb_width)
        y_sb = _attn(qkv, bsz, seq, tri_attn, bias_attn)
        y_hg = _hgrn(rest, hg_lb_logits, hg_norm_g[l], bsz, seq, l, tri_hg, bd)

        pad = LANES - N_EXPERTS - N_GROUPS
        wr = jnp.concatenate([w_router_expert[l], w_router_group[l],
                              jnp.zeros((d, pad), F32)], axis=1)
        wr_hi = wr.astype(BF16)
        wr_lo = (wr - wr_hi.astype(F32)).astype(BF16)
        wr_split = jnp.concatenate([wr_hi, wr_lo], axis=1)
        br = jnp.concatenate([b_router_expert[l], b_router_group[l],
                              jnp.zeros((pad,), F32)]).reshape(1, LANES)

        last = l == depth - 1
        x1, t, route, tot = _merge(x2, y_sb, y_hg, rest, w_branch_sb[l], w_branch_hg[l],
                                   w_out[l], ln2_g[l], wr_split, br)
        assert last, "final rmsnorm is fused into the last layer's combine kernel"
        x2 = _moe_sparse(t, route, tot, w_exp_gate[l], w_exp_up[l], w_exp_down[l], x1, final_g)
    return x2.reshape(bsz, seq, d)
```
